```python
import math, functools
import jax, jax.numpy as jnp
from jax import lax
import numpy as np

D_MODEL = 1024
BATCH = 8
SEQ = 2048
DEPTH = 1
DEC_BATCH = 128
DEC_SEQ = 1
PAST_LEN = 16384
PAGE_SIZE = 128

RET_HEADS = 4
RET_DK = 128
RET_DV = 256
RET_QK_W = RET_HEADS * RET_DK
RET_V_W = RET_HEADS * RET_DV
RET_CHUNK = 128
ROPE_BASE = 10000.0
GDN_HEADS = 8
GDN_DK = 128
GDN_DV = 128
GDN_QK_W = GDN_HEADS * GDN_DK
GDN_V_W = GDN_HEADS * GDN_DV
GDN_CHUNK = 64
CONV_W = 4
CONV_CH = 2 * GDN_QK_W + GDN_V_W
IN_SPLITS = (RET_QK_W, RET_QK_W, RET_V_W, RET_V_W, CONV_CH, GDN_V_W, GDN_HEADS, GDN_HEADS, D_MODEL, D_MODEL)
IN_WIDTH = sum(IN_SPLITS)
N_EXPERTS = 32
TOP_K = 4
D_FF = D_MODEL
SWIGLU_LIMIT = 7.0
SWIGLU_ALPHA = 1.702
MOE_BLOCK = 128
NORM_EPS = 1e-6

kernel_name = "hybrid_retention_gdn_moe_step"


def rms_normalize(x):
    xf = x.astype(jnp.float32)
    return xf * lax.rsqrt(jnp.mean(xf * xf, axis=-1, keepdims=True) + NORM_EPS)


def rmsnorm(x, w):
    return (rms_normalize(x) * w.astype(jnp.float32)).astype(x.dtype)


def l2norm(x):
    return x * lax.rsqrt(jnp.sum(x * x, axis=-1, keepdims=True) + NORM_EPS)


def rotary(x, pos):
    half = x.shape[-1] // 2
    inv = 1.0 / (ROPE_BASE ** (jnp.arange(half, dtype=jnp.float32) / half))
    ang = pos.astype(jnp.float32)[:, None] * inv[None, :]
    cos = jnp.cos(ang)[None, :, None, :]
    sin = jnp.sin(ang)[None, :, None, :]
    x1, x2 = x[..., :half], x[..., half:]
    return jnp.concatenate([x1 * cos - x2 * sin, x1 * sin + x2 * cos], axis=-1)


def chunk_scan(step, s0, seqs, chunk):
    B, H, L = seqs[0].shape[:3]
    c = chunk if L % chunk == 0 else L
    n = L // c
    xs = tuple(jnp.moveaxis(a.reshape((B, H, n, c) + a.shape[3:]), 2, 0) for a in seqs)
    s, ys = lax.scan(step, s0, xs)
    ys = jnp.moveaxis(ys, 0, 2)
    return s, ys.reshape((B, H, L) + ys.shape[4:])


def retention_step(lg, S, xs):
    q, k, v = xs
    c = q.shape[2]
    idx = jnp.arange(c, dtype=jnp.float32)
    diff = idx[:, None] - idx[None, :]
    causal = diff >= 0
    dmask = jnp.where(causal[None], jnp.exp(jnp.where(causal, diff, 0.0)[None] * lg[:, None, None]), 0.0)
    inner = jnp.einsum('bhid,bhjd->bhij', q, k) * dmask[None]
    q_dec = q * jnp.exp((idx + 1.0)[None, :] * lg[:, None])[None, :, :, None]
    o = jnp.einsum('bhij,bhje->bhie', inner, v) + jnp.einsum('bhid,bhde->bhie', q_dec, S)
    k_dec = k * jnp.exp((c - 1.0 - idx)[None, :] * lg[:, None])[None, :, :, None]
    S_new = S * jnp.exp(c * lg)[None, :, None, None] + jnp.einsum('bhjd,bhje->bhde', k_dec, v)
    return S_new, o


def gdn_step(S, xs):
    q, k, v, g, beta = xs
    c = q.shape[2]
    G = jnp.cumsum(g, axis=-1)
    diff = G[..., :, None] - G[..., None, :]
    lower = jnp.tril(jnp.ones((c, c), dtype=bool))
    strict = jnp.tril(jnp.ones((c, c), dtype=bool), -1)
    decay = jnp.exp(jnp.where(lower, diff, -jnp.inf))
    kb = k * beta[..., None]
    A = jnp.einsum('bhid,bhjd->bhij', kb, k) * jnp.where(strict, decay, 0.0)
    eye = jnp.eye(c, dtype=jnp.float32)
    rhs = jnp.concatenate([v * beta[..., None], kb * jnp.exp(G)[..., None]], axis=-1)
    sol = lax.linalg.triangular_solve(A + eye, rhs, left_side=True, lower=True, unit_diagonal=True)
    u, w = sol[..., :GDN_DV], sol[..., GDN_DV:]
    v_new = u - jnp.einsum('bhik,bhkv->bhiv', w, S)
    attn = jnp.einsum('bhik,bhjk->bhij', q, k) * decay
    o = jnp.einsum('bhik,bhkv->bhiv', q * jnp.exp(G)[..., None], S) + jnp.einsum('bhij,bhjv->bhiv', attn, v_new)
    G_last = G[..., -1:]
    S_new = S * jnp.exp(G_last)[..., None] + jnp.einsum('bhik,bhiv->bhkv', k * jnp.exp(G_last - G)[..., None], v_new)
    return S_new, o


def token_mixer(h, pos, s_ret, s_gdn, s_conv, w_in, conv_w, a_log, dt_bias, gdn_norm_w,
                w_branch_a, w_branch_b, w_out):
    B, L, _ = h.shape
    dt = h.dtype
    f32 = jnp.float32
    p = h @ w_in
    split_idx = np.cumsum(IN_SPLITS)[:-1].tolist()
    q_r, k_r, v_r, g_r, qkv_g, z_g, a_g, b_g, gate_a, gate_b = jnp.split(p, split_idx, axis=-1)
    tr = lambda a: jnp.swapaxes(a, 1, 2)

    lg = jnp.log1p(-jnp.exp2(-5.0 - jnp.arange(RET_HEADS, dtype=f32)))
    q = rotary(q_r.reshape(B, L, RET_HEADS, RET_DK).astype(f32), pos)
    k = rotary(k_r.reshape(B, L, RET_HEADS, RET_DK).astype(f32), pos) * (RET_DK ** -0.5)
    v = v_r.reshape(B, L, RET_HEADS, RET_DV).astype(f32)
    s_ret_new, o = chunk_scan(functools.partial(retention_step, lg), s_ret.astype(f32),
                              (tr(q), tr(k), tr(v)), RET_CHUNK)
    o = rms_normalize(tr(o)).reshape(B, L, RET_V_W)
    o_ret = (o * jax.nn.silu(g_r.astype(f32))).astype(dt)

    xc = jnp.concatenate([s_conv.astype(dt), qkv_g], axis=1)
    acc = xc[:, 0:L].astype(f32) * conv_w[0].astype(f32)
    for i in range(1, CONV_W):
        acc = acc + xc[:, i:i + L].astype(f32) * conv_w[i].astype(f32)
    s_conv_new = xc[:, L:]
    u = jax.nn.silu(acc)
    qg, kg, vg = jnp.split(u, [GDN_QK_W, 2 * GDN_QK_W], axis=-1)
    qg = l2norm(qg.reshape(B, L, GDN_HEADS, GDN_DK)) * (GDN_DK ** -0.5)
    kg = l2norm(kg.reshape(B, L, GDN_HEADS, GDN_DK))
    vg = vg.reshape(B, L, GDN_HEADS, GDN_DV)
    g = -jnp.exp(a_log.astype(f32)) * jax.nn.softplus(a_g.astype(f32) + dt_bias.astype(f32))
    beta = jax.nn.sigmoid(b_g.astype(f32))
    s_gdn_new, og = chunk_scan(gdn_step, s_gdn.astype(f32),
                               (tr(qg), tr(kg), tr(vg), tr(g), tr(beta)), GDN_CHUNK)
    og = rms_normalize(tr(og)) * gdn_norm_w.astype(f32)
    og = og * jax.nn.silu(z_g.astype(f32).reshape(B, L, GDN_HEADS, GDN_DV))
    o_gdn = og.reshape(B, L, GDN_V_W).astype(dt)

    ya = o_ret @ w_branch_a
    yb = o_gdn @ w_branch_b
    m = jax.nn.sigmoid(gate_a) * ya + jax.nn.sigmoid(gate_b) * yb
    return m @ w_out, s_ret_new, s_gdn_new, s_conv_new


def moe_ffn(h, w_router, b_router, w_gate_up, b_gate_up, w_down, b_down):
    B, L, D = h.shape
    T = B * L
    f32 = jnp.float32
    x2 = h.reshape(T, D)
    logits = x2.astype(f32) @ w_router.astype(f32) + b_router.astype(f32)
    top_v, top_i = lax.top_k(logits, TOP_K)
    gate = jax.nn.softmax(top_v, axis=-1)
    n = T * TOP_K
    flat_e = top_i.reshape(n).astype(jnp.int32)
    flat_w = gate.reshape(n)
    flat_t = jnp.arange(n, dtype=jnp.int32) // TOP_K
    order = jnp.argsort(flat_e)
    se = flat_e[order]
    counts = jnp.bincount(flat_e, length=N_EXPERTS)
    start = jnp.cumsum(counts) - counts
    pcounts = (counts + MOE_BLOCK - 1) // MOE_BLOCK * MOE_BLOCK
    pend = jnp.cumsum(pcounts)
    pstart = pend - pcounts
    dest = pstart[se] + (jnp.arange(n, dtype=jnp.int32) - start[se])
    R = (-(-n // MOE_BLOCK) + N_EXPERTS) * MOE_BLOCK
    nb = R // MOE_BLOCK
    rows_t = jnp.full((R,), T, dtype=jnp.int32).at[dest].set(flat_t[order])
    rows_w = jnp.zeros((R,), f32).at[dest].set(flat_w[order])
    block_e = jnp.minimum(jnp.searchsorted(pend, jnp.arange(nb, dtype=jnp.int32) * MOE_BLOCK, side='right'),
                          N_EXPERTS - 1)
    xs = jnp.concatenate([x2, jnp.zeros((1, D), x2.dtype)], axis=0)[rows_t].reshape(nb, MOE_BLOCK, D)

    def expert_block(args):
        xb, e = args
        hb = (xb @ w_gate_up[e] + b_gate_up[e]).astype(f32)
        glu = jnp.minimum(hb[:, :D_FF], SWIGLU_LIMIT)
        lin = jnp.clip(hb[:, D_FF:], -SWIGLU_LIMIT, SWIGLU_LIMIT)
        act = (glu * jax.nn.sigmoid(SWIGLU_ALPHA * glu) * (lin + 1.0)).astype(xb.dtype)
        return act @ w_down[e] + b_down[e]

    ys = lax.map(expert_block, (xs, block_e))
    out = jnp.zeros((T + 1, D), f32).at[rows_t].add(ys.reshape(R, D).astype(f32) * rows_w[:, None])
    return out[:T].reshape(B, L, D).astype(h.dtype)


def decoder_layer(x, pos, s_ret, s_gdn, s_conv, attn_norm_w, w_in, conv_w, a_log, dt_bias, gdn_norm_w,
                  w_branch_a, w_branch_b, w_out, ffn_norm_w, w_router, b_router, w_gate_up, b_gate_up,
                  w_down, b_down):
    mix, sr, sg, sc = token_mixer(rmsnorm(x, attn_norm_w), pos, s_ret, s_gdn, s_conv, w_in, conv_w, a_log,
                                  dt_bias, gdn_norm_w, w_branch_a, w_branch_b, w_out)
    x = x + mix
    x = x + moe_ffn(rmsnorm(x, ffn_norm_w), w_router, b_router, w_gate_up, b_gate_up, w_down, b_down)
    return x, sr, sg, sc


def setup_inputs(seed: int = 0) -> dict:
    key = jax.random.key(seed)
    ks = jax.random.split(key, 24)
    f32 = jnp.float32

    def nrm(k, shape, scale):
        return jax.random.normal(k, shape, f32) * scale

    x_prompt = nrm(ks[0], (BATCH, SEQ, D_MODEL), 1.0)
    x_sample = nrm(ks[1], (DEC_BATCH, DEC_SEQ, D_MODEL), 1.0)
    state_ret = nrm(ks[2], (DEPTH, DEC_BATCH, RET_HEADS, RET_DK, RET_DV), RET_DK ** -0.5)
    state_gdn = nrm(ks[3], (DEPTH, DEC_BATCH, GDN_HEADS, GDN_DK, GDN_DV), GDN_DK ** -0.5)
    state_conv = nrm(ks[4], (DEPTH, DEC_BATCH, CONV_W - 1, CONV_CH), 1.0)
    attn_norm_w = 1.0 + nrm(ks[5], (DEPTH, D_MODEL), 0.02)
    w_in = nrm(ks[6], (DEPTH, D_MODEL, IN_WIDTH), D_MODEL ** -0.5)
    conv_w = nrm(ks[7], (DEPTH, CONV_W, CONV_CH), CONV_W ** -0.5)
    a_log = jnp.log(jax.random.uniform(ks[8], (DEPTH, GDN_HEADS), f32, 1.0, 16.0))
    dtv = jnp.exp(jax.random.uniform(ks[9], (DEPTH, GDN_HEADS), f32, math.log(1e-3), math.log(1e-1)))
    dt_bias = dtv + jnp.log(-jnp.expm1(-dtv))
    gdn_norm_w = 1.0 + nrm(ks[10], (DEPTH, GDN_DV), 0.02)
    w_branch_a = nrm(ks[11], (DEPTH, RET_V_W, D_MODEL), RET_V_W ** -0.5)
    w_branch_b = nrm(ks[12], (DEPTH, GDN_V_W, D_MODEL), GDN_V_W ** -0.5)
    w_out = nrm(ks[13], (DEPTH, D_MODEL, D_MODEL), D_MODEL ** -0.5)
    ffn_norm_w = 1.0 + nrm(ks[14], (DEPTH, D_MODEL), 0.02)
    w_router = nrm(ks[15], (DEPTH, D_MODEL, N_EXPERTS), D_MODEL ** -0.5)
    b_router = nrm(ks[16], (DEPTH, N_EXPERTS), 0.01)
    w_gate_up = nrm(ks[17], (DEPTH, N_EXPERTS, D_MODEL, 2 * D_FF), D_MODEL ** -0.5)
    b_gate_up = nrm(ks[18], (DEPTH, N_EXPERTS, 2 * D_FF), 0.01)
    w_down = nrm(ks[19], (DEPTH, N_EXPERTS, D_FF, D_MODEL), D_FF ** -0.5)
    b_down = nrm(ks[20], (DEPTH, N_EXPERTS, D_MODEL), 0.01)
    final_norm_w = 1.0 + nrm(ks[21], (D_MODEL,), 0.02)
    return {"x_prompt": x_prompt, "x_sample": x_sample, "state_ret": state_ret, "state_gdn": state_gdn,
            "state_conv": state_conv, "attn_norm_w": attn_norm_w, "w_in": w_in, "conv_w": conv_w,
            "a_log": a_log, "dt_bias": dt_bias, "gdn_norm_w": gdn_norm_w, "w_branch_a": w_branch_a,
            "w_branch_b": w_branch_b, "w_out": w_out, "ffn_norm_w": ffn_norm_w, "w_router": w_router,
            "b_router": b_router, "w_gate_up": w_gate_up, "b_gate_up": b_gate_up, "w_down": w_down,
            "b_down": b_down, "final_norm_w": final_norm_w}


def reference(x_prompt, x_sample, state_ret, state_gdn, state_conv, attn_norm_w, w_in, conv_w, a_log, dt_bias,
              gdn_norm_w, w_branch_a, w_branch_b, w_out, ffn_norm_w, w_router, b_router, w_gate_up, b_gate_up,
              w_down, b_down, final_norm_w):
    f32 = jnp.float32
    Bp, Lp = x_prompt.shape[0], x_prompt.shape[1]
    pos_p = jnp.arange(Lp, dtype=jnp.int32)
    pos_s = PAST_LEN + jnp.arange(x_sample.shape[1], dtype=jnp.int32)
    z_ret = jnp.zeros((Bp, RET_HEADS, RET_DK, RET_DV), f32)
    z_gdn = jnp.zeros((Bp, GDN_HEADS, GDN_DK, GDN_DV), f32)
    z_conv = jnp.zeros((Bp, CONV_W - 1, CONV_CH), x_prompt.dtype)
    xp, xs = x_prompt, x_sample
    rp, gp, cp, rs, gs, cs = [], [], [], [], [], []
    for l in range(DEPTH):
        lw = (attn_norm_w[l], w_in[l], conv_w[l], a_log[l], dt_bias[l], gdn_norm_w[l], w_branch_a[l],
              w_branch_b[l], w_out[l], ffn_norm_w[l], w_router[l], b_router[l], w_gate_up[l], b_gate_up[l],
              w_down[l], b_down[l])
        xp, a1, a2, a3 = decoder_layer(xp, pos_p, z_ret, z_gdn, z_conv, *lw)
        xs, b1, b2, b3 = decoder_layer(xs, pos_s, state_ret[l], state_gdn[l], state_conv[l], *lw)
        rp.append(a1); gp.append(a2); cp.append(a3)
        rs.append(b1); gs.append(b2); cs.append(b3)
    y_prompt = rmsnorm(xp, final_norm_w)
    y_sample = rmsnorm(xs, final_norm_w)
    ret_prompt = jnp.stack(rp).astype(state_ret.dtype)
    gdn_prompt = jnp.stack(gp).astype(state_gdn.dtype)
    conv_prompt = jnp.stack(cp).astype(state_conv.dtype)
    ret_sample = jnp.stack(rs).astype(state_ret.dtype)
    gdn_sample = jnp.stack(gs).astype(state_gdn.dtype)
    conv_sample = jnp.stack(cs).astype(state_conv.dtype)
    return (y_prompt, y_sample, ret_prompt, gdn_prompt, conv_prompt, ret_sample, gdn_sample, conv_sample)
```

```python
import functools
import math

import numpy as np
import jax
import jax.numpy as jnp
from jax import lax
from jax.experimental import pallas as pl
from jax.experimental.pallas import tpu as pltpu

F32 = jnp.float32
BF16 = jnp.bfloat16
HIGHEST = lax.Precision.HIGHEST

D_MODEL = 1024
PAST_LEN = 16384
RET_HEADS, RET_DK, RET_DV = 4, 128, 256
RET_QK_W, RET_V_W = RET_HEADS * RET_DK, RET_HEADS * RET_DV
RET_CHUNK = 128
ROPE_BASE = 10000.0
GDN_HEADS, GDN_DK, GDN_DV = 8, 128, 128
GDN_QK_W, GDN_V_W = GDN_HEADS * GDN_DK, GDN_HEADS * GDN_DV
GDN_CHUNK = 64
CONV_W = 4
CONV_CH = 2 * GDN_QK_W + GDN_V_W
N_EXPERTS = 32
TOP_K = 4
D_FF = D_MODEL
SWIGLU_LIMIT = 7.0
SWIGLU_ALPHA = 1.702
NORM_EPS = 1e-6

_RET_W = 2 * RET_QK_W + 2 * RET_V_W
_GDN_W = CONV_CH + GDN_V_W
_AB_OFF = _RET_W + _GDN_W
_GATE_OFF = _AB_OFF + 2 * GDN_HEADS

LANES = 128
VMEM_LIMIT = 56 * 1024 * 1024
MOE_ROWS = 256
SAMPLE_TILE = 8


def _pick_tile(n, candidates):
    for c in candidates:
        if n % c == 0:
            return c
    raise ValueError(f"no tile in {candidates} divides {n}")


def _params(sem, vmem=VMEM_LIMIT):
    return pltpu.CompilerParams(dimension_semantics=sem, vmem_limit_bytes=vmem)


def _resident(shape):
    nd = len(shape)
    return pl.BlockSpec(shape, lambda *_: (0,) * nd, pipeline_mode=pl.Buffered(1))


def _silu(x):
    return x * (1.0 / (1.0 + jnp.exp(-x)))


def _sigmoid(x):
    return 1.0 / (1.0 + jnp.exp(-x))


def _softplus(x):
    return jnp.maximum(x, 0.0) + jnp.log1p(jnp.exp(-jnp.abs(x)))


def _rms(x):
    return x * lax.rsqrt(jnp.mean(x * x, axis=-1, keepdims=True) + NORM_EPS)


def _dot(a, b):
    return jnp.dot(a, b, preferred_element_type=F32)


def _dot_nt(a, b):
    return lax.dot_general(a, b, (((1,), (1,)), ((), ())), preferred_element_type=F32)


def _dot_tn(a, b):
    return lax.dot_general(a, b, (((0,), (0,)), ((), ())), preferred_element_type=F32)


def _dot_hi(a, b):
    return jnp.dot(a, b, preferred_element_type=F32, precision=HIGHEST)


def _inproj_kernel(x_ref, nw_ref, wret_ref, wgdn_ref, wgate_ref, wab_ref, oret_ref, ogdn_ref, ogate_ref, oab_ref):
    h = (_rms(x_ref[...]) * nw_ref[...]).astype(BF16)
    oret_ref[...] = _dot(h, wret_ref[...]).astype(BF16)
    ogdn_ref[...] = _dot(h, wgdn_ref[...]).astype(BF16)
    ogate_ref[...] = _dot(h, wgate_ref[...]).astype(BF16)
    oab_ref[...] = _dot(h, wab_ref[...])


def _inproj(x, norm_w, w_in):
    t = x.shape[0]
    tm = _pick_tile(t, (384, 256, 128, 64, 8))
    wb = w_in.astype(BF16)
    w_ret = wb[:, :_RET_W]
    w_gdn = wb[:, _RET_W:_AB_OFF]
    w_ab = jnp.pad(wb[:, _AB_OFF:_GATE_OFF], ((0, 0), (0, LANES - 2 * GDN_HEADS)))
    w_gate = wb[:, _GATE_OFF:]
    row = lambda n: pl.BlockSpec((tm, n), lambda i: (i, 0))
    return pl.pallas_call(
        _inproj_kernel,
        grid=(t // tm,),
        in_specs=[row(D_MODEL), _resident((1, D_MODEL)), _resident(w_ret.shape), _resident(w_gdn.shape),
                  _resident(w_gate.shape), _resident(w_ab.shape)],
        out_specs=[row(_RET_W), row(_GDN_W), row(2 * D_MODEL), row(LANES)],
        out_shape=[jax.ShapeDtypeStruct((t, _RET_W), BF16), jax.ShapeDtypeStruct((t, _GDN_W), BF16),
                   jax.ShapeDtypeStruct((t, 2 * D_MODEL), BF16), jax.ShapeDtypeStruct((t, LANES), F32)],
        compiler_params=_params(("parallel",)),
        name="inproj",
    )(x, norm_w.reshape(1, D_MODEL), w_ret, w_gdn, w_gate, w_ab)


def _ret_log_gamma():
    return np.log1p(-np.exp2(-5.0 - np.arange(RET_HEADS, dtype=np.float64)))


def _rope_tables(pos):
    half = RET_DK // 2
    inv = 1.0 / (ROPE_BASE ** (jnp.arange(half, dtype=F32) / half))
    ang = pos.astype(F32)[:, None] * inv[None, :]
    cos, sin = jnp.cos(ang), jnp.sin(ang)
    return jnp.concatenate([cos, cos], axis=-1), jnp.concatenate([-sin, sin], axis=-1)


def _rotary(x, cos, sin):
    return x * cos + pltpu.roll(x, RET_DK // 2, 1) * sin


def _ret_prompt_kernel(q_ref, k_ref, v_ref, g_ref, cos_ref, sin_ref, dmask_ref, qdec_ref, kdec_ref,
                       o_ref, s_ref, *, gammas):
    @pl.when(pl.program_id(1) == 0)
    def _():
        s_ref[...] = jnp.zeros_like(s_ref)

    cos, sin = cos_ref[...], sin_ref[...]
    for h in range(RET_HEADS):
        qk = slice(h * RET_DK, (h + 1) * RET_DK)
        vv = slice(h * RET_DV, (h + 1) * RET_DV)
        q = _rotary(q_ref[:, qk].astype(F32), cos, sin)
        k = _rotary(k_ref[:, qk].astype(F32), cos, sin) * (RET_DK ** -0.5)
        v = v_ref[:, vv]
        s = s_ref[0, h]
        inner = _dot_nt(q.astype(BF16), k.astype(BF16)) * dmask_ref[h]
        o = _dot(inner.astype(BF16), v) + _dot((q * qdec_ref[h]).astype(BF16), s.astype(BF16))
        s_ref[0, h] = s * gammas[h] + _dot_tn((k * kdec_ref[h]).astype(BF16), v)
        o_ref[:, vv] = (_rms(o) * _silu(g_ref[:, vv].astype(F32))).astype(BF16)


def _ret_prompt(p_ret, batch, seq):
    c = RET_CHUNK
    n = seq // c
    lg = _ret_log_gamma()
    idx = np.arange(c, dtype=np.float64)
    diff = idx[:, None] - idx[None, :]
    dmask = np.where(diff >= 0, np.exp(np.maximum(diff, 0.0)[None] * lg[:, None, None]), 0.0)
    qdec = np.broadcast_to(np.exp((idx + 1.0)[None, :] * lg[:, None])[:, :, None], (RET_HEADS, c, RET_DK))
    kdec = np.broadcast_to(np.exp((c - 1.0 - idx)[None, :] * lg[:, None])[:, :, None], (RET_HEADS, c, RET_DK))
    gammas = tuple(float(g) for g in np.exp(c * lg))
    cos, sin = _rope_tables(jnp.arange(seq, dtype=jnp.int32))
    nq = RET_QK_W // RET_QK_W
    del nq
    tab = lambda: _resident((RET_HEADS, c, RET_DK))
    return pl.pallas_call(
        functools.partial(_ret_prompt_kernel, gammas=gammas),
        grid=(batch, n),
        in_specs=[pl.BlockSpec((c, RET_QK_W), lambda b, j: (b * n + j, 0)),
                  pl.BlockSpec((c, RET_QK_W), lambda b, j: (b * n + j, 1)),
                  pl.BlockSpec((c, RET_V_W), lambda b, j: (b * n + j, 1)),
                  pl.BlockSpec((c, RET_V_W), lambda b, j: (b * n + j, 2)),
                  pl.BlockSpec((c, RET_DK), lambda b, j: (j, 0)),
                  pl.BlockSpec((c, RET_DK), lambda b, j: (j, 0)),
                  tab(), tab(), tab()],
        out_specs=[pl.BlockSpec((c, RET_V_W), lambda b, j: (b * n + j, 0)),
                   pl.BlockSpec((1, RET_HEADS, RET_DK, RET_DV), lambda b, j: (b, 0, 0, 0))],
        out_shape=[jax.ShapeDtypeStruct((batch * seq, RET_V_W), BF16),
                   jax.ShapeDtypeStruct((batch, RET_HEADS, RET_DK, RET_DV), F32)],
        compiler_params=_params(("parallel", "arbitrary")),
        name="ret_prompt",
    )(p_ret, p_ret, p_ret, p_ret, cos, sin, jnp.asarray(dmask, F32), jnp.asarray(qdec, F32),
      jnp.asarray(kdec, F32))


def _columns(x):
    n = x.shape[0]
    if n < LANES:
        x = jnp.concatenate([x, jnp.zeros((LANES - n, x.shape[1]), x.dtype)], axis=0)
    return x.T


def _ret_sample_kernel(p_ref, cos_ref, sin_ref, s_ref, o_ref, so_ref, *, gammas):
    cos, sin = cos_ref[...], sin_ref[...]
    nb = p_ref.shape[0]
    for h in range(RET_HEADS):
        qk = slice(h * RET_DK, (h + 1) * RET_DK)
        q = _rotary(p_ref[:, qk], cos, sin)
        k = _rotary(p_ref[:, RET_QK_W + h * RET_DK:RET_QK_W + (h + 1) * RET_DK], cos, sin) * (RET_DK ** -0.5)
        v = p_ref[:, 2 * RET_QK_W + h * RET_DV:2 * RET_QK_W + (h + 1) * RET_DV]
        g = p_ref[:, 2 * RET_QK_W + RET_V_W + h * RET_DV:2 * RET_QK_W + RET_V_W + (h + 1) * RET_DV]
        qk_dot = jnp.sum(q * k, axis=-1, keepdims=True)
        qt, kt = _columns(q), _columns(k)
        rows = []
        for j in range(nb):
            s = s_ref[j, h]
            qs = jnp.sum(qt[:, j:j + 1] * s, axis=0, keepdims=True)
            rows.append(qk_dot[j:j + 1] * v[j:j + 1] + gammas[h] * qs)
            so_ref[j, h] = s * gammas[h] + kt[:, j:j + 1] * v[j:j + 1]
        o = jnp.concatenate(rows, axis=0)
        o_ref[:, h * RET_DV:(h + 1) * RET_DV] = (_rms(o) * _silu(g)).astype(BF16)


def _ret_sample(p_ret_s, state):
    ts = p_ret_s.shape[0]
    sb = SAMPLE_TILE
    gammas = tuple(float(g) for g in np.exp(_ret_log_gamma()))
    cos, sin = _rope_tables(jnp.full((1,), PAST_LEN, jnp.int32))
    st = pl.BlockSpec((sb, RET_HEADS, RET_DK, RET_DV), lambda i: (i, 0, 0, 0))
    return pl.pallas_call(
        functools.partial(_ret_sample_kernel, gammas=gammas),
        grid=(ts // sb,),
        in_specs=[pl.BlockSpec((sb, _RET_W), lambda i: (i, 0)), _resident((1, RET_DK)), _resident((1, RET_DK)), st],
        out_specs=[pl.BlockSpec((sb, RET_V_W), lambda i: (i, 0)), st],
        out_shape=[jax.ShapeDtypeStruct((ts, RET_V_W), BF16), jax.ShapeDtypeStruct(state.shape, F32)],
        compiler_params=_params(("parallel",)),
        name="ret_sample",
    )(p_ret_s, cos, sin, state)


def _l2norm(x):
    return x * lax.rsqrt(jnp.sum(x * x, axis=-1, keepdims=True) + NORM_EPS)


def _unit_lower_inverse(a):
    c = a.shape[0]
    eye = (lax.broadcasted_iota(jnp.int32, (c, c), 0) == lax.broadcasted_iota(jnp.int32, (c, c), 1)).astype(F32)
    x = eye - a
    p = _dot_hi(a, a)
    steps = int(math.log2(c)) - 1
    for i in range(steps):
        x = x + _dot_hi(x, p)
        if i + 1 < steps:
            p = _dot_hi(p, p)
    return x


def _gdn_gates(ab, alog, dtb):
    g = -jnp.exp(alog) * _softplus(ab + dtb)
    return g, _sigmoid(ab)


def _gdn_prompt_kernel(x_ref, z_ref, ab_ref, cw_ref, alog_ref, dtb_ref, nw_ref, o_ref, s_ref, cv_ref, xc_ref):
    c = GDN_CHUNK
    first = pl.program_id(1) == 0

    @pl.when(first)
    def _():
        s_ref[...] = jnp.zeros_like(s_ref)
        xc_ref[0:8, :] = jnp.zeros((8, CONV_CH), F32)

    xc_ref[8:8 + c, :] = x_ref[...].astype(F32)
    acc = xc_ref[5:5 + c, :] * cw_ref[0:1, :]
    for i in range(1, CONV_W):
        acc = acc + xc_ref[5 + i:5 + i + c, :] * cw_ref[i:i + 1, :]
    tail = xc_ref[c:c + 8, :]
    xc_ref[0:8, :] = tail
    cv_ref[0] = tail[8 - (CONV_W - 1):, :]
    u = _silu(acc)

    g_all, beta_all = _gdn_gates(ab_ref[...], alog_ref[...], dtb_ref[...])
    ri = lax.broadcasted_iota(jnp.int32, (c, c), 0)
    ci = lax.broadcasted_iota(jnp.int32, (c, c), 1)
    lower, strict = ri >= ci, ri > ci
    gc_all = _dot_hi(lower.astype(F32), g_all)
    gr_all = _columns(gc_all)
    nw = nw_ref[...]
    for h in range(GDN_HEADS):
        sl = slice(h * GDN_DK, (h + 1) * GDN_DK)
        q = _l2norm(u[:, sl]) * (GDN_DK ** -0.5)
        k = _l2norm(u[:, GDN_QK_W + h * GDN_DK:GDN_QK_W + (h + 1) * GDN_DK])
        v = u[:, 2 * GDN_QK_W + h * GDN_DV:2 * GDN_QK_W + (h + 1) * GDN_DV]
        beta = beta_all[:, GDN_HEADS + h:GDN_HEADS + h + 1]
        gc = gc_all[:, h:h + 1]
        gr = gr_all[h:h + 1, :c]
        g_last = gc[c - 1:c, :]
        decay = jnp.exp(jnp.where(lower, gc - gr, -jnp.inf))
        exp_g = jnp.exp(gc)
        kb = k * beta
        kbf = k.astype(BF16)
        a = _dot_nt(kb.astype(BF16), kbf) * jnp.where(strict, decay, 0.0)
        t = _unit_lower_inverse(a)
        uu = _dot_hi(t, v * beta)
        w = _dot_hi(t, kb * exp_g)
        s = s_ref[0, h]
        sb = s.astype(BF16)
        v_new = uu - _dot(w.astype(BF16), sb)
        attn = _dot_nt(q.astype(BF16), kbf) * decay
        vnb = v_new.astype(BF16)
        o = _dot((q * exp_g).astype(BF16), sb) + _dot(attn.astype(BF16), vnb)
        s_ref[0, h] = s * jnp.exp(g_last) + _dot_tn((k * jnp.exp(g_last - gc)).astype(BF16), vnb)
        og = _rms(o) * nw * _silu(z_ref[:, sl].astype(F32))
        o_ref[:, sl] = og.astype(BF16)


def _gdn_prompt(p_gdn, p_ab, batch, seq, conv_w, a_log, dt_bias, gdn_norm_w):
    c = GDN_CHUNK
    n = seq // c
    nq = CONV_CH // GDN_V_W
    alog = jnp.pad(a_log.astype(F32), (0, LANES - GDN_HEADS)).reshape(1, LANES)
    dtb = jnp.pad(dt_bias.astype(F32), (0, LANES - GDN_HEADS)).reshape(1, LANES)
    return pl.pallas_call(
        _gdn_prompt_kernel,
        grid=(batch, n),
        in_specs=[pl.BlockSpec((c, CONV_CH), lambda b, j: (b * n + j, 0)),
                  pl.BlockSpec((c, GDN_V_W), lambda b, j: (b * n + j, nq)),
                  pl.BlockSpec((c, LANES), lambda b, j: (b * n + j, 0)),
                  _resident((CONV_W, CONV_CH)), _resident((1, LANES)), _resident((1, LANES)),
                  _resident((1, GDN_DV))],
        out_specs=[pl.BlockSpec((c, GDN_V_W), lambda b, j: (b * n + j, 0)),
                   pl.BlockSpec((1, GDN_HEADS, GDN_DK, GDN_DV), lambda b, j: (b, 0, 0, 0)),
                   pl.BlockSpec((1, CONV_W - 1, CONV_CH), lambda b, j: (b, 0, 0))],
        out_shape=[jax.ShapeDtypeStruct((batch * seq, GDN_V_W), BF16),
                   jax.ShapeDtypeStruct((batch, GDN_HEADS, GDN_DK, GDN_DV), F32),
                   jax.ShapeDtypeStruct((batch, CONV_W - 1, CONV_CH), F32)],
        scratch_shapes=[pltpu.VMEM((c + 8, CONV_CH), F32)],
        compiler_params=_params(("parallel", "arbitrary")),
        name="gdn_prompt",
    )(p_gdn, p_gdn, p_ab, conv_w.astype(F32), alog, dtb, gdn_norm_w.astype(F32).reshape(1, GDN_DV))


def _gdn_sample_kernel(x_ref, ab_ref, sc_ref, cw_ref, alog_ref, dtb_ref, nw_ref, s_ref, o_ref, so_ref, sco_ref):
    nb = x_ref.shape[0]
    x = x_ref[:, :CONV_CH]
    acc = x * cw_ref[CONV_W - 1:CONV_W, :]
    for i in range(CONV_W - 1):
        acc = acc + sc_ref[i] * cw_ref[i:i + 1, :]
    for i in range(CONV_W - 2):
        sco_ref[i] = sc_ref[i + 1]
    sco_ref[CONV_W - 2] = x
    u = _silu(acc)
    g_all, beta_all = _gdn_gates(ab_ref[...], alog_ref[...], dtb_ref[...])
    eg_all = jnp.exp(g_all)
    nw = nw_ref[...]
    for h in range(GDN_HEADS):
        sl = slice(h * GDN_DK, (h + 1) * GDN_DK)
        q = _l2norm(u[:, sl]) * (GDN_DK ** -0.5)
        k = _l2norm(u[:, GDN_QK_W + h * GDN_DK:GDN_QK_W + (h + 1) * GDN_DK])
        v = u[:, 2 * GDN_QK_W + h * GDN_DV:2 * GDN_QK_W + (h + 1) * GDN_DV]
        beta = beta_all[:, GDN_HEADS + h:GDN_HEADS + h + 1]
        eg = eg_all[:, h:h + 1]
        qk_dot = jnp.sum(q * k, axis=-1, keepdims=True)
        qt, kt = _columns(q), _columns(k)
        rows = []
        for j in range(nb):
            s = s_ref[j, h]
            kcol = kt[:, j:j + 1]
            ks = jnp.sum(kcol * s, axis=0, keepdims=True)
            qs = jnp.sum(qt[:, j:j + 1] * s, axis=0, keepdims=True)
            ej = eg[j:j + 1]
            v_new = beta[j:j + 1] * (v[j:j + 1] - ej * ks)
            rows.append(ej * qs + qk_dot[j:j + 1] * v_new)
            so_ref[j, h] = s * ej + kcol * v_new
        o = jnp.concatenate(rows, axis=0)
        z = x_ref[:, CONV_CH + h * GDN_DV:CONV_CH + (h + 1) * GDN_DV]
        o_ref[:, sl] = (_rms(o) * nw * _silu(z)).astype(BF16)


def _gdn_sample(p_gdn_s, p_ab_s, state, conv_state, conv_w, a_log, dt_bias, gdn_norm_w):
    ts = p_gdn_s.shape[0]
    sb = SAMPLE_TILE
    alog = jnp.pad(a_log.astype(F32), (0, LANES - GDN_HEADS)).reshape(1, LANES)
    dtb = jnp.pad(dt_bias.astype(F32), (0, LANES - GDN_HEADS)).reshape(1, LANES)
    sc = jnp.swapaxes(conv_state.astype(F32), 0, 1)
    st = pl.BlockSpec((sb, GDN_HEADS, GDN_DK, GDN_DV), lambda i: (i, 0, 0, 0))
    scs = pl.BlockSpec((CONV_W - 1, sb, CONV_CH), lambda i: (0, i, 0))
    o, s_new, sc_new = pl.pallas_call(
        _gdn_sample_kernel,
        grid=(ts // sb,),
        in_specs=[pl.BlockSpec((sb, _GDN_W), lambda i: (i, 0)), pl.BlockSpec((sb, LANES), lambda i: (i, 0)), scs,
                  _resident((CONV_W, CONV_CH)), _resident((1, LANES)), _resident((1, LANES)),
                  _resident((1, GDN_DV)), st],
        out_specs=[pl.BlockSpec((sb, GDN_V_W), lambda i: (i, 0)), st, scs],
        out_shape=[jax.ShapeDtypeStruct((ts, GDN_V_W), BF16), jax.ShapeDtypeStruct(state.shape, F32),
                   jax.ShapeDtypeStruct(sc.shape, F32)],
        compiler_params=_params(("parallel",)),
        name="gdn_sample",
    )(p_gdn_s, p_ab_s, sc, conv_w.astype(F32), alog, dtb, gdn_norm_w.astype(F32).reshape(1, GDN_DV), state)
    return o, s_new, jnp.swapaxes(sc_new, 0, 1)


def _merge_kernel(oa_ref, ob_ref, gate_ref, x_ref, wa_ref, wb_ref, wo_ref, nw_ref, wr_ref, br_ref,
                  x1_ref, h2_ref, lg_ref):
    ya = _dot(oa_ref[...], wa_ref[...])
    yb = _dot(ob_ref[...], wb_ref[...])
    ga = gate_ref[:, :D_MODEL].astype(F32)
    gb = gate_ref[:, D_MODEL:].astype(F32)
    m = _sigmoid(ga) * ya + _sigmoid(gb) * yb
    x1 = x_ref[...] + _dot(m.astype(BF16), wo_ref[...])
    x1_ref[...] = x1
    h2 = _rms(x1) * nw_ref[...]
    h2_ref[...] = h2
    lg_ref[...] = _dot_hi(h2, wr_ref[...]) + br_ref[...]


def _merge(o_ret, o_gdn, p_gate, x, w_a, w_b, w_o, ffn_norm_w, w_router, b_router):
    t = x.shape[0]
    tm = _pick_tile(t, (384, 256, 128, 64, 8))
    wr = jnp.pad(w_router.astype(F32), ((0, 0), (0, LANES - N_EXPERTS)))
    br = jnp.pad(b_router.astype(F32), (0, LANES - N_EXPERTS)).reshape(1, LANES)
    row = lambda n: pl.BlockSpec((tm, n), lambda i: (i, 0))
    sq = (D_MODEL, D_MODEL)
    return pl.pallas_call(
        _merge_kernel,
        grid=(t // tm,),
        in_specs=[row(RET_V_W), row(GDN_V_W), row(2 * D_MODEL), row(D_MODEL), _resident(sq), _resident(sq),
                  _resident(sq), _resident((1, D_MODEL)), _resident((D_MODEL, LANES)), _resident((1, LANES))],
        out_specs=[row(D_MODEL), row(D_MODEL), row(LANES)],
        out_shape=[jax.ShapeDtypeStruct((t, D_MODEL), F32), jax.ShapeDtypeStruct((t, D_MODEL), F32),
                   jax.ShapeDtypeStruct((t, LANES), F32)],
        compiler_params=_params(("parallel",)),
        name="merge",
    )(o_ret, o_gdn, p_gate, x, w_a.astype(BF16), w_b.astype(BF16), w_o.astype(BF16),
      ffn_norm_w.astype(F32).reshape(1, D_MODEL), wr, br)


def _route(logits, n_tokens):
    rows = MOE_ROWS
    top_v, top_i = lax.top_k(logits, TOP_K)
    gate = jax.nn.softmax(top_v, axis=-1)
    n = n_tokens * TOP_K
    flat_e = top_i.reshape(n).astype(jnp.int32)
    order = jnp.argsort(flat_e).astype(jnp.int32)
    counts = jnp.bincount(flat_e, length=N_EXPERTS).astype(jnp.int32)
    start = jnp.cumsum(counts) - counts
    pcounts = (counts + rows - 1) // rows * rows
    pend = jnp.cumsum(pcounts)
    pstart = pend - pcounts
    nb = -(-n // rows) + N_EXPERTS
    block_e = jnp.minimum(jnp.searchsorted(pend, jnp.arange(nb, dtype=jnp.int32) * rows, side='right'),
                          N_EXPERTS - 1).astype(jnp.int32)
    nb_used = (pend[-1] // rows).astype(jnp.int32).reshape(1)
    r = jnp.arange(nb * rows, dtype=jnp.int32)
    e_r = jnp.repeat(block_e, rows)
    within = r - pstart[e_r]
    valid = (within < counts[e_r]) & (r < pend[-1])
    flat = order[jnp.clip(start[e_r] + within, 0, n - 1)]
    spare = n + (r % (2 * rows))
    src = jnp.where(valid, flat // TOP_K, 0).reshape(nb, rows)
    dst = jnp.where(valid, flat, spare).reshape(nb, rows)
    nxt = jnp.concatenate([src[1:], src[-1:]], axis=0)
    slab = jnp.concatenate([src, dst, nxt], axis=1)
    return gate, block_e, nb_used, slab


def _expert_kernel(be_ref, nbu_ref, slab_ref, h_ref, wgu_ref, bgu_ref, wd_ref, bd_ref, y_ref,
                   idx_ref, xbuf_ref, ybuf_ref, wgu_bf_ref, wd_bf_ref, isem, gsem, ssem):
    rows = MOE_ROWS
    i = pl.program_id(0)
    nbu = nbu_ref[0]
    slot = i % 2
    active = i < nbu

    def slab_copy(blk, sl):
        return pltpu.make_async_copy(slab_ref.at[blk], idx_ref.at[sl], isem.at[sl])

    def gather_all(sl, base):
        def body(r, carry):
            tok = idx_ref[slot, base + r]
            pltpu.make_async_copy(h_ref.at[pl.ds(tok, 1)], xbuf_ref.at[sl, pl.ds(r, 1)], gsem.at[sl]).start()
            return carry
        lax.fori_loop(0, rows, body, 0, unroll=8)

    def gather_wait(sl):
        pltpu.make_async_copy(h_ref.at[pl.ds(0, rows)], xbuf_ref.at[sl], gsem.at[sl]).wait()

    def scatter_wait(sl):
        pltpu.make_async_copy(ybuf_ref.at[sl], y_ref.at[pl.ds(0, rows)], ssem.at[sl]).wait()

    @pl.when(i == 0)
    def _():
        slab_copy(0, 0).start()
        n_real = y_ref.shape[0] - 2 * rows
        ybuf_ref[...] = jnp.zeros_like(ybuf_ref)
        for sl in range(2):
            pltpu.make_async_copy(ybuf_ref.at[sl], y_ref.at[pl.ds(n_real + sl * rows, rows)], ssem.at[sl]).start()
        for sl in range(2):
            scatter_wait(sl)

    @pl.when(active)
    def _():
        slab_copy(i, slot).wait()

        @pl.when(i + 1 < nbu)
        def _():
            slab_copy(i + 1, 1 - slot).start()

        @pl.when(i == 0)
        def _():
            gather_all(0, 0)

        @pl.when(i + 1 < nbu)
        def _():
            gather_all(1 - slot, 2 * rows)

        changed = jnp.logical_or(i == 0, be_ref[i] != be_ref[jnp.maximum(i - 1, 0)])

        @pl.when(changed)
        def _():
            wgu_bf_ref[...] = wgu_ref[0].astype(BF16)
            wd_bf_ref[...] = wd_ref[0].astype(BF16)

        gather_wait(slot)

        @pl.when(i >= 2)
        def _():
            scatter_wait(slot)

        xb = xbuf_ref[slot].astype(BF16)
        hb = _dot(xb, wgu_bf_ref[...]) + bgu_ref[0]
        glu = jnp.minimum(hb[:, :D_FF], SWIGLU_LIMIT)
        lin = jnp.clip(hb[:, D_FF:], -SWIGLU_LIMIT, SWIGLU_LIMIT)
        act = (glu * _sigmoid(SWIGLU_ALPHA * glu) * (lin + 1.0)).astype(BF16)
        ybuf_ref[slot] = _dot(act, wd_bf_ref[...]) + bd_ref[0]

        def body(r, carry):
            dst = idx_ref[slot, rows + r]
            pltpu.make_async_copy(ybuf_ref.at[slot, pl.ds(r, 1)], y_ref.at[pl.ds(dst, 1)], ssem.at[slot]).start()
            return carry
        lax.fori_loop(0, rows, body, 0, unroll=8)

        @pl.when(i == nbu - 1)
        def _():
            scatter_wait(slot)

            @pl.when(i >= 1)
            def _():
                scatter_wait(1 - slot)


def _experts(h2, block_e, nb_used, slab, w_gate_up, b_gate_up, w_down, b_down):
    t = h2.shape[0]
    rows = MOE_ROWS
    nb = slab.shape[0]
    grid_spec = pltpu.PrefetchScalarGridSpec(
        num_scalar_prefetch=2,
        grid=(nb,),
        in_specs=[pl.BlockSpec(memory_space=pl.ANY),
                  pl.BlockSpec(memory_space=pl.ANY),
                  pl.BlockSpec((1, D_MODEL, 2 * D_FF), lambda i, be, nbu: (be[i], 0, 0)),
                  pl.BlockSpec((1, 1, 2 * D_FF), lambda i, be, nbu: (be[i], 0, 0)),
                  pl.BlockSpec((1, D_FF, D_MODEL), lambda i, be, nbu: (be[i], 0, 0)),
                  pl.BlockSpec((1, 1, D_MODEL), lambda i, be, nbu: (be[i], 0, 0))],
        out_specs=pl.BlockSpec(memory_space=pl.ANY),
        scratch_shapes=[pltpu.SMEM((2, 3 * rows), jnp.int32),
                        pltpu.VMEM((2, rows, D_MODEL), F32),
                        pltpu.VMEM((2, rows, D_MODEL), F32),
                        pltpu.VMEM((D_MODEL, 2 * D_FF), BF16),
                        pltpu.VMEM((D_FF, D_MODEL), BF16),
                        pltpu.SemaphoreType.DMA((2,)),
                        pltpu.SemaphoreType.DMA((2,)),
                        pltpu.SemaphoreType.DMA((2,))])
    return pl.pallas_call(
        _expert_kernel,
        grid_spec=grid_spec,
        out_shape=jax.ShapeDtypeStruct((t * TOP_K + 2 * rows, D_MODEL), F32),
        compiler_params=_params(("arbitrary",)),
        name="experts",
    )(block_e, nb_used, slab, h2, w_gate_up, b_gate_up.reshape(N_EXPERTS, 1, 2 * D_FF), w_down,
      b_down.reshape(N_EXPERTS, 1, D_MODEL))


def _combine_kernel(y_ref, w_ref, x1_ref, nw_ref, o_ref, *, final):
    acc = x1_ref[...]
    for k in range(TOP_K):
        acc = acc + w_ref[:, k:k + 1] * y_ref[:, k * D_MODEL:(k + 1) * D_MODEL]
    o_ref[...] = _rms(acc) * nw_ref[...] if final else acc


def _combine(y, gate, x1, norm_w, final):
    t = x1.shape[0]
    tm = _pick_tile(t, (384, 256, 128, 64, 8))
    y2 = y.reshape(y.shape[0] // TOP_K, TOP_K * D_MODEL)
    return pl.pallas_call(
        functools.partial(_combine_kernel, final=final),
        grid=(t // tm,),
        in_specs=[pl.BlockSpec((tm, TOP_K * D_MODEL), lambda i: (i, 0)),
                  pl.BlockSpec((tm, TOP_K), lambda i: (i, 0)),
                  pl.BlockSpec((tm, D_MODEL), lambda i: (i, 0)), _resident((1, D_MODEL))],
        out_specs=pl.BlockSpec((tm, D_MODEL), lambda i: (i, 0)),
        out_shape=jax.ShapeDtypeStruct((t, D_MODEL), F32),
        compiler_params=_params(("parallel",)),
        name="combine",
    )(y2, gate, x1, norm_w.astype(F32).reshape(1, D_MODEL))


def kernel(x_prompt, x_sample, state_ret, state_gdn, state_conv, attn_norm_w, w_in, conv_w, a_log, dt_bias, gdn_norm_w, w_branch_a, w_branch_b, w_out, ffn_norm_w, w_router, b_router, w_gate_up, b_gate_up, w_down, b_down, final_norm_w):
    bp, lp, d = x_prompt.shape
    bs, ls, _ = x_sample.shape
    assert ls == 1 and d == D_MODEL and lp % RET_CHUNK == 0 and bs % SAMPLE_TILE == 0
    depth = w_in.shape[0]
    tp = bp * lp
    x = jnp.concatenate([x_prompt.reshape(tp, d), x_sample.reshape(bs, d)], axis=0).astype(F32)
    t = tp + bs
    rp, gp, cp, rs, gs, cs = [], [], [], [], [], []
    for l in range(depth):
        p_ret, p_gdn, p_gate, p_ab = _inproj(x, attn_norm_w[l], w_in[l])
        o_ret_p, s_ret_p = _ret_prompt(p_ret, bp, lp)
        o_ret_s, s_ret_s = _ret_sample(p_ret[tp:].astype(F32), state_ret[l].astype(F32))
        o_gdn_p, s_gdn_p, conv_p = _gdn_prompt(p_gdn, p_ab, bp, lp, conv_w[l], a_log[l], dt_bias[l], gdn_norm_w[l])
        o_gdn_s, s_gdn_s, conv_s = _gdn_sample(p_gdn[tp:].astype(F32), p_ab[tp:], state_gdn[l].astype(F32),
                                               state_conv[l], conv_w[l], a_log[l], dt_bias[l], gdn_norm_w[l])
        o_ret = jnp.concatenate([o_ret_p, o_ret_s], axis=0)
        o_gdn = jnp.concatenate([o_gdn_p, o_gdn_s], axis=0)
        x1, h2, logits = _merge(o_ret, o_gdn, p_gate, x, w_branch_a[l], w_branch_b[l], w_out[l], ffn_norm_w[l],
                                w_router[l], b_router[l])
        gate, block_e, nb_used, slab = _route(logits[:, :N_EXPERTS], t)
        y = _experts(h2, block_e, nb_used, slab, w_gate_up[l], b_gate_up[l], w_down[l], b_down[l])
        last = l == depth - 1
        x = _combine(y, gate, x1, final_norm_w if last else jnp.ones((d,), F32), last)
        rp.append(s_ret_p); gp.append(s_gdn_p); cp.append(conv_p)
        rs.append(s_ret_s); gs.append(s_gdn_s); cs.append(conv_s)
    y_prompt = x[:tp].reshape(bp, lp, d).astype(x_prompt.dtype)
    y_sample = x[tp:].reshape(bs, ls, d).astype(x_sample.dtype)
    return (y_prompt, y_sample,
            jnp.stack(rp).astype(state_ret.dtype), jnp.stack(gp).astype(state_gdn.dtype),
            jnp.stack(cp).astype(state_conv.dtype),
            jnp.stack(rs).astype(state_ret.dtype), jnp.stack(gs).astype(state_gdn.dtype),
            jnp.stack(cs).astype(state_conv.dtype))
```

```python
import functools
import math

import numpy as np
import jax
import jax.numpy as jnp
from jax import lax
from jax.experimental import pallas as pl
from jax.experimental.pallas import tpu as pltpu

F32 = jnp.float32
BF16 = jnp.bfloat16
HIGHEST = lax.Precision.HIGHEST

D_MODEL = 1024
PAST_LEN = 16384
RET_HEADS, RET_DK, RET_DV = 4, 128, 256
RET_QK_W, RET_V_W = RET_HEADS * RET_DK, RET_HEADS * RET_DV
RET_CHUNK = 128
ROPE_BASE = 10000.0
GDN_HEADS, GDN_DK, GDN_DV = 8, 128, 128
GDN_QK_W, GDN_V_W = GDN_HEADS * GDN_DK, GDN_HEADS * GDN_DV
GDN_CHUNK = 64
CONV_W = 4
CONV_CH = 2 * GDN_QK_W + GDN_V_W
N_EXPERTS = 32
TOP_K = 4
D_FF = D_MODEL
SWIGLU_LIMIT = 7.0
SWIGLU_ALPHA = 1.702
NORM_EPS = 1e-6

_RET_W = 2 * RET_QK_W + 2 * RET_V_W
_GDN_W = CONV_CH + GDN_V_W
_AB_OFF = _RET_W + _GDN_W
_GATE_OFF = _AB_OFF + 2 * GDN_HEADS

LANES = 128
VMEM_LIMIT = 56 * 1024 * 1024
MOE_ROWS = 256
SAMPLE_TILE = 8
GDN_PREP_ROWS = 128


def _pick_tile(n, candidates):
    for c in candidates:
        if n % c == 0:
            return c
    raise ValueError(f"no tile in {candidates} divides {n}")


def _params(sem, vmem=VMEM_LIMIT):
    return pltpu.CompilerParams(dimension_semantics=sem, vmem_limit_bytes=vmem)


def _resident(shape):
    nd = len(shape)
    return pl.BlockSpec(shape, lambda *_: (0,) * nd, pipeline_mode=pl.Buffered(1))


def _silu(x):
    return x * (1.0 / (1.0 + jnp.exp(-x)))


def _sigmoid(x):
    return 1.0 / (1.0 + jnp.exp(-x))


def _softplus(x):
    return jnp.maximum(x, 0.0) + jnp.log1p(jnp.exp(-jnp.abs(x)))


def _rms(x):
    return x * lax.rsqrt(jnp.mean(x * x, axis=-1, keepdims=True) + NORM_EPS)


def _dot(a, b):
    return jnp.dot(a, b, preferred_element_type=F32)


def _dot_nt(a, b):
    return lax.dot_general(a, b, (((1,), (1,)), ((), ())), preferred_element_type=F32)


def _dot_tn(a, b):
    return lax.dot_general(a, b, (((0,), (0,)), ((), ())), preferred_element_type=F32)


def _dot_hi(a, b):
    return jnp.dot(a, b, preferred_element_type=F32, precision=HIGHEST)


def _inproj_kernel(x_ref, nw_ref, wret_ref, wgdn_ref, wgate_ref, wab_ref, oret_ref, ogdn_ref, ogate_ref, oab_ref):
    h = (_rms(x_ref[...]) * nw_ref[...]).astype(BF16)
    oret_ref[...] = _dot(h, wret_ref[...]).astype(BF16)
    ogdn_ref[...] = _dot(h, wgdn_ref[...]).astype(BF16)
    ogate_ref[...] = _dot(h, wgate_ref[...]).astype(BF16)
    oab_ref[...] = _dot(h, wab_ref[...])


def _inproj(x, norm_w, w_in):
    t = x.shape[0]
    tm = _pick_tile(t, (384, 256, 128, 64, 8))
    wb = w_in.astype(BF16)
    w_ret = wb[:, :_RET_W]
    w_gdn = wb[:, _RET_W:_AB_OFF]
    w_ab = jnp.pad(wb[:, _AB_OFF:_GATE_OFF], ((0, 0), (0, LANES - 2 * GDN_HEADS)))
    w_gate = wb[:, _GATE_OFF:]
    row = lambda n: pl.BlockSpec((tm, n), lambda i: (i, 0))
    return pl.pallas_call(
        _inproj_kernel,
        grid=(t // tm,),
        in_specs=[row(D_MODEL), _resident((1, D_MODEL)), _resident(w_ret.shape), _resident(w_gdn.shape),
                  _resident(w_gate.shape), _resident(w_ab.shape)],
        out_specs=[row(_RET_W), row(_GDN_W), row(2 * D_MODEL), row(LANES)],
        out_shape=[jax.ShapeDtypeStruct((t, _RET_W), BF16), jax.ShapeDtypeStruct((t, _GDN_W), BF16),
                   jax.ShapeDtypeStruct((t, 2 * D_MODEL), BF16), jax.ShapeDtypeStruct((t, LANES), F32)],
        compiler_params=_params(("parallel",)),
        name="inproj",
    )(x, norm_w.reshape(1, D_MODEL), w_ret, w_gdn, w_gate, w_ab)


def _ret_log_gamma():
    return np.log1p(-np.exp2(-5.0 - np.arange(RET_HEADS, dtype=np.float64)))


def _rope_tables(pos):
    half = RET_DK // 2
    inv = 1.0 / (ROPE_BASE ** (jnp.arange(half, dtype=F32) / half))
    ang = pos.astype(F32)[:, None] * inv[None, :]
    cos, sin = jnp.cos(ang), jnp.sin(ang)
    return jnp.concatenate([cos, cos], axis=-1), jnp.concatenate([-sin, sin], axis=-1)


def _rotary(x, cos, sin):
    return x * cos + pltpu.roll(x, RET_DK // 2, 1) * sin


def _ret_prompt_kernel(q_ref, k_ref, v_ref, g_ref, cos_ref, sin_ref, dmask_ref, qdec_ref, kdec_ref,
                       o_ref, s_ref, *, gammas):
    @pl.when(pl.program_id(1) == 0)
    def _():
        s_ref[...] = jnp.zeros_like(s_ref)

    cos, sin = cos_ref[...], sin_ref[...]
    for h in range(RET_HEADS):
        qk = slice(h * RET_DK, (h + 1) * RET_DK)
        vv = slice(h * RET_DV, (h + 1) * RET_DV)
        q = _rotary(q_ref[:, qk].astype(F32), cos, sin)
        k = _rotary(k_ref[:, qk].astype(F32), cos, sin) * (RET_DK ** -0.5)
        v = v_ref[:, vv]
        s = s_ref[0, h]
        inner = _dot_nt(q.astype(BF16), k.astype(BF16)) * dmask_ref[h]
        o = _dot(inner.astype(BF16), v) + _dot((q * qdec_ref[h]).astype(BF16), s.astype(BF16))
        s_ref[0, h] = s * gammas[h] + _dot_tn((k * kdec_ref[h]).astype(BF16), v)
        o_ref[:, vv] = (_rms(o) * _silu(g_ref[:, vv].astype(F32))).astype(BF16)


def _ret_prompt(p_ret, batch, seq):
    c = RET_CHUNK
    n = seq // c
    lg = _ret_log_gamma()
    idx = np.arange(c, dtype=np.float64)
    diff = idx[:, None] - idx[None, :]
    dmask = np.where(diff >= 0, np.exp(np.maximum(diff, 0.0)[None] * lg[:, None, None]), 0.0)
    qdec = np.broadcast_to(np.exp((idx + 1.0)[None, :] * lg[:, None])[:, :, None], (RET_HEADS, c, RET_DK))
    kdec = np.broadcast_to(np.exp((c - 1.0 - idx)[None, :] * lg[:, None])[:, :, None], (RET_HEADS, c, RET_DK))
    gammas = tuple(float(g) for g in np.exp(c * lg))
    cos, sin = _rope_tables(jnp.arange(seq, dtype=jnp.int32))
    nq = RET_QK_W // RET_QK_W
    del nq
    tab = lambda: _resident((RET_HEADS, c, RET_DK))
    return pl.pallas_call(
        functools.partial(_ret_prompt_kernel, gammas=gammas),
        grid=(batch, n),
        in_specs=[pl.BlockSpec((c, RET_QK_W), lambda b, j: (b * n + j, 0)),
                  pl.BlockSpec((c, RET_QK_W), lambda b, j: (b * n + j, 1)),
                  pl.BlockSpec((c, RET_V_W), lambda b, j: (b * n + j, 1)),
                  pl.BlockSpec((c, RET_V_W), lambda b, j: (b * n + j, 2)),
                  pl.BlockSpec((c, RET_DK), lambda b, j: (j, 0)),
                  pl.BlockSpec((c, RET_DK), lambda b, j: (j, 0)),
                  tab(), tab(), tab()],
        out_specs=[pl.BlockSpec((c, RET_V_W), lambda b, j: (b * n + j, 0)),
                   pl.BlockSpec((1, RET_HEADS, RET_DK, RET_DV), lambda b, j: (b, 0, 0, 0))],
        out_shape=[jax.ShapeDtypeStruct((batch * seq, RET_V_W), BF16),
                   jax.ShapeDtypeStruct((batch, RET_HEADS, RET_DK, RET_DV), F32)],
        compiler_params=_params(("parallel", "arbitrary")),
        name="ret_prompt",
    )(p_ret, p_ret, p_ret, p_ret, cos, sin, jnp.asarray(dmask, F32), jnp.asarray(qdec, F32),
      jnp.asarray(kdec, F32))


def _columns(x):
    n = x.shape[0]
    if n < LANES:
        x = jnp.concatenate([x, jnp.zeros((LANES - n, x.shape[1]), x.dtype)], axis=0)
    return x.T


def _ret_sample_kernel(p_ref, cos_ref, sin_ref, s_ref, o_ref, so_ref, *, gammas):
    cos, sin = cos_ref[...], sin_ref[...]
    nb = p_ref.shape[0]
    for h in range(RET_HEADS):
        qk = slice(h * RET_DK, (h + 1) * RET_DK)
        q = _rotary(p_ref[:, qk], cos, sin)
        k = _rotary(p_ref[:, RET_QK_W + h * RET_DK:RET_QK_W + (h + 1) * RET_DK], cos, sin) * (RET_DK ** -0.5)
        v = p_ref[:, 2 * RET_QK_W + h * RET_DV:2 * RET_QK_W + (h + 1) * RET_DV]
        g = p_ref[:, 2 * RET_QK_W + RET_V_W + h * RET_DV:2 * RET_QK_W + RET_V_W + (h + 1) * RET_DV]
        qk_dot = jnp.sum(q * k, axis=-1, keepdims=True)
        qt, kt = _columns(q), _columns(k)
        rows = []
        for j in range(nb):
            s = s_ref[j, h]
            qs = jnp.sum(qt[:, j:j + 1] * s, axis=0, keepdims=True)
            rows.append(qk_dot[j:j + 1] * v[j:j + 1] + gammas[h] * qs)
            so_ref[j, h] = s * gammas[h] + kt[:, j:j + 1] * v[j:j + 1]
        o = jnp.concatenate(rows, axis=0)
        o_ref[:, h * RET_DV:(h + 1) * RET_DV] = (_rms(o) * _silu(g)).astype(BF16)


def _ret_sample(p_ret_s, state):
    ts = p_ret_s.shape[0]
    sb = SAMPLE_TILE
    gammas = tuple(float(g) for g in np.exp(_ret_log_gamma()))
    cos, sin = _rope_tables(jnp.full((1,), PAST_LEN, jnp.int32))
    st = pl.BlockSpec((sb, RET_HEADS, RET_DK, RET_DV), lambda i: (i, 0, 0, 0))
    return pl.pallas_call(
        functools.partial(_ret_sample_kernel, gammas=gammas),
        grid=(ts // sb,),
        in_specs=[pl.BlockSpec((sb, _RET_W), lambda i: (i, 0)), _resident((1, RET_DK)), _resident((1, RET_DK)), st],
        out_specs=[pl.BlockSpec((sb, RET_V_W), lambda i: (i, 0)), st],
        out_shape=[jax.ShapeDtypeStruct((ts, RET_V_W), BF16), jax.ShapeDtypeStruct(state.shape, F32)],
        compiler_params=_params(("parallel",)),
        name="ret_sample",
    )(p_ret_s, cos, sin, state)


def _l2norm(x):
    return x * lax.rsqrt(jnp.sum(x * x, axis=-1, keepdims=True) + NORM_EPS)


def _bdot(a, b):
    return _dot(a.astype(BF16), b.astype(BF16))


def _chunk_masks(c):
    ri = lax.broadcasted_iota(jnp.int32, (c, c), 0)
    ci = lax.broadcasted_iota(jnp.int32, (c, c), 1)
    eye = (ri == ci).astype(F32)
    diag16 = (ri // 16 == ci // 16).astype(F32)
    low32 = jnp.logical_and(ri // 32 == ci // 32, ri // 16 > ci // 16).astype(F32)
    low64 = (ri // 32 > ci // 32).astype(F32)
    return ri >= ci, ri > ci, eye, diag16, low32, low64


def _unit_lower_inverse(a, eye, diag16, low32, low64):
    n = -(a * diag16)
    n2 = _bdot(n, n)
    n4 = _bdot(n2, n2)
    x = eye + n + n2 + _bdot(n, n2)
    x = x + _bdot(x, n4)
    x = x + _bdot(x, _bdot(n4, n4))
    x = x - _bdot(x, _bdot(a * low32, x))
    x = x - _bdot(x, _bdot(a * low64, x))
    return x


def _gdn_gates(ab, alog, dtb):
    g = -jnp.exp(alog) * _softplus(ab + dtb)
    return g, _sigmoid(ab)


def _gdn_prep_kernel(x_ref, prev_ref, ab_ref, cw_ref, alog_ref, dtb_ref,
                     u_ref, w_ref, qg_ref, kg_ref, at_ref, eg_ref, cv_ref, xc_ref, act_ref):
    c = GDN_CHUNK
    rows = x_ref.shape[0]
    prev = prev_ref[...].astype(F32)[8:16, :]
    xc_ref[0:8, :] = jnp.where(pl.program_id(1) == 0, 0.0, prev)
    xc_ref[8:8 + rows, :] = x_ref[...].astype(F32)
    acc = xc_ref[5:5 + rows, :] * cw_ref[0:1, :]
    for i in range(1, CONV_W):
        acc = acc + xc_ref[5 + i:5 + i + rows, :] * cw_ref[i:i + 1, :]
    cv_ref[0] = xc_ref[rows + 8 - (CONV_W - 1):rows + 8, :]
    act_ref[...] = _silu(acc)

    g_all, beta_all = _gdn_gates(ab_ref[...], alog_ref[...], dtb_ref[...])
    lower, strict, eye, diag16, low32, low64 = _chunk_masks(c)
    for ck in range(rows // c):
        r = slice(ck * c, (ck + 1) * c)
        gc_all = _dot_hi(lower.astype(F32), g_all[r])
        gr_all = _columns(gc_all)
        eg_ref[ck] = jnp.exp(gc_all[c - 1:c, :])
        for h in range(GDN_HEADS):
            sl = slice(h * GDN_DK, (h + 1) * GDN_DK)
            q = _l2norm(act_ref[r, sl]) * (GDN_DK ** -0.5)
            k = _l2norm(act_ref[r, GDN_QK_W + h * GDN_DK:GDN_QK_W + (h + 1) * GDN_DK])
            v = act_ref[r, 2 * GDN_QK_W + h * GDN_DV:2 * GDN_QK_W + (h + 1) * GDN_DV]
            beta = beta_all[r, GDN_HEADS + h:GDN_HEADS + h + 1]
            gc = gc_all[:, h:h + 1]
            gr = gr_all[h:h + 1, :c]
            decay = jnp.exp(jnp.where(lower, gc - gr, -jnp.inf))
            exp_g = jnp.exp(gc)
            kb = k * beta
            kbf = k.astype(BF16)
            a = _dot_nt(kb.astype(BF16), kbf) * jnp.where(strict, decay, 0.0)
            t = _unit_lower_inverse(a, eye, diag16, low32, low64)
            u_ref[r, sl] = _bdot(t, v * beta)
            w_ref[r, sl] = _bdot(t, kb * exp_g).astype(BF16)
            attn = _dot_nt(q.astype(BF16), kbf) * decay
            at_ref[r, sl] = jnp.concatenate([attn, jnp.zeros((c, GDN_DK - c), F32)], axis=1).astype(BF16)
            qg_ref[r, sl] = (q * exp_g).astype(BF16)
            kg_ref[r, sl] = (k * jnp.exp(gc[c - 1:c, :] - gc)).astype(BF16)


def _gdn_scan_kernel(u_ref, w_ref, qg_ref, kg_ref, at_ref, eg_ref, z_ref, nw_ref, o_ref, s_ref):
    c = GDN_CHUNK

    @pl.when(pl.program_id(1) == 0)
    def _():
        s_ref[...] = jnp.zeros_like(s_ref)

    nw = nw_ref[...]
    eg = eg_ref[0]
    for h in range(GDN_HEADS):
        sl = slice(h * GDN_DK, (h + 1) * GDN_DK)
        s = s_ref[0, h]
        sb = s.astype(BF16)
        vnb = (u_ref[:, sl] - _dot(w_ref[:, sl], sb)).astype(BF16)
        o = _dot(qg_ref[:, sl], sb) + _dot(at_ref[:, h * GDN_DK:h * GDN_DK + c], vnb)
        s_ref[0, h] = s * eg[:, h:h + 1] + _dot_tn(kg_ref[:, sl], vnb)
        o_ref[:, sl] = (_rms(o) * nw * _silu(z_ref[:, sl].astype(F32))).astype(BF16)


def _gdn_prompt(p_gdn, p_ab, batch, seq, conv_w, a_log, dt_bias, gdn_norm_w):
    c = GDN_CHUNK
    rows = GDN_PREP_ROWS
    t = batch * seq
    nt = seq // rows
    alog = jnp.pad(a_log.astype(F32), (0, LANES - GDN_HEADS)).reshape(1, LANES)
    dtb = jnp.pad(dt_bias.astype(F32), (0, LANES - GDN_HEADS)).reshape(1, LANES)
    wide = lambda: pl.BlockSpec((rows, GDN_V_W), lambda b, j: (b * nt + j, 0))
    u, w, qg, kg, at, eg, conv_new = pl.pallas_call(
        _gdn_prep_kernel,
        grid=(batch, nt),
        in_specs=[pl.BlockSpec((rows, CONV_CH), lambda b, j: (b * nt + j, 0)),
                  pl.BlockSpec((16, CONV_CH), lambda b, j: (jnp.maximum((b * nt + j) * (rows // 16) - 1, 0), 0)),
                  pl.BlockSpec((rows, LANES), lambda b, j: (b * nt + j, 0)),
                  _resident((CONV_W, CONV_CH)), _resident((1, LANES)), _resident((1, LANES))],
        out_specs=[wide(), wide(), wide(), wide(), wide(),
                   pl.BlockSpec((rows // c, 1, LANES), lambda b, j: (b * nt + j, 0, 0)),
                   pl.BlockSpec((1, CONV_W - 1, CONV_CH), lambda b, j: (b, 0, 0))],
        out_shape=[jax.ShapeDtypeStruct((t, GDN_V_W), F32)] + [jax.ShapeDtypeStruct((t, GDN_V_W), BF16)] * 4
        + [jax.ShapeDtypeStruct((t // c, 1, LANES), F32),
           jax.ShapeDtypeStruct((batch, CONV_W - 1, CONV_CH), F32)],
        scratch_shapes=[pltpu.VMEM((rows + 8, CONV_CH), F32), pltpu.VMEM((rows, CONV_CH), F32)],
        compiler_params=_params(("parallel", "arbitrary")),
        name="gdn_prep",
    )(p_gdn, p_gdn, p_ab, conv_w.astype(F32), alog, dtb)
    n = seq // c
    nq = CONV_CH // GDN_V_W
    blk = lambda: pl.BlockSpec((c, GDN_V_W), lambda b, j: (b * n + j, 0))
    o, s_new = pl.pallas_call(
        _gdn_scan_kernel,
        grid=(batch, n),
        in_specs=[blk(), blk(), blk(), blk(), blk(),
                  pl.BlockSpec((1, 1, LANES), lambda b, j: (b * n + j, 0, 0)),
                  pl.BlockSpec((c, GDN_V_W), lambda b, j: (b * n + j, nq)),
                  _resident((1, GDN_DV))],
        out_specs=[blk(), pl.BlockSpec((1, GDN_HEADS, GDN_DK, GDN_DV), lambda b, j: (b, 0, 0, 0))],
        out_shape=[jax.ShapeDtypeStruct((t, GDN_V_W), BF16),
                   jax.ShapeDtypeStruct((batch, GDN_HEADS, GDN_DK, GDN_DV), F32)],
        compiler_params=_params(("parallel", "arbitrary")),
        name="gdn_scan",
    )(u, w, qg, kg, at, eg, p_gdn, gdn_norm_w.astype(F32).reshape(1, GDN_DV))
    return o, s_new, conv_new


def _gdn_sample_kernel(x_ref, ab_ref, sc_ref, cw_ref, alog_ref, dtb_ref, nw_ref, s_ref, o_ref, so_ref, sco_ref):
    nb = x_ref.shape[0]
    x = x_ref[:, :CONV_CH]
    acc = x * cw_ref[CONV_W - 1:CONV_W, :]
    for i in range(CONV_W - 1):
        acc = acc + sc_ref[i] * cw_ref[i:i + 1, :]
    for i in range(CONV_W - 2):
        sco_ref[i] = sc_ref[i + 1]
    sco_ref[CONV_W - 2] = x
    u = _silu(acc)
    g_all, beta_all = _gdn_gates(ab_ref[...], alog_ref[...], dtb_ref[...])
    eg_all = jnp.exp(g_all)
    nw = nw_ref[...]
    for h in range(GDN_HEADS):
        sl = slice(h * GDN_DK, (h + 1) * GDN_DK)
        q = _l2norm(u[:, sl]) * (GDN_DK ** -0.5)
        k = _l2norm(u[:, GDN_QK_W + h * GDN_DK:GDN_QK_W + (h + 1) * GDN_DK])
        v = u[:, 2 * GDN_QK_W + h * GDN_DV:2 * GDN_QK_W + (h + 1) * GDN_DV]
        beta = beta_all[:, GDN_HEADS + h:GDN_HEADS + h + 1]
        eg = eg_all[:, h:h + 1]
        qk_dot = jnp.sum(q * k, axis=-1, keepdims=True)
        qt, kt = _columns(q), _columns(k)
        rows = []
        for j in range(nb):
            s = s_ref[j, h]
            kcol = kt[:, j:j + 1]
            ks = jnp.sum(kcol * s, axis=0, keepdims=True)
            qs = jnp.sum(qt[:, j:j + 1] * s, axis=0, keepdims=True)
            ej = eg[j:j + 1]
            v_new = beta[j:j + 1] * (v[j:j + 1] - ej * ks)
            rows.append(ej * qs + qk_dot[j:j + 1] * v_new)
            so_ref[j, h] = s * ej + kcol * v_new
        o = jnp.concatenate(rows, axis=0)
        z = x_ref[:, CONV_CH + h * GDN_DV:CONV_CH + (h + 1) * GDN_DV]
        o_ref[:, sl] = (_rms(o) * nw * _silu(z)).astype(BF16)


def _gdn_sample(p_gdn_s, p_ab_s, state, conv_state, conv_w, a_log, dt_bias, gdn_norm_w):
    ts = p_gdn_s.shape[0]
    sb = SAMPLE_TILE
    alog = jnp.pad(a_log.astype(F32), (0, LANES - GDN_HEADS)).reshape(1, LANES)
    dtb = jnp.pad(dt_bias.astype(F32), (0, LANES - GDN_HEADS)).reshape(1, LANES)
    sc = jnp.swapaxes(conv_state.astype(F32), 0, 1)
    st = pl.BlockSpec((sb, GDN_HEADS, GDN_DK, GDN_DV), lambda i: (i, 0, 0, 0))
    scs = pl.BlockSpec((CONV_W - 1, sb, CONV_CH), lambda i: (0, i, 0))
    o, s_new, sc_new = pl.pallas_call(
        _gdn_sample_kernel,
        grid=(ts // sb,),
        in_specs=[pl.BlockSpec((sb, _GDN_W), lambda i: (i, 0)), pl.BlockSpec((sb, LANES), lambda i: (i, 0)), scs,
                  _resident((CONV_W, CONV_CH)), _resident((1, LANES)), _resident((1, LANES)),
                  _resident((1, GDN_DV)), st],
        out_specs=[pl.BlockSpec((sb, GDN_V_W), lambda i: (i, 0)), st, scs],
        out_shape=[jax.ShapeDtypeStruct((ts, GDN_V_W), BF16), jax.ShapeDtypeStruct(state.shape, F32),
                   jax.ShapeDtypeStruct(sc.shape, F32)],
        compiler_params=_params(("parallel",)),
        name="gdn_sample",
    )(p_gdn_s, p_ab_s, sc, conv_w.astype(F32), alog, dtb, gdn_norm_w.astype(F32).reshape(1, GDN_DV), state)
    return o, s_new, jnp.swapaxes(sc_new, 0, 1)


def _merge_kernel(oa_ref, ob_ref, gate_ref, x_ref, wa_ref, wb_ref, wo_ref, nw_ref, wr_ref, br_ref,
                  x1_ref, h2_ref, lg_ref):
    ya = _dot(oa_ref[...], wa_ref[...])
    yb = _dot(ob_ref[...], wb_ref[...])
    ga = gate_ref[:, :D_MODEL].astype(F32)
    gb = gate_ref[:, D_MODEL:].astype(F32)
    m = _sigmoid(ga) * ya + _sigmoid(gb) * yb
    x1 = x_ref[...] + _dot(m.astype(BF16), wo_ref[...])
    x1_ref[...] = x1
    h2 = _rms(x1) * nw_ref[...]
    h2_ref[...] = h2
    lg_ref[...] = _dot_hi(h2, wr_ref[...]) + br_ref[...]


def _merge(o_ret, o_gdn, p_gate, x, w_a, w_b, w_o, ffn_norm_w, w_router, b_router):
    t = x.shape[0]
    tm = _pick_tile(t, (384, 256, 128, 64, 8))
    wr = jnp.pad(w_router.astype(F32), ((0, 0), (0, LANES - N_EXPERTS)))
    br = jnp.pad(b_router.astype(F32), (0, LANES - N_EXPERTS)).reshape(1, LANES)
    row = lambda n: pl.BlockSpec((tm, n), lambda i: (i, 0))
    sq = (D_MODEL, D_MODEL)
    return pl.pallas_call(
        _merge_kernel,
        grid=(t // tm,),
        in_specs=[row(RET_V_W), row(GDN_V_W), row(2 * D_MODEL), row(D_MODEL), _resident(sq), _resident(sq),
                  _resident(sq), _resident((1, D_MODEL)), _resident((D_MODEL, LANES)), _resident((1, LANES))],
        out_specs=[row(D_MODEL), row(D_MODEL), row(LANES)],
        out_shape=[jax.ShapeDtypeStruct((t, D_MODEL), F32), jax.ShapeDtypeStruct((t, D_MODEL), F32),
                   jax.ShapeDtypeStruct((t, LANES), F32)],
        compiler_params=_params(("parallel",)),
        name="merge",
    )(o_ret, o_gdn, p_gate, x, w_a.astype(BF16), w_b.astype(BF16), w_o.astype(BF16),
      ffn_norm_w.astype(F32).reshape(1, D_MODEL), wr, br)


def _route(logits, n_tokens):
    rows = MOE_ROWS
    top_v, top_i = lax.top_k(logits, TOP_K)
    gate = jax.nn.softmax(top_v, axis=-1)
    n = n_tokens * TOP_K
    flat_e = top_i.reshape(n).astype(jnp.int32)
    order = jnp.argsort(flat_e).astype(jnp.int32)
    counts = jnp.bincount(flat_e, length=N_EXPERTS).astype(jnp.int32)
    start = jnp.cumsum(counts) - counts
    pcounts = (counts + rows - 1) // rows * rows
    pend = jnp.cumsum(pcounts)
    pstart = pend - pcounts
    nb = -(-n // rows) + N_EXPERTS
    block_e = jnp.minimum(jnp.searchsorted(pend, jnp.arange(nb, dtype=jnp.int32) * rows, side='right'),
                          N_EXPERTS - 1).astype(jnp.int32)
    nb_used = (pend[-1] // rows).astype(jnp.int32).reshape(1)
    r = jnp.arange(nb * rows, dtype=jnp.int32)
    e_r = jnp.repeat(block_e, rows)
    within = r - pstart[e_r]
    valid = (within < counts[e_r]) & (r < pend[-1])
    flat = order[jnp.clip(start[e_r] + within, 0, n - 1)]
    spare = n + (r % (2 * rows))
    src = jnp.where(valid, flat // TOP_K, 0).reshape(nb, rows)
    dst = jnp.where(valid, flat, spare).reshape(nb, rows)
    nxt = jnp.concatenate([src[1:], src[-1:]], axis=0)
    slab = jnp.concatenate([src, dst, nxt], axis=1)
    return gate, block_e, nb_used, slab


def _expert_kernel(be_ref, nbu_ref, slab_ref, h_ref, wgu_ref, bgu_ref, wd_ref, bd_ref, y_ref,
                   idx_ref, xbuf_ref, ybuf_ref, wgu_bf_ref, wd_bf_ref, isem, gsem, ssem):
    rows = MOE_ROWS
    i = pl.program_id(0)
    nbu = nbu_ref[0]
    slot = i % 2
    active = i < nbu

    def slab_copy(blk, sl):
        return pltpu.make_async_copy(slab_ref.at[blk], idx_ref.at[sl], isem.at[sl])

    def gather_all(sl, base):
        def body(r, carry):
            tok = idx_ref[slot, base + r]
            pltpu.make_async_copy(h_ref.at[pl.ds(tok, 1)], xbuf_ref.at[sl, pl.ds(r, 1)], gsem.at[sl]).start()
            return carry
        lax.fori_loop(0, rows, body, 0, unroll=8)

    def gather_wait(sl):
        pltpu.make_async_copy(h_ref.at[pl.ds(0, rows)], xbuf_ref.at[sl], gsem.at[sl]).wait()

    def scatter_wait(sl):
        pltpu.make_async_copy(ybuf_ref.at[sl], y_ref.at[pl.ds(0, rows)], ssem.at[sl]).wait()

    @pl.when(i == 0)
    def _():
        slab_copy(0, 0).start()
        n_real = y_ref.shape[0] - 2 * rows
        ybuf_ref[...] = jnp.zeros_like(ybuf_ref)
        for sl in range(2):
            pltpu.make_async_copy(ybuf_ref.at[sl], y_ref.at[pl.ds(n_real + sl * rows, rows)], ssem.at[sl]).start()
        for sl in range(2):
            scatter_wait(sl)

    @pl.when(active)
    def _():
        slab_copy(i, slot).wait()

        @pl.when(i + 1 < nbu)
        def _():
            slab_copy(i + 1, 1 - slot).start()

        @pl.when(i == 0)
        def _():
            gather_all(0, 0)

        @pl.when(i + 1 < nbu)
        def _():
            gather_all(1 - slot, 2 * rows)

        changed = jnp.logical_or(i == 0, be_ref[i] != be_ref[jnp.maximum(i - 1, 0)])

        @pl.when(changed)
        def _():
            wgu_bf_ref[...] = wgu_ref[0].astype(BF16)
            wd_bf_ref[...] = wd_ref[0].astype(BF16)

        gather_wait(slot)

        @pl.when(i >= 2)
        def _():
            scatter_wait(slot)

        xb = xbuf_ref[slot].astype(BF16)
        hb = _dot(xb, wgu_bf_ref[...]) + bgu_ref[0]
        glu = jnp.minimum(hb[:, :D_FF], SWIGLU_LIMIT)
        lin = jnp.clip(hb[:, D_FF:], -SWIGLU_LIMIT, SWIGLU_LIMIT)
        act = (glu * _sigmoid(SWIGLU_ALPHA * glu) * (lin + 1.0)).astype(BF16)
        ybuf_ref[slot] = _dot(act, wd_bf_ref[...]) + bd_ref[0]

        def body(r, carry):
            dst = idx_ref[slot, rows + r]
            pltpu.make_async_copy(ybuf_ref.at[slot, pl.ds(r, 1)], y_ref.at[pl.ds(dst, 1)], ssem.at[slot]).start()
            return carry
        lax.fori_loop(0, rows, body, 0, unroll=8)

        @pl.when(i == nbu - 1)
        def _():
            scatter_wait(slot)

            @pl.when(i >= 1)
            def _():
                scatter_wait(1 - slot)


def _experts(h2, block_e, nb_used, slab, w_gate_up, b_gate_up, w_down, b_down):
    t = h2.shape[0]
    rows = MOE_ROWS
    nb = slab.shape[0]
    grid_spec = pltpu.PrefetchScalarGridSpec(
        num_scalar_prefetch=2,
        grid=(nb,),
        in_specs=[pl.BlockSpec(memory_space=pl.ANY),
                  pl.BlockSpec(memory_space=pl.ANY),
                  pl.BlockSpec((1, D_MODEL, 2 * D_FF), lambda i, be, nbu: (be[i], 0, 0)),
                  pl.BlockSpec((1, 1, 2 * D_FF), lambda i, be, nbu: (be[i], 0, 0)),
                  pl.BlockSpec((1, D_FF, D_MODEL), lambda i, be, nbu: (be[i], 0, 0)),
                  pl.BlockSpec((1, 1, D_MODEL), lambda i, be, nbu: (be[i], 0, 0))],
        out_specs=pl.BlockSpec(memory_space=pl.ANY),
        scratch_shapes=[pltpu.SMEM((2, 3 * rows), jnp.int32),
                        pltpu.VMEM((2, rows, D_MODEL), F32),
                        pltpu.VMEM((2, rows, D_MODEL), F32),
                        pltpu.VMEM((D_MODEL, 2 * D_FF), BF16),
                        pltpu.VMEM((D_FF, D_MODEL), BF16),
                        pltpu.SemaphoreType.DMA((2,)),
                        pltpu.SemaphoreType.DMA((2,)),
                        pltpu.SemaphoreType.DMA((2,))])
    return pl.pallas_call(
        _expert_kernel,
        grid_spec=grid_spec,
        out_shape=jax.ShapeDtypeStruct((t * TOP_K + 2 * rows, D_MODEL), F32),
        compiler_params=_params(("arbitrary",)),
        name="experts",
    )(block_e, nb_used, slab, h2, w_gate_up, b_gate_up.reshape(N_EXPERTS, 1, 2 * D_FF), w_down,
      b_down.reshape(N_EXPERTS, 1, D_MODEL))


def _combine_kernel(y_ref, w_ref, x1_ref, nw_ref, o_ref, *, final):
    acc = x1_ref[...]
    for k in range(TOP_K):
        acc = acc + w_ref[:, k:k + 1] * y_ref[:, k * D_MODEL:(k + 1) * D_MODEL]
    o_ref[...] = _rms(acc) * nw_ref[...] if final else acc


def _combine(y, gate, x1, norm_w, final):
    t = x1.shape[0]
    tm = _pick_tile(t, (384, 256, 128, 64, 8))
    y2 = y.reshape(y.shape[0] // TOP_K, TOP_K * D_MODEL)
    return pl.pallas_call(
        functools.partial(_combine_kernel, final=final),
        grid=(t // tm,),
        in_specs=[pl.BlockSpec((tm, TOP_K * D_MODEL), lambda i: (i, 0)),
                  pl.BlockSpec((tm, TOP_K), lambda i: (i, 0)),
                  pl.BlockSpec((tm, D_MODEL), lambda i: (i, 0)), _resident((1, D_MODEL))],
        out_specs=pl.BlockSpec((tm, D_MODEL), lambda i: (i, 0)),
        out_shape=jax.ShapeDtypeStruct((t, D_MODEL), F32),
        compiler_params=_params(("parallel",)),
        name="combine",
    )(y2, gate, x1, norm_w.astype(F32).reshape(1, D_MODEL))


def kernel(x_prompt, x_sample, state_ret, state_gdn, state_conv, attn_norm_w, w_in, conv_w, a_log, dt_bias, gdn_norm_w, w_branch_a, w_branch_b, w_out, ffn_norm_w, w_router, b_router, w_gate_up, b_gate_up, w_down, b_down, final_norm_w):
    bp, lp, d = x_prompt.shape
    bs, ls, _ = x_sample.shape
    assert ls == 1 and d == D_MODEL and lp % RET_CHUNK == 0 and bs % SAMPLE_TILE == 0
    depth = w_in.shape[0]
    tp = bp * lp
    x = jnp.concatenate([x_prompt.reshape(tp, d), x_sample.reshape(bs, d)], axis=0).astype(F32)
    t = tp + bs
    rp, gp, cp, rs, gs, cs = [], [], [], [], [], []
    for l in range(depth):
        p_ret, p_gdn, p_gate, p_ab = _inproj(x, attn_norm_w[l], w_in[l])
        o_ret_p, s_ret_p = _ret_prompt(p_ret, bp, lp)
        o_ret_s, s_ret_s = _ret_sample(p_ret[tp:].astype(F32), state_ret[l].astype(F32))
        o_gdn_p, s_gdn_p, conv_p = _gdn_prompt(p_gdn, p_ab, bp, lp, conv_w[l], a_log[l], dt_bias[l], gdn_norm_w[l])
        o_gdn_s, s_gdn_s, conv_s = _gdn_sample(p_gdn[tp:].astype(F32), p_ab[tp:], state_gdn[l].astype(F32),
                                               state_conv[l], conv_w[l], a_log[l], dt_bias[l], gdn_norm_w[l])
        o_ret = jnp.concatenate([o_ret_p, o_ret_s], axis=0)
        o_gdn = jnp.concatenate([o_gdn_p, o_gdn_s], axis=0)
        x1, h2, logits = _merge(o_ret, o_gdn, p_gate, x, w_branch_a[l], w_branch_b[l], w_out[l], ffn_norm_w[l],
                                w_router[l], b_router[l])
        gate, block_e, nb_used, slab = _route(logits[:, :N_EXPERTS], t)
        y = _experts(h2, block_e, nb_used, slab, w_gate_up[l], b_gate_up[l], w_down[l], b_down[l])
        last = l == depth - 1
        x = _combine(y, gate, x1, final_norm_w if last else jnp.ones((d,), F32), last)
        rp.append(s_ret_p); gp.append(s_gdn_p); cp.append(conv_p)
        rs.append(s_ret_s); gs.append(s_gdn_s); cs.append(conv_s)
    y_prompt = x[:tp].reshape(bp, lp, d).astype(x_prompt.dtype)
    y_sample = x[tp:].reshape(bs, ls, d).astype(x_sample.dtype)
    return (y_prompt, y_sample,
            jnp.stack(rp).astype(state_ret.dtype), jnp.stack(gp).astype(state_gdn.dtype),
            jnp.stack(cp).astype(state_conv.dtype),
            jnp.stack(rs).astype(state_ret.dtype), jnp.stack(gs).astype(state_gdn.dtype),
            jnp.stack(cs).astype(state_conv.dtype))
```

```python
import functools
import math

import numpy as np
import jax
import jax.numpy as jnp
from jax import lax
from jax.experimental import pallas as pl
from jax.experimental.pallas import tpu as pltpu

F32 = jnp.float32
BF16 = jnp.bfloat16
HIGHEST = lax.Precision.HIGHEST

D_MODEL = 1024
PAST_LEN = 16384
RET_HEADS, RET_DK, RET_DV = 4, 128, 256
RET_QK_W, RET_V_W = RET_HEADS * RET_DK, RET_HEADS * RET_DV
RET_CHUNK = 128
ROPE_BASE = 10000.0
GDN_HEADS, GDN_DK, GDN_DV = 8, 128, 128
GDN_QK_W, GDN_V_W = GDN_HEADS * GDN_DK, GDN_HEADS * GDN_DV
GDN_CHUNK = 64
CONV_W = 4
CONV_CH = 2 * GDN_QK_W + GDN_V_W
N_EXPERTS = 32
TOP_K = 4
D_FF = D_MODEL
SWIGLU_LIMIT = 7.0
SWIGLU_ALPHA = 1.702
NORM_EPS = 1e-6

_RET_W = 2 * RET_QK_W + 2 * RET_V_W
_GDN_W = CONV_CH + GDN_V_W
_AB_OFF = _RET_W + _GDN_W
_GATE_OFF = _AB_OFF + 2 * GDN_HEADS

LANES = 128
VMEM_LIMIT = 56 * 1024 * 1024
MOE_ROWS = 256
SAMPLE_TILE = 8
GDN_PREP_ROWS = 128
GDN_SCAN_ROWS = 128


def _pick_tile(n, candidates):
    for c in candidates:
        if n % c == 0:
            return c
    raise ValueError(f"no tile in {candidates} divides {n}")


def _params(sem, vmem=VMEM_LIMIT):
    return pltpu.CompilerParams(dimension_semantics=sem, vmem_limit_bytes=vmem)


def _resident(shape):
    nd = len(shape)
    return pl.BlockSpec(shape, lambda *_: (0,) * nd, pipeline_mode=pl.Buffered(1))


def _silu(x):
    return x * (1.0 / (1.0 + jnp.exp(-x)))


def _sigmoid(x):
    return 1.0 / (1.0 + jnp.exp(-x))


def _softplus(x):
    return jnp.maximum(x, 0.0) + jnp.log1p(jnp.exp(-jnp.abs(x)))


def _rms(x):
    return x * lax.rsqrt(jnp.mean(x * x, axis=-1, keepdims=True) + NORM_EPS)


def _dot(a, b):
    return jnp.dot(a, b, preferred_element_type=F32)


def _dot_nt(a, b):
    return lax.dot_general(a, b, (((1,), (1,)), ((), ())), preferred_element_type=F32)


def _dot_tn(a, b):
    return lax.dot_general(a, b, (((0,), (0,)), ((), ())), preferred_element_type=F32)


def _dot_hi(a, b):
    return jnp.dot(a, b, preferred_element_type=F32, precision=HIGHEST)


def _inproj_kernel(x_ref, nw_ref, w_ref, oret_ref, ogdn_ref, ogate_ref, oab_ref):
    h = (_rms(x_ref[...]) * nw_ref[...]).astype(BF16)
    oret_ref[...] = _dot(h, w_ref[:, :_RET_W]).astype(BF16)
    ogdn_ref[...] = _dot(h, w_ref[:, _RET_W:_AB_OFF]).astype(BF16)
    tail = _dot(h, w_ref[:, _AB_OFF:])
    oab_ref[...] = tail[:, :LANES]
    ogate_ref[...] = tail[:, _GATE_OFF - _AB_OFF:_GATE_OFF - _AB_OFF + 2 * D_MODEL].astype(BF16)


def _inproj(x, norm_w, w_in):
    t = x.shape[0]
    tm = _pick_tile(t, (384, 256, 128, 64, 8))
    width = w_in.shape[1]
    wb = jnp.pad(w_in.astype(BF16), ((0, 0), (0, -width % LANES)))
    row = lambda n: pl.BlockSpec((tm, n), lambda i: (i, 0))
    return pl.pallas_call(
        _inproj_kernel,
        grid=(t // tm,),
        in_specs=[row(D_MODEL), _resident((1, D_MODEL)), _resident(wb.shape)],
        out_specs=[row(_RET_W), row(_GDN_W), row(2 * D_MODEL), row(LANES)],
        out_shape=[jax.ShapeDtypeStruct((t, _RET_W), BF16), jax.ShapeDtypeStruct((t, _GDN_W), BF16),
                   jax.ShapeDtypeStruct((t, 2 * D_MODEL), BF16), jax.ShapeDtypeStruct((t, LANES), F32)],
        compiler_params=_params(("parallel",)),
        name="inproj",
    )(x, norm_w.reshape(1, D_MODEL), wb)


def _ret_log_gamma():
    return np.log1p(-np.exp2(-5.0 - np.arange(RET_HEADS, dtype=np.float64)))


def _rope_tables(pos):
    half = RET_DK // 2
    inv = 1.0 / (ROPE_BASE ** (jnp.arange(half, dtype=F32) / half))
    ang = pos.astype(F32)[:, None] * inv[None, :]
    cos, sin = jnp.cos(ang), jnp.sin(ang)
    return jnp.concatenate([cos, cos], axis=-1), jnp.concatenate([-sin, sin], axis=-1)


def _rotary(x, cos, sin):
    return x * cos + pltpu.roll(x, RET_DK // 2, 1) * sin


def _ret_prompt_kernel(q_ref, k_ref, v_ref, g_ref, cos_ref, sin_ref, dmask_ref, qdec_ref, kdec_ref,
                       o_ref, s_ref, *, gammas):
    @pl.when(pl.program_id(1) == 0)
    def _():
        s_ref[...] = jnp.zeros_like(s_ref)

    cos, sin = cos_ref[...], sin_ref[...]
    heads = range(RET_HEADS)
    qk = [slice(h * RET_DK, (h + 1) * RET_DK) for h in heads]
    vv = [slice(h * RET_DV, (h + 1) * RET_DV) for h in heads]
    q = [_rotary(q_ref[:, qk[h]].astype(F32), cos, sin) for h in heads]
    k = [_rotary(k_ref[:, qk[h]].astype(F32), cos, sin) * (RET_DK ** -0.5) for h in heads]
    v = [v_ref[:, vv[h]] for h in heads]
    s = [s_ref[0, h] for h in heads]
    qb = [x.astype(BF16) for x in q]
    inner = [_dot_nt(qb[h], k[h].astype(BF16)) * dmask_ref[h] for h in heads]
    cross = [_dot((q[h] * qdec_ref[h]).astype(BF16), s[h].astype(BF16)) for h in heads]
    upd = [_dot_tn((k[h] * kdec_ref[h]).astype(BF16), v[h]) for h in heads]
    o = [_dot(inner[h].astype(BF16), v[h]) + cross[h] for h in heads]
    for h in heads:
        s_ref[0, h] = s[h] * gammas[h] + upd[h]
        o_ref[:, vv[h]] = (_rms(o[h]) * _silu(g_ref[:, vv[h]].astype(F32))).astype(BF16)


def _ret_prompt(p_ret, batch, seq):
    c = RET_CHUNK
    n = seq // c
    lg = _ret_log_gamma()
    idx = np.arange(c, dtype=np.float64)
    diff = idx[:, None] - idx[None, :]
    dmask = np.where(diff >= 0, np.exp(np.maximum(diff, 0.0)[None] * lg[:, None, None]), 0.0)
    qdec = np.broadcast_to(np.exp((idx + 1.0)[None, :] * lg[:, None])[:, :, None], (RET_HEADS, c, RET_DK))
    kdec = np.broadcast_to(np.exp((c - 1.0 - idx)[None, :] * lg[:, None])[:, :, None], (RET_HEADS, c, RET_DK))
    gammas = tuple(float(g) for g in np.exp(c * lg))
    cos, sin = _rope_tables(jnp.arange(seq, dtype=jnp.int32))
    nq = RET_QK_W // RET_QK_W
    del nq
    tab = lambda: _resident((RET_HEADS, c, RET_DK))
    return pl.pallas_call(
        functools.partial(_ret_prompt_kernel, gammas=gammas),
        grid=(batch, n),
        in_specs=[pl.BlockSpec((c, RET_QK_W), lambda b, j: (b * n + j, 0)),
                  pl.BlockSpec((c, RET_QK_W), lambda b, j: (b * n + j, 1)),
                  pl.BlockSpec((c, RET_V_W), lambda b, j: (b * n + j, 1)),
                  pl.BlockSpec((c, RET_V_W), lambda b, j: (b * n + j, 2)),
                  pl.BlockSpec((c, RET_DK), lambda b, j: (j, 0)),
                  pl.BlockSpec((c, RET_DK), lambda b, j: (j, 0)),
                  tab(), tab(), tab()],
        out_specs=[pl.BlockSpec((c, RET_V_W), lambda b, j: (b * n + j, 0)),
                   pl.BlockSpec((1, RET_HEADS, RET_DK, RET_DV), lambda b, j: (b, 0, 0, 0))],
        out_shape=[jax.ShapeDtypeStruct((batch * seq, RET_V_W), BF16),
                   jax.ShapeDtypeStruct((batch, RET_HEADS, RET_DK, RET_DV), F32)],
        compiler_params=_params(("parallel", "arbitrary")),
        name="ret_prompt",
    )(p_ret, p_ret, p_ret, p_ret, cos, sin, jnp.asarray(dmask, F32), jnp.asarray(qdec, F32),
      jnp.asarray(kdec, F32))


def _columns(x):
    n = x.shape[0]
    if n < LANES:
        x = jnp.concatenate([x, jnp.zeros((LANES - n, x.shape[1]), x.dtype)], axis=0)
    return x.T


def _ret_sample_kernel(p_ref, cos_ref, sin_ref, s_ref, o_ref, so_ref, *, gammas):
    cos, sin = cos_ref[...], sin_ref[...]
    nb = p_ref.shape[0]
    for h in range(RET_HEADS):
        qk = slice(h * RET_DK, (h + 1) * RET_DK)
        q = _rotary(p_ref[:, qk], cos, sin)
        k = _rotary(p_ref[:, RET_QK_W + h * RET_DK:RET_QK_W + (h + 1) * RET_DK], cos, sin) * (RET_DK ** -0.5)
        v = p_ref[:, 2 * RET_QK_W + h * RET_DV:2 * RET_QK_W + (h + 1) * RET_DV]
        g = p_ref[:, 2 * RET_QK_W + RET_V_W + h * RET_DV:2 * RET_QK_W + RET_V_W + (h + 1) * RET_DV]
        qk_dot = jnp.sum(q * k, axis=-1, keepdims=True)
        qt, kt = _columns(q), _columns(k)
        rows = []
        for j in range(nb):
            s = s_ref[j, h]
            qs = jnp.sum(qt[:, j:j + 1] * s, axis=0, keepdims=True)
            rows.append(qk_dot[j:j + 1] * v[j:j + 1] + gammas[h] * qs)
            so_ref[j, h] = s * gammas[h] + kt[:, j:j + 1] * v[j:j + 1]
        o = jnp.concatenate(rows, axis=0)
        o_ref[:, h * RET_DV:(h + 1) * RET_DV] = (_rms(o) * _silu(g)).astype(BF16)


def _ret_sample(p_ret_s, state):
    ts = p_ret_s.shape[0]
    sb = SAMPLE_TILE
    gammas = tuple(float(g) for g in np.exp(_ret_log_gamma()))
    cos, sin = _rope_tables(jnp.full((1,), PAST_LEN, jnp.int32))
    st = pl.BlockSpec((sb, RET_HEADS, RET_DK, RET_DV), lambda i: (i, 0, 0, 0))
    return pl.pallas_call(
        functools.partial(_ret_sample_kernel, gammas=gammas),
        grid=(ts // sb,),
        in_specs=[pl.BlockSpec((sb, _RET_W), lambda i: (i, 0)), _resident((1, RET_DK)), _resident((1, RET_DK)), st],
        out_specs=[pl.BlockSpec((sb, RET_V_W), lambda i: (i, 0)), st],
        out_shape=[jax.ShapeDtypeStruct((ts, RET_V_W), BF16), jax.ShapeDtypeStruct(state.shape, F32)],
        compiler_params=_params(("parallel",)),
        name="ret_sample",
    )(p_ret_s, cos, sin, state)


def _l2norm(x):
    return x * lax.rsqrt(jnp.sum(x * x, axis=-1, keepdims=True) + NORM_EPS)


def _bdot(a, b):
    return _dot(a.astype(BF16), b.astype(BF16))


def _chunk_masks(c):
    ri = lax.broadcasted_iota(jnp.int32, (c, c), 0)
    ci = lax.broadcasted_iota(jnp.int32, (c, c), 1)
    eye = (ri == ci).astype(F32)
    diag16 = (ri // 16 == ci // 16).astype(F32)
    low32 = jnp.logical_and(ri // 32 == ci // 32, ri // 16 > ci // 16).astype(F32)
    low64 = (ri // 32 > ci // 32).astype(F32)
    return ri >= ci, ri > ci, eye, diag16, low32, low64


def _unit_lower_inverse(a, eye, diag16, low32, low64):
    many = lambda f, *ls: [f(*args) for args in zip(*ls)]
    n = [-(x * diag16) for x in a]
    n2 = many(_bdot, n, n)
    n3 = many(_bdot, n, n2)
    n4 = many(_bdot, n2, n2)
    n8 = many(_bdot, n4, n4)
    x = [eye + p + q + r for p, q, r in zip(n, n2, n3)]
    x = many(lambda u, v: u + v, x, many(_bdot, x, n4))
    x = many(lambda u, v: u + v, x, many(_bdot, x, n8))
    for mask in (low32, low64):
        r = many(_bdot, [y * mask for y in a], x)
        x = many(lambda u, v: u - v, x, many(_bdot, x, r))
    return x


def _gdn_gates(ab, alog, dtb):
    g = -jnp.exp(alog) * _softplus(ab + dtb)
    return g, _sigmoid(ab)


def _gdn_prep_kernel(x_ref, prev_ref, ab_ref, cw_ref, alog_ref, dtb_ref,
                     u_ref, w_ref, qg_ref, kg_ref, at_ref, eg_ref, cv_ref, xc_ref, act_ref):
    c = GDN_CHUNK
    rows = x_ref.shape[0]
    prev = prev_ref[...].astype(F32)[8:16, :]
    xc_ref[0:8, :] = jnp.where(pl.program_id(1) == 0, 0.0, prev)
    xc_ref[8:8 + rows, :] = x_ref[...].astype(F32)
    acc = xc_ref[5:5 + rows, :] * cw_ref[0:1, :]
    for i in range(1, CONV_W):
        acc = acc + xc_ref[5 + i:5 + i + rows, :] * cw_ref[i:i + 1, :]
    cv_ref[0] = xc_ref[rows + 8 - (CONV_W - 1):rows + 8, :]
    act_ref[...] = _silu(acc)

    g_all, beta_all = _gdn_gates(ab_ref[...], alog_ref[...], dtb_ref[...])
    lower, strict, eye, diag16, low32, low64 = _chunk_masks(c)
    gc_all, gr_all = [], []
    for ck in range(rows // c):
        gc_ck = _dot_hi(lower.astype(F32), g_all[ck * c:(ck + 1) * c])
        gc_all.append(gc_ck)
        gr_all.append(_columns(gc_ck))
        eg_ref[ck] = jnp.exp(gc_ck[c - 1:c, :])
    chains = [(ck, h) for ck in range(rows // c) for h in range(GDN_HEADS)]
    rs = [slice(ck * c, (ck + 1) * c) for ck, _ in chains]
    sl = [slice(h * GDN_DK, (h + 1) * GDN_DK) for _, h in chains]
    nc = range(len(chains))
    q = [_l2norm(act_ref[rs[i], sl[i]]) * (GDN_DK ** -0.5) for i in nc]
    k = [_l2norm(act_ref[rs[i], GDN_QK_W + sl[i].start:GDN_QK_W + sl[i].stop]) for i in nc]
    v = [act_ref[rs[i], 2 * GDN_QK_W + sl[i].start:2 * GDN_QK_W + sl[i].stop] for i in nc]
    beta = [beta_all[rs[i], GDN_HEADS + h:GDN_HEADS + h + 1] for i, (_, h) in enumerate(chains)]
    gc = [gc_all[ck][:, h:h + 1] for ck, h in chains]
    gr = [gr_all[ck][h:h + 1, :c] for ck, h in chains]
    decay = [jnp.exp(jnp.where(lower, gc[i] - gr[i], -jnp.inf)) for i in nc]
    exp_g = [jnp.exp(x) for x in gc]
    kb = [k[i] * beta[i] for i in nc]
    kbf = [x.astype(BF16) for x in k]
    a = [_dot_nt(kb[i].astype(BF16), kbf[i]) * jnp.where(strict, decay[i], 0.0) for i in nc]
    attn = [_dot_nt(q[i].astype(BF16), kbf[i]) * decay[i] for i in nc]
    t = _unit_lower_inverse(a, eye, diag16, low32, low64)
    uu = [_bdot(t[i], v[i] * beta[i]) for i in nc]
    ww = [_bdot(t[i], kb[i] * exp_g[i]) for i in nc]
    for i in nc:
        u_ref[rs[i], sl[i]] = uu[i]
        w_ref[rs[i], sl[i]] = ww[i].astype(BF16)
        at_ref[rs[i], sl[i]] = jnp.concatenate([attn[i], jnp.zeros((c, GDN_DK - c), F32)], axis=1).astype(BF16)
        qg_ref[rs[i], sl[i]] = (q[i] * exp_g[i]).astype(BF16)
        kg_ref[rs[i], sl[i]] = (k[i] * jnp.exp(gc[i][c - 1:c, :] - gc[i])).astype(BF16)


def _gdn_scan_kernel(u_ref, w_ref, qg_ref, kg_ref, at_ref, eg_ref, z_ref, nw_ref, o_ref, s_ref):
    c = GDN_CHUNK

    @pl.when(pl.program_id(1) == 0)
    def _():
        s_ref[...] = jnp.zeros_like(s_ref)

    nw = nw_ref[...]
    heads = range(GDN_HEADS)
    sl = [slice(h * GDN_DK, (h + 1) * GDN_DK) for h in heads]
    s = [s_ref[0, h] for h in heads]
    for ck in range(u_ref.shape[0] // c):
        r = slice(ck * c, (ck + 1) * c)
        eg = eg_ref[ck]
        sb = [x.astype(BF16) for x in s]
        ws = [_dot(w_ref[r, sl[h]], sb[h]) for h in heads]
        qs = [_dot(qg_ref[r, sl[h]], sb[h]) for h in heads]
        vnb = [(u_ref[r, sl[h]] - ws[h]).astype(BF16) for h in heads]
        o = [qs[h] + _dot(at_ref[r, h * GDN_DK:h * GDN_DK + c], vnb[h]) for h in heads]
        s = [s[h] * eg[:, h:h + 1] + _dot_tn(kg_ref[r, sl[h]], vnb[h]) for h in heads]
        for h in heads:
            o_ref[r, sl[h]] = (_rms(o[h]) * nw * _silu(z_ref[r, sl[h]].astype(F32))).astype(BF16)
    for h in heads:
        s_ref[0, h] = s[h]


def _gdn_prompt(p_gdn, p_ab, batch, seq, conv_w, a_log, dt_bias, gdn_norm_w):
    c = GDN_CHUNK
    rows = GDN_PREP_ROWS
    t = batch * seq
    nt = seq // rows
    alog = jnp.pad(a_log.astype(F32), (0, LANES - GDN_HEADS)).reshape(1, LANES)
    dtb = jnp.pad(dt_bias.astype(F32), (0, LANES - GDN_HEADS)).reshape(1, LANES)
    wide = lambda: pl.BlockSpec((rows, GDN_V_W), lambda b, j: (b * nt + j, 0))
    u, w, qg, kg, at, eg, conv_new = pl.pallas_call(
        _gdn_prep_kernel,
        grid=(batch, nt),
        in_specs=[pl.BlockSpec((rows, CONV_CH), lambda b, j: (b * nt + j, 0)),
                  pl.BlockSpec((16, CONV_CH), lambda b, j: (jnp.maximum((b * nt + j) * (rows // 16) - 1, 0), 0)),
                  pl.BlockSpec((rows, LANES), lambda b, j: (b * nt + j, 0)),
                  _resident((CONV_W, CONV_CH)), _resident((1, LANES)), _resident((1, LANES))],
        out_specs=[wide(), wide(), wide(), wide(), wide(),
                   pl.BlockSpec((rows // c, 1, LANES), lambda b, j: (b * nt + j, 0, 0)),
                   pl.BlockSpec((1, CONV_W - 1, CONV_CH), lambda b, j: (b, 0, 0))],
        out_shape=[jax.ShapeDtypeStruct((t, GDN_V_W), F32)] + [jax.ShapeDtypeStruct((t, GDN_V_W), BF16)] * 4
        + [jax.ShapeDtypeStruct((t // c, 1, LANES), F32),
           jax.ShapeDtypeStruct((batch, CONV_W - 1, CONV_CH), F32)],
        scratch_shapes=[pltpu.VMEM((rows + 8, CONV_CH), F32), pltpu.VMEM((rows, CONV_CH), F32)],
        compiler_params=_params(("parallel", "arbitrary")),
        name="gdn_prep",
    )(p_gdn, p_gdn, p_ab, conv_w.astype(F32), alog, dtb)
    srows = GDN_SCAN_ROWS
    n = seq // srows
    nq = CONV_CH // GDN_V_W
    blk = lambda: pl.BlockSpec((srows, GDN_V_W), lambda b, j: (b * n + j, 0))
    o, s_new = pl.pallas_call(
        _gdn_scan_kernel,
        grid=(batch, n),
        in_specs=[blk(), blk(), blk(), blk(), blk(),
                  pl.BlockSpec((srows // c, 1, LANES), lambda b, j: (b * n + j, 0, 0)),
                  pl.BlockSpec((srows, GDN_V_W), lambda b, j: (b * n + j, nq)),
                  _resident((1, GDN_DV))],
        out_specs=[blk(), pl.BlockSpec((1, GDN_HEADS, GDN_DK, GDN_DV), lambda b, j: (b, 0, 0, 0))],
        out_shape=[jax.ShapeDtypeStruct((t, GDN_V_W), BF16),
                   jax.ShapeDtypeStruct((batch, GDN_HEADS, GDN_DK, GDN_DV), F32)],
        compiler_params=_params(("parallel", "arbitrary")),
        name="gdn_scan",
    )(u, w, qg, kg, at, eg, p_gdn, gdn_norm_w.astype(F32).reshape(1, GDN_DV))
    return o, s_new, conv_new


def _gdn_sample_kernel(x_ref, ab_ref, sc_ref, cw_ref, alog_ref, dtb_ref, nw_ref, s_ref, o_ref, so_ref, sco_ref):
    nb = x_ref.shape[0]
    x = x_ref[:, :CONV_CH]
    acc = x * cw_ref[CONV_W - 1:CONV_W, :]
    for i in range(CONV_W - 1):
        acc = acc + sc_ref[i] * cw_ref[i:i + 1, :]
    for i in range(CONV_W - 2):
        sco_ref[i] = sc_ref[i + 1]
    sco_ref[CONV_W - 2] = x
    u = _silu(acc)
    g_all, beta_all = _gdn_gates(ab_ref[...], alog_ref[...], dtb_ref[...])
    eg_all = jnp.exp(g_all)
    nw = nw_ref[...]
    for h in range(GDN_HEADS):
        sl = slice(h * GDN_DK, (h + 1) * GDN_DK)
        q = _l2norm(u[:, sl]) * (GDN_DK ** -0.5)
        k = _l2norm(u[:, GDN_QK_W + h * GDN_DK:GDN_QK_W + (h + 1) * GDN_DK])
        v = u[:, 2 * GDN_QK_W + h * GDN_DV:2 * GDN_QK_W + (h + 1) * GDN_DV]
        beta = beta_all[:, GDN_HEADS + h:GDN_HEADS + h + 1]
        eg = eg_all[:, h:h + 1]
        qk_dot = jnp.sum(q * k, axis=-1, keepdims=True)
        qt, kt = _columns(q), _columns(k)
        rows = []
        for j in range(nb):
            s = s_ref[j, h]
            kcol = kt[:, j:j + 1]
            ks = jnp.sum(kcol * s, axis=0, keepdims=True)
            qs = jnp.sum(qt[:, j:j + 1] * s, axis=0, keepdims=True)
            ej = eg[j:j + 1]
            v_new = beta[j:j + 1] * (v[j:j + 1] - ej * ks)
            rows.append(ej * qs + qk_dot[j:j + 1] * v_new)
            so_ref[j, h] = s * ej + kcol * v_new
        o = jnp.concatenate(rows, axis=0)
        z = x_ref[:, CONV_CH + h * GDN_DV:CONV_CH + (h + 1) * GDN_DV]
        o_ref[:, sl] = (_rms(o) * nw * _silu(z)).astype(BF16)


def _gdn_sample(p_gdn_s, p_ab_s, state, conv_state, conv_w, a_log, dt_bias, gdn_norm_w):
    ts = p_gdn_s.shape[0]
    sb = SAMPLE_TILE
    alog = jnp.pad(a_log.astype(F32), (0, LANES - GDN_HEADS)).reshape(1, LANES)
    dtb = jnp.pad(dt_bias.astype(F32), (0, LANES - GDN_HEADS)).reshape(1, LANES)
    sc = jnp.swapaxes(conv_state.astype(F32), 0, 1)
    st = pl.BlockSpec((sb, GDN_HEADS, GDN_DK, GDN_DV), lambda i: (i, 0, 0, 0))
    scs = pl.BlockSpec((CONV_W - 1, sb, CONV_CH), lambda i: (0, i, 0))
    o, s_new, sc_new = pl.pallas_call(
        _gdn_sample_kernel,
        grid=(ts // sb,),
        in_specs=[pl.BlockSpec((sb, _GDN_W), lambda i: (i, 0)), pl.BlockSpec((sb, LANES), lambda i: (i, 0)), scs,
                  _resident((CONV_W, CONV_CH)), _resident((1, LANES)), _resident((1, LANES)),
                  _resident((1, GDN_DV)), st],
        out_specs=[pl.BlockSpec((sb, GDN_V_W), lambda i: (i, 0)), st, scs],
        out_shape=[jax.ShapeDtypeStruct((ts, GDN_V_W), BF16), jax.ShapeDtypeStruct(state.shape, F32),
                   jax.ShapeDtypeStruct(sc.shape, F32)],
        compiler_params=_params(("parallel",)),
        name="gdn_sample",
    )(p_gdn_s, p_ab_s, sc, conv_w.astype(F32), alog, dtb, gdn_norm_w.astype(F32).reshape(1, GDN_DV), state)
    return o, s_new, jnp.swapaxes(sc_new, 0, 1)


_NO_EXPERT = -1e30


def _merge_kernel(oa_ref, ob_ref, gate_ref, x_ref, wa_ref, wb_ref, wo_ref, nw_ref, wr_ref, br_ref,
                  x1_ref, h2_ref, ti_ref, tw_ref):
    ya = _dot(oa_ref[...], wa_ref[...])
    yb = _dot(ob_ref[...], wb_ref[...])
    ga = gate_ref[:, :D_MODEL].astype(F32)
    gb = gate_ref[:, D_MODEL:].astype(F32)
    m = _sigmoid(ga) * ya + _sigmoid(gb) * yb
    x1 = x_ref[...] + _dot(m.astype(BF16), wo_ref[...])
    x1_ref[...] = x1
    h2 = _rms(x1) * nw_ref[...]
    h2_ref[...] = h2
    lg = _dot_hi(h2, wr_ref[...]) + br_ref[...]
    lane = lax.broadcasted_iota(jnp.int32, lg.shape, 1).astype(F32)
    vals, idxs = [], []
    for _ in range(TOP_K):
        top = jnp.max(lg, axis=-1, keepdims=True)
        idx = jnp.min(jnp.where(lg == top, lane, float(LANES)), axis=-1, keepdims=True)
        vals.append(top)
        idxs.append(idx)
        lg = jnp.where(lane == idx, _NO_EXPERT, lg)
    es = [jnp.exp(v - vals[0]) for v in vals]
    inv_total = 1.0 / functools.reduce(lambda a, b: a + b, es)
    ti = jnp.zeros_like(lg)
    tw = jnp.zeros_like(lg)
    for k in range(TOP_K):
        ti = jnp.where(lane == float(k), idxs[k], ti)
        tw = jnp.where(lane == float(k), es[k] * inv_total, tw)
    ti_ref[...] = ti.astype(jnp.int32)
    tw_ref[...] = tw


def _merge(o_ret, o_gdn, p_gate, x, w_a, w_b, w_o, ffn_norm_w, w_router, b_router):
    t = x.shape[0]
    tm = _pick_tile(t, (384, 256, 128, 64, 8))
    wr = jnp.pad(w_router.astype(F32), ((0, 0), (0, LANES - N_EXPERTS)))
    br = jnp.pad(b_router.astype(F32), (0, LANES - N_EXPERTS), constant_values=_NO_EXPERT).reshape(1, LANES)
    row = lambda n: pl.BlockSpec((tm, n), lambda i: (i, 0))
    sq = (D_MODEL, D_MODEL)
    return pl.pallas_call(
        _merge_kernel,
        grid=(t // tm,),
        in_specs=[row(RET_V_W), row(GDN_V_W), row(2 * D_MODEL), row(D_MODEL), _resident(sq), _resident(sq),
                  _resident(sq), _resident((1, D_MODEL)), _resident((D_MODEL, LANES)), _resident((1, LANES))],
        out_specs=[row(D_MODEL), row(D_MODEL), row(LANES), row(LANES)],
        out_shape=[jax.ShapeDtypeStruct((t, D_MODEL), F32), jax.ShapeDtypeStruct((t, D_MODEL), F32),
                   jax.ShapeDtypeStruct((t, LANES), jnp.int32), jax.ShapeDtypeStruct((t, LANES), F32)],
        compiler_params=_params(("parallel",)),
        name="merge",
    )(o_ret, o_gdn, p_gate, x, w_a.astype(BF16), w_b.astype(BF16), w_o.astype(BF16),
      ffn_norm_w.astype(F32).reshape(1, D_MODEL), wr, br)


def _route(top_i, n_tokens):
    rows = MOE_ROWS
    n = n_tokens * TOP_K
    flat_e = top_i.reshape(n).astype(jnp.int32)
    onehot = flat_e[:, None] == jnp.arange(N_EXPERTS, dtype=jnp.int32)[None, :]
    seg = 256
    if n % seg == 0:
        tri = (jnp.arange(seg)[:, None] >= jnp.arange(seg)[None, :]).astype(BF16)
        inside = jnp.einsum('ij,bjk->bik', tri, onehot.astype(BF16).reshape(n // seg, seg, N_EXPERTS),
                            preferred_element_type=F32)
        totals = inside[:, -1, :]
        running = (inside + (jnp.cumsum(totals, axis=0) - totals)[:, None, :]).reshape(n, N_EXPERTS).astype(jnp.int32)
    else:
        running = jnp.cumsum(onehot.astype(jnp.int32), axis=0)
    counts = running[-1]
    pcounts = (counts + rows - 1) // rows * rows
    pend = jnp.cumsum(pcounts)
    pstart = pend - pcounts
    dest = jnp.sum(jnp.where(onehot, running - 1 + pstart[None, :], 0), axis=1)
    nb = -(-n // rows) + N_EXPERTS
    inv = jnp.full((nb * rows,), -1, jnp.int32).at[dest].set(jnp.arange(n, dtype=jnp.int32), unique_indices=True)
    block_e = jnp.minimum(jnp.sum(pend[None, :] <= (jnp.arange(nb, dtype=jnp.int32) * rows)[:, None], axis=1),
                          N_EXPERTS - 1).astype(jnp.int32)
    nb_used = (pend[-1] // rows).astype(jnp.int32).reshape(1)
    valid = inv >= 0
    spare = n + (jnp.arange(nb * rows, dtype=jnp.int32) % (2 * rows))
    src = jnp.where(valid, inv // TOP_K, 0).reshape(nb, rows)
    dst = jnp.where(valid, inv, spare).reshape(nb, rows)
    nxt = jnp.concatenate([src[1:], src[-1:]], axis=0)
    slab = jnp.concatenate([src, dst, nxt], axis=1)
    return block_e, nb_used, slab


def _expert_kernel(be_ref, nbu_ref, slab_ref, h_ref, wgu_ref, bgu_ref, wd_ref, bd_ref, y_ref,
                   idx_ref, xbuf_ref, ybuf_ref, wgu_bf_ref, wd_bf_ref, isem, gsem, ssem):
    rows = MOE_ROWS
    i = pl.program_id(0)
    nbu = nbu_ref[0]
    slot = i % 2
    active = i < nbu

    other = 1 - slot
    spare_rows = 2 * rows // TOP_K
    n_tok = y_ref.shape[0] - spare_rows

    def slab_copy(blk, sl):
        return pltpu.make_async_copy(slab_ref.at[blk], idx_ref.at[sl], isem.at[sl])

    def gather_row(tok, sl, r):
        return pltpu.make_async_copy(h_ref.at[pl.ds(tok, 1)], xbuf_ref.at[sl, pl.ds(r, 1)], gsem.at[sl])

    def gather_wait(sl):
        pltpu.make_async_copy(h_ref.at[pl.ds(0, rows)], xbuf_ref.at[sl], gsem.at[sl]).wait()

    def scatter_wait(sl):
        pltpu.make_async_copy(ybuf_ref.at[sl], y_ref.at[pl.ds(0, rows), pl.ds(0, D_MODEL)], ssem.at[sl]).wait()

    @pl.when(i == 0)
    def _():
        slab_copy(0, 0).start()
        ybuf_ref[...] = jnp.zeros_like(ybuf_ref)
        part = rows // TOP_K
        for sl in range(2):
            for kk in range(TOP_K):
                pltpu.make_async_copy(
                    ybuf_ref.at[sl, pl.ds(kk * part, part)],
                    y_ref.at[pl.ds(n_tok + sl * part, part), pl.ds(kk * D_MODEL, D_MODEL)], ssem.at[sl]).start()

    @pl.when(active)
    def _():
        slab_copy(i, slot).wait()

        @pl.when(i + 1 < nbu)
        def _():
            slab_copy(i + 1, other).start()

        @pl.when(i == 0)
        def _():
            def body(r, carry):
                gather_row(idx_ref[0, r], 0, r).start()
                return carry
            lax.fori_loop(0, rows, body, 0)

        changed = jnp.logical_or(i == 0, be_ref[i] != be_ref[jnp.maximum(i - 1, 0)])

        @pl.when(changed)
        def _():
            wgu_bf_ref[...] = wgu_ref[0].astype(BF16)
            wd_bf_ref[...] = wd_ref[0].astype(BF16)

        gather_wait(slot)
        scatter_wait(slot)

        xb = xbuf_ref[slot].astype(BF16)
        hb = _dot(xb, wgu_bf_ref[...]) + bgu_ref[0]
        for r in range(rows):
            gather_row(idx_ref[slot, 2 * rows + r], other, r).start()
        glu = jnp.minimum(hb[:, :D_FF], SWIGLU_LIMIT)
        lin = jnp.clip(hb[:, D_FF:], -SWIGLU_LIMIT, SWIGLU_LIMIT)
        act = (glu * _sigmoid(SWIGLU_ALPHA * glu) * (lin + 1.0)).astype(BF16)
        ybuf_ref[slot] = _dot(act, wd_bf_ref[...]) + bd_ref[0]
        for r in range(rows):
            f = idx_ref[slot, rows + r]
            col = pl.multiple_of((f % TOP_K) * D_MODEL, D_MODEL)
            pltpu.make_async_copy(ybuf_ref.at[slot, pl.ds(r, 1)],
                                  y_ref.at[pl.ds(f // TOP_K, 1), pl.ds(col, D_MODEL)], ssem.at[slot]).start()

        @pl.when(i == nbu - 1)
        def _():
            scatter_wait(slot)
            scatter_wait(other)
            gather_wait(other)


def _experts(h2, block_e, nb_used, slab, w_gate_up, b_gate_up, w_down, b_down):
    t = h2.shape[0]
    rows = MOE_ROWS
    nb = slab.shape[0]
    grid_spec = pltpu.PrefetchScalarGridSpec(
        num_scalar_prefetch=2,
        grid=(nb,),
        in_specs=[pl.BlockSpec(memory_space=pl.ANY),
                  pl.BlockSpec(memory_space=pl.ANY),
                  pl.BlockSpec((1, D_MODEL, 2 * D_FF), lambda i, be, nbu: (be[i], 0, 0)),
                  pl.BlockSpec((1, 1, 2 * D_FF), lambda i, be, nbu: (be[i], 0, 0)),
                  pl.BlockSpec((1, D_FF, D_MODEL), lambda i, be, nbu: (be[i], 0, 0)),
                  pl.BlockSpec((1, 1, D_MODEL), lambda i, be, nbu: (be[i], 0, 0))],
        out_specs=pl.BlockSpec(memory_space=pl.ANY),
        scratch_shapes=[pltpu.SMEM((2, 3 * rows), jnp.int32),
                        pltpu.VMEM((2, rows, D_MODEL), F32),
                        pltpu.VMEM((2, rows, D_MODEL), F32),
                        pltpu.VMEM((D_MODEL, 2 * D_FF), BF16),
                        pltpu.VMEM((D_FF, D_MODEL), BF16),
                        pltpu.SemaphoreType.DMA((2,)),
                        pltpu.SemaphoreType.DMA((2,)),
                        pltpu.SemaphoreType.DMA((2,))])
    return pl.pallas_call(
        _expert_kernel,
        grid_spec=grid_spec,
        out_shape=jax.ShapeDtypeStruct((t + 2 * rows // TOP_K, TOP_K * D_MODEL), F32),
        compiler_params=_params(("arbitrary",)),
        name="experts",
    )(block_e, nb_used, slab, h2, w_gate_up, b_gate_up.reshape(N_EXPERTS, 1, 2 * D_FF), w_down,
      b_down.reshape(N_EXPERTS, 1, D_MODEL))


def _combine_kernel(y_ref, w_ref, x1_ref, nw_ref, o_ref, *, final):
    acc = x1_ref[...]
    for k in range(TOP_K):
        acc = acc + w_ref[:, k:k + 1] * y_ref[:, k * D_MODEL:(k + 1) * D_MODEL]
    o_ref[...] = _rms(acc) * nw_ref[...] if final else acc


def _combine(y, gate, x1, norm_w, final):
    t = x1.shape[0]
    tm = _pick_tile(t, (384, 256, 128, 64, 8))
    return pl.pallas_call(
        functools.partial(_combine_kernel, final=final),
        grid=(t // tm,),
        in_specs=[pl.BlockSpec((tm, TOP_K * D_MODEL), lambda i: (i, 0)),
                  pl.BlockSpec((tm, LANES), lambda i: (i, 0)),
                  pl.BlockSpec((tm, D_MODEL), lambda i: (i, 0)), _resident((1, D_MODEL))],
        out_specs=pl.BlockSpec((tm, D_MODEL), lambda i: (i, 0)),
        out_shape=jax.ShapeDtypeStruct((t, D_MODEL), F32),
        compiler_params=_params(("parallel",)),
        name="combine",
    )(y, gate, x1, norm_w.astype(F32).reshape(1, D_MODEL))


def kernel(x_prompt, x_sample, state_ret, state_gdn, state_conv, attn_norm_w, w_in, conv_w, a_log, dt_bias, gdn_norm_w, w_branch_a, w_branch_b, w_out, ffn_norm_w, w_router, b_router, w_gate_up, b_gate_up, w_down, b_down, final_norm_w):
    bp, lp, d = x_prompt.shape
    bs, ls, _ = x_sample.shape
    assert ls == 1 and d == D_MODEL and lp % RET_CHUNK == 0 and bs % SAMPLE_TILE == 0
    depth = w_in.shape[0]
    tp = bp * lp
    x = jnp.concatenate([x_prompt.reshape(tp, d), x_sample.reshape(bs, d)], axis=0).astype(F32)
    t = tp + bs
    rp, gp, cp, rs, gs, cs = [], [], [], [], [], []
    for l in range(depth):
        p_ret, p_gdn, p_gate, p_ab = _inproj(x, attn_norm_w[l], w_in[l])
        o_ret_p, s_ret_p = _ret_prompt(p_ret, bp, lp)
        o_ret_s, s_ret_s = _ret_sample(p_ret[tp:].astype(F32), state_ret[l].astype(F32))
        o_gdn_p, s_gdn_p, conv_p = _gdn_prompt(p_gdn, p_ab, bp, lp, conv_w[l], a_log[l], dt_bias[l], gdn_norm_w[l])
        o_gdn_s, s_gdn_s, conv_s = _gdn_sample(p_gdn[tp:].astype(F32), p_ab[tp:], state_gdn[l].astype(F32),
                                               state_conv[l], conv_w[l], a_log[l], dt_bias[l], gdn_norm_w[l])
        o_ret = jnp.concatenate([o_ret_p, o_ret_s], axis=0)
        o_gdn = jnp.concatenate([o_gdn_p, o_gdn_s], axis=0)
        x1, h2, top_i, gate = _merge(o_ret, o_gdn, p_gate, x, w_branch_a[l], w_branch_b[l], w_out[l], ffn_norm_w[l],
                                     w_router[l], b_router[l])
        block_e, nb_used, slab = _route(top_i[:, :TOP_K], t)
        y = _experts(h2, block_e, nb_used, slab, w_gate_up[l], b_gate_up[l], w_down[l], b_down[l])
        last = l == depth - 1
        x = _combine(y, gate, x1, final_norm_w if last else jnp.ones((d,), F32), last)
        rp.append(s_ret_p); gp.append(s_gdn_p); cp.append(conv_p)
        rs.append(s_ret_s); gs.append(s_gdn_s); cs.append(conv_s)
    y_prompt = x[:tp].reshape(bp, lp, d).astype(x_prompt.dtype)
    y_sample = x[tp:].reshape(bs, ls, d).astype(x_sample.dtype)
    return (y_prompt, y_sample,
            jnp.stack(rp).astype(state_ret.dtype), jnp.stack(gp).astype(state_gdn.dtype),
            jnp.stack(cp).astype(state_conv.dtype),
            jnp.stack(rs).astype(state_ret.dtype), jnp.stack(gs).astype(state_gdn.dtype),
            jnp.stack(cs).astype(state_conv.dtype))
```

```python
import functools
import math

import numpy as np
import jax
import jax.numpy as jnp
from jax import lax
from jax.experimental import pallas as pl
from jax.experimental.pallas import tpu as pltpu

F32 = jnp.float32
BF16 = jnp.bfloat16
HIGHEST = lax.Precision.HIGHEST

D_MODEL = 1024
PAST_LEN = 16384
RET_HEADS, RET_DK, RET_DV = 4, 128, 256
RET_QK_W, RET_V_W = RET_HEADS * RET_DK, RET_HEADS * RET_DV
RET_CHUNK = 128
ROPE_BASE = 10000.0
GDN_HEADS, GDN_DK, GDN_DV = 8, 128, 128
GDN_QK_W, GDN_V_W = GDN_HEADS * GDN_DK, GDN_HEADS * GDN_DV
GDN_CHUNK = 64
CONV_W = 4
CONV_CH = 2 * GDN_QK_W + GDN_V_W
N_EXPERTS = 32
TOP_K = 4
D_FF = D_MODEL
SWIGLU_LIMIT = 7.0
SWIGLU_ALPHA = 1.702
NORM_EPS = 1e-6

_RET_W = 2 * RET_QK_W + 2 * RET_V_W
_GDN_W = CONV_CH + GDN_V_W
_AB_OFF = _RET_W + _GDN_W
_GATE_OFF = _AB_OFF + 2 * GDN_HEADS

LANES = 128
VMEM_LIMIT = 56 * 1024 * 1024
MOE_ROWS = 256
SAMPLE_TILE = 8
RET_STEP_ROWS = 256
GDN_PREP_ROWS = 128
GDN_SCAN_ROWS = 128


def _pick_tile(n, candidates):
    for c in candidates:
        if n % c == 0:
            return c
    raise ValueError(f"no tile in {candidates} divides {n}")


def _params(sem, vmem=VMEM_LIMIT):
    return pltpu.CompilerParams(dimension_semantics=sem, vmem_limit_bytes=vmem)


def _resident(shape):
    nd = len(shape)
    return pl.BlockSpec(shape, lambda *_: (0,) * nd, pipeline_mode=pl.Buffered(1))


def _silu(x):
    return x * (1.0 / (1.0 + jnp.exp(-x)))


def _sigmoid(x):
    return 1.0 / (1.0 + jnp.exp(-x))


def _softplus(x):
    return jnp.maximum(x, 0.0) + jnp.log1p(jnp.exp(-jnp.abs(x)))


def _rms(x):
    return x * lax.rsqrt(jnp.mean(x * x, axis=-1, keepdims=True) + NORM_EPS)


def _dot(a, b):
    return jnp.dot(a, b, preferred_element_type=F32)


def _dot_nt(a, b):
    return lax.dot_general(a, b, (((1,), (1,)), ((), ())), preferred_element_type=F32)


def _dot_tn(a, b):
    return lax.dot_general(a, b, (((0,), (0,)), ((), ())), preferred_element_type=F32)


def _dot_hi(a, b):
    return jnp.dot(a, b, preferred_element_type=F32, precision=HIGHEST)


def _dot_split(a, b):
    a_hi, b_hi = a.astype(BF16), b.astype(BF16)
    a_lo = (a - a_hi.astype(F32)).astype(BF16)
    b_lo = (b - b_hi.astype(F32)).astype(BF16)
    return _dot(a_hi, b_hi) + (_dot(a_hi, b_lo) + _dot(a_lo, b_hi))


def _row_tile(rows, row0, candidates):
    return _pick_tile(math.gcd(rows, row0) if row0 else rows, candidates)


def _inproj_kernel(x_ref, nw_ref, w_ref, oret_ref, ogdn_ref, ogate_ref, oab_ref):
    h = (_rms(x_ref[...]) * nw_ref[...]).astype(BF16)
    oret_ref[...] = _dot(h, w_ref[:, :_RET_W]).astype(BF16)
    ogdn_ref[...] = _dot(h, w_ref[:, _RET_W:_AB_OFF]).astype(BF16)
    tail = _dot(h, w_ref[:, _AB_OFF:])
    oab_ref[...] = tail[:, :LANES]
    ogate_ref[...] = tail[:, _GATE_OFF - _AB_OFF:_GATE_OFF - _AB_OFF + 2 * D_MODEL].astype(BF16)


def _inproj(x, norm_w, wb):
    rows = x.shape[0]
    tm = _pick_tile(rows, (256, 128, 64, 32, 16, 8))
    row = lambda n: pl.BlockSpec((tm, n), lambda i: (i, 0))
    return pl.pallas_call(
        _inproj_kernel,
        grid=(rows // tm,),
        in_specs=[row(D_MODEL), _resident((1, D_MODEL)), _resident(wb.shape)],
        out_specs=[row(_RET_W), row(_GDN_W), row(2 * D_MODEL), row(LANES)],
        out_shape=[jax.ShapeDtypeStruct((rows, _RET_W), BF16), jax.ShapeDtypeStruct((rows, _GDN_W), BF16),
                   jax.ShapeDtypeStruct((rows, 2 * D_MODEL), BF16), jax.ShapeDtypeStruct((rows, LANES), F32)],
        compiler_params=_params(("parallel",)),
        name="inproj",
    )(x, norm_w.reshape(1, D_MODEL), wb)


def _ret_log_gamma():
    return np.log1p(-np.exp2(-5.0 - np.arange(RET_HEADS, dtype=np.float64)))


def _rope_tables(pos):
    half = RET_DK // 2
    inv = 1.0 / (ROPE_BASE ** (jnp.arange(half, dtype=F32) / half))
    ang = pos.astype(F32)[:, None] * inv[None, :]
    cos, sin = jnp.cos(ang), jnp.sin(ang)
    return jnp.concatenate([cos, cos], axis=-1), jnp.concatenate([-sin, sin], axis=-1)


def _rotary(x, cos, sin):
    return x * cos + pltpu.roll(x, RET_DK // 2, 1) * sin


def _ret_prompt_kernel(q_ref, k_ref, v_ref, g_ref, cos_ref, sin_ref, dmask_ref, qdec_ref, kdec_ref,
                       o_ref, s_ref, *, gammas):
    @pl.when(pl.program_id(1) == 0)
    def _():
        s_ref[...] = jnp.zeros_like(s_ref)

    c = RET_CHUNK
    heads = range(RET_HEADS)
    qk = [slice(h * RET_DK, (h + 1) * RET_DK) for h in heads]
    vv = [slice(h * RET_DV, (h + 1) * RET_DV) for h in heads]
    s = [s_ref[0, h] for h in heads]
    for ck in range(q_ref.shape[0] // c):
        r = slice(ck * c, (ck + 1) * c)
        cos, sin = cos_ref[r, :], sin_ref[r, :]
        q = [_rotary(q_ref[r, qk[h]].astype(F32), cos, sin) for h in heads]
        k = [_rotary(k_ref[r, qk[h]].astype(F32), cos, sin) * (RET_DK ** -0.5) for h in heads]
        v = [v_ref[r, vv[h]] for h in heads]
        qb = [x.astype(BF16) for x in q]
        inner = [_dot_nt(qb[h], k[h].astype(BF16)) * dmask_ref[h] for h in heads]
        cross = [_dot((q[h] * qdec_ref[h]).astype(BF16), s[h].astype(BF16)) for h in heads]
        upd = [_dot_tn((k[h] * kdec_ref[h]).astype(BF16), v[h]) for h in heads]
        o = [_dot(inner[h].astype(BF16), v[h]) + cross[h] for h in heads]
        s = [s[h] * gammas[h] + upd[h] for h in heads]
        for h in heads:
            o_ref[r, vv[h]] = (_rms(o[h]) * _silu(g_ref[r, vv[h]].astype(F32))).astype(BF16)
    for h in heads:
        s_ref[0, h] = s[h]


def _ret_prompt(p_ret, batch, seq):
    c = RET_CHUNK
    step = RET_STEP_ROWS if seq % RET_STEP_ROWS == 0 else c
    n = seq // step
    lg = _ret_log_gamma()
    idx = np.arange(c, dtype=np.float64)
    diff = idx[:, None] - idx[None, :]
    dmask = np.where(diff >= 0, np.exp(np.maximum(diff, 0.0)[None] * lg[:, None, None]), 0.0)
    qdec = np.broadcast_to(np.exp((idx + 1.0)[None, :] * lg[:, None])[:, :, None], (RET_HEADS, c, RET_DK))
    kdec = np.broadcast_to(np.exp((c - 1.0 - idx)[None, :] * lg[:, None])[:, :, None], (RET_HEADS, c, RET_DK))
    gammas = tuple(float(g) for g in np.exp(c * lg))
    cos, sin = _rope_tables(jnp.arange(seq, dtype=jnp.int32))
    tab = lambda: _resident((RET_HEADS, c, RET_DK))
    return pl.pallas_call(
        functools.partial(_ret_prompt_kernel, gammas=gammas),
        grid=(batch, n),
        in_specs=[pl.BlockSpec((step, RET_QK_W), lambda b, j: (b * n + j, 0)),
                  pl.BlockSpec((step, RET_QK_W), lambda b, j: (b * n + j, 1)),
                  pl.BlockSpec((step, RET_V_W), lambda b, j: (b * n + j, 1)),
                  pl.BlockSpec((step, RET_V_W), lambda b, j: (b * n + j, 2)),
                  pl.BlockSpec((step, RET_DK), lambda b, j: (j, 0)),
                  pl.BlockSpec((step, RET_DK), lambda b, j: (j, 0)),
                  tab(), tab(), tab()],
        out_specs=[pl.BlockSpec((step, RET_V_W), lambda b, j: (b * n + j, 0)),
                   pl.BlockSpec((1, RET_HEADS, RET_DK, RET_DV), lambda b, j: (b, 0, 0, 0))],
        out_shape=[jax.ShapeDtypeStruct((batch * seq, RET_V_W), BF16),
                   jax.ShapeDtypeStruct((batch, RET_HEADS, RET_DK, RET_DV), F32)],
        compiler_params=_params(("parallel", "arbitrary")),
        name="ret_prompt",
    )(p_ret, p_ret, p_ret, p_ret, cos, sin, jnp.asarray(dmask, F32), jnp.asarray(qdec, F32),
      jnp.asarray(kdec, F32))


def _columns(x):
    n = x.shape[0]
    if n < LANES:
        x = jnp.concatenate([x, jnp.zeros((LANES - n, x.shape[1]), x.dtype)], axis=0)
    return x.T


def _ret_sample_kernel(p_ref, cos_ref, sin_ref, s_ref, o_ref, so_ref, *, gammas):
    cos, sin = cos_ref[...], sin_ref[...]
    nb = p_ref.shape[0]
    for h in range(RET_HEADS):
        qk = slice(h * RET_DK, (h + 1) * RET_DK)
        q = _rotary(p_ref[:, qk], cos, sin)
        k = _rotary(p_ref[:, RET_QK_W + h * RET_DK:RET_QK_W + (h + 1) * RET_DK], cos, sin) * (RET_DK ** -0.5)
        v = p_ref[:, 2 * RET_QK_W + h * RET_DV:2 * RET_QK_W + (h + 1) * RET_DV]
        g = p_ref[:, 2 * RET_QK_W + RET_V_W + h * RET_DV:2 * RET_QK_W + RET_V_W + (h + 1) * RET_DV]
        qk_dot = jnp.sum(q * k, axis=-1, keepdims=True)
        qt, kt = _columns(q), _columns(k)
        rows = []
        for j in range(nb):
            s = s_ref[j, h]
            qs = jnp.sum(qt[:, j:j + 1] * s, axis=0, keepdims=True)
            rows.append(qk_dot[j:j + 1] * v[j:j + 1] + gammas[h] * qs)
            so_ref[j, h] = s * gammas[h] + kt[:, j:j + 1] * v[j:j + 1]
        o = jnp.concatenate(rows, axis=0)
        o_ref[:, h * RET_DV:(h + 1) * RET_DV] = (_rms(o) * _silu(g)).astype(BF16)


def _ret_sample(p_ret_s, state):
    ts = p_ret_s.shape[0]
    sb = SAMPLE_TILE
    gammas = tuple(float(g) for g in np.exp(_ret_log_gamma()))
    cos, sin = _rope_tables(jnp.full((1,), PAST_LEN, jnp.int32))
    st = pl.BlockSpec((sb, RET_HEADS, RET_DK, RET_DV), lambda i: (i, 0, 0, 0))
    return pl.pallas_call(
        functools.partial(_ret_sample_kernel, gammas=gammas),
        grid=(ts // sb,),
        in_specs=[pl.BlockSpec((sb, _RET_W), lambda i: (i, 0)), _resident((1, RET_DK)), _resident((1, RET_DK)), st],
        out_specs=[pl.BlockSpec((sb, RET_V_W), lambda i: (i, 0)), st],
        out_shape=[jax.ShapeDtypeStruct((ts, RET_V_W), BF16), jax.ShapeDtypeStruct(state.shape, F32)],
        compiler_params=_params(("parallel",)),
        name="ret_sample",
    )(p_ret_s, cos, sin, state)


def _l2norm(x):
    return x * lax.rsqrt(jnp.sum(x * x, axis=-1, keepdims=True) + NORM_EPS)


def _bdot(a, b):
    return _dot(a.astype(BF16), b.astype(BF16))


def _chunk_masks(c):
    ri = lax.broadcasted_iota(jnp.int32, (c, c), 0)
    ci = lax.broadcasted_iota(jnp.int32, (c, c), 1)
    eye = (ri == ci).astype(F32)
    diag16 = (ri // 16 == ci // 16).astype(F32)
    low32 = jnp.logical_and(ri // 32 == ci // 32, ri // 16 > ci // 16).astype(F32)
    low64 = (ri // 32 > ci // 32).astype(F32)
    return ri >= ci, ri > ci, eye, diag16, low32, low64


def _unit_lower_inverse(a, eye, diag16, low32, low64):
    many = lambda f, *ls: [f(*args) for args in zip(*ls)]
    n = [-(x * diag16) for x in a]
    n2 = many(_bdot, n, n)
    n3 = many(_bdot, n, n2)
    n4 = many(_bdot, n2, n2)
    n8 = many(_bdot, n4, n4)
    x = [eye + p + q + r for p, q, r in zip(n, n2, n3)]
    x = many(lambda u, v: u + v, x, many(_bdot, x, n4))
    x = many(lambda u, v: u + v, x, many(_bdot, x, n8))
    for mask in (low32, low64):
        r = many(_bdot, [y * mask for y in a], x)
        x = many(lambda u, v: u - v, x, many(_bdot, x, r))
    return x


def _gdn_gates(ab, alog, dtb):
    g = -jnp.exp(alog) * _softplus(ab + dtb)
    return g, _sigmoid(ab)


def _gdn_prep_kernel(x_ref, prev_ref, ab_ref, cw_ref, alog_ref, dtb_ref,
                     u_ref, w_ref, qg_ref, kg_ref, at_ref, eg_ref, cv_ref, xc_ref, act_ref):
    c = GDN_CHUNK
    rows = x_ref.shape[0]
    prev = prev_ref[...].astype(F32)[8:16, :]
    xc_ref[0:8, :] = jnp.where(pl.program_id(1) == 0, 0.0, prev)
    xc_ref[8:8 + rows, :] = x_ref[...].astype(F32)
    acc = xc_ref[5:5 + rows, :] * cw_ref[0:1, :]
    for i in range(1, CONV_W):
        acc = acc + xc_ref[5 + i:5 + i + rows, :] * cw_ref[i:i + 1, :]
    cv_ref[0] = xc_ref[rows + 8 - (CONV_W - 1):rows + 8, :]
    act_ref[...] = _silu(acc)

    g_all, beta_all = _gdn_gates(ab_ref[...], alog_ref[...], dtb_ref[...])
    lower, strict, eye, diag16, low32, low64 = _chunk_masks(c)
    gc_all, gr_all = [], []
    for ck in range(rows // c):
        gc_ck = _dot_hi(lower.astype(F32), g_all[ck * c:(ck + 1) * c])
        gc_all.append(gc_ck)
        gr_all.append(_columns(gc_ck))
        eg_ref[ck] = jnp.exp(gc_ck[c - 1:c, :])
    chains = [(ck, h) for ck in range(rows // c) for h in range(GDN_HEADS)]
    rs = [slice(ck * c, (ck + 1) * c) for ck, _ in chains]
    sl = [slice(h * GDN_DK, (h + 1) * GDN_DK) for _, h in chains]
    nc = range(len(chains))
    q = [_l2norm(act_ref[rs[i], sl[i]]) * (GDN_DK ** -0.5) for i in nc]
    k = [_l2norm(act_ref[rs[i], GDN_QK_W + sl[i].start:GDN_QK_W + sl[i].stop]) for i in nc]
    v = [act_ref[rs[i], 2 * GDN_QK_W + sl[i].start:2 * GDN_QK_W + sl[i].stop] for i in nc]
    beta = [beta_all[rs[i], GDN_HEADS + h:GDN_HEADS + h + 1] for i, (_, h) in enumerate(chains)]
    gc = [gc_all[ck][:, h:h + 1] for ck, h in chains]
    gr = [gr_all[ck][h:h + 1, :c] for ck, h in chains]
    decay = [jnp.exp(jnp.where(lower, gc[i] - gr[i], -jnp.inf)) for i in nc]
    exp_g = [jnp.exp(x) for x in gc]
    kb = [k[i] * beta[i] for i in nc]
    kbf = [x.astype(BF16) for x in k]
    a = [_dot_nt(kb[i].astype(BF16), kbf[i]) * jnp.where(strict, decay[i], 0.0) for i in nc]
    attn = [_dot_nt(q[i].astype(BF16), kbf[i]) * decay[i] for i in nc]
    t = _unit_lower_inverse(a, eye, diag16, low32, low64)
    uu = [_bdot(t[i], v[i] * beta[i]) for i in nc]
    ww = [_bdot(t[i], kb[i] * exp_g[i]) for i in nc]
    for i in nc:
        u_ref[rs[i], sl[i]] = uu[i]
        w_ref[rs[i], sl[i]] = ww[i].astype(BF16)
        at_ref[rs[i], sl[i]] = jnp.concatenate([attn[i], jnp.zeros((c, GDN_DK - c), F32)], axis=1).astype(BF16)
        qg_ref[rs[i], sl[i]] = (q[i] * exp_g[i]).astype(BF16)
        kg_ref[rs[i], sl[i]] = (k[i] * jnp.exp(gc[i][c - 1:c, :] - gc[i])).astype(BF16)


def _gdn_scan_kernel(u_ref, w_ref, qg_ref, kg_ref, at_ref, eg_ref, z_ref, nw_ref, o_ref, s_ref):
    c = GDN_CHUNK

    @pl.when(pl.program_id(1) == 0)
    def _():
        s_ref[...] = jnp.zeros_like(s_ref)

    nw = nw_ref[...]
    heads = range(GDN_HEADS)
    sl = [slice(h * GDN_DK, (h + 1) * GDN_DK) for h in heads]
    s = [s_ref[0, h] for h in heads]
    for ck in range(u_ref.shape[0] // c):
        r = slice(ck * c, (ck + 1) * c)
        eg = eg_ref[ck]
        sb = [x.astype(BF16) for x in s]
        ws = [_dot(w_ref[r, sl[h]], sb[h]) for h in heads]
        qs = [_dot(qg_ref[r, sl[h]], sb[h]) for h in heads]
        vnb = [(u_ref[r, sl[h]] - ws[h]).astype(BF16) for h in heads]
        o = [qs[h] + _dot(at_ref[r, h * GDN_DK:h * GDN_DK + c], vnb[h]) for h in heads]
        s = [s[h] * eg[:, h:h + 1] + _dot_tn(kg_ref[r, sl[h]], vnb[h]) for h in heads]
        for h in heads:
            o_ref[r, sl[h]] = (_rms(o[h]) * nw * _silu(z_ref[r, sl[h]].astype(F32))).astype(BF16)
    for h in heads:
        s_ref[0, h] = s[h]


def _gdn_prompt(p_gdn, p_ab, batch, seq, conv_w, a_log, dt_bias, gdn_norm_w):
    c = GDN_CHUNK
    rows = GDN_PREP_ROWS
    t = batch * seq
    nt = seq // rows
    alog = jnp.pad(a_log.astype(F32), (0, LANES - GDN_HEADS)).reshape(1, LANES)
    dtb = jnp.pad(dt_bias.astype(F32), (0, LANES - GDN_HEADS)).reshape(1, LANES)
    wide = lambda: pl.BlockSpec((rows, GDN_V_W), lambda b, j: (b * nt + j, 0))
    u, w, qg, kg, at, eg, conv_new = pl.pallas_call(
        _gdn_prep_kernel,
        grid=(batch, nt),
        in_specs=[pl.BlockSpec((rows, CONV_CH), lambda b, j: (b * nt + j, 0)),
                  pl.BlockSpec((16, CONV_CH), lambda b, j: (jnp.maximum((b * nt + j) * (rows // 16) - 1, 0), 0)),
                  pl.BlockSpec((rows, LANES), lambda b, j: (b * nt + j, 0)),
                  _resident((CONV_W, CONV_CH)), _resident((1, LANES)), _resident((1, LANES))],
        out_specs=[wide(), wide(), wide(), wide(), wide(),
                   pl.BlockSpec((rows // c, 1, LANES), lambda b, j: (b * nt + j, 0, 0)),
                   pl.BlockSpec((1, CONV_W - 1, CONV_CH), lambda b, j: (b, 0, 0))],
        out_shape=[jax.ShapeDtypeStruct((t, GDN_V_W), F32)] + [jax.ShapeDtypeStruct((t, GDN_V_W), BF16)] * 4
        + [jax.ShapeDtypeStruct((t // c, 1, LANES), F32),
           jax.ShapeDtypeStruct((batch, CONV_W - 1, CONV_CH), F32)],
        scratch_shapes=[pltpu.VMEM((rows + 8, CONV_CH), F32), pltpu.VMEM((rows, CONV_CH), F32)],
        compiler_params=_params(("parallel", "arbitrary")),
        name="gdn_prep",
    )(p_gdn, p_gdn, p_ab, conv_w.astype(F32), alog, dtb)
    srows = GDN_SCAN_ROWS
    n = seq // srows
    nq = CONV_CH // GDN_V_W
    blk = lambda: pl.BlockSpec((srows, GDN_V_W), lambda b, j: (b * n + j, 0))
    o, s_new = pl.pallas_call(
        _gdn_scan_kernel,
        grid=(batch, n),
        in_specs=[blk(), blk(), blk(), blk(), blk(),
                  pl.BlockSpec((srows // c, 1, LANES), lambda b, j: (b * n + j, 0, 0)),
                  pl.BlockSpec((srows, GDN_V_W), lambda b, j: (b * n + j, nq)),
                  _resident((1, GDN_DV))],
        out_specs=[blk(), pl.BlockSpec((1, GDN_HEADS, GDN_DK, GDN_DV), lambda b, j: (b, 0, 0, 0))],
        out_shape=[jax.ShapeDtypeStruct((t, GDN_V_W), BF16),
                   jax.ShapeDtypeStruct((batch, GDN_HEADS, GDN_DK, GDN_DV), F32)],
        compiler_params=_params(("parallel", "arbitrary")),
        name="gdn_scan",
    )(u, w, qg, kg, at, eg, p_gdn, gdn_norm_w.astype(F32).reshape(1, GDN_DV))
    return o, s_new, conv_new


def _gdn_sample_kernel(x_ref, ab_ref, sc_ref, cw_ref, alog_ref, dtb_ref, nw_ref, s_ref, o_ref, so_ref, sco_ref):
    nb = x_ref.shape[0]
    x = x_ref[:, :CONV_CH]
    acc = x * cw_ref[CONV_W - 1:CONV_W, :]
    for i in range(CONV_W - 1):
        acc = acc + sc_ref[i] * cw_ref[i:i + 1, :]
    for i in range(CONV_W - 2):
        sco_ref[i] = sc_ref[i + 1]
    sco_ref[CONV_W - 2] = x
    u = _silu(acc)
    g_all, beta_all = _gdn_gates(ab_ref[...], alog_ref[...], dtb_ref[...])
    eg_all = jnp.exp(g_all)
    nw = nw_ref[...]
    for h in range(GDN_HEADS):
        sl = slice(h * GDN_DK, (h + 1) * GDN_DK)
        q = _l2norm(u[:, sl]) * (GDN_DK ** -0.5)
        k = _l2norm(u[:, GDN_QK_W + h * GDN_DK:GDN_QK_W + (h + 1) * GDN_DK])
        v = u[:, 2 * GDN_QK_W + h * GDN_DV:2 * GDN_QK_W + (h + 1) * GDN_DV]
        beta = beta_all[:, GDN_HEADS + h:GDN_HEADS + h + 1]
        eg = eg_all[:, h:h + 1]
        qk_dot = jnp.sum(q * k, axis=-1, keepdims=True)
        qt, kt = _columns(q), _columns(k)
        rows = []
        for j in range(nb):
            s = s_ref[j, h]
            kcol = kt[:, j:j + 1]
            ks = jnp.sum(kcol * s, axis=0, keepdims=True)
            qs = jnp.sum(qt[:, j:j + 1] * s, axis=0, keepdims=True)
            ej = eg[j:j + 1]
            v_new = beta[j:j + 1] * (v[j:j + 1] - ej * ks)
            rows.append(ej * qs + qk_dot[j:j + 1] * v_new)
            so_ref[j, h] = s * ej + kcol * v_new
        o = jnp.concatenate(rows, axis=0)
        z = x_ref[:, CONV_CH + h * GDN_DV:CONV_CH + (h + 1) * GDN_DV]
        o_ref[:, sl] = (_rms(o) * nw * _silu(z)).astype(BF16)


def _gdn_sample(p_gdn_s, p_ab_s, state, conv_state, conv_w, a_log, dt_bias, gdn_norm_w):
    ts = p_gdn_s.shape[0]
    sb = SAMPLE_TILE
    alog = jnp.pad(a_log.astype(F32), (0, LANES - GDN_HEADS)).reshape(1, LANES)
    dtb = jnp.pad(dt_bias.astype(F32), (0, LANES - GDN_HEADS)).reshape(1, LANES)
    sc = jnp.swapaxes(conv_state.astype(F32), 0, 1)
    st = pl.BlockSpec((sb, GDN_HEADS, GDN_DK, GDN_DV), lambda i: (i, 0, 0, 0))
    scs = pl.BlockSpec((CONV_W - 1, sb, CONV_CH), lambda i: (0, i, 0))
    o, s_new, sc_new = pl.pallas_call(
        _gdn_sample_kernel,
        grid=(ts // sb,),
        in_specs=[pl.BlockSpec((sb, _GDN_W), lambda i: (i, 0)), pl.BlockSpec((sb, LANES), lambda i: (i, 0)), scs,
                  _resident((CONV_W, CONV_CH)), _resident((1, LANES)), _resident((1, LANES)),
                  _resident((1, GDN_DV)), st],
        out_specs=[pl.BlockSpec((sb, GDN_V_W), lambda i: (i, 0)), st, scs],
        out_shape=[jax.ShapeDtypeStruct((ts, GDN_V_W), BF16), jax.ShapeDtypeStruct(state.shape, F32),
                   jax.ShapeDtypeStruct(sc.shape, F32)],
        compiler_params=_params(("parallel",)),
        name="gdn_sample",
    )(p_gdn_s, p_ab_s, sc, conv_w.astype(F32), alog, dtb, gdn_norm_w.astype(F32).reshape(1, GDN_DV), state)
    return o, s_new, jnp.swapaxes(sc_new, 0, 1)


_NO_EXPERT = -1e30


def _merge_kernel(oa0_ref, ob0_ref, gate0_ref, x0_ref, oa1_ref, ob1_ref, gate1_ref, x1in_ref,
                  wa_ref, wb_ref, wo_ref, nw_ref, wr_ref, br_ref, x1_ref, h2_ref, ti_ref, tw_ref, *, n_first):
    first = pl.program_id(0) < n_first
    pick = lambda a, b: jnp.where(first, a[...], b[...])
    gate = pick(gate0_ref, gate1_ref)
    ya = _dot(pick(oa0_ref, oa1_ref), wa_ref[...])
    yb = _dot(pick(ob0_ref, ob1_ref), wb_ref[...])
    ga = gate[:, :D_MODEL].astype(F32)
    gb = gate[:, D_MODEL:].astype(F32)
    m = _sigmoid(ga) * ya + _sigmoid(gb) * yb
    x1 = pick(x0_ref, x1in_ref) + _dot(m.astype(BF16), wo_ref[...])
    x1_ref[...] = x1
    h2 = _rms(x1) * nw_ref[...]
    for j in range(D_MODEL // LANES):
        h2_ref[:, j, :] = h2[:, j * LANES:(j + 1) * LANES]
    lg = _dot_split(h2, wr_ref[...]) + br_ref[...]
    lane = lax.broadcasted_iota(jnp.int32, lg.shape, 1).astype(F32)
    vals, idxs = [], []
    for _ in range(TOP_K):
        top = jnp.max(lg, axis=-1, keepdims=True)
        idx = jnp.min(jnp.where(lg == top, lane, float(LANES)), axis=-1, keepdims=True)
        vals.append(top)
        idxs.append(idx)
        lg = jnp.where(lane == idx, _NO_EXPERT, lg)
    es = [jnp.exp(v - vals[0]) for v in vals]
    inv_total = 1.0 / functools.reduce(lambda a, b: a + b, es)
    ti = jnp.zeros_like(lg)
    tw = jnp.zeros_like(lg)
    for k in range(TOP_K):
        ti = jnp.where(lane == float(k), idxs[k], ti)
        tw = jnp.where(lane == float(k), es[k] * inv_total, tw)
    ti_ref[...] = ti.astype(jnp.int32)
    tw_ref[...] = tw


def _merge(group0, group1, weights):
    r0, r1 = group0[3].shape[0], group1[3].shape[0]
    tm = _pick_tile(math.gcd(r0, r1), (128, 64, 32, 16, 8))
    n0, total = r0 // tm, r0 + r1
    widths = (RET_V_W, GDN_V_W, 2 * D_MODEL, D_MODEL)
    specs0 = [pl.BlockSpec((tm, n), lambda i: (jnp.minimum(i, n0 - 1), 0)) for n in widths]
    specs1 = [pl.BlockSpec((tm, n), lambda i: (jnp.maximum(i - n0, 0), 0)) for n in widths]
    out = lambda n: pl.BlockSpec((tm, n), lambda i: (i, 0))
    sub = D_MODEL // LANES
    sq = (D_MODEL, D_MODEL)
    return pl.pallas_call(
        functools.partial(_merge_kernel, n_first=n0),
        grid=(total // tm,),
        in_specs=specs0 + specs1 + [_resident(sq), _resident(sq), _resident(sq), _resident((1, D_MODEL)),
                                    _resident((D_MODEL, LANES)), _resident((1, LANES))],
        out_specs=[out(D_MODEL), pl.BlockSpec((tm, sub, LANES), lambda i: (i, 0, 0)), out(LANES), out(LANES)],
        out_shape=[jax.ShapeDtypeStruct((total, D_MODEL), F32), jax.ShapeDtypeStruct((total, sub, LANES), F32),
                   jax.ShapeDtypeStruct((total, LANES), jnp.int32), jax.ShapeDtypeStruct((total, LANES), F32)],
        compiler_params=_params(("parallel",)),
        name="merge",
    )(*group0, *group1, *weights)


def _merge_weights(w_a, w_b, w_o, ffn_norm_w, w_router, b_router):
    wr = jnp.pad(w_router.astype(F32), ((0, 0), (0, LANES - N_EXPERTS)))
    br = jnp.pad(b_router.astype(F32), (0, LANES - N_EXPERTS), constant_values=_NO_EXPERT).reshape(1, LANES)
    return (w_a.astype(BF16), w_b.astype(BF16), w_o.astype(BF16), ffn_norm_w.astype(F32).reshape(1, D_MODEL), wr, br)


def _route(top_i, n_tokens):
    rows = MOE_ROWS
    n = n_tokens * TOP_K
    flat_e = top_i.reshape(n).astype(jnp.int32)
    bits = max(1, (n - 1).bit_length())
    assert bits + (N_EXPERTS - 1).bit_length() <= 31
    order = lax.sort((flat_e << bits) | jnp.arange(n, dtype=jnp.int32)) & ((1 << bits) - 1)
    counts = jnp.sum((flat_e[:, None] == jnp.arange(N_EXPERTS, dtype=jnp.int32)[None, :]).astype(jnp.int32), axis=0)
    start = jnp.cumsum(counts) - counts
    pcounts = (counts + rows - 1) // rows * rows
    pend = jnp.cumsum(pcounts)
    pstart = pend - pcounts
    nb = -(-n // rows) + N_EXPERTS
    blk = jnp.arange(nb, dtype=jnp.int32)
    block_e = jnp.minimum(jnp.sum((pend[None, :] <= (blk * rows)[:, None]).astype(jnp.int32), axis=1),
                          N_EXPERTS - 1).astype(jnp.int32)
    nb_used = (pend[-1] // rows).astype(jnp.int32).reshape(1)
    within = (blk * rows - pstart[block_e])[:, None] + jnp.arange(rows, dtype=jnp.int32)[None, :]
    valid = jnp.logical_and(within < counts[block_e][:, None], (blk < nb_used[0])[:, None])
    flat = order[jnp.clip(start[block_e][:, None] + within, 0, n - 1)]
    spare = n + (blk % 2)[:, None] * rows + jnp.arange(rows, dtype=jnp.int32)[None, :]
    src = jnp.where(valid, flat // TOP_K, 0)
    dst = jnp.where(valid, flat, spare)
    nxt = jnp.concatenate([src[1:], src[-1:]], axis=0)
    slab = jnp.concatenate([src, dst, nxt], axis=1)
    return block_e, nb_used, slab


def _expert_kernel(be_ref, nbu_ref, slab_ref, h_ref, wgu_ref, bgu_ref, wd_ref, bd_ref, y_ref,
                   idx_ref, xbuf_ref, ybuf_ref, wgu_bf_ref, wd_bf_ref, isem, gsem, ssem):
    rows = MOE_ROWS
    i = pl.program_id(0)
    nbu = nbu_ref[0]
    slot = i % 2
    active = i < nbu

    other = 1 - slot
    sub = D_MODEL // LANES
    n_real = y_ref.shape[0] - 2 * rows

    def slab_copy(blk, sl):
        return pltpu.make_async_copy(slab_ref.at[blk], idx_ref.at[sl], isem.at[sl])

    def gather_row(tok, sl, r):
        return pltpu.make_async_copy(h_ref.at[tok], xbuf_ref.at[sl, r], gsem.at[sl])

    def gather_wait(sl):
        pltpu.make_async_copy(h_ref.at[pl.ds(0, rows)], xbuf_ref.at[sl], gsem.at[sl]).wait()

    def scatter_wait(sl):
        pltpu.make_async_copy(ybuf_ref.at[sl], y_ref.at[pl.ds(0, rows)], ssem.at[sl]).wait()

    @pl.when(i == 0)
    def _():
        slab_copy(0, 0).start()
        ybuf_ref[...] = jnp.zeros_like(ybuf_ref)
        for sl in range(2):
            pltpu.make_async_copy(ybuf_ref.at[sl], y_ref.at[pl.ds(n_real + sl * rows, rows)], ssem.at[sl]).start()

    @pl.when(active)
    def _():
        slab_copy(i, slot).wait()

        @pl.when(i + 1 < nbu)
        def _():
            slab_copy(i + 1, other).start()

        @pl.when(i == 0)
        def _():
            def body(r, carry):
                gather_row(idx_ref[0, r], 0, r).start()
                return carry
            lax.fori_loop(0, rows, body, 0)

        changed = jnp.logical_or(i == 0, be_ref[i] != be_ref[jnp.maximum(i - 1, 0)])

        @pl.when(changed)
        def _():
            wgu_bf_ref[...] = wgu_ref[0].astype(BF16)
            wd_bf_ref[...] = wd_ref[0].astype(BF16)

        gather_wait(slot)
        scatter_wait(slot)

        xb = jnp.concatenate([xbuf_ref[slot, :, j, :] for j in range(sub)], axis=1).astype(BF16)
        hb = _dot(xb, wgu_bf_ref[...]) + bgu_ref[0]
        for r in range(rows):
            gather_row(idx_ref[slot, 2 * rows + r], other, r).start()
        glu = jnp.minimum(hb[:, :D_FF], SWIGLU_LIMIT)
        lin = jnp.clip(hb[:, D_FF:], -SWIGLU_LIMIT, SWIGLU_LIMIT)
        act = (glu * _sigmoid(SWIGLU_ALPHA * glu) * (lin + 1.0)).astype(BF16)
        yv = _dot(act, wd_bf_ref[...]) + bd_ref[0]
        for j in range(sub):
            ybuf_ref[slot, :, j, :] = yv[:, j * LANES:(j + 1) * LANES]
        for r in range(rows):
            pltpu.make_async_copy(ybuf_ref.at[slot, r], y_ref.at[idx_ref[slot, rows + r]], ssem.at[slot]).start()

        @pl.when(i == nbu - 1)
        def _():
            scatter_wait(slot)
            scatter_wait(other)
            gather_wait(other)


def _experts(h2, block_e, nb_used, slab, w_gate_up, b_gate_up, w_down, b_down):
    t = h2.shape[0]
    rows = MOE_ROWS
    nb = slab.shape[0]
    sub = D_MODEL // LANES
    grid_spec = pltpu.PrefetchScalarGridSpec(
        num_scalar_prefetch=2,
        grid=(nb,),
        in_specs=[pl.BlockSpec(memory_space=pl.ANY),
                  pl.BlockSpec(memory_space=pl.ANY),
                  pl.BlockSpec((1, D_MODEL, 2 * D_FF), lambda i, be, nbu: (be[i], 0, 0)),
                  pl.BlockSpec((1, 1, 2 * D_FF), lambda i, be, nbu: (be[i], 0, 0)),
                  pl.BlockSpec((1, D_FF, D_MODEL), lambda i, be, nbu: (be[i], 0, 0)),
                  pl.BlockSpec((1, 1, D_MODEL), lambda i, be, nbu: (be[i], 0, 0))],
        out_specs=pl.BlockSpec(memory_space=pl.ANY),
        scratch_shapes=[pltpu.SMEM((2, 3 * rows), jnp.int32),
                        pltpu.VMEM((2, rows, sub, LANES), F32),
                        pltpu.VMEM((2, rows, sub, LANES), F32),
                        pltpu.VMEM((D_MODEL, 2 * D_FF), BF16),
                        pltpu.VMEM((D_FF, D_MODEL), BF16),
                        pltpu.SemaphoreType.DMA((2,)),
                        pltpu.SemaphoreType.DMA((2,)),
                        pltpu.SemaphoreType.DMA((2,))])
    return pl.pallas_call(
        _expert_kernel,
        grid_spec=grid_spec,
        out_shape=jax.ShapeDtypeStruct((t * TOP_K + 2 * rows, sub, LANES), F32),
        compiler_params=_params(("arbitrary",)),
        name="experts",
    )(block_e, nb_used, slab, h2, w_gate_up, b_gate_up.reshape(N_EXPERTS, 1, 2 * D_FF), w_down,
      b_down.reshape(N_EXPERTS, 1, D_MODEL))


def _combine_kernel(y_ref, w_ref, x1_ref, nw_ref, o_ref, *, final):
    sub = D_MODEL // LANES
    w = [w_ref[:, k:k + 1] for k in range(TOP_K)]
    cols = []
    for j in range(sub):
        acc = x1_ref[:, j * LANES:(j + 1) * LANES]
        for k in range(TOP_K):
            acc = acc + w[k] * y_ref[:, k, j, :]
        cols.append(acc)
    if final:
        ss = functools.reduce(lambda a, b: a + b, [jnp.sum(c * c, axis=-1, keepdims=True) for c in cols])
        scale = lax.rsqrt(ss * (1.0 / D_MODEL) + NORM_EPS)
        cols = [c * scale * nw_ref[:, j * LANES:(j + 1) * LANES] for j, c in enumerate(cols)]
    for j, c in enumerate(cols):
        o_ref[:, j * LANES:(j + 1) * LANES] = c


def _combine(y, gate, x1, row0, rows, norm_w, final):
    tm = _row_tile(rows, row0, (128, 64, 32, 16, 8))
    off = row0 // tm
    sub = D_MODEL // LANES
    y4 = y.reshape(y.shape[0] // TOP_K, TOP_K, sub, LANES)
    return pl.pallas_call(
        functools.partial(_combine_kernel, final=final),
        grid=(rows // tm,),
        in_specs=[pl.BlockSpec((tm, TOP_K, sub, LANES), lambda i: (off + i, 0, 0, 0)),
                  pl.BlockSpec((tm, LANES), lambda i: (off + i, 0)),
                  pl.BlockSpec((tm, D_MODEL), lambda i: (off + i, 0)), _resident((1, D_MODEL))],
        out_specs=pl.BlockSpec((tm, D_MODEL), lambda i: (i, 0)),
        out_shape=jax.ShapeDtypeStruct((rows, D_MODEL), F32),
        compiler_params=_params(("parallel",)),
        name="combine",
    )(y4, gate, x1, norm_w.astype(F32).reshape(1, D_MODEL))


def kernel(x_prompt, x_sample, state_ret, state_gdn, state_conv, attn_norm_w, w_in, conv_w, a_log, dt_bias, gdn_norm_w, w_branch_a, w_branch_b, w_out, ffn_norm_w, w_router, b_router, w_gate_up, b_gate_up, w_down, b_down, final_norm_w):
    bp, lp, d = x_prompt.shape
    bs, ls, _ = x_sample.shape
    assert ls == 1 and d == D_MODEL and lp % RET_CHUNK == 0 and bs % SAMPLE_TILE == 0
    depth = w_in.shape[0]
    tp = bp * lp
    t = tp + bs
    xp, xs = x_prompt.reshape(tp, d).astype(F32), x_sample.reshape(bs, d).astype(F32)
    rp, gp, cp, rs, gs, cs = [], [], [], [], [], []
    for l in range(depth):
        wb = jnp.pad(w_in[l].astype(BF16), ((0, 0), (0, -w_in.shape[2] % LANES)))
        pp_ret, pp_gdn, pp_gate, pp_ab = _inproj(xp, attn_norm_w[l], wb)
        ps_ret, ps_gdn, ps_gate, ps_ab = _inproj(xs, attn_norm_w[l], wb)
        op_ret, s_ret_p = _ret_prompt(pp_ret, bp, lp)
        os_ret, s_ret_s = _ret_sample(ps_ret.astype(F32), state_ret[l].astype(F32))
        op_gdn, s_gdn_p, conv_p = _gdn_prompt(pp_gdn, pp_ab, bp, lp, conv_w[l], a_log[l], dt_bias[l], gdn_norm_w[l])
        os_gdn, s_gdn_s, conv_s = _gdn_sample(ps_gdn.astype(F32), ps_ab, state_gdn[l].astype(F32), state_conv[l],
                                              conv_w[l], a_log[l], dt_bias[l], gdn_norm_w[l])
        mw = _merge_weights(w_branch_a[l], w_branch_b[l], w_out[l], ffn_norm_w[l], w_router[l], b_router[l])
        x1, h2, top_i, gate = _merge((op_ret, op_gdn, pp_gate, xp), (os_ret, os_gdn, ps_gate, xs), mw)
        block_e, nb_used, slab = _route(top_i[:, :TOP_K], t)
        y = _experts(h2, block_e, nb_used, slab, w_gate_up[l], b_gate_up[l], w_down[l], b_down[l])
        last = l == depth - 1
        norm_w = final_norm_w if last else jnp.ones((d,), F32)
        xp = _combine(y, gate, x1, 0, tp, norm_w, last)
        xs = _combine(y, gate, x1, tp, bs, norm_w, last)
        rp.append(s_ret_p); gp.append(s_gdn_p); cp.append(conv_p)
        rs.append(s_ret_s); gs.append(s_gdn_s); cs.append(conv_s)
    y_prompt = xp.reshape(bp, lp, d).astype(x_prompt.dtype)
    y_sample = xs.reshape(bs, ls, d).astype(x_sample.dtype)
    return (y_prompt, y_sample,
            jnp.stack(rp).astype(state_ret.dtype), jnp.stack(gp).astype(state_gdn.dtype),
            jnp.stack(cp).astype(state_conv.dtype),
            jnp.stack(rs).astype(state_ret.dtype), jnp.stack(gs).astype(state_gdn.dtype),
            jnp.stack(cs).astype(state_conv.dtype))
```

```python
import functools
import math

import numpy as np
import jax
import jax.numpy as jnp
from jax import lax
from jax.experimental import pallas as pl
from jax.experimental.pallas import tpu as pltpu

F32 = jnp.float32
BF16 = jnp.bfloat16
HIGHEST = lax.Precision.HIGHEST

D_MODEL = 1024
PAST_LEN = 16384
RET_HEADS, RET_DK, RET_DV = 4, 128, 256
RET_QK_W, RET_V_W = RET_HEADS * RET_DK, RET_HEADS * RET_DV
RET_CHUNK = 128
ROPE_BASE = 10000.0
GDN_HEADS, GDN_DK, GDN_DV = 8, 128, 128
GDN_QK_W, GDN_V_W = GDN_HEADS * GDN_DK, GDN_HEADS * GDN_DV
GDN_CHUNK = 64
CONV_W = 4
CONV_CH = 2 * GDN_QK_W + GDN_V_W
N_EXPERTS = 32
TOP_K = 4
D_FF = D_MODEL
SWIGLU_LIMIT = 7.0
SWIGLU_ALPHA = 1.702
NORM_EPS = 1e-6

_RET_W = 2 * RET_QK_W + 2 * RET_V_W
_GDN_W = CONV_CH + GDN_V_W
_AB_OFF = _RET_W + _GDN_W
_GATE_OFF = _AB_OFF + 2 * GDN_HEADS

LANES = 128
VMEM_LIMIT = 56 * 1024 * 1024
MOE_ROWS = 256
SAMPLE_TILE = 8
RET_STEP_ROWS = 256
GDN_PREP_ROWS = 128
GDN_SCAN_ROWS = 128


def _pick_tile(n, candidates):
    for c in candidates:
        if n % c == 0:
            return c
    raise ValueError(f"no tile in {candidates} divides {n}")


def _params(sem, vmem=VMEM_LIMIT):
    return pltpu.CompilerParams(dimension_semantics=sem, vmem_limit_bytes=vmem)


def _resident(shape):
    nd = len(shape)
    return pl.BlockSpec(shape, lambda *_: (0,) * nd, pipeline_mode=pl.Buffered(1))


def _silu(x):
    return x * (1.0 / (1.0 + jnp.exp(-x)))


def _sigmoid(x):
    return 1.0 / (1.0 + jnp.exp(-x))


def _softplus(x):
    return jnp.maximum(x, 0.0) + jnp.log1p(jnp.exp(-jnp.abs(x)))


def _rms(x):
    return x * lax.rsqrt(jnp.mean(x * x, axis=-1, keepdims=True) + NORM_EPS)


def _dot(a, b):
    return jnp.dot(a, b, preferred_element_type=F32)


def _dot_nt(a, b):
    return lax.dot_general(a, b, (((1,), (1,)), ((), ())), preferred_element_type=F32)


def _dot_tn(a, b):
    return lax.dot_general(a, b, (((0,), (0,)), ((), ())), preferred_element_type=F32)


def _dot_hi(a, b):
    return jnp.dot(a, b, preferred_element_type=F32, precision=HIGHEST)


def _dot_split(a, b):
    a_hi, b_hi = a.astype(BF16), b.astype(BF16)
    a_lo = (a - a_hi.astype(F32)).astype(BF16)
    b_lo = (b - b_hi.astype(F32)).astype(BF16)
    return _dot(a_hi, b_hi) + (_dot(a_hi, b_lo) + _dot(a_lo, b_hi))


def _row_tile(rows, row0, candidates):
    return _pick_tile(math.gcd(rows, row0) if row0 else rows, candidates)


def _inproj_kernel(x_ref, nw_ref, w_ref, oret_ref, ogdn_ref, ogate_ref, oab_ref):
    h = (_rms(x_ref[...]) * nw_ref[...]).astype(BF16)
    oret_ref[...] = _dot(h, w_ref[:, :_RET_W]).astype(BF16)
    ogdn_ref[...] = _dot(h, w_ref[:, _RET_W:_AB_OFF]).astype(BF16)
    tail = _dot(h, w_ref[:, _AB_OFF:])
    oab_ref[...] = tail[:, :LANES]
    ogate_ref[...] = tail[:, _GATE_OFF - _AB_OFF:_GATE_OFF - _AB_OFF + 2 * D_MODEL].astype(BF16)


def _inproj(x, norm_w, wb):
    rows = x.shape[0]
    tm = _pick_tile(rows, (256, 128, 64, 32, 16, 8))
    row = lambda n: pl.BlockSpec((tm, n), lambda i: (i, 0))
    return pl.pallas_call(
        _inproj_kernel,
        grid=(rows // tm,),
        in_specs=[row(D_MODEL), _resident((1, D_MODEL)), _resident(wb.shape)],
        out_specs=[row(_RET_W), row(_GDN_W), row(2 * D_MODEL), row(LANES)],
        out_shape=[jax.ShapeDtypeStruct((rows, _RET_W), BF16), jax.ShapeDtypeStruct((rows, _GDN_W), BF16),
                   jax.ShapeDtypeStruct((rows, 2 * D_MODEL), BF16), jax.ShapeDtypeStruct((rows, LANES), F32)],
        compiler_params=_params(("parallel",)),
        name="inproj",
    )(x, norm_w.reshape(1, D_MODEL), wb)


def _ret_log_gamma():
    return np.log1p(-np.exp2(-5.0 - np.arange(RET_HEADS, dtype=np.float64)))


def _rope_tables(pos):
    half = RET_DK // 2
    inv = 1.0 / (ROPE_BASE ** (jnp.arange(half, dtype=F32) / half))
    ang = pos.astype(F32)[:, None] * inv[None, :]
    cos, sin = jnp.cos(ang), jnp.sin(ang)
    return jnp.concatenate([cos, cos], axis=-1), jnp.concatenate([-sin, sin], axis=-1)


def _rotary(x, cos, sin):
    return x * cos + pltpu.roll(x, RET_DK // 2, 1) * sin


def _ret_prompt_kernel(q_ref, k_ref, v_ref, g_ref, cos_ref, sin_ref, dmask_ref, qdec_ref, kdec_ref,
                       o_ref, s_ref, *, gammas):
    @pl.when(pl.program_id(1) == 0)
    def _():
        s_ref[...] = jnp.zeros_like(s_ref)

    c = RET_CHUNK
    heads = range(RET_HEADS)
    qk = [slice(h * RET_DK, (h + 1) * RET_DK) for h in heads]
    vv = [slice(h * RET_DV, (h + 1) * RET_DV) for h in heads]
    s = [s_ref[0, h] for h in heads]
    for ck in range(q_ref.shape[0] // c):
        r = slice(ck * c, (ck + 1) * c)
        cos, sin = cos_ref[r, :], sin_ref[r, :]
        q = [_rotary(q_ref[r, qk[h]].astype(F32), cos, sin) for h in heads]
        k = [_rotary(k_ref[r, qk[h]].astype(F32), cos, sin) * (RET_DK ** -0.5) for h in heads]
        v = [v_ref[r, vv[h]] for h in heads]
        qb = [x.astype(BF16) for x in q]
        inner = [_dot_nt(qb[h], k[h].astype(BF16)) * dmask_ref[h] for h in heads]
        cross = [_dot((q[h] * qdec_ref[h]).astype(BF16), s[h].astype(BF16)) for h in heads]
        upd = [_dot_tn((k[h] * kdec_ref[h]).astype(BF16), v[h]) for h in heads]
        o = [_dot(inner[h].astype(BF16), v[h]) + cross[h] for h in heads]
        s = [s[h] * gammas[h] + upd[h] for h in heads]
        for h in heads:
            o_ref[r, vv[h]] = (_rms(o[h]) * _silu(g_ref[r, vv[h]].astype(F32))).astype(BF16)
    for h in heads:
        s_ref[0, h] = s[h]


def _ret_prompt(p_ret, batch, seq):
    c = RET_CHUNK
    step = RET_STEP_ROWS if seq % RET_STEP_ROWS == 0 else c
    n = seq // step
    lg = _ret_log_gamma()
    idx = np.arange(c, dtype=np.float64)
    diff = idx[:, None] - idx[None, :]
    dmask = np.where(diff >= 0, np.exp(np.maximum(diff, 0.0)[None] * lg[:, None, None]), 0.0)
    qdec = np.broadcast_to(np.exp((idx + 1.0)[None, :] * lg[:, None])[:, :, None], (RET_HEADS, c, RET_DK))
    kdec = np.broadcast_to(np.exp((c - 1.0 - idx)[None, :] * lg[:, None])[:, :, None], (RET_HEADS, c, RET_DK))
    gammas = tuple(float(g) for g in np.exp(c * lg))
    cos, sin = _rope_tables(jnp.arange(seq, dtype=jnp.int32))
    tab = lambda: _resident((RET_HEADS, c, RET_DK))
    return pl.pallas_call(
        functools.partial(_ret_prompt_kernel, gammas=gammas),
        grid=(batch, n),
        in_specs=[pl.BlockSpec((step, RET_QK_W), lambda b, j: (b * n + j, 0)),
                  pl.BlockSpec((step, RET_QK_W), lambda b, j: (b * n + j, 1)),
                  pl.BlockSpec((step, RET_V_W), lambda b, j: (b * n + j, 1)),
                  pl.BlockSpec((step, RET_V_W), lambda b, j: (b * n + j, 2)),
                  pl.BlockSpec((step, RET_DK), lambda b, j: (j, 0)),
                  pl.BlockSpec((step, RET_DK), lambda b, j: (j, 0)),
                  tab(), tab(), tab()],
        out_specs=[pl.BlockSpec((step, RET_V_W), lambda b, j: (b * n + j, 0)),
                   pl.BlockSpec((1, RET_HEADS, RET_DK, RET_DV), lambda b, j: (b, 0, 0, 0))],
        out_shape=[jax.ShapeDtypeStruct((batch * seq, RET_V_W), BF16),
                   jax.ShapeDtypeStruct((batch, RET_HEADS, RET_DK, RET_DV), F32)],
        compiler_params=_params(("parallel", "arbitrary")),
        name="ret_prompt",
    )(p_ret, p_ret, p_ret, p_ret, cos, sin, jnp.asarray(dmask, F32), jnp.asarray(qdec, F32),
      jnp.asarray(kdec, F32))


def _columns(x):
    n = x.shape[0]
    if n < LANES:
        x = jnp.concatenate([x, jnp.zeros((LANES - n, x.shape[1]), x.dtype)], axis=0)
    return x.T


def _ret_sample_kernel(p_ref, cos_ref, sin_ref, s_ref, o_ref, so_ref, *, gammas):
    cos, sin = cos_ref[...], sin_ref[...]
    nb = p_ref.shape[0]
    for h in range(RET_HEADS):
        qk = slice(h * RET_DK, (h + 1) * RET_DK)
        q = _rotary(p_ref[:, qk], cos, sin)
        k = _rotary(p_ref[:, RET_QK_W + h * RET_DK:RET_QK_W + (h + 1) * RET_DK], cos, sin) * (RET_DK ** -0.5)
        v = p_ref[:, 2 * RET_QK_W + h * RET_DV:2 * RET_QK_W + (h + 1) * RET_DV]
        g = p_ref[:, 2 * RET_QK_W + RET_V_W + h * RET_DV:2 * RET_QK_W + RET_V_W + (h + 1) * RET_DV]
        qk_dot = jnp.sum(q * k, axis=-1, keepdims=True)
        qt, kt = _columns(q), _columns(k)
        rows = []
        for j in range(nb):
            s = s_ref[j, h]
            qs = jnp.sum(qt[:, j:j + 1] * s, axis=0, keepdims=True)
            rows.append(qk_dot[j:j + 1] * v[j:j + 1] + gammas[h] * qs)
            so_ref[j, h] = s * gammas[h] + kt[:, j:j + 1] * v[j:j + 1]
        o = jnp.concatenate(rows, axis=0)
        o_ref[:, h * RET_DV:(h + 1) * RET_DV] = (_rms(o) * _silu(g)).astype(BF16)


def _ret_sample(p_ret_s, state):
    ts = p_ret_s.shape[0]
    sb = SAMPLE_TILE
    gammas = tuple(float(g) for g in np.exp(_ret_log_gamma()))
    cos, sin = _rope_tables(jnp.full((1,), PAST_LEN, jnp.int32))
    st = pl.BlockSpec((sb, RET_HEADS, RET_DK, RET_DV), lambda i: (i, 0, 0, 0))
    return pl.pallas_call(
        functools.partial(_ret_sample_kernel, gammas=gammas),
        grid=(ts // sb,),
        in_specs=[pl.BlockSpec((sb, _RET_W), lambda i: (i, 0)), _resident((1, RET_DK)), _resident((1, RET_DK)), st],
        out_specs=[pl.BlockSpec((sb, RET_V_W), lambda i: (i, 0)), st],
        out_shape=[jax.ShapeDtypeStruct((ts, RET_V_W), BF16), jax.ShapeDtypeStruct(state.shape, F32)],
        compiler_params=_params(("parallel",)),
        name="ret_sample",
    )(p_ret_s, cos, sin, state)


def _l2norm(x):
    return x * lax.rsqrt(jnp.sum(x * x, axis=-1, keepdims=True) + NORM_EPS)


def _bdot(a, b):
    return _dot(a.astype(BF16), b.astype(BF16))


def _chunk_masks(c):
    ri = lax.broadcasted_iota(jnp.int32, (c, c), 0)
    ci = lax.broadcasted_iota(jnp.int32, (c, c), 1)
    eye = (ri == ci).astype(F32)
    diag16 = (ri // 16 == ci // 16).astype(F32)
    low32 = jnp.logical_and(ri // 32 == ci // 32, ri // 16 > ci // 16).astype(F32)
    low64 = (ri // 32 > ci // 32).astype(F32)
    return ri >= ci, ri > ci, eye, diag16, low32, low64


def _unit_lower_inverse(a, eye, diag16, low32, low64):
    many = lambda f, *ls: [f(*args) for args in zip(*ls)]
    n = [-(x * diag16) for x in a]
    n2 = many(_bdot, n, n)
    n3 = many(_bdot, n, n2)
    n4 = many(_bdot, n2, n2)
    n8 = many(_bdot, n4, n4)
    x = [eye + p + q + r for p, q, r in zip(n, n2, n3)]
    x = many(lambda u, v: u + v, x, many(_bdot, x, n4))
    x = many(lambda u, v: u + v, x, many(_bdot, x, n8))
    for mask in (low32, low64):
        r = many(_bdot, [y * mask for y in a], x)
        x = many(lambda u, v: u - v, x, many(_bdot, x, r))
    return x


def _gdn_gates(ab, alog, dtb):
    g = -jnp.exp(alog) * _softplus(ab + dtb)
    return g, _sigmoid(ab)


def _gdn_prep_kernel(x_ref, prev_ref, ab_ref, cw_ref, alog_ref, dtb_ref,
                     u_ref, w_ref, qg_ref, kg_ref, at_ref, eg_ref, cv_ref, xc_ref, act_ref):
    c = GDN_CHUNK
    rows = x_ref.shape[0]
    prev = prev_ref[...].astype(F32)[8:16, :]
    xc_ref[0:8, :] = jnp.where(pl.program_id(1) == 0, 0.0, prev)
    xc_ref[8:8 + rows, :] = x_ref[...].astype(F32)
    acc = xc_ref[5:5 + rows, :] * cw_ref[0:1, :]
    for i in range(1, CONV_W):
        acc = acc + xc_ref[5 + i:5 + i + rows, :] * cw_ref[i:i + 1, :]
    cv_ref[0] = xc_ref[rows + 8 - (CONV_W - 1):rows + 8, :]
    act_ref[...] = _silu(acc)

    g_all, beta_all = _gdn_gates(ab_ref[...], alog_ref[...], dtb_ref[...])
    lower, strict, eye, diag16, low32, low64 = _chunk_masks(c)
    gc_all, gr_all = [], []
    for ck in range(rows // c):
        gc_ck = _dot_hi(lower.astype(F32), g_all[ck * c:(ck + 1) * c])
        gc_all.append(gc_ck)
        gr_all.append(_columns(gc_ck))
        eg_ref[ck] = jnp.exp(gc_ck[c - 1:c, :])
    chains = [(ck, h) for ck in range(rows // c) for h in range(GDN_HEADS)]
    rs = [slice(ck * c, (ck + 1) * c) for ck, _ in chains]
    sl = [slice(h * GDN_DK, (h + 1) * GDN_DK) for _, h in chains]
    nc = range(len(chains))
    q = [_l2norm(act_ref[rs[i], sl[i]]) * (GDN_DK ** -0.5) for i in nc]
    k = [_l2norm(act_ref[rs[i], GDN_QK_W + sl[i].start:GDN_QK_W + sl[i].stop]) for i in nc]
    v = [act_ref[rs[i], 2 * GDN_QK_W + sl[i].start:2 * GDN_QK_W + sl[i].stop] for i in nc]
    beta = [beta_all[rs[i], GDN_HEADS + h:GDN_HEADS + h + 1] for i, (_, h) in enumerate(chains)]
    gc = [gc_all[ck][:, h:h + 1] for ck, h in chains]
    gr = [gr_all[ck][h:h + 1, :c] for ck, h in chains]
    decay = [jnp.exp(jnp.where(lower, gc[i] - gr[i], -jnp.inf)) for i in nc]
    exp_g = [jnp.exp(x) for x in gc]
    kb = [k[i] * beta[i] for i in nc]
    kbf = [x.astype(BF16) for x in k]
    a = [_dot_nt(kb[i].astype(BF16), kbf[i]) * jnp.where(strict, decay[i], 0.0) for i in nc]
    attn = [_dot_nt(q[i].astype(BF16), kbf[i]) * decay[i] for i in nc]
    t = _unit_lower_inverse(a, eye, diag16, low32, low64)
    uu = [_bdot(t[i], v[i] * beta[i]) for i in nc]
    ww = [_bdot(t[i], kb[i] * exp_g[i]) for i in nc]
    for i in nc:
        u_ref[rs[i], sl[i]] = uu[i]
        w_ref[rs[i], sl[i]] = ww[i].astype(BF16)
        at_ref[rs[i], sl[i]] = jnp.concatenate([attn[i], jnp.zeros((c, GDN_DK - c), F32)], axis=1).astype(BF16)
        qg_ref[rs[i], sl[i]] = (q[i] * exp_g[i]).astype(BF16)
        kg_ref[rs[i], sl[i]] = (k[i] * jnp.exp(gc[i][c - 1:c, :] - gc[i])).astype(BF16)


def _gdn_scan_kernel(u_ref, w_ref, qg_ref, kg_ref, at_ref, eg_ref, z_ref, nw_ref, o_ref, s_ref):
    c = GDN_CHUNK

    @pl.when(pl.program_id(1) == 0)
    def _():
        s_ref[...] = jnp.zeros_like(s_ref)

    nw = nw_ref[...]
    heads = range(GDN_HEADS)
    sl = [slice(h * GDN_DK, (h + 1) * GDN_DK) for h in heads]
    s = [s_ref[0, h] for h in heads]
    for ck in range(u_ref.shape[0] // c):
        r = slice(ck * c, (ck + 1) * c)
        eg = eg_ref[ck]
        sb = [x.astype(BF16) for x in s]
        ws = [_dot(w_ref[r, sl[h]], sb[h]) for h in heads]
        qs = [_dot(qg_ref[r, sl[h]], sb[h]) for h in heads]
        vnb = [(u_ref[r, sl[h]] - ws[h]).astype(BF16) for h in heads]
        o = [qs[h] + _dot(at_ref[r, h * GDN_DK:h * GDN_DK + c], vnb[h]) for h in heads]
        s = [s[h] * eg[:, h:h + 1] + _dot_tn(kg_ref[r, sl[h]], vnb[h]) for h in heads]
        for h in heads:
            o_ref[r, sl[h]] = (_rms(o[h]) * nw * _silu(z_ref[r, sl[h]].astype(F32))).astype(BF16)
    for h in heads:
        s_ref[0, h] = s[h]


def _gdn_prompt(p_gdn, p_ab, batch, seq, conv_w, a_log, dt_bias, gdn_norm_w):
    c = GDN_CHUNK
    rows = GDN_PREP_ROWS
    t = batch * seq
    nt = seq // rows
    alog = jnp.pad(a_log.astype(F32), (0, LANES - GDN_HEADS)).reshape(1, LANES)
    dtb = jnp.pad(dt_bias.astype(F32), (0, LANES - GDN_HEADS)).reshape(1, LANES)
    wide = lambda: pl.BlockSpec((rows, GDN_V_W), lambda b, j: (b * nt + j, 0))
    u, w, qg, kg, at, eg, conv_new = pl.pallas_call(
        _gdn_prep_kernel,
        grid=(batch, nt),
        in_specs=[pl.BlockSpec((rows, CONV_CH), lambda b, j: (b * nt + j, 0)),
                  pl.BlockSpec((16, CONV_CH), lambda b, j: (jnp.maximum((b * nt + j) * (rows // 16) - 1, 0), 0)),
                  pl.BlockSpec((rows, LANES), lambda b, j: (b * nt + j, 0)),
                  _resident((CONV_W, CONV_CH)), _resident((1, LANES)), _resident((1, LANES))],
        out_specs=[wide(), wide(), wide(), wide(), wide(),
                   pl.BlockSpec((rows // c, 1, LANES), lambda b, j: (b * nt + j, 0, 0)),
                   pl.BlockSpec((1, CONV_W - 1, CONV_CH), lambda b, j: (b, 0, 0))],
        out_shape=[jax.ShapeDtypeStruct((t, GDN_V_W), F32)] + [jax.ShapeDtypeStruct((t, GDN_V_W), BF16)] * 4
        + [jax.ShapeDtypeStruct((t // c, 1, LANES), F32),
           jax.ShapeDtypeStruct((batch, CONV_W - 1, CONV_CH), F32)],
        scratch_shapes=[pltpu.VMEM((rows + 8, CONV_CH), F32), pltpu.VMEM((rows, CONV_CH), F32)],
        compiler_params=_params(("parallel", "arbitrary")),
        name="gdn_prep",
    )(p_gdn, p_gdn, p_ab, conv_w.astype(F32), alog, dtb)
    srows = GDN_SCAN_ROWS
    n = seq // srows
    nq = CONV_CH // GDN_V_W
    blk = lambda: pl.BlockSpec((srows, GDN_V_W), lambda b, j: (b * n + j, 0))
    o, s_new = pl.pallas_call(
        _gdn_scan_kernel,
        grid=(batch, n),
        in_specs=[blk(), blk(), blk(), blk(), blk(),
                  pl.BlockSpec((srows // c, 1, LANES), lambda b, j: (b * n + j, 0, 0)),
                  pl.BlockSpec((srows, GDN_V_W), lambda b, j: (b * n + j, nq)),
                  _resident((1, GDN_DV))],
        out_specs=[blk(), pl.BlockSpec((1, GDN_HEADS, GDN_DK, GDN_DV), lambda b, j: (b, 0, 0, 0))],
        out_shape=[jax.ShapeDtypeStruct((t, GDN_V_W), BF16),
                   jax.ShapeDtypeStruct((batch, GDN_HEADS, GDN_DK, GDN_DV), F32)],
        compiler_params=_params(("parallel", "arbitrary")),
        name="gdn_scan",
    )(u, w, qg, kg, at, eg, p_gdn, gdn_norm_w.astype(F32).reshape(1, GDN_DV))
    return o, s_new, conv_new


def _gdn_sample_kernel(x_ref, ab_ref, sc_ref, cw_ref, alog_ref, dtb_ref, nw_ref, s_ref, o_ref, so_ref, sco_ref):
    nb = x_ref.shape[0]
    x = x_ref[:, :CONV_CH]
    acc = x * cw_ref[CONV_W - 1:CONV_W, :]
    for i in range(CONV_W - 1):
        acc = acc + sc_ref[i] * cw_ref[i:i + 1, :]
    for i in range(CONV_W - 2):
        sco_ref[i] = sc_ref[i + 1]
    sco_ref[CONV_W - 2] = x
    u = _silu(acc)
    g_all, beta_all = _gdn_gates(ab_ref[...], alog_ref[...], dtb_ref[...])
    eg_all = jnp.exp(g_all)
    nw = nw_ref[...]
    for h in range(GDN_HEADS):
        sl = slice(h * GDN_DK, (h + 1) * GDN_DK)
        q = _l2norm(u[:, sl]) * (GDN_DK ** -0.5)
        k = _l2norm(u[:, GDN_QK_W + h * GDN_DK:GDN_QK_W + (h + 1) * GDN_DK])
        v = u[:, 2 * GDN_QK_W + h * GDN_DV:2 * GDN_QK_W + (h + 1) * GDN_DV]
        beta = beta_all[:, GDN_HEADS + h:GDN_HEADS + h + 1]
        eg = eg_all[:, h:h + 1]
        qk_dot = jnp.sum(q * k, axis=-1, keepdims=True)
        qt, kt = _columns(q), _columns(k)
        rows = []
        for j in range(nb):
            s = s_ref[j, h]
            kcol = kt[:, j:j + 1]
            ks = jnp.sum(kcol * s, axis=0, keepdims=True)
            qs = jnp.sum(qt[:, j:j + 1] * s, axis=0, keepdims=True)
            ej = eg[j:j + 1]
            v_new = beta[j:j + 1] * (v[j:j + 1] - ej * ks)
            rows.append(ej * qs + qk_dot[j:j + 1] * v_new)
            so_ref[j, h] = s * ej + kcol * v_new
        o = jnp.concatenate(rows, axis=0)
        z = x_ref[:, CONV_CH + h * GDN_DV:CONV_CH + (h + 1) * GDN_DV]
        o_ref[:, sl] = (_rms(o) * nw * _silu(z)).astype(BF16)


def _gdn_sample(p_gdn_s, p_ab_s, state, conv_state, conv_w, a_log, dt_bias, gdn_norm_w):
    ts = p_gdn_s.shape[0]
    sb = SAMPLE_TILE
    alog = jnp.pad(a_log.astype(F32), (0, LANES - GDN_HEADS)).reshape(1, LANES)
    dtb = jnp.pad(dt_bias.astype(F32), (0, LANES - GDN_HEADS)).reshape(1, LANES)
    sc = jnp.swapaxes(conv_state.astype(F32), 0, 1)
    st = pl.BlockSpec((sb, GDN_HEADS, GDN_DK, GDN_DV), lambda i: (i, 0, 0, 0))
    scs = pl.BlockSpec((CONV_W - 1, sb, CONV_CH), lambda i: (0, i, 0))
    o, s_new, sc_new = pl.pallas_call(
        _gdn_sample_kernel,
        grid=(ts // sb,),
        in_specs=[pl.BlockSpec((sb, _GDN_W), lambda i: (i, 0)), pl.BlockSpec((sb, LANES), lambda i: (i, 0)), scs,
                  _resident((CONV_W, CONV_CH)), _resident((1, LANES)), _resident((1, LANES)),
                  _resident((1, GDN_DV)), st],
        out_specs=[pl.BlockSpec((sb, GDN_V_W), lambda i: (i, 0)), st, scs],
        out_shape=[jax.ShapeDtypeStruct((ts, GDN_V_W), BF16), jax.ShapeDtypeStruct(state.shape, F32),
                   jax.ShapeDtypeStruct(sc.shape, F32)],
        compiler_params=_params(("parallel",)),
        name="gdn_sample",
    )(p_gdn_s, p_ab_s, sc, conv_w.astype(F32), alog, dtb, gdn_norm_w.astype(F32).reshape(1, GDN_DV), state)
    return o, s_new, jnp.swapaxes(sc_new, 0, 1)


_NO_EXPERT = -1e30


def _merge_kernel(oa0_ref, ob0_ref, gate0_ref, x0_ref, oa1_ref, ob1_ref, gate1_ref, x1in_ref,
                  wa_ref, wb_ref, wo_ref, nw_ref, wr_ref, br_ref, x1_ref, h2_ref, ti_ref, tw_ref, *, n_first):
    first = pl.program_id(0) < n_first
    pick = lambda a, b: jnp.where(first, a[...], b[...])
    gate = pick(gate0_ref, gate1_ref)
    ya = _dot(pick(oa0_ref, oa1_ref), wa_ref[...])
    yb = _dot(pick(ob0_ref, ob1_ref), wb_ref[...])
    ga = gate[:, :D_MODEL].astype(F32)
    gb = gate[:, D_MODEL:].astype(F32)
    m = _sigmoid(ga) * ya + _sigmoid(gb) * yb
    x1 = pick(x0_ref, x1in_ref) + _dot(m.astype(BF16), wo_ref[...])
    x1_ref[...] = x1
    h2 = _rms(x1) * nw_ref[...]
    h2_ref[...] = h2
    lg = _dot_split(h2, wr_ref[...]) + br_ref[...]
    lane = lax.broadcasted_iota(jnp.int32, lg.shape, 1).astype(F32)
    vals, idxs = [], []
    for _ in range(TOP_K):
        top = jnp.max(lg, axis=-1, keepdims=True)
        idx = jnp.min(jnp.where(lg == top, lane, float(LANES)), axis=-1, keepdims=True)
        vals.append(top)
        idxs.append(idx)
        lg = jnp.where(lane == idx, _NO_EXPERT, lg)
    es = [jnp.exp(v - vals[0]) for v in vals]
    inv_total = 1.0 / functools.reduce(lambda a, b: a + b, es)
    ti = jnp.zeros_like(lg)
    tw = jnp.zeros_like(lg)
    for k in range(TOP_K):
        ti = jnp.where(lane == float(k), idxs[k], ti)
        tw = jnp.where(lane == float(k), es[k] * inv_total, tw)
    ti_ref[...] = ti.astype(jnp.int32)
    tw_ref[...] = tw


def _merge(group0, group1, weights):
    r0, r1 = group0[3].shape[0], group1[3].shape[0]
    tm = _pick_tile(math.gcd(r0, r1), (128, 64, 32, 16, 8))
    n0, total = r0 // tm, r0 + r1
    widths = (RET_V_W, GDN_V_W, 2 * D_MODEL, D_MODEL)
    specs0 = [pl.BlockSpec((tm, n), lambda i: (jnp.minimum(i, n0 - 1), 0)) for n in widths]
    specs1 = [pl.BlockSpec((tm, n), lambda i: (jnp.maximum(i - n0, 0), 0)) for n in widths]
    out = lambda n: pl.BlockSpec((tm, n), lambda i: (i, 0))
    sq = (D_MODEL, D_MODEL)
    return pl.pallas_call(
        functools.partial(_merge_kernel, n_first=n0),
        grid=(total // tm,),
        in_specs=specs0 + specs1 + [_resident(sq), _resident(sq), _resident(sq), _resident((1, D_MODEL)),
                                    _resident((D_MODEL, LANES)), _resident((1, LANES))],
        out_specs=[out(D_MODEL), out(D_MODEL), out(LANES), out(LANES)],
        out_shape=[jax.ShapeDtypeStruct((total, D_MODEL), F32), jax.ShapeDtypeStruct((total, D_MODEL), F32),
                   jax.ShapeDtypeStruct((total, LANES), jnp.int32), jax.ShapeDtypeStruct((total, LANES), F32)],
        compiler_params=_params(("parallel",)),
        name="merge",
    )(*group0, *group1, *weights)


def _merge_weights(w_a, w_b, w_o, ffn_norm_w, w_router, b_router):
    wr = jnp.pad(w_router.astype(F32), ((0, 0), (0, LANES - N_EXPERTS)))
    br = jnp.pad(b_router.astype(F32), (0, LANES - N_EXPERTS), constant_values=_NO_EXPERT).reshape(1, LANES)
    return (w_a.astype(BF16), w_b.astype(BF16), w_o.astype(BF16), ffn_norm_w.astype(F32).reshape(1, D_MODEL), wr, br)


def _route(top_i, n_tokens):
    rows = MOE_ROWS
    n = n_tokens * TOP_K
    flat_e = top_i.reshape(n).astype(jnp.int32)
    bits = max(1, (n - 1).bit_length())
    assert bits + (N_EXPERTS - 1).bit_length() <= 31
    order = lax.sort((flat_e << bits) | jnp.arange(n, dtype=jnp.int32)) & ((1 << bits) - 1)
    counts = jnp.sum((flat_e[:, None] == jnp.arange(N_EXPERTS, dtype=jnp.int32)[None, :]).astype(jnp.int32), axis=0)
    start = jnp.cumsum(counts) - counts
    pcounts = (counts + rows - 1) // rows * rows
    pend = jnp.cumsum(pcounts)
    pstart = pend - pcounts
    nb = -(-n // rows) + N_EXPERTS
    blk = jnp.arange(nb, dtype=jnp.int32)
    block_e = jnp.minimum(jnp.sum((pend[None, :] <= (blk * rows)[:, None]).astype(jnp.int32), axis=1),
                          N_EXPERTS - 1).astype(jnp.int32)
    nb_used = (pend[-1] // rows).astype(jnp.int32).reshape(1)
    within = (blk * rows - pstart[block_e])[:, None] + jnp.arange(rows, dtype=jnp.int32)[None, :]
    valid = jnp.logical_and(within < counts[block_e][:, None], (blk < nb_used[0])[:, None])
    flat = order[jnp.clip(start[block_e][:, None] + within, 0, n - 1)]
    spare = n + (blk % 2)[:, None] * rows + jnp.arange(rows, dtype=jnp.int32)[None, :]
    src = jnp.where(valid, flat // TOP_K, 0)
    dst = jnp.where(valid, flat, spare)
    nxt = jnp.concatenate([src[1:], src[-1:]], axis=0)
    slab = jnp.concatenate([src, dst, nxt], axis=1)
    return block_e, nb_used, slab


def _expert_kernel(be_ref, nbu_ref, slab_ref, h_ref, wgu_ref, bgu_ref, wd_ref, bd_ref, y_ref,
                   idx_ref, xbuf_ref, ybuf_ref, wgu_bf_ref, wd_bf_ref, isem, gsem, ssem):
    rows = MOE_ROWS
    i = pl.program_id(0)
    nbu = nbu_ref[0]
    slot = i % 2
    active = i < nbu

    other = 1 - slot
    part = rows // TOP_K
    n_tok = y_ref.shape[0] - 2 * part
    k_bits = TOP_K.bit_length() - 1

    def slab_copy(blk, sl):
        return pltpu.make_async_copy(slab_ref.at[blk], idx_ref.at[sl], isem.at[sl])

    def gather_row(tok, sl, r):
        return pltpu.make_async_copy(h_ref.at[pl.ds(tok, 1)], xbuf_ref.at[sl, pl.ds(r, 1)], gsem.at[sl])

    def gather_wait(sl):
        pltpu.make_async_copy(h_ref.at[pl.ds(0, rows)], xbuf_ref.at[sl], gsem.at[sl]).wait()

    def scatter_wait(sl):
        pltpu.make_async_copy(ybuf_ref.at[sl], y_ref.at[pl.ds(0, rows), pl.ds(0, D_MODEL)], ssem.at[sl]).wait()

    @pl.when(i == 0)
    def _():
        slab_copy(0, 0).start()
        ybuf_ref[...] = jnp.zeros_like(ybuf_ref)
        for sl in range(2):
            for kk in range(TOP_K):
                pltpu.make_async_copy(
                    ybuf_ref.at[sl, pl.ds(kk * part, part)],
                    y_ref.at[pl.ds(n_tok + sl * part, part), pl.ds(kk * D_MODEL, D_MODEL)], ssem.at[sl]).start()

    @pl.when(active)
    def _():
        slab_copy(i, slot).wait()

        @pl.when(i + 1 < nbu)
        def _():
            slab_copy(i + 1, other).start()

        @pl.when(i == 0)
        def _():
            def body(r, carry):
                gather_row(idx_ref[0, r], 0, r).start()
                return carry
            lax.fori_loop(0, rows, body, 0)

        changed = jnp.logical_or(i == 0, be_ref[i] != be_ref[jnp.maximum(i - 1, 0)])

        @pl.when(changed)
        def _():
            wgu_bf_ref[...] = wgu_ref[0].astype(BF16)
            wd_bf_ref[...] = wd_ref[0].astype(BF16)

        gather_wait(slot)
        scatter_wait(slot)

        xb = xbuf_ref[slot].astype(BF16)
        hb = _dot(xb, wgu_bf_ref[...]) + bgu_ref[0]
        for r in range(rows):
            gather_row(idx_ref[slot, 2 * rows + r], other, r).start()
        glu = jnp.minimum(hb[:, :D_FF], SWIGLU_LIMIT)
        lin = jnp.clip(hb[:, D_FF:], -SWIGLU_LIMIT, SWIGLU_LIMIT)
        act = (glu * _sigmoid(SWIGLU_ALPHA * glu) * (lin + 1.0)).astype(BF16)
        ybuf_ref[slot] = _dot(act, wd_bf_ref[...]) + bd_ref[0]
        for r in range(rows):
            f = idx_ref[slot, rows + r]
            col = pl.multiple_of(jnp.bitwise_and(f, TOP_K - 1) * D_MODEL, D_MODEL)
            pltpu.make_async_copy(
                ybuf_ref.at[slot, pl.ds(r, 1)],
                y_ref.at[pl.ds(lax.shift_right_logical(f, k_bits), 1), pl.ds(col, D_MODEL)],
                ssem.at[slot]).start(priority=1)

        @pl.when(i == nbu - 1)
        def _():
            scatter_wait(slot)
            scatter_wait(other)
            gather_wait(other)


def _experts(h2, block_e, nb_used, slab, w_gate_up, b_gate_up, w_down, b_down):
    t = h2.shape[0]
    rows = MOE_ROWS
    nb = slab.shape[0]
    assert TOP_K & (TOP_K - 1) == 0 and rows % TOP_K == 0
    grid_spec = pltpu.PrefetchScalarGridSpec(
        num_scalar_prefetch=2,
        grid=(nb,),
        in_specs=[pl.BlockSpec(memory_space=pl.ANY),
                  pl.BlockSpec(memory_space=pl.ANY),
                  pl.BlockSpec((1, D_MODEL, 2 * D_FF), lambda i, be, nbu: (be[i], 0, 0)),
                  pl.BlockSpec((1, 1, 2 * D_FF), lambda i, be, nbu: (be[i], 0, 0)),
                  pl.BlockSpec((1, D_FF, D_MODEL), lambda i, be, nbu: (be[i], 0, 0)),
                  pl.BlockSpec((1, 1, D_MODEL), lambda i, be, nbu: (be[i], 0, 0))],
        out_specs=pl.BlockSpec(memory_space=pl.ANY),
        scratch_shapes=[pltpu.SMEM((2, 3 * rows), jnp.int32),
                        pltpu.VMEM((2, rows, D_MODEL), F32),
                        pltpu.VMEM((2, rows, D_MODEL), F32),
                        pltpu.VMEM((D_MODEL, 2 * D_FF), BF16),
                        pltpu.VMEM((D_FF, D_MODEL), BF16),
                        pltpu.SemaphoreType.DMA((2,)),
                        pltpu.SemaphoreType.DMA((2,)),
                        pltpu.SemaphoreType.DMA((2,))])
    return pl.pallas_call(
        _expert_kernel,
        grid_spec=grid_spec,
        out_shape=jax.ShapeDtypeStruct((t + 2 * rows // TOP_K, TOP_K * D_MODEL), F32),
        compiler_params=_params(("arbitrary",)),
        name="experts",
    )(block_e, nb_used, slab, h2, w_gate_up, b_gate_up.reshape(N_EXPERTS, 1, 2 * D_FF), w_down,
      b_down.reshape(N_EXPERTS, 1, D_MODEL))


def _combine_kernel(y_ref, w_ref, x1_ref, nw_ref, o_ref, *, final):
    acc = x1_ref[...]
    for k in range(TOP_K):
        acc = acc + w_ref[:, k:k + 1] * y_ref[:, k * D_MODEL:(k + 1) * D_MODEL]
    o_ref[...] = _rms(acc) * nw_ref[...] if final else acc


def _combine(y, gate, x1, row0, rows, norm_w, final):
    tm = _row_tile(rows, row0, (256, 128, 64, 32, 16, 8))
    off = row0 // tm
    return pl.pallas_call(
        functools.partial(_combine_kernel, final=final),
        grid=(rows // tm,),
        in_specs=[pl.BlockSpec((tm, TOP_K * D_MODEL), lambda i: (off + i, 0)),
                  pl.BlockSpec((tm, LANES), lambda i: (off + i, 0)),
                  pl.BlockSpec((tm, D_MODEL), lambda i: (off + i, 0)), _resident((1, D_MODEL))],
        out_specs=pl.BlockSpec((tm, D_MODEL), lambda i: (i, 0)),
        out_shape=jax.ShapeDtypeStruct((rows, D_MODEL), F32),
        compiler_params=_params(("parallel",)),
        name="combine",
    )(y, gate, x1, norm_w.astype(F32).reshape(1, D_MODEL))


def kernel(x_prompt, x_sample, state_ret, state_gdn, state_conv, attn_norm_w, w_in, conv_w, a_log, dt_bias, gdn_norm_w, w_branch_a, w_branch_b, w_out, ffn_norm_w, w_router, b_router, w_gate_up, b_gate_up, w_down, b_down, final_norm_w):
    bp, lp, d = x_prompt.shape
    bs, ls, _ = x_sample.shape
    assert ls == 1 and d == D_MODEL and lp % RET_CHUNK == 0 and bs % SAMPLE_TILE == 0
    depth = w_in.shape[0]
    tp = bp * lp
    t = tp + bs
    xp, xs = x_prompt.reshape(tp, d).astype(F32), x_sample.reshape(bs, d).astype(F32)
    rp, gp, cp, rs, gs, cs = [], [], [], [], [], []
    for l in range(depth):
        wb = jnp.pad(w_in[l].astype(BF16), ((0, 0), (0, -w_in.shape[2] % LANES)))
        pp_ret, pp_gdn, pp_gate, pp_ab = _inproj(xp, attn_norm_w[l], wb)
        ps_ret, ps_gdn, ps_gate, ps_ab = _inproj(xs, attn_norm_w[l], wb)
        op_ret, s_ret_p = _ret_prompt(pp_ret, bp, lp)
        os_ret, s_ret_s = _ret_sample(ps_ret.astype(F32), state_ret[l].astype(F32))
        op_gdn, s_gdn_p, conv_p = _gdn_prompt(pp_gdn, pp_ab, bp, lp, conv_w[l], a_log[l], dt_bias[l], gdn_norm_w[l])
        os_gdn, s_gdn_s, conv_s = _gdn_sample(ps_gdn.astype(F32), ps_ab, state_gdn[l].astype(F32), state_conv[l],
                                              conv_w[l], a_log[l], dt_bias[l], gdn_norm_w[l])
        mw = _merge_weights(w_branch_a[l], w_branch_b[l], w_out[l], ffn_norm_w[l], w_router[l], b_router[l])
        x1, h2, top_i, gate = _merge((op_ret, op_gdn, pp_gate, xp), (os_ret, os_gdn, ps_gate, xs), mw)
        block_e, nb_used, slab = _route(top_i[:, :TOP_K], t)
        y = _experts(h2, block_e, nb_used, slab, w_gate_up[l], b_gate_up[l], w_down[l], b_down[l])
        last = l == depth - 1
        norm_w = final_norm_w if last else jnp.ones((d,), F32)
        xp = _combine(y, gate, x1, 0, tp, norm_w, last)
        xs = _combine(y, gate, x1, tp, bs, norm_w, last)
        rp.append(s_ret_p); gp.append(s_gdn_p); cp.append(conv_p)
        rs.append(s_ret_s); gs.append(s_gdn_s); cs.append(conv_s)
    y_prompt = xp.reshape(bp, lp, d).astype(x_prompt.dtype)
    y_sample = xs.reshape(bs, ls, d).astype(x_sample.dtype)
    return (y_prompt, y_sample,
            jnp.stack(rp).astype(state_ret.dtype), jnp.stack(gp).astype(state_gdn.dtype),
            jnp.stack(cp).astype(state_conv.dtype),
            jnp.stack(rs).astype(state_ret.dtype), jnp.stack(gs).astype(state_gdn.dtype),
            jnp.stack(cs).astype(state_conv.dtype))
```

```python
import functools
import math

import numpy as np
import jax
import jax.numpy as jnp
from jax import lax
from jax.experimental import pallas as pl
from jax.experimental.pallas import tpu as pltpu

F32 = jnp.float32
BF16 = jnp.bfloat16
HIGHEST = lax.Precision.HIGHEST

D_MODEL = 1024
PAST_LEN = 16384
RET_HEADS, RET_DK, RET_DV = 4, 128, 256
RET_QK_W, RET_V_W = RET_HEADS * RET_DK, RET_HEADS * RET_DV
RET_CHUNK = 128
ROPE_BASE = 10000.0
GDN_HEADS, GDN_DK, GDN_DV = 8, 128, 128
GDN_QK_W, GDN_V_W = GDN_HEADS * GDN_DK, GDN_HEADS * GDN_DV
GDN_CHUNK = 64
CONV_W = 4
CONV_CH = 2 * GDN_QK_W + GDN_V_W
N_EXPERTS = 32
TOP_K = 4
D_FF = D_MODEL
SWIGLU_LIMIT = 7.0
SWIGLU_ALPHA = 1.702
NORM_EPS = 1e-6

_RET_W = 2 * RET_QK_W + 2 * RET_V_W
_GDN_W = CONV_CH + GDN_V_W
_AB_OFF = _RET_W + _GDN_W
_GATE_OFF = _AB_OFF + 2 * GDN_HEADS

LANES = 128
VMEM_LIMIT = 56 * 1024 * 1024
MOE_ROWS = 256
SAMPLE_TILE = 8
RET_STEP_ROWS = 256
GDN_PREP_ROWS = 128
GDN_SCAN_ROWS = 128


def _pick_tile(n, candidates):
    for c in candidates:
        if n % c == 0:
            return c
    raise ValueError(f"no tile in {candidates} divides {n}")


def _params(sem, vmem=VMEM_LIMIT):
    return pltpu.CompilerParams(dimension_semantics=sem, vmem_limit_bytes=vmem)


def _resident(shape):
    nd = len(shape)
    return pl.BlockSpec(shape, lambda *_: (0,) * nd, pipeline_mode=pl.Buffered(1))


def _silu(x):
    return x * (1.0 / (1.0 + jnp.exp(-x)))


def _sigmoid(x):
    return 1.0 / (1.0 + jnp.exp(-x))


def _softplus(x):
    return jnp.maximum(x, 0.0) + jnp.log1p(jnp.exp(-jnp.abs(x)))


def _rms(x):
    return x * lax.rsqrt(jnp.mean(x * x, axis=-1, keepdims=True) + NORM_EPS)


def _dot(a, b):
    return jnp.dot(a, b, preferred_element_type=F32)


def _dot_nt(a, b):
    return lax.dot_general(a, b, (((1,), (1,)), ((), ())), preferred_element_type=F32)


def _dot_tn(a, b):
    return lax.dot_general(a, b, (((0,), (0,)), ((), ())), preferred_element_type=F32)


def _dot_hi(a, b):
    return jnp.dot(a, b, preferred_element_type=F32, precision=HIGHEST)


def _dot_split(a, b):
    a_hi, b_hi = a.astype(BF16), b.astype(BF16)
    a_lo = (a - a_hi.astype(F32)).astype(BF16)
    b_lo = (b - b_hi.astype(F32)).astype(BF16)
    return _dot(a_hi, b_hi) + (_dot(a_hi, b_lo) + _dot(a_lo, b_hi))


def _row_tile(rows, row0, candidates):
    return _pick_tile(math.gcd(rows, row0) if row0 else rows, candidates)


def _inproj_kernel(x_ref, nw_ref, w_ref, oret_ref, ogdn_ref, ogate_ref, oab_ref):
    h = (_rms(x_ref[...]) * nw_ref[...]).astype(BF16)
    oret_ref[...] = _dot(h, w_ref[:, :_RET_W]).astype(BF16)
    ogdn_ref[...] = _dot(h, w_ref[:, _RET_W:_AB_OFF]).astype(BF16)
    tail = _dot(h, w_ref[:, _AB_OFF:])
    oab_ref[...] = tail[:, :LANES]
    ogate_ref[...] = tail[:, _GATE_OFF - _AB_OFF:_GATE_OFF - _AB_OFF + 2 * D_MODEL].astype(BF16)


def _inproj(x, norm_w, wb):
    rows = x.shape[0]
    tm = _pick_tile(rows, (256, 128, 64, 32, 16, 8))
    row = lambda n: pl.BlockSpec((tm, n), lambda i: (i, 0))
    return pl.pallas_call(
        _inproj_kernel,
        grid=(rows // tm,),
        in_specs=[row(D_MODEL), _resident((1, D_MODEL)), _resident(wb.shape)],
        out_specs=[row(_RET_W), row(_GDN_W), row(2 * D_MODEL), row(LANES)],
        out_shape=[jax.ShapeDtypeStruct((rows, _RET_W), BF16), jax.ShapeDtypeStruct((rows, _GDN_W), BF16),
                   jax.ShapeDtypeStruct((rows, 2 * D_MODEL), BF16), jax.ShapeDtypeStruct((rows, LANES), F32)],
        compiler_params=_params(("parallel",)),
        name="inproj",
    )(x, norm_w.reshape(1, D_MODEL), wb)


def _ret_log_gamma():
    return np.log1p(-np.exp2(-5.0 - np.arange(RET_HEADS, dtype=np.float64)))


def _rope_tables(pos):
    half = RET_DK // 2
    inv = 1.0 / (ROPE_BASE ** (jnp.arange(half, dtype=F32) / half))
    ang = pos.astype(F32)[:, None] * inv[None, :]
    cos, sin = jnp.cos(ang), jnp.sin(ang)
    return jnp.concatenate([cos, cos], axis=-1), jnp.concatenate([-sin, sin], axis=-1)


def _rotary(x, cos, sin):
    return x * cos + pltpu.roll(x, RET_DK // 2, 1) * sin


def _ret_prompt_kernel(q_ref, k_ref, v_ref, g_ref, cos_ref, sin_ref, dmask_ref, qdec_ref, kdec_ref,
                       o_ref, s_ref, *, gammas):
    @pl.when(pl.program_id(1) == 0)
    def _():
        s_ref[...] = jnp.zeros_like(s_ref)

    c = RET_CHUNK
    heads = range(RET_HEADS)
    qk = [slice(h * RET_DK, (h + 1) * RET_DK) for h in heads]
    vv = [slice(h * RET_DV, (h + 1) * RET_DV) for h in heads]
    s = [s_ref[0, h] for h in heads]
    for ck in range(q_ref.shape[0] // c):
        r = slice(ck * c, (ck + 1) * c)
        cos, sin = cos_ref[r, :], sin_ref[r, :]
        q = [_rotary(q_ref[r, qk[h]].astype(F32), cos, sin) for h in heads]
        k = [_rotary(k_ref[r, qk[h]].astype(F32), cos, sin) * (RET_DK ** -0.5) for h in heads]
        v = [v_ref[r, vv[h]] for h in heads]
        qb = [x.astype(BF16) for x in q]
        inner = [_dot_nt(qb[h], k[h].astype(BF16)) * dmask_ref[h] for h in heads]
        cross = [_dot((q[h] * qdec_ref[h]).astype(BF16), s[h].astype(BF16)) for h in heads]
        upd = [_dot_tn((k[h] * kdec_ref[h]).astype(BF16), v[h]) for h in heads]
        o = [_dot(inner[h].astype(BF16), v[h]) + cross[h] for h in heads]
        s = [s[h] * gammas[h] + upd[h] for h in heads]
        for h in heads:
            o_ref[r, vv[h]] = (_rms(o[h]) * _silu(g_ref[r, vv[h]].astype(F32))).astype(BF16)
    for h in heads:
        s_ref[0, h] = s[h]


def _ret_prompt(p_ret, batch, seq):
    c = RET_CHUNK
    step = RET_STEP_ROWS if seq % RET_STEP_ROWS == 0 else c
    n = seq // step
    lg = _ret_log_gamma()
    idx = np.arange(c, dtype=np.float64)
    diff = idx[:, None] - idx[None, :]
    dmask = np.where(diff >= 0, np.exp(np.maximum(diff, 0.0)[None] * lg[:, None, None]), 0.0)
    qdec = np.broadcast_to(np.exp((idx + 1.0)[None, :] * lg[:, None])[:, :, None], (RET_HEADS, c, RET_DK))
    kdec = np.broadcast_to(np.exp((c - 1.0 - idx)[None, :] * lg[:, None])[:, :, None], (RET_HEADS, c, RET_DK))
    gammas = tuple(float(g) for g in np.exp(c * lg))
    cos, sin = _rope_tables(jnp.arange(seq, dtype=jnp.int32))
    tab = lambda: _resident((RET_HEADS, c, RET_DK))
    return pl.pallas_call(
        functools.partial(_ret_prompt_kernel, gammas=gammas),
        grid=(batch, n),
        in_specs=[pl.BlockSpec((step, RET_QK_W), lambda b, j: (b * n + j, 0)),
                  pl.BlockSpec((step, RET_QK_W), lambda b, j: (b * n + j, 1)),
                  pl.BlockSpec((step, RET_V_W), lambda b, j: (b * n + j, 1)),
                  pl.BlockSpec((step, RET_V_W), lambda b, j: (b * n + j, 2)),
                  pl.BlockSpec((step, RET_DK), lambda b, j: (j, 0)),
                  pl.BlockSpec((step, RET_DK), lambda b, j: (j, 0)),
                  tab(), tab(), tab()],
        out_specs=[pl.BlockSpec((step, RET_V_W), lambda b, j: (b * n + j, 0)),
                   pl.BlockSpec((1, RET_HEADS, RET_DK, RET_DV), lambda b, j: (b, 0, 0, 0))],
        out_shape=[jax.ShapeDtypeStruct((batch * seq, RET_V_W), BF16),
                   jax.ShapeDtypeStruct((batch, RET_HEADS, RET_DK, RET_DV), F32)],
        compiler_params=_params(("parallel", "arbitrary")),
        name="ret_prompt",
    )(p_ret, p_ret, p_ret, p_ret, cos, sin, jnp.asarray(dmask, F32), jnp.asarray(qdec, F32),
      jnp.asarray(kdec, F32))


def _columns(x):
    n = x.shape[0]
    if n < LANES:
        x = jnp.concatenate([x, jnp.zeros((LANES - n, x.shape[1]), x.dtype)], axis=0)
    return x.T


def _ret_sample_kernel(p_ref, cos_ref, sin_ref, s_ref, o_ref, so_ref, *, gammas):
    cos, sin = cos_ref[...], sin_ref[...]
    nb = p_ref.shape[0]
    for h in range(RET_HEADS):
        qk = slice(h * RET_DK, (h + 1) * RET_DK)
        q = _rotary(p_ref[:, qk], cos, sin)
        k = _rotary(p_ref[:, RET_QK_W + h * RET_DK:RET_QK_W + (h + 1) * RET_DK], cos, sin) * (RET_DK ** -0.5)
        v = p_ref[:, 2 * RET_QK_W + h * RET_DV:2 * RET_QK_W + (h + 1) * RET_DV]
        g = p_ref[:, 2 * RET_QK_W + RET_V_W + h * RET_DV:2 * RET_QK_W + RET_V_W + (h + 1) * RET_DV]
        qk_dot = jnp.sum(q * k, axis=-1, keepdims=True)
        qt, kt = _columns(q), _columns(k)
        rows = []
        for j in range(nb):
            s = s_ref[j, h]
            qs = jnp.sum(qt[:, j:j + 1] * s, axis=0, keepdims=True)
            rows.append(qk_dot[j:j + 1] * v[j:j + 1] + gammas[h] * qs)
            so_ref[j, h] = s * gammas[h] + kt[:, j:j + 1] * v[j:j + 1]
        o = jnp.concatenate(rows, axis=0)
        o_ref[:, h * RET_DV:(h + 1) * RET_DV] = (_rms(o) * _silu(g)).astype(BF16)


def _ret_sample(p_ret_s, state):
    ts = p_ret_s.shape[0]
    sb = SAMPLE_TILE
    gammas = tuple(float(g) for g in np.exp(_ret_log_gamma()))
    cos, sin = _rope_tables(jnp.full((1,), PAST_LEN, jnp.int32))
    st = pl.BlockSpec((sb, RET_HEADS, RET_DK, RET_DV), lambda i: (i, 0, 0, 0))
    return pl.pallas_call(
        functools.partial(_ret_sample_kernel, gammas=gammas),
        grid=(ts // sb,),
        in_specs=[pl.BlockSpec((sb, _RET_W), lambda i: (i, 0)), _resident((1, RET_DK)), _resident((1, RET_DK)), st],
        out_specs=[pl.BlockSpec((sb, RET_V_W), lambda i: (i, 0)), st],
        out_shape=[jax.ShapeDtypeStruct((ts, RET_V_W), BF16), jax.ShapeDtypeStruct(state.shape, F32)],
        compiler_params=_params(("parallel",)),
        name="ret_sample",
    )(p_ret_s, cos, sin, state)


def _l2norm(x):
    return x * lax.rsqrt(jnp.sum(x * x, axis=-1, keepdims=True) + NORM_EPS)


def _bdot(a, b):
    return _dot(a.astype(BF16), b.astype(BF16))


def _chunk_masks(c):
    ri = lax.broadcasted_iota(jnp.int32, (c, c), 0)
    ci = lax.broadcasted_iota(jnp.int32, (c, c), 1)
    eye = (ri == ci).astype(F32)
    diag16 = (ri // 16 == ci // 16).astype(F32)
    low32 = jnp.logical_and(ri // 32 == ci // 32, ri // 16 > ci // 16).astype(F32)
    low64 = (ri // 32 > ci // 32).astype(F32)
    return ri >= ci, ri > ci, eye, diag16, low32, low64


def _unit_lower_inverse(a, eye, diag16, low32, low64):
    many = lambda f, *ls: [f(*args) for args in zip(*ls)]
    n = [-(x * diag16) for x in a]
    n2 = many(_bdot, n, n)
    n3 = many(_bdot, n, n2)
    n4 = many(_bdot, n2, n2)
    n8 = many(_bdot, n4, n4)
    x = [eye + p + q + r for p, q, r in zip(n, n2, n3)]
    x = many(lambda u, v: u + v, x, many(_bdot, x, n4))
    x = many(lambda u, v: u + v, x, many(_bdot, x, n8))
    for mask in (low32, low64):
        r = many(_bdot, [y * mask for y in a], x)
        x = many(lambda u, v: u - v, x, many(_bdot, x, r))
    return x


def _gdn_gates(ab, alog, dtb):
    g = -jnp.exp(alog) * _softplus(ab + dtb)
    return g, _sigmoid(ab)


def _gdn_prep_kernel(x_ref, prev_ref, ab_ref, cw_ref, alog_ref, dtb_ref,
                     u_ref, w_ref, qg_ref, kg_ref, at_ref, eg_ref, cv_ref, xc_ref, act_ref):
    c = GDN_CHUNK
    rows = x_ref.shape[0]
    prev = prev_ref[...].astype(F32)[8:16, :]
    xc_ref[0:8, :] = jnp.where(pl.program_id(1) == 0, 0.0, prev)
    xc_ref[8:8 + rows, :] = x_ref[...].astype(F32)
    acc = xc_ref[5:5 + rows, :] * cw_ref[0:1, :]
    for i in range(1, CONV_W):
        acc = acc + xc_ref[5 + i:5 + i + rows, :] * cw_ref[i:i + 1, :]
    cv_ref[0] = xc_ref[rows + 8 - (CONV_W - 1):rows + 8, :]
    act_ref[...] = _silu(acc)

    g_all, beta_all = _gdn_gates(ab_ref[...], alog_ref[...], dtb_ref[...])
    lower, strict, eye, diag16, low32, low64 = _chunk_masks(c)
    gc_all, gr_all = [], []
    for ck in range(rows // c):
        gc_ck = _dot_hi(lower.astype(F32), g_all[ck * c:(ck + 1) * c])
        gc_all.append(gc_ck)
        gr_all.append(_columns(gc_ck))
        eg_ref[ck] = jnp.exp(gc_ck[c - 1:c, :])
    chains = [(ck, h) for ck in range(rows // c) for h in range(GDN_HEADS)]
    rs = [slice(ck * c, (ck + 1) * c) for ck, _ in chains]
    sl = [slice(h * GDN_DK, (h + 1) * GDN_DK) for _, h in chains]
    nc = range(len(chains))
    q = [_l2norm(act_ref[rs[i], sl[i]]) * (GDN_DK ** -0.5) for i in nc]
    k = [_l2norm(act_ref[rs[i], GDN_QK_W + sl[i].start:GDN_QK_W + sl[i].stop]) for i in nc]
    v = [act_ref[rs[i], 2 * GDN_QK_W + sl[i].start:2 * GDN_QK_W + sl[i].stop] for i in nc]
    beta = [beta_all[rs[i], GDN_HEADS + h:GDN_HEADS + h + 1] for i, (_, h) in enumerate(chains)]
    gc = [gc_all[ck][:, h:h + 1] for ck, h in chains]
    gr = [gr_all[ck][h:h + 1, :c] for ck, h in chains]
    decay = [jnp.exp(jnp.where(lower, gc[i] - gr[i], -jnp.inf)) for i in nc]
    exp_g = [jnp.exp(x) for x in gc]
    kb = [k[i] * beta[i] for i in nc]
    kbf = [x.astype(BF16) for x in k]
    a = [_dot_nt(kb[i].astype(BF16), kbf[i]) * jnp.where(strict, decay[i], 0.0) for i in nc]
    attn = [_dot_nt(q[i].astype(BF16), kbf[i]) * decay[i] for i in nc]
    t = _unit_lower_inverse(a, eye, diag16, low32, low64)
    uu = [_bdot(t[i], v[i] * beta[i]) for i in nc]
    ww = [_bdot(t[i], kb[i] * exp_g[i]) for i in nc]
    for i in nc:
        u_ref[rs[i], sl[i]] = uu[i]
        w_ref[rs[i], sl[i]] = ww[i].astype(BF16)
        at_ref[rs[i], sl[i]] = jnp.concatenate([attn[i], jnp.zeros((c, GDN_DK - c), F32)], axis=1).astype(BF16)
        qg_ref[rs[i], sl[i]] = (q[i] * exp_g[i]).astype(BF16)
        kg_ref[rs[i], sl[i]] = (k[i] * jnp.exp(gc[i][c - 1:c, :] - gc[i])).astype(BF16)


def _gdn_scan_kernel(u_ref, w_ref, qg_ref, kg_ref, at_ref, eg_ref, z_ref, nw_ref, o_ref, s_ref):
    c = GDN_CHUNK

    @pl.when(pl.program_id(1) == 0)
    def _():
        s_ref[...] = jnp.zeros_like(s_ref)

    nw = nw_ref[...]
    heads = range(GDN_HEADS)
    sl = [slice(h * GDN_DK, (h + 1) * GDN_DK) for h in heads]
    s = [s_ref[0, h] for h in heads]
    for ck in range(u_ref.shape[0] // c):
        r = slice(ck * c, (ck + 1) * c)
        eg = eg_ref[ck]
        sb = [x.astype(BF16) for x in s]
        ws = [_dot(w_ref[r, sl[h]], sb[h]) for h in heads]
        qs = [_dot(qg_ref[r, sl[h]], sb[h]) for h in heads]
        vnb = [(u_ref[r, sl[h]] - ws[h]).astype(BF16) for h in heads]
        o = [qs[h] + _dot(at_ref[r, h * GDN_DK:h * GDN_DK + c], vnb[h]) for h in heads]
        s = [s[h] * eg[:, h:h + 1] + _dot_tn(kg_ref[r, sl[h]], vnb[h]) for h in heads]
        for h in heads:
            o_ref[r, sl[h]] = (_rms(o[h]) * nw * _silu(z_ref[r, sl[h]].astype(F32))).astype(BF16)
    for h in heads:
        s_ref[0, h] = s[h]


def _gdn_prompt(p_gdn, p_ab, batch, seq, conv_w, a_log, dt_bias, gdn_norm_w):
    c = GDN_CHUNK
    rows = GDN_PREP_ROWS
    t = batch * seq
    nt = seq // rows
    alog = jnp.pad(a_log.astype(F32), (0, LANES - GDN_HEADS)).reshape(1, LANES)
    dtb = jnp.pad(dt_bias.astype(F32), (0, LANES - GDN_HEADS)).reshape(1, LANES)
    wide = lambda: pl.BlockSpec((rows, GDN_V_W), lambda b, j: (b * nt + j, 0))
    u, w, qg, kg, at, eg, conv_new = pl.pallas_call(
        _gdn_prep_kernel,
        grid=(batch, nt),
        in_specs=[pl.BlockSpec((rows, CONV_CH), lambda b, j: (b * nt + j, 0)),
                  pl.BlockSpec((16, CONV_CH), lambda b, j: (jnp.maximum((b * nt + j) * (rows // 16) - 1, 0), 0)),
                  pl.BlockSpec((rows, LANES), lambda b, j: (b * nt + j, 0)),
                  _resident((CONV_W, CONV_CH)), _resident((1, LANES)), _resident((1, LANES))],
        out_specs=[wide(), wide(), wide(), wide(), wide(),
                   pl.BlockSpec((rows // c, 1, LANES), lambda b, j: (b * nt + j, 0, 0)),
                   pl.BlockSpec((1, CONV_W - 1, CONV_CH), lambda b, j: (b, 0, 0))],
        out_shape=[jax.ShapeDtypeStruct((t, GDN_V_W), F32)] + [jax.ShapeDtypeStruct((t, GDN_V_W), BF16)] * 4
        + [jax.ShapeDtypeStruct((t // c, 1, LANES), F32),
           jax.ShapeDtypeStruct((batch, CONV_W - 1, CONV_CH), F32)],
        scratch_shapes=[pltpu.VMEM((rows + 8, CONV_CH), F32), pltpu.VMEM((rows, CONV_CH), F32)],
        compiler_params=_params(("parallel", "arbitrary")),
        name="gdn_prep",
    )(p_gdn, p_gdn, p_ab, conv_w.astype(F32), alog, dtb)
    srows = GDN_SCAN_ROWS
    n = seq // srows
    nq = CONV_CH // GDN_V_W
    blk = lambda: pl.BlockSpec((srows, GDN_V_W), lambda b, j: (b * n + j, 0))
    o, s_new = pl.pallas_call(
        _gdn_scan_kernel,
        grid=(batch, n),
        in_specs=[blk(), blk(), blk(), blk(), blk(),
                  pl.BlockSpec((srows // c, 1, LANES), lambda b, j: (b * n + j, 0, 0)),
                  pl.BlockSpec((srows, GDN_V_W), lambda b, j: (b * n + j, nq)),
                  _resident((1, GDN_DV))],
        out_specs=[blk(), pl.BlockSpec((1, GDN_HEADS, GDN_DK, GDN_DV), lambda b, j: (b, 0, 0, 0))],
        out_shape=[jax.ShapeDtypeStruct((t, GDN_V_W), BF16),
                   jax.ShapeDtypeStruct((batch, GDN_HEADS, GDN_DK, GDN_DV), F32)],
        compiler_params=_params(("parallel", "arbitrary")),
        name="gdn_scan",
    )(u, w, qg, kg, at, eg, p_gdn, gdn_norm_w.astype(F32).reshape(1, GDN_DV))
    return o, s_new, conv_new


def _gdn_sample_kernel(x_ref, ab_ref, sc_ref, cw_ref, alog_ref, dtb_ref, nw_ref, s_ref, o_ref, so_ref, sco_ref):
    nb = x_ref.shape[0]
    x = x_ref[:, :CONV_CH]
    acc = x * cw_ref[CONV_W - 1:CONV_W, :]
    for i in range(CONV_W - 1):
        acc = acc + sc_ref[i] * cw_ref[i:i + 1, :]
    for i in range(CONV_W - 2):
        sco_ref[i] = sc_ref[i + 1]
    sco_ref[CONV_W - 2] = x
    u = _silu(acc)
    g_all, beta_all = _gdn_gates(ab_ref[...], alog_ref[...], dtb_ref[...])
    eg_all = jnp.exp(g_all)
    nw = nw_ref[...]
    for h in range(GDN_HEADS):
        sl = slice(h * GDN_DK, (h + 1) * GDN_DK)
        q = _l2norm(u[:, sl]) * (GDN_DK ** -0.5)
        k = _l2norm(u[:, GDN_QK_W + h * GDN_DK:GDN_QK_W + (h + 1) * GDN_DK])
        v = u[:, 2 * GDN_QK_W + h * GDN_DV:2 * GDN_QK_W + (h + 1) * GDN_DV]
        beta = beta_all[:, GDN_HEADS + h:GDN_HEADS + h + 1]
        eg = eg_all[:, h:h + 1]
        qk_dot = jnp.sum(q * k, axis=-1, keepdims=True)
        qt, kt = _columns(q), _columns(k)
        rows = []
        for j in range(nb):
            s = s_ref[j, h]
            kcol = kt[:, j:j + 1]
            ks = jnp.sum(kcol * s, axis=0, keepdims=True)
            qs = jnp.sum(qt[:, j:j + 1] * s, axis=0, keepdims=True)
            ej = eg[j:j + 1]
            v_new = beta[j:j + 1] * (v[j:j + 1] - ej * ks)
            rows.append(ej * qs + qk_dot[j:j + 1] * v_new)
            so_ref[j, h] = s * ej + kcol * v_new
        o = jnp.concatenate(rows, axis=0)
        z = x_ref[:, CONV_CH + h * GDN_DV:CONV_CH + (h + 1) * GDN_DV]
        o_ref[:, sl] = (_rms(o) * nw * _silu(z)).astype(BF16)


def _gdn_sample(p_gdn_s, p_ab_s, state, conv_state, conv_w, a_log, dt_bias, gdn_norm_w):
    ts = p_gdn_s.shape[0]
    sb = SAMPLE_TILE
    alog = jnp.pad(a_log.astype(F32), (0, LANES - GDN_HEADS)).reshape(1, LANES)
    dtb = jnp.pad(dt_bias.astype(F32), (0, LANES - GDN_HEADS)).reshape(1, LANES)
    sc = jnp.swapaxes(conv_state.astype(F32), 0, 1)
    st = pl.BlockSpec((sb, GDN_HEADS, GDN_DK, GDN_DV), lambda i: (i, 0, 0, 0))
    scs = pl.BlockSpec((CONV_W - 1, sb, CONV_CH), lambda i: (0, i, 0))
    o, s_new, sc_new = pl.pallas_call(
        _gdn_sample_kernel,
        grid=(ts // sb,),
        in_specs=[pl.BlockSpec((sb, _GDN_W), lambda i: (i, 0)), pl.BlockSpec((sb, LANES), lambda i: (i, 0)), scs,
                  _resident((CONV_W, CONV_CH)), _resident((1, LANES)), _resident((1, LANES)),
                  _resident((1, GDN_DV)), st],
        out_specs=[pl.BlockSpec((sb, GDN_V_W), lambda i: (i, 0)), st, scs],
        out_shape=[jax.ShapeDtypeStruct((ts, GDN_V_W), BF16), jax.ShapeDtypeStruct(state.shape, F32),
                   jax.ShapeDtypeStruct(sc.shape, F32)],
        compiler_params=_params(("parallel",)),
        name="gdn_sample",
    )(p_gdn_s, p_ab_s, sc, conv_w.astype(F32), alog, dtb, gdn_norm_w.astype(F32).reshape(1, GDN_DV), state)
    return o, s_new, jnp.swapaxes(sc_new, 0, 1)


_NO_EXPERT = -1e30


def _merge_kernel(oa0_ref, ob0_ref, gate0_ref, x0_ref, oa1_ref, ob1_ref, gate1_ref, x1in_ref,
                  wa_ref, wb_ref, wo_ref, nw_ref, wr_ref, br_ref, x1_ref, h2_ref, ti_ref, tw_ref, *, n_first):
    first = pl.program_id(0) < n_first
    pick = lambda a, b: jnp.where(first, a[...], b[...])
    gate = pick(gate0_ref, gate1_ref)
    ya = _dot(pick(oa0_ref, oa1_ref), wa_ref[...])
    yb = _dot(pick(ob0_ref, ob1_ref), wb_ref[...])
    ga = gate[:, :D_MODEL].astype(F32)
    gb = gate[:, D_MODEL:].astype(F32)
    m = _sigmoid(ga) * ya + _sigmoid(gb) * yb
    x1 = pick(x0_ref, x1in_ref) + _dot(m.astype(BF16), wo_ref[...])
    x1_ref[...] = x1
    h2 = _rms(x1) * nw_ref[...]
    h2_ref[...] = h2
    lg = _dot_split(h2, wr_ref[...]) + br_ref[...]
    lane = lax.broadcasted_iota(jnp.int32, lg.shape, 1).astype(F32)
    vals, idxs = [], []
    for _ in range(TOP_K):
        top = jnp.max(lg, axis=-1, keepdims=True)
        idx = jnp.min(jnp.where(lg == top, lane, float(LANES)), axis=-1, keepdims=True)
        vals.append(top)
        idxs.append(idx)
        lg = jnp.where(lane == idx, _NO_EXPERT, lg)
    es = [jnp.exp(v - vals[0]) for v in vals]
    inv_total = 1.0 / functools.reduce(lambda a, b: a + b, es)
    ti = jnp.zeros_like(lg)
    tw = jnp.zeros_like(lg)
    for k in range(TOP_K):
        ti = jnp.where(lane == float(k), idxs[k], ti)
        tw = jnp.where(lane == float(k), es[k] * inv_total, tw)
    ti_ref[...] = ti.astype(jnp.int32)
    tw_ref[...] = tw


def _merge(group0, group1, weights):
    r0, r1 = group0[3].shape[0], group1[3].shape[0]
    tm = _pick_tile(math.gcd(r0, r1), (128, 64, 32, 16, 8))
    n0, total = r0 // tm, r0 + r1
    widths = (RET_V_W, GDN_V_W, 2 * D_MODEL, D_MODEL)
    specs0 = [pl.BlockSpec((tm, n), lambda i: (jnp.minimum(i, n0 - 1), 0)) for n in widths]
    specs1 = [pl.BlockSpec((tm, n), lambda i: (jnp.maximum(i - n0, 0), 0)) for n in widths]
    out = lambda n: pl.BlockSpec((tm, n), lambda i: (i, 0))
    sq = (D_MODEL, D_MODEL)
    return pl.pallas_call(
        functools.partial(_merge_kernel, n_first=n0),
        grid=(total // tm,),
        in_specs=specs0 + specs1 + [_resident(sq), _resident(sq), _resident(sq), _resident((1, D_MODEL)),
                                    _resident((D_MODEL, LANES)), _resident((1, LANES))],
        out_specs=[out(D_MODEL), out(D_MODEL), out(LANES), out(LANES)],
        out_shape=[jax.ShapeDtypeStruct((total, D_MODEL), F32), jax.ShapeDtypeStruct((total, D_MODEL), F32),
                   jax.ShapeDtypeStruct((total, LANES), jnp.int32), jax.ShapeDtypeStruct((total, LANES), F32)],
        compiler_params=_params(("parallel",)),
        name="merge",
    )(*group0, *group1, *weights)


def _merge_weights(w_a, w_b, w_o, ffn_norm_w, w_router, b_router):
    wr = jnp.pad(w_router.astype(F32), ((0, 0), (0, LANES - N_EXPERTS)))
    br = jnp.pad(b_router.astype(F32), (0, LANES - N_EXPERTS), constant_values=_NO_EXPERT).reshape(1, LANES)
    return (w_a.astype(BF16), w_b.astype(BF16), w_o.astype(BF16), ffn_norm_w.astype(F32).reshape(1, D_MODEL), wr, br)


def _route(top_i, n_tokens):
    rows = MOE_ROWS
    n = n_tokens * TOP_K
    flat_e = top_i.reshape(n).astype(jnp.int32)
    bits = max(1, (n - 1).bit_length())
    assert bits + (N_EXPERTS - 1).bit_length() <= 31
    order = lax.sort((flat_e << bits) | jnp.arange(n, dtype=jnp.int32)) & ((1 << bits) - 1)
    counts = jnp.sum((flat_e[:, None] == jnp.arange(N_EXPERTS, dtype=jnp.int32)[None, :]).astype(jnp.int32), axis=0)
    start = jnp.cumsum(counts) - counts
    pcounts = (counts + rows - 1) // rows * rows
    pend = jnp.cumsum(pcounts)
    pstart = pend - pcounts
    nb = -(-n // rows) + N_EXPERTS
    blk = jnp.arange(nb, dtype=jnp.int32)
    block_e = jnp.minimum(jnp.sum((pend[None, :] <= (blk * rows)[:, None]).astype(jnp.int32), axis=1),
                          N_EXPERTS - 1).astype(jnp.int32)
    nb_used = (pend[-1] // rows).astype(jnp.int32).reshape(1)
    within = (blk * rows - pstart[block_e])[:, None] + jnp.arange(rows, dtype=jnp.int32)[None, :]
    valid = jnp.logical_and(within < counts[block_e][:, None], (blk < nb_used[0])[:, None])
    flat = order[jnp.clip(start[block_e][:, None] + within, 0, n - 1)]
    spare = n + (blk % 2)[:, None] * rows + jnp.arange(rows, dtype=jnp.int32)[None, :]
    src = jnp.where(valid, flat // TOP_K, 0)
    dst = jnp.where(valid, flat, spare)
    nxt = jnp.concatenate([src[1:], src[-1:]], axis=0)
    slab = jnp.concatenate([src, dst, nxt], axis=1)
    return block_e, nb_used, slab


def _expert_kernel(be_ref, nbu_ref, slab_ref, h_ref, wgu_ref, bgu_ref, wd_ref, bd_ref, y_ref,
                   idx_ref, xbuf_ref, ybuf_ref, wgu_bf_ref, wd_bf_ref, isem, gsem, ssem):
    rows = MOE_ROWS
    i = pl.program_id(0)
    nbu = nbu_ref[0]
    slot = i % 2
    active = i < nbu

    other = 1 - slot
    part = rows // TOP_K
    n_tok = y_ref.shape[0] - 2 * part
    k_bits = TOP_K.bit_length() - 1

    def slab_copy(blk, sl):
        return pltpu.make_async_copy(slab_ref.at[blk], idx_ref.at[sl], isem.at[sl])

    def gather_row(tok, sl, r):
        return pltpu.make_async_copy(h_ref.at[pl.ds(tok, 1)], xbuf_ref.at[sl, pl.ds(r, 1)], gsem.at[sl])

    def gather_wait(sl):
        pltpu.make_async_copy(h_ref.at[pl.ds(0, rows)], xbuf_ref.at[sl], gsem.at[sl]).wait()

    def scatter_wait(sl):
        pltpu.make_async_copy(ybuf_ref.at[sl], y_ref.at[pl.ds(0, rows), pl.ds(0, D_MODEL)], ssem.at[sl]).wait()

    @pl.when(i == 0)
    def _():
        slab_copy(0, 0).start()
        ybuf_ref[...] = jnp.zeros_like(ybuf_ref)
        for sl in range(2):
            for kk in range(TOP_K):
                pltpu.make_async_copy(
                    ybuf_ref.at[sl, pl.ds(kk * part, part)],
                    y_ref.at[pl.ds(n_tok + sl * part, part), pl.ds(kk * D_MODEL, D_MODEL)], ssem.at[sl]).start()

    @pl.when(active)
    def _():
        slab_copy(i, slot).wait()

        @pl.when(i + 1 < nbu)
        def _():
            slab_copy(i + 1, other).start()

        @pl.when(i == 0)
        def _():
            def body(r, carry):
                gather_row(idx_ref[0, r], 0, r).start()
                return carry
            lax.fori_loop(0, rows, body, 0)

        changed = jnp.logical_or(i == 0, be_ref[i] != be_ref[jnp.maximum(i - 1, 0)])

        @pl.when(changed)
        def _():
            wgu_bf_ref[...] = wgu_ref[0].astype(BF16)
            wd_bf_ref[...] = wd_ref[0].astype(BF16)

        def block(sl):
            gather_wait(sl)
            scatter_wait(sl)
            xb = xbuf_ref[sl].astype(BF16)
            hb = _dot(xb, wgu_bf_ref[...]) + bgu_ref[0]
            for r in range(rows):
                gather_row(idx_ref[sl, 2 * rows + r], 1 - sl, r).start()
            glu = jnp.minimum(hb[:, :D_FF], SWIGLU_LIMIT)
            lin = jnp.clip(hb[:, D_FF:], -SWIGLU_LIMIT, SWIGLU_LIMIT)
            act = (glu * _sigmoid(SWIGLU_ALPHA * glu) * (lin + 1.0)).astype(BF16)
            ybuf_ref[sl] = _dot(act, wd_bf_ref[...]) + bd_ref[0]
            for r in range(rows):
                f = idx_ref[sl, rows + r]
                col = pl.multiple_of(jnp.bitwise_and(f, TOP_K - 1) * D_MODEL, D_MODEL)
                pltpu.make_async_copy(
                    ybuf_ref.at[sl, pl.ds(r, 1)],
                    y_ref.at[pl.ds(lax.shift_right_logical(f, k_bits), 1), pl.ds(col, D_MODEL)],
                    ssem.at[sl]).start(priority=1)

        for sl in range(2):
            pl.when(slot == sl)(functools.partial(block, sl))

        @pl.when(i == nbu - 1)
        def _():
            scatter_wait(slot)
            scatter_wait(other)
            gather_wait(other)


def _experts(h2, block_e, nb_used, slab, w_gate_up, b_gate_up, w_down, b_down):
    t = h2.shape[0]
    rows = MOE_ROWS
    nb = slab.shape[0]
    assert TOP_K & (TOP_K - 1) == 0 and rows % TOP_K == 0
    grid_spec = pltpu.PrefetchScalarGridSpec(
        num_scalar_prefetch=2,
        grid=(nb,),
        in_specs=[pl.BlockSpec(memory_space=pl.ANY),
                  pl.BlockSpec(memory_space=pl.ANY),
                  pl.BlockSpec((1, D_MODEL, 2 * D_FF), lambda i, be, nbu: (be[i], 0, 0)),
                  pl.BlockSpec((1, 1, 2 * D_FF), lambda i, be, nbu: (be[i], 0, 0)),
                  pl.BlockSpec((1, D_FF, D_MODEL), lambda i, be, nbu: (be[i], 0, 0)),
                  pl.BlockSpec((1, 1, D_MODEL), lambda i, be, nbu: (be[i], 0, 0))],
        out_specs=pl.BlockSpec(memory_space=pl.ANY),
        scratch_shapes=[pltpu.SMEM((2, 3 * rows), jnp.int32),
                        pltpu.VMEM((2, rows, D_MODEL), F32),
                        pltpu.VMEM((2, rows, D_MODEL), F32),
                        pltpu.VMEM((D_MODEL, 2 * D_FF), BF16),
                        pltpu.VMEM((D_FF, D_MODEL), BF16),
                        pltpu.SemaphoreType.DMA((2,)),
                        pltpu.SemaphoreType.DMA((2,)),
                        pltpu.SemaphoreType.DMA((2,))])
    return pl.pallas_call(
        _expert_kernel,
        grid_spec=grid_spec,
        out_shape=jax.ShapeDtypeStruct((t + 2 * rows // TOP_K, TOP_K * D_MODEL), F32),
        compiler_params=_params(("arbitrary",)),
        name="experts",
    )(block_e, nb_used, slab, h2, w_gate_up, b_gate_up.reshape(N_EXPERTS, 1, 2 * D_FF), w_down,
      b_down.reshape(N_EXPERTS, 1, D_MODEL))


def _combine_kernel(y_ref, w_ref, x1_ref, nw_ref, o_ref, *, final):
    acc = x1_ref[...]
    for k in range(TOP_K):
        acc = acc + w_ref[:, k:k + 1] * y_ref[:, k * D_MODEL:(k + 1) * D_MODEL]
    o_ref[...] = _rms(acc) * nw_ref[...] if final else acc


def _combine(y, gate, x1, row0, rows, norm_w, final):
    tm = _row_tile(rows, row0, (256, 128, 64, 32, 16, 8))
    off = row0 // tm
    return pl.pallas_call(
        functools.partial(_combine_kernel, final=final),
        grid=(rows // tm,),
        in_specs=[pl.BlockSpec((tm, TOP_K * D_MODEL), lambda i: (off + i, 0)),
                  pl.BlockSpec((tm, LANES), lambda i: (off + i, 0)),
                  pl.BlockSpec((tm, D_MODEL), lambda i: (off + i, 0)), _resident((1, D_MODEL))],
        out_specs=pl.BlockSpec((tm, D_MODEL), lambda i: (i, 0)),
        out_shape=jax.ShapeDtypeStruct((rows, D_MODEL), F32),
        compiler_params=_params(("parallel",)),
        name="combine",
    )(y, gate, x1, norm_w.astype(F32).reshape(1, D_MODEL))


def kernel(x_prompt, x_sample, state_ret, state_gdn, state_conv, attn_norm_w, w_in, conv_w, a_log, dt_bias, gdn_norm_w, w_branch_a, w_branch_b, w_out, ffn_norm_w, w_router, b_router, w_gate_up, b_gate_up, w_down, b_down, final_norm_w):
    bp, lp, d = x_prompt.shape
    bs, ls, _ = x_sample.shape
    assert ls == 1 and d == D_MODEL and lp % RET_CHUNK == 0 and bs % SAMPLE_TILE == 0
    depth = w_in.shape[0]
    tp = bp * lp
    t = tp + bs
    xp, xs = x_prompt.reshape(tp, d).astype(F32), x_sample.reshape(bs, d).astype(F32)
    rp, gp, cp, rs, gs, cs = [], [], [], [], [], []
    for l in range(depth):
        wb = jnp.pad(w_in[l].astype(BF16), ((0, 0), (0, -w_in.shape[2] % LANES)))
        pp_ret, pp_gdn, pp_gate, pp_ab = _inproj(xp, attn_norm_w[l], wb)
        ps_ret, ps_gdn, ps_gate, ps_ab = _inproj(xs, attn_norm_w[l], wb)
        op_ret, s_ret_p = _ret_prompt(pp_ret, bp, lp)
        os_ret, s_ret_s = _ret_sample(ps_ret.astype(F32), state_ret[l].astype(F32))
        op_gdn, s_gdn_p, conv_p = _gdn_prompt(pp_gdn, pp_ab, bp, lp, conv_w[l], a_log[l], dt_bias[l], gdn_norm_w[l])
        os_gdn, s_gdn_s, conv_s = _gdn_sample(ps_gdn.astype(F32), ps_ab, state_gdn[l].astype(F32), state_conv[l],
                                              conv_w[l], a_log[l], dt_bias[l], gdn_norm_w[l])
        mw = _merge_weights(w_branch_a[l], w_branch_b[l], w_out[l], ffn_norm_w[l], w_router[l], b_router[l])
        x1, h2, top_i, gate = _merge((op_ret, op_gdn, pp_gate, xp), (os_ret, os_gdn, ps_gate, xs), mw)
        block_e, nb_used, slab = _route(top_i[:, :TOP_K], t)
        y = _experts(h2, block_e, nb_used, slab, w_gate_up[l], b_gate_up[l], w_down[l], b_down[l])
        last = l == depth - 1
        norm_w = final_norm_w if last else jnp.ones((d,), F32)
        xp = _combine(y, gate, x1, 0, tp, norm_w, last)
        xs = _combine(y, gate, x1, tp, bs, norm_w, last)
        rp.append(s_ret_p); gp.append(s_gdn_p); cp.append(conv_p)
        rs.append(s_ret_s); gs.append(s_gdn_s); cs.append(conv_s)
    y_prompt = xp.reshape(bp, lp, d).astype(x_prompt.dtype)
    y_sample = xs.reshape(bs, ls, d).astype(x_sample.dtype)
    return (y_prompt, y_sample,
            jnp.stack(rp).astype(state_ret.dtype), jnp.stack(gp).astype(state_gdn.dtype),
            jnp.stack(cp).astype(state_conv.dtype),
            jnp.stack(rs).astype(state_ret.dtype), jnp.stack(gs).astype(state_gdn.dtype),
            jnp.stack(cs).astype(state_conv.dtype))
```

```python
import functools
import math

import numpy as np
import jax
import jax.numpy as jnp
from jax import lax
from jax.experimental import pallas as pl
from jax.experimental.pallas import tpu as pltpu

F32 = jnp.float32
BF16 = jnp.bfloat16
HIGHEST = lax.Precision.HIGHEST

D_MODEL = 1024
PAST_LEN = 16384
RET_HEADS, RET_DK, RET_DV = 4, 128, 256
RET_QK_W, RET_V_W = RET_HEADS * RET_DK, RET_HEADS * RET_DV
RET_CHUNK = 128
ROPE_BASE = 10000.0
GDN_HEADS, GDN_DK, GDN_DV = 8, 128, 128
GDN_QK_W, GDN_V_W = GDN_HEADS * GDN_DK, GDN_HEADS * GDN_DV
GDN_CHUNK = 64
CONV_W = 4
CONV_CH = 2 * GDN_QK_W + GDN_V_W
N_EXPERTS = 32
TOP_K = 4
D_FF = D_MODEL
SWIGLU_LIMIT = 7.0
SWIGLU_ALPHA = 1.702
NORM_EPS = 1e-6

_RET_W = 2 * RET_QK_W + 2 * RET_V_W
_GDN_W = CONV_CH + GDN_V_W
_AB_OFF = _RET_W + _GDN_W
_GATE_OFF = _AB_OFF + 2 * GDN_HEADS

LANES = 128
VMEM_LIMIT = 56 * 1024 * 1024
MOE_ROWS = 256
SAMPLE_TILE = 8
RET_STEP_ROWS = 256
GDN_PREP_ROWS = 128
GDN_SCAN_ROWS = 128


def _pick_tile(n, candidates):
    for c in candidates:
        if n % c == 0:
            return c
    raise ValueError(f"no tile in {candidates} divides {n}")


def _params(sem, vmem=VMEM_LIMIT):
    return pltpu.CompilerParams(dimension_semantics=sem, vmem_limit_bytes=vmem)


def _resident(shape):
    nd = len(shape)
    return pl.BlockSpec(shape, lambda *_: (0,) * nd, pipeline_mode=pl.Buffered(1))


def _silu(x):
    return x * (1.0 / (1.0 + jnp.exp(-x)))


def _sigmoid(x):
    return 1.0 / (1.0 + jnp.exp(-x))


def _softplus(x):
    return jnp.maximum(x, 0.0) + jnp.log1p(jnp.exp(-jnp.abs(x)))


def _rms(x):
    return x * lax.rsqrt(jnp.mean(x * x, axis=-1, keepdims=True) + NORM_EPS)


def _dot(a, b):
    return jnp.dot(a, b, preferred_element_type=F32)


def _dot_nt(a, b):
    return lax.dot_general(a, b, (((1,), (1,)), ((), ())), preferred_element_type=F32)


def _dot_tn(a, b):
    return lax.dot_general(a, b, (((0,), (0,)), ((), ())), preferred_element_type=F32)


def _dot_hi(a, b):
    return jnp.dot(a, b, preferred_element_type=F32, precision=HIGHEST)


def _dot_split(a, b):
    a_hi, b_hi = a.astype(BF16), b.astype(BF16)
    a_lo = (a - a_hi.astype(F32)).astype(BF16)
    b_lo = (b - b_hi.astype(F32)).astype(BF16)
    return _dot(a_hi, b_hi) + (_dot(a_hi, b_lo) + _dot(a_lo, b_hi))


def _row_tile(rows, row0, candidates):
    return _pick_tile(math.gcd(rows, row0) if row0 else rows, candidates)


def _inproj_kernel(x_ref, nw_ref, w_ref, oret_ref, ogdn_ref, ogate_ref, oab_ref):
    h = (_rms(x_ref[...]) * nw_ref[...]).astype(BF16)
    oret_ref[...] = _dot(h, w_ref[:, :_RET_W]).astype(BF16)
    ogdn_ref[...] = _dot(h, w_ref[:, _RET_W:_AB_OFF]).astype(BF16)
    tail = _dot(h, w_ref[:, _AB_OFF:])
    oab_ref[...] = tail[:, :LANES]
    ogate_ref[...] = tail[:, _GATE_OFF - _AB_OFF:_GATE_OFF - _AB_OFF + 2 * D_MODEL].astype(BF16)


def _inproj(x, norm_w, wb):
    rows = x.shape[0]
    tm = _pick_tile(rows, (256, 128, 64, 32, 16, 8))
    row = lambda n: pl.BlockSpec((tm, n), lambda i: (i, 0))
    return pl.pallas_call(
        _inproj_kernel,
        grid=(rows // tm,),
        in_specs=[row(D_MODEL), _resident((1, D_MODEL)), _resident(wb.shape)],
        out_specs=[row(_RET_W), row(_GDN_W), row(2 * D_MODEL), row(LANES)],
        out_shape=[jax.ShapeDtypeStruct((rows, _RET_W), BF16), jax.ShapeDtypeStruct((rows, _GDN_W), BF16),
                   jax.ShapeDtypeStruct((rows, 2 * D_MODEL), BF16), jax.ShapeDtypeStruct((rows, LANES), F32)],
        compiler_params=_params(("parallel",)),
        name="inproj",
    )(x, norm_w.reshape(1, D_MODEL), wb)


def _ret_log_gamma():
    return np.log1p(-np.exp2(-5.0 - np.arange(RET_HEADS, dtype=np.float64)))


def _rope_tables(pos):
    half = RET_DK // 2
    inv = 1.0 / (ROPE_BASE ** (jnp.arange(half, dtype=F32) / half))
    ang = pos.astype(F32)[:, None] * inv[None, :]
    cos, sin = jnp.cos(ang), jnp.sin(ang)
    return jnp.concatenate([cos, cos], axis=-1), jnp.concatenate([-sin, sin], axis=-1)


def _rotary(x, cos, sin):
    return x * cos + pltpu.roll(x, RET_DK // 2, 1) * sin


def _ret_prompt_kernel(q_ref, k_ref, v_ref, g_ref, cos_ref, sin_ref, dmask_ref, qdec_ref, kdec_ref,
                       o_ref, s_ref, *, gammas):
    @pl.when(pl.program_id(1) == 0)
    def _():
        s_ref[...] = jnp.zeros_like(s_ref)

    c = RET_CHUNK
    heads = range(RET_HEADS)
    qk = [slice(h * RET_DK, (h + 1) * RET_DK) for h in heads]
    vv = [slice(h * RET_DV, (h + 1) * RET_DV) for h in heads]
    s = [s_ref[0, h] for h in heads]
    for ck in range(q_ref.shape[0] // c):
        r = slice(ck * c, (ck + 1) * c)
        cos, sin = cos_ref[r, :], sin_ref[r, :]
        q = [_rotary(q_ref[r, qk[h]].astype(F32), cos, sin) for h in heads]
        k = [_rotary(k_ref[r, qk[h]].astype(F32), cos, sin) * (RET_DK ** -0.5) for h in heads]
        v = [v_ref[r, vv[h]] for h in heads]
        qb = [x.astype(BF16) for x in q]
        inner = [_dot_nt(qb[h], k[h].astype(BF16)) * dmask_ref[h] for h in heads]
        cross = [_dot((q[h] * qdec_ref[h]).astype(BF16), s[h].astype(BF16)) for h in heads]
        upd = [_dot_tn((k[h] * kdec_ref[h]).astype(BF16), v[h]) for h in heads]
        o = [_dot(inner[h].astype(BF16), v[h]) + cross[h] for h in heads]
        s = [s[h] * gammas[h] + upd[h] for h in heads]
        for h in heads:
            o_ref[r, vv[h]] = (_rms(o[h]) * _silu(g_ref[r, vv[h]].astype(F32))).astype(BF16)
    for h in heads:
        s_ref[0, h] = s[h]


def _ret_prompt(p_ret, batch, seq):
    c = RET_CHUNK
    step = RET_STEP_ROWS if seq % RET_STEP_ROWS == 0 else c
    n = seq // step
    lg = _ret_log_gamma()
    idx = np.arange(c, dtype=np.float64)
    diff = idx[:, None] - idx[None, :]
    dmask = np.where(diff >= 0, np.exp(np.maximum(diff, 0.0)[None] * lg[:, None, None]), 0.0)
    qdec = np.broadcast_to(np.exp((idx + 1.0)[None, :] * lg[:, None])[:, :, None], (RET_HEADS, c, RET_DK))
    kdec = np.broadcast_to(np.exp((c - 1.0 - idx)[None, :] * lg[:, None])[:, :, None], (RET_HEADS, c, RET_DK))
    gammas = tuple(float(g) for g in np.exp(c * lg))
    cos, sin = _rope_tables(jnp.arange(seq, dtype=jnp.int32))
    tab = lambda: _resident((RET_HEADS, c, RET_DK))
    return pl.pallas_call(
        functools.partial(_ret_prompt_kernel, gammas=gammas),
        grid=(batch, n),
        in_specs=[pl.BlockSpec((step, RET_QK_W), lambda b, j: (b * n + j, 0)),
                  pl.BlockSpec((step, RET_QK_W), lambda b, j: (b * n + j, 1)),
                  pl.BlockSpec((step, RET_V_W), lambda b, j: (b * n + j, 1)),
                  pl.BlockSpec((step, RET_V_W), lambda b, j: (b * n + j, 2)),
                  pl.BlockSpec((step, RET_DK), lambda b, j: (j, 0)),
                  pl.BlockSpec((step, RET_DK), lambda b, j: (j, 0)),
                  tab(), tab(), tab()],
        out_specs=[pl.BlockSpec((step, RET_V_W), lambda b, j: (b * n + j, 0)),
                   pl.BlockSpec((1, RET_HEADS, RET_DK, RET_DV), lambda b, j: (b, 0, 0, 0))],
        out_shape=[jax.ShapeDtypeStruct((batch * seq, RET_V_W), BF16),
                   jax.ShapeDtypeStruct((batch, RET_HEADS, RET_DK, RET_DV), F32)],
        compiler_params=_params(("parallel", "arbitrary")),
        name="ret_prompt",
    )(p_ret, p_ret, p_ret, p_ret, cos, sin, jnp.asarray(dmask, F32), jnp.asarray(qdec, F32),
      jnp.asarray(kdec, F32))


def _columns(x):
    n = x.shape[0]
    if n < LANES:
        x = jnp.concatenate([x, jnp.zeros((LANES - n, x.shape[1]), x.dtype)], axis=0)
    return x.T


def _ret_sample_kernel(p_ref, cos_ref, sin_ref, s_ref, o_ref, so_ref, *, gammas):
    cos, sin = cos_ref[...], sin_ref[...]
    nb = p_ref.shape[0]
    for h in range(RET_HEADS):
        qk = slice(h * RET_DK, (h + 1) * RET_DK)
        q = _rotary(p_ref[:, qk], cos, sin)
        k = _rotary(p_ref[:, RET_QK_W + h * RET_DK:RET_QK_W + (h + 1) * RET_DK], cos, sin) * (RET_DK ** -0.5)
        v = p_ref[:, 2 * RET_QK_W + h * RET_DV:2 * RET_QK_W + (h + 1) * RET_DV]
        g = p_ref[:, 2 * RET_QK_W + RET_V_W + h * RET_DV:2 * RET_QK_W + RET_V_W + (h + 1) * RET_DV]
        qk_dot = jnp.sum(q * k, axis=-1, keepdims=True)
        qt, kt = _columns(q), _columns(k)
        rows = []
        for j in range(nb):
            s = s_ref[j, h]
            qs = jnp.sum(qt[:, j:j + 1] * s, axis=0, keepdims=True)
            rows.append(qk_dot[j:j + 1] * v[j:j + 1] + gammas[h] * qs)
            so_ref[j, h] = s * gammas[h] + kt[:, j:j + 1] * v[j:j + 1]
        o = jnp.concatenate(rows, axis=0)
        o_ref[:, h * RET_DV:(h + 1) * RET_DV] = (_rms(o) * _silu(g)).astype(BF16)


def _ret_sample(p_ret_s, state):
    ts = p_ret_s.shape[0]
    sb = SAMPLE_TILE
    gammas = tuple(float(g) for g in np.exp(_ret_log_gamma()))
    cos, sin = _rope_tables(jnp.full((1,), PAST_LEN, jnp.int32))
    st = pl.BlockSpec((sb, RET_HEADS, RET_DK, RET_DV), lambda i: (i, 0, 0, 0))
    return pl.pallas_call(
        functools.partial(_ret_sample_kernel, gammas=gammas),
        grid=(ts // sb,),
        in_specs=[pl.BlockSpec((sb, _RET_W), lambda i: (i, 0)), _resident((1, RET_DK)), _resident((1, RET_DK)), st],
        out_specs=[pl.BlockSpec((sb, RET_V_W), lambda i: (i, 0)), st],
        out_shape=[jax.ShapeDtypeStruct((ts, RET_V_W), BF16), jax.ShapeDtypeStruct(state.shape, F32)],
        compiler_params=_params(("parallel",)),
        name="ret_sample",
    )(p_ret_s, cos, sin, state)


def _l2norm(x):
    return x * lax.rsqrt(jnp.sum(x * x, axis=-1, keepdims=True) + NORM_EPS)


def _bdot(a, b):
    return _dot(a.astype(BF16), b.astype(BF16))


def _chunk_masks(c):
    ri = lax.broadcasted_iota(jnp.int32, (c, c), 0)
    ci = lax.broadcasted_iota(jnp.int32, (c, c), 1)
    eye = (ri == ci).astype(F32)
    diag16 = (ri // 16 == ci // 16).astype(F32)
    low32 = jnp.logical_and(ri // 32 == ci // 32, ri // 16 > ci // 16).astype(F32)
    low64 = (ri // 32 > ci // 32).astype(F32)
    return ri >= ci, ri > ci, eye, diag16, low32, low64


def _unit_lower_inverse(a, eye, diag16, low32, low64):
    many = lambda f, *ls: [f(*args) for args in zip(*ls)]
    n = [-(x * diag16) for x in a]
    n2 = many(_bdot, n, n)
    n3 = many(_bdot, n, n2)
    n4 = many(_bdot, n2, n2)
    n8 = many(_bdot, n4, n4)
    x = [eye + p + q + r for p, q, r in zip(n, n2, n3)]
    x = many(lambda u, v: u + v, x, many(_bdot, x, n4))
    x = many(lambda u, v: u + v, x, many(_bdot, x, n8))
    for mask in (low32, low64):
        r = many(_bdot, [y * mask for y in a], x)
        x = many(lambda u, v: u - v, x, many(_bdot, x, r))
    return x


def _gdn_gates(ab, alog, dtb):
    g = -jnp.exp(alog) * _softplus(ab + dtb)
    return g, _sigmoid(ab)


def _gdn_prep_kernel(x_ref, prev_ref, ab_ref, cw_ref, alog_ref, dtb_ref,
                     u_ref, w_ref, qg_ref, kg_ref, at_ref, eg_ref, cv_ref, xc_ref, act_ref):
    c = GDN_CHUNK
    rows = x_ref.shape[0]
    prev = prev_ref[...].astype(F32)[8:16, :]
    xc_ref[0:8, :] = jnp.where(pl.program_id(1) == 0, 0.0, prev)
    xc_ref[8:8 + rows, :] = x_ref[...].astype(F32)
    acc = xc_ref[5:5 + rows, :] * cw_ref[0:1, :]
    for i in range(1, CONV_W):
        acc = acc + xc_ref[5 + i:5 + i + rows, :] * cw_ref[i:i + 1, :]
    cv_ref[0] = xc_ref[rows + 8 - (CONV_W - 1):rows + 8, :]
    act_ref[...] = _silu(acc)

    g_all, beta_all = _gdn_gates(ab_ref[...], alog_ref[...], dtb_ref[...])
    lower, strict, eye, diag16, low32, low64 = _chunk_masks(c)
    gc_all, gr_all = [], []
    for ck in range(rows // c):
        gc_ck = _dot_hi(lower.astype(F32), g_all[ck * c:(ck + 1) * c])
        gc_all.append(gc_ck)
        gr_all.append(_columns(gc_ck))
        eg_ref[ck] = jnp.exp(gc_ck[c - 1:c, :])
    chains = [(ck, h) for ck in range(rows // c) for h in range(GDN_HEADS)]
    rs = [slice(ck * c, (ck + 1) * c) for ck, _ in chains]
    sl = [slice(h * GDN_DK, (h + 1) * GDN_DK) for _, h in chains]
    nc = range(len(chains))
    q = [_l2norm(act_ref[rs[i], sl[i]]) * (GDN_DK ** -0.5) for i in nc]
    k = [_l2norm(act_ref[rs[i], GDN_QK_W + sl[i].start:GDN_QK_W + sl[i].stop]) for i in nc]
    v = [act_ref[rs[i], 2 * GDN_QK_W + sl[i].start:2 * GDN_QK_W + sl[i].stop] for i in nc]
    beta = [beta_all[rs[i], GDN_HEADS + h:GDN_HEADS + h + 1] for i, (_, h) in enumerate(chains)]
    gc = [gc_all[ck][:, h:h + 1] for ck, h in chains]
    gr = [gr_all[ck][h:h + 1, :c] for ck, h in chains]
    decay = [jnp.exp(jnp.where(lower, gc[i] - gr[i], -jnp.inf)) for i in nc]
    exp_g = [jnp.exp(x) for x in gc]
    kb = [k[i] * beta[i] for i in nc]
    kbf = [x.astype(BF16) for x in k]
    a = [_dot_nt(kb[i].astype(BF16), kbf[i]) * jnp.where(strict, decay[i], 0.0) for i in nc]
    attn = [_dot_nt(q[i].astype(BF16), kbf[i]) * decay[i] for i in nc]
    t = _unit_lower_inverse(a, eye, diag16, low32, low64)
    uu = [_bdot(t[i], v[i] * beta[i]) for i in nc]
    ww = [_bdot(t[i], kb[i] * exp_g[i]) for i in nc]
    for i in nc:
        u_ref[rs[i], sl[i]] = uu[i]
        w_ref[rs[i], sl[i]] = ww[i].astype(BF16)
        at_ref[rs[i], sl[i]] = jnp.concatenate([attn[i], jnp.zeros((c, GDN_DK - c), F32)], axis=1).astype(BF16)
        qg_ref[rs[i], sl[i]] = (q[i] * exp_g[i]).astype(BF16)
        kg_ref[rs[i], sl[i]] = (k[i] * jnp.exp(gc[i][c - 1:c, :] - gc[i])).astype(BF16)


def _gdn_scan_kernel(u_ref, w_ref, qg_ref, kg_ref, at_ref, eg_ref, z_ref, nw_ref, o_ref, s_ref):
    c = GDN_CHUNK

    @pl.when(pl.program_id(1) == 0)
    def _():
        s_ref[...] = jnp.zeros_like(s_ref)

    nw = nw_ref[...]
    heads = range(GDN_HEADS)
    sl = [slice(h * GDN_DK, (h + 1) * GDN_DK) for h in heads]
    s = [s_ref[0, h] for h in heads]
    for ck in range(u_ref.shape[0] // c):
        r = slice(ck * c, (ck + 1) * c)
        eg = eg_ref[ck]
        sb = [x.astype(BF16) for x in s]
        ws = [_dot(w_ref[r, sl[h]], sb[h]) for h in heads]
        qs = [_dot(qg_ref[r, sl[h]], sb[h]) for h in heads]
        vnb = [(u_ref[r, sl[h]] - ws[h]).astype(BF16) for h in heads]
        o = [qs[h] + _dot(at_ref[r, h * GDN_DK:h * GDN_DK + c], vnb[h]) for h in heads]
        s = [s[h] * eg[:, h:h + 1] + _dot_tn(kg_ref[r, sl[h]], vnb[h]) for h in heads]
        for h in heads:
            o_ref[r, sl[h]] = (_rms(o[h]) * nw * _silu(z_ref[r, sl[h]].astype(F32))).astype(BF16)
    for h in heads:
        s_ref[0, h] = s[h]


def _gdn_prompt(p_gdn, p_ab, batch, seq, conv_w, a_log, dt_bias, gdn_norm_w):
    c = GDN_CHUNK
    rows = GDN_PREP_ROWS
    t = batch * seq
    nt = seq // rows
    alog = jnp.pad(a_log.astype(F32), (0, LANES - GDN_HEADS)).reshape(1, LANES)
    dtb = jnp.pad(dt_bias.astype(F32), (0, LANES - GDN_HEADS)).reshape(1, LANES)
    wide = lambda: pl.BlockSpec((rows, GDN_V_W), lambda b, j: (b * nt + j, 0))
    u, w, qg, kg, at, eg, conv_new = pl.pallas_call(
        _gdn_prep_kernel,
        grid=(batch, nt),
        in_specs=[pl.BlockSpec((rows, CONV_CH), lambda b, j: (b * nt + j, 0)),
                  pl.BlockSpec((16, CONV_CH), lambda b, j: (jnp.maximum((b * nt + j) * (rows // 16) - 1, 0), 0)),
                  pl.BlockSpec((rows, LANES), lambda b, j: (b * nt + j, 0)),
                  _resident((CONV_W, CONV_CH)), _resident((1, LANES)), _resident((1, LANES))],
        out_specs=[wide(), wide(), wide(), wide(), wide(),
                   pl.BlockSpec((rows // c, 1, LANES), lambda b, j: (b * nt + j, 0, 0)),
                   pl.BlockSpec((1, CONV_W - 1, CONV_CH), lambda b, j: (b, 0, 0))],
        out_shape=[jax.ShapeDtypeStruct((t, GDN_V_W), F32)] + [jax.ShapeDtypeStruct((t, GDN_V_W), BF16)] * 4
        + [jax.ShapeDtypeStruct((t // c, 1, LANES), F32),
           jax.ShapeDtypeStruct((batch, CONV_W - 1, CONV_CH), F32)],
        scratch_shapes=[pltpu.VMEM((rows + 8, CONV_CH), F32), pltpu.VMEM((rows, CONV_CH), F32)],
        compiler_params=_params(("parallel", "arbitrary")),
        name="gdn_prep",
    )(p_gdn, p_gdn, p_ab, conv_w.astype(F32), alog, dtb)
    srows = GDN_SCAN_ROWS
    n = seq // srows
    nq = CONV_CH // GDN_V_W
    blk = lambda: pl.BlockSpec((srows, GDN_V_W), lambda b, j: (b * n + j, 0))
    o, s_new = pl.pallas_call(
        _gdn_scan_kernel,
        grid=(batch, n),
        in_specs=[blk(), blk(), blk(), blk(), blk(),
                  pl.BlockSpec((srows // c, 1, LANES), lambda b, j: (b * n + j, 0, 0)),
                  pl.BlockSpec((srows, GDN_V_W), lambda b, j: (b * n + j, nq)),
                  _resident((1, GDN_DV))],
        out_specs=[blk(), pl.BlockSpec((1, GDN_HEADS, GDN_DK, GDN_DV), lambda b, j: (b, 0, 0, 0))],
        out_shape=[jax.ShapeDtypeStruct((t, GDN_V_W), BF16),
                   jax.ShapeDtypeStruct((batch, GDN_HEADS, GDN_DK, GDN_DV), F32)],
        compiler_params=_params(("parallel", "arbitrary")),
        name="gdn_scan",
    )(u, w, qg, kg, at, eg, p_gdn, gdn_norm_w.astype(F32).reshape(1, GDN_DV))
    return o, s_new, conv_new


def _gdn_sample_kernel(x_ref, ab_ref, sc_ref, cw_ref, alog_ref, dtb_ref, nw_ref, s_ref, o_ref, so_ref, sco_ref):
    nb = x_ref.shape[0]
    x = x_ref[:, :CONV_CH]
    acc = x * cw_ref[CONV_W - 1:CONV_W, :]
    for i in range(CONV_W - 1):
        acc = acc + sc_ref[i] * cw_ref[i:i + 1, :]
    for i in range(CONV_W - 2):
        sco_ref[i] = sc_ref[i + 1]
    sco_ref[CONV_W - 2] = x
    u = _silu(acc)
    g_all, beta_all = _gdn_gates(ab_ref[...], alog_ref[...], dtb_ref[...])
    eg_all = jnp.exp(g_all)
    nw = nw_ref[...]
    for h in range(GDN_HEADS):
        sl = slice(h * GDN_DK, (h + 1) * GDN_DK)
        q = _l2norm(u[:, sl]) * (GDN_DK ** -0.5)
        k = _l2norm(u[:, GDN_QK_W + h * GDN_DK:GDN_QK_W + (h + 1) * GDN_DK])
        v = u[:, 2 * GDN_QK_W + h * GDN_DV:2 * GDN_QK_W + (h + 1) * GDN_DV]
        beta = beta_all[:, GDN_HEADS + h:GDN_HEADS + h + 1]
        eg = eg_all[:, h:h + 1]
        qk_dot = jnp.sum(q * k, axis=-1, keepdims=True)
        qt, kt = _columns(q), _columns(k)
        rows = []
        for j in range(nb):
            s = s_ref[j, h]
            kcol = kt[:, j:j + 1]
            ks = jnp.sum(kcol * s, axis=0, keepdims=True)
            qs = jnp.sum(qt[:, j:j + 1] * s, axis=0, keepdims=True)
            ej = eg[j:j + 1]
            v_new = beta[j:j + 1] * (v[j:j + 1] - ej * ks)
            rows.append(ej * qs + qk_dot[j:j + 1] * v_new)
            so_ref[j, h] = s * ej + kcol * v_new
        o = jnp.concatenate(rows, axis=0)
        z = x_ref[:, CONV_CH + h * GDN_DV:CONV_CH + (h + 1) * GDN_DV]
        o_ref[:, sl] = (_rms(o) * nw * _silu(z)).astype(BF16)


def _gdn_sample(p_gdn_s, p_ab_s, state, conv_state, conv_w, a_log, dt_bias, gdn_norm_w):
    ts = p_gdn_s.shape[0]
    sb = SAMPLE_TILE
    alog = jnp.pad(a_log.astype(F32), (0, LANES - GDN_HEADS)).reshape(1, LANES)
    dtb = jnp.pad(dt_bias.astype(F32), (0, LANES - GDN_HEADS)).reshape(1, LANES)
    sc = jnp.swapaxes(conv_state.astype(F32), 0, 1)
    st = pl.BlockSpec((sb, GDN_HEADS, GDN_DK, GDN_DV), lambda i: (i, 0, 0, 0))
    scs = pl.BlockSpec((CONV_W - 1, sb, CONV_CH), lambda i: (0, i, 0))
    o, s_new, sc_new = pl.pallas_call(
        _gdn_sample_kernel,
        grid=(ts // sb,),
        in_specs=[pl.BlockSpec((sb, _GDN_W), lambda i: (i, 0)), pl.BlockSpec((sb, LANES), lambda i: (i, 0)), scs,
                  _resident((CONV_W, CONV_CH)), _resident((1, LANES)), _resident((1, LANES)),
                  _resident((1, GDN_DV)), st],
        out_specs=[pl.BlockSpec((sb, GDN_V_W), lambda i: (i, 0)), st, scs],
        out_shape=[jax.ShapeDtypeStruct((ts, GDN_V_W), BF16), jax.ShapeDtypeStruct(state.shape, F32),
                   jax.ShapeDtypeStruct(sc.shape, F32)],
        compiler_params=_params(("parallel",)),
        name="gdn_sample",
    )(p_gdn_s, p_ab_s, sc, conv_w.astype(F32), alog, dtb, gdn_norm_w.astype(F32).reshape(1, GDN_DV), state)
    return o, s_new, jnp.swapaxes(sc_new, 0, 1)


_NO_EXPERT = -1e30


def _merge_kernel(oa0_ref, ob0_ref, gate0_ref, x0_ref, oa1_ref, ob1_ref, gate1_ref, x1in_ref,
                  wa_ref, wb_ref, wo_ref, nw_ref, wr_ref, br_ref, x1_ref, h2_ref, ti_ref, tw_ref, *, n_first):
    first = pl.program_id(0) < n_first
    pick = lambda a, b: jnp.where(first, a[...], b[...])
    gate = pick(gate0_ref, gate1_ref)
    ya = _dot(pick(oa0_ref, oa1_ref), wa_ref[...])
    yb = _dot(pick(ob0_ref, ob1_ref), wb_ref[...])
    ga = gate[:, :D_MODEL].astype(F32)
    gb = gate[:, D_MODEL:].astype(F32)
    m = _sigmoid(ga) * ya + _sigmoid(gb) * yb
    x1 = pick(x0_ref, x1in_ref) + _dot(m.astype(BF16), wo_ref[...])
    x1_ref[...] = x1
    h2 = _rms(x1) * nw_ref[...]
    for j in range(D_MODEL // LANES):
        h2_ref[:, j, :] = h2[:, j * LANES:(j + 1) * LANES]
    lg = _dot_split(h2, wr_ref[...]) + br_ref[...]
    lane = lax.broadcasted_iota(jnp.int32, lg.shape, 1).astype(F32)
    vals, idxs = [], []
    for _ in range(TOP_K):
        top = jnp.max(lg, axis=-1, keepdims=True)
        idx = jnp.min(jnp.where(lg == top, lane, float(LANES)), axis=-1, keepdims=True)
        vals.append(top)
        idxs.append(idx)
        lg = jnp.where(lane == idx, _NO_EXPERT, lg)
    es = [jnp.exp(v - vals[0]) for v in vals]
    inv_total = 1.0 / functools.reduce(lambda a, b: a + b, es)
    ti = jnp.zeros_like(lg)
    tw = jnp.zeros_like(lg)
    for k in range(TOP_K):
        ti = jnp.where(lane == float(k), idxs[k], ti)
        tw = jnp.where(lane == float(k), es[k] * inv_total, tw)
    ti_ref[...] = ti.astype(jnp.int32)
    tw_ref[...] = tw


def _merge(group0, group1, weights):
    r0, r1 = group0[3].shape[0], group1[3].shape[0]
    tm = _pick_tile(math.gcd(r0, r1), (128, 64, 32, 16, 8))
    n0, total = r0 // tm, r0 + r1
    widths = (RET_V_W, GDN_V_W, 2 * D_MODEL, D_MODEL)
    specs0 = [pl.BlockSpec((tm, n), lambda i: (jnp.minimum(i, n0 - 1), 0)) for n in widths]
    specs1 = [pl.BlockSpec((tm, n), lambda i: (jnp.maximum(i - n0, 0), 0)) for n in widths]
    out = lambda n: pl.BlockSpec((tm, n), lambda i: (i, 0))
    sub = D_MODEL // LANES
    sq = (D_MODEL, D_MODEL)
    return pl.pallas_call(
        functools.partial(_merge_kernel, n_first=n0),
        grid=(total // tm,),
        in_specs=specs0 + specs1 + [_resident(sq), _resident(sq), _resident(sq), _resident((1, D_MODEL)),
                                    _resident((D_MODEL, LANES)), _resident((1, LANES))],
        out_specs=[out(D_MODEL), pl.BlockSpec((tm, sub, LANES), lambda i: (i, 0, 0)), out(LANES), out(LANES)],
        out_shape=[jax.ShapeDtypeStruct((total, D_MODEL), F32), jax.ShapeDtypeStruct((total, sub, LANES), F32),
                   jax.ShapeDtypeStruct((total, LANES), jnp.int32), jax.ShapeDtypeStruct((total, LANES), F32)],
        compiler_params=_params(("parallel",)),
        name="merge",
    )(*group0, *group1, *weights)


def _merge_weights(w_a, w_b, w_o, ffn_norm_w, w_router, b_router):
    wr = jnp.pad(w_router.astype(F32), ((0, 0), (0, LANES - N_EXPERTS)))
    br = jnp.pad(b_router.astype(F32), (0, LANES - N_EXPERTS), constant_values=_NO_EXPERT).reshape(1, LANES)
    return (w_a.astype(BF16), w_b.astype(BF16), w_o.astype(BF16), ffn_norm_w.astype(F32).reshape(1, D_MODEL), wr, br)


def _route(top_i, gate, n_tokens):
    rows = MOE_ROWS
    n = n_tokens * TOP_K
    flat_e = top_i.reshape(n).astype(jnp.int32)
    bits = max(1, (n - 1).bit_length())
    assert bits + (N_EXPERTS - 1).bit_length() <= 31
    order = lax.sort((flat_e << bits) | jnp.arange(n, dtype=jnp.int32)) & ((1 << bits) - 1)
    counts = jnp.sum((flat_e[:, None] == jnp.arange(N_EXPERTS, dtype=jnp.int32)[None, :]).astype(jnp.int32), axis=0)
    start = jnp.cumsum(counts) - counts
    pcounts = (counts + rows - 1) // rows * rows
    pend = jnp.cumsum(pcounts)
    pstart = pend - pcounts
    nb = -(-n // rows) + N_EXPERTS
    blk = jnp.arange(nb, dtype=jnp.int32)
    block_e = jnp.minimum(jnp.sum((pend[None, :] <= (blk * rows)[:, None]).astype(jnp.int32), axis=1),
                          N_EXPERTS - 1).astype(jnp.int32)
    nb_used = (pend[-1] // rows).astype(jnp.int32).reshape(1)
    within = (blk * rows - pstart[block_e])[:, None] + jnp.arange(rows, dtype=jnp.int32)[None, :]
    valid = jnp.logical_and(within < counts[block_e][:, None], (blk < nb_used[0])[:, None])
    flat = order[jnp.clip(start[block_e][:, None] + within, 0, n - 1)]
    spare = n + (blk % 2)[:, None] * rows + jnp.arange(rows, dtype=jnp.int32)[None, :]
    src = jnp.where(valid, flat // TOP_K, 0)
    dst = jnp.where(valid, flat, spare)
    nxt = jnp.concatenate([src[1:], src[-1:]], axis=0)
    slab = jnp.concatenate([src, dst, nxt], axis=1)
    row_w = jnp.where(valid, gate[:, :TOP_K].reshape(n)[flat], 0.0)
    row_w = jnp.broadcast_to(row_w[:, :, None], (nb, rows, LANES))
    return block_e, nb_used, slab, row_w


def _expert_kernel(be_ref, nbu_ref, slab_ref, h_ref, roww_ref, wgu_ref, bgu_ref, wd_ref, bd_ref, y_ref,
                   idx_ref, xbuf_ref, ybuf_ref, wgu_bf_ref, wd_bf_ref, isem, gsem, ssem):
    rows = MOE_ROWS
    sub = D_MODEL // LANES
    i = pl.program_id(0)
    nbu = nbu_ref[0]
    slot = i % 2
    active = i < nbu
    other = 1 - slot
    n_real = y_ref.shape[0] - 2 * rows

    def slab_copy(blk, sl):
        return pltpu.make_async_copy(slab_ref.at[blk], idx_ref.at[sl], isem.at[sl])

    def gather_row(tok, sl, r):
        return pltpu.make_async_copy(h_ref.at[tok], xbuf_ref.at[sl, :, r, :], gsem.at[sl])

    def scatter_row(sl, r, f):
        return pltpu.make_async_copy(ybuf_ref.at[sl, :, r, :], y_ref.at[f], ssem.at[sl])

    def gather_wait(sl):
        pltpu.make_async_copy(xbuf_ref.at[sl], xbuf_ref.at[sl], gsem.at[sl]).wait()

    def scatter_wait(sl):
        pltpu.make_async_copy(ybuf_ref.at[sl], ybuf_ref.at[sl], ssem.at[sl]).wait()

    @pl.when(i == 0)
    def _():
        slab_copy(0, 0).start()
        ybuf_ref[...] = jnp.zeros_like(ybuf_ref)
        for sl in range(2):
            def fill(r, carry, sl=sl):
                scatter_row(sl, r, n_real + sl * rows + r).start()
                return carry
            lax.fori_loop(0, rows, fill, 0)

    @pl.when(active)
    def _():
        slab_copy(i, slot).wait()

        @pl.when(i + 1 < nbu)
        def _():
            slab_copy(i + 1, other).start()

        @pl.when(i == 0)
        def _():
            def body(r, carry):
                gather_row(idx_ref[0, r], 0, r).start()
                return carry
            lax.fori_loop(0, rows, body, 0)

        changed = jnp.logical_or(i == 0, be_ref[i] != be_ref[jnp.maximum(i - 1, 0)])

        @pl.when(changed)
        def _():
            wgu_bf_ref[...] = wgu_ref[0].astype(BF16)
            wd_bf_ref[...] = wd_ref[0].astype(BF16)

        def block(sl):
            gather_wait(sl)
            scatter_wait(sl)
            xb = jnp.concatenate([xbuf_ref[sl, j] for j in range(sub)], axis=1).astype(BF16)
            hb = _dot(xb, wgu_bf_ref[...]) + bgu_ref[0]
            for r in range(rows):
                gather_row(idx_ref[sl, 2 * rows + r], 1 - sl, r).start()
            glu = jnp.minimum(hb[:, :D_FF], SWIGLU_LIMIT)
            lin = jnp.clip(hb[:, D_FF:], -SWIGLU_LIMIT, SWIGLU_LIMIT)
            act = (glu * _sigmoid(SWIGLU_ALPHA * glu) * (lin + 1.0)).astype(BF16)
            yv = (_dot(act, wd_bf_ref[...]) + bd_ref[0]) * roww_ref[0, :, 0:1]
            for j in range(sub):
                ybuf_ref[sl, j] = yv[:, j * LANES:(j + 1) * LANES]
            for r in range(rows):
                scatter_row(sl, r, idx_ref[sl, rows + r]).start()

        for sl in range(2):
            pl.when(slot == sl)(functools.partial(block, sl))

        @pl.when(i == nbu - 1)
        def _():
            scatter_wait(slot)
            scatter_wait(other)
            gather_wait(other)


def _experts(h2, block_e, nb_used, slab, row_w, w_gate_up, b_gate_up, w_down, b_down):
    t = h2.shape[0]
    rows = MOE_ROWS
    nb = slab.shape[0]
    sub = D_MODEL // LANES
    grid_spec = pltpu.PrefetchScalarGridSpec(
        num_scalar_prefetch=2,
        grid=(nb,),
        in_specs=[pl.BlockSpec(memory_space=pl.ANY),
                  pl.BlockSpec(memory_space=pl.ANY),
                  pl.BlockSpec((1, rows, LANES), lambda i, be, nbu: (i, 0, 0)),
                  pl.BlockSpec((1, D_MODEL, 2 * D_FF), lambda i, be, nbu: (be[i], 0, 0)),
                  pl.BlockSpec((1, 1, 2 * D_FF), lambda i, be, nbu: (be[i], 0, 0)),
                  pl.BlockSpec((1, D_FF, D_MODEL), lambda i, be, nbu: (be[i], 0, 0)),
                  pl.BlockSpec((1, 1, D_MODEL), lambda i, be, nbu: (be[i], 0, 0))],
        out_specs=pl.BlockSpec(memory_space=pl.ANY),
        scratch_shapes=[pltpu.SMEM((2, 3 * rows), jnp.int32),
                        pltpu.VMEM((2, sub, rows, LANES), F32),
                        pltpu.VMEM((2, sub, rows, LANES), F32),
                        pltpu.VMEM((D_MODEL, 2 * D_FF), BF16),
                        pltpu.VMEM((D_FF, D_MODEL), BF16),
                        pltpu.SemaphoreType.DMA((2,)),
                        pltpu.SemaphoreType.DMA((2,)),
                        pltpu.SemaphoreType.DMA((2,))])
    return pl.pallas_call(
        _expert_kernel,
        grid_spec=grid_spec,
        out_shape=jax.ShapeDtypeStruct((t * TOP_K + 2 * rows, sub, LANES), F32),
        compiler_params=_params(("arbitrary",)),
        name="experts",
    )(block_e, nb_used, slab, h2, row_w, w_gate_up, b_gate_up.reshape(N_EXPERTS, 1, 2 * D_FF), w_down,
      b_down.reshape(N_EXPERTS, 1, D_MODEL))


def _combine_kernel(y_ref, x1_ref, nw_ref, o_ref, sum_ref, *, final):
    total = y_ref[:, 0]
    for k in range(1, TOP_K):
        total = total + y_ref[:, k]
    sum_ref[...] = total
    acc = x1_ref[...] + jnp.concatenate([sum_ref[:, j, :] for j in range(D_MODEL // LANES)], axis=1)
    o_ref[...] = _rms(acc) * nw_ref[...] if final else acc


def _combine(y, x1, row0, rows, norm_w, final):
    tm = _row_tile(rows, row0, (256, 128, 64, 32, 16, 8))
    off = row0 // tm
    sub = D_MODEL // LANES
    y4 = y.reshape(y.shape[0] // TOP_K, TOP_K, sub, LANES)
    return pl.pallas_call(
        functools.partial(_combine_kernel, final=final),
        grid=(rows // tm,),
        in_specs=[pl.BlockSpec((tm, TOP_K, sub, LANES), lambda i: (off + i, 0, 0, 0)),
                  pl.BlockSpec((tm, D_MODEL), lambda i: (off + i, 0)), _resident((1, D_MODEL))],
        out_specs=pl.BlockSpec((tm, D_MODEL), lambda i: (i, 0)),
        out_shape=jax.ShapeDtypeStruct((rows, D_MODEL), F32),
        scratch_shapes=[pltpu.VMEM((tm, sub, LANES), F32)],
        compiler_params=_params(("parallel",)),
        name="combine",
    )(y4, x1, norm_w.astype(F32).reshape(1, D_MODEL))


def kernel(x_prompt, x_sample, state_ret, state_gdn, state_conv, attn_norm_w, w_in, conv_w, a_log, dt_bias, gdn_norm_w, w_branch_a, w_branch_b, w_out, ffn_norm_w, w_router, b_router, w_gate_up, b_gate_up, w_down, b_down, final_norm_w):
    bp, lp, d = x_prompt.shape
    bs, ls, _ = x_sample.shape
    assert ls == 1 and d == D_MODEL and lp % RET_CHUNK == 0 and bs % SAMPLE_TILE == 0
    depth = w_in.shape[0]
    tp = bp * lp
    t = tp + bs
    xp, xs = x_prompt.reshape(tp, d).astype(F32), x_sample.reshape(bs, d).astype(F32)
    rp, gp, cp, rs, gs, cs = [], [], [], [], [], []
    for l in range(depth):
        wb = jnp.pad(w_in[l].astype(BF16), ((0, 0), (0, -w_in.shape[2] % LANES)))
        pp_ret, pp_gdn, pp_gate, pp_ab = _inproj(xp, attn_norm_w[l], wb)
        ps_ret, ps_gdn, ps_gate, ps_ab = _inproj(xs, attn_norm_w[l], wb)
        op_ret, s_ret_p = _ret_prompt(pp_ret, bp, lp)
        os_ret, s_ret_s = _ret_sample(ps_ret.astype(F32), state_ret[l].astype(F32))
        op_gdn, s_gdn_p, conv_p = _gdn_prompt(pp_gdn, pp_ab, bp, lp, conv_w[l], a_log[l], dt_bias[l], gdn_norm_w[l])
        os_gdn, s_gdn_s, conv_s = _gdn_sample(ps_gdn.astype(F32), ps_ab, state_gdn[l].astype(F32), state_conv[l],
                                              conv_w[l], a_log[l], dt_bias[l], gdn_norm_w[l])
        mw = _merge_weights(w_branch_a[l], w_branch_b[l], w_out[l], ffn_norm_w[l], w_router[l], b_router[l])
        x1, h2, top_i, gate = _merge((op_ret, op_gdn, pp_gate, xp), (os_ret, os_gdn, ps_gate, xs), mw)
        block_e, nb_used, slab, row_w = _route(top_i[:, :TOP_K], gate, t)
        y = _experts(h2, block_e, nb_used, slab, row_w, w_gate_up[l], b_gate_up[l], w_down[l], b_down[l])
        last = l == depth - 1
        norm_w = final_norm_w if last else jnp.ones((d,), F32)
        xp = _combine(y, x1, 0, tp, norm_w, last)
        xs = _combine(y, x1, tp, bs, norm_w, last)
        rp.append(s_ret_p); gp.append(s_gdn_p); cp.append(conv_p)
        rs.append(s_ret_s); gs.append(s_gdn_s); cs.append(conv_s)
    y_prompt = xp.reshape(bp, lp, d).astype(x_prompt.dtype)
    y_sample = xs.reshape(bs, ls, d).astype(x_sample.dtype)
    return (y_prompt, y_sample,
            jnp.stack(rp).astype(state_ret.dtype), jnp.stack(gp).astype(state_gdn.dtype),
            jnp.stack(cp).astype(state_conv.dtype),
            jnp.stack(rs).astype(state_ret.dtype), jnp.stack(gs).astype(state_gdn.dtype),
            jnp.stack(cs).astype(state_conv.dtype))
```

```python
import functools
import math

import numpy as np
import jax
import jax.numpy as jnp
from jax import lax
from jax.experimental import pallas as pl
from jax.experimental.pallas import tpu as pltpu

F32 = jnp.float32
BF16 = jnp.bfloat16
HIGHEST = lax.Precision.HIGHEST

D_MODEL = 1024
PAST_LEN = 16384
RET_HEADS, RET_DK, RET_DV = 4, 128, 256
RET_QK_W, RET_V_W = RET_HEADS * RET_DK, RET_HEADS * RET_DV
RET_CHUNK = 128
ROPE_BASE = 10000.0
GDN_HEADS, GDN_DK, GDN_DV = 8, 128, 128
GDN_QK_W, GDN_V_W = GDN_HEADS * GDN_DK, GDN_HEADS * GDN_DV
GDN_CHUNK = 64
CONV_W = 4
CONV_CH = 2 * GDN_QK_W + GDN_V_W
N_EXPERTS = 32
TOP_K = 4
D_FF = D_MODEL
SWIGLU_LIMIT = 7.0
SWIGLU_ALPHA = 1.702
NORM_EPS = 1e-6

_RET_W = 2 * RET_QK_W + 2 * RET_V_W
_GDN_W = CONV_CH + GDN_V_W
_AB_OFF = _RET_W + _GDN_W
_GATE_OFF = _AB_OFF + 2 * GDN_HEADS

LANES = 128
VMEM_LIMIT = 56 * 1024 * 1024
MOE_ROWS = 256
SAMPLE_TILE = 8
RET_STEP_ROWS = 256
GDN_PREP_ROWS = 128
GDN_SCAN_ROWS = 128


def _pick_tile(n, candidates):
    for c in candidates:
        if n % c == 0:
            return c
    raise ValueError(f"no tile in {candidates} divides {n}")


def _params(sem, vmem=VMEM_LIMIT):
    return pltpu.CompilerParams(dimension_semantics=sem, vmem_limit_bytes=vmem)


def _resident(shape):
    nd = len(shape)
    return pl.BlockSpec(shape, lambda *_: (0,) * nd, pipeline_mode=pl.Buffered(1))


def _silu(x):
    return x * (1.0 / (1.0 + jnp.exp(-x)))


def _sigmoid(x):
    return 1.0 / (1.0 + jnp.exp(-x))


def _softplus(x):
    return jnp.maximum(x, 0.0) + jnp.log1p(jnp.exp(-jnp.abs(x)))


def _rms(x):
    return x * lax.rsqrt(jnp.mean(x * x, axis=-1, keepdims=True) + NORM_EPS)


def _dot(a, b):
    return jnp.dot(a, b, preferred_element_type=F32)


def _dot_nt(a, b):
    return lax.dot_general(a, b, (((1,), (1,)), ((), ())), preferred_element_type=F32)


def _dot_tn(a, b):
    return lax.dot_general(a, b, (((0,), (0,)), ((), ())), preferred_element_type=F32)


def _dot_hi(a, b):
    return jnp.dot(a, b, preferred_element_type=F32, precision=HIGHEST)


def _dot_split(a, b):
    a_hi, b_hi = a.astype(BF16), b.astype(BF16)
    a_lo = (a - a_hi.astype(F32)).astype(BF16)
    b_lo = (b - b_hi.astype(F32)).astype(BF16)
    return _dot(a_hi, b_hi) + (_dot(a_hi, b_lo) + _dot(a_lo, b_hi))


def _row_tile(rows, row0, candidates):
    return _pick_tile(math.gcd(rows, row0) if row0 else rows, candidates)


def _inproj_kernel(x_ref, nw_ref, w_ref, oret_ref, ogdn_ref, ogate_ref, oab_ref):
    h = (_rms(x_ref[...]) * nw_ref[...]).astype(BF16)
    oret_ref[...] = _dot(h, w_ref[:, :_RET_W]).astype(BF16)
    ogdn_ref[...] = _dot(h, w_ref[:, _RET_W:_AB_OFF]).astype(BF16)
    tail = _dot(h, w_ref[:, _AB_OFF:])
    oab_ref[...] = tail[:, :LANES]
    ogate_ref[...] = tail[:, _GATE_OFF - _AB_OFF:_GATE_OFF - _AB_OFF + 2 * D_MODEL].astype(BF16)


def _inproj(x, norm_w, wb):
    rows = x.shape[0]
    tm = _pick_tile(rows, (256, 128, 64, 32, 16, 8))
    row = lambda n: pl.BlockSpec((tm, n), lambda i: (i, 0))
    return pl.pallas_call(
        _inproj_kernel,
        grid=(rows // tm,),
        in_specs=[row(D_MODEL), _resident((1, D_MODEL)), _resident(wb.shape)],
        out_specs=[row(_RET_W), row(_GDN_W), row(2 * D_MODEL), row(LANES)],
        out_shape=[jax.ShapeDtypeStruct((rows, _RET_W), BF16), jax.ShapeDtypeStruct((rows, _GDN_W), BF16),
                   jax.ShapeDtypeStruct((rows, 2 * D_MODEL), BF16), jax.ShapeDtypeStruct((rows, LANES), F32)],
        compiler_params=_params(("parallel",)),
        name="inproj",
    )(x, norm_w.reshape(1, D_MODEL), wb)


def _ret_log_gamma():
    return np.log1p(-np.exp2(-5.0 - np.arange(RET_HEADS, dtype=np.float64)))


def _rope_tables(pos):
    half = RET_DK // 2
    inv = 1.0 / (ROPE_BASE ** (jnp.arange(half, dtype=F32) / half))
    ang = pos.astype(F32)[:, None] * inv[None, :]
    cos, sin = jnp.cos(ang), jnp.sin(ang)
    return jnp.concatenate([cos, cos], axis=-1), jnp.concatenate([-sin, sin], axis=-1)


def _rotary(x, cos, sin):
    return x * cos + pltpu.roll(x, RET_DK // 2, 1) * sin


def _ret_prompt_kernel(q_ref, k_ref, v_ref, g_ref, cos_ref, sin_ref, dmask_ref, qdec_ref, kdec_ref,
                       o_ref, s_ref, *, gammas):
    @pl.when(pl.program_id(1) == 0)
    def _():
        s_ref[...] = jnp.zeros_like(s_ref)

    c = RET_CHUNK
    heads = range(RET_HEADS)
    qk = [slice(h * RET_DK, (h + 1) * RET_DK) for h in heads]
    vv = [slice(h * RET_DV, (h + 1) * RET_DV) for h in heads]
    s = [s_ref[0, h] for h in heads]
    for ck in range(q_ref.shape[0] // c):
        r = slice(ck * c, (ck + 1) * c)
        cos, sin = cos_ref[r, :], sin_ref[r, :]
        q = [_rotary(q_ref[r, qk[h]].astype(F32), cos, sin) for h in heads]
        k = [_rotary(k_ref[r, qk[h]].astype(F32), cos, sin) * (RET_DK ** -0.5) for h in heads]
        v = [v_ref[r, vv[h]] for h in heads]
        qb = [x.astype(BF16) for x in q]
        inner = [_dot_nt(qb[h], k[h].astype(BF16)) * dmask_ref[h] for h in heads]
        cross = [_dot((q[h] * qdec_ref[h]).astype(BF16), s[h].astype(BF16)) for h in heads]
        upd = [_dot_tn((k[h] * kdec_ref[h]).astype(BF16), v[h]) for h in heads]
        o = [_dot(inner[h].astype(BF16), v[h]) + cross[h] for h in heads]
        s = [s[h] * gammas[h] + upd[h] for h in heads]
        for h in heads:
            o_ref[r, vv[h]] = (_rms(o[h]) * _silu(g_ref[r, vv[h]].astype(F32))).astype(BF16)
    for h in heads:
        s_ref[0, h] = s[h]


def _ret_prompt(p_ret, batch, seq):
    c = RET_CHUNK
    step = RET_STEP_ROWS if seq % RET_STEP_ROWS == 0 else c
    n = seq // step
    lg = _ret_log_gamma()
    idx = np.arange(c, dtype=np.float64)
    diff = idx[:, None] - idx[None, :]
    dmask = np.where(diff >= 0, np.exp(np.maximum(diff, 0.0)[None] * lg[:, None, None]), 0.0)
    qdec = np.broadcast_to(np.exp((idx + 1.0)[None, :] * lg[:, None])[:, :, None], (RET_HEADS, c, RET_DK))
    kdec = np.broadcast_to(np.exp((c - 1.0 - idx)[None, :] * lg[:, None])[:, :, None], (RET_HEADS, c, RET_DK))
    gammas = tuple(float(g) for g in np.exp(c * lg))
    cos, sin = _rope_tables(jnp.arange(seq, dtype=jnp.int32))
    tab = lambda: _resident((RET_HEADS, c, RET_DK))
    return pl.pallas_call(
        functools.partial(_ret_prompt_kernel, gammas=gammas),
        grid=(batch, n),
        in_specs=[pl.BlockSpec((step, RET_QK_W), lambda b, j: (b * n + j, 0)),
                  pl.BlockSpec((step, RET_QK_W), lambda b, j: (b * n + j, 1)),
                  pl.BlockSpec((step, RET_V_W), lambda b, j: (b * n + j, 1)),
                  pl.BlockSpec((step, RET_V_W), lambda b, j: (b * n + j, 2)),
                  pl.BlockSpec((step, RET_DK), lambda b, j: (j, 0)),
                  pl.BlockSpec((step, RET_DK), lambda b, j: (j, 0)),
                  tab(), tab(), tab()],
        out_specs=[pl.BlockSpec((step, RET_V_W), lambda b, j: (b * n + j, 0)),
                   pl.BlockSpec((1, RET_HEADS, RET_DK, RET_DV), lambda b, j: (b, 0, 0, 0))],
        out_shape=[jax.ShapeDtypeStruct((batch * seq, RET_V_W), BF16),
                   jax.ShapeDtypeStruct((batch, RET_HEADS, RET_DK, RET_DV), F32)],
        compiler_params=_params(("parallel", "arbitrary")),
        name="ret_prompt",
    )(p_ret, p_ret, p_ret, p_ret, cos, sin, jnp.asarray(dmask, F32), jnp.asarray(qdec, F32),
      jnp.asarray(kdec, F32))


def _columns(x):
    n = x.shape[0]
    if n < LANES:
        x = jnp.concatenate([x, jnp.zeros((LANES - n, x.shape[1]), x.dtype)], axis=0)
    return x.T


def _ret_sample_kernel(p_ref, cos_ref, sin_ref, s_ref, o_ref, so_ref, *, gammas):
    cos, sin = cos_ref[...], sin_ref[...]
    nb = p_ref.shape[0]
    for h in range(RET_HEADS):
        qk = slice(h * RET_DK, (h + 1) * RET_DK)
        q = _rotary(p_ref[:, qk], cos, sin)
        k = _rotary(p_ref[:, RET_QK_W + h * RET_DK:RET_QK_W + (h + 1) * RET_DK], cos, sin) * (RET_DK ** -0.5)
        v = p_ref[:, 2 * RET_QK_W + h * RET_DV:2 * RET_QK_W + (h + 1) * RET_DV]
        g = p_ref[:, 2 * RET_QK_W + RET_V_W + h * RET_DV:2 * RET_QK_W + RET_V_W + (h + 1) * RET_DV]
        qk_dot = jnp.sum(q * k, axis=-1, keepdims=True)
        qt, kt = _columns(q), _columns(k)
        rows = []
        for j in range(nb):
            s = s_ref[j, h]
            qs = jnp.sum(qt[:, j:j + 1] * s, axis=0, keepdims=True)
            rows.append(qk_dot[j:j + 1] * v[j:j + 1] + gammas[h] * qs)
            so_ref[j, h] = s * gammas[h] + kt[:, j:j + 1] * v[j:j + 1]
        o = jnp.concatenate(rows, axis=0)
        o_ref[:, h * RET_DV:(h + 1) * RET_DV] = (_rms(o) * _silu(g)).astype(BF16)


def _ret_sample(p_ret_s, state):
    ts = p_ret_s.shape[0]
    sb = SAMPLE_TILE
    gammas = tuple(float(g) for g in np.exp(_ret_log_gamma()))
    cos, sin = _rope_tables(jnp.full((1,), PAST_LEN, jnp.int32))
    st = pl.BlockSpec((sb, RET_HEADS, RET_DK, RET_DV), lambda i: (i, 0, 0, 0))
    return pl.pallas_call(
        functools.partial(_ret_sample_kernel, gammas=gammas),
        grid=(ts // sb,),
        in_specs=[pl.BlockSpec((sb, _RET_W), lambda i: (i, 0)), _resident((1, RET_DK)), _resident((1, RET_DK)), st],
        out_specs=[pl.BlockSpec((sb, RET_V_W), lambda i: (i, 0)), st],
        out_shape=[jax.ShapeDtypeStruct((ts, RET_V_W), BF16), jax.ShapeDtypeStruct(state.shape, F32)],
        compiler_params=_params(("parallel",)),
        name="ret_sample",
    )(p_ret_s, cos, sin, state)


def _l2norm(x):
    return x * lax.rsqrt(jnp.sum(x * x, axis=-1, keepdims=True) + NORM_EPS)


def _bdot(a, b):
    return _dot(a.astype(BF16), b.astype(BF16))


def _chunk_masks(c):
    ri = lax.broadcasted_iota(jnp.int32, (c, c), 0)
    ci = lax.broadcasted_iota(jnp.int32, (c, c), 1)
    eye = (ri == ci).astype(F32)
    diag16 = (ri // 16 == ci // 16).astype(F32)
    low32 = jnp.logical_and(ri // 32 == ci // 32, ri // 16 > ci // 16).astype(F32)
    low64 = (ri // 32 > ci // 32).astype(F32)
    return ri >= ci, ri > ci, eye, diag16, low32, low64


def _unit_lower_inverse(a, eye, diag16, low32, low64):
    many = lambda f, *ls: [f(*args) for args in zip(*ls)]
    n = [-(x * diag16) for x in a]
    n2 = many(_bdot, n, n)
    n3 = many(_bdot, n, n2)
    n4 = many(_bdot, n2, n2)
    n8 = many(_bdot, n4, n4)
    x = [eye + p + q + r for p, q, r in zip(n, n2, n3)]
    x = many(lambda u, v: u + v, x, many(_bdot, x, n4))
    x = many(lambda u, v: u + v, x, many(_bdot, x, n8))
    for mask in (low32, low64):
        r = many(_bdot, [y * mask for y in a], x)
        x = many(lambda u, v: u - v, x, many(_bdot, x, r))
    return x


def _gdn_gates(ab, alog, dtb):
    g = -jnp.exp(alog) * _softplus(ab + dtb)
    return g, _sigmoid(ab)


def _gdn_prep_kernel(x_ref, prev_ref, ab_ref, cw_ref, alog_ref, dtb_ref,
                     u_ref, w_ref, qg_ref, kg_ref, at_ref, eg_ref, cv_ref, act_ref):
    c = GDN_CHUNK
    rows = x_ref.shape[0]
    x = x_ref[...]
    halo = prev_ref.shape[0]
    prev = jnp.where(pl.program_id(1) == 0, jnp.zeros_like(prev_ref), prev_ref[...])
    xcat = jnp.concatenate([prev, x], axis=0)
    ti = lax.broadcasted_iota(jnp.int32, (rows, rows + halo), 0)
    ui = lax.broadcasted_iota(jnp.int32, (rows, rows + halo), 1)
    xf = x.astype(F32)
    acc = xf * cw_ref[CONV_W - 1:CONV_W, :]
    for i in range(CONV_W - 1):
        shift = (ui == ti + (halo - (CONV_W - 1) + i)).astype(BF16)
        acc = acc + _dot(shift, xcat) * cw_ref[i:i + 1, :]
    cv_ref[0] = xf[rows - (CONV_W - 1):, :]
    act_ref[...] = _silu(acc)

    g_all, beta_all = _gdn_gates(ab_ref[...], alog_ref[...], dtb_ref[...])
    lower, strict, eye, diag16, low32, low64 = _chunk_masks(c)
    gc_all, gr_all = [], []
    for ck in range(rows // c):
        gc_ck = _dot_hi(lower.astype(F32), g_all[ck * c:(ck + 1) * c])
        gc_all.append(gc_ck)
        gr_all.append(_columns(gc_ck))
        eg_ref[ck] = jnp.exp(gc_ck[c - 1:c, :])
    chains = [(ck, h) for ck in range(rows // c) for h in range(GDN_HEADS)]
    rs = [slice(ck * c, (ck + 1) * c) for ck, _ in chains]
    sl = [slice(h * GDN_DK, (h + 1) * GDN_DK) for _, h in chains]
    nc = range(len(chains))
    q = [_l2norm(act_ref[rs[i], sl[i]]) * (GDN_DK ** -0.5) for i in nc]
    k = [_l2norm(act_ref[rs[i], GDN_QK_W + sl[i].start:GDN_QK_W + sl[i].stop]) for i in nc]
    v = [act_ref[rs[i], 2 * GDN_QK_W + sl[i].start:2 * GDN_QK_W + sl[i].stop] for i in nc]
    beta = [beta_all[rs[i], GDN_HEADS + h:GDN_HEADS + h + 1] for i, (_, h) in enumerate(chains)]
    gc = [gc_all[ck][:, h:h + 1] for ck, h in chains]
    gr = [gr_all[ck][h:h + 1, :c] for ck, h in chains]
    decay = [jnp.exp(jnp.where(lower, gc[i] - gr[i], -jnp.inf)) for i in nc]
    exp_g = [jnp.exp(x) for x in gc]
    kb = [k[i] * beta[i] for i in nc]
    kbf = [x.astype(BF16) for x in k]
    a = [_dot_nt(kb[i].astype(BF16), kbf[i]) * jnp.where(strict, decay[i], 0.0) for i in nc]
    attn = [_dot_nt(q[i].astype(BF16), kbf[i]) * decay[i] for i in nc]
    t = _unit_lower_inverse(a, eye, diag16, low32, low64)
    uu = [_bdot(t[i], v[i] * beta[i]) for i in nc]
    ww = [_bdot(t[i], kb[i] * exp_g[i]) for i in nc]
    for i in nc:
        u_ref[rs[i], sl[i]] = uu[i]
        w_ref[rs[i], sl[i]] = ww[i].astype(BF16)
        at_ref[rs[i], sl[i]] = jnp.concatenate([attn[i], jnp.zeros((c, GDN_DK - c), F32)], axis=1).astype(BF16)
        qg_ref[rs[i], sl[i]] = (q[i] * exp_g[i]).astype(BF16)
        kg_ref[rs[i], sl[i]] = (k[i] * jnp.exp(gc[i][c - 1:c, :] - gc[i])).astype(BF16)


def _gdn_scan_kernel(u_ref, w_ref, qg_ref, kg_ref, at_ref, eg_ref, z_ref, nw_ref, o_ref, s_ref):
    c = GDN_CHUNK

    @pl.when(pl.program_id(1) == 0)
    def _():
        s_ref[...] = jnp.zeros_like(s_ref)

    nw = nw_ref[...]
    heads = range(GDN_HEADS)
    sl = [slice(h * GDN_DK, (h + 1) * GDN_DK) for h in heads]
    s = [s_ref[0, h] for h in heads]
    for ck in range(u_ref.shape[0] // c):
        r = slice(ck * c, (ck + 1) * c)
        eg = eg_ref[ck]
        sb = [x.astype(BF16) for x in s]
        ws = [_dot(w_ref[r, sl[h]], sb[h]) for h in heads]
        qs = [_dot(qg_ref[r, sl[h]], sb[h]) for h in heads]
        vnb = [(u_ref[r, sl[h]] - ws[h]).astype(BF16) for h in heads]
        o = [qs[h] + _dot(at_ref[r, h * GDN_DK:h * GDN_DK + c], vnb[h]) for h in heads]
        s = [s[h] * eg[:, h:h + 1] + _dot_tn(kg_ref[r, sl[h]], vnb[h]) for h in heads]
        for h in heads:
            o_ref[r, sl[h]] = (_rms(o[h]) * nw * _silu(z_ref[r, sl[h]].astype(F32))).astype(BF16)
    for h in heads:
        s_ref[0, h] = s[h]


def _gdn_prompt(p_gdn, p_ab, batch, seq, conv_w, a_log, dt_bias, gdn_norm_w):
    c = GDN_CHUNK
    rows = GDN_PREP_ROWS
    t = batch * seq
    nt = seq // rows
    alog = jnp.pad(a_log.astype(F32), (0, LANES - GDN_HEADS)).reshape(1, LANES)
    dtb = jnp.pad(dt_bias.astype(F32), (0, LANES - GDN_HEADS)).reshape(1, LANES)
    wide = lambda: pl.BlockSpec((rows, GDN_V_W), lambda b, j: (b * nt + j, 0))
    u, w, qg, kg, at, eg, conv_new = pl.pallas_call(
        _gdn_prep_kernel,
        grid=(batch, nt),
        in_specs=[pl.BlockSpec((rows, CONV_CH), lambda b, j: (b * nt + j, 0)),
                  pl.BlockSpec((16, CONV_CH), lambda b, j: (jnp.maximum((b * nt + j) * (rows // 16) - 1, 0), 0)),
                  pl.BlockSpec((rows, LANES), lambda b, j: (b * nt + j, 0)),
                  _resident((CONV_W, CONV_CH)), _resident((1, LANES)), _resident((1, LANES))],
        out_specs=[wide(), wide(), wide(), wide(), wide(),
                   pl.BlockSpec((rows // c, 1, LANES), lambda b, j: (b * nt + j, 0, 0)),
                   pl.BlockSpec((1, CONV_W - 1, CONV_CH), lambda b, j: (b, 0, 0))],
        out_shape=[jax.ShapeDtypeStruct((t, GDN_V_W), F32)] + [jax.ShapeDtypeStruct((t, GDN_V_W), BF16)] * 4
        + [jax.ShapeDtypeStruct((t // c, 1, LANES), F32),
           jax.ShapeDtypeStruct((batch, CONV_W - 1, CONV_CH), F32)],
        scratch_shapes=[pltpu.VMEM((rows, CONV_CH), F32)],
        compiler_params=_params(("parallel", "arbitrary")),
        name="gdn_prep",
    )(p_gdn, p_gdn, p_ab, conv_w.astype(F32), alog, dtb)
    srows = GDN_SCAN_ROWS
    n = seq // srows
    nq = CONV_CH // GDN_V_W
    blk = lambda: pl.BlockSpec((srows, GDN_V_W), lambda b, j: (b * n + j, 0))
    o, s_new = pl.pallas_call(
        _gdn_scan_kernel,
        grid=(batch, n),
        in_specs=[blk(), blk(), blk(), blk(), blk(),
                  pl.BlockSpec((srows // c, 1, LANES), lambda b, j: (b * n + j, 0, 0)),
                  pl.BlockSpec((srows, GDN_V_W), lambda b, j: (b * n + j, nq)),
                  _resident((1, GDN_DV))],
        out_specs=[blk(), pl.BlockSpec((1, GDN_HEADS, GDN_DK, GDN_DV), lambda b, j: (b, 0, 0, 0))],
        out_shape=[jax.ShapeDtypeStruct((t, GDN_V_W), BF16),
                   jax.ShapeDtypeStruct((batch, GDN_HEADS, GDN_DK, GDN_DV), F32)],
        compiler_params=_params(("parallel", "arbitrary")),
        name="gdn_scan",
    )(u, w, qg, kg, at, eg, p_gdn, gdn_norm_w.astype(F32).reshape(1, GDN_DV))
    return o, s_new, conv_new


def _gdn_sample_kernel(x_ref, ab_ref, sc_ref, cw_ref, alog_ref, dtb_ref, nw_ref, s_ref, o_ref, so_ref, sco_ref):
    nb = x_ref.shape[0]
    x = x_ref[:, :CONV_CH]
    acc = x * cw_ref[CONV_W - 1:CONV_W, :]
    for i in range(CONV_W - 1):
        acc = acc + sc_ref[i] * cw_ref[i:i + 1, :]
    for i in range(CONV_W - 2):
        sco_ref[i] = sc_ref[i + 1]
    sco_ref[CONV_W - 2] = x
    u = _silu(acc)
    g_all, beta_all = _gdn_gates(ab_ref[...], alog_ref[...], dtb_ref[...])
    eg_all = jnp.exp(g_all)
    nw = nw_ref[...]
    for h in range(GDN_HEADS):
        sl = slice(h * GDN_DK, (h + 1) * GDN_DK)
        q = _l2norm(u[:, sl]) * (GDN_DK ** -0.5)
        k = _l2norm(u[:, GDN_QK_W + h * GDN_DK:GDN_QK_W + (h + 1) * GDN_DK])
        v = u[:, 2 * GDN_QK_W + h * GDN_DV:2 * GDN_QK_W + (h + 1) * GDN_DV]
        beta = beta_all[:, GDN_HEADS + h:GDN_HEADS + h + 1]
        eg = eg_all[:, h:h + 1]
        qk_dot = jnp.sum(q * k, axis=-1, keepdims=True)
        qt, kt = _columns(q), _columns(k)
        rows = []
        for j in range(nb):
            s = s_ref[j, h]
            kcol = kt[:, j:j + 1]
            ks = jnp.sum(kcol * s, axis=0, keepdims=True)
            qs = jnp.sum(qt[:, j:j + 1] * s, axis=0, keepdims=True)
            ej = eg[j:j + 1]
            v_new = beta[j:j + 1] * (v[j:j + 1] - ej * ks)
            rows.append(ej * qs + qk_dot[j:j + 1] * v_new)
            so_ref[j, h] = s * ej + kcol * v_new
        o = jnp.concatenate(rows, axis=0)
        z = x_ref[:, CONV_CH + h * GDN_DV:CONV_CH + (h + 1) * GDN_DV]
        o_ref[:, sl] = (_rms(o) * nw * _silu(z)).astype(BF16)


def _gdn_sample(p_gdn_s, p_ab_s, state, conv_state, conv_w, a_log, dt_bias, gdn_norm_w):
    ts = p_gdn_s.shape[0]
    sb = SAMPLE_TILE
    alog = jnp.pad(a_log.astype(F32), (0, LANES - GDN_HEADS)).reshape(1, LANES)
    dtb = jnp.pad(dt_bias.astype(F32), (0, LANES - GDN_HEADS)).reshape(1, LANES)
    sc = jnp.swapaxes(conv_state.astype(F32), 0, 1)
    st = pl.BlockSpec((sb, GDN_HEADS, GDN_DK, GDN_DV), lambda i: (i, 0, 0, 0))
    scs = pl.BlockSpec((CONV_W - 1, sb, CONV_CH), lambda i: (0, i, 0))
    o, s_new, sc_new = pl.pallas_call(
        _gdn_sample_kernel,
        grid=(ts // sb,),
        in_specs=[pl.BlockSpec((sb, _GDN_W), lambda i: (i, 0)), pl.BlockSpec((sb, LANES), lambda i: (i, 0)), scs,
                  _resident((CONV_W, CONV_CH)), _resident((1, LANES)), _resident((1, LANES)),
                  _resident((1, GDN_DV)), st],
        out_specs=[pl.BlockSpec((sb, GDN_V_W), lambda i: (i, 0)), st, scs],
        out_shape=[jax.ShapeDtypeStruct((ts, GDN_V_W), BF16), jax.ShapeDtypeStruct(state.shape, F32),
                   jax.ShapeDtypeStruct(sc.shape, F32)],
        compiler_params=_params(("parallel",)),
        name="gdn_sample",
    )(p_gdn_s, p_ab_s, sc, conv_w.astype(F32), alog, dtb, gdn_norm_w.astype(F32).reshape(1, GDN_DV), state)
    return o, s_new, jnp.swapaxes(sc_new, 0, 1)


_NO_EXPERT = -1e30


def _merge_kernel(oa0_ref, ob0_ref, gate0_ref, x0_ref, oa1_ref, ob1_ref, gate1_ref, x1in_ref,
                  wa_ref, wb_ref, wo_ref, nw_ref, wr_ref, br_ref, x1_ref, h2_ref, ti_ref, tw_ref, *, n_first):
    first = pl.program_id(0) < n_first
    pick = lambda a, b: jnp.where(first, a[...], b[...])
    gate = pick(gate0_ref, gate1_ref)
    ya = _dot(pick(oa0_ref, oa1_ref), wa_ref[...])
    yb = _dot(pick(ob0_ref, ob1_ref), wb_ref[...])
    ga = gate[:, :D_MODEL].astype(F32)
    gb = gate[:, D_MODEL:].astype(F32)
    m = _sigmoid(ga) * ya + _sigmoid(gb) * yb
    x1 = pick(x0_ref, x1in_ref) + _dot(m.astype(BF16), wo_ref[...])
    x1_ref[...] = x1
    h2 = _rms(x1) * nw_ref[...]
    for j in range(D_MODEL // LANES):
        h2_ref[:, j, :] = h2[:, j * LANES:(j + 1) * LANES]
    lg = _dot_split(h2, wr_ref[...]) + br_ref[...]
    lane = lax.broadcasted_iota(jnp.int32, lg.shape, 1).astype(F32)
    vals, idxs = [], []
    for _ in range(TOP_K):
        top = jnp.max(lg, axis=-1, keepdims=True)
        idx = jnp.min(jnp.where(lg == top, lane, float(LANES)), axis=-1, keepdims=True)
        vals.append(top)
        idxs.append(idx)
        lg = jnp.where(lane == idx, _NO_EXPERT, lg)
    es = [jnp.exp(v - vals[0]) for v in vals]
    inv_total = 1.0 / functools.reduce(lambda a, b: a + b, es)
    ti = jnp.zeros_like(lg)
    tw = jnp.zeros_like(lg)
    for k in range(TOP_K):
        ti = jnp.where(lane == float(k), idxs[k], ti)
        tw = jnp.where(lane == float(k), es[k] * inv_total, tw)
    ti_ref[...] = ti.astype(jnp.int32)
    tw_ref[...] = tw


def _merge(group0, group1, weights):
    r0, r1 = group0[3].shape[0], group1[3].shape[0]
    tm = _pick_tile(math.gcd(r0, r1), (128, 64, 32, 16, 8))
    n0, total = r0 // tm, r0 + r1
    widths = (RET_V_W, GDN_V_W, 2 * D_MODEL, D_MODEL)
    specs0 = [pl.BlockSpec((tm, n), lambda i: (jnp.minimum(i, n0 - 1), 0)) for n in widths]
    specs1 = [pl.BlockSpec((tm, n), lambda i: (jnp.maximum(i - n0, 0), 0)) for n in widths]
    out = lambda n: pl.BlockSpec((tm, n), lambda i: (i, 0))
    sub = D_MODEL // LANES
    sq = (D_MODEL, D_MODEL)
    return pl.pallas_call(
        functools.partial(_merge_kernel, n_first=n0),
        grid=(total // tm,),
        in_specs=specs0 + specs1 + [_resident(sq), _resident(sq), _resident(sq), _resident((1, D_MODEL)),
                                    _resident((D_MODEL, LANES)), _resident((1, LANES))],
        out_specs=[out(D_MODEL), pl.BlockSpec((tm, sub, LANES), lambda i: (i, 0, 0)), out(LANES), out(LANES)],
        out_shape=[jax.ShapeDtypeStruct((total, D_MODEL), F32), jax.ShapeDtypeStruct((total, sub, LANES), F32),
                   jax.ShapeDtypeStruct((total, LANES), jnp.int32), jax.ShapeDtypeStruct((total, LANES), F32)],
        compiler_params=_params(("parallel",)),
        name="merge",
    )(*group0, *group1, *weights)


def _merge_weights(w_a, w_b, w_o, ffn_norm_w, w_router, b_router):
    wr = jnp.pad(w_router.astype(F32), ((0, 0), (0, LANES - N_EXPERTS)))
    br = jnp.pad(b_router.astype(F32), (0, LANES - N_EXPERTS), constant_values=_NO_EXPERT).reshape(1, LANES)
    return (w_a.astype(BF16), w_b.astype(BF16), w_o.astype(BF16), ffn_norm_w.astype(F32).reshape(1, D_MODEL), wr, br)


def _route(top_i, gate, n_tokens):
    rows = MOE_ROWS
    n = n_tokens * TOP_K
    flat_e = top_i.reshape(n).astype(jnp.int32)
    bits = max(1, (n - 1).bit_length())
    assert bits + (N_EXPERTS - 1).bit_length() <= 31
    order = lax.sort((flat_e << bits) | jnp.arange(n, dtype=jnp.int32)) & ((1 << bits) - 1)
    counts = jnp.sum((flat_e[:, None] == jnp.arange(N_EXPERTS, dtype=jnp.int32)[None, :]).astype(jnp.int32), axis=0)
    start = jnp.cumsum(counts) - counts
    pcounts = (counts + rows - 1) // rows * rows
    pend = jnp.cumsum(pcounts)
    pstart = pend - pcounts
    nb = -(-n // rows) + N_EXPERTS
    blk = jnp.arange(nb, dtype=jnp.int32)
    block_e = jnp.minimum(jnp.sum((pend[None, :] <= (blk * rows)[:, None]).astype(jnp.int32), axis=1),
                          N_EXPERTS - 1).astype(jnp.int32)
    nb_used = (pend[-1] // rows).astype(jnp.int32).reshape(1)
    within = (blk * rows - pstart[block_e])[:, None] + jnp.arange(rows, dtype=jnp.int32)[None, :]
    valid = jnp.logical_and(within < counts[block_e][:, None], (blk < nb_used[0])[:, None])
    flat = order[jnp.clip(start[block_e][:, None] + within, 0, n - 1)]
    spare = n + (blk % 2)[:, None] * rows + jnp.arange(rows, dtype=jnp.int32)[None, :]
    src = jnp.where(valid, flat // TOP_K, 0)
    dst = jnp.where(valid, flat, spare)
    nxt = jnp.concatenate([src[1:], src[-1:]], axis=0)
    slab = jnp.concatenate([src, dst, nxt], axis=1)
    row_w = jnp.where(valid, gate[:, :TOP_K].reshape(n)[flat], 0.0)
    row_w = jnp.broadcast_to(row_w[:, :, None], (nb, rows, LANES))
    return block_e, nb_used, slab, row_w


def _expert_kernel(be_ref, nbu_ref, slab_ref, h_ref, roww_ref, wgu_ref, bgu_ref, wd_ref, bd_ref, y_ref,
                   idx_ref, xbuf_ref, ybuf_ref, wgu_bf_ref, wd_bf_ref, isem, gsem, ssem):
    rows = MOE_ROWS
    sub = D_MODEL // LANES
    i = pl.program_id(0)
    nbu = nbu_ref[0]
    slot = i % 2
    active = i < nbu
    other = 1 - slot
    n_real = y_ref.shape[0] - 2 * rows

    def slab_copy(blk, sl):
        return pltpu.make_async_copy(slab_ref.at[blk], idx_ref.at[sl], isem.at[sl])

    def gather_row(tok, sl, r):
        return pltpu.make_async_copy(h_ref.at[tok], xbuf_ref.at[sl, :, r, :], gsem.at[sl])

    def scatter_row(sl, r, f):
        return pltpu.make_async_copy(ybuf_ref.at[sl, :, r, :], y_ref.at[f], ssem.at[sl])

    def gather_wait(sl):
        pltpu.make_async_copy(xbuf_ref.at[sl], xbuf_ref.at[sl], gsem.at[sl]).wait()

    def scatter_wait(sl):
        pltpu.make_async_copy(ybuf_ref.at[sl], ybuf_ref.at[sl], ssem.at[sl]).wait()

    @pl.when(i == 0)
    def _():
        slab_copy(0, 0).start()
        ybuf_ref[...] = jnp.zeros_like(ybuf_ref)
        for sl in range(2):
            def fill(r, carry, sl=sl):
                scatter_row(sl, r, n_real + sl * rows + r).start()
                return carry
            lax.fori_loop(0, rows, fill, 0)

    @pl.when(active)
    def _():
        slab_copy(i, slot).wait()

        @pl.when(i + 1 < nbu)
        def _():
            slab_copy(i + 1, other).start()

        @pl.when(i == 0)
        def _():
            def body(r, carry):
                gather_row(idx_ref[0, r], 0, r).start()
                return carry
            lax.fori_loop(0, rows, body, 0)

        changed = jnp.logical_or(i == 0, be_ref[i] != be_ref[jnp.maximum(i - 1, 0)])

        @pl.when(changed)
        def _():
            wgu_bf_ref[...] = wgu_ref[0].astype(BF16)
            wd_bf_ref[...] = wd_ref[0].astype(BF16)

        def block(sl):
            gather_wait(sl)
            scatter_wait(sl)
            xb = jnp.concatenate([xbuf_ref[sl, j] for j in range(sub)], axis=1).astype(BF16)
            hb = _dot(xb, wgu_bf_ref[...]) + bgu_ref[0]
            for r in range(rows):
                gather_row(idx_ref[sl, 2 * rows + r], 1 - sl, r).start()
            glu = jnp.minimum(hb[:, :D_FF], SWIGLU_LIMIT)
            lin = jnp.clip(hb[:, D_FF:], -SWIGLU_LIMIT, SWIGLU_LIMIT)
            act = (glu * _sigmoid(SWIGLU_ALPHA * glu) * (lin + 1.0)).astype(BF16)
            yv = (_dot(act, wd_bf_ref[...]) + bd_ref[0]) * roww_ref[0, :, 0:1]
            for j in range(sub):
                ybuf_ref[sl, j] = yv[:, j * LANES:(j + 1) * LANES]
            for r in range(rows):
                scatter_row(sl, r, idx_ref[sl, rows + r]).start(priority=1)

        for sl in range(2):
            pl.when(slot == sl)(functools.partial(block, sl))

        @pl.when(i == nbu - 1)
        def _():
            scatter_wait(slot)
            scatter_wait(other)
            gather_wait(other)


def _experts(h2, block_e, nb_used, slab, row_w, w_gate_up, b_gate_up, w_down, b_down):
    t = h2.shape[0]
    rows = MOE_ROWS
    nb = slab.shape[0]
    sub = D_MODEL // LANES
    grid_spec = pltpu.PrefetchScalarGridSpec(
        num_scalar_prefetch=2,
        grid=(nb,),
        in_specs=[pl.BlockSpec(memory_space=pl.ANY),
                  pl.BlockSpec(memory_space=pl.ANY),
                  pl.BlockSpec((1, rows, LANES), lambda i, be, nbu: (i, 0, 0)),
                  pl.BlockSpec((1, D_MODEL, 2 * D_FF), lambda i, be, nbu: (be[i], 0, 0)),
                  pl.BlockSpec((1, 1, 2 * D_FF), lambda i, be, nbu: (be[i], 0, 0)),
                  pl.BlockSpec((1, D_FF, D_MODEL), lambda i, be, nbu: (be[i], 0, 0)),
                  pl.BlockSpec((1, 1, D_MODEL), lambda i, be, nbu: (be[i], 0, 0))],
        out_specs=pl.BlockSpec(memory_space=pl.ANY),
        scratch_shapes=[pltpu.SMEM((2, 3 * rows), jnp.int32),
                        pltpu.VMEM((2, sub, rows, LANES), F32),
                        pltpu.VMEM((2, sub, rows, LANES), F32),
                        pltpu.VMEM((D_MODEL, 2 * D_FF), BF16),
                        pltpu.VMEM((D_FF, D_MODEL), BF16),
                        pltpu.SemaphoreType.DMA((2,)),
                        pltpu.SemaphoreType.DMA((2,)),
                        pltpu.SemaphoreType.DMA((2,))])
    return pl.pallas_call(
        _expert_kernel,
        grid_spec=grid_spec,
        out_shape=jax.ShapeDtypeStruct((t * TOP_K + 2 * rows, sub, LANES), F32),
        compiler_params=_params(("arbitrary",)),
        name="experts",
    )(block_e, nb_used, slab, h2, row_w, w_gate_up, b_gate_up.reshape(N_EXPERTS, 1, 2 * D_FF), w_down,
      b_down.reshape(N_EXPERTS, 1, D_MODEL))


def _combine_kernel(y_ref, x1_ref, nw_ref, o_ref, sum_ref, *, final):
    total = y_ref[:, 0]
    for k in range(1, TOP_K):
        total = total + y_ref[:, k]
    sum_ref[...] = total
    acc = x1_ref[...] + jnp.concatenate([sum_ref[:, j, :] for j in range(D_MODEL // LANES)], axis=1)
    o_ref[...] = _rms(acc) * nw_ref[...] if final else acc


def _combine(y, x1, row0, rows, norm_w, final):
    tm = _row_tile(rows, row0, (256, 128, 64, 32, 16, 8))
    off = row0 // tm
    sub = D_MODEL // LANES
    y4 = y.reshape(y.shape[0] // TOP_K, TOP_K, sub, LANES)
    return pl.pallas_call(
        functools.partial(_combine_kernel, final=final),
        grid=(rows // tm,),
        in_specs=[pl.BlockSpec((tm, TOP_K, sub, LANES), lambda i: (off + i, 0, 0, 0)),
                  pl.BlockSpec((tm, D_MODEL), lambda i: (off + i, 0)), _resident((1, D_MODEL))],
        out_specs=pl.BlockSpec((tm, D_MODEL), lambda i: (i, 0)),
        out_shape=jax.ShapeDtypeStruct((rows, D_MODEL), F32),
        scratch_shapes=[pltpu.VMEM((tm, sub, LANES), F32)],
        compiler_params=_params(("parallel",)),
        name="combine",
    )(y4, x1, norm_w.astype(F32).reshape(1, D_MODEL))


def kernel(x_prompt, x_sample, state_ret, state_gdn, state_conv, attn_norm_w, w_in, conv_w, a_log, dt_bias, gdn_norm_w, w_branch_a, w_branch_b, w_out, ffn_norm_w, w_router, b_router, w_gate_up, b_gate_up, w_down, b_down, final_norm_w):
    bp, lp, d = x_prompt.shape
    bs, ls, _ = x_sample.shape
    assert ls == 1 and d == D_MODEL and lp % RET_CHUNK == 0 and bs % SAMPLE_TILE == 0
    depth = w_in.shape[0]
    tp = bp * lp
    t = tp + bs
    xp, xs = x_prompt.reshape(tp, d).astype(F32), x_sample.reshape(bs, d).astype(F32)
    rp, gp, cp, rs, gs, cs = [], [], [], [], [], []
    for l in range(depth):
        wb = jnp.pad(w_in[l].astype(BF16), ((0, 0), (0, -w_in.shape[2] % LANES)))
        pp_ret, pp_gdn, pp_gate, pp_ab = _inproj(xp, attn_norm_w[l], wb)
        ps_ret, ps_gdn, ps_gate, ps_ab = _inproj(xs, attn_norm_w[l], wb)
        op_ret, s_ret_p = _ret_prompt(pp_ret, bp, lp)
        os_ret, s_ret_s = _ret_sample(ps_ret.astype(F32), state_ret[l].astype(F32))
        op_gdn, s_gdn_p, conv_p = _gdn_prompt(pp_gdn, pp_ab, bp, lp, conv_w[l], a_log[l], dt_bias[l], gdn_norm_w[l])
        os_gdn, s_gdn_s, conv_s = _gdn_sample(ps_gdn.astype(F32), ps_ab, state_gdn[l].astype(F32), state_conv[l],
                                              conv_w[l], a_log[l], dt_bias[l], gdn_norm_w[l])
        mw = _merge_weights(w_branch_a[l], w_branch_b[l], w_out[l], ffn_norm_w[l], w_router[l], b_router[l])
        x1, h2, top_i, gate = _merge((op_ret, op_gdn, pp_gate, xp), (os_ret, os_gdn, ps_gate, xs), mw)
        block_e, nb_used, slab, row_w = _route(top_i[:, :TOP_K], gate, t)
        y = _experts(h2, block_e, nb_used, slab, row_w, w_gate_up[l], b_gate_up[l], w_down[l], b_down[l])
        last = l == depth - 1
        norm_w = final_norm_w if last else jnp.ones((d,), F32)
        xp = _combine(y, x1, 0, tp, norm_w, last)
        xs = _combine(y, x1, tp, bs, norm_w, last)
        rp.append(s_ret_p); gp.append(s_gdn_p); cp.append(conv_p)
        rs.append(s_ret_s); gs.append(s_gdn_s); cs.append(conv_s)
    y_prompt = xp.reshape(bp, lp, d).astype(x_prompt.dtype)
    y_sample = xs.reshape(bs, ls, d).astype(x_sample.dtype)
    return (y_prompt, y_sample,
            jnp.stack(rp).astype(state_ret.dtype), jnp.stack(gp).astype(state_gdn.dtype),
            jnp.stack(cp).astype(state_conv.dtype),
            jnp.stack(rs).astype(state_ret.dtype), jnp.stack(gs).astype(state_gdn.dtype),
            jnp.stack(cs).astype(state_conv.dtype))
```

```python
import functools
import math

import numpy as np
import jax
import jax.numpy as jnp
from jax import lax
from jax.experimental import pallas as pl
from jax.experimental.pallas import tpu as pltpu

F32 = jnp.float32
BF16 = jnp.bfloat16
HIGHEST = lax.Precision.HIGHEST

D_MODEL = 1024
PAST_LEN = 16384
RET_HEADS, RET_DK, RET_DV = 4, 128, 256
RET_QK_W, RET_V_W = RET_HEADS * RET_DK, RET_HEADS * RET_DV
RET_CHUNK = 128
ROPE_BASE = 10000.0
GDN_HEADS, GDN_DK, GDN_DV = 8, 128, 128
GDN_QK_W, GDN_V_W = GDN_HEADS * GDN_DK, GDN_HEADS * GDN_DV
GDN_CHUNK = 64
CONV_W = 4
CONV_CH = 2 * GDN_QK_W + GDN_V_W
N_EXPERTS = 32
TOP_K = 4
D_FF = D_MODEL
SWIGLU_LIMIT = 7.0
SWIGLU_ALPHA = 1.702
NORM_EPS = 1e-6

_RET_W = 2 * RET_QK_W + 2 * RET_V_W
_GDN_W = CONV_CH + GDN_V_W
_AB_OFF = _RET_W + _GDN_W
_GATE_OFF = _AB_OFF + 2 * GDN_HEADS

LANES = 128
VMEM_LIMIT = 56 * 1024 * 1024
MOE_ROWS = 256
MOE_RING = 3
SAMPLE_TILE = 8
RET_STEP_ROWS = 256
GDN_PREP_ROWS = 128
GDN_SCAN_ROWS = 128


def _pick_tile(n, candidates):
    for c in candidates:
        if n % c == 0:
            return c
    raise ValueError(f"no tile in {candidates} divides {n}")


def _params(sem, vmem=VMEM_LIMIT):
    return pltpu.CompilerParams(dimension_semantics=sem, vmem_limit_bytes=vmem)


def _resident(shape):
    nd = len(shape)
    return pl.BlockSpec(shape, lambda *_: (0,) * nd, pipeline_mode=pl.Buffered(1))


def _silu(x):
    return x * (1.0 / (1.0 + jnp.exp(-x)))


def _sigmoid(x):
    return 1.0 / (1.0 + jnp.exp(-x))


def _softplus(x):
    return jnp.maximum(x, 0.0) + jnp.log1p(jnp.exp(-jnp.abs(x)))


def _rms(x):
    return x * lax.rsqrt(jnp.mean(x * x, axis=-1, keepdims=True) + NORM_EPS)


def _dot(a, b):
    return jnp.dot(a, b, preferred_element_type=F32)


def _dot_nt(a, b):
    return lax.dot_general(a, b, (((1,), (1,)), ((), ())), preferred_element_type=F32)


def _dot_tn(a, b):
    return lax.dot_general(a, b, (((0,), (0,)), ((), ())), preferred_element_type=F32)


def _dot_hi(a, b):
    return jnp.dot(a, b, preferred_element_type=F32, precision=HIGHEST)


def _dot_split(a, b):
    a_hi, b_hi = a.astype(BF16), b.astype(BF16)
    a_lo = (a - a_hi.astype(F32)).astype(BF16)
    b_lo = (b - b_hi.astype(F32)).astype(BF16)
    return _dot(a_hi, b_hi) + (_dot(a_hi, b_lo) + _dot(a_lo, b_hi))


def _row_tile(rows, row0, candidates):
    return _pick_tile(math.gcd(rows, row0) if row0 else rows, candidates)


def _inproj_kernel(x_ref, nw_ref, w_ref, oret_ref, ogdn_ref, ogate_ref, oab_ref):
    h = (_rms(x_ref[...]) * nw_ref[...]).astype(BF16)
    oret_ref[...] = _dot(h, w_ref[:, :_RET_W]).astype(BF16)
    ogdn_ref[...] = _dot(h, w_ref[:, _RET_W:_AB_OFF]).astype(BF16)
    tail = _dot(h, w_ref[:, _AB_OFF:])
    oab_ref[...] = tail[:, :LANES]
    ogate_ref[...] = tail[:, _GATE_OFF - _AB_OFF:_GATE_OFF - _AB_OFF + 2 * D_MODEL].astype(BF16)


def _inproj(x, norm_w, wb):
    rows = x.shape[0]
    tm = _pick_tile(rows, (256, 128, 64, 32, 16, 8))
    row = lambda n: pl.BlockSpec((tm, n), lambda i: (i, 0))
    return pl.pallas_call(
        _inproj_kernel,
        grid=(rows // tm,),
        in_specs=[row(D_MODEL), _resident((1, D_MODEL)), _resident(wb.shape)],
        out_specs=[row(_RET_W), row(_GDN_W), row(2 * D_MODEL), row(LANES)],
        out_shape=[jax.ShapeDtypeStruct((rows, _RET_W), BF16), jax.ShapeDtypeStruct((rows, _GDN_W), BF16),
                   jax.ShapeDtypeStruct((rows, 2 * D_MODEL), BF16), jax.ShapeDtypeStruct((rows, LANES), F32)],
        compiler_params=_params(("parallel",)),
        name="inproj",
    )(x, norm_w.reshape(1, D_MODEL), wb)


def _ret_log_gamma():
    return np.log1p(-np.exp2(-5.0 - np.arange(RET_HEADS, dtype=np.float64)))


def _rope_tables(pos):
    half = RET_DK // 2
    inv = 1.0 / (ROPE_BASE ** (jnp.arange(half, dtype=F32) / half))
    ang = pos.astype(F32)[:, None] * inv[None, :]
    cos, sin = jnp.cos(ang), jnp.sin(ang)
    return jnp.concatenate([cos, cos], axis=-1), jnp.concatenate([-sin, sin], axis=-1)


def _rotary(x, cos, sin):
    return x * cos + pltpu.roll(x, RET_DK // 2, 1) * sin


def _ret_prompt_kernel(q_ref, k_ref, v_ref, g_ref, cos_ref, sin_ref, dmask_ref, qdec_ref, kdec_ref,
                       o_ref, s_ref, *, gammas):
    @pl.when(pl.program_id(1) == 0)
    def _():
        s_ref[...] = jnp.zeros_like(s_ref)

    c = RET_CHUNK
    heads = range(RET_HEADS)
    qk = [slice(h * RET_DK, (h + 1) * RET_DK) for h in heads]
    vv = [slice(h * RET_DV, (h + 1) * RET_DV) for h in heads]
    s = [s_ref[0, h] for h in heads]
    for ck in range(q_ref.shape[0] // c):
        r = slice(ck * c, (ck + 1) * c)
        cos, sin = cos_ref[r, :], sin_ref[r, :]
        q = [_rotary(q_ref[r, qk[h]].astype(F32), cos, sin) for h in heads]
        k = [_rotary(k_ref[r, qk[h]].astype(F32), cos, sin) * (RET_DK ** -0.5) for h in heads]
        v = [v_ref[r, vv[h]] for h in heads]
        qb = [x.astype(BF16) for x in q]
        inner = [_dot_nt(qb[h], k[h].astype(BF16)) * dmask_ref[h] for h in heads]
        cross = [_dot((q[h] * qdec_ref[h]).astype(BF16), s[h].astype(BF16)) for h in heads]
        upd = [_dot_tn((k[h] * kdec_ref[h]).astype(BF16), v[h]) for h in heads]
        o = [_dot(inner[h].astype(BF16), v[h]) + cross[h] for h in heads]
        s = [s[h] * gammas[h] + upd[h] for h in heads]
        for h in heads:
            o_ref[r, vv[h]] = (_rms(o[h]) * _silu(g_ref[r, vv[h]].astype(F32))).astype(BF16)
    for h in heads:
        s_ref[0, h] = s[h]


def _ret_prompt(p_ret, batch, seq):
    c = RET_CHUNK
    step = RET_STEP_ROWS if seq % RET_STEP_ROWS == 0 else c
    n = seq // step
    lg = _ret_log_gamma()
    idx = np.arange(c, dtype=np.float64)
    diff = idx[:, None] - idx[None, :]
    dmask = np.where(diff >= 0, np.exp(np.maximum(diff, 0.0)[None] * lg[:, None, None]), 0.0)
    qdec = np.broadcast_to(np.exp((idx + 1.0)[None, :] * lg[:, None])[:, :, None], (RET_HEADS, c, RET_DK))
    kdec = np.broadcast_to(np.exp((c - 1.0 - idx)[None, :] * lg[:, None])[:, :, None], (RET_HEADS, c, RET_DK))
    gammas = tuple(float(g) for g in np.exp(c * lg))
    cos, sin = _rope_tables(jnp.arange(seq, dtype=jnp.int32))
    tab = lambda: _resident((RET_HEADS, c, RET_DK))
    return pl.pallas_call(
        functools.partial(_ret_prompt_kernel, gammas=gammas),
        grid=(batch, n),
        in_specs=[pl.BlockSpec((step, RET_QK_W), lambda b, j: (b * n + j, 0)),
                  pl.BlockSpec((step, RET_QK_W), lambda b, j: (b * n + j, 1)),
                  pl.BlockSpec((step, RET_V_W), lambda b, j: (b * n + j, 1)),
                  pl.BlockSpec((step, RET_V_W), lambda b, j: (b * n + j, 2)),
                  pl.BlockSpec((step, RET_DK), lambda b, j: (j, 0)),
                  pl.BlockSpec((step, RET_DK), lambda b, j: (j, 0)),
                  tab(), tab(), tab()],
        out_specs=[pl.BlockSpec((step, RET_V_W), lambda b, j: (b * n + j, 0)),
                   pl.BlockSpec((1, RET_HEADS, RET_DK, RET_DV), lambda b, j: (b, 0, 0, 0))],
        out_shape=[jax.ShapeDtypeStruct((batch * seq, RET_V_W), BF16),
                   jax.ShapeDtypeStruct((batch, RET_HEADS, RET_DK, RET_DV), F32)],
        compiler_params=_params(("parallel", "arbitrary")),
        name="ret_prompt",
    )(p_ret, p_ret, p_ret, p_ret, cos, sin, jnp.asarray(dmask, F32), jnp.asarray(qdec, F32),
      jnp.asarray(kdec, F32))


def _columns(x):
    n = x.shape[0]
    if n < LANES:
        x = jnp.concatenate([x, jnp.zeros((LANES - n, x.shape[1]), x.dtype)], axis=0)
    return x.T


def _ret_sample_kernel(p_ref, cos_ref, sin_ref, s_ref, o_ref, so_ref, *, gammas):
    cos, sin = cos_ref[...], sin_ref[...]
    nb = p_ref.shape[0]
    for h in range(RET_HEADS):
        qk = slice(h * RET_DK, (h + 1) * RET_DK)
        q = _rotary(p_ref[:, qk], cos, sin)
        k = _rotary(p_ref[:, RET_QK_W + h * RET_DK:RET_QK_W + (h + 1) * RET_DK], cos, sin) * (RET_DK ** -0.5)
        v = p_ref[:, 2 * RET_QK_W + h * RET_DV:2 * RET_QK_W + (h + 1) * RET_DV]
        g = p_ref[:, 2 * RET_QK_W + RET_V_W + h * RET_DV:2 * RET_QK_W + RET_V_W + (h + 1) * RET_DV]
        qk_dot = jnp.sum(q * k, axis=-1, keepdims=True)
        qt, kt = _columns(q), _columns(k)
        rows = []
        for j in range(nb):
            s = s_ref[j, h]
            qs = jnp.sum(qt[:, j:j + 1] * s, axis=0, keepdims=True)
            rows.append(qk_dot[j:j + 1] * v[j:j + 1] + gammas[h] * qs)
            so_ref[j, h] = s * gammas[h] + kt[:, j:j + 1] * v[j:j + 1]
        o = jnp.concatenate(rows, axis=0)
        o_ref[:, h * RET_DV:(h + 1) * RET_DV] = (_rms(o) * _silu(g)).astype(BF16)


def _ret_sample(p_ret_s, state):
    ts = p_ret_s.shape[0]
    sb = SAMPLE_TILE
    gammas = tuple(float(g) for g in np.exp(_ret_log_gamma()))
    cos, sin = _rope_tables(jnp.full((1,), PAST_LEN, jnp.int32))
    st = pl.BlockSpec((sb, RET_HEADS, RET_DK, RET_DV), lambda i: (i, 0, 0, 0))
    return pl.pallas_call(
        functools.partial(_ret_sample_kernel, gammas=gammas),
        grid=(ts // sb,),
        in_specs=[pl.BlockSpec((sb, _RET_W), lambda i: (i, 0)), _resident((1, RET_DK)), _resident((1, RET_DK)), st],
        out_specs=[pl.BlockSpec((sb, RET_V_W), lambda i: (i, 0)), st],
        out_shape=[jax.ShapeDtypeStruct((ts, RET_V_W), BF16), jax.ShapeDtypeStruct(state.shape, F32)],
        compiler_params=_params(("parallel",)),
        name="ret_sample",
    )(p_ret_s, cos, sin, state)


def _l2norm(x):
    return x * lax.rsqrt(jnp.sum(x * x, axis=-1, keepdims=True) + NORM_EPS)


def _bdot(a, b):
    return _dot(a.astype(BF16), b.astype(BF16))


def _chunk_masks(c):
    ri = lax.broadcasted_iota(jnp.int32, (c, c), 0)
    ci = lax.broadcasted_iota(jnp.int32, (c, c), 1)
    eye = (ri == ci).astype(F32)
    diag16 = (ri // 16 == ci // 16).astype(F32)
    low32 = jnp.logical_and(ri // 32 == ci // 32, ri // 16 > ci // 16).astype(F32)
    low64 = (ri // 32 > ci // 32).astype(F32)
    return ri >= ci, ri > ci, eye, diag16, low32, low64


def _unit_lower_inverse(a, eye, diag16, low32, low64):
    many = lambda f, *ls: [f(*args) for args in zip(*ls)]
    n = [-(x * diag16) for x in a]
    n2 = many(_bdot, n, n)
    n3 = many(_bdot, n, n2)
    n4 = many(_bdot, n2, n2)
    n8 = many(_bdot, n4, n4)
    x = [eye + p + q + r for p, q, r in zip(n, n2, n3)]
    x = many(lambda u, v: u + v, x, many(_bdot, x, n4))
    x = many(lambda u, v: u + v, x, many(_bdot, x, n8))
    for mask in (low32, low64):
        r = many(_bdot, [y * mask for y in a], x)
        x = many(lambda u, v: u - v, x, many(_bdot, x, r))
    return x


def _gdn_gates(ab, alog, dtb):
    g = -jnp.exp(alog) * _softplus(ab + dtb)
    return g, _sigmoid(ab)


def _gdn_prep_kernel(x_ref, prev_ref, ab_ref, cw_ref, alog_ref, dtb_ref,
                     u_ref, w_ref, qg_ref, kg_ref, at_ref, eg_ref, cv_ref, act_ref):
    c = GDN_CHUNK
    rows = x_ref.shape[0]
    x = x_ref[...]
    halo = prev_ref.shape[0]
    prev = jnp.where(pl.program_id(1) == 0, jnp.zeros_like(prev_ref), prev_ref[...])
    xcat = jnp.concatenate([prev, x], axis=0)
    ti = lax.broadcasted_iota(jnp.int32, (rows, rows + halo), 0)
    ui = lax.broadcasted_iota(jnp.int32, (rows, rows + halo), 1)
    xf = x.astype(F32)
    acc = xf * cw_ref[CONV_W - 1:CONV_W, :]
    for i in range(CONV_W - 1):
        shift = (ui == ti + (halo - (CONV_W - 1) + i)).astype(BF16)
        acc = acc + _dot(shift, xcat) * cw_ref[i:i + 1, :]
    cv_ref[0] = xf[rows - (CONV_W - 1):, :]
    act_ref[...] = _silu(acc)

    g_all, beta_all = _gdn_gates(ab_ref[...], alog_ref[...], dtb_ref[...])
    lower, strict, eye, diag16, low32, low64 = _chunk_masks(c)
    gc_all, gr_all = [], []
    for ck in range(rows // c):
        gc_ck = _dot_hi(lower.astype(F32), g_all[ck * c:(ck + 1) * c])
        gc_all.append(gc_ck)
        gr_all.append(_columns(gc_ck))
        eg_ref[ck] = jnp.exp(gc_ck[c - 1:c, :])
    chains = [(ck, h) for ck in range(rows // c) for h in range(GDN_HEADS)]
    rs = [slice(ck * c, (ck + 1) * c) for ck, _ in chains]
    sl = [slice(h * GDN_DK, (h + 1) * GDN_DK) for _, h in chains]
    nc = range(len(chains))
    q = [_l2norm(act_ref[rs[i], sl[i]]) * (GDN_DK ** -0.5) for i in nc]
    k = [_l2norm(act_ref[rs[i], GDN_QK_W + sl[i].start:GDN_QK_W + sl[i].stop]) for i in nc]
    v = [act_ref[rs[i], 2 * GDN_QK_W + sl[i].start:2 * GDN_QK_W + sl[i].stop] for i in nc]
    beta = [beta_all[rs[i], GDN_HEADS + h:GDN_HEADS + h + 1] for i, (_, h) in enumerate(chains)]
    gc = [gc_all[ck][:, h:h + 1] for ck, h in chains]
    gr = [gr_all[ck][h:h + 1, :c] for ck, h in chains]
    decay = [jnp.exp(jnp.where(lower, gc[i] - gr[i], -jnp.inf)) for i in nc]
    exp_g = [jnp.exp(x) for x in gc]
    kb = [k[i] * beta[i] for i in nc]
    kbf = [x.astype(BF16) for x in k]
    a = [_dot_nt(kb[i].astype(BF16), kbf[i]) * jnp.where(strict, decay[i], 0.0) for i in nc]
    attn = [_dot_nt(q[i].astype(BF16), kbf[i]) * decay[i] for i in nc]
    t = _unit_lower_inverse(a, eye, diag16, low32, low64)
    uu = [_bdot(t[i], v[i] * beta[i]) for i in nc]
    ww = [_bdot(t[i], kb[i] * exp_g[i]) for i in nc]
    for i in nc:
        u_ref[rs[i], sl[i]] = uu[i]
        w_ref[rs[i], sl[i]] = ww[i].astype(BF16)
        at_ref[rs[i], sl[i]] = jnp.concatenate([attn[i], jnp.zeros((c, GDN_DK - c), F32)], axis=1).astype(BF16)
        qg_ref[rs[i], sl[i]] = (q[i] * exp_g[i]).astype(BF16)
        kg_ref[rs[i], sl[i]] = (k[i] * jnp.exp(gc[i][c - 1:c, :] - gc[i])).astype(BF16)


def _gdn_scan_kernel(u_ref, w_ref, qg_ref, kg_ref, at_ref, eg_ref, z_ref, nw_ref, o_ref, s_ref):
    c = GDN_CHUNK

    @pl.when(pl.program_id(1) == 0)
    def _():
        s_ref[...] = jnp.zeros_like(s_ref)

    nw = nw_ref[...]
    heads = range(GDN_HEADS)
    sl = [slice(h * GDN_DK, (h + 1) * GDN_DK) for h in heads]
    s = [s_ref[0, h] for h in heads]
    for ck in range(u_ref.shape[0] // c):
        r = slice(ck * c, (ck + 1) * c)
        eg = eg_ref[ck]
        sb = [x.astype(BF16) for x in s]
        ws = [_dot(w_ref[r, sl[h]], sb[h]) for h in heads]
        qs = [_dot(qg_ref[r, sl[h]], sb[h]) for h in heads]
        vnb = [(u_ref[r, sl[h]] - ws[h]).astype(BF16) for h in heads]
        o = [qs[h] + _dot(at_ref[r, h * GDN_DK:h * GDN_DK + c], vnb[h]) for h in heads]
        s = [s[h] * eg[:, h:h + 1] + _dot_tn(kg_ref[r, sl[h]], vnb[h]) for h in heads]
        for h in heads:
            o_ref[r, sl[h]] = (_rms(o[h]) * nw * _silu(z_ref[r, sl[h]].astype(F32))).astype(BF16)
    for h in heads:
        s_ref[0, h] = s[h]


def _gdn_prompt(p_gdn, p_ab, batch, seq, conv_w, a_log, dt_bias, gdn_norm_w):
    c = GDN_CHUNK
    rows = GDN_PREP_ROWS
    t = batch * seq
    nt = seq // rows
    alog = jnp.pad(a_log.astype(F32), (0, LANES - GDN_HEADS)).reshape(1, LANES)
    dtb = jnp.pad(dt_bias.astype(F32), (0, LANES - GDN_HEADS)).reshape(1, LANES)
    wide = lambda: pl.BlockSpec((rows, GDN_V_W), lambda b, j: (b * nt + j, 0))
    u, w, qg, kg, at, eg, conv_new = pl.pallas_call(
        _gdn_prep_kernel,
        grid=(batch, nt),
        in_specs=[pl.BlockSpec((rows, CONV_CH), lambda b, j: (b * nt + j, 0)),
                  pl.BlockSpec((16, CONV_CH), lambda b, j: (jnp.maximum((b * nt + j) * (rows // 16) - 1, 0), 0)),
                  pl.BlockSpec((rows, LANES), lambda b, j: (b * nt + j, 0)),
                  _resident((CONV_W, CONV_CH)), _resident((1, LANES)), _resident((1, LANES))],
        out_specs=[wide(), wide(), wide(), wide(), wide(),
                   pl.BlockSpec((rows // c, 1, LANES), lambda b, j: (b * nt + j, 0, 0)),
                   pl.BlockSpec((1, CONV_W - 1, CONV_CH), lambda b, j: (b, 0, 0))],
        out_shape=[jax.ShapeDtypeStruct((t, GDN_V_W), F32)] + [jax.ShapeDtypeStruct((t, GDN_V_W), BF16)] * 4
        + [jax.ShapeDtypeStruct((t // c, 1, LANES), F32),
           jax.ShapeDtypeStruct((batch, CONV_W - 1, CONV_CH), F32)],
        scratch_shapes=[pltpu.VMEM((rows, CONV_CH), F32)],
        compiler_params=_params(("parallel", "arbitrary")),
        name="gdn_prep",
    )(p_gdn, p_gdn, p_ab, conv_w.astype(F32), alog, dtb)
    srows = GDN_SCAN_ROWS
    n = seq // srows
    nq = CONV_CH // GDN_V_W
    blk = lambda: pl.BlockSpec((srows, GDN_V_W), lambda b, j: (b * n + j, 0))
    o, s_new = pl.pallas_call(
        _gdn_scan_kernel,
        grid=(batch, n),
        in_specs=[blk(), blk(), blk(), blk(), blk(),
                  pl.BlockSpec((srows // c, 1, LANES), lambda b, j: (b * n + j, 0, 0)),
                  pl.BlockSpec((srows, GDN_V_W), lambda b, j: (b * n + j, nq)),
                  _resident((1, GDN_DV))],
        out_specs=[blk(), pl.BlockSpec((1, GDN_HEADS, GDN_DK, GDN_DV), lambda b, j: (b, 0, 0, 0))],
        out_shape=[jax.ShapeDtypeStruct((t, GDN_V_W), BF16),
                   jax.ShapeDtypeStruct((batch, GDN_HEADS, GDN_DK, GDN_DV), F32)],
        compiler_params=_params(("parallel", "arbitrary")),
        name="gdn_scan",
    )(u, w, qg, kg, at, eg, p_gdn, gdn_norm_w.astype(F32).reshape(1, GDN_DV))
    return o, s_new, conv_new


def _gdn_sample_kernel(x_ref, ab_ref, sc_ref, cw_ref, alog_ref, dtb_ref, nw_ref, s_ref, o_ref, so_ref, sco_ref):
    nb = x_ref.shape[0]
    x = x_ref[:, :CONV_CH]
    acc = x * cw_ref[CONV_W - 1:CONV_W, :]
    for i in range(CONV_W - 1):
        acc = acc + sc_ref[i] * cw_ref[i:i + 1, :]
    for i in range(CONV_W - 2):
        sco_ref[i] = sc_ref[i + 1]
    sco_ref[CONV_W - 2] = x
    u = _silu(acc)
    g_all, beta_all = _gdn_gates(ab_ref[...], alog_ref[...], dtb_ref[...])
    eg_all = jnp.exp(g_all)
    nw = nw_ref[...]
    for h in range(GDN_HEADS):
        sl = slice(h * GDN_DK, (h + 1) * GDN_DK)
        q = _l2norm(u[:, sl]) * (GDN_DK ** -0.5)
        k = _l2norm(u[:, GDN_QK_W + h * GDN_DK:GDN_QK_W + (h + 1) * GDN_DK])
        v = u[:, 2 * GDN_QK_W + h * GDN_DV:2 * GDN_QK_W + (h + 1) * GDN_DV]
        beta = beta_all[:, GDN_HEADS + h:GDN_HEADS + h + 1]
        eg = eg_all[:, h:h + 1]
        qk_dot = jnp.sum(q * k, axis=-1, keepdims=True)
        qt, kt = _columns(q), _columns(k)
        rows = []
        for j in range(nb):
            s = s_ref[j, h]
            kcol = kt[:, j:j + 1]
            ks = jnp.sum(kcol * s, axis=0, keepdims=True)
            qs = jnp.sum(qt[:, j:j + 1] * s, axis=0, keepdims=True)
            ej = eg[j:j + 1]
            v_new = beta[j:j + 1] * (v[j:j + 1] - ej * ks)
            rows.append(ej * qs + qk_dot[j:j + 1] * v_new)
            so_ref[j, h] = s * ej + kcol * v_new
        o = jnp.concatenate(rows, axis=0)
        z = x_ref[:, CONV_CH + h * GDN_DV:CONV_CH + (h + 1) * GDN_DV]
        o_ref[:, sl] = (_rms(o) * nw * _silu(z)).astype(BF16)


def _gdn_sample(p_gdn_s, p_ab_s, state, conv_state, conv_w, a_log, dt_bias, gdn_norm_w):
    ts = p_gdn_s.shape[0]
    sb = SAMPLE_TILE
    alog = jnp.pad(a_log.astype(F32), (0, LANES - GDN_HEADS)).reshape(1, LANES)
    dtb = jnp.pad(dt_bias.astype(F32), (0, LANES - GDN_HEADS)).reshape(1, LANES)
    sc = jnp.swapaxes(conv_state.astype(F32), 0, 1)
    st = pl.BlockSpec((sb, GDN_HEADS, GDN_DK, GDN_DV), lambda i: (i, 0, 0, 0))
    scs = pl.BlockSpec((CONV_W - 1, sb, CONV_CH), lambda i: (0, i, 0))
    o, s_new, sc_new = pl.pallas_call(
        _gdn_sample_kernel,
        grid=(ts // sb,),
        in_specs=[pl.BlockSpec((sb, _GDN_W), lambda i: (i, 0)), pl.BlockSpec((sb, LANES), lambda i: (i, 0)), scs,
                  _resident((CONV_W, CONV_CH)), _resident((1, LANES)), _resident((1, LANES)),
                  _resident((1, GDN_DV)), st],
        out_specs=[pl.BlockSpec((sb, GDN_V_W), lambda i: (i, 0)), st, scs],
        out_shape=[jax.ShapeDtypeStruct((ts, GDN_V_W), BF16), jax.ShapeDtypeStruct(state.shape, F32),
                   jax.ShapeDtypeStruct(sc.shape, F32)],
        compiler_params=_params(("parallel",)),
        name="gdn_sample",
    )(p_gdn_s, p_ab_s, sc, conv_w.astype(F32), alog, dtb, gdn_norm_w.astype(F32).reshape(1, GDN_DV), state)
    return o, s_new, jnp.swapaxes(sc_new, 0, 1)


_NO_EXPERT = -1e30


def _merge_kernel(oa0_ref, ob0_ref, gate0_ref, x0_ref, oa1_ref, ob1_ref, gate1_ref, x1in_ref,
                  wa_ref, wb_ref, wo_ref, nw_ref, wr_ref, br_ref, x1_ref, h2_ref, ti_ref, tw_ref, *, n_first):
    first = pl.program_id(0) < n_first
    pick = lambda a, b: jnp.where(first, a[...], b[...])
    gate = pick(gate0_ref, gate1_ref)
    ya = _dot(pick(oa0_ref, oa1_ref), wa_ref[...])
    yb = _dot(pick(ob0_ref, ob1_ref), wb_ref[...])
    ga = gate[:, :D_MODEL].astype(F32)
    gb = gate[:, D_MODEL:].astype(F32)
    m = _sigmoid(ga) * ya + _sigmoid(gb) * yb
    x1 = pick(x0_ref, x1in_ref) + _dot(m.astype(BF16), wo_ref[...])
    x1_ref[...] = x1
    h2 = _rms(x1) * nw_ref[...]
    for j in range(D_MODEL // LANES):
        h2_ref[:, j, :] = h2[:, j * LANES:(j + 1) * LANES]
    lg = _dot_split(h2, wr_ref[...]) + br_ref[...]
    lane = lax.broadcasted_iota(jnp.int32, lg.shape, 1).astype(F32)
    vals, idxs = [], []
    for _ in range(TOP_K):
        top = jnp.max(lg, axis=-1, keepdims=True)
        idx = jnp.min(jnp.where(lg == top, lane, float(LANES)), axis=-1, keepdims=True)
        vals.append(top)
        idxs.append(idx)
        lg = jnp.where(lane == idx, _NO_EXPERT, lg)
    es = [jnp.exp(v - vals[0]) for v in vals]
    inv_total = 1.0 / functools.reduce(lambda a, b: a + b, es)
    ti = jnp.zeros_like(lg)
    tw = jnp.zeros_like(lg)
    for k in range(TOP_K):
        ti = jnp.where(lane == float(k), idxs[k], ti)
        tw = jnp.where(lane == float(k), es[k] * inv_total, tw)
    ti_ref[...] = ti.astype(jnp.int32)
    tw_ref[...] = tw


def _merge(group0, group1, weights):
    r0, r1 = group0[3].shape[0], group1[3].shape[0]
    tm = _pick_tile(math.gcd(r0, r1), (128, 64, 32, 16, 8))
    n0, total = r0 // tm, r0 + r1
    widths = (RET_V_W, GDN_V_W, 2 * D_MODEL, D_MODEL)
    specs0 = [pl.BlockSpec((tm, n), lambda i: (jnp.minimum(i, n0 - 1), 0)) for n in widths]
    specs1 = [pl.BlockSpec((tm, n), lambda i: (jnp.maximum(i - n0, 0), 0)) for n in widths]
    out = lambda n: pl.BlockSpec((tm, n), lambda i: (i, 0))
    sub = D_MODEL // LANES
    sq = (D_MODEL, D_MODEL)
    return pl.pallas_call(
        functools.partial(_merge_kernel, n_first=n0),
        grid=(total // tm,),
        in_specs=specs0 + specs1 + [_resident(sq), _resident(sq), _resident(sq), _resident((1, D_MODEL)),
                                    _resident((D_MODEL, LANES)), _resident((1, LANES))],
        out_specs=[out(D_MODEL), pl.BlockSpec((tm, sub, LANES), lambda i: (i, 0, 0)), out(LANES), out(LANES)],
        out_shape=[jax.ShapeDtypeStruct((total, D_MODEL), F32), jax.ShapeDtypeStruct((total, sub, LANES), F32),
                   jax.ShapeDtypeStruct((total, LANES), jnp.int32), jax.ShapeDtypeStruct((total, LANES), F32)],
        compiler_params=_params(("parallel",)),
        name="merge",
    )(*group0, *group1, *weights)


def _merge_weights(w_a, w_b, w_o, ffn_norm_w, w_router, b_router):
    wr = jnp.pad(w_router.astype(F32), ((0, 0), (0, LANES - N_EXPERTS)))
    br = jnp.pad(b_router.astype(F32), (0, LANES - N_EXPERTS), constant_values=_NO_EXPERT).reshape(1, LANES)
    return (w_a.astype(BF16), w_b.astype(BF16), w_o.astype(BF16), ffn_norm_w.astype(F32).reshape(1, D_MODEL), wr, br)


def _route(top_i, gate, n_tokens):
    rows = MOE_ROWS
    n = n_tokens * TOP_K
    flat_e = top_i.reshape(n).astype(jnp.int32)
    bits = max(1, (n - 1).bit_length())
    assert bits + (N_EXPERTS - 1).bit_length() <= 31
    order = lax.sort((flat_e << bits) | jnp.arange(n, dtype=jnp.int32)) & ((1 << bits) - 1)
    counts = jnp.sum((flat_e[:, None] == jnp.arange(N_EXPERTS, dtype=jnp.int32)[None, :]).astype(jnp.int32), axis=0)
    start = jnp.cumsum(counts) - counts
    pcounts = (counts + rows - 1) // rows * rows
    pend = jnp.cumsum(pcounts)
    pstart = pend - pcounts
    nb = -(-n // rows) + N_EXPERTS
    blk = jnp.arange(nb, dtype=jnp.int32)
    block_e = jnp.minimum(jnp.sum((pend[None, :] <= (blk * rows)[:, None]).astype(jnp.int32), axis=1),
                          N_EXPERTS - 1).astype(jnp.int32)
    nb_used = (pend[-1] // rows).astype(jnp.int32).reshape(1)
    within = (blk * rows - pstart[block_e])[:, None] + jnp.arange(rows, dtype=jnp.int32)[None, :]
    valid = jnp.logical_and(within < counts[block_e][:, None], (blk < nb_used[0])[:, None])
    flat = order[jnp.clip(start[block_e][:, None] + within, 0, n - 1)]
    spare = n + (blk % MOE_RING)[:, None] * rows + jnp.arange(rows, dtype=jnp.int32)[None, :]
    src = jnp.where(valid, flat // TOP_K, 0)
    dst = jnp.where(valid, flat, spare)
    ahead = lambda k: jnp.concatenate([src[k:]] + [src[-1:]] * k, axis=0)
    slab = jnp.concatenate([src, dst, ahead(1), ahead(2)], axis=1)
    row_w = jnp.where(valid, gate[:, :TOP_K].reshape(n)[flat], 0.0)
    row_w = jnp.broadcast_to(row_w[:, :, None], (nb, rows, LANES))
    return block_e, nb_used, slab, row_w


def _expert_kernel(be_ref, nbu_ref, slab_ref, h_ref, roww_ref, wgu_ref, bgu_ref, wd_ref, bd_ref, y_ref,
                   idx0_ref, idx1_ref, idx2_ref, xbuf_ref, ybuf_ref, wgu_bf_ref, wd_bf_ref, isem, gsem, ssem):
    rows = MOE_ROWS
    ring = MOE_RING
    idx_refs = (idx0_ref, idx1_ref, idx2_ref)
    assert len(idx_refs) == ring
    sub = D_MODEL // LANES
    i = pl.program_id(0)
    nbu = nbu_ref[0]
    slot = i % ring
    active = i < nbu
    n_real = y_ref.shape[0] - ring * rows

    def slab_copy(blk, sl):
        return pltpu.make_async_copy(slab_ref.at[blk], idx_refs[sl], isem.at[sl])

    def gather_row(tok, sl, r):
        return pltpu.make_async_copy(h_ref.at[tok], xbuf_ref.at[sl, :, r, :], gsem.at[sl])

    def scatter_row(sl, r, f):
        return pltpu.make_async_copy(ybuf_ref.at[sl, :, r, :], y_ref.at[f], ssem.at[sl])

    def gather_wait(sl):
        pltpu.make_async_copy(xbuf_ref.at[sl], xbuf_ref.at[sl], gsem.at[sl]).wait()

    def scatter_wait(sl):
        pltpu.make_async_copy(ybuf_ref.at[sl], ybuf_ref.at[sl], ssem.at[sl]).wait()

    @pl.when(i == 0)
    def _():
        slab_copy(0, 0).start()
        ybuf_ref[...] = jnp.zeros_like(ybuf_ref)
        for sl in range(ring):
            def fill(r, carry, sl=sl):
                scatter_row(sl, r, n_real + sl * rows + r).start()
                return carry
            lax.fori_loop(0, rows, fill, 0)

    def block(sl):
        nxt, nxt2 = (sl + 1) % ring, (sl + 2) % ring
        idx_ref = idx_refs[sl]
        slab_copy(i, sl).wait()

        @pl.when(i + 1 < nbu)
        def _():
            slab_copy(i + 1, nxt).start()

        if sl == 0:
            @pl.when(i == 0)
            def _():
                def first(r, carry):
                    gather_row(idx_ref[r], 0, r).start()
                    gather_row(idx_ref[2 * rows + r], 1, r).start()
                    return carry
                lax.fori_loop(0, rows, first, 0)

        changed = jnp.logical_or(i == 0, be_ref[i] != be_ref[jnp.maximum(i - 1, 0)])

        @pl.when(changed)
        def _():
            wgu_bf_ref[...] = wgu_ref[0].astype(BF16)
            wd_bf_ref[...] = wd_ref[0].astype(BF16)

        gather_wait(sl)
        scatter_wait(sl)
        xb = jnp.concatenate([xbuf_ref[sl, j] for j in range(sub)], axis=1).astype(BF16)
        hb = _dot(xb, wgu_bf_ref[...]) + bgu_ref[0]
        for r in range(rows):
            gather_row(idx_ref[3 * rows + r], nxt2, r).start(priority=r % 2)
        glu = jnp.minimum(hb[:, :D_FF], SWIGLU_LIMIT)
        lin = jnp.clip(hb[:, D_FF:], -SWIGLU_LIMIT, SWIGLU_LIMIT)
        act = (glu * _sigmoid(SWIGLU_ALPHA * glu) * (lin + 1.0)).astype(BF16)
        yv = (_dot(act, wd_bf_ref[...]) + bd_ref[0]) * roww_ref[0, :, 0:1]
        for j in range(sub):
            ybuf_ref[sl, j] = yv[:, j * LANES:(j + 1) * LANES]
        for r in range(rows):
            scatter_row(sl, r, idx_ref[rows + r]).start(priority=(r + 1) % 2)

        @pl.when(i == nbu - 1)
        def _():
            for s in (sl, nxt, nxt2):
                scatter_wait(s)
            gather_wait(nxt)
            gather_wait(nxt2)

    for sl in range(ring):
        pl.when(jnp.logical_and(active, slot == sl))(functools.partial(block, sl))


def _experts(h2, block_e, nb_used, slab, row_w, w_gate_up, b_gate_up, w_down, b_down):
    t = h2.shape[0]
    rows = MOE_ROWS
    ring = MOE_RING
    assert (ring * rows) % TOP_K == 0
    nb = slab.shape[0]
    sub = D_MODEL // LANES
    grid_spec = pltpu.PrefetchScalarGridSpec(
        num_scalar_prefetch=2,
        grid=(nb,),
        in_specs=[pl.BlockSpec(memory_space=pl.ANY),
                  pl.BlockSpec(memory_space=pl.ANY),
                  pl.BlockSpec((1, rows, LANES), lambda i, be, nbu: (i, 0, 0)),
                  pl.BlockSpec((1, D_MODEL, 2 * D_FF), lambda i, be, nbu: (be[i], 0, 0)),
                  pl.BlockSpec((1, 1, 2 * D_FF), lambda i, be, nbu: (be[i], 0, 0)),
                  pl.BlockSpec((1, D_FF, D_MODEL), lambda i, be, nbu: (be[i], 0, 0)),
                  pl.BlockSpec((1, 1, D_MODEL), lambda i, be, nbu: (be[i], 0, 0))],
        out_specs=pl.BlockSpec(memory_space=pl.ANY),
        scratch_shapes=[pltpu.SMEM((slab.shape[1],), jnp.int32)] * ring + [
                        pltpu.VMEM((ring, sub, rows, LANES), F32),
                        pltpu.VMEM((ring, sub, rows, LANES), F32),
                        pltpu.VMEM((D_MODEL, 2 * D_FF), BF16),
                        pltpu.VMEM((D_FF, D_MODEL), BF16),
                        pltpu.SemaphoreType.DMA((ring,)),
                        pltpu.SemaphoreType.DMA((ring,)),
                        pltpu.SemaphoreType.DMA((ring,))])
    return pl.pallas_call(
        _expert_kernel,
        grid_spec=grid_spec,
        out_shape=jax.ShapeDtypeStruct((t * TOP_K + ring * rows, sub, LANES), F32),
        compiler_params=_params(("arbitrary",)),
        name="experts",
    )(block_e, nb_used, slab, h2, row_w, w_gate_up, b_gate_up.reshape(N_EXPERTS, 1, 2 * D_FF), w_down,
      b_down.reshape(N_EXPERTS, 1, D_MODEL))


def _combine_kernel(y_ref, x1_ref, nw_ref, o_ref, sum_ref, *, final):
    total = y_ref[:, 0]
    for k in range(1, TOP_K):
        total = total + y_ref[:, k]
    sum_ref[...] = total
    acc = x1_ref[...] + jnp.concatenate([sum_ref[:, j, :] for j in range(D_MODEL // LANES)], axis=1)
    o_ref[...] = _rms(acc) * nw_ref[...] if final else acc


def _combine(y, x1, row0, rows, norm_w, final):
    tm = _row_tile(rows, row0, (256, 128, 64, 32, 16, 8))
    off = row0 // tm
    sub = D_MODEL // LANES
    y4 = y.reshape(y.shape[0] // TOP_K, TOP_K, sub, LANES)
    return pl.pallas_call(
        functools.partial(_combine_kernel, final=final),
        grid=(rows // tm,),
        in_specs=[pl.BlockSpec((tm, TOP_K, sub, LANES), lambda i: (off + i, 0, 0, 0)),
                  pl.BlockSpec((tm, D_MODEL), lambda i: (off + i, 0)), _resident((1, D_MODEL))],
        out_specs=pl.BlockSpec((tm, D_MODEL), lambda i: (i, 0)),
        out_shape=jax.ShapeDtypeStruct((rows, D_MODEL), F32),
        scratch_shapes=[pltpu.VMEM((tm, sub, LANES), F32)],
        compiler_params=_params(("parallel",)),
        name="combine",
    )(y4, x1, norm_w.astype(F32).reshape(1, D_MODEL))


def kernel(x_prompt, x_sample, state_ret, state_gdn, state_conv, attn_norm_w, w_in, conv_w, a_log, dt_bias, gdn_norm_w, w_branch_a, w_branch_b, w_out, ffn_norm_w, w_router, b_router, w_gate_up, b_gate_up, w_down, b_down, final_norm_w):
    bp, lp, d = x_prompt.shape
    bs, ls, _ = x_sample.shape
    assert ls == 1 and d == D_MODEL and lp % RET_CHUNK == 0 and bs % SAMPLE_TILE == 0
    depth = w_in.shape[0]
    tp = bp * lp
    t = tp + bs
    xp, xs = x_prompt.reshape(tp, d).astype(F32), x_sample.reshape(bs, d).astype(F32)
    rp, gp, cp, rs, gs, cs = [], [], [], [], [], []
    for l in range(depth):
        wb = jnp.pad(w_in[l].astype(BF16), ((0, 0), (0, -w_in.shape[2] % LANES)))
        pp_ret, pp_gdn, pp_gate, pp_ab = _inproj(xp, attn_norm_w[l], wb)
        ps_ret, ps_gdn, ps_gate, ps_ab = _inproj(xs, attn_norm_w[l], wb)
        op_ret, s_ret_p = _ret_prompt(pp_ret, bp, lp)
        os_ret, s_ret_s = _ret_sample(ps_ret.astype(F32), state_ret[l].astype(F32))
        op_gdn, s_gdn_p, conv_p = _gdn_prompt(pp_gdn, pp_ab, bp, lp, conv_w[l], a_log[l], dt_bias[l], gdn_norm_w[l])
        os_gdn, s_gdn_s, conv_s = _gdn_sample(ps_gdn.astype(F32), ps_ab, state_gdn[l].astype(F32), state_conv[l],
                                              conv_w[l], a_log[l], dt_bias[l], gdn_norm_w[l])
        mw = _merge_weights(w_branch_a[l], w_branch_b[l], w_out[l], ffn_norm_w[l], w_router[l], b_router[l])
        x1, h2, top_i, gate = _merge((op_ret, op_gdn, pp_gate, xp), (os_ret, os_gdn, ps_gate, xs), mw)
        block_e, nb_used, slab, row_w = _route(top_i[:, :TOP_K], gate, t)
        y = _experts(h2, block_e, nb_used, slab, row_w, w_gate_up[l], b_gate_up[l], w_down[l], b_down[l])
        last = l == depth - 1
        norm_w = final_norm_w if last else jnp.ones((d,), F32)
        xp = _combine(y, x1, 0, tp, norm_w, last)
        xs = _combine(y, x1, tp, bs, norm_w, last)
        rp.append(s_ret_p); gp.append(s_gdn_p); cp.append(conv_p)
        rs.append(s_ret_s); gs.append(s_gdn_s); cs.append(conv_s)
    y_prompt = xp.reshape(bp, lp, d).astype(x_prompt.dtype)
    y_sample = xs.reshape(bs, ls, d).astype(x_sample.dtype)
    return (y_prompt, y_sample,
            jnp.stack(rp).astype(state_ret.dtype), jnp.stack(gp).astype(state_gdn.dtype),
            jnp.stack(cp).astype(state_conv.dtype),
            jnp.stack(rs).astype(state_ret.dtype), jnp.stack(gs).astype(state_gdn.dtype),
            jnp.stack(cs).astype(state_conv.dtype))
```

```python
import functools
import math

import numpy as np
import jax
import jax.numpy as jnp
from jax import lax
from jax.experimental import pallas as pl
from jax.experimental.pallas import tpu as pltpu

F32 = jnp.float32
BF16 = jnp.bfloat16
HIGHEST = lax.Precision.HIGHEST

D_MODEL = 1024
PAST_LEN = 16384
RET_HEADS, RET_DK, RET_DV = 4, 128, 256
RET_QK_W, RET_V_W = RET_HEADS * RET_DK, RET_HEADS * RET_DV
RET_CHUNK = 128
ROPE_BASE = 10000.0
GDN_HEADS, GDN_DK, GDN_DV = 8, 128, 128
GDN_QK_W, GDN_V_W = GDN_HEADS * GDN_DK, GDN_HEADS * GDN_DV
GDN_CHUNK = 64
CONV_W = 4
CONV_CH = 2 * GDN_QK_W + GDN_V_W
N_EXPERTS = 32
TOP_K = 4
D_FF = D_MODEL
SWIGLU_LIMIT = 7.0
SWIGLU_ALPHA = 1.702
NORM_EPS = 1e-6

_RET_W = 2 * RET_QK_W + 2 * RET_V_W
_GDN_W = CONV_CH + GDN_V_W
_AB_OFF = _RET_W + _GDN_W
_GATE_OFF = _AB_OFF + 2 * GDN_HEADS

LANES = 128
VMEM_LIMIT = 56 * 1024 * 1024
MERGE_ROWS = 256
MOE_ROWS = 256
MOE_RING = 3
SAMPLE_TILE = 8
RET_STEP_ROWS = 256
GDN_PREP_ROWS = 128
GDN_SCAN_ROWS = 128


def _pick_tile(n, candidates):
    for c in candidates:
        if n % c == 0:
            return c
    raise ValueError(f"no tile in {candidates} divides {n}")


def _params(sem, vmem=VMEM_LIMIT):
    return pltpu.CompilerParams(dimension_semantics=sem, vmem_limit_bytes=vmem)


def _resident(shape):
    nd = len(shape)
    return pl.BlockSpec(shape, lambda *_: (0,) * nd, pipeline_mode=pl.Buffered(1))


def _silu(x):
    return x * (1.0 / (1.0 + jnp.exp(-x)))


def _sigmoid(x):
    return 1.0 / (1.0 + jnp.exp(-x))


def _softplus(x):
    return jnp.maximum(x, 0.0) + jnp.log1p(jnp.exp(-jnp.abs(x)))


def _rms(x):
    return x * lax.rsqrt(jnp.mean(x * x, axis=-1, keepdims=True) + NORM_EPS)


def _dot(a, b):
    return jnp.dot(a, b, preferred_element_type=F32)


def _dot_nt(a, b):
    return lax.dot_general(a, b, (((1,), (1,)), ((), ())), preferred_element_type=F32)


def _dot_tn(a, b):
    return lax.dot_general(a, b, (((0,), (0,)), ((), ())), preferred_element_type=F32)


def _dot_hi(a, b):
    return jnp.dot(a, b, preferred_element_type=F32, precision=HIGHEST)


def _dot_split(a, b):
    a_hi, b_hi = a.astype(BF16), b.astype(BF16)
    a_lo = (a - a_hi.astype(F32)).astype(BF16)
    b_lo = (b - b_hi.astype(F32)).astype(BF16)
    return _dot(a_hi, b_hi) + (_dot(a_hi, b_lo) + _dot(a_lo, b_hi))


def _row_tile(rows, row0, candidates):
    return _pick_tile(math.gcd(rows, row0) if row0 else rows, candidates)


def _inproj_kernel(x_ref, nw_ref, w_ref, oret_ref, ogdn_ref, ogate_ref, oab_ref):
    h = (_rms(x_ref[...]) * nw_ref[...]).astype(BF16)
    oret_ref[...] = _dot(h, w_ref[:, :_RET_W]).astype(BF16)
    ogdn_ref[...] = _dot(h, w_ref[:, _RET_W:_AB_OFF]).astype(BF16)
    tail = _dot(h, w_ref[:, _AB_OFF:])
    oab_ref[...] = tail[:, :LANES]
    ogate_ref[...] = tail[:, _GATE_OFF - _AB_OFF:_GATE_OFF - _AB_OFF + 2 * D_MODEL].astype(BF16)


def _inproj(x, norm_w, wb):
    rows = x.shape[0]
    tm = _pick_tile(rows, (256, 128, 64, 32, 16, 8))
    row = lambda n: pl.BlockSpec((tm, n), lambda i: (i, 0))
    return pl.pallas_call(
        _inproj_kernel,
        grid=(rows // tm,),
        in_specs=[row(D_MODEL), _resident((1, D_MODEL)), _resident(wb.shape)],
        out_specs=[row(_RET_W), row(_GDN_W), row(2 * D_MODEL), row(LANES)],
        out_shape=[jax.ShapeDtypeStruct((rows, _RET_W), BF16), jax.ShapeDtypeStruct((rows, _GDN_W), BF16),
                   jax.ShapeDtypeStruct((rows, 2 * D_MODEL), BF16), jax.ShapeDtypeStruct((rows, LANES), F32)],
        compiler_params=_params(("parallel",)),
        name="inproj",
    )(x, norm_w.reshape(1, D_MODEL), wb)


def _ret_log_gamma():
    return np.log1p(-np.exp2(-5.0 - np.arange(RET_HEADS, dtype=np.float64)))


def _rope_tables(pos):
    half = RET_DK // 2
    inv = 1.0 / (ROPE_BASE ** (jnp.arange(half, dtype=F32) / half))
    ang = pos.astype(F32)[:, None] * inv[None, :]
    cos, sin = jnp.cos(ang), jnp.sin(ang)
    return jnp.concatenate([cos, cos], axis=-1), jnp.concatenate([-sin, sin], axis=-1)


def _rotary(x, cos, sin):
    return x * cos + pltpu.roll(x, RET_DK // 2, 1) * sin


def _ret_prompt_kernel(q_ref, k_ref, v_ref, g_ref, cos_ref, sin_ref, dmask_ref, qdec_ref, kdec_ref,
                       o_ref, s_ref, *, gammas):
    @pl.when(pl.program_id(1) == 0)
    def _():
        s_ref[...] = jnp.zeros_like(s_ref)

    c = RET_CHUNK
    heads = range(RET_HEADS)
    qk = [slice(h * RET_DK, (h + 1) * RET_DK) for h in heads]
    vv = [slice(h * RET_DV, (h + 1) * RET_DV) for h in heads]
    s = [s_ref[0, h] for h in heads]
    for ck in range(q_ref.shape[0] // c):
        r = slice(ck * c, (ck + 1) * c)
        cos, sin = cos_ref[r, :], sin_ref[r, :]
        q = [_rotary(q_ref[r, qk[h]].astype(F32), cos, sin) for h in heads]
        k = [_rotary(k_ref[r, qk[h]].astype(F32), cos, sin) * (RET_DK ** -0.5) for h in heads]
        v = [v_ref[r, vv[h]] for h in heads]
        qb = [x.astype(BF16) for x in q]
        inner = [_dot_nt(qb[h], k[h].astype(BF16)) * dmask_ref[h] for h in heads]
        cross = [_dot((q[h] * qdec_ref[h]).astype(BF16), s[h].astype(BF16)) for h in heads]
        upd = [_dot_tn((k[h] * kdec_ref[h]).astype(BF16), v[h]) for h in heads]
        o = [_dot(inner[h].astype(BF16), v[h]) + cross[h] for h in heads]
        s = [s[h] * gammas[h] + upd[h] for h in heads]
        for h in heads:
            o_ref[r, vv[h]] = (_rms(o[h]) * _silu(g_ref[r, vv[h]].astype(F32))).astype(BF16)
    for h in heads:
        s_ref[0, h] = s[h]


def _ret_prompt(p_ret, batch, seq):
    c = RET_CHUNK
    step = RET_STEP_ROWS if seq % RET_STEP_ROWS == 0 else c
    n = seq // step
    lg = _ret_log_gamma()
    idx = np.arange(c, dtype=np.float64)
    diff = idx[:, None] - idx[None, :]
    dmask = np.where(diff >= 0, np.exp(np.maximum(diff, 0.0)[None] * lg[:, None, None]), 0.0)
    qdec = np.broadcast_to(np.exp((idx + 1.0)[None, :] * lg[:, None])[:, :, None], (RET_HEADS, c, RET_DK))
    kdec = np.broadcast_to(np.exp((c - 1.0 - idx)[None, :] * lg[:, None])[:, :, None], (RET_HEADS, c, RET_DK))
    gammas = tuple(float(g) for g in np.exp(c * lg))
    cos, sin = _rope_tables(jnp.arange(seq, dtype=jnp.int32))
    tab = lambda: _resident((RET_HEADS, c, RET_DK))
    return pl.pallas_call(
        functools.partial(_ret_prompt_kernel, gammas=gammas),
        grid=(batch, n),
        in_specs=[pl.BlockSpec((step, RET_QK_W), lambda b, j: (b * n + j, 0)),
                  pl.BlockSpec((step, RET_QK_W), lambda b, j: (b * n + j, 1)),
                  pl.BlockSpec((step, RET_V_W), lambda b, j: (b * n + j, 1)),
                  pl.BlockSpec((step, RET_V_W), lambda b, j: (b * n + j, 2)),
                  pl.BlockSpec((step, RET_DK), lambda b, j: (j, 0)),
                  pl.BlockSpec((step, RET_DK), lambda b, j: (j, 0)),
                  tab(), tab(), tab()],
        out_specs=[pl.BlockSpec((step, RET_V_W), lambda b, j: (b * n + j, 0)),
                   pl.BlockSpec((1, RET_HEADS, RET_DK, RET_DV), lambda b, j: (b, 0, 0, 0))],
        out_shape=[jax.ShapeDtypeStruct((batch * seq, RET_V_W), BF16),
                   jax.ShapeDtypeStruct((batch, RET_HEADS, RET_DK, RET_DV), F32)],
        compiler_params=_params(("parallel", "arbitrary")),
        name="ret_prompt",
    )(p_ret, p_ret, p_ret, p_ret, cos, sin, jnp.asarray(dmask, F32), jnp.asarray(qdec, F32),
      jnp.asarray(kdec, F32))


def _columns(x):
    n = x.shape[0]
    if n < LANES:
        x = jnp.concatenate([x, jnp.zeros((LANES - n, x.shape[1]), x.dtype)], axis=0)
    return x.T


def _ret_sample_kernel(p_ref, cos_ref, sin_ref, s_ref, o_ref, so_ref, *, gammas):
    cos, sin = cos_ref[...], sin_ref[...]
    nb = p_ref.shape[0]
    for h in range(RET_HEADS):
        qk = slice(h * RET_DK, (h + 1) * RET_DK)
        q = _rotary(p_ref[:, qk], cos, sin)
        k = _rotary(p_ref[:, RET_QK_W + h * RET_DK:RET_QK_W + (h + 1) * RET_DK], cos, sin) * (RET_DK ** -0.5)
        v = p_ref[:, 2 * RET_QK_W + h * RET_DV:2 * RET_QK_W + (h + 1) * RET_DV]
        g = p_ref[:, 2 * RET_QK_W + RET_V_W + h * RET_DV:2 * RET_QK_W + RET_V_W + (h + 1) * RET_DV]
        qk_dot = jnp.sum(q * k, axis=-1, keepdims=True)
        qt, kt = _columns(q), _columns(k)
        rows = []
        for j in range(nb):
            s = s_ref[j, h]
            qs = jnp.sum(qt[:, j:j + 1] * s, axis=0, keepdims=True)
            rows.append(qk_dot[j:j + 1] * v[j:j + 1] + gammas[h] * qs)
            so_ref[j, h] = s * gammas[h] + kt[:, j:j + 1] * v[j:j + 1]
        o = jnp.concatenate(rows, axis=0)
        o_ref[:, h * RET_DV:(h + 1) * RET_DV] = (_rms(o) * _silu(g)).astype(BF16)


def _ret_sample(p_ret_s, state):
    ts = p_ret_s.shape[0]
    sb = SAMPLE_TILE
    gammas = tuple(float(g) for g in np.exp(_ret_log_gamma()))
    cos, sin = _rope_tables(jnp.full((1,), PAST_LEN, jnp.int32))
    st = pl.BlockSpec((sb, RET_HEADS, RET_DK, RET_DV), lambda i: (i, 0, 0, 0))
    return pl.pallas_call(
        functools.partial(_ret_sample_kernel, gammas=gammas),
        grid=(ts // sb,),
        in_specs=[pl.BlockSpec((sb, _RET_W), lambda i: (i, 0)), _resident((1, RET_DK)), _resident((1, RET_DK)), st],
        out_specs=[pl.BlockSpec((sb, RET_V_W), lambda i: (i, 0)), st],
        out_shape=[jax.ShapeDtypeStruct((ts, RET_V_W), BF16), jax.ShapeDtypeStruct(state.shape, F32)],
        compiler_params=_params(("parallel",)),
        name="ret_sample",
    )(p_ret_s, cos, sin, state)


def _l2norm(x):
    return x * lax.rsqrt(jnp.sum(x * x, axis=-1, keepdims=True) + NORM_EPS)


def _bdot(a, b):
    return _dot(a.astype(BF16), b.astype(BF16))


def _chunk_masks(c):
    ri = lax.broadcasted_iota(jnp.int32, (c, c), 0)
    ci = lax.broadcasted_iota(jnp.int32, (c, c), 1)
    eye = (ri == ci).astype(F32)
    diag16 = (ri // 16 == ci // 16).astype(F32)
    low32 = jnp.logical_and(ri // 32 == ci // 32, ri // 16 > ci // 16).astype(F32)
    low64 = (ri // 32 > ci // 32).astype(F32)
    return ri >= ci, ri > ci, eye, diag16, low32, low64


def _unit_lower_inverse(a, eye, diag16, low32, low64):
    many = lambda f, *ls: [f(*args) for args in zip(*ls)]
    n = [-(x * diag16) for x in a]
    n2 = many(_bdot, n, n)
    n3 = many(_bdot, n, n2)
    n4 = many(_bdot, n2, n2)
    n8 = many(_bdot, n4, n4)
    x = [eye + p + q + r for p, q, r in zip(n, n2, n3)]
    x = many(lambda u, v: u + v, x, many(_bdot, x, n4))
    x = many(lambda u, v: u + v, x, many(_bdot, x, n8))
    for mask in (low32, low64):
        r = many(_bdot, [y * mask for y in a], x)
        x = many(lambda u, v: u - v, x, many(_bdot, x, r))
    return x


def _gdn_gates(ab, alog, dtb):
    g = -jnp.exp(alog) * _softplus(ab + dtb)
    return g, _sigmoid(ab)


def _gdn_prep_kernel(x_ref, prev_ref, ab_ref, cw_ref, alog_ref, dtb_ref,
                     u_ref, w_ref, qg_ref, kg_ref, at_ref, eg_ref, cv_ref, act_ref):
    c = GDN_CHUNK
    rows = x_ref.shape[0]
    x = x_ref[...]
    halo = prev_ref.shape[0]
    prev = jnp.where(pl.program_id(1) == 0, jnp.zeros_like(prev_ref), prev_ref[...])
    xcat = jnp.concatenate([prev, x], axis=0)
    ti = lax.broadcasted_iota(jnp.int32, (rows, rows + halo), 0)
    ui = lax.broadcasted_iota(jnp.int32, (rows, rows + halo), 1)
    xf = x.astype(F32)
    acc = xf * cw_ref[CONV_W - 1:CONV_W, :]
    for i in range(CONV_W - 1):
        shift = (ui == ti + (halo - (CONV_W - 1) + i)).astype(BF16)
        acc = acc + _dot(shift, xcat) * cw_ref[i:i + 1, :]
    cv_ref[0] = xf[rows - (CONV_W - 1):, :]
    act_ref[...] = _silu(acc)

    g_all, beta_all = _gdn_gates(ab_ref[...], alog_ref[...], dtb_ref[...])
    lower, strict, eye, diag16, low32, low64 = _chunk_masks(c)
    gc_all, gr_all = [], []
    for ck in range(rows // c):
        gc_ck = _dot_hi(lower.astype(F32), g_all[ck * c:(ck + 1) * c])
        gc_all.append(gc_ck)
        gr_all.append(_columns(gc_ck))
        eg_ref[ck] = jnp.exp(gc_ck[c - 1:c, :])
    chains = [(ck, h) for ck in range(rows // c) for h in range(GDN_HEADS)]
    rs = [slice(ck * c, (ck + 1) * c) for ck, _ in chains]
    sl = [slice(h * GDN_DK, (h + 1) * GDN_DK) for _, h in chains]
    nc = range(len(chains))
    q = [_l2norm(act_ref[rs[i], sl[i]]) * (GDN_DK ** -0.5) for i in nc]
    k = [_l2norm(act_ref[rs[i], GDN_QK_W + sl[i].start:GDN_QK_W + sl[i].stop]) for i in nc]
    v = [act_ref[rs[i], 2 * GDN_QK_W + sl[i].start:2 * GDN_QK_W + sl[i].stop] for i in nc]
    beta = [beta_all[rs[i], GDN_HEADS + h:GDN_HEADS + h + 1] for i, (_, h) in enumerate(chains)]
    gc = [gc_all[ck][:, h:h + 1] for ck, h in chains]
    gr = [gr_all[ck][h:h + 1, :c] for ck, h in chains]
    decay = [jnp.exp(jnp.where(lower, gc[i] - gr[i], -jnp.inf)) for i in nc]
    exp_g = [jnp.exp(x) for x in gc]
    kb = [k[i] * beta[i] for i in nc]
    kbf = [x.astype(BF16) for x in k]
    a = [_dot_nt(kb[i].astype(BF16), kbf[i]) * jnp.where(strict, decay[i], 0.0) for i in nc]
    attn = [_dot_nt(q[i].astype(BF16), kbf[i]) * decay[i] for i in nc]
    t = _unit_lower_inverse(a, eye, diag16, low32, low64)
    uu = [_bdot(t[i], v[i] * beta[i]) for i in nc]
    ww = [_bdot(t[i], kb[i] * exp_g[i]) for i in nc]
    for i in nc:
        u_ref[rs[i], sl[i]] = uu[i]
        w_ref[rs[i], sl[i]] = ww[i].astype(BF16)
        at_ref[rs[i], sl[i]] = jnp.concatenate([attn[i], jnp.zeros((c, GDN_DK - c), F32)], axis=1).astype(BF16)
        qg_ref[rs[i], sl[i]] = (q[i] * exp_g[i]).astype(BF16)
        kg_ref[rs[i], sl[i]] = (k[i] * jnp.exp(gc[i][c - 1:c, :] - gc[i])).astype(BF16)


def _gdn_scan_kernel(u_ref, w_ref, qg_ref, kg_ref, at_ref, eg_ref, z_ref, nw_ref, o_ref, s_ref):
    c = GDN_CHUNK

    @pl.when(pl.program_id(1) == 0)
    def _():
        s_ref[...] = jnp.zeros_like(s_ref)

    nw = nw_ref[...]
    heads = range(GDN_HEADS)
    sl = [slice(h * GDN_DK, (h + 1) * GDN_DK) for h in heads]
    s = [s_ref[0, h] for h in heads]
    for ck in range(u_ref.shape[0] // c):
        r = slice(ck * c, (ck + 1) * c)
        eg = eg_ref[ck]
        sb = [x.astype(BF16) for x in s]
        ws = [_dot(w_ref[r, sl[h]], sb[h]) for h in heads]
        qs = [_dot(qg_ref[r, sl[h]], sb[h]) for h in heads]
        vnb = [(u_ref[r, sl[h]] - ws[h]).astype(BF16) for h in heads]
        o = [qs[h] + _dot(at_ref[r, h * GDN_DK:h * GDN_DK + c], vnb[h]) for h in heads]
        s = [s[h] * eg[:, h:h + 1] + _dot_tn(kg_ref[r, sl[h]], vnb[h]) for h in heads]
        for h in heads:
            o_ref[r, sl[h]] = (_rms(o[h]) * nw * _silu(z_ref[r, sl[h]].astype(F32))).astype(BF16)
    for h in heads:
        s_ref[0, h] = s[h]


def _gdn_prompt(p_gdn, p_ab, batch, seq, conv_w, a_log, dt_bias, gdn_norm_w):
    c = GDN_CHUNK
    rows = GDN_PREP_ROWS
    t = batch * seq
    nt = seq // rows
    alog = jnp.pad(a_log.astype(F32), (0, LANES - GDN_HEADS)).reshape(1, LANES)
    dtb = jnp.pad(dt_bias.astype(F32), (0, LANES - GDN_HEADS)).reshape(1, LANES)
    wide = lambda: pl.BlockSpec((rows, GDN_V_W), lambda b, j: (b * nt + j, 0))
    u, w, qg, kg, at, eg, conv_new = pl.pallas_call(
        _gdn_prep_kernel,
        grid=(batch, nt),
        in_specs=[pl.BlockSpec((rows, CONV_CH), lambda b, j: (b * nt + j, 0)),
                  pl.BlockSpec((16, CONV_CH), lambda b, j: (jnp.maximum((b * nt + j) * (rows // 16) - 1, 0), 0)),
                  pl.BlockSpec((rows, LANES), lambda b, j: (b * nt + j, 0)),
                  _resident((CONV_W, CONV_CH)), _resident((1, LANES)), _resident((1, LANES))],
        out_specs=[wide(), wide(), wide(), wide(), wide(),
                   pl.BlockSpec((rows // c, 1, LANES), lambda b, j: (b * nt + j, 0, 0)),
                   pl.BlockSpec((1, CONV_W - 1, CONV_CH), lambda b, j: (b, 0, 0))],
        out_shape=[jax.ShapeDtypeStruct((t, GDN_V_W), F32)] + [jax.ShapeDtypeStruct((t, GDN_V_W), BF16)] * 4
        + [jax.ShapeDtypeStruct((t // c, 1, LANES), F32),
           jax.ShapeDtypeStruct((batch, CONV_W - 1, CONV_CH), F32)],
        scratch_shapes=[pltpu.VMEM((rows, CONV_CH), F32)],
        compiler_params=_params(("parallel", "arbitrary")),
        name="gdn_prep",
    )(p_gdn, p_gdn, p_ab, conv_w.astype(F32), alog, dtb)
    srows = GDN_SCAN_ROWS
    n = seq // srows
    nq = CONV_CH // GDN_V_W
    blk = lambda: pl.BlockSpec((srows, GDN_V_W), lambda b, j: (b * n + j, 0))
    o, s_new = pl.pallas_call(
        _gdn_scan_kernel,
        grid=(batch, n),
        in_specs=[blk(), blk(), blk(), blk(), blk(),
                  pl.BlockSpec((srows // c, 1, LANES), lambda b, j: (b * n + j, 0, 0)),
                  pl.BlockSpec((srows, GDN_V_W), lambda b, j: (b * n + j, nq)),
                  _resident((1, GDN_DV))],
        out_specs=[blk(), pl.BlockSpec((1, GDN_HEADS, GDN_DK, GDN_DV), lambda b, j: (b, 0, 0, 0))],
        out_shape=[jax.ShapeDtypeStruct((t, GDN_V_W), BF16),
                   jax.ShapeDtypeStruct((batch, GDN_HEADS, GDN_DK, GDN_DV), F32)],
        compiler_params=_params(("parallel", "arbitrary")),
        name="gdn_scan",
    )(u, w, qg, kg, at, eg, p_gdn, gdn_norm_w.astype(F32).reshape(1, GDN_DV))
    return o, s_new, conv_new


def _gdn_sample_kernel(x_ref, ab_ref, sc_ref, cw_ref, alog_ref, dtb_ref, nw_ref, s_ref, o_ref, so_ref, sco_ref):
    nb = x_ref.shape[0]
    x = x_ref[:, :CONV_CH]
    acc = x * cw_ref[CONV_W - 1:CONV_W, :]
    for i in range(CONV_W - 1):
        acc = acc + sc_ref[i] * cw_ref[i:i + 1, :]
    for i in range(CONV_W - 2):
        sco_ref[i] = sc_ref[i + 1]
    sco_ref[CONV_W - 2] = x
    u = _silu(acc)
    g_all, beta_all = _gdn_gates(ab_ref[...], alog_ref[...], dtb_ref[...])
    eg_all = jnp.exp(g_all)
    nw = nw_ref[...]
    for h in range(GDN_HEADS):
        sl = slice(h * GDN_DK, (h + 1) * GDN_DK)
        q = _l2norm(u[:, sl]) * (GDN_DK ** -0.5)
        k = _l2norm(u[:, GDN_QK_W + h * GDN_DK:GDN_QK_W + (h + 1) * GDN_DK])
        v = u[:, 2 * GDN_QK_W + h * GDN_DV:2 * GDN_QK_W + (h + 1) * GDN_DV]
        beta = beta_all[:, GDN_HEADS + h:GDN_HEADS + h + 1]
        eg = eg_all[:, h:h + 1]
        qk_dot = jnp.sum(q * k, axis=-1, keepdims=True)
        qt, kt = _columns(q), _columns(k)
        rows = []
        for j in range(nb):
            s = s_ref[j, h]
            kcol = kt[:, j:j + 1]
            ks = jnp.sum(kcol * s, axis=0, keepdims=True)
            qs = jnp.sum(qt[:, j:j + 1] * s, axis=0, keepdims=True)
            ej = eg[j:j + 1]
            v_new = beta[j:j + 1] * (v[j:j + 1] - ej * ks)
            rows.append(ej * qs + qk_dot[j:j + 1] * v_new)
            so_ref[j, h] = s * ej + kcol * v_new
        o = jnp.concatenate(rows, axis=0)
        z = x_ref[:, CONV_CH + h * GDN_DV:CONV_CH + (h + 1) * GDN_DV]
        o_ref[:, sl] = (_rms(o) * nw * _silu(z)).astype(BF16)


def _gdn_sample(p_gdn_s, p_ab_s, state, conv_state, conv_w, a_log, dt_bias, gdn_norm_w):
    ts = p_gdn_s.shape[0]
    sb = SAMPLE_TILE
    alog = jnp.pad(a_log.astype(F32), (0, LANES - GDN_HEADS)).reshape(1, LANES)
    dtb = jnp.pad(dt_bias.astype(F32), (0, LANES - GDN_HEADS)).reshape(1, LANES)
    sc = jnp.swapaxes(conv_state.astype(F32), 0, 1)
    st = pl.BlockSpec((sb, GDN_HEADS, GDN_DK, GDN_DV), lambda i: (i, 0, 0, 0))
    scs = pl.BlockSpec((CONV_W - 1, sb, CONV_CH), lambda i: (0, i, 0))
    o, s_new, sc_new = pl.pallas_call(
        _gdn_sample_kernel,
        grid=(ts // sb,),
        in_specs=[pl.BlockSpec((sb, _GDN_W), lambda i: (i, 0)), pl.BlockSpec((sb, LANES), lambda i: (i, 0)), scs,
                  _resident((CONV_W, CONV_CH)), _resident((1, LANES)), _resident((1, LANES)),
                  _resident((1, GDN_DV)), st],
        out_specs=[pl.BlockSpec((sb, GDN_V_W), lambda i: (i, 0)), st, scs],
        out_shape=[jax.ShapeDtypeStruct((ts, GDN_V_W), BF16), jax.ShapeDtypeStruct(state.shape, F32),
                   jax.ShapeDtypeStruct(sc.shape, F32)],
        compiler_params=_params(("parallel",)),
        name="gdn_sample",
    )(p_gdn_s, p_ab_s, sc, conv_w.astype(F32), alog, dtb, gdn_norm_w.astype(F32).reshape(1, GDN_DV), state)
    return o, s_new, jnp.swapaxes(sc_new, 0, 1)


_NO_EXPERT = -1e30


def _merge_kernel(oa0_ref, ob0_ref, gate0_ref, x0_ref, oa1_ref, ob1_ref, gate1_ref, x1in_ref,
                  wa_ref, wb_ref, wo_ref, nw_ref, wr_ref, br_ref, x1_ref, h2_ref, ti_ref, tw_ref, *, n_first):
    first = pl.program_id(0) < n_first
    pick = lambda a, b: jnp.where(first, a[...], b[...])
    gate = pick(gate0_ref, gate1_ref)
    ya = _dot(pick(oa0_ref, oa1_ref), wa_ref[...])
    yb = _dot(pick(ob0_ref, ob1_ref), wb_ref[...])
    ga = gate[:, :D_MODEL].astype(F32)
    gb = gate[:, D_MODEL:].astype(F32)
    m = _sigmoid(ga) * ya + _sigmoid(gb) * yb
    x1 = pick(x0_ref, x1in_ref) + _dot(m.astype(BF16), wo_ref[...])
    x1_ref[...] = x1
    h2 = _rms(x1) * nw_ref[...]
    for j in range(D_MODEL // LANES):
        h2_ref[:, j, :] = h2[:, j * LANES:(j + 1) * LANES]
    lg = _dot_split(h2, wr_ref[...]) + br_ref[...]
    lane = lax.broadcasted_iota(jnp.int32, lg.shape, 1).astype(F32)
    vals, idxs = [], []
    for _ in range(TOP_K):
        top = jnp.max(lg, axis=-1, keepdims=True)
        idx = jnp.min(jnp.where(lg == top, lane, float(LANES)), axis=-1, keepdims=True)
        vals.append(top)
        idxs.append(idx)
        lg = jnp.where(lane == idx, _NO_EXPERT, lg)
    es = [jnp.exp(v - vals[0]) for v in vals]
    inv_total = 1.0 / functools.reduce(lambda a, b: a + b, es)
    ti = jnp.zeros_like(lg)
    tw = jnp.zeros_like(lg)
    for k in range(TOP_K):
        ti = jnp.where(lane == float(k), idxs[k], ti)
        tw = jnp.where(lane == float(k), es[k] * inv_total, tw)
    ti_ref[...] = ti.astype(jnp.int32)
    tw_ref[...] = tw


def _merge(group0, group1, weights):
    r0 = group0[3].shape[0]
    tm = _pick_tile(r0, (MERGE_ROWS, 128, 64, 32, 16, 8))
    pad = -group1[3].shape[0] % tm
    group1 = tuple(jnp.pad(a, ((0, pad), (0, 0))) for a in group1)
    r1 = group1[3].shape[0]
    n0, total = r0 // tm, r0 + r1
    widths = (RET_V_W, GDN_V_W, 2 * D_MODEL, D_MODEL)
    specs0 = [pl.BlockSpec((tm, n), lambda i: (jnp.minimum(i, n0 - 1), 0)) for n in widths]
    specs1 = [pl.BlockSpec((tm, n), lambda i: (jnp.maximum(i - n0, 0), 0)) for n in widths]
    out = lambda n: pl.BlockSpec((tm, n), lambda i: (i, 0))
    sub = D_MODEL // LANES
    sq = (D_MODEL, D_MODEL)
    return pl.pallas_call(
        functools.partial(_merge_kernel, n_first=n0),
        grid=(total // tm,),
        in_specs=specs0 + specs1 + [_resident(sq), _resident(sq), _resident(sq), _resident((1, D_MODEL)),
                                    _resident((D_MODEL, LANES)), _resident((1, LANES))],
        out_specs=[out(D_MODEL), pl.BlockSpec((tm, sub, LANES), lambda i: (i, 0, 0)), out(LANES), out(LANES)],
        out_shape=[jax.ShapeDtypeStruct((total, D_MODEL), F32), jax.ShapeDtypeStruct((total, sub, LANES), F32),
                   jax.ShapeDtypeStruct((total, LANES), jnp.int32), jax.ShapeDtypeStruct((total, LANES), F32)],
        compiler_params=_params(("parallel",)),
        name="merge",
    )(*group0, *group1, *weights)


def _merge_weights(w_a, w_b, w_o, ffn_norm_w, w_router, b_router):
    wr = jnp.pad(w_router.astype(F32), ((0, 0), (0, LANES - N_EXPERTS)))
    br = jnp.pad(b_router.astype(F32), (0, LANES - N_EXPERTS), constant_values=_NO_EXPERT).reshape(1, LANES)
    return (w_a.astype(BF16), w_b.astype(BF16), w_o.astype(BF16), ffn_norm_w.astype(F32).reshape(1, D_MODEL), wr, br)


def _route(top_i, gate, n_tokens):
    rows = MOE_ROWS
    n = n_tokens * TOP_K
    flat_e = top_i.reshape(n).astype(jnp.int32)
    bits = max(1, (n - 1).bit_length())
    assert bits + (N_EXPERTS - 1).bit_length() <= 31
    order = lax.sort((flat_e << bits) | jnp.arange(n, dtype=jnp.int32)) & ((1 << bits) - 1)
    counts = jnp.sum((flat_e[:, None] == jnp.arange(N_EXPERTS, dtype=jnp.int32)[None, :]).astype(jnp.int32), axis=0)
    start = jnp.cumsum(counts) - counts
    pcounts = (counts + rows - 1) // rows * rows
    pend = jnp.cumsum(pcounts)
    pstart = pend - pcounts
    nb = -(-n // rows) + N_EXPERTS
    blk = jnp.arange(nb, dtype=jnp.int32)
    block_e = jnp.minimum(jnp.sum((pend[None, :] <= (blk * rows)[:, None]).astype(jnp.int32), axis=1),
                          N_EXPERTS - 1).astype(jnp.int32)
    nb_used = (pend[-1] // rows).astype(jnp.int32).reshape(1)
    within = (blk * rows - pstart[block_e])[:, None] + jnp.arange(rows, dtype=jnp.int32)[None, :]
    valid = jnp.logical_and(within < counts[block_e][:, None], (blk < nb_used[0])[:, None])
    flat = order[jnp.clip(start[block_e][:, None] + within, 0, n - 1)]
    spare = n + (blk % MOE_RING)[:, None] * rows + jnp.arange(rows, dtype=jnp.int32)[None, :]
    src = jnp.where(valid, flat // TOP_K, 0)
    dst = jnp.where(valid, flat, spare)
    ahead = lambda k: jnp.concatenate([src[k:]] + [src[-1:]] * k, axis=0)
    slab = jnp.concatenate([src, dst, ahead(1), ahead(2)], axis=1)
    row_w = jnp.where(valid, gate[:, :TOP_K].reshape(n)[flat], 0.0)
    row_w = jnp.broadcast_to(row_w[:, :, None], (nb, rows, LANES))
    return block_e, nb_used, slab, row_w


def _expert_kernel(be_ref, nbu_ref, slab_ref, h_ref, roww_ref, wgu_ref, bgu_ref, wd_ref, bd_ref, y_ref,
                   idx0_ref, idx1_ref, idx2_ref, xbuf_ref, ybuf_ref, wgu_bf_ref, wd_bf_ref, isem, gsem, ssem):
    rows = MOE_ROWS
    ring = MOE_RING
    idx_refs = (idx0_ref, idx1_ref, idx2_ref)
    assert len(idx_refs) == ring
    sub = D_MODEL // LANES
    i = pl.program_id(0)
    nbu = nbu_ref[0]
    slot = i % ring
    active = i < nbu
    n_real = y_ref.shape[0] - ring * rows

    def slab_copy(blk, sl):
        return pltpu.make_async_copy(slab_ref.at[blk], idx_refs[sl], isem.at[sl])

    def gather_row(tok, sl, r):
        return pltpu.make_async_copy(h_ref.at[tok], xbuf_ref.at[sl, :, r, :], gsem.at[sl])

    def scatter_row(sl, r, f):
        return pltpu.make_async_copy(ybuf_ref.at[sl, :, r, :], y_ref.at[f], ssem.at[sl])

    def gather_wait(sl):
        pltpu.make_async_copy(xbuf_ref.at[sl], xbuf_ref.at[sl], gsem.at[sl]).wait()

    def scatter_wait(sl):
        pltpu.make_async_copy(ybuf_ref.at[sl], ybuf_ref.at[sl], ssem.at[sl]).wait()

    @pl.when(i == 0)
    def _():
        slab_copy(0, 0).start()
        ybuf_ref[...] = jnp.zeros_like(ybuf_ref)
        for sl in range(ring):
            def fill(r, carry, sl=sl):
                scatter_row(sl, r, n_real + sl * rows + r).start()
                return carry
            lax.fori_loop(0, rows, fill, 0)

    def block(sl):
        nxt, nxt2 = (sl + 1) % ring, (sl + 2) % ring
        idx_ref = idx_refs[sl]
        slab_copy(i, sl).wait()

        @pl.when(i + 1 < nbu)
        def _():
            slab_copy(i + 1, nxt).start()

        if sl == 0:
            @pl.when(i == 0)
            def _():
                def first(r, carry):
                    gather_row(idx_ref[r], 0, r).start()
                    gather_row(idx_ref[2 * rows + r], 1, r).start()
                    return carry
                lax.fori_loop(0, rows, first, 0)

        changed = jnp.logical_or(i == 0, be_ref[i] != be_ref[jnp.maximum(i - 1, 0)])

        @pl.when(changed)
        def _():
            wgu_bf_ref[...] = wgu_ref[0].astype(BF16)
            wd_bf_ref[...] = wd_ref[0].astype(BF16)

        gather_wait(sl)
        scatter_wait(sl)
        xb = jnp.concatenate([xbuf_ref[sl, j] for j in range(sub)], axis=1).astype(BF16)
        hb = _dot(xb, wgu_bf_ref[...]) + bgu_ref[0]
        for r in range(rows):
            gather_row(idx_ref[3 * rows + r], nxt2, r).start(priority=r % 2)
        glu = jnp.minimum(hb[:, :D_FF], SWIGLU_LIMIT)
        lin = jnp.clip(hb[:, D_FF:], -SWIGLU_LIMIT, SWIGLU_LIMIT)
        act = (glu * _sigmoid(SWIGLU_ALPHA * glu) * (lin + 1.0)).astype(BF16)
        yv = (_dot(act, wd_bf_ref[...]) + bd_ref[0]) * roww_ref[0, :, 0:1]
        for j in range(sub):
            ybuf_ref[sl, j] = yv[:, j * LANES:(j + 1) * LANES]
        for r in range(rows):
            scatter_row(sl, r, idx_ref[rows + r]).start(priority=(r + 1) % 2)

        @pl.when(i == nbu - 1)
        def _():
            for s in (sl, nxt, nxt2):
                scatter_wait(s)
            gather_wait(nxt)
            gather_wait(nxt2)

    for sl in range(ring):
        pl.when(jnp.logical_and(active, slot == sl))(functools.partial(block, sl))


def _experts(h2, t, block_e, nb_used, slab, row_w, w_gate_up, b_gate_up, w_down, b_down):
    rows = MOE_ROWS
    ring = MOE_RING
    assert (ring * rows) % TOP_K == 0
    nb = slab.shape[0]
    sub = D_MODEL // LANES
    grid_spec = pltpu.PrefetchScalarGridSpec(
        num_scalar_prefetch=2,
        grid=(nb,),
        in_specs=[pl.BlockSpec(memory_space=pl.ANY),
                  pl.BlockSpec(memory_space=pl.ANY),
                  pl.BlockSpec((1, rows, LANES), lambda i, be, nbu: (i, 0, 0)),
                  pl.BlockSpec((1, D_MODEL, 2 * D_FF), lambda i, be, nbu: (be[i], 0, 0)),
                  pl.BlockSpec((1, 1, 2 * D_FF), lambda i, be, nbu: (be[i], 0, 0)),
                  pl.BlockSpec((1, D_FF, D_MODEL), lambda i, be, nbu: (be[i], 0, 0)),
                  pl.BlockSpec((1, 1, D_MODEL), lambda i, be, nbu: (be[i], 0, 0))],
        out_specs=pl.BlockSpec(memory_space=pl.ANY),
        scratch_shapes=[pltpu.SMEM((slab.shape[1],), jnp.int32)] * ring + [
                        pltpu.VMEM((ring, sub, rows, LANES), F32),
                        pltpu.VMEM((ring, sub, rows, LANES), F32),
                        pltpu.VMEM((D_MODEL, 2 * D_FF), BF16),
                        pltpu.VMEM((D_FF, D_MODEL), BF16),
                        pltpu.SemaphoreType.DMA((ring,)),
                        pltpu.SemaphoreType.DMA((ring,)),
                        pltpu.SemaphoreType.DMA((ring,))])
    return pl.pallas_call(
        _expert_kernel,
        grid_spec=grid_spec,
        out_shape=jax.ShapeDtypeStruct((t * TOP_K + ring * rows, sub, LANES), F32),
        compiler_params=_params(("arbitrary",)),
        name="experts",
    )(block_e, nb_used, slab, h2, row_w, w_gate_up, b_gate_up.reshape(N_EXPERTS, 1, 2 * D_FF), w_down,
      b_down.reshape(N_EXPERTS, 1, D_MODEL))


def _combine_kernel(y_ref, x1_ref, nw_ref, o_ref, sum_ref, *, final):
    total = y_ref[:, 0]
    for k in range(1, TOP_K):
        total = total + y_ref[:, k]
    sum_ref[...] = total
    acc = x1_ref[...] + jnp.concatenate([sum_ref[:, j, :] for j in range(D_MODEL // LANES)], axis=1)
    o_ref[...] = _rms(acc) * nw_ref[...] if final else acc


def _combine(y, x1, row0, rows, norm_w, final):
    tm = _row_tile(rows, row0, (256, 128, 64, 32, 16, 8))
    off = row0 // tm
    sub = D_MODEL // LANES
    y4 = y.reshape(y.shape[0] // TOP_K, TOP_K, sub, LANES)
    return pl.pallas_call(
        functools.partial(_combine_kernel, final=final),
        grid=(rows // tm,),
        in_specs=[pl.BlockSpec((tm, TOP_K, sub, LANES), lambda i: (off + i, 0, 0, 0)),
                  pl.BlockSpec((tm, D_MODEL), lambda i: (off + i, 0)), _resident((1, D_MODEL))],
        out_specs=pl.BlockSpec((tm, D_MODEL), lambda i: (i, 0)),
        out_shape=jax.ShapeDtypeStruct((rows, D_MODEL), F32),
        scratch_shapes=[pltpu.VMEM((tm, sub, LANES), F32)],
        compiler_params=_params(("parallel",)),
        name="combine",
    )(y4, x1, norm_w.astype(F32).reshape(1, D_MODEL))


def kernel(x_prompt, x_sample, state_ret, state_gdn, state_conv, attn_norm_w, w_in, conv_w, a_log, dt_bias, gdn_norm_w, w_branch_a, w_branch_b, w_out, ffn_norm_w, w_router, b_router, w_gate_up, b_gate_up, w_down, b_down, final_norm_w):
    bp, lp, d = x_prompt.shape
    bs, ls, _ = x_sample.shape
    assert ls == 1 and d == D_MODEL and lp % RET_CHUNK == 0 and bs % SAMPLE_TILE == 0
    depth = w_in.shape[0]
    tp = bp * lp
    t = tp + bs
    xp, xs = x_prompt.reshape(tp, d).astype(F32), x_sample.reshape(bs, d).astype(F32)
    rp, gp, cp, rs, gs, cs = [], [], [], [], [], []
    for l in range(depth):
        wb = jnp.pad(w_in[l].astype(BF16), ((0, 0), (0, -w_in.shape[2] % LANES)))
        pp_ret, pp_gdn, pp_gate, pp_ab = _inproj(xp, attn_norm_w[l], wb)
        ps_ret, ps_gdn, ps_gate, ps_ab = _inproj(xs, attn_norm_w[l], wb)
        op_ret, s_ret_p = _ret_prompt(pp_ret, bp, lp)
        os_ret, s_ret_s = _ret_sample(ps_ret.astype(F32), state_ret[l].astype(F32))
        op_gdn, s_gdn_p, conv_p = _gdn_prompt(pp_gdn, pp_ab, bp, lp, conv_w[l], a_log[l], dt_bias[l], gdn_norm_w[l])
        os_gdn, s_gdn_s, conv_s = _gdn_sample(ps_gdn.astype(F32), ps_ab, state_gdn[l].astype(F32), state_conv[l],
                                              conv_w[l], a_log[l], dt_bias[l], gdn_norm_w[l])
        mw = _merge_weights(w_branch_a[l], w_branch_b[l], w_out[l], ffn_norm_w[l], w_router[l], b_router[l])
        x1, h2, top_i, gate = _merge((op_ret, op_gdn, pp_gate, xp), (os_ret, os_gdn, ps_gate, xs), mw)
        block_e, nb_used, slab, row_w = _route(top_i[:t, :TOP_K], gate[:t], t)
        y = _experts(h2, t, block_e, nb_used, slab, row_w, w_gate_up[l], b_gate_up[l], w_down[l], b_down[l])
        last = l == depth - 1
        norm_w = final_norm_w if last else jnp.ones((d,), F32)
        xp = _combine(y, x1, 0, tp, norm_w, last)
        xs = _combine(y, x1, tp, bs, norm_w, last)
        rp.append(s_ret_p); gp.append(s_gdn_p); cp.append(conv_p)
        rs.append(s_ret_s); gs.append(s_gdn_s); cs.append(conv_s)
    y_prompt = xp.reshape(bp, lp, d).astype(x_prompt.dtype)
    y_sample = xs.reshape(bs, ls, d).astype(x_sample.dtype)
    return (y_prompt, y_sample,
            jnp.stack(rp).astype(state_ret.dtype), jnp.stack(gp).astype(state_gdn.dtype),
            jnp.stack(cp).astype(state_conv.dtype),
            jnp.stack(rs).astype(state_ret.dtype), jnp.stack(gs).astype(state_gdn.dtype),
            jnp.stack(cs).astype(state_conv.dtype))
```

```python
import functools
import math

import numpy as np
import jax
import jax.numpy as jnp
from jax import lax
from jax.experimental import pallas as pl
from jax.experimental.pallas import tpu as pltpu

F32 = jnp.float32
BF16 = jnp.bfloat16
HIGHEST = lax.Precision.HIGHEST

D_MODEL = 1024
PAST_LEN = 16384
RET_HEADS, RET_DK, RET_DV = 4, 128, 256
RET_QK_W, RET_V_W = RET_HEADS * RET_DK, RET_HEADS * RET_DV
RET_CHUNK = 128
ROPE_BASE = 10000.0
GDN_HEADS, GDN_DK, GDN_DV = 8, 128, 128
GDN_QK_W, GDN_V_W = GDN_HEADS * GDN_DK, GDN_HEADS * GDN_DV
GDN_CHUNK = 64
CONV_W = 4
CONV_CH = 2 * GDN_QK_W + GDN_V_W
N_EXPERTS = 32
TOP_K = 4
D_FF = D_MODEL
SWIGLU_LIMIT = 7.0
SWIGLU_ALPHA = 1.702
NORM_EPS = 1e-6

_RET_W = 2 * RET_QK_W + 2 * RET_V_W
_GDN_W = CONV_CH + GDN_V_W
_AB_OFF = _RET_W + _GDN_W
_GATE_OFF = _AB_OFF + 2 * GDN_HEADS

LANES = 128
VMEM_LIMIT = 56 * 1024 * 1024
MERGE_ROWS = 512
MOE_ROWS = 256
MOE_RING = 3
SAMPLE_TILE = 8
RET_STEP_ROWS = 256
GDN_PREP_ROWS = 128
GDN_SCAN_ROWS = 128
GDN_SCAN_SEQS = 2


def _pick_tile(n, candidates):
    for c in candidates:
        if n % c == 0:
            return c
    raise ValueError(f"no tile in {candidates} divides {n}")


def _params(sem, vmem=VMEM_LIMIT):
    return pltpu.CompilerParams(dimension_semantics=sem, vmem_limit_bytes=vmem)


def _resident(shape):
    nd = len(shape)
    return pl.BlockSpec(shape, lambda *_: (0,) * nd, pipeline_mode=pl.Buffered(1))


def _silu(x):
    return x * (1.0 / (1.0 + jnp.exp(-x)))


def _sigmoid(x):
    return 1.0 / (1.0 + jnp.exp(-x))


def _softplus(x):
    return jnp.maximum(x, 0.0) + jnp.log1p(jnp.exp(-jnp.abs(x)))


def _rms(x):
    return x * lax.rsqrt(jnp.mean(x * x, axis=-1, keepdims=True) + NORM_EPS)


def _dot(a, b):
    return jnp.dot(a, b, preferred_element_type=F32)


def _dot_nt(a, b):
    return lax.dot_general(a, b, (((1,), (1,)), ((), ())), preferred_element_type=F32)


def _dot_tn(a, b):
    return lax.dot_general(a, b, (((0,), (0,)), ((), ())), preferred_element_type=F32)


def _dot_hi(a, b):
    return jnp.dot(a, b, preferred_element_type=F32, precision=HIGHEST)


def _dot_split(a, b):
    a_hi, b_hi = a.astype(BF16), b.astype(BF16)
    a_lo = (a - a_hi.astype(F32)).astype(BF16)
    b_lo = (b - b_hi.astype(F32)).astype(BF16)
    return _dot(a_hi, b_hi) + (_dot(a_hi, b_lo) + _dot(a_lo, b_hi))


def _row_tile(rows, row0, candidates):
    return _pick_tile(math.gcd(rows, row0) if row0 else rows, candidates)


def _inproj_kernel(x_ref, nw_ref, w_ref, oret_ref, ogdn_ref, ogate_ref, oab_ref):
    h = (_rms(x_ref[...]) * nw_ref[...]).astype(BF16)
    oret_ref[...] = _dot(h, w_ref[:, :_RET_W]).astype(BF16)
    ogdn_ref[...] = _dot(h, w_ref[:, _RET_W:_AB_OFF]).astype(BF16)
    tail = _dot(h, w_ref[:, _AB_OFF:])
    oab_ref[...] = tail[:, :LANES]
    ogate_ref[...] = tail[:, _GATE_OFF - _AB_OFF:_GATE_OFF - _AB_OFF + 2 * D_MODEL].astype(BF16)


def _inproj(x, norm_w, wb):
    rows = x.shape[0]
    tm = _pick_tile(rows, (256, 128, 64, 32, 16, 8))
    row = lambda n: pl.BlockSpec((tm, n), lambda i: (i, 0))
    return pl.pallas_call(
        _inproj_kernel,
        grid=(rows // tm,),
        in_specs=[row(D_MODEL), _resident((1, D_MODEL)), _resident(wb.shape)],
        out_specs=[row(_RET_W), row(_GDN_W), row(2 * D_MODEL), row(LANES)],
        out_shape=[jax.ShapeDtypeStruct((rows, _RET_W), BF16), jax.ShapeDtypeStruct((rows, _GDN_W), BF16),
                   jax.ShapeDtypeStruct((rows, 2 * D_MODEL), BF16), jax.ShapeDtypeStruct((rows, LANES), F32)],
        compiler_params=_params(("parallel",)),
        name="inproj",
    )(x, norm_w.reshape(1, D_MODEL), wb)


def _ret_log_gamma():
    return np.log1p(-np.exp2(-5.0 - np.arange(RET_HEADS, dtype=np.float64)))


def _rope_tables(pos):
    half = RET_DK // 2
    inv = 1.0 / (ROPE_BASE ** (jnp.arange(half, dtype=F32) / half))
    ang = pos.astype(F32)[:, None] * inv[None, :]
    cos, sin = jnp.cos(ang), jnp.sin(ang)
    return jnp.concatenate([cos, cos], axis=-1), jnp.concatenate([-sin, sin], axis=-1)


def _rotary(x, cos, sin):
    return x * cos + pltpu.roll(x, RET_DK // 2, 1) * sin


def _ret_prompt_kernel(q_ref, k_ref, v_ref, g_ref, cos_ref, sin_ref, dmask_ref, qdec_ref, kdec_ref,
                       o_ref, s_ref, *, gammas):
    @pl.when(pl.program_id(1) == 0)
    def _():
        s_ref[...] = jnp.zeros_like(s_ref)

    c = RET_CHUNK
    heads = range(RET_HEADS)
    qk = [slice(h * RET_DK, (h + 1) * RET_DK) for h in heads]
    vv = [slice(h * RET_DV, (h + 1) * RET_DV) for h in heads]
    s = [s_ref[0, h] for h in heads]
    for ck in range(q_ref.shape[0] // c):
        r = slice(ck * c, (ck + 1) * c)
        cos, sin = cos_ref[r, :], sin_ref[r, :]
        q = [_rotary(q_ref[r, qk[h]].astype(F32), cos, sin) for h in heads]
        k = [_rotary(k_ref[r, qk[h]].astype(F32), cos, sin) * (RET_DK ** -0.5) for h in heads]
        v = [v_ref[r, vv[h]] for h in heads]
        qb = [x.astype(BF16) for x in q]
        inner = [_dot_nt(qb[h], k[h].astype(BF16)) * dmask_ref[h] for h in heads]
        cross = [_dot((q[h] * qdec_ref[h]).astype(BF16), s[h].astype(BF16)) for h in heads]
        upd = [_dot_tn((k[h] * kdec_ref[h]).astype(BF16), v[h]) for h in heads]
        o = [_dot(inner[h].astype(BF16), v[h]) + cross[h] for h in heads]
        s = [s[h] * gammas[h] + upd[h] for h in heads]
        for h in heads:
            o_ref[r, vv[h]] = (_rms(o[h]) * _silu(g_ref[r, vv[h]].astype(F32))).astype(BF16)
    for h in heads:
        s_ref[0, h] = s[h]


def _ret_prompt(p_ret, batch, seq):
    c = RET_CHUNK
    step = RET_STEP_ROWS if seq % RET_STEP_ROWS == 0 else c
    n = seq // step
    lg = _ret_log_gamma()
    idx = np.arange(c, dtype=np.float64)
    diff = idx[:, None] - idx[None, :]
    dmask = np.where(diff >= 0, np.exp(np.maximum(diff, 0.0)[None] * lg[:, None, None]), 0.0)
    qdec = np.broadcast_to(np.exp((idx + 1.0)[None, :] * lg[:, None])[:, :, None], (RET_HEADS, c, RET_DK))
    kdec = np.broadcast_to(np.exp((c - 1.0 - idx)[None, :] * lg[:, None])[:, :, None], (RET_HEADS, c, RET_DK))
    gammas = tuple(float(g) for g in np.exp(c * lg))
    cos, sin = _rope_tables(jnp.arange(seq, dtype=jnp.int32))
    tab = lambda: _resident((RET_HEADS, c, RET_DK))
    return pl.pallas_call(
        functools.partial(_ret_prompt_kernel, gammas=gammas),
        grid=(batch, n),
        in_specs=[pl.BlockSpec((step, RET_QK_W), lambda b, j: (b * n + j, 0)),
                  pl.BlockSpec((step, RET_QK_W), lambda b, j: (b * n + j, 1)),
                  pl.BlockSpec((step, RET_V_W), lambda b, j: (b * n + j, 1)),
                  pl.BlockSpec((step, RET_V_W), lambda b, j: (b * n + j, 2)),
                  pl.BlockSpec((step, RET_DK), lambda b, j: (j, 0)),
                  pl.BlockSpec((step, RET_DK), lambda b, j: (j, 0)),
                  tab(), tab(), tab()],
        out_specs=[pl.BlockSpec((step, RET_V_W), lambda b, j: (b * n + j, 0)),
                   pl.BlockSpec((1, RET_HEADS, RET_DK, RET_DV), lambda b, j: (b, 0, 0, 0))],
        out_shape=[jax.ShapeDtypeStruct((batch * seq, RET_V_W), BF16),
                   jax.ShapeDtypeStruct((batch, RET_HEADS, RET_DK, RET_DV), F32)],
        compiler_params=_params(("parallel", "arbitrary")),
        name="ret_prompt",
    )(p_ret, p_ret, p_ret, p_ret, cos, sin, jnp.asarray(dmask, F32), jnp.asarray(qdec, F32),
      jnp.asarray(kdec, F32))


def _columns(x):
    n = x.shape[0]
    if n < LANES:
        x = jnp.concatenate([x, jnp.zeros((LANES - n, x.shape[1]), x.dtype)], axis=0)
    return x.T


def _ret_sample_kernel(p_ref, cos_ref, sin_ref, s_ref, o_ref, so_ref, *, gammas):
    cos, sin = cos_ref[...], sin_ref[...]
    nb = p_ref.shape[0]
    for h in range(RET_HEADS):
        qk = slice(h * RET_DK, (h + 1) * RET_DK)
        q = _rotary(p_ref[:, qk], cos, sin)
        k = _rotary(p_ref[:, RET_QK_W + h * RET_DK:RET_QK_W + (h + 1) * RET_DK], cos, sin) * (RET_DK ** -0.5)
        v = p_ref[:, 2 * RET_QK_W + h * RET_DV:2 * RET_QK_W + (h + 1) * RET_DV]
        g = p_ref[:, 2 * RET_QK_W + RET_V_W + h * RET_DV:2 * RET_QK_W + RET_V_W + (h + 1) * RET_DV]
        qk_dot = jnp.sum(q * k, axis=-1, keepdims=True)
        qt, kt = _columns(q), _columns(k)
        rows = []
        for j in range(nb):
            s = s_ref[j, h]
            qs = jnp.sum(qt[:, j:j + 1] * s, axis=0, keepdims=True)
            rows.append(qk_dot[j:j + 1] * v[j:j + 1] + gammas[h] * qs)
            so_ref[j, h] = s * gammas[h] + kt[:, j:j + 1] * v[j:j + 1]
        o = jnp.concatenate(rows, axis=0)
        o_ref[:, h * RET_DV:(h + 1) * RET_DV] = (_rms(o) * _silu(g)).astype(BF16)


def _ret_sample(p_ret_s, state):
    ts = p_ret_s.shape[0]
    sb = SAMPLE_TILE
    gammas = tuple(float(g) for g in np.exp(_ret_log_gamma()))
    cos, sin = _rope_tables(jnp.full((1,), PAST_LEN, jnp.int32))
    st = pl.BlockSpec((sb, RET_HEADS, RET_DK, RET_DV), lambda i: (i, 0, 0, 0))
    return pl.pallas_call(
        functools.partial(_ret_sample_kernel, gammas=gammas),
        grid=(ts // sb,),
        in_specs=[pl.BlockSpec((sb, _RET_W), lambda i: (i, 0)), _resident((1, RET_DK)), _resident((1, RET_DK)), st],
        out_specs=[pl.BlockSpec((sb, RET_V_W), lambda i: (i, 0)), st],
        out_shape=[jax.ShapeDtypeStruct((ts, RET_V_W), BF16), jax.ShapeDtypeStruct(state.shape, F32)],
        compiler_params=_params(("parallel",)),
        name="ret_sample",
    )(p_ret_s, cos, sin, state)


def _l2norm(x):
    return x * lax.rsqrt(jnp.sum(x * x, axis=-1, keepdims=True) + NORM_EPS)


def _bdot(a, b):
    return _dot(a.astype(BF16), b.astype(BF16))


def _chunk_masks(c):
    ri = lax.broadcasted_iota(jnp.int32, (c, c), 0)
    ci = lax.broadcasted_iota(jnp.int32, (c, c), 1)
    eye = (ri == ci).astype(F32)
    diag16 = (ri // 16 == ci // 16).astype(F32)
    low32 = jnp.logical_and(ri // 32 == ci // 32, ri // 16 > ci // 16).astype(F32)
    low64 = (ri // 32 > ci // 32).astype(F32)
    return ri >= ci, ri > ci, eye, diag16, low32, low64


def _unit_lower_inverse(a, eye, diag16, low32, low64):
    many = lambda f, *ls: [f(*args) for args in zip(*ls)]
    n = [-(x * diag16) for x in a]
    n2 = many(_bdot, n, n)
    n3 = many(_bdot, n, n2)
    n4 = many(_bdot, n2, n2)
    n8 = many(_bdot, n4, n4)
    x = [eye + p + q + r for p, q, r in zip(n, n2, n3)]
    x = many(lambda u, v: u + v, x, many(_bdot, x, n4))
    x = many(lambda u, v: u + v, x, many(_bdot, x, n8))
    for mask in (low32, low64):
        r = many(_bdot, [y * mask for y in a], x)
        x = many(lambda u, v: u - v, x, many(_bdot, x, r))
    return x


def _gdn_gates(ab, alog, dtb):
    g = -jnp.exp(alog) * _softplus(ab + dtb)
    return g, _sigmoid(ab)


def _gdn_prep_kernel(x_ref, prev_ref, ab_ref, cw_ref, alog_ref, dtb_ref,
                     u_ref, w_ref, qg_ref, kg_ref, at_ref, eg_ref, cv_ref, act_ref):
    c = GDN_CHUNK
    rows = x_ref.shape[0]
    x = x_ref[...]
    halo = prev_ref.shape[0]
    prev = jnp.where(pl.program_id(1) == 0, jnp.zeros_like(prev_ref), prev_ref[...])
    xcat = jnp.concatenate([prev, x], axis=0)
    ti = lax.broadcasted_iota(jnp.int32, (rows, rows + halo), 0)
    ui = lax.broadcasted_iota(jnp.int32, (rows, rows + halo), 1)
    xf = x.astype(F32)
    acc = xf * cw_ref[CONV_W - 1:CONV_W, :]
    for i in range(CONV_W - 1):
        shift = (ui == ti + (halo - (CONV_W - 1) + i)).astype(BF16)
        acc = acc + _dot(shift, xcat) * cw_ref[i:i + 1, :]
    cv_ref[0] = xf[rows - (CONV_W - 1):, :]
    act_ref[...] = _silu(acc)

    g_all, beta_all = _gdn_gates(ab_ref[...], alog_ref[...], dtb_ref[...])
    lower, strict, eye, diag16, low32, low64 = _chunk_masks(c)
    gc_all, gr_all = [], []
    for ck in range(rows // c):
        gc_ck = _dot_hi(lower.astype(F32), g_all[ck * c:(ck + 1) * c])
        gc_all.append(gc_ck)
        gr_all.append(_columns(gc_ck))
        eg_ref[ck] = jnp.exp(gc_ck[c - 1:c, :])
    chains = [(ck, h) for ck in range(rows // c) for h in range(GDN_HEADS)]
    rs = [slice(ck * c, (ck + 1) * c) for ck, _ in chains]
    sl = [slice(h * GDN_DK, (h + 1) * GDN_DK) for _, h in chains]
    nc = range(len(chains))
    q = [_l2norm(act_ref[rs[i], sl[i]]) * (GDN_DK ** -0.5) for i in nc]
    k = [_l2norm(act_ref[rs[i], GDN_QK_W + sl[i].start:GDN_QK_W + sl[i].stop]) for i in nc]
    v = [act_ref[rs[i], 2 * GDN_QK_W + sl[i].start:2 * GDN_QK_W + sl[i].stop] for i in nc]
    beta = [beta_all[rs[i], GDN_HEADS + h:GDN_HEADS + h + 1] for i, (_, h) in enumerate(chains)]
    gc = [gc_all[ck][:, h:h + 1] for ck, h in chains]
    gr = [gr_all[ck][h:h + 1, :c] for ck, h in chains]
    decay = [jnp.exp(jnp.where(lower, gc[i] - gr[i], -jnp.inf)) for i in nc]
    exp_g = [jnp.exp(x) for x in gc]
    kb = [k[i] * beta[i] for i in nc]
    kbf = [x.astype(BF16) for x in k]
    a = [_dot_nt(kb[i].astype(BF16), kbf[i]) * jnp.where(strict, decay[i], 0.0) for i in nc]
    attn = [_dot_nt(q[i].astype(BF16), kbf[i]) * decay[i] for i in nc]
    t = _unit_lower_inverse(a, eye, diag16, low32, low64)
    uu = [_bdot(t[i], v[i] * beta[i]) for i in nc]
    ww = [_bdot(t[i], kb[i] * exp_g[i]) for i in nc]
    for i in nc:
        u_ref[rs[i], sl[i]] = uu[i]
        w_ref[rs[i], sl[i]] = ww[i].astype(BF16)
        at_ref[rs[i], sl[i]] = jnp.concatenate([attn[i], jnp.zeros((c, GDN_DK - c), F32)], axis=1).astype(BF16)
        qg_ref[rs[i], sl[i]] = (q[i] * exp_g[i]).astype(BF16)
        kg_ref[rs[i], sl[i]] = (k[i] * jnp.exp(gc[i][c - 1:c, :] - gc[i])).astype(BF16)


def _gdn_scan_kernel(u_ref, w_ref, qg_ref, kg_ref, at_ref, eg_ref, z_ref, nw_ref, o_ref, s_ref):
    c = GDN_CHUNK

    @pl.when(pl.program_id(1) == 0)
    def _():
        s_ref[...] = jnp.zeros_like(s_ref)

    nw = nw_ref[...]
    chains = [(q, h) for q in range(u_ref.shape[0]) for h in range(GDN_HEADS)]
    sl = [slice(h * GDN_DK, (h + 1) * GDN_DK) for _, h in chains]
    nc = range(len(chains))
    s = [s_ref[q, h] for q, h in chains]
    for ck in range(u_ref.shape[1] // c):
        r = slice(ck * c, (ck + 1) * c)
        sb = [x.astype(BF16) for x in s]
        ws = [_dot(w_ref[q, r, sl[i]], sb[i]) for i, (q, _) in enumerate(chains)]
        qs = [_dot(qg_ref[q, r, sl[i]], sb[i]) for i, (q, _) in enumerate(chains)]
        vnb = [(u_ref[q, r, sl[i]] - ws[i]).astype(BF16) for i, (q, _) in enumerate(chains)]
        o = [qs[i] + _dot(at_ref[q, r, h * GDN_DK:h * GDN_DK + c], vnb[i]) for i, (q, h) in enumerate(chains)]
        s = [s[i] * eg_ref[q, ck][:, h:h + 1] + _dot_tn(kg_ref[q, r, sl[i]], vnb[i])
             for i, (q, h) in enumerate(chains)]
        for i, (q, _) in enumerate(chains):
            o_ref[q, r, sl[i]] = (_rms(o[i]) * nw * _silu(z_ref[q, r, sl[i]].astype(F32))).astype(BF16)
    for i, (q, h) in enumerate(chains):
        s_ref[q, h] = s[i]


def _gdn_prompt(p_gdn, p_ab, batch, seq, conv_w, a_log, dt_bias, gdn_norm_w):
    c = GDN_CHUNK
    rows = GDN_PREP_ROWS
    t = batch * seq
    nt = seq // rows
    alog = jnp.pad(a_log.astype(F32), (0, LANES - GDN_HEADS)).reshape(1, LANES)
    dtb = jnp.pad(dt_bias.astype(F32), (0, LANES - GDN_HEADS)).reshape(1, LANES)
    wide = lambda: pl.BlockSpec((rows, GDN_V_W), lambda b, j: (b * nt + j, 0))
    u, w, qg, kg, at, eg, conv_new = pl.pallas_call(
        _gdn_prep_kernel,
        grid=(batch, nt),
        in_specs=[pl.BlockSpec((rows, CONV_CH), lambda b, j: (b * nt + j, 0)),
                  pl.BlockSpec((16, CONV_CH), lambda b, j: (jnp.maximum((b * nt + j) * (rows // 16) - 1, 0), 0)),
                  pl.BlockSpec((rows, LANES), lambda b, j: (b * nt + j, 0)),
                  _resident((CONV_W, CONV_CH)), _resident((1, LANES)), _resident((1, LANES))],
        out_specs=[wide(), wide(), wide(), wide(), wide(),
                   pl.BlockSpec((rows // c, 1, LANES), lambda b, j: (b * nt + j, 0, 0)),
                   pl.BlockSpec((1, CONV_W - 1, CONV_CH), lambda b, j: (b, 0, 0))],
        out_shape=[jax.ShapeDtypeStruct((t, GDN_V_W), F32)] + [jax.ShapeDtypeStruct((t, GDN_V_W), BF16)] * 4
        + [jax.ShapeDtypeStruct((t // c, 1, LANES), F32),
           jax.ShapeDtypeStruct((batch, CONV_W - 1, CONV_CH), F32)],
        scratch_shapes=[pltpu.VMEM((rows, CONV_CH), F32)],
        compiler_params=_params(("parallel", "arbitrary")),
        name="gdn_prep",
    )(p_gdn, p_gdn, p_ab, conv_w.astype(F32), alog, dtb)
    srows = GDN_SCAN_ROWS
    n = seq // srows
    nq = CONV_CH // GDN_V_W
    nseq = GDN_SCAN_SEQS if batch % GDN_SCAN_SEQS == 0 else 1
    per_seq = lambda a: a.reshape((batch, seq // (t // a.shape[0])) + a.shape[1:])
    blk = lambda: pl.BlockSpec((nseq, srows, GDN_V_W), lambda b, j: (b, j, 0))
    o, s_new = pl.pallas_call(
        _gdn_scan_kernel,
        grid=(batch // nseq, n),
        in_specs=[blk(), blk(), blk(), blk(), blk(),
                  pl.BlockSpec((nseq, srows // c, 1, LANES), lambda b, j: (b, j, 0, 0)),
                  pl.BlockSpec((nseq, srows, GDN_V_W), lambda b, j: (b, j, nq)),
                  _resident((1, GDN_DV))],
        out_specs=[blk(), pl.BlockSpec((nseq, GDN_HEADS, GDN_DK, GDN_DV), lambda b, j: (b, 0, 0, 0))],
        out_shape=[jax.ShapeDtypeStruct((batch, seq, GDN_V_W), BF16),
                   jax.ShapeDtypeStruct((batch, GDN_HEADS, GDN_DK, GDN_DV), F32)],
        compiler_params=_params(("parallel", "arbitrary")),
        name="gdn_scan",
    )(per_seq(u), per_seq(w), per_seq(qg), per_seq(kg), per_seq(at), per_seq(eg), per_seq(p_gdn),
      gdn_norm_w.astype(F32).reshape(1, GDN_DV))
    return o.reshape(t, GDN_V_W), s_new, conv_new


def _gdn_sample_kernel(x_ref, ab_ref, sc_ref, cw_ref, alog_ref, dtb_ref, nw_ref, s_ref, o_ref, so_ref, sco_ref):
    nb = x_ref.shape[0]
    x = x_ref[:, :CONV_CH]
    acc = x * cw_ref[CONV_W - 1:CONV_W, :]
    for i in range(CONV_W - 1):
        acc = acc + sc_ref[i] * cw_ref[i:i + 1, :]
    for i in range(CONV_W - 2):
        sco_ref[i] = sc_ref[i + 1]
    sco_ref[CONV_W - 2] = x
    u = _silu(acc)
    g_all, beta_all = _gdn_gates(ab_ref[...], alog_ref[...], dtb_ref[...])
    eg_all = jnp.exp(g_all)
    nw = nw_ref[...]
    for h in range(GDN_HEADS):
        sl = slice(h * GDN_DK, (h + 1) * GDN_DK)
        q = _l2norm(u[:, sl]) * (GDN_DK ** -0.5)
        k = _l2norm(u[:, GDN_QK_W + h * GDN_DK:GDN_QK_W + (h + 1) * GDN_DK])
        v = u[:, 2 * GDN_QK_W + h * GDN_DV:2 * GDN_QK_W + (h + 1) * GDN_DV]
        beta = beta_all[:, GDN_HEADS + h:GDN_HEADS + h + 1]
        eg = eg_all[:, h:h + 1]
        qk_dot = jnp.sum(q * k, axis=-1, keepdims=True)
        qt, kt = _columns(q), _columns(k)
        rows = []
        for j in range(nb):
            s = s_ref[j, h]
            kcol = kt[:, j:j + 1]
            ks = jnp.sum(kcol * s, axis=0, keepdims=True)
            qs = jnp.sum(qt[:, j:j + 1] * s, axis=0, keepdims=True)
            ej = eg[j:j + 1]
            v_new = beta[j:j + 1] * (v[j:j + 1] - ej * ks)
            rows.append(ej * qs + qk_dot[j:j + 1] * v_new)
            so_ref[j, h] = s * ej + kcol * v_new
        o = jnp.concatenate(rows, axis=0)
        z = x_ref[:, CONV_CH + h * GDN_DV:CONV_CH + (h + 1) * GDN_DV]
        o_ref[:, sl] = (_rms(o) * nw * _silu(z)).astype(BF16)


def _gdn_sample(p_gdn_s, p_ab_s, state, conv_state, conv_w, a_log, dt_bias, gdn_norm_w):
    ts = p_gdn_s.shape[0]
    sb = SAMPLE_TILE
    alog = jnp.pad(a_log.astype(F32), (0, LANES - GDN_HEADS)).reshape(1, LANES)
    dtb = jnp.pad(dt_bias.astype(F32), (0, LANES - GDN_HEADS)).reshape(1, LANES)
    sc = jnp.swapaxes(conv_state.astype(F32), 0, 1)
    st = pl.BlockSpec((sb, GDN_HEADS, GDN_DK, GDN_DV), lambda i: (i, 0, 0, 0))
    scs = pl.BlockSpec((CONV_W - 1, sb, CONV_CH), lambda i: (0, i, 0))
    o, s_new, sc_new = pl.pallas_call(
        _gdn_sample_kernel,
        grid=(ts // sb,),
        in_specs=[pl.BlockSpec((sb, _GDN_W), lambda i: (i, 0)), pl.BlockSpec((sb, LANES), lambda i: (i, 0)), scs,
                  _resident((CONV_W, CONV_CH)), _resident((1, LANES)), _resident((1, LANES)),
                  _resident((1, GDN_DV)), st],
        out_specs=[pl.BlockSpec((sb, GDN_V_W), lambda i: (i, 0)), st, scs],
        out_shape=[jax.ShapeDtypeStruct((ts, GDN_V_W), BF16), jax.ShapeDtypeStruct(state.shape, F32),
                   jax.ShapeDtypeStruct(sc.shape, F32)],
        compiler_params=_params(("parallel",)),
        name="gdn_sample",
    )(p_gdn_s, p_ab_s, sc, conv_w.astype(F32), alog, dtb, gdn_norm_w.astype(F32).reshape(1, GDN_DV), state)
    return o, s_new, jnp.swapaxes(sc_new, 0, 1)


_NO_EXPERT = -1e30


def _merge_kernel(oa0_ref, ob0_ref, gate0_ref, x0_ref, oa1_ref, ob1_ref, gate1_ref, x1in_ref,
                  wa_ref, wb_ref, wo_ref, nw_ref, wr_ref, br_ref, x1_ref, h2_ref, ti_ref, tw_ref, *, n_first):
    first = pl.program_id(0) < n_first
    pick = lambda a, b: jnp.where(first, a[...], b[...])
    gate = pick(gate0_ref, gate1_ref)
    ya = _dot(pick(oa0_ref, oa1_ref), wa_ref[...])
    yb = _dot(pick(ob0_ref, ob1_ref), wb_ref[...])
    ga = gate[:, :D_MODEL].astype(F32)
    gb = gate[:, D_MODEL:].astype(F32)
    m = _sigmoid(ga) * ya + _sigmoid(gb) * yb
    x1 = pick(x0_ref, x1in_ref) + _dot(m.astype(BF16), wo_ref[...])
    x1_ref[...] = x1
    h2 = _rms(x1) * nw_ref[...]
    for j in range(D_MODEL // LANES):
        h2_ref[:, j, :] = h2[:, j * LANES:(j + 1) * LANES]
    lg = _dot_split(h2, wr_ref[...]) + br_ref[...]
    lane = lax.broadcasted_iota(jnp.int32, lg.shape, 1).astype(F32)
    vals, idxs = [], []
    for _ in range(TOP_K):
        top = jnp.max(lg, axis=-1, keepdims=True)
        idx = jnp.min(jnp.where(lg == top, lane, float(LANES)), axis=-1, keepdims=True)
        vals.append(top)
        idxs.append(idx)
        lg = jnp.where(lane == idx, _NO_EXPERT, lg)
    es = [jnp.exp(v - vals[0]) for v in vals]
    inv_total = 1.0 / functools.reduce(lambda a, b: a + b, es)
    ti = jnp.zeros_like(lg)
    tw = jnp.zeros_like(lg)
    for k in range(TOP_K):
        ti = jnp.where(lane == float(k), idxs[k], ti)
        tw = jnp.where(lane == float(k), es[k] * inv_total, tw)
    ti_ref[...] = ti.astype(jnp.int32)
    tw_ref[...] = tw


def _merge(group0, group1, weights):
    r0 = group0[3].shape[0]
    tm = _pick_tile(r0, (MERGE_ROWS, 128, 64, 32, 16, 8))
    pad = -group1[3].shape[0] % tm
    group1 = tuple(jnp.pad(a, ((0, pad), (0, 0))) for a in group1)
    r1 = group1[3].shape[0]
    n0, total = r0 // tm, r0 + r1
    widths = (RET_V_W, GDN_V_W, 2 * D_MODEL, D_MODEL)
    specs0 = [pl.BlockSpec((tm, n), lambda i: (jnp.minimum(i, n0 - 1), 0)) for n in widths]
    specs1 = [pl.BlockSpec((tm, n), lambda i: (jnp.maximum(i - n0, 0), 0)) for n in widths]
    out = lambda n: pl.BlockSpec((tm, n), lambda i: (i, 0))
    sub = D_MODEL // LANES
    sq = (D_MODEL, D_MODEL)
    return pl.pallas_call(
        functools.partial(_merge_kernel, n_first=n0),
        grid=(total // tm,),
        in_specs=specs0 + specs1 + [_resident(sq), _resident(sq), _resident(sq), _resident((1, D_MODEL)),
                                    _resident((D_MODEL, LANES)), _resident((1, LANES))],
        out_specs=[out(D_MODEL), pl.BlockSpec((tm, sub, LANES), lambda i: (i, 0, 0)), out(LANES), out(LANES)],
        out_shape=[jax.ShapeDtypeStruct((total, D_MODEL), F32), jax.ShapeDtypeStruct((total, sub, LANES), F32),
                   jax.ShapeDtypeStruct((total, LANES), jnp.int32), jax.ShapeDtypeStruct((total, LANES), F32)],
        compiler_params=_params(("parallel",)),
        name="merge",
    )(*group0, *group1, *weights)


def _merge_weights(w_a, w_b, w_o, ffn_norm_w, w_router, b_router):
    wr = jnp.pad(w_router.astype(F32), ((0, 0), (0, LANES - N_EXPERTS)))
    br = jnp.pad(b_router.astype(F32), (0, LANES - N_EXPERTS), constant_values=_NO_EXPERT).reshape(1, LANES)
    return (w_a.astype(BF16), w_b.astype(BF16), w_o.astype(BF16), ffn_norm_w.astype(F32).reshape(1, D_MODEL), wr, br)


def _route(top_i, gate, n_tokens):
    rows = MOE_ROWS
    n = n_tokens * TOP_K
    flat_e = top_i.reshape(n).astype(jnp.int32)
    bits = max(1, (n - 1).bit_length())
    assert bits + (N_EXPERTS - 1).bit_length() <= 31
    order = lax.sort((flat_e << bits) | jnp.arange(n, dtype=jnp.int32), is_stable=False) & ((1 << bits) - 1)
    counts = jnp.sum((flat_e[:, None] == jnp.arange(N_EXPERTS, dtype=jnp.int32)[None, :]).astype(jnp.int32), axis=0)
    start = jnp.cumsum(counts) - counts
    pcounts = (counts + rows - 1) // rows * rows
    pend = jnp.cumsum(pcounts)
    pstart = pend - pcounts
    nb = -(-n // rows) + N_EXPERTS
    blk = jnp.arange(nb, dtype=jnp.int32)
    block_e = jnp.minimum(jnp.sum((pend[None, :] <= (blk * rows)[:, None]).astype(jnp.int32), axis=1),
                          N_EXPERTS - 1).astype(jnp.int32)
    nb_used = (pend[-1] // rows).astype(jnp.int32).reshape(1)
    within = (blk * rows - pstart[block_e])[:, None] + jnp.arange(rows, dtype=jnp.int32)[None, :]
    valid = jnp.logical_and(within < counts[block_e][:, None], (blk < nb_used[0])[:, None])
    flat = order[jnp.clip(start[block_e][:, None] + within, 0, n - 1)]
    spare = n + (blk % MOE_RING)[:, None] * rows + jnp.arange(rows, dtype=jnp.int32)[None, :]
    src = jnp.where(valid, flat // TOP_K, 0)
    dst = jnp.where(valid, flat, spare)
    ahead = lambda k: jnp.concatenate([src[k:]] + [src[-1:]] * k, axis=0)
    slab = jnp.concatenate([src, dst, ahead(1), ahead(2)], axis=1)
    row_w = jnp.where(valid, gate[:, :TOP_K].reshape(n)[flat], 0.0)
    row_w = jnp.broadcast_to(row_w[:, :, None], (nb, rows, LANES))
    return block_e, nb_used, slab, row_w


def _expert_kernel(be_ref, nbu_ref, slab_ref, h_ref, roww_ref, wgu_ref, bgu_ref, wd_ref, bd_ref, y_ref,
                   idx0_ref, idx1_ref, idx2_ref, xbuf_ref, ybuf_ref, wgu_bf_ref, wd_bf_ref, isem, gsem, ssem):
    rows = MOE_ROWS
    ring = MOE_RING
    idx_refs = (idx0_ref, idx1_ref, idx2_ref)
    assert len(idx_refs) == ring
    sub = D_MODEL // LANES
    i = pl.program_id(0)
    nbu = nbu_ref[0]
    slot = i % ring
    active = i < nbu
    n_real = y_ref.shape[0] - ring * rows

    def slab_copy(blk, sl):
        return pltpu.make_async_copy(slab_ref.at[blk], idx_refs[sl], isem.at[sl])

    def gather_row(tok, sl, r):
        return pltpu.make_async_copy(h_ref.at[tok], xbuf_ref.at[sl, :, r, :], gsem.at[sl])

    def scatter_row(sl, r, f):
        return pltpu.make_async_copy(ybuf_ref.at[sl, :, r, :], y_ref.at[f], ssem.at[sl])

    def gather_wait(sl):
        pltpu.make_async_copy(xbuf_ref.at[sl], xbuf_ref.at[sl], gsem.at[sl]).wait()

    def scatter_wait(sl):
        pltpu.make_async_copy(ybuf_ref.at[sl], ybuf_ref.at[sl], ssem.at[sl]).wait()

    @pl.when(i == 0)
    def _():
        slab_copy(0, 0).start()
        ybuf_ref[...] = jnp.zeros_like(ybuf_ref)
        for sl in range(ring):
            def fill(r, carry, sl=sl):
                scatter_row(sl, r, n_real + sl * rows + r).start()
                return carry
            lax.fori_loop(0, rows, fill, 0)

    def block(sl):
        nxt, nxt2 = (sl + 1) % ring, (sl + 2) % ring
        idx_ref = idx_refs[sl]
        slab_copy(i, sl).wait()

        @pl.when(i + 1 < nbu)
        def _():
            slab_copy(i + 1, nxt).start()

        if sl == 0:
            @pl.when(i == 0)
            def _():
                def first(r, carry):
                    gather_row(idx_ref[r], 0, r).start()
                    gather_row(idx_ref[2 * rows + r], 1, r).start()
                    return carry
                lax.fori_loop(0, rows, first, 0)

        changed = jnp.logical_or(i == 0, be_ref[i] != be_ref[jnp.maximum(i - 1, 0)])

        @pl.when(changed)
        def _():
            wgu_bf_ref[...] = wgu_ref[0].astype(BF16)
            wd_bf_ref[...] = wd_ref[0].astype(BF16)

        gather_wait(sl)
        scatter_wait(sl)
        xb = jnp.concatenate([xbuf_ref[sl, j] for j in range(sub)], axis=1).astype(BF16)
        hb = _dot(xb, wgu_bf_ref[...]) + bgu_ref[0]
        for r in range(rows):
            gather_row(idx_ref[3 * rows + r], nxt2, r).start(priority=r % 2)
        glu = jnp.minimum(hb[:, :D_FF], SWIGLU_LIMIT)
        lin = jnp.clip(hb[:, D_FF:], -SWIGLU_LIMIT, SWIGLU_LIMIT)
        act = (glu * _sigmoid(SWIGLU_ALPHA * glu) * (lin + 1.0)).astype(BF16)
        yv = (_dot(act, wd_bf_ref[...]) + bd_ref[0]) * roww_ref[0, :, 0:1]
        for j in range(sub):
            ybuf_ref[sl, j] = yv[:, j * LANES:(j + 1) * LANES]
        for r in range(rows):
            scatter_row(sl, r, idx_ref[rows + r]).start(priority=(r + 1) % 2)

        @pl.when(i == nbu - 1)
        def _():
            for s in (sl, nxt, nxt2):
                scatter_wait(s)
            gather_wait(nxt)
            gather_wait(nxt2)

    for sl in range(ring):
        pl.when(jnp.logical_and(active, slot == sl))(functools.partial(block, sl))


def _experts(h2, t, block_e, nb_used, slab, row_w, w_gate_up, b_gate_up, w_down, b_down):
    rows = MOE_ROWS
    ring = MOE_RING
    assert (ring * rows) % TOP_K == 0
    nb = slab.shape[0]
    sub = D_MODEL // LANES
    grid_spec = pltpu.PrefetchScalarGridSpec(
        num_scalar_prefetch=2,
        grid=(nb,),
        in_specs=[pl.BlockSpec(memory_space=pl.ANY),
                  pl.BlockSpec(memory_space=pl.ANY),
                  pl.BlockSpec((1, rows, LANES), lambda i, be, nbu: (i, 0, 0)),
                  pl.BlockSpec((1, D_MODEL, 2 * D_FF), lambda i, be, nbu: (be[i], 0, 0)),
                  pl.BlockSpec((1, 1, 2 * D_FF), lambda i, be, nbu: (be[i], 0, 0)),
                  pl.BlockSpec((1, D_FF, D_MODEL), lambda i, be, nbu: (be[i], 0, 0)),
                  pl.BlockSpec((1, 1, D_MODEL), lambda i, be, nbu: (be[i], 0, 0))],
        out_specs=pl.BlockSpec(memory_space=pl.ANY),
        scratch_shapes=[pltpu.SMEM((slab.shape[1],), jnp.int32)] * ring + [
                        pltpu.VMEM((ring, sub, rows, LANES), F32),
                        pltpu.VMEM((ring, sub, rows, LANES), F32),
                        pltpu.VMEM((D_MODEL, 2 * D_FF), BF16),
                        pltpu.VMEM((D_FF, D_MODEL), BF16),
                        pltpu.SemaphoreType.DMA((ring,)),
                        pltpu.SemaphoreType.DMA((ring,)),
                        pltpu.SemaphoreType.DMA((ring,))])
    return pl.pallas_call(
        _expert_kernel,
        grid_spec=grid_spec,
        out_shape=jax.ShapeDtypeStruct((t * TOP_K + ring * rows, sub, LANES), F32),
        compiler_params=_params(("arbitrary",)),
        name="experts",
    )(block_e, nb_used, slab, h2, row_w, w_gate_up, b_gate_up.reshape(N_EXPERTS, 1, 2 * D_FF), w_down,
      b_down.reshape(N_EXPERTS, 1, D_MODEL))


def _combine_kernel(y_ref, x1_ref, nw_ref, o_ref, sum_ref, *, final):
    total = y_ref[:, 0]
    for k in range(1, TOP_K):
        total = total + y_ref[:, k]
    sum_ref[...] = total
    acc = x1_ref[...] + jnp.concatenate([sum_ref[:, j, :] for j in range(D_MODEL // LANES)], axis=1)
    o_ref[...] = _rms(acc) * nw_ref[...] if final else acc


def _combine(y, x1, row0, rows, norm_w, final):
    tm = _row_tile(rows, row0, (256, 128, 64, 32, 16, 8))
    off = row0 // tm
    sub = D_MODEL // LANES
    y4 = y.reshape(y.shape[0] // TOP_K, TOP_K, sub, LANES)
    return pl.pallas_call(
        functools.partial(_combine_kernel, final=final),
        grid=(rows // tm,),
        in_specs=[pl.BlockSpec((tm, TOP_K, sub, LANES), lambda i: (off + i, 0, 0, 0)),
                  pl.BlockSpec((tm, D_MODEL), lambda i: (off + i, 0)), _resident((1, D_MODEL))],
        out_specs=pl.BlockSpec((tm, D_MODEL), lambda i: (i, 0)),
        out_shape=jax.ShapeDtypeStruct((rows, D_MODEL), F32),
        scratch_shapes=[pltpu.VMEM((tm, sub, LANES), F32)],
        compiler_params=_params(("parallel",)),
        name="combine",
    )(y4, x1, norm_w.astype(F32).reshape(1, D_MODEL))


def kernel(x_prompt, x_sample, state_ret, state_gdn, state_conv, attn_norm_w, w_in, conv_w, a_log, dt_bias, gdn_norm_w, w_branch_a, w_branch_b, w_out, ffn_norm_w, w_router, b_router, w_gate_up, b_gate_up, w_down, b_down, final_norm_w):
    bp, lp, d = x_prompt.shape
    bs, ls, _ = x_sample.shape
    assert ls == 1 and d == D_MODEL and lp % RET_CHUNK == 0 and bs % SAMPLE_TILE == 0
    depth = w_in.shape[0]
    tp = bp * lp
    t = tp + bs
    xp, xs = x_prompt.reshape(tp, d).astype(F32), x_sample.reshape(bs, d).astype(F32)
    rp, gp, cp, rs, gs, cs = [], [], [], [], [], []
    for l in range(depth):
        wb = jnp.pad(w_in[l].astype(BF16), ((0, 0), (0, -w_in.shape[2] % LANES)))
        pp_ret, pp_gdn, pp_gate, pp_ab = _inproj(xp, attn_norm_w[l], wb)
        ps_ret, ps_gdn, ps_gate, ps_ab = _inproj(xs, attn_norm_w[l], wb)
        op_ret, s_ret_p = _ret_prompt(pp_ret, bp, lp)
        os_ret, s_ret_s = _ret_sample(ps_ret.astype(F32), state_ret[l].astype(F32))
        op_gdn, s_gdn_p, conv_p = _gdn_prompt(pp_gdn, pp_ab, bp, lp, conv_w[l], a_log[l], dt_bias[l], gdn_norm_w[l])
        os_gdn, s_gdn_s, conv_s = _gdn_sample(ps_gdn.astype(F32), ps_ab, state_gdn[l].astype(F32), state_conv[l],
                                              conv_w[l], a_log[l], dt_bias[l], gdn_norm_w[l])
        mw = _merge_weights(w_branch_a[l], w_branch_b[l], w_out[l], ffn_norm_w[l], w_router[l], b_router[l])
        x1, h2, top_i, gate = _merge((op_ret, op_gdn, pp_gate, xp), (os_ret, os_gdn, ps_gate, xs), mw)
        block_e, nb_used, slab, row_w = _route(top_i[:t, :TOP_K], gate[:t], t)
        y = _experts(h2, t, block_e, nb_used, slab, row_w, w_gate_up[l], b_gate_up[l], w_down[l], b_down[l])
        last = l == depth - 1
        norm_w = final_norm_w if last else jnp.ones((d,), F32)
        xp = _combine(y, x1, 0, tp, norm_w, last)
        xs = _combine(y, x1, tp, bs, norm_w, last)
        rp.append(s_ret_p); gp.append(s_gdn_p); cp.append(conv_p)
        rs.append(s_ret_s); gs.append(s_gdn_s); cs.append(conv_s)
    y_prompt = xp.reshape(bp, lp, d).astype(x_prompt.dtype)
    y_sample = xs.reshape(bs, ls, d).astype(x_sample.dtype)
    return (y_prompt, y_sample,
            jnp.stack(rp).astype(state_ret.dtype), jnp.stack(gp).astype(state_gdn.dtype),
            jnp.stack(cp).astype(state_conv.dtype),
            jnp.stack(rs).astype(state_ret.dtype), jnp.stack(gs).astype(state_gdn.dtype),
            jnp.stack(cs).astype(state_conv.dtype))
```

```python
import functools
import math

import numpy as np
import jax
import jax.numpy as jnp
from jax import lax
from jax.experimental import pallas as pl
from jax.experimental.pallas import tpu as pltpu

F32 = jnp.float32
BF16 = jnp.bfloat16
HIGHEST = lax.Precision.HIGHEST

D_MODEL = 1024
PAST_LEN = 16384
RET_HEADS, RET_DK, RET_DV = 4, 128, 256
RET_QK_W, RET_V_W = RET_HEADS * RET_DK, RET_HEADS * RET_DV
RET_CHUNK = 128
ROPE_BASE = 10000.0
GDN_HEADS, GDN_DK, GDN_DV = 8, 128, 128
GDN_QK_W, GDN_V_W = GDN_HEADS * GDN_DK, GDN_HEADS * GDN_DV
GDN_CHUNK = 64
CONV_W = 4
CONV_CH = 2 * GDN_QK_W + GDN_V_W
N_EXPERTS = 32
TOP_K = 4
D_FF = D_MODEL
SWIGLU_LIMIT = 7.0
SWIGLU_ALPHA = 1.702
NORM_EPS = 1e-6

_RET_W = 2 * RET_QK_W + 2 * RET_V_W
_GDN_W = CONV_CH + GDN_V_W
_AB_OFF = _RET_W + _GDN_W
_GATE_OFF = _AB_OFF + 2 * GDN_HEADS

LANES = 128
VMEM_LIMIT = 56 * 1024 * 1024
MERGE_ROWS = 512
MOE_ROWS = 256
MOE_RING = 3
SAMPLE_TILE = 8
RET_STEP_ROWS = 256
RET_STEP_SEQS = 2
GDN_PREP_ROWS = 128
GDN_SCAN_ROWS = 128
GDN_SCAN_SEQS = 2


def _pick_tile(n, candidates):
    for c in candidates:
        if n % c == 0:
            return c
    raise ValueError(f"no tile in {candidates} divides {n}")


def _params(sem, vmem=VMEM_LIMIT):
    return pltpu.CompilerParams(dimension_semantics=sem, vmem_limit_bytes=vmem)


def _resident(shape):
    nd = len(shape)
    return pl.BlockSpec(shape, lambda *_: (0,) * nd, pipeline_mode=pl.Buffered(1))


def _silu(x):
    return x * (1.0 / (1.0 + jnp.exp(-x)))


def _sigmoid(x):
    return 1.0 / (1.0 + jnp.exp(-x))


def _softplus(x):
    return jnp.maximum(x, 0.0) + jnp.log1p(jnp.exp(-jnp.abs(x)))


def _rms(x):
    return x * lax.rsqrt(jnp.mean(x * x, axis=-1, keepdims=True) + NORM_EPS)


def _dot(a, b):
    return jnp.dot(a, b, preferred_element_type=F32)


def _dot_nt(a, b):
    return lax.dot_general(a, b, (((1,), (1,)), ((), ())), preferred_element_type=F32)


def _dot_tn(a, b):
    return lax.dot_general(a, b, (((0,), (0,)), ((), ())), preferred_element_type=F32)


def _dot_hi(a, b):
    return jnp.dot(a, b, preferred_element_type=F32, precision=HIGHEST)


def _dot_split(a, b):
    a_hi, b_hi = a.astype(BF16), b.astype(BF16)
    a_lo = (a - a_hi.astype(F32)).astype(BF16)
    b_lo = (b - b_hi.astype(F32)).astype(BF16)
    return _dot(a_hi, b_hi) + (_dot(a_hi, b_lo) + _dot(a_lo, b_hi))


def _row_tile(rows, row0, candidates):
    return _pick_tile(math.gcd(rows, row0) if row0 else rows, candidates)


def _inproj_kernel(x_ref, nw_ref, w_ref, oret_ref, ogdn_ref, ogate_ref, oab_ref):
    h = (_rms(x_ref[...]) * nw_ref[...]).astype(BF16)
    oret_ref[...] = _dot(h, w_ref[:, :_RET_W]).astype(BF16)
    ogdn_ref[...] = _dot(h, w_ref[:, _RET_W:_AB_OFF]).astype(BF16)
    tail = _dot(h, w_ref[:, _AB_OFF:])
    oab_ref[...] = tail[:, :LANES]
    ogate_ref[...] = tail[:, _GATE_OFF - _AB_OFF:_GATE_OFF - _AB_OFF + 2 * D_MODEL].astype(BF16)


def _inproj(x, norm_w, wb):
    rows = x.shape[0]
    tm = _pick_tile(rows, (256, 128, 64, 32, 16, 8))
    row = lambda n: pl.BlockSpec((tm, n), lambda i: (i, 0))
    return pl.pallas_call(
        _inproj_kernel,
        grid=(rows // tm,),
        in_specs=[row(D_MODEL), _resident((1, D_MODEL)), _resident(wb.shape)],
        out_specs=[row(_RET_W), row(_GDN_W), row(2 * D_MODEL), row(LANES)],
        out_shape=[jax.ShapeDtypeStruct((rows, _RET_W), BF16), jax.ShapeDtypeStruct((rows, _GDN_W), BF16),
                   jax.ShapeDtypeStruct((rows, 2 * D_MODEL), BF16), jax.ShapeDtypeStruct((rows, LANES), F32)],
        compiler_params=_params(("parallel",)),
        name="inproj",
    )(x, norm_w.reshape(1, D_MODEL), wb)


def _ret_log_gamma():
    return np.log1p(-np.exp2(-5.0 - np.arange(RET_HEADS, dtype=np.float64)))


def _rope_tables(pos):
    half = RET_DK // 2
    inv = 1.0 / (ROPE_BASE ** (jnp.arange(half, dtype=F32) / half))
    ang = pos.astype(F32)[:, None] * inv[None, :]
    cos, sin = jnp.cos(ang), jnp.sin(ang)
    return jnp.concatenate([cos, cos], axis=-1), jnp.concatenate([-sin, sin], axis=-1)


def _rotary(x, cos, sin):
    return x * cos + pltpu.roll(x, RET_DK // 2, 1) * sin


def _ret_prompt_kernel(q_ref, k_ref, v_ref, g_ref, cos_ref, sin_ref, dmask_ref, qdec_ref, kdec_ref,
                       o_ref, s_ref, *, gammas):
    @pl.when(pl.program_id(1) == 0)
    def _():
        s_ref[...] = jnp.zeros_like(s_ref)

    c = RET_CHUNK
    chains = [(sq, h) for sq in range(q_ref.shape[0]) for h in range(RET_HEADS)]
    nc = range(len(chains))
    qk = [slice(h * RET_DK, (h + 1) * RET_DK) for _, h in chains]
    vv = [slice(h * RET_DV, (h + 1) * RET_DV) for _, h in chains]
    hd = [h for _, h in chains]
    sq = [s_ for s_, _ in chains]
    s = [s_ref[sq[i], hd[i]] for i in nc]
    for ck in range(q_ref.shape[1] // c):
        r = slice(ck * c, (ck + 1) * c)
        cos, sin = cos_ref[r, :], sin_ref[r, :]
        q = [_rotary(q_ref[sq[i], r, qk[i]].astype(F32), cos, sin) for i in nc]
        k = [_rotary(k_ref[sq[i], r, qk[i]].astype(F32), cos, sin) * (RET_DK ** -0.5) for i in nc]
        v = [v_ref[sq[i], r, vv[i]] for i in nc]
        qb = [x.astype(BF16) for x in q]
        inner = [_dot_nt(qb[i], k[i].astype(BF16)) * dmask_ref[hd[i]] for i in nc]
        cross = [_dot((q[i] * qdec_ref[hd[i]]).astype(BF16), s[i].astype(BF16)) for i in nc]
        upd = [_dot_tn((k[i] * kdec_ref[hd[i]]).astype(BF16), v[i]) for i in nc]
        o = [_dot(inner[i].astype(BF16), v[i]) + cross[i] for i in nc]
        s = [s[i] * gammas[hd[i]] + upd[i] for i in nc]
        for i in nc:
            o_ref[sq[i], r, vv[i]] = (_rms(o[i]) * _silu(g_ref[sq[i], r, vv[i]].astype(F32))).astype(BF16)
    for i in nc:
        s_ref[sq[i], hd[i]] = s[i]


def _ret_prompt(p_ret, batch, seq):
    c = RET_CHUNK
    step = RET_STEP_ROWS if seq % RET_STEP_ROWS == 0 else c
    n = seq // step
    lg = _ret_log_gamma()
    idx = np.arange(c, dtype=np.float64)
    diff = idx[:, None] - idx[None, :]
    dmask = np.where(diff >= 0, np.exp(np.maximum(diff, 0.0)[None] * lg[:, None, None]), 0.0)
    qdec = np.broadcast_to(np.exp((idx + 1.0)[None, :] * lg[:, None])[:, :, None], (RET_HEADS, c, RET_DK))
    kdec = np.broadcast_to(np.exp((c - 1.0 - idx)[None, :] * lg[:, None])[:, :, None], (RET_HEADS, c, RET_DK))
    gammas = tuple(float(g) for g in np.exp(c * lg))
    cos, sin = _rope_tables(jnp.arange(seq, dtype=jnp.int32))
    tab = lambda: _resident((RET_HEADS, c, RET_DK))
    nseq = RET_STEP_SEQS if batch % RET_STEP_SEQS == 0 else 1
    p3 = p_ret.reshape(batch, seq, _RET_W)
    o, s_new = pl.pallas_call(
        functools.partial(_ret_prompt_kernel, gammas=gammas),
        grid=(batch // nseq, n),
        in_specs=[pl.BlockSpec((nseq, step, RET_QK_W), lambda b, j: (b, j, 0)),
                  pl.BlockSpec((nseq, step, RET_QK_W), lambda b, j: (b, j, 1)),
                  pl.BlockSpec((nseq, step, RET_V_W), lambda b, j: (b, j, 1)),
                  pl.BlockSpec((nseq, step, RET_V_W), lambda b, j: (b, j, 2)),
                  pl.BlockSpec((step, RET_DK), lambda b, j: (j, 0)),
                  pl.BlockSpec((step, RET_DK), lambda b, j: (j, 0)),
                  tab(), tab(), tab()],
        out_specs=[pl.BlockSpec((nseq, step, RET_V_W), lambda b, j: (b, j, 0)),
                   pl.BlockSpec((nseq, RET_HEADS, RET_DK, RET_DV), lambda b, j: (b, 0, 0, 0))],
        out_shape=[jax.ShapeDtypeStruct((batch, seq, RET_V_W), BF16),
                   jax.ShapeDtypeStruct((batch, RET_HEADS, RET_DK, RET_DV), F32)],
        compiler_params=_params(("parallel", "arbitrary")),
        name="ret_prompt",
    )(p3, p3, p3, p3, cos, sin, jnp.asarray(dmask, F32), jnp.asarray(qdec, F32), jnp.asarray(kdec, F32))
    return o.reshape(batch * seq, RET_V_W), s_new


def _columns(x):
    n = x.shape[0]
    if n < LANES:
        x = jnp.concatenate([x, jnp.zeros((LANES - n, x.shape[1]), x.dtype)], axis=0)
    return x.T


def _ret_sample_kernel(p_ref, cos_ref, sin_ref, s_ref, o_ref, so_ref, *, gammas):
    cos, sin = cos_ref[...], sin_ref[...]
    nb = p_ref.shape[0]
    for h in range(RET_HEADS):
        qk = slice(h * RET_DK, (h + 1) * RET_DK)
        q = _rotary(p_ref[:, qk], cos, sin)
        k = _rotary(p_ref[:, RET_QK_W + h * RET_DK:RET_QK_W + (h + 1) * RET_DK], cos, sin) * (RET_DK ** -0.5)
        v = p_ref[:, 2 * RET_QK_W + h * RET_DV:2 * RET_QK_W + (h + 1) * RET_DV]
        g = p_ref[:, 2 * RET_QK_W + RET_V_W + h * RET_DV:2 * RET_QK_W + RET_V_W + (h + 1) * RET_DV]
        qk_dot = jnp.sum(q * k, axis=-1, keepdims=True)
        qt, kt = _columns(q), _columns(k)
        rows = []
        for j in range(nb):
            s = s_ref[j, h]
            qs = jnp.sum(qt[:, j:j + 1] * s, axis=0, keepdims=True)
            rows.append(qk_dot[j:j + 1] * v[j:j + 1] + gammas[h] * qs)
            so_ref[j, h] = s * gammas[h] + kt[:, j:j + 1] * v[j:j + 1]
        o = jnp.concatenate(rows, axis=0)
        o_ref[:, h * RET_DV:(h + 1) * RET_DV] = (_rms(o) * _silu(g)).astype(BF16)


def _ret_sample(p_ret_s, state):
    ts = p_ret_s.shape[0]
    sb = SAMPLE_TILE
    gammas = tuple(float(g) for g in np.exp(_ret_log_gamma()))
    cos, sin = _rope_tables(jnp.full((1,), PAST_LEN, jnp.int32))
    st = pl.BlockSpec((sb, RET_HEADS, RET_DK, RET_DV), lambda i: (i, 0, 0, 0))
    return pl.pallas_call(
        functools.partial(_ret_sample_kernel, gammas=gammas),
        grid=(ts // sb,),
        in_specs=[pl.BlockSpec((sb, _RET_W), lambda i: (i, 0)), _resident((1, RET_DK)), _resident((1, RET_DK)), st],
        out_specs=[pl.BlockSpec((sb, RET_V_W), lambda i: (i, 0)), st],
        out_shape=[jax.ShapeDtypeStruct((ts, RET_V_W), BF16), jax.ShapeDtypeStruct(state.shape, F32)],
        compiler_params=_params(("parallel",)),
        name="ret_sample",
    )(p_ret_s, cos, sin, state)


def _l2norm(x):
    return x * lax.rsqrt(jnp.sum(x * x, axis=-1, keepdims=True) + NORM_EPS)


def _bdot(a, b):
    return _dot(a.astype(BF16), b.astype(BF16))


def _chunk_masks(c):
    ri = lax.broadcasted_iota(jnp.int32, (c, c), 0)
    ci = lax.broadcasted_iota(jnp.int32, (c, c), 1)
    eye = (ri == ci).astype(F32)
    diag16 = (ri // 16 == ci // 16).astype(F32)
    low32 = jnp.logical_and(ri // 32 == ci // 32, ri // 16 > ci // 16).astype(F32)
    low64 = (ri // 32 > ci // 32).astype(F32)
    return ri >= ci, ri > ci, eye, diag16, low32, low64


def _unit_lower_inverse(a, eye, diag16, low32, low64):
    many = lambda f, *ls: [f(*args) for args in zip(*ls)]
    c = eye.shape[0]
    pair = lambda p, q, rhs: _bdot(jnp.concatenate([p, q], axis=0), rhs)
    n = [-(x * diag16) for x in a]
    n2 = many(_bdot, n, n)
    n34 = many(pair, n, n2, n2)
    x = [eye + p + q + r[:c] for p, q, r in zip(n, n2, n34)]
    n4 = [r[c:] for r in n34]
    xn = many(pair, x, n4, n4)
    x = [u + v[:c] for u, v in zip(x, xn)]
    x = many(lambda u, v: u + v, x, many(_bdot, x, [v[c:] for v in xn]))
    for mask in (low32, low64):
        r = many(_bdot, [y * mask for y in a], x)
        x = many(lambda u, v: u - v, x, many(_bdot, x, r))
    return x


def _gdn_gates(ab, alog, dtb):
    g = -jnp.exp(alog) * _softplus(ab + dtb)
    return g, _sigmoid(ab)


def _gdn_prep_kernel(x_ref, prev_ref, ab_ref, cw_ref, alog_ref, dtb_ref,
                     u_ref, w_ref, qg_ref, kg_ref, at_ref, eg_ref, cv_ref, act_ref):
    c = GDN_CHUNK
    rows = x_ref.shape[0]
    x = x_ref[...]
    halo = prev_ref.shape[0]
    prev = jnp.where(pl.program_id(1) == 0, jnp.zeros_like(prev_ref), prev_ref[...])
    xcat = jnp.concatenate([prev, x], axis=0)
    ti = lax.broadcasted_iota(jnp.int32, (rows, rows + halo), 0)
    ui = lax.broadcasted_iota(jnp.int32, (rows, rows + halo), 1)
    xf = x.astype(F32)
    acc = xf * cw_ref[CONV_W - 1:CONV_W, :]
    for i in range(CONV_W - 1):
        shift = (ui == ti + (halo - (CONV_W - 1) + i)).astype(BF16)
        acc = acc + _dot(shift, xcat) * cw_ref[i:i + 1, :]
    cv_ref[0] = xf[rows - (CONV_W - 1):, :]
    act_ref[...] = _silu(acc)

    g_all, beta_all = _gdn_gates(ab_ref[...], alog_ref[...], dtb_ref[...])
    lower, strict, eye, diag16, low32, low64 = _chunk_masks(c)
    gc_all, gr_all = [], []
    for ck in range(rows // c):
        gc_ck = _dot_hi(lower.astype(F32), g_all[ck * c:(ck + 1) * c])
        gc_all.append(gc_ck)
        gr_all.append(_columns(gc_ck))
        eg_ref[ck] = jnp.exp(gc_ck[c - 1:c, :])
    chains = [(ck, h) for ck in range(rows // c) for h in range(GDN_HEADS)]
    rs = [slice(ck * c, (ck + 1) * c) for ck, _ in chains]
    sl = [slice(h * GDN_DK, (h + 1) * GDN_DK) for _, h in chains]
    nc = range(len(chains))
    q = [_l2norm(act_ref[rs[i], sl[i]]) * (GDN_DK ** -0.5) for i in nc]
    k = [_l2norm(act_ref[rs[i], GDN_QK_W + sl[i].start:GDN_QK_W + sl[i].stop]) for i in nc]
    v = [act_ref[rs[i], 2 * GDN_QK_W + sl[i].start:2 * GDN_QK_W + sl[i].stop] for i in nc]
    beta = [beta_all[rs[i], GDN_HEADS + h:GDN_HEADS + h + 1] for i, (_, h) in enumerate(chains)]
    gc = [gc_all[ck][:, h:h + 1] for ck, h in chains]
    gr = [gr_all[ck][h:h + 1, :c] for ck, h in chains]
    decay = [jnp.exp(jnp.where(lower, gc[i] - gr[i], -jnp.inf)) for i in nc]
    exp_g = [jnp.exp(x) for x in gc]
    kb = [k[i] * beta[i] for i in nc]
    kbf = [x.astype(BF16) for x in k]
    kq = [_dot_nt(jnp.concatenate([kb[i].astype(BF16), q[i].astype(BF16)], axis=0), kbf[i]) for i in nc]
    a = [kq[i][:c] * jnp.where(strict, decay[i], 0.0) for i in nc]
    attn = [kq[i][c:] * decay[i] for i in nc]
    t = _unit_lower_inverse(a, eye, diag16, low32, low64)
    uw = [_bdot(t[i], jnp.concatenate([v[i] * beta[i], kb[i] * exp_g[i]], axis=1)) for i in nc]
    for i in nc:
        u_ref[rs[i], sl[i]] = uw[i][:, :GDN_DV]
        w_ref[rs[i], sl[i]] = uw[i][:, GDN_DV:].astype(BF16)
        at_ref[rs[i], sl[i]] = jnp.concatenate([attn[i], jnp.zeros((c, GDN_DK - c), F32)], axis=1).astype(BF16)
        qg_ref[rs[i], sl[i]] = (q[i] * exp_g[i]).astype(BF16)
        kg_ref[rs[i], sl[i]] = (k[i] * jnp.exp(gc[i][c - 1:c, :] - gc[i])).astype(BF16)


def _gdn_scan_kernel(u_ref, w_ref, qg_ref, kg_ref, at_ref, eg_ref, z_ref, nw_ref, o_ref, s_ref):
    c = GDN_CHUNK

    @pl.when(pl.program_id(1) == 0)
    def _():
        s_ref[...] = jnp.zeros_like(s_ref)

    nw = nw_ref[...]
    chains = [(q, h) for q in range(u_ref.shape[0]) for h in range(GDN_HEADS)]
    sl = [slice(h * GDN_DK, (h + 1) * GDN_DK) for _, h in chains]
    nc = range(len(chains))
    s = [s_ref[q, h] for q, h in chains]
    for ck in range(u_ref.shape[1] // c):
        r = slice(ck * c, (ck + 1) * c)
        sb = [x.astype(BF16) for x in s]
        ws = [_dot(w_ref[q, r, sl[i]], sb[i]) for i, (q, _) in enumerate(chains)]
        qs = [_dot(qg_ref[q, r, sl[i]], sb[i]) for i, (q, _) in enumerate(chains)]
        vnb = [(u_ref[q, r, sl[i]] - ws[i]).astype(BF16) for i, (q, _) in enumerate(chains)]
        o = [qs[i] + _dot(at_ref[q, r, h * GDN_DK:h * GDN_DK + c], vnb[i]) for i, (q, h) in enumerate(chains)]
        s = [s[i] * eg_ref[q, ck][:, h:h + 1] + _dot_tn(kg_ref[q, r, sl[i]], vnb[i])
             for i, (q, h) in enumerate(chains)]
        for i, (q, _) in enumerate(chains):
            o_ref[q, r, sl[i]] = (_rms(o[i]) * nw * _silu(z_ref[q, r, sl[i]].astype(F32))).astype(BF16)
    for i, (q, h) in enumerate(chains):
        s_ref[q, h] = s[i]


def _gdn_prompt(p_gdn, p_ab, batch, seq, conv_w, a_log, dt_bias, gdn_norm_w):
    c = GDN_CHUNK
    rows = GDN_PREP_ROWS
    t = batch * seq
    nt = seq // rows
    alog = jnp.pad(a_log.astype(F32), (0, LANES - GDN_HEADS)).reshape(1, LANES)
    dtb = jnp.pad(dt_bias.astype(F32), (0, LANES - GDN_HEADS)).reshape(1, LANES)
    wide = lambda: pl.BlockSpec((rows, GDN_V_W), lambda b, j: (b * nt + j, 0))
    u, w, qg, kg, at, eg, conv_new = pl.pallas_call(
        _gdn_prep_kernel,
        grid=(batch, nt),
        in_specs=[pl.BlockSpec((rows, CONV_CH), lambda b, j: (b * nt + j, 0)),
                  pl.BlockSpec((16, CONV_CH), lambda b, j: (jnp.maximum((b * nt + j) * (rows // 16) - 1, 0), 0)),
                  pl.BlockSpec((rows, LANES), lambda b, j: (b * nt + j, 0)),
                  _resident((CONV_W, CONV_CH)), _resident((1, LANES)), _resident((1, LANES))],
        out_specs=[wide(), wide(), wide(), wide(), wide(),
                   pl.BlockSpec((rows // c, 1, LANES), lambda b, j: (b * nt + j, 0, 0)),
                   pl.BlockSpec((1, CONV_W - 1, CONV_CH), lambda b, j: (b, 0, 0))],
        out_shape=[jax.ShapeDtypeStruct((t, GDN_V_W), F32)] + [jax.ShapeDtypeStruct((t, GDN_V_W), BF16)] * 4
        + [jax.ShapeDtypeStruct((t // c, 1, LANES), F32),
           jax.ShapeDtypeStruct((batch, CONV_W - 1, CONV_CH), F32)],
        scratch_shapes=[pltpu.VMEM((rows, CONV_CH), F32)],
        compiler_params=_params(("parallel", "arbitrary")),
        name="gdn_prep",
    )(p_gdn, p_gdn, p_ab, conv_w.astype(F32), alog, dtb)
    srows = GDN_SCAN_ROWS
    n = seq // srows
    nq = CONV_CH // GDN_V_W
    nseq = GDN_SCAN_SEQS if batch % GDN_SCAN_SEQS == 0 else 1
    per_seq = lambda a: a.reshape((batch, seq // (t // a.shape[0])) + a.shape[1:])
    blk = lambda: pl.BlockSpec((nseq, srows, GDN_V_W), lambda b, j: (b, j, 0))
    o, s_new = pl.pallas_call(
        _gdn_scan_kernel,
        grid=(batch // nseq, n),
        in_specs=[blk(), blk(), blk(), blk(), blk(),
                  pl.BlockSpec((nseq, srows // c, 1, LANES), lambda b, j: (b, j, 0, 0)),
                  pl.BlockSpec((nseq, srows, GDN_V_W), lambda b, j: (b, j, nq)),
                  _resident((1, GDN_DV))],
        out_specs=[blk(), pl.BlockSpec((nseq, GDN_HEADS, GDN_DK, GDN_DV), lambda b, j: (b, 0, 0, 0))],
        out_shape=[jax.ShapeDtypeStruct((batch, seq, GDN_V_W), BF16),
                   jax.ShapeDtypeStruct((batch, GDN_HEADS, GDN_DK, GDN_DV), F32)],
        compiler_params=_params(("parallel", "arbitrary")),
        name="gdn_scan",
    )(per_seq(u), per_seq(w), per_seq(qg), per_seq(kg), per_seq(at), per_seq(eg), per_seq(p_gdn),
      gdn_norm_w.astype(F32).reshape(1, GDN_DV))
    return o.reshape(t, GDN_V_W), s_new, conv_new


def _gdn_sample_kernel(x_ref, ab_ref, sc_ref, cw_ref, alog_ref, dtb_ref, nw_ref, s_ref, o_ref, so_ref, sco_ref):
    nb = x_ref.shape[0]
    x = x_ref[:, :CONV_CH]
    acc = x * cw_ref[CONV_W - 1:CONV_W, :]
    for i in range(CONV_W - 1):
        acc = acc + sc_ref[i] * cw_ref[i:i + 1, :]
    for i in range(CONV_W - 2):
        sco_ref[i] = sc_ref[i + 1]
    sco_ref[CONV_W - 2] = x
    u = _silu(acc)
    g_all, beta_all = _gdn_gates(ab_ref[...], alog_ref[...], dtb_ref[...])
    eg_all = jnp.exp(g_all)
    nw = nw_ref[...]
    for h in range(GDN_HEADS):
        sl = slice(h * GDN_DK, (h + 1) * GDN_DK)
        q = _l2norm(u[:, sl]) * (GDN_DK ** -0.5)
        k = _l2norm(u[:, GDN_QK_W + h * GDN_DK:GDN_QK_W + (h + 1) * GDN_DK])
        v = u[:, 2 * GDN_QK_W + h * GDN_DV:2 * GDN_QK_W + (h + 1) * GDN_DV]
        beta = beta_all[:, GDN_HEADS + h:GDN_HEADS + h + 1]
        eg = eg_all[:, h:h + 1]
        qk_dot = jnp.sum(q * k, axis=-1, keepdims=True)
        qt, kt = _columns(q), _columns(k)
        rows = []
        for j in range(nb):
            s = s_ref[j, h]
            kcol = kt[:, j:j + 1]
            ks = jnp.sum(kcol * s, axis=0, keepdims=True)
            qs = jnp.sum(qt[:, j:j + 1] * s, axis=0, keepdims=True)
            ej = eg[j:j + 1]
            v_new = beta[j:j + 1] * (v[j:j + 1] - ej * ks)
            rows.append(ej * qs + qk_dot[j:j + 1] * v_new)
            so_ref[j, h] = s * ej + kcol * v_new
        o = jnp.concatenate(rows, axis=0)
        z = x_ref[:, CONV_CH + h * GDN_DV:CONV_CH + (h + 1) * GDN_DV]
        o_ref[:, sl] = (_rms(o) * nw * _silu(z)).astype(BF16)


def _gdn_sample(p_gdn_s, p_ab_s, state, conv_state, conv_w, a_log, dt_bias, gdn_norm_w):
    ts = p_gdn_s.shape[0]
    sb = SAMPLE_TILE
    alog = jnp.pad(a_log.astype(F32), (0, LANES - GDN_HEADS)).reshape(1, LANES)
    dtb = jnp.pad(dt_bias.astype(F32), (0, LANES - GDN_HEADS)).reshape(1, LANES)
    sc = jnp.swapaxes(conv_state.astype(F32), 0, 1)
    st = pl.BlockSpec((sb, GDN_HEADS, GDN_DK, GDN_DV), lambda i: (i, 0, 0, 0))
    scs = pl.BlockSpec((CONV_W - 1, sb, CONV_CH), lambda i: (0, i, 0))
    o, s_new, sc_new = pl.pallas_call(
        _gdn_sample_kernel,
        grid=(ts // sb,),
        in_specs=[pl.BlockSpec((sb, _GDN_W), lambda i: (i, 0)), pl.BlockSpec((sb, LANES), lambda i: (i, 0)), scs,
                  _resident((CONV_W, CONV_CH)), _resident((1, LANES)), _resident((1, LANES)),
                  _resident((1, GDN_DV)), st],
        out_specs=[pl.BlockSpec((sb, GDN_V_W), lambda i: (i, 0)), st, scs],
        out_shape=[jax.ShapeDtypeStruct((ts, GDN_V_W), BF16), jax.ShapeDtypeStruct(state.shape, F32),
                   jax.ShapeDtypeStruct(sc.shape, F32)],
        compiler_params=_params(("parallel",)),
        name="gdn_sample",
    )(p_gdn_s, p_ab_s, sc, conv_w.astype(F32), alog, dtb, gdn_norm_w.astype(F32).reshape(1, GDN_DV), state)
    return o, s_new, jnp.swapaxes(sc_new, 0, 1)


_NO_EXPERT = -1e30


def _merge_kernel(oa0_ref, ob0_ref, gate0_ref, x0_ref, oa1_ref, ob1_ref, gate1_ref, x1in_ref,
                  wa_ref, wb_ref, wo_ref, nw_ref, wr_ref, br_ref, x1_ref, h2_ref, ti_ref, tw_ref, *, n_first):
    first = pl.program_id(0) < n_first
    pick = lambda a, b: jnp.where(first, a[...], b[...])
    gate = pick(gate0_ref, gate1_ref)
    ya = _dot(pick(oa0_ref, oa1_ref), wa_ref[...])
    yb = _dot(pick(ob0_ref, ob1_ref), wb_ref[...])
    ga = gate[:, :D_MODEL].astype(F32)
    gb = gate[:, D_MODEL:].astype(F32)
    m = _sigmoid(ga) * ya + _sigmoid(gb) * yb
    x1 = pick(x0_ref, x1in_ref) + _dot(m.astype(BF16), wo_ref[...])
    x1_ref[...] = x1
    h2 = _rms(x1) * nw_ref[...]
    for j in range(D_MODEL // LANES):
        h2_ref[:, j, :] = h2[:, j * LANES:(j + 1) * LANES]
    lg = _dot_split(h2, wr_ref[...]) + br_ref[...]
    lane = lax.broadcasted_iota(jnp.int32, lg.shape, 1).astype(F32)
    vals, idxs = [], []
    for _ in range(TOP_K):
        top = jnp.max(lg, axis=-1, keepdims=True)
        idx = jnp.min(jnp.where(lg == top, lane, float(LANES)), axis=-1, keepdims=True)
        vals.append(top)
        idxs.append(idx)
        lg = jnp.where(lane == idx, _NO_EXPERT, lg)
    es = [jnp.exp(v - vals[0]) for v in vals]
    inv_total = 1.0 / functools.reduce(lambda a, b: a + b, es)
    ti = jnp.zeros_like(lg)
    tw = jnp.zeros_like(lg)
    for k in range(TOP_K):
        ti = jnp.where(lane == float(k), idxs[k], ti)
        tw = jnp.where(lane == float(k), es[k] * inv_total, tw)
    ti_ref[...] = ti.astype(jnp.int32)
    tw_ref[...] = tw


def _merge(group0, group1, weights):
    r0 = group0[3].shape[0]
    tm = _pick_tile(r0, (MERGE_ROWS, 128, 64, 32, 16, 8))
    pad = -group1[3].shape[0] % tm
    group1 = tuple(jnp.pad(a, ((0, pad), (0, 0))) for a in group1)
    r1 = group1[3].shape[0]
    n0, total = r0 // tm, r0 + r1
    widths = (RET_V_W, GDN_V_W, 2 * D_MODEL, D_MODEL)
    specs0 = [pl.BlockSpec((tm, n), lambda i: (jnp.minimum(i, n0 - 1), 0)) for n in widths]
    specs1 = [pl.BlockSpec((tm, n), lambda i: (jnp.maximum(i - n0, 0), 0)) for n in widths]
    out = lambda n: pl.BlockSpec((tm, n), lambda i: (i, 0))
    sub = D_MODEL // LANES
    sq = (D_MODEL, D_MODEL)
    return pl.pallas_call(
        functools.partial(_merge_kernel, n_first=n0),
        grid=(total // tm,),
        in_specs=specs0 + specs1 + [_resident(sq), _resident(sq), _resident(sq), _resident((1, D_MODEL)),
                                    _resident((D_MODEL, LANES)), _resident((1, LANES))],
        out_specs=[out(D_MODEL), pl.BlockSpec((tm, sub, LANES), lambda i: (i, 0, 0)), out(LANES), out(LANES)],
        out_shape=[jax.ShapeDtypeStruct((total, D_MODEL), F32), jax.ShapeDtypeStruct((total, sub, LANES), F32),
                   jax.ShapeDtypeStruct((total, LANES), jnp.int32), jax.ShapeDtypeStruct((total, LANES), F32)],
        compiler_params=_params(("parallel",)),
        name="merge",
    )(*group0, *group1, *weights)


def _merge_weights(w_a, w_b, w_o, ffn_norm_w, w_router, b_router):
    wr = jnp.pad(w_router.astype(F32), ((0, 0), (0, LANES - N_EXPERTS)))
    br = jnp.pad(b_router.astype(F32), (0, LANES - N_EXPERTS), constant_values=_NO_EXPERT).reshape(1, LANES)
    return (w_a.astype(BF16), w_b.astype(BF16), w_o.astype(BF16), ffn_norm_w.astype(F32).reshape(1, D_MODEL), wr, br)


def _route(top_i, gate, n_tokens):
    rows = MOE_ROWS
    n = n_tokens * TOP_K
    flat_e = top_i.reshape(n).astype(jnp.int32)
    bits = max(1, (n - 1).bit_length())
    assert bits + (N_EXPERTS - 1).bit_length() <= 31
    order = lax.sort((flat_e << bits) | jnp.arange(n, dtype=jnp.int32), is_stable=False) & ((1 << bits) - 1)
    counts = jnp.sum((flat_e[:, None] == jnp.arange(N_EXPERTS, dtype=jnp.int32)[None, :]).astype(jnp.int32), axis=0)
    start = jnp.cumsum(counts) - counts
    pcounts = (counts + rows - 1) // rows * rows
    pend = jnp.cumsum(pcounts)
    pstart = pend - pcounts
    nb = -(-n // rows) + N_EXPERTS
    blk = jnp.arange(nb, dtype=jnp.int32)
    block_e = jnp.minimum(jnp.sum((pend[None, :] <= (blk * rows)[:, None]).astype(jnp.int32), axis=1),
                          N_EXPERTS - 1).astype(jnp.int32)
    nb_used = (pend[-1] // rows).astype(jnp.int32).reshape(1)
    within = (blk * rows - pstart[block_e])[:, None] + jnp.arange(rows, dtype=jnp.int32)[None, :]
    valid = jnp.logical_and(within < counts[block_e][:, None], (blk < nb_used[0])[:, None])
    flat = order[jnp.clip(start[block_e][:, None] + within, 0, n - 1)]
    spare = n + (blk % MOE_RING)[:, None] * rows + jnp.arange(rows, dtype=jnp.int32)[None, :]
    src = jnp.where(valid, flat // TOP_K, 0)
    dst = jnp.where(valid, flat, spare)
    ahead = lambda k: jnp.concatenate([src[k:]] + [src[-1:]] * k, axis=0)
    slab = jnp.concatenate([src, dst, ahead(1), ahead(2)], axis=1)
    row_w = jnp.where(valid, gate[:, :TOP_K].reshape(n)[flat], 0.0)
    row_w = jnp.broadcast_to(row_w[:, :, None], (nb, rows, LANES))
    return block_e, nb_used, slab, row_w


def _expert_kernel(be_ref, nbu_ref, slab_ref, h_ref, roww_ref, wgu_ref, bgu_ref, wd_ref, bd_ref, y_ref,
                   idx0_ref, idx1_ref, idx2_ref, xbuf_ref, ybuf_ref, wgu_bf_ref, wd_bf_ref, isem, gsem, ssem):
    rows = MOE_ROWS
    ring = MOE_RING
    idx_refs = (idx0_ref, idx1_ref, idx2_ref)
    assert len(idx_refs) == ring
    sub = D_MODEL // LANES
    i = pl.program_id(0)
    nbu = nbu_ref[0]
    slot = i % ring
    active = i < nbu
    n_real = y_ref.shape[0] - ring * rows

    def slab_copy(blk, sl):
        return pltpu.make_async_copy(slab_ref.at[blk], idx_refs[sl], isem.at[sl])

    def gather_row(tok, sl, r):
        return pltpu.make_async_copy(h_ref.at[tok], xbuf_ref.at[sl, :, r, :], gsem.at[sl])

    def scatter_row(sl, r, f):
        return pltpu.make_async_copy(ybuf_ref.at[sl, :, r, :], y_ref.at[f], ssem.at[sl])

    def gather_wait(sl):
        pltpu.make_async_copy(xbuf_ref.at[sl], xbuf_ref.at[sl], gsem.at[sl]).wait()

    def scatter_wait(sl):
        pltpu.make_async_copy(ybuf_ref.at[sl], ybuf_ref.at[sl], ssem.at[sl]).wait()

    @pl.when(i == 0)
    def _():
        slab_copy(0, 0).start()
        ybuf_ref[...] = jnp.zeros_like(ybuf_ref)
        for sl in range(ring):
            def fill(r, carry, sl=sl):
                scatter_row(sl, r, n_real + sl * rows + r).start()
                return carry
            lax.fori_loop(0, rows, fill, 0)

    def block(sl):
        nxt, nxt2 = (sl + 1) % ring, (sl + 2) % ring
        idx_ref = idx_refs[sl]
        slab_copy(i, sl).wait()

        @pl.when(i + 1 < nbu)
        def _():
            slab_copy(i + 1, nxt).start()

        if sl == 0:
            @pl.when(i == 0)
            def _():
                def first(r, carry):
                    gather_row(idx_ref[r], 0, r).start()
                    gather_row(idx_ref[2 * rows + r], 1, r).start()
                    return carry
                lax.fori_loop(0, rows, first, 0)

        changed = jnp.logical_or(i == 0, be_ref[i] != be_ref[jnp.maximum(i - 1, 0)])

        @pl.when(changed)
        def _():
            wgu_bf_ref[...] = wgu_ref[0].astype(BF16)
            wd_bf_ref[...] = wd_ref[0].astype(BF16)

        gather_wait(sl)
        scatter_wait(sl)
        xb = jnp.concatenate([xbuf_ref[sl, j] for j in range(sub)], axis=1).astype(BF16)
        hb = _dot(xb, wgu_bf_ref[...]) + bgu_ref[0]
        for r in range(rows):
            gather_row(idx_ref[3 * rows + r], nxt2, r).start(priority=r % 2)
        glu = jnp.minimum(hb[:, :D_FF], SWIGLU_LIMIT)
        lin = jnp.clip(hb[:, D_FF:], -SWIGLU_LIMIT, SWIGLU_LIMIT)
        act = (glu * _sigmoid(SWIGLU_ALPHA * glu) * (lin + 1.0)).astype(BF16)
        yv = (_dot(act, wd_bf_ref[...]) + bd_ref[0]) * roww_ref[0, :, 0:1]
        for j in range(sub):
            ybuf_ref[sl, j] = yv[:, j * LANES:(j + 1) * LANES]
        for r in range(rows):
            scatter_row(sl, r, idx_ref[rows + r]).start(priority=(r + 1) % 2)

        @pl.when(i == nbu - 1)
        def _():
            for s in (sl, nxt, nxt2):
                scatter_wait(s)
            gather_wait(nxt)
            gather_wait(nxt2)

    for sl in range(ring):
        pl.when(jnp.logical_and(active, slot == sl))(functools.partial(block, sl))


def _experts(h2, t, block_e, nb_used, slab, row_w, w_gate_up, b_gate_up, w_down, b_down):
    rows = MOE_ROWS
    ring = MOE_RING
    assert (ring * rows) % TOP_K == 0
    nb = slab.shape[0]
    sub = D_MODEL // LANES
    grid_spec = pltpu.PrefetchScalarGridSpec(
        num_scalar_prefetch=2,
        grid=(nb,),
        in_specs=[pl.BlockSpec(memory_space=pl.ANY),
                  pl.BlockSpec(memory_space=pl.ANY),
                  pl.BlockSpec((1, rows, LANES), lambda i, be, nbu: (i, 0, 0)),
                  pl.BlockSpec((1, D_MODEL, 2 * D_FF), lambda i, be, nbu: (be[i], 0, 0)),
                  pl.BlockSpec((1, 1, 2 * D_FF), lambda i, be, nbu: (be[i], 0, 0)),
                  pl.BlockSpec((1, D_FF, D_MODEL), lambda i, be, nbu: (be[i], 0, 0)),
                  pl.BlockSpec((1, 1, D_MODEL), lambda i, be, nbu: (be[i], 0, 0))],
        out_specs=pl.BlockSpec(memory_space=pl.ANY),
        scratch_shapes=[pltpu.SMEM((slab.shape[1],), jnp.int32)] * ring + [
                        pltpu.VMEM((ring, sub, rows, LANES), F32),
                        pltpu.VMEM((ring, sub, rows, LANES), F32),
                        pltpu.VMEM((D_MODEL, 2 * D_FF), BF16),
                        pltpu.VMEM((D_FF, D_MODEL), BF16),
                        pltpu.SemaphoreType.DMA((ring,)),
                        pltpu.SemaphoreType.DMA((ring,)),
                        pltpu.SemaphoreType.DMA((ring,))])
    return pl.pallas_call(
        _expert_kernel,
        grid_spec=grid_spec,
        out_shape=jax.ShapeDtypeStruct((t * TOP_K + ring * rows, sub, LANES), F32),
        compiler_params=_params(("arbitrary",)),
        name="experts",
    )(block_e, nb_used, slab, h2, row_w, w_gate_up, b_gate_up.reshape(N_EXPERTS, 1, 2 * D_FF), w_down,
      b_down.reshape(N_EXPERTS, 1, D_MODEL))


def _combine_kernel(y_ref, x1_ref, nw_ref, o_ref, sum_ref, *, final):
    total = y_ref[:, 0]
    for k in range(1, TOP_K):
        total = total + y_ref[:, k]
    sum_ref[...] = total
    acc = x1_ref[...] + jnp.concatenate([sum_ref[:, j, :] for j in range(D_MODEL // LANES)], axis=1)
    o_ref[...] = _rms(acc) * nw_ref[...] if final else acc


def _combine(y, x1, row0, rows, norm_w, final):
    tm = _row_tile(rows, row0, (256, 128, 64, 32, 16, 8))
    off = row0 // tm
    sub = D_MODEL // LANES
    y4 = y.reshape(y.shape[0] // TOP_K, TOP_K, sub, LANES)
    return pl.pallas_call(
        functools.partial(_combine_kernel, final=final),
        grid=(rows // tm,),
        in_specs=[pl.BlockSpec((tm, TOP_K, sub, LANES), lambda i: (off + i, 0, 0, 0)),
                  pl.BlockSpec((tm, D_MODEL), lambda i: (off + i, 0)), _resident((1, D_MODEL))],
        out_specs=pl.BlockSpec((tm, D_MODEL), lambda i: (i, 0)),
        out_shape=jax.ShapeDtypeStruct((rows, D_MODEL), F32),
        scratch_shapes=[pltpu.VMEM((tm, sub, LANES), F32)],
        compiler_params=_params(("parallel",)),
        name="combine",
    )(y4, x1, norm_w.astype(F32).reshape(1, D_MODEL))


def kernel(x_prompt, x_sample, state_ret, state_gdn, state_conv, attn_norm_w, w_in, conv_w, a_log, dt_bias, gdn_norm_w, w_branch_a, w_branch_b, w_out, ffn_norm_w, w_router, b_router, w_gate_up, b_gate_up, w_down, b_down, final_norm_w):
    bp, lp, d = x_prompt.shape
    bs, ls, _ = x_sample.shape
    assert ls == 1 and d == D_MODEL and lp % RET_CHUNK == 0 and bs % SAMPLE_TILE == 0
    depth = w_in.shape[0]
    tp = bp * lp
    t = tp + bs
    xp, xs = x_prompt.reshape(tp, d).astype(F32), x_sample.reshape(bs, d).astype(F32)
    rp, gp, cp, rs, gs, cs = [], [], [], [], [], []
    for l in range(depth):
        wb = jnp.pad(w_in[l].astype(BF16), ((0, 0), (0, -w_in.shape[2] % LANES)))
        pp_ret, pp_gdn, pp_gate, pp_ab = _inproj(xp, attn_norm_w[l], wb)
        ps_ret, ps_gdn, ps_gate, ps_ab = _inproj(xs, attn_norm_w[l], wb)
        op_ret, s_ret_p = _ret_prompt(pp_ret, bp, lp)
        os_ret, s_ret_s = _ret_sample(ps_ret.astype(F32), state_ret[l].astype(F32))
        op_gdn, s_gdn_p, conv_p = _gdn_prompt(pp_gdn, pp_ab, bp, lp, conv_w[l], a_log[l], dt_bias[l], gdn_norm_w[l])
        os_gdn, s_gdn_s, conv_s = _gdn_sample(ps_gdn.astype(F32), ps_ab, state_gdn[l].astype(F32), state_conv[l],
                                              conv_w[l], a_log[l], dt_bias[l], gdn_norm_w[l])
        mw = _merge_weights(w_branch_a[l], w_branch_b[l], w_out[l], ffn_norm_w[l], w_router[l], b_router[l])
        x1, h2, top_i, gate = _merge((op_ret, op_gdn, pp_gate, xp), (os_ret, os_gdn, ps_gate, xs), mw)
        block_e, nb_used, slab, row_w = _route(top_i[:t, :TOP_K], gate[:t], t)
        y = _experts(h2, t, block_e, nb_used, slab, row_w, w_gate_up[l], b_gate_up[l], w_down[l], b_down[l])
        last = l == depth - 1
        norm_w = final_norm_w if last else jnp.ones((d,), F32)
        xp = _combine(y, x1, 0, tp, norm_w, last)
        xs = _combine(y, x1, tp, bs, norm_w, last)
        rp.append(s_ret_p); gp.append(s_gdn_p); cp.append(conv_p)
        rs.append(s_ret_s); gs.append(s_gdn_s); cs.append(conv_s)
    y_prompt = xp.reshape(bp, lp, d).astype(x_prompt.dtype)
    y_sample = xs.reshape(bs, ls, d).astype(x_sample.dtype)
    return (y_prompt, y_sample,
            jnp.stack(rp).astype(state_ret.dtype), jnp.stack(gp).astype(state_gdn.dtype),
            jnp.stack(cp).astype(state_conv.dtype),
            jnp.stack(rs).astype(state_ret.dtype), jnp.stack(gs).astype(state_gdn.dtype),
            jnp.stack(cs).astype(state_conv.dtype))
```

```python
import functools
import math

import numpy as np
import jax
import jax.numpy as jnp
from jax import lax
from jax.experimental import pallas as pl
from jax.experimental.pallas import tpu as pltpu

F32 = jnp.float32
BF16 = jnp.bfloat16
HIGHEST = lax.Precision.HIGHEST

D_MODEL = 1024
PAST_LEN = 16384
RET_HEADS, RET_DK, RET_DV = 4, 128, 256
RET_QK_W, RET_V_W = RET_HEADS * RET_DK, RET_HEADS * RET_DV
RET_CHUNK = 128
ROPE_BASE = 10000.0
GDN_HEADS, GDN_DK, GDN_DV = 8, 128, 128
GDN_QK_W, GDN_V_W = GDN_HEADS * GDN_DK, GDN_HEADS * GDN_DV
GDN_CHUNK = 64
CONV_W = 4
CONV_CH = 2 * GDN_QK_W + GDN_V_W
N_EXPERTS = 32
TOP_K = 4
D_FF = D_MODEL
SWIGLU_LIMIT = 7.0
SWIGLU_ALPHA = 1.702
NORM_EPS = 1e-6

_RET_W = 2 * RET_QK_W + 2 * RET_V_W
_GDN_W = CONV_CH + GDN_V_W
_AB_OFF = _RET_W + _GDN_W
_GATE_OFF = _AB_OFF + 2 * GDN_HEADS

LANES = 128
VMEM_LIMIT = 56 * 1024 * 1024
MERGE_ROWS = 512
MOE_ROWS = 256
MOE_RING = 3
SAMPLE_TILE = 8
RET_STEP_ROWS = 256
RET_STEP_SEQS = 2
GDN_PREP_ROWS = 128
GDN_SCAN_ROWS = 128
GDN_SCAN_SEQS = 2


def _pick_tile(n, candidates):
    for c in candidates:
        if n % c == 0:
            return c
    raise ValueError(f"no tile in {candidates} divides {n}")


def _params(sem, vmem=VMEM_LIMIT):
    return pltpu.CompilerParams(dimension_semantics=sem, vmem_limit_bytes=vmem)


def _resident(shape):
    nd = len(shape)
    return pl.BlockSpec(shape, lambda *_: (0,) * nd, pipeline_mode=pl.Buffered(1))


def _silu(x):
    return x * (1.0 / (1.0 + jnp.exp(-x)))


def _sigmoid(x):
    return 1.0 / (1.0 + jnp.exp(-x))


def _softplus(x):
    return jnp.maximum(x, 0.0) + jnp.log1p(jnp.exp(-jnp.abs(x)))


def _rms(x):
    return x * lax.rsqrt(jnp.mean(x * x, axis=-1, keepdims=True) + NORM_EPS)


def _dot(a, b):
    return jnp.dot(a, b, preferred_element_type=F32)


def _dot_nt(a, b):
    return lax.dot_general(a, b, (((1,), (1,)), ((), ())), preferred_element_type=F32)


def _dot_tn(a, b):
    return lax.dot_general(a, b, (((0,), (0,)), ((), ())), preferred_element_type=F32)


def _dot_hi(a, b):
    return jnp.dot(a, b, preferred_element_type=F32, precision=HIGHEST)


PACK_PLANES = D_MODEL // (2 * LANES)


def _pack_rows(x):
    as_bits = lambda v: pltpu.bitcast(v.astype(BF16).astype(F32), jnp.uint32)
    planes = []
    for j in range(PACK_PLANES):
        lo = as_bits(x[:, j * LANES:(j + 1) * LANES])
        hi = as_bits(x[:, (j + PACK_PLANES) * LANES:(j + PACK_PLANES + 1) * LANES])
        planes.append(jnp.bitwise_or(hi, lax.shift_right_logical(lo, jnp.uint32(16))))
    return planes


def _unpack_planes(planes):
    lo = [pltpu.bitcast(lax.shift_left(p, jnp.uint32(16)), F32) for p in planes]
    hi = [pltpu.bitcast(jnp.bitwise_and(p, jnp.uint32(0xFFFF0000)), F32) for p in planes]
    return lo, hi


def _dot_split(a, b):
    a_hi, b_hi = a.astype(BF16), b.astype(BF16)
    a_lo = (a - a_hi.astype(F32)).astype(BF16)
    b_lo = (b - b_hi.astype(F32)).astype(BF16)
    return _dot(a_hi, b_hi) + (_dot(a_hi, b_lo) + _dot(a_lo, b_hi))


def _row_tile(rows, row0, candidates):
    return _pick_tile(math.gcd(rows, row0) if row0 else rows, candidates)


def _inproj_kernel(x_ref, nw_ref, w_ref, oret_ref, ogdn_ref, ogate_ref, oab_ref):
    h = (_rms(x_ref[...]) * nw_ref[...]).astype(BF16)
    oret_ref[...] = _dot(h, w_ref[:, :_RET_W]).astype(BF16)
    ogdn_ref[...] = _dot(h, w_ref[:, _RET_W:_AB_OFF]).astype(BF16)
    tail = _dot(h, w_ref[:, _AB_OFF:])
    oab_ref[...] = tail[:, :LANES]
    ogate_ref[...] = tail[:, _GATE_OFF - _AB_OFF:_GATE_OFF - _AB_OFF + 2 * D_MODEL].astype(BF16)


def _inproj(x, norm_w, wb):
    rows = x.shape[0]
    tm = _pick_tile(rows, (256, 128, 64, 32, 16, 8))
    row = lambda n: pl.BlockSpec((tm, n), lambda i: (i, 0))
    return pl.pallas_call(
        _inproj_kernel,
        grid=(rows // tm,),
        in_specs=[row(D_MODEL), _resident((1, D_MODEL)), _resident(wb.shape)],
        out_specs=[row(_RET_W), row(_GDN_W), row(2 * D_MODEL), row(LANES)],
        out_shape=[jax.ShapeDtypeStruct((rows, _RET_W), BF16), jax.ShapeDtypeStruct((rows, _GDN_W), BF16),
                   jax.ShapeDtypeStruct((rows, 2 * D_MODEL), BF16), jax.ShapeDtypeStruct((rows, LANES), F32)],
        compiler_params=_params(("parallel",)),
        name="inproj",
    )(x, norm_w.reshape(1, D_MODEL), wb)


def _ret_log_gamma():
    return np.log1p(-np.exp2(-5.0 - np.arange(RET_HEADS, dtype=np.float64)))


def _rope_tables(pos):
    half = RET_DK // 2
    inv = 1.0 / (ROPE_BASE ** (jnp.arange(half, dtype=F32) / half))
    ang = pos.astype(F32)[:, None] * inv[None, :]
    cos, sin = jnp.cos(ang), jnp.sin(ang)
    return jnp.concatenate([cos, cos], axis=-1), jnp.concatenate([-sin, sin], axis=-1)


def _rotary(x, cos, sin):
    return x * cos + pltpu.roll(x, RET_DK // 2, 1) * sin


def _ret_prompt_kernel(q_ref, k_ref, v_ref, g_ref, cos_ref, sin_ref, dmask_ref, qdec_ref, kdec_ref,
                       o_ref, s_ref, *, gammas):
    @pl.when(pl.program_id(1) == 0)
    def _():
        s_ref[...] = jnp.zeros_like(s_ref)

    c = RET_CHUNK
    chains = [(sq, h) for sq in range(q_ref.shape[0]) for h in range(RET_HEADS)]
    nc = range(len(chains))
    qk = [slice(h * RET_DK, (h + 1) * RET_DK) for _, h in chains]
    vv = [slice(h * RET_DV, (h + 1) * RET_DV) for _, h in chains]
    hd = [h for _, h in chains]
    sq = [s_ for s_, _ in chains]
    s = [s_ref[sq[i], hd[i]] for i in nc]
    for ck in range(q_ref.shape[1] // c):
        r = slice(ck * c, (ck + 1) * c)
        cos, sin = cos_ref[r, :], sin_ref[r, :]
        q = [_rotary(q_ref[sq[i], r, qk[i]].astype(F32), cos, sin) for i in nc]
        k = [_rotary(k_ref[sq[i], r, qk[i]].astype(F32), cos, sin) * (RET_DK ** -0.5) for i in nc]
        v = [v_ref[sq[i], r, vv[i]] for i in nc]
        qb = [x.astype(BF16) for x in q]
        inner = [_dot_nt(qb[i], k[i].astype(BF16)) * dmask_ref[hd[i]] for i in nc]
        cross = [_dot((q[i] * qdec_ref[hd[i]]).astype(BF16), s[i].astype(BF16)) for i in nc]
        upd = [_dot_tn((k[i] * kdec_ref[hd[i]]).astype(BF16), v[i]) for i in nc]
        o = [_dot(inner[i].astype(BF16), v[i]) + cross[i] for i in nc]
        s = [s[i] * gammas[hd[i]] + upd[i] for i in nc]
        for i in nc:
            o_ref[sq[i], r, vv[i]] = (_rms(o[i]) * _silu(g_ref[sq[i], r, vv[i]].astype(F32))).astype(BF16)
    for i in nc:
        s_ref[sq[i], hd[i]] = s[i]


def _ret_prompt(p_ret, batch, seq):
    c = RET_CHUNK
    step = RET_STEP_ROWS if seq % RET_STEP_ROWS == 0 else c
    n = seq // step
    lg = _ret_log_gamma()
    idx = np.arange(c, dtype=np.float64)
    diff = idx[:, None] - idx[None, :]
    dmask = np.where(diff >= 0, np.exp(np.maximum(diff, 0.0)[None] * lg[:, None, None]), 0.0)
    qdec = np.broadcast_to(np.exp((idx + 1.0)[None, :] * lg[:, None])[:, :, None], (RET_HEADS, c, RET_DK))
    kdec = np.broadcast_to(np.exp((c - 1.0 - idx)[None, :] * lg[:, None])[:, :, None], (RET_HEADS, c, RET_DK))
    gammas = tuple(float(g) for g in np.exp(c * lg))
    cos, sin = _rope_tables(jnp.arange(seq, dtype=jnp.int32))
    tab = lambda: _resident((RET_HEADS, c, RET_DK))
    nseq = RET_STEP_SEQS if batch % RET_STEP_SEQS == 0 else 1
    p3 = p_ret.reshape(batch, seq, _RET_W)
    o, s_new = pl.pallas_call(
        functools.partial(_ret_prompt_kernel, gammas=gammas),
        grid=(batch // nseq, n),
        in_specs=[pl.BlockSpec((nseq, step, RET_QK_W), lambda b, j: (b, j, 0)),
                  pl.BlockSpec((nseq, step, RET_QK_W), lambda b, j: (b, j, 1)),
                  pl.BlockSpec((nseq, step, RET_V_W), lambda b, j: (b, j, 1)),
                  pl.BlockSpec((nseq, step, RET_V_W), lambda b, j: (b, j, 2)),
                  pl.BlockSpec((step, RET_DK), lambda b, j: (j, 0)),
                  pl.BlockSpec((step, RET_DK), lambda b, j: (j, 0)),
                  tab(), tab(), tab()],
        out_specs=[pl.BlockSpec((nseq, step, RET_V_W), lambda b, j: (b, j, 0)),
                   pl.BlockSpec((nseq, RET_HEADS, RET_DK, RET_DV), lambda b, j: (b, 0, 0, 0))],
        out_shape=[jax.ShapeDtypeStruct((batch, seq, RET_V_W), BF16),
                   jax.ShapeDtypeStruct((batch, RET_HEADS, RET_DK, RET_DV), F32)],
        compiler_params=_params(("parallel", "arbitrary")),
        name="ret_prompt",
    )(p3, p3, p3, p3, cos, sin, jnp.asarray(dmask, F32), jnp.asarray(qdec, F32), jnp.asarray(kdec, F32))
    return o.reshape(batch * seq, RET_V_W), s_new


def _columns(x):
    n = x.shape[0]
    if n < LANES:
        x = jnp.concatenate([x, jnp.zeros((LANES - n, x.shape[1]), x.dtype)], axis=0)
    return x.T


def _ret_sample_kernel(p_ref, cos_ref, sin_ref, s_ref, o_ref, so_ref, *, gammas):
    cos, sin = cos_ref[...], sin_ref[...]
    nb = p_ref.shape[0]
    for h in range(RET_HEADS):
        qk = slice(h * RET_DK, (h + 1) * RET_DK)
        q = _rotary(p_ref[:, qk], cos, sin)
        k = _rotary(p_ref[:, RET_QK_W + h * RET_DK:RET_QK_W + (h + 1) * RET_DK], cos, sin) * (RET_DK ** -0.5)
        v = p_ref[:, 2 * RET_QK_W + h * RET_DV:2 * RET_QK_W + (h + 1) * RET_DV]
        g = p_ref[:, 2 * RET_QK_W + RET_V_W + h * RET_DV:2 * RET_QK_W + RET_V_W + (h + 1) * RET_DV]
        qk_dot = jnp.sum(q * k, axis=-1, keepdims=True)
        qt, kt = _columns(q), _columns(k)
        rows = []
        for j in range(nb):
            s = s_ref[j, h]
            qs = jnp.sum(qt[:, j:j + 1] * s, axis=0, keepdims=True)
            rows.append(qk_dot[j:j + 1] * v[j:j + 1] + gammas[h] * qs)
            so_ref[j, h] = s * gammas[h] + kt[:, j:j + 1] * v[j:j + 1]
        o = jnp.concatenate(rows, axis=0)
        o_ref[:, h * RET_DV:(h + 1) * RET_DV] = (_rms(o) * _silu(g)).astype(BF16)


def _ret_sample(p_ret_s, state):
    ts = p_ret_s.shape[0]
    sb = SAMPLE_TILE
    gammas = tuple(float(g) for g in np.exp(_ret_log_gamma()))
    cos, sin = _rope_tables(jnp.full((1,), PAST_LEN, jnp.int32))
    st = pl.BlockSpec((sb, RET_HEADS, RET_DK, RET_DV), lambda i: (i, 0, 0, 0))
    return pl.pallas_call(
        functools.partial(_ret_sample_kernel, gammas=gammas),
        grid=(ts // sb,),
        in_specs=[pl.BlockSpec((sb, _RET_W), lambda i: (i, 0)), _resident((1, RET_DK)), _resident((1, RET_DK)), st],
        out_specs=[pl.BlockSpec((sb, RET_V_W), lambda i: (i, 0)), st],
        out_shape=[jax.ShapeDtypeStruct((ts, RET_V_W), BF16), jax.ShapeDtypeStruct(state.shape, F32)],
        compiler_params=_params(("parallel",)),
        name="ret_sample",
    )(p_ret_s, cos, sin, state)


def _l2norm(x):
    return x * lax.rsqrt(jnp.sum(x * x, axis=-1, keepdims=True) + NORM_EPS)


def _bdot(a, b):
    return _dot(a.astype(BF16), b.astype(BF16))


def _chunk_masks(c):
    ri = lax.broadcasted_iota(jnp.int32, (c, c), 0)
    ci = lax.broadcasted_iota(jnp.int32, (c, c), 1)
    eye = (ri == ci).astype(F32)
    diag16 = (ri // 16 == ci // 16).astype(F32)
    low32 = jnp.logical_and(ri // 32 == ci // 32, ri // 16 > ci // 16).astype(F32)
    low64 = (ri // 32 > ci // 32).astype(F32)
    return ri >= ci, ri > ci, eye, diag16, low32, low64


def _unit_lower_inverse(a, eye, diag16, low32, low64):
    many = lambda f, *ls: [f(*args) for args in zip(*ls)]
    c = eye.shape[0]
    pair = lambda p, q, rhs: _bdot(jnp.concatenate([p, q], axis=0), rhs)
    n = [-(x * diag16) for x in a]
    n2 = many(_bdot, n, n)
    n34 = many(pair, n, n2, n2)
    x = [eye + p + q + r[:c] for p, q, r in zip(n, n2, n34)]
    n4 = [r[c:] for r in n34]
    xn = many(pair, x, n4, n4)
    x = [u + v[:c] for u, v in zip(x, xn)]
    x = many(lambda u, v: u + v, x, many(_bdot, x, [v[c:] for v in xn]))
    for mask in (low32, low64):
        r = many(_bdot, [y * mask for y in a], x)
        x = many(lambda u, v: u - v, x, many(_bdot, x, r))
    return x


def _gdn_gates(ab, alog, dtb):
    g = -jnp.exp(alog) * _softplus(ab + dtb)
    return g, _sigmoid(ab)


def _gdn_prep_kernel(x_ref, prev_ref, ab_ref, cw_ref, alog_ref, dtb_ref,
                     u_ref, w_ref, qg_ref, kg_ref, at_ref, eg_ref, cv_ref, act_ref):
    c = GDN_CHUNK
    rows = x_ref.shape[0]
    x = x_ref[...]
    halo = prev_ref.shape[0]
    prev = jnp.where(pl.program_id(1) == 0, jnp.zeros_like(prev_ref), prev_ref[...])
    xcat = jnp.concatenate([prev, x], axis=0)
    ti = lax.broadcasted_iota(jnp.int32, (rows, rows + halo), 0)
    ui = lax.broadcasted_iota(jnp.int32, (rows, rows + halo), 1)
    xf = x.astype(F32)
    acc = xf * cw_ref[CONV_W - 1:CONV_W, :]
    for i in range(CONV_W - 1):
        shift = (ui == ti + (halo - (CONV_W - 1) + i)).astype(BF16)
        acc = acc + _dot(shift, xcat) * cw_ref[i:i + 1, :]
    cv_ref[0] = xf[rows - (CONV_W - 1):, :]
    act_ref[...] = _silu(acc)

    g_all, beta_all = _gdn_gates(ab_ref[...], alog_ref[...], dtb_ref[...])
    lower, strict, eye, diag16, low32, low64 = _chunk_masks(c)
    gc_all, gr_all = [], []
    for ck in range(rows // c):
        gc_ck = _dot_hi(lower.astype(F32), g_all[ck * c:(ck + 1) * c])
        gc_all.append(gc_ck)
        gr_all.append(_columns(gc_ck))
        eg_ref[ck] = jnp.exp(gc_ck[c - 1:c, :])
    chains = [(ck, h) for ck in range(rows // c) for h in range(GDN_HEADS)]
    rs = [slice(ck * c, (ck + 1) * c) for ck, _ in chains]
    sl = [slice(h * GDN_DK, (h + 1) * GDN_DK) for _, h in chains]
    nc = range(len(chains))
    q = [_l2norm(act_ref[rs[i], sl[i]]) * (GDN_DK ** -0.5) for i in nc]
    k = [_l2norm(act_ref[rs[i], GDN_QK_W + sl[i].start:GDN_QK_W + sl[i].stop]) for i in nc]
    v = [act_ref[rs[i], 2 * GDN_QK_W + sl[i].start:2 * GDN_QK_W + sl[i].stop] for i in nc]
    beta = [beta_all[rs[i], GDN_HEADS + h:GDN_HEADS + h + 1] for i, (_, h) in enumerate(chains)]
    gc = [gc_all[ck][:, h:h + 1] for ck, h in chains]
    gr = [gr_all[ck][h:h + 1, :c] for ck, h in chains]
    decay = [jnp.exp(jnp.where(lower, gc[i] - gr[i], -jnp.inf)) for i in nc]
    exp_g = [jnp.exp(x) for x in gc]
    kb = [k[i] * beta[i] for i in nc]
    kbf = [x.astype(BF16) for x in k]
    kq = [_dot_nt(jnp.concatenate([kb[i].astype(BF16), q[i].astype(BF16)], axis=0), kbf[i]) for i in nc]
    a = [kq[i][:c] * jnp.where(strict, decay[i], 0.0) for i in nc]
    attn = [kq[i][c:] * decay[i] for i in nc]
    t = _unit_lower_inverse(a, eye, diag16, low32, low64)
    uw = [_bdot(t[i], jnp.concatenate([v[i] * beta[i], kb[i] * exp_g[i]], axis=1)) for i in nc]
    for i in nc:
        u_ref[rs[i], sl[i]] = uw[i][:, :GDN_DV]
        w_ref[rs[i], sl[i]] = uw[i][:, GDN_DV:].astype(BF16)
        at_ref[rs[i], sl[i]] = jnp.concatenate([attn[i], jnp.zeros((c, GDN_DK - c), F32)], axis=1).astype(BF16)
        qg_ref[rs[i], sl[i]] = (q[i] * exp_g[i]).astype(BF16)
        kg_ref[rs[i], sl[i]] = (k[i] * jnp.exp(gc[i][c - 1:c, :] - gc[i])).astype(BF16)


def _gdn_scan_kernel(u_ref, w_ref, qg_ref, kg_ref, at_ref, eg_ref, z_ref, nw_ref, o_ref, s_ref):
    c = GDN_CHUNK

    @pl.when(pl.program_id(1) == 0)
    def _():
        s_ref[...] = jnp.zeros_like(s_ref)

    nw = nw_ref[...]
    chains = [(q, h) for q in range(u_ref.shape[0]) for h in range(GDN_HEADS)]
    sl = [slice(h * GDN_DK, (h + 1) * GDN_DK) for _, h in chains]
    nc = range(len(chains))
    s = [s_ref[q, h] for q, h in chains]
    for ck in range(u_ref.shape[1] // c):
        r = slice(ck * c, (ck + 1) * c)
        sb = [x.astype(BF16) for x in s]
        ws = [_dot(w_ref[q, r, sl[i]], sb[i]) for i, (q, _) in enumerate(chains)]
        qs = [_dot(qg_ref[q, r, sl[i]], sb[i]) for i, (q, _) in enumerate(chains)]
        vnb = [(u_ref[q, r, sl[i]] - ws[i]).astype(BF16) for i, (q, _) in enumerate(chains)]
        o = [qs[i] + _dot(at_ref[q, r, h * GDN_DK:h * GDN_DK + c], vnb[i]) for i, (q, h) in enumerate(chains)]
        s = [s[i] * eg_ref[q, ck][:, h:h + 1] + _dot_tn(kg_ref[q, r, sl[i]], vnb[i])
             for i, (q, h) in enumerate(chains)]
        for i, (q, _) in enumerate(chains):
            o_ref[q, r, sl[i]] = (_rms(o[i]) * nw * _silu(z_ref[q, r, sl[i]].astype(F32))).astype(BF16)
    for i, (q, h) in enumerate(chains):
        s_ref[q, h] = s[i]


def _gdn_prompt(p_gdn, p_ab, batch, seq, conv_w, a_log, dt_bias, gdn_norm_w):
    c = GDN_CHUNK
    rows = GDN_PREP_ROWS
    t = batch * seq
    nt = seq // rows
    alog = jnp.pad(a_log.astype(F32), (0, LANES - GDN_HEADS)).reshape(1, LANES)
    dtb = jnp.pad(dt_bias.astype(F32), (0, LANES - GDN_HEADS)).reshape(1, LANES)
    wide = lambda: pl.BlockSpec((rows, GDN_V_W), lambda b, j: (b * nt + j, 0))
    u, w, qg, kg, at, eg, conv_new = pl.pallas_call(
        _gdn_prep_kernel,
        grid=(batch, nt),
        in_specs=[pl.BlockSpec((rows, CONV_CH), lambda b, j: (b * nt + j, 0)),
                  pl.BlockSpec((16, CONV_CH), lambda b, j: (jnp.maximum((b * nt + j) * (rows // 16) - 1, 0), 0)),
                  pl.BlockSpec((rows, LANES), lambda b, j: (b * nt + j, 0)),
                  _resident((CONV_W, CONV_CH)), _resident((1, LANES)), _resident((1, LANES))],
        out_specs=[wide(), wide(), wide(), wide(), wide(),
                   pl.BlockSpec((rows // c, 1, LANES), lambda b, j: (b * nt + j, 0, 0)),
                   pl.BlockSpec((1, CONV_W - 1, CONV_CH), lambda b, j: (b, 0, 0))],
        out_shape=[jax.ShapeDtypeStruct((t, GDN_V_W), F32)] + [jax.ShapeDtypeStruct((t, GDN_V_W), BF16)] * 4
        + [jax.ShapeDtypeStruct((t // c, 1, LANES), F32),
           jax.ShapeDtypeStruct((batch, CONV_W - 1, CONV_CH), F32)],
        scratch_shapes=[pltpu.VMEM((rows, CONV_CH), F32)],
        compiler_params=_params(("parallel", "arbitrary")),
        name="gdn_prep",
    )(p_gdn, p_gdn, p_ab, conv_w.astype(F32), alog, dtb)
    srows = GDN_SCAN_ROWS
    n = seq // srows
    nq = CONV_CH // GDN_V_W
    nseq = GDN_SCAN_SEQS if batch % GDN_SCAN_SEQS == 0 else 1
    per_seq = lambda a: a.reshape((batch, seq // (t // a.shape[0])) + a.shape[1:])
    blk = lambda: pl.BlockSpec((nseq, srows, GDN_V_W), lambda b, j: (b, j, 0))
    o, s_new = pl.pallas_call(
        _gdn_scan_kernel,
        grid=(batch // nseq, n),
        in_specs=[blk(), blk(), blk(), blk(), blk(),
                  pl.BlockSpec((nseq, srows // c, 1, LANES), lambda b, j: (b, j, 0, 0)),
                  pl.BlockSpec((nseq, srows, GDN_V_W), lambda b, j: (b, j, nq)),
                  _resident((1, GDN_DV))],
        out_specs=[blk(), pl.BlockSpec((nseq, GDN_HEADS, GDN_DK, GDN_DV), lambda b, j: (b, 0, 0, 0))],
        out_shape=[jax.ShapeDtypeStruct((batch, seq, GDN_V_W), BF16),
                   jax.ShapeDtypeStruct((batch, GDN_HEADS, GDN_DK, GDN_DV), F32)],
        compiler_params=_params(("parallel", "arbitrary")),
        name="gdn_scan",
    )(per_seq(u), per_seq(w), per_seq(qg), per_seq(kg), per_seq(at), per_seq(eg), per_seq(p_gdn),
      gdn_norm_w.astype(F32).reshape(1, GDN_DV))
    return o.reshape(t, GDN_V_W), s_new, conv_new


def _gdn_sample_kernel(x_ref, ab_ref, sc_ref, cw_ref, alog_ref, dtb_ref, nw_ref, s_ref, o_ref, so_ref, sco_ref):
    nb = x_ref.shape[0]
    x = x_ref[:, :CONV_CH]
    acc = x * cw_ref[CONV_W - 1:CONV_W, :]
    for i in range(CONV_W - 1):
        acc = acc + sc_ref[i] * cw_ref[i:i + 1, :]
    for i in range(CONV_W - 2):
        sco_ref[i] = sc_ref[i + 1]
    sco_ref[CONV_W - 2] = x
    u = _silu(acc)
    g_all, beta_all = _gdn_gates(ab_ref[...], alog_ref[...], dtb_ref[...])
    eg_all = jnp.exp(g_all)
    nw = nw_ref[...]
    for h in range(GDN_HEADS):
        sl = slice(h * GDN_DK, (h + 1) * GDN_DK)
        q = _l2norm(u[:, sl]) * (GDN_DK ** -0.5)
        k = _l2norm(u[:, GDN_QK_W + h * GDN_DK:GDN_QK_W + (h + 1) * GDN_DK])
        v = u[:, 2 * GDN_QK_W + h * GDN_DV:2 * GDN_QK_W + (h + 1) * GDN_DV]
        beta = beta_all[:, GDN_HEADS + h:GDN_HEADS + h + 1]
        eg = eg_all[:, h:h + 1]
        qk_dot = jnp.sum(q * k, axis=-1, keepdims=True)
        qt, kt = _columns(q), _columns(k)
        rows = []
        for j in range(nb):
            s = s_ref[j, h]
            kcol = kt[:, j:j + 1]
            ks = jnp.sum(kcol * s, axis=0, keepdims=True)
            qs = jnp.sum(qt[:, j:j + 1] * s, axis=0, keepdims=True)
            ej = eg[j:j + 1]
            v_new = beta[j:j + 1] * (v[j:j + 1] - ej * ks)
            rows.append(ej * qs + qk_dot[j:j + 1] * v_new)
            so_ref[j, h] = s * ej + kcol * v_new
        o = jnp.concatenate(rows, axis=0)
        z = x_ref[:, CONV_CH + h * GDN_DV:CONV_CH + (h + 1) * GDN_DV]
        o_ref[:, sl] = (_rms(o) * nw * _silu(z)).astype(BF16)


def _gdn_sample(p_gdn_s, p_ab_s, state, conv_state, conv_w, a_log, dt_bias, gdn_norm_w):
    ts = p_gdn_s.shape[0]
    sb = SAMPLE_TILE
    alog = jnp.pad(a_log.astype(F32), (0, LANES - GDN_HEADS)).reshape(1, LANES)
    dtb = jnp.pad(dt_bias.astype(F32), (0, LANES - GDN_HEADS)).reshape(1, LANES)
    sc = jnp.swapaxes(conv_state.astype(F32), 0, 1)
    st = pl.BlockSpec((sb, GDN_HEADS, GDN_DK, GDN_DV), lambda i: (i, 0, 0, 0))
    scs = pl.BlockSpec((CONV_W - 1, sb, CONV_CH), lambda i: (0, i, 0))
    o, s_new, sc_new = pl.pallas_call(
        _gdn_sample_kernel,
        grid=(ts // sb,),
        in_specs=[pl.BlockSpec((sb, _GDN_W), lambda i: (i, 0)), pl.BlockSpec((sb, LANES), lambda i: (i, 0)), scs,
                  _resident((CONV_W, CONV_CH)), _resident((1, LANES)), _resident((1, LANES)),
                  _resident((1, GDN_DV)), st],
        out_specs=[pl.BlockSpec((sb, GDN_V_W), lambda i: (i, 0)), st, scs],
        out_shape=[jax.ShapeDtypeStruct((ts, GDN_V_W), BF16), jax.ShapeDtypeStruct(state.shape, F32),
                   jax.ShapeDtypeStruct(sc.shape, F32)],
        compiler_params=_params(("parallel",)),
        name="gdn_sample",
    )(p_gdn_s, p_ab_s, sc, conv_w.astype(F32), alog, dtb, gdn_norm_w.astype(F32).reshape(1, GDN_DV), state)
    return o, s_new, jnp.swapaxes(sc_new, 0, 1)


_NO_EXPERT = -1e30


def _merge_kernel(oa0_ref, ob0_ref, gate0_ref, x0_ref, oa1_ref, ob1_ref, gate1_ref, x1in_ref,
                  wa_ref, wb_ref, wo_ref, nw_ref, wr_ref, br_ref, x1_ref, h2_ref, ti_ref, tw_ref, *, n_first):
    first = pl.program_id(0) < n_first
    pick = lambda a, b: jnp.where(first, a[...], b[...])
    gate = pick(gate0_ref, gate1_ref)
    ya = _dot(pick(oa0_ref, oa1_ref), wa_ref[...])
    yb = _dot(pick(ob0_ref, ob1_ref), wb_ref[...])
    ga = gate[:, :D_MODEL].astype(F32)
    gb = gate[:, D_MODEL:].astype(F32)
    m = _sigmoid(ga) * ya + _sigmoid(gb) * yb
    x1 = pick(x0_ref, x1in_ref) + _dot(m.astype(BF16), wo_ref[...])
    x1_ref[...] = x1
    h2 = _rms(x1) * nw_ref[...]
    for j, plane in enumerate(_pack_rows(h2)):
        h2_ref[:, j, :] = plane
    lg = _dot_split(h2, wr_ref[...]) + br_ref[...]
    lane = lax.broadcasted_iota(jnp.int32, lg.shape, 1).astype(F32)
    vals, idxs = [], []
    for _ in range(TOP_K):
        top = jnp.max(lg, axis=-1, keepdims=True)
        idx = jnp.min(jnp.where(lg == top, lane, float(LANES)), axis=-1, keepdims=True)
        vals.append(top)
        idxs.append(idx)
        lg = jnp.where(lane == idx, _NO_EXPERT, lg)
    es = [jnp.exp(v - vals[0]) for v in vals]
    inv_total = 1.0 / functools.reduce(lambda a, b: a + b, es)
    ti = jnp.zeros_like(lg)
    tw = jnp.zeros_like(lg)
    for k in range(TOP_K):
        ti = jnp.where(lane == float(k), idxs[k], ti)
        tw = jnp.where(lane == float(k), es[k] * inv_total, tw)
    ti_ref[...] = ti.astype(jnp.int32)
    tw_ref[...] = tw


def _merge(group0, group1, weights):
    r0 = group0[3].shape[0]
    tm = _pick_tile(r0, (MERGE_ROWS, 128, 64, 32, 16, 8))
    pad = -group1[3].shape[0] % tm
    group1 = tuple(jnp.pad(a, ((0, pad), (0, 0))) for a in group1)
    r1 = group1[3].shape[0]
    n0, total = r0 // tm, r0 + r1
    widths = (RET_V_W, GDN_V_W, 2 * D_MODEL, D_MODEL)
    specs0 = [pl.BlockSpec((tm, n), lambda i: (jnp.minimum(i, n0 - 1), 0)) for n in widths]
    specs1 = [pl.BlockSpec((tm, n), lambda i: (jnp.maximum(i - n0, 0), 0)) for n in widths]
    out = lambda n: pl.BlockSpec((tm, n), lambda i: (i, 0))
    sub = PACK_PLANES
    sq = (D_MODEL, D_MODEL)
    return pl.pallas_call(
        functools.partial(_merge_kernel, n_first=n0),
        grid=(total // tm,),
        in_specs=specs0 + specs1 + [_resident(sq), _resident(sq), _resident(sq), _resident((1, D_MODEL)),
                                    _resident((D_MODEL, LANES)), _resident((1, LANES))],
        out_specs=[out(D_MODEL), pl.BlockSpec((tm, sub, LANES), lambda i: (i, 0, 0)), out(LANES), out(LANES)],
        out_shape=[jax.ShapeDtypeStruct((total, D_MODEL), F32), jax.ShapeDtypeStruct((total, sub, LANES), jnp.uint32),
                   jax.ShapeDtypeStruct((total, LANES), jnp.int32), jax.ShapeDtypeStruct((total, LANES), F32)],
        compiler_params=_params(("parallel",)),
        name="merge",
    )(*group0, *group1, *weights)


def _merge_weights(w_a, w_b, w_o, ffn_norm_w, w_router, b_router):
    wr = jnp.pad(w_router.astype(F32), ((0, 0), (0, LANES - N_EXPERTS)))
    br = jnp.pad(b_router.astype(F32), (0, LANES - N_EXPERTS), constant_values=_NO_EXPERT).reshape(1, LANES)
    return (w_a.astype(BF16), w_b.astype(BF16), w_o.astype(BF16), ffn_norm_w.astype(F32).reshape(1, D_MODEL), wr, br)


def _route(top_i, gate, n_tokens):
    rows = MOE_ROWS
    n = n_tokens * TOP_K
    flat_e = top_i.reshape(n).astype(jnp.int32)
    bits = max(1, (n - 1).bit_length())
    assert bits + (N_EXPERTS - 1).bit_length() <= 31
    order = lax.sort((flat_e << bits) | jnp.arange(n, dtype=jnp.int32), is_stable=False) & ((1 << bits) - 1)
    counts = jnp.sum((flat_e[:, None] == jnp.arange(N_EXPERTS, dtype=jnp.int32)[None, :]).astype(jnp.int32), axis=0)
    start = jnp.cumsum(counts) - counts
    pcounts = (counts + rows - 1) // rows * rows
    pend = jnp.cumsum(pcounts)
    pstart = pend - pcounts
    nb = -(-n // rows) + N_EXPERTS
    blk = jnp.arange(nb, dtype=jnp.int32)
    block_e = jnp.minimum(jnp.sum((pend[None, :] <= (blk * rows)[:, None]).astype(jnp.int32), axis=1),
                          N_EXPERTS - 1).astype(jnp.int32)
    nb_used = (pend[-1] // rows).astype(jnp.int32).reshape(1)
    within = (blk * rows - pstart[block_e])[:, None] + jnp.arange(rows, dtype=jnp.int32)[None, :]
    valid = jnp.logical_and(within < counts[block_e][:, None], (blk < nb_used[0])[:, None])
    flat = order[jnp.clip(start[block_e][:, None] + within, 0, n - 1)]
    spare = n + (blk % MOE_RING)[:, None] * rows + jnp.arange(rows, dtype=jnp.int32)[None, :]
    src = jnp.where(valid, flat // TOP_K, 0)
    dst = jnp.where(valid, flat, spare)
    ahead = lambda k: jnp.concatenate([src[k:]] + [src[-1:]] * k, axis=0)
    slab = jnp.concatenate([src, dst, ahead(1), ahead(2)], axis=1)
    row_w = jnp.where(valid, gate[:, :TOP_K].reshape(n)[flat], 0.0)
    row_w = jnp.broadcast_to(row_w[:, :, None], (nb, rows, LANES))
    return block_e, nb_used, slab, row_w


def _expert_kernel(be_ref, nbu_ref, slab_ref, h_ref, roww_ref, wgu_ref, bgu_ref, wd_ref, bd_ref, y_ref,
                   idx0_ref, idx1_ref, idx2_ref, xbuf_ref, ybuf_ref, wgu_bf_ref, wd_bf_ref, isem, gsem, ssem):
    rows = MOE_ROWS
    ring = MOE_RING
    idx_refs = (idx0_ref, idx1_ref, idx2_ref)
    assert len(idx_refs) == ring
    sub = PACK_PLANES
    i = pl.program_id(0)
    nbu = nbu_ref[0]
    slot = i % ring
    active = i < nbu
    n_real = y_ref.shape[0] - ring * rows

    def slab_copy(blk, sl):
        return pltpu.make_async_copy(slab_ref.at[blk], idx_refs[sl], isem.at[sl])

    def gather_row(tok, sl, r):
        return pltpu.make_async_copy(h_ref.at[tok], xbuf_ref.at[sl, :, r, :], gsem.at[sl])

    def scatter_row(sl, r, f):
        return pltpu.make_async_copy(ybuf_ref.at[sl, :, r, :], y_ref.at[f], ssem.at[sl])

    def gather_wait(sl):
        pltpu.make_async_copy(xbuf_ref.at[sl], xbuf_ref.at[sl], gsem.at[sl]).wait()

    def scatter_wait(sl):
        pltpu.make_async_copy(ybuf_ref.at[sl], ybuf_ref.at[sl], ssem.at[sl]).wait()

    @pl.when(i == 0)
    def _():
        slab_copy(0, 0).start()
        ybuf_ref[...] = jnp.zeros_like(ybuf_ref)
        for sl in range(ring):
            def fill(r, carry, sl=sl):
                scatter_row(sl, r, n_real + sl * rows + r).start()
                return carry
            lax.fori_loop(0, rows, fill, 0)

    def block(sl):
        nxt, nxt2 = (sl + 1) % ring, (sl + 2) % ring
        idx_ref = idx_refs[sl]
        slab_copy(i, sl).wait()

        @pl.when(i + 1 < nbu)
        def _():
            slab_copy(i + 1, nxt).start()

        if sl == 0:
            @pl.when(i == 0)
            def _():
                def first(r, carry):
                    gather_row(idx_ref[r], 0, r).start()
                    gather_row(idx_ref[2 * rows + r], 1, r).start()
                    return carry
                lax.fori_loop(0, rows, first, 0)

        changed = jnp.logical_or(i == 0, be_ref[i] != be_ref[jnp.maximum(i - 1, 0)])

        @pl.when(changed)
        def _():
            wgu_bf_ref[...] = wgu_ref[0].astype(BF16)
            wd_bf_ref[...] = wd_ref[0].astype(BF16)

        gather_wait(sl)
        scatter_wait(sl)
        lo, hi = _unpack_planes([xbuf_ref[sl, j] for j in range(sub)])
        xb = jnp.concatenate(lo + hi, axis=1).astype(BF16)
        hb = _dot(xb, wgu_bf_ref[...]) + bgu_ref[0]
        for r in range(rows):
            gather_row(idx_ref[3 * rows + r], nxt2, r).start(priority=r % 2)
        glu = jnp.minimum(hb[:, :D_FF], SWIGLU_LIMIT)
        lin = jnp.clip(hb[:, D_FF:], -SWIGLU_LIMIT, SWIGLU_LIMIT)
        act = (glu * _sigmoid(SWIGLU_ALPHA * glu) * (lin + 1.0)).astype(BF16)
        yv = (_dot(act, wd_bf_ref[...]) + bd_ref[0]) * roww_ref[0, :, 0:1]
        for j, plane in enumerate(_pack_rows(yv)):
            ybuf_ref[sl, j] = plane
        for r in range(rows):
            scatter_row(sl, r, idx_ref[rows + r]).start(priority=(r + 1) % 2)

        @pl.when(i == nbu - 1)
        def _():
            for s in (sl, nxt, nxt2):
                scatter_wait(s)
            gather_wait(nxt)
            gather_wait(nxt2)

    for sl in range(ring):
        pl.when(jnp.logical_and(active, slot == sl))(functools.partial(block, sl))


def _experts(h2, t, block_e, nb_used, slab, row_w, w_gate_up, b_gate_up, w_down, b_down):
    rows = MOE_ROWS
    ring = MOE_RING
    assert (ring * rows) % TOP_K == 0
    nb = slab.shape[0]
    sub = PACK_PLANES
    grid_spec = pltpu.PrefetchScalarGridSpec(
        num_scalar_prefetch=2,
        grid=(nb,),
        in_specs=[pl.BlockSpec(memory_space=pl.ANY),
                  pl.BlockSpec(memory_space=pl.ANY),
                  pl.BlockSpec((1, rows, LANES), lambda i, be, nbu: (i, 0, 0)),
                  pl.BlockSpec((1, D_MODEL, 2 * D_FF), lambda i, be, nbu: (be[i], 0, 0)),
                  pl.BlockSpec((1, 1, 2 * D_FF), lambda i, be, nbu: (be[i], 0, 0)),
                  pl.BlockSpec((1, D_FF, D_MODEL), lambda i, be, nbu: (be[i], 0, 0)),
                  pl.BlockSpec((1, 1, D_MODEL), lambda i, be, nbu: (be[i], 0, 0))],
        out_specs=pl.BlockSpec(memory_space=pl.ANY),
        scratch_shapes=[pltpu.SMEM((slab.shape[1],), jnp.int32)] * ring + [
                        pltpu.VMEM((ring, sub, rows, LANES), jnp.uint32),
                        pltpu.VMEM((ring, sub, rows, LANES), jnp.uint32),
                        pltpu.VMEM((D_MODEL, 2 * D_FF), BF16),
                        pltpu.VMEM((D_FF, D_MODEL), BF16),
                        pltpu.SemaphoreType.DMA((ring,)),
                        pltpu.SemaphoreType.DMA((ring,)),
                        pltpu.SemaphoreType.DMA((ring,))])
    return pl.pallas_call(
        _expert_kernel,
        grid_spec=grid_spec,
        out_shape=jax.ShapeDtypeStruct((t * TOP_K + ring * rows, sub, LANES), jnp.uint32),
        compiler_params=_params(("arbitrary",)),
        name="experts",
    )(block_e, nb_used, slab, h2, row_w, w_gate_up, b_gate_up.reshape(N_EXPERTS, 1, 2 * D_FF), w_down,
      b_down.reshape(N_EXPERTS, 1, D_MODEL))


def _combine_kernel(y_ref, x1_ref, nw_ref, o_ref, sum_ref, *, final):
    lo, hi = None, None
    for k in range(TOP_K):
        (l,), (h,) = _unpack_planes([y_ref[:, k]])
        lo, hi = (l, h) if lo is None else (lo + l, hi + h)
    sum_ref[:, :PACK_PLANES, :] = lo
    sum_ref[:, PACK_PLANES:, :] = hi
    acc = x1_ref[...] + jnp.concatenate([sum_ref[:, j, :] for j in range(D_MODEL // LANES)], axis=1)
    o_ref[...] = _rms(acc) * nw_ref[...] if final else acc


def _combine(y, x1, row0, rows, norm_w, final):
    tm = _row_tile(rows, row0, (256, 128, 64, 32, 16, 8))
    off = row0 // tm
    sub = PACK_PLANES
    y4 = y.reshape(y.shape[0] // TOP_K, TOP_K, sub, LANES)
    return pl.pallas_call(
        functools.partial(_combine_kernel, final=final),
        grid=(rows // tm,),
        in_specs=[pl.BlockSpec((tm, TOP_K, sub, LANES), lambda i: (off + i, 0, 0, 0)),
                  pl.BlockSpec((tm, D_MODEL), lambda i: (off + i, 0)), _resident((1, D_MODEL))],
        out_specs=pl.BlockSpec((tm, D_MODEL), lambda i: (i, 0)),
        out_shape=jax.ShapeDtypeStruct((rows, D_MODEL), F32),
        scratch_shapes=[pltpu.VMEM((tm, D_MODEL // LANES, LANES), F32)],
        compiler_params=_params(("parallel",)),
        name="combine",
    )(y4, x1, norm_w.astype(F32).reshape(1, D_MODEL))


def kernel(x_prompt, x_sample, state_ret, state_gdn, state_conv, attn_norm_w, w_in, conv_w, a_log, dt_bias, gdn_norm_w, w_branch_a, w_branch_b, w_out, ffn_norm_w, w_router, b_router, w_gate_up, b_gate_up, w_down, b_down, final_norm_w):
    bp, lp, d = x_prompt.shape
    bs, ls, _ = x_sample.shape
    assert ls == 1 and d == D_MODEL and lp % RET_CHUNK == 0 and bs % SAMPLE_TILE == 0
    depth = w_in.shape[0]
    tp = bp * lp
    t = tp + bs
    xp, xs = x_prompt.reshape(tp, d).astype(F32), x_sample.reshape(bs, d).astype(F32)
    rp, gp, cp, rs, gs, cs = [], [], [], [], [], []
    for l in range(depth):
        wb = jnp.pad(w_in[l], ((0, 0), (0, -w_in.shape[2] % LANES))).astype(BF16)
        pp_ret, pp_gdn, pp_gate, pp_ab = _inproj(xp, attn_norm_w[l], wb)
        ps_ret, ps_gdn, ps_gate, ps_ab = _inproj(xs, attn_norm_w[l], wb)
        op_ret, s_ret_p = _ret_prompt(pp_ret, bp, lp)
        os_ret, s_ret_s = _ret_sample(ps_ret.astype(F32), state_ret[l].astype(F32))
        op_gdn, s_gdn_p, conv_p = _gdn_prompt(pp_gdn, pp_ab, bp, lp, conv_w[l], a_log[l], dt_bias[l], gdn_norm_w[l])
        os_gdn, s_gdn_s, conv_s = _gdn_sample(ps_gdn.astype(F32), ps_ab, state_gdn[l].astype(F32), state_conv[l],
                                              conv_w[l], a_log[l], dt_bias[l], gdn_norm_w[l])
        mw = _merge_weights(w_branch_a[l], w_branch_b[l], w_out[l], ffn_norm_w[l], w_router[l], b_router[l])
        x1, h2, top_i, gate = _merge((op_ret, op_gdn, pp_gate, xp), (os_ret, os_gdn, ps_gate, xs), mw)
        block_e, nb_used, slab, row_w = _route(top_i[:t, :TOP_K], gate[:t], t)
        y = _experts(h2, t, block_e, nb_used, slab, row_w, w_gate_up[l], b_gate_up[l], w_down[l], b_down[l])
        last = l == depth - 1
        norm_w = final_norm_w if last else jnp.ones((d,), F32)
        xp = _combine(y, x1, 0, tp, norm_w, last)
        xs = _combine(y, x1, tp, bs, norm_w, last)
        rp.append(s_ret_p); gp.append(s_gdn_p); cp.append(conv_p)
        rs.append(s_ret_s); gs.append(s_gdn_s); cs.append(conv_s)
    y_prompt = xp.reshape(bp, lp, d).astype(x_prompt.dtype)
    y_sample = xs.reshape(bs, ls, d).astype(x_sample.dtype)
    return (y_prompt, y_sample,
            jnp.stack(rp).astype(state_ret.dtype), jnp.stack(gp).astype(state_gdn.dtype),
            jnp.stack(cp).astype(state_conv.dtype),
            jnp.stack(rs).astype(state_ret.dtype), jnp.stack(gs).astype(state_gdn.dtype),
            jnp.stack(cs).astype(state_conv.dtype))
```

```python
import functools
import math

import numpy as np
import jax
import jax.numpy as jnp
from jax import lax
from jax.experimental import pallas as pl
from jax.experimental.pallas import tpu as pltpu

F32 = jnp.float32
BF16 = jnp.bfloat16
HIGHEST = lax.Precision.HIGHEST

D_MODEL = 1024
PAST_LEN = 16384
RET_HEADS, RET_DK, RET_DV = 4, 128, 256
RET_QK_W, RET_V_W = RET_HEADS * RET_DK, RET_HEADS * RET_DV
RET_CHUNK = 128
ROPE_BASE = 10000.0
GDN_HEADS, GDN_DK, GDN_DV = 8, 128, 128
GDN_QK_W, GDN_V_W = GDN_HEADS * GDN_DK, GDN_HEADS * GDN_DV
GDN_CHUNK = 64
CONV_W = 4
CONV_CH = 2 * GDN_QK_W + GDN_V_W
N_EXPERTS = 32
TOP_K = 4
D_FF = D_MODEL
SWIGLU_LIMIT = 7.0
SWIGLU_ALPHA = 1.702
NORM_EPS = 1e-6

_RET_W = 2 * RET_QK_W + 2 * RET_V_W
_GDN_W = CONV_CH + GDN_V_W
_AB_OFF = _RET_W + _GDN_W
_GATE_OFF = _AB_OFF + 2 * GDN_HEADS

LANES = 128
VMEM_LIMIT = 56 * 1024 * 1024
MERGE_ROWS = 512
MOE_ROWS = 128
MOE_RING = 3
SAMPLE_TILE = 8
RET_STEP_ROWS = 256
RET_STEP_SEQS = 2
GDN_PREP_ROWS = 128
GDN_SCAN_ROWS = 128
GDN_SCAN_SEQS = 2


def _pick_tile(n, candidates):
    for c in candidates:
        if n % c == 0:
            return c
    raise ValueError(f"no tile in {candidates} divides {n}")


def _params(sem, vmem=VMEM_LIMIT):
    return pltpu.CompilerParams(dimension_semantics=sem, vmem_limit_bytes=vmem)


def _resident(shape):
    nd = len(shape)
    return pl.BlockSpec(shape, lambda *_: (0,) * nd, pipeline_mode=pl.Buffered(1))


def _silu(x):
    return x * (1.0 / (1.0 + jnp.exp(-x)))


def _sigmoid(x):
    return 1.0 / (1.0 + jnp.exp(-x))


def _softplus(x):
    return jnp.maximum(x, 0.0) + jnp.log1p(jnp.exp(-jnp.abs(x)))


def _rms(x):
    return x * lax.rsqrt(jnp.mean(x * x, axis=-1, keepdims=True) + NORM_EPS)


def _dot(a, b):
    return jnp.dot(a, b, preferred_element_type=F32)


def _dot_nt(a, b):
    return lax.dot_general(a, b, (((1,), (1,)), ((), ())), preferred_element_type=F32)


def _dot_tn(a, b):
    return lax.dot_general(a, b, (((0,), (0,)), ((), ())), preferred_element_type=F32)


def _dot_hi(a, b):
    return jnp.dot(a, b, preferred_element_type=F32, precision=HIGHEST)


PACK_PLANES = D_MODEL // (2 * LANES)


def _pack_rows(x):
    as_bits = lambda v: pltpu.bitcast(v.astype(BF16).astype(F32), jnp.uint32)
    planes = []
    for j in range(PACK_PLANES):
        lo = as_bits(x[:, j * LANES:(j + 1) * LANES])
        hi = as_bits(x[:, (j + PACK_PLANES) * LANES:(j + PACK_PLANES + 1) * LANES])
        planes.append(jnp.bitwise_or(hi, lax.shift_right_logical(lo, jnp.uint32(16))))
    return planes


def _unpack_planes(planes):
    lo = [pltpu.bitcast(lax.shift_left(p, jnp.uint32(16)), F32) for p in planes]
    hi = [pltpu.bitcast(jnp.bitwise_and(p, jnp.uint32(0xFFFF0000)), F32) for p in planes]
    return lo, hi


def _dot_split(a, b):
    a_hi, b_hi = a.astype(BF16), b.astype(BF16)
    a_lo = (a - a_hi.astype(F32)).astype(BF16)
    b_lo = (b - b_hi.astype(F32)).astype(BF16)
    return _dot(a_hi, b_hi) + (_dot(a_hi, b_lo) + _dot(a_lo, b_hi))


def _row_tile(rows, row0, candidates):
    return _pick_tile(math.gcd(rows, row0) if row0 else rows, candidates)


def _inproj_kernel(x_ref, nw_ref, w_ref, oret_ref, ogdn_ref, ogate_ref, oab_ref):
    h = (_rms(x_ref[...]) * nw_ref[...]).astype(BF16)
    oret_ref[...] = _dot(h, w_ref[:, :_RET_W]).astype(BF16)
    ogdn_ref[...] = _dot(h, w_ref[:, _RET_W:_AB_OFF]).astype(BF16)
    tail = _dot(h, w_ref[:, _AB_OFF:])
    oab_ref[...] = tail[:, :LANES]
    ogate_ref[...] = tail[:, _GATE_OFF - _AB_OFF:_GATE_OFF - _AB_OFF + 2 * D_MODEL].astype(BF16)


def _inproj(x, norm_w, wb):
    rows = x.shape[0]
    tm = _pick_tile(rows, (256, 128, 64, 32, 16, 8))
    row = lambda n: pl.BlockSpec((tm, n), lambda i: (i, 0))
    return pl.pallas_call(
        _inproj_kernel,
        grid=(rows // tm,),
        in_specs=[row(D_MODEL), _resident((1, D_MODEL)), _resident(wb.shape)],
        out_specs=[row(_RET_W), row(_GDN_W), row(2 * D_MODEL), row(LANES)],
        out_shape=[jax.ShapeDtypeStruct((rows, _RET_W), BF16), jax.ShapeDtypeStruct((rows, _GDN_W), BF16),
                   jax.ShapeDtypeStruct((rows, 2 * D_MODEL), BF16), jax.ShapeDtypeStruct((rows, LANES), F32)],
        compiler_params=_params(("parallel",)),
        name="inproj",
    )(x, norm_w.reshape(1, D_MODEL), wb)


def _ret_log_gamma():
    return np.log1p(-np.exp2(-5.0 - np.arange(RET_HEADS, dtype=np.float64)))


def _rope_tables(pos):
    half = RET_DK // 2
    inv = 1.0 / (ROPE_BASE ** (jnp.arange(half, dtype=F32) / half))
    ang = pos.astype(F32)[:, None] * inv[None, :]
    cos, sin = jnp.cos(ang), jnp.sin(ang)
    return jnp.concatenate([cos, cos], axis=-1), jnp.concatenate([-sin, sin], axis=-1)


def _rotary(x, cos, sin):
    return x * cos + pltpu.roll(x, RET_DK // 2, 1) * sin


def _ret_prompt_kernel(q_ref, k_ref, v_ref, g_ref, cos_ref, sin_ref, dmask_ref, qdec_ref, kdec_ref,
                       o_ref, s_ref, *, gammas):
    @pl.when(pl.program_id(1) == 0)
    def _():
        s_ref[...] = jnp.zeros_like(s_ref)

    c = RET_CHUNK
    chains = [(sq, h) for sq in range(q_ref.shape[0]) for h in range(RET_HEADS)]
    nc = range(len(chains))
    qk = [slice(h * RET_DK, (h + 1) * RET_DK) for _, h in chains]
    vv = [slice(h * RET_DV, (h + 1) * RET_DV) for _, h in chains]
    hd = [h for _, h in chains]
    sq = [s_ for s_, _ in chains]
    s = [s_ref[sq[i], hd[i]] for i in nc]
    for ck in range(q_ref.shape[1] // c):
        r = slice(ck * c, (ck + 1) * c)
        cos, sin = cos_ref[r, :], sin_ref[r, :]
        q = [_rotary(q_ref[sq[i], r, qk[i]].astype(F32), cos, sin) for i in nc]
        k = [_rotary(k_ref[sq[i], r, qk[i]].astype(F32), cos, sin) * (RET_DK ** -0.5) for i in nc]
        v = [v_ref[sq[i], r, vv[i]] for i in nc]
        qb = [x.astype(BF16) for x in q]
        inner = [_dot_nt(qb[i], k[i].astype(BF16)) * dmask_ref[hd[i]] for i in nc]
        cross = [_dot((q[i] * qdec_ref[hd[i]]).astype(BF16), s[i].astype(BF16)) for i in nc]
        upd = [_dot_tn((k[i] * kdec_ref[hd[i]]).astype(BF16), v[i]) for i in nc]
        o = [_dot(inner[i].astype(BF16), v[i]) + cross[i] for i in nc]
        s = [s[i] * gammas[hd[i]] + upd[i] for i in nc]
        for i in nc:
            o_ref[sq[i], r, vv[i]] = (_rms(o[i]) * _silu(g_ref[sq[i], r, vv[i]].astype(F32))).astype(BF16)
    for i in nc:
        s_ref[sq[i], hd[i]] = s[i]


def _ret_prompt(p_ret, batch, seq):
    c = RET_CHUNK
    step = RET_STEP_ROWS if seq % RET_STEP_ROWS == 0 else c
    n = seq // step
    lg = _ret_log_gamma()
    idx = np.arange(c, dtype=np.float64)
    diff = idx[:, None] - idx[None, :]
    dmask = np.where(diff >= 0, np.exp(np.maximum(diff, 0.0)[None] * lg[:, None, None]), 0.0)
    qdec = np.broadcast_to(np.exp((idx + 1.0)[None, :] * lg[:, None])[:, :, None], (RET_HEADS, c, RET_DK))
    kdec = np.broadcast_to(np.exp((c - 1.0 - idx)[None, :] * lg[:, None])[:, :, None], (RET_HEADS, c, RET_DK))
    gammas = tuple(float(g) for g in np.exp(c * lg))
    cos, sin = _rope_tables(jnp.arange(seq, dtype=jnp.int32))
    tab = lambda: _resident((RET_HEADS, c, RET_DK))
    nseq = RET_STEP_SEQS if batch % RET_STEP_SEQS == 0 else 1
    p3 = p_ret.reshape(batch, seq, _RET_W)
    o, s_new = pl.pallas_call(
        functools.partial(_ret_prompt_kernel, gammas=gammas),
        grid=(batch // nseq, n),
        in_specs=[pl.BlockSpec((nseq, step, RET_QK_W), lambda b, j: (b, j, 0)),
                  pl.BlockSpec((nseq, step, RET_QK_W), lambda b, j: (b, j, 1)),
                  pl.BlockSpec((nseq, step, RET_V_W), lambda b, j: (b, j, 1)),
                  pl.BlockSpec((nseq, step, RET_V_W), lambda b, j: (b, j, 2)),
                  pl.BlockSpec((step, RET_DK), lambda b, j: (j, 0)),
                  pl.BlockSpec((step, RET_DK), lambda b, j: (j, 0)),
                  tab(), tab(), tab()],
        out_specs=[pl.BlockSpec((nseq, step, RET_V_W), lambda b, j: (b, j, 0)),
                   pl.BlockSpec((nseq, RET_HEADS, RET_DK, RET_DV), lambda b, j: (b, 0, 0, 0))],
        out_shape=[jax.ShapeDtypeStruct((batch, seq, RET_V_W), BF16),
                   jax.ShapeDtypeStruct((batch, RET_HEADS, RET_DK, RET_DV), F32)],
        compiler_params=_params(("parallel", "arbitrary")),
        name="ret_prompt",
    )(p3, p3, p3, p3, cos, sin, jnp.asarray(dmask, F32), jnp.asarray(qdec, F32), jnp.asarray(kdec, F32))
    return o.reshape(batch * seq, RET_V_W), s_new


def _columns(x):
    n = x.shape[0]
    if n < LANES:
        x = jnp.concatenate([x, jnp.zeros((LANES - n, x.shape[1]), x.dtype)], axis=0)
    return x.T


def _ret_sample_kernel(p_ref, cos_ref, sin_ref, s_ref, o_ref, so_ref, *, gammas):
    cos, sin = cos_ref[...], sin_ref[...]
    nb = p_ref.shape[0]
    for h in range(RET_HEADS):
        qk = slice(h * RET_DK, (h + 1) * RET_DK)
        q = _rotary(p_ref[:, qk], cos, sin)
        k = _rotary(p_ref[:, RET_QK_W + h * RET_DK:RET_QK_W + (h + 1) * RET_DK], cos, sin) * (RET_DK ** -0.5)
        v = p_ref[:, 2 * RET_QK_W + h * RET_DV:2 * RET_QK_W + (h + 1) * RET_DV]
        g = p_ref[:, 2 * RET_QK_W + RET_V_W + h * RET_DV:2 * RET_QK_W + RET_V_W + (h + 1) * RET_DV]
        qk_dot = jnp.sum(q * k, axis=-1, keepdims=True)
        qt, kt = _columns(q), _columns(k)
        rows = []
        for j in range(nb):
            s = s_ref[j, h]
            qs = jnp.sum(qt[:, j:j + 1] * s, axis=0, keepdims=True)
            rows.append(qk_dot[j:j + 1] * v[j:j + 1] + gammas[h] * qs)
            so_ref[j, h] = s * gammas[h] + kt[:, j:j + 1] * v[j:j + 1]
        o = jnp.concatenate(rows, axis=0)
        o_ref[:, h * RET_DV:(h + 1) * RET_DV] = (_rms(o) * _silu(g)).astype(BF16)


def _ret_sample(p_ret_s, state):
    ts = p_ret_s.shape[0]
    sb = SAMPLE_TILE
    gammas = tuple(float(g) for g in np.exp(_ret_log_gamma()))
    cos, sin = _rope_tables(jnp.full((1,), PAST_LEN, jnp.int32))
    st = pl.BlockSpec((sb, RET_HEADS, RET_DK, RET_DV), lambda i: (i, 0, 0, 0))
    return pl.pallas_call(
        functools.partial(_ret_sample_kernel, gammas=gammas),
        grid=(ts // sb,),
        in_specs=[pl.BlockSpec((sb, _RET_W), lambda i: (i, 0)), _resident((1, RET_DK)), _resident((1, RET_DK)), st],
        out_specs=[pl.BlockSpec((sb, RET_V_W), lambda i: (i, 0)), st],
        out_shape=[jax.ShapeDtypeStruct((ts, RET_V_W), BF16), jax.ShapeDtypeStruct(state.shape, F32)],
        compiler_params=_params(("parallel",)),
        name="ret_sample",
    )(p_ret_s, cos, sin, state)


def _l2norm(x):
    return x * lax.rsqrt(jnp.sum(x * x, axis=-1, keepdims=True) + NORM_EPS)


def _bdot(a, b):
    return _dot(a.astype(BF16), b.astype(BF16))


def _chunk_masks(c):
    ri = lax.broadcasted_iota(jnp.int32, (c, c), 0)
    ci = lax.broadcasted_iota(jnp.int32, (c, c), 1)
    eye = (ri == ci).astype(F32)
    diag16 = (ri // 16 == ci // 16).astype(F32)
    low32 = jnp.logical_and(ri // 32 == ci // 32, ri // 16 > ci // 16).astype(F32)
    low64 = (ri // 32 > ci // 32).astype(F32)
    return ri >= ci, ri > ci, eye, diag16, low32, low64


def _unit_lower_inverse(a, eye, diag16, low32, low64):
    many = lambda f, *ls: [f(*args) for args in zip(*ls)]
    c = eye.shape[0]
    pair = lambda p, q, rhs: _bdot(jnp.concatenate([p, q], axis=0), rhs)
    n = [-(x * diag16) for x in a]
    n2 = many(_bdot, n, n)
    n34 = many(pair, n, n2, n2)
    x = [eye + p + q + r[:c] for p, q, r in zip(n, n2, n34)]
    n4 = [r[c:] for r in n34]
    xn = many(pair, x, n4, n4)
    x = [u + v[:c] for u, v in zip(x, xn)]
    x = many(lambda u, v: u + v, x, many(_bdot, x, [v[c:] for v in xn]))
    for mask in (low32, low64):
        r = many(_bdot, [y * mask for y in a], x)
        x = many(lambda u, v: u - v, x, many(_bdot, x, r))
    return x


def _gdn_gates(ab, alog, dtb):
    g = -jnp.exp(alog) * _softplus(ab + dtb)
    return g, _sigmoid(ab)


def _gdn_prep_kernel(x_ref, prev_ref, ab_ref, cw_ref, alog_ref, dtb_ref,
                     u_ref, w_ref, qg_ref, kg_ref, at_ref, eg_ref, cv_ref, act_ref):
    c = GDN_CHUNK
    rows = x_ref.shape[0]
    x = x_ref[...]
    halo = prev_ref.shape[0]
    prev = jnp.where(pl.program_id(1) == 0, jnp.zeros_like(prev_ref), prev_ref[...])
    xcat = jnp.concatenate([prev, x], axis=0)
    ti = lax.broadcasted_iota(jnp.int32, (rows, rows + halo), 0)
    ui = lax.broadcasted_iota(jnp.int32, (rows, rows + halo), 1)
    xf = x.astype(F32)
    acc = xf * cw_ref[CONV_W - 1:CONV_W, :]
    for i in range(CONV_W - 1):
        shift = (ui == ti + (halo - (CONV_W - 1) + i)).astype(BF16)
        acc = acc + _dot(shift, xcat) * cw_ref[i:i + 1, :]
    cv_ref[0] = xf[rows - (CONV_W - 1):, :]
    act_ref[...] = _silu(acc)

    g_all, beta_all = _gdn_gates(ab_ref[...], alog_ref[...], dtb_ref[...])
    lower, strict, eye, diag16, low32, low64 = _chunk_masks(c)
    gc_all, gr_all = [], []
    for ck in range(rows // c):
        gc_ck = _dot_hi(lower.astype(F32), g_all[ck * c:(ck + 1) * c])
        gc_all.append(gc_ck)
        gr_all.append(_columns(gc_ck))
        eg_ref[ck] = jnp.exp(gc_ck[c - 1:c, :])
    chains = [(ck, h) for ck in range(rows // c) for h in range(GDN_HEADS)]
    rs = [slice(ck * c, (ck + 1) * c) for ck, _ in chains]
    sl = [slice(h * GDN_DK, (h + 1) * GDN_DK) for _, h in chains]
    nc = range(len(chains))
    q = [_l2norm(act_ref[rs[i], sl[i]]) * (GDN_DK ** -0.5) for i in nc]
    k = [_l2norm(act_ref[rs[i], GDN_QK_W + sl[i].start:GDN_QK_W + sl[i].stop]) for i in nc]
    v = [act_ref[rs[i], 2 * GDN_QK_W + sl[i].start:2 * GDN_QK_W + sl[i].stop] for i in nc]
    beta = [beta_all[rs[i], GDN_HEADS + h:GDN_HEADS + h + 1] for i, (_, h) in enumerate(chains)]
    gc = [gc_all[ck][:, h:h + 1] for ck, h in chains]
    gr = [gr_all[ck][h:h + 1, :c] for ck, h in chains]
    decay = [jnp.exp(jnp.where(lower, gc[i] - gr[i], -jnp.inf)) for i in nc]
    exp_g = [jnp.exp(x) for x in gc]
    kb = [k[i] * beta[i] for i in nc]
    kbf = [x.astype(BF16) for x in k]
    kq = [_dot_nt(jnp.concatenate([kb[i].astype(BF16), q[i].astype(BF16)], axis=0), kbf[i]) for i in nc]
    a = [kq[i][:c] * jnp.where(strict, decay[i], 0.0) for i in nc]
    attn = [kq[i][c:] * decay[i] for i in nc]
    t = _unit_lower_inverse(a, eye, diag16, low32, low64)
    uw = [_bdot(t[i], jnp.concatenate([v[i] * beta[i], kb[i] * exp_g[i]], axis=1)) for i in nc]
    for i in nc:
        u_ref[rs[i], sl[i]] = uw[i][:, :GDN_DV]
        w_ref[rs[i], sl[i]] = uw[i][:, GDN_DV:].astype(BF16)
        at_ref[rs[i], sl[i]] = jnp.concatenate([attn[i], jnp.zeros((c, GDN_DK - c), F32)], axis=1).astype(BF16)
        qg_ref[rs[i], sl[i]] = (q[i] * exp_g[i]).astype(BF16)
        kg_ref[rs[i], sl[i]] = (k[i] * jnp.exp(gc[i][c - 1:c, :] - gc[i])).astype(BF16)


def _gdn_scan_kernel(u_ref, w_ref, qg_ref, kg_ref, at_ref, eg_ref, z_ref, nw_ref, o_ref, s_ref):
    c = GDN_CHUNK

    @pl.when(pl.program_id(1) == 0)
    def _():
        s_ref[...] = jnp.zeros_like(s_ref)

    nw = nw_ref[...]
    chains = [(q, h) for q in range(u_ref.shape[0]) for h in range(GDN_HEADS)]
    sl = [slice(h * GDN_DK, (h + 1) * GDN_DK) for _, h in chains]
    nc = range(len(chains))
    s = [s_ref[q, h] for q, h in chains]
    for ck in range(u_ref.shape[1] // c):
        r = slice(ck * c, (ck + 1) * c)
        sb = [x.astype(BF16) for x in s]
        ws = [_dot(w_ref[q, r, sl[i]], sb[i]) for i, (q, _) in enumerate(chains)]
        qs = [_dot(qg_ref[q, r, sl[i]], sb[i]) for i, (q, _) in enumerate(chains)]
        vnb = [(u_ref[q, r, sl[i]] - ws[i]).astype(BF16) for i, (q, _) in enumerate(chains)]
        o = [qs[i] + _dot(at_ref[q, r, h * GDN_DK:h * GDN_DK + c], vnb[i]) for i, (q, h) in enumerate(chains)]
        s = [s[i] * eg_ref[q, ck][:, h:h + 1] + _dot_tn(kg_ref[q, r, sl[i]], vnb[i])
             for i, (q, h) in enumerate(chains)]
        for i, (q, _) in enumerate(chains):
            o_ref[q, r, sl[i]] = (_rms(o[i]) * nw * _silu(z_ref[q, r, sl[i]].astype(F32))).astype(BF16)
    for i, (q, h) in enumerate(chains):
        s_ref[q, h] = s[i]


def _gdn_prompt(p_gdn, p_ab, batch, seq, conv_w, a_log, dt_bias, gdn_norm_w):
    c = GDN_CHUNK
    rows = GDN_PREP_ROWS
    t = batch * seq
    nt = seq // rows
    alog = jnp.pad(a_log.astype(F32), (0, LANES - GDN_HEADS)).reshape(1, LANES)
    dtb = jnp.pad(dt_bias.astype(F32), (0, LANES - GDN_HEADS)).reshape(1, LANES)
    wide = lambda: pl.BlockSpec((rows, GDN_V_W), lambda b, j: (b * nt + j, 0))
    u, w, qg, kg, at, eg, conv_new = pl.pallas_call(
        _gdn_prep_kernel,
        grid=(batch, nt),
        in_specs=[pl.BlockSpec((rows, CONV_CH), lambda b, j: (b * nt + j, 0)),
                  pl.BlockSpec((16, CONV_CH), lambda b, j: (jnp.maximum((b * nt + j) * (rows // 16) - 1, 0), 0)),
                  pl.BlockSpec((rows, LANES), lambda b, j: (b * nt + j, 0)),
                  _resident((CONV_W, CONV_CH)), _resident((1, LANES)), _resident((1, LANES))],
        out_specs=[wide(), wide(), wide(), wide(), wide(),
                   pl.BlockSpec((rows // c, 1, LANES), lambda b, j: (b * nt + j, 0, 0)),
                   pl.BlockSpec((1, CONV_W - 1, CONV_CH), lambda b, j: (b, 0, 0))],
        out_shape=[jax.ShapeDtypeStruct((t, GDN_V_W), F32)] + [jax.ShapeDtypeStruct((t, GDN_V_W), BF16)] * 4
        + [jax.ShapeDtypeStruct((t // c, 1, LANES), F32),
           jax.ShapeDtypeStruct((batch, CONV_W - 1, CONV_CH), F32)],
        scratch_shapes=[pltpu.VMEM((rows, CONV_CH), F32)],
        compiler_params=_params(("parallel", "arbitrary")),
        name="gdn_prep",
    )(p_gdn, p_gdn, p_ab, conv_w.astype(F32), alog, dtb)
    srows = GDN_SCAN_ROWS
    n = seq // srows
    nq = CONV_CH // GDN_V_W
    nseq = GDN_SCAN_SEQS if batch % GDN_SCAN_SEQS == 0 else 1
    per_seq = lambda a: a.reshape((batch, seq // (t // a.shape[0])) + a.shape[1:])
    blk = lambda: pl.BlockSpec((nseq, srows, GDN_V_W), lambda b, j: (b, j, 0))
    o, s_new = pl.pallas_call(
        _gdn_scan_kernel,
        grid=(batch // nseq, n),
        in_specs=[blk(), blk(), blk(), blk(), blk(),
                  pl.BlockSpec((nseq, srows // c, 1, LANES), lambda b, j: (b, j, 0, 0)),
                  pl.BlockSpec((nseq, srows, GDN_V_W), lambda b, j: (b, j, nq)),
                  _resident((1, GDN_DV))],
        out_specs=[blk(), pl.BlockSpec((nseq, GDN_HEADS, GDN_DK, GDN_DV), lambda b, j: (b, 0, 0, 0))],
        out_shape=[jax.ShapeDtypeStruct((batch, seq, GDN_V_W), BF16),
                   jax.ShapeDtypeStruct((batch, GDN_HEADS, GDN_DK, GDN_DV), F32)],
        compiler_params=_params(("parallel", "arbitrary")),
        name="gdn_scan",
    )(per_seq(u), per_seq(w), per_seq(qg), per_seq(kg), per_seq(at), per_seq(eg), per_seq(p_gdn),
      gdn_norm_w.astype(F32).reshape(1, GDN_DV))
    return o.reshape(t, GDN_V_W), s_new, conv_new


def _gdn_sample_kernel(x_ref, ab_ref, sc_ref, cw_ref, alog_ref, dtb_ref, nw_ref, s_ref, o_ref, so_ref, sco_ref):
    nb = x_ref.shape[0]
    x = x_ref[:, :CONV_CH]
    acc = x * cw_ref[CONV_W - 1:CONV_W, :]
    for i in range(CONV_W - 1):
        acc = acc + sc_ref[i] * cw_ref[i:i + 1, :]
    for i in range(CONV_W - 2):
        sco_ref[i] = sc_ref[i + 1]
    sco_ref[CONV_W - 2] = x
    u = _silu(acc)
    g_all, beta_all = _gdn_gates(ab_ref[...], alog_ref[...], dtb_ref[...])
    eg_all = jnp.exp(g_all)
    nw = nw_ref[...]
    for h in range(GDN_HEADS):
        sl = slice(h * GDN_DK, (h + 1) * GDN_DK)
        q = _l2norm(u[:, sl]) * (GDN_DK ** -0.5)
        k = _l2norm(u[:, GDN_QK_W + h * GDN_DK:GDN_QK_W + (h + 1) * GDN_DK])
        v = u[:, 2 * GDN_QK_W + h * GDN_DV:2 * GDN_QK_W + (h + 1) * GDN_DV]
        beta = beta_all[:, GDN_HEADS + h:GDN_HEADS + h + 1]
        eg = eg_all[:, h:h + 1]
        qk_dot = jnp.sum(q * k, axis=-1, keepdims=True)
        qt, kt = _columns(q), _columns(k)
        rows = []
        for j in range(nb):
            s = s_ref[j, h]
            kcol = kt[:, j:j + 1]
            ks = jnp.sum(kcol * s, axis=0, keepdims=True)
            qs = jnp.sum(qt[:, j:j + 1] * s, axis=0, keepdims=True)
            ej = eg[j:j + 1]
            v_new = beta[j:j + 1] * (v[j:j + 1] - ej * ks)
            rows.append(ej * qs + qk_dot[j:j + 1] * v_new)
            so_ref[j, h] = s * ej + kcol * v_new
        o = jnp.concatenate(rows, axis=0)
        z = x_ref[:, CONV_CH + h * GDN_DV:CONV_CH + (h + 1) * GDN_DV]
        o_ref[:, sl] = (_rms(o) * nw * _silu(z)).astype(BF16)


def _gdn_sample(p_gdn_s, p_ab_s, state, conv_state, conv_w, a_log, dt_bias, gdn_norm_w):
    ts = p_gdn_s.shape[0]
    sb = SAMPLE_TILE
    alog = jnp.pad(a_log.astype(F32), (0, LANES - GDN_HEADS)).reshape(1, LANES)
    dtb = jnp.pad(dt_bias.astype(F32), (0, LANES - GDN_HEADS)).reshape(1, LANES)
    sc = jnp.swapaxes(conv_state.astype(F32), 0, 1)
    st = pl.BlockSpec((sb, GDN_HEADS, GDN_DK, GDN_DV), lambda i: (i, 0, 0, 0))
    scs = pl.BlockSpec((CONV_W - 1, sb, CONV_CH), lambda i: (0, i, 0))
    o, s_new, sc_new = pl.pallas_call(
        _gdn_sample_kernel,
        grid=(ts // sb,),
        in_specs=[pl.BlockSpec((sb, _GDN_W), lambda i: (i, 0)), pl.BlockSpec((sb, LANES), lambda i: (i, 0)), scs,
                  _resident((CONV_W, CONV_CH)), _resident((1, LANES)), _resident((1, LANES)),
                  _resident((1, GDN_DV)), st],
        out_specs=[pl.BlockSpec((sb, GDN_V_W), lambda i: (i, 0)), st, scs],
        out_shape=[jax.ShapeDtypeStruct((ts, GDN_V_W), BF16), jax.ShapeDtypeStruct(state.shape, F32),
                   jax.ShapeDtypeStruct(sc.shape, F32)],
        compiler_params=_params(("parallel",)),
        name="gdn_sample",
    )(p_gdn_s, p_ab_s, sc, conv_w.astype(F32), alog, dtb, gdn_norm_w.astype(F32).reshape(1, GDN_DV), state)
    return o, s_new, jnp.swapaxes(sc_new, 0, 1)


_NO_EXPERT = -1e30


def _merge_kernel(oa0_ref, ob0_ref, gate0_ref, x0_ref, oa1_ref, ob1_ref, gate1_ref, x1in_ref,
                  wa_ref, wb_ref, wo_ref, nw_ref, wr_ref, br_ref, x1_ref, h2_ref, ti_ref, tw_ref, *, n_first):
    first = pl.program_id(0) < n_first
    pick = lambda a, b: jnp.where(first, a[...], b[...])
    gate = pick(gate0_ref, gate1_ref)
    ya = _dot(pick(oa0_ref, oa1_ref), wa_ref[...])
    yb = _dot(pick(ob0_ref, ob1_ref), wb_ref[...])
    ga = gate[:, :D_MODEL].astype(F32)
    gb = gate[:, D_MODEL:].astype(F32)
    m = _sigmoid(ga) * ya + _sigmoid(gb) * yb
    x1 = pick(x0_ref, x1in_ref) + _dot(m.astype(BF16), wo_ref[...])
    x1_ref[...] = x1
    h2 = _rms(x1) * nw_ref[...]
    for j, plane in enumerate(_pack_rows(h2)):
        h2_ref[:, j, :] = plane
    lg = _dot_split(h2, wr_ref[...]) + br_ref[...]
    lane = lax.broadcasted_iota(jnp.int32, lg.shape, 1).astype(F32)
    vals, idxs = [], []
    for _ in range(TOP_K):
        top = jnp.max(lg, axis=-1, keepdims=True)
        idx = jnp.min(jnp.where(lg == top, lane, float(LANES)), axis=-1, keepdims=True)
        vals.append(top)
        idxs.append(idx)
        lg = jnp.where(lane == idx, _NO_EXPERT, lg)
    es = [jnp.exp(v - vals[0]) for v in vals]
    inv_total = 1.0 / functools.reduce(lambda a, b: a + b, es)
    ti = jnp.zeros_like(lg)
    tw = jnp.zeros_like(lg)
    for k in range(TOP_K):
        ti = jnp.where(lane == float(k), idxs[k], ti)
        tw = jnp.where(lane == float(k), es[k] * inv_total, tw)
    ti_ref[...] = ti.astype(jnp.int32)
    tw_ref[...] = tw


def _merge(group0, group1, weights):
    r0 = group0[3].shape[0]
    tm = _pick_tile(r0, (MERGE_ROWS, 128, 64, 32, 16, 8))
    pad = -group1[3].shape[0] % tm
    group1 = tuple(jnp.pad(a, ((0, pad), (0, 0))) for a in group1)
    r1 = group1[3].shape[0]
    n0, total = r0 // tm, r0 + r1
    widths = (RET_V_W, GDN_V_W, 2 * D_MODEL, D_MODEL)
    specs0 = [pl.BlockSpec((tm, n), lambda i: (jnp.minimum(i, n0 - 1), 0)) for n in widths]
    specs1 = [pl.BlockSpec((tm, n), lambda i: (jnp.maximum(i - n0, 0), 0)) for n in widths]
    out = lambda n: pl.BlockSpec((tm, n), lambda i: (i, 0))
    sub = PACK_PLANES
    sq = (D_MODEL, D_MODEL)
    return pl.pallas_call(
        functools.partial(_merge_kernel, n_first=n0),
        grid=(total // tm,),
        in_specs=specs0 + specs1 + [_resident(sq), _resident(sq), _resident(sq), _resident((1, D_MODEL)),
                                    _resident((D_MODEL, LANES)), _resident((1, LANES))],
        out_specs=[out(D_MODEL), pl.BlockSpec((tm, sub, LANES), lambda i: (i, 0, 0)), out(LANES), out(LANES)],
        out_shape=[jax.ShapeDtypeStruct((total, D_MODEL), F32), jax.ShapeDtypeStruct((total, sub, LANES), jnp.uint32),
                   jax.ShapeDtypeStruct((total, LANES), jnp.int32), jax.ShapeDtypeStruct((total, LANES), F32)],
        compiler_params=_params(("parallel",)),
        name="merge",
    )(*group0, *group1, *weights)


def _merge_weights(w_a, w_b, w_o, ffn_norm_w, w_router, b_router):
    wr = jnp.pad(w_router.astype(F32), ((0, 0), (0, LANES - N_EXPERTS)))
    br = jnp.pad(b_router.astype(F32), (0, LANES - N_EXPERTS), constant_values=_NO_EXPERT).reshape(1, LANES)
    return (w_a.astype(BF16), w_b.astype(BF16), w_o.astype(BF16), ffn_norm_w.astype(F32).reshape(1, D_MODEL), wr, br)


def _route(top_i, gate, n_tokens):
    rows = MOE_ROWS
    n = n_tokens * TOP_K
    flat_e = top_i.reshape(n).astype(jnp.int32)
    bits = max(1, (n - 1).bit_length())
    assert bits + (N_EXPERTS - 1).bit_length() <= 31
    order = lax.sort((flat_e << bits) | jnp.arange(n, dtype=jnp.int32), is_stable=False) & ((1 << bits) - 1)
    counts = jnp.sum((flat_e[:, None] == jnp.arange(N_EXPERTS, dtype=jnp.int32)[None, :]).astype(jnp.int32), axis=0)
    start = jnp.cumsum(counts) - counts
    pcounts = (counts + rows - 1) // rows * rows
    pend = jnp.cumsum(pcounts)
    pstart = pend - pcounts
    nb = -(-n // rows) + N_EXPERTS
    blk = jnp.arange(nb, dtype=jnp.int32)
    block_e = jnp.minimum(jnp.sum((pend[None, :] <= (blk * rows)[:, None]).astype(jnp.int32), axis=1),
                          N_EXPERTS - 1).astype(jnp.int32)
    nb_used = (pend[-1] // rows).astype(jnp.int32).reshape(1)
    is_e = block_e[:, None] == jnp.arange(N_EXPERTS, dtype=jnp.int32)[None, :]
    of_block = lambda table: jnp.sum(jnp.where(is_e, table[None, :], 0), axis=1)
    within = (blk * rows - of_block(pstart))[:, None] + jnp.arange(rows, dtype=jnp.int32)[None, :]
    valid = jnp.logical_and(within < of_block(counts)[:, None], (blk < nb_used[0])[:, None])
    flat = order[jnp.clip(of_block(start)[:, None] + within, 0, n - 1)]
    spare = n + (blk % MOE_RING)[:, None] * rows + jnp.arange(rows, dtype=jnp.int32)[None, :]
    src = jnp.where(valid, flat // TOP_K, 0)
    dst = jnp.where(valid, flat, spare)
    ahead = lambda k: jnp.concatenate([src[k:]] + [src[-1:]] * k, axis=0)
    slab = jnp.concatenate([src, dst, ahead(1), ahead(2)], axis=1)
    row_w = jnp.where(valid, gate[:, :TOP_K].reshape(n)[flat], 0.0)
    row_w = jnp.broadcast_to(row_w[:, :, None], (nb, rows, LANES))
    return block_e, nb_used, slab, row_w


def _expert_kernel(be_ref, nbu_ref, slab_ref, h_ref, roww_ref, wgu_ref, bgu_ref, wd_ref, bd_ref, y_ref,
                   idx0_ref, idx1_ref, idx2_ref, xbuf_ref, ybuf_ref, wgu_bf_ref, wd_bf_ref, isem, gsem, ssem):
    rows = MOE_ROWS
    ring = MOE_RING
    idx_refs = (idx0_ref, idx1_ref, idx2_ref)
    assert len(idx_refs) == ring
    sub = PACK_PLANES
    i = pl.program_id(0)
    nbu = nbu_ref[0]
    slot = i % ring
    active = i < nbu
    n_real = y_ref.shape[0] - ring * rows

    def slab_copy(blk, sl):
        return pltpu.make_async_copy(slab_ref.at[blk], idx_refs[sl], isem.at[sl])

    def gather_row(tok, sl, r):
        return pltpu.make_async_copy(h_ref.at[tok], xbuf_ref.at[sl, :, r, :], gsem.at[sl])

    def scatter_row(sl, r, f):
        return pltpu.make_async_copy(ybuf_ref.at[sl, :, r, :], y_ref.at[f], ssem.at[sl])

    def gather_wait(sl):
        pltpu.make_async_copy(xbuf_ref.at[sl], xbuf_ref.at[sl], gsem.at[sl]).wait()

    def scatter_wait(sl):
        pltpu.make_async_copy(ybuf_ref.at[sl], ybuf_ref.at[sl], ssem.at[sl]).wait()

    @pl.when(i == 0)
    def _():
        slab_copy(0, 0).start()
        ybuf_ref[...] = jnp.zeros_like(ybuf_ref)
        for sl in range(ring):
            def fill(r, carry, sl=sl):
                scatter_row(sl, r, n_real + sl * rows + r).start()
                return carry
            lax.fori_loop(0, rows, fill, 0)

    def block(sl):
        nxt, nxt2 = (sl + 1) % ring, (sl + 2) % ring
        idx_ref = idx_refs[sl]
        slab_copy(i, sl).wait()

        @pl.when(i + 1 < nbu)
        def _():
            slab_copy(i + 1, nxt).start()

        if sl == 0:
            @pl.when(i == 0)
            def _():
                def first(r, carry):
                    gather_row(idx_ref[r], 0, r).start()
                    gather_row(idx_ref[2 * rows + r], 1, r).start()
                    return carry
                lax.fori_loop(0, rows, first, 0)

        changed = jnp.logical_or(i == 0, be_ref[i] != be_ref[jnp.maximum(i - 1, 0)])

        @pl.when(changed)
        def _():
            wgu_bf_ref[...] = wgu_ref[0].astype(BF16)
            wd_bf_ref[...] = wd_ref[0].astype(BF16)

        gather_wait(sl)
        scatter_wait(sl)
        lo, hi = _unpack_planes([xbuf_ref[sl, j] for j in range(sub)])
        xb = jnp.concatenate(lo + hi, axis=1).astype(BF16)
        hb = _dot(xb, wgu_bf_ref[...]) + bgu_ref[0]
        for r in range(rows):
            gather_row(idx_ref[3 * rows + r], nxt2, r).start(priority=r % 2)
        glu = jnp.minimum(hb[:, :D_FF], SWIGLU_LIMIT)
        lin = jnp.clip(hb[:, D_FF:], -SWIGLU_LIMIT, SWIGLU_LIMIT)
        act = (glu * _sigmoid(SWIGLU_ALPHA * glu) * (lin + 1.0)).astype(BF16)
        yv = (_dot(act, wd_bf_ref[...]) + bd_ref[0]) * roww_ref[0, :, 0:1]
        for j, plane in enumerate(_pack_rows(yv)):
            ybuf_ref[sl, j] = plane
        for r in range(rows):
            scatter_row(sl, r, idx_ref[rows + r]).start(priority=(r + 1) % 2)

        @pl.when(i == nbu - 1)
        def _():
            for s in (sl, nxt, nxt2):
                scatter_wait(s)
            gather_wait(nxt)
            gather_wait(nxt2)

    for sl in range(ring):
        pl.when(jnp.logical_and(active, slot == sl))(functools.partial(block, sl))


def _experts(h2, t, block_e, nb_used, slab, row_w, w_gate_up, b_gate_up, w_down, b_down):
    rows = MOE_ROWS
    ring = MOE_RING
    assert (ring * rows) % TOP_K == 0
    nb = slab.shape[0]
    sub = PACK_PLANES
    grid_spec = pltpu.PrefetchScalarGridSpec(
        num_scalar_prefetch=2,
        grid=(nb,),
        in_specs=[pl.BlockSpec(memory_space=pl.ANY),
                  pl.BlockSpec(memory_space=pl.ANY),
                  pl.BlockSpec((1, rows, LANES), lambda i, be, nbu: (i, 0, 0)),
                  pl.BlockSpec((1, D_MODEL, 2 * D_FF), lambda i, be, nbu: (be[i], 0, 0)),
                  pl.BlockSpec((1, 1, 2 * D_FF), lambda i, be, nbu: (be[i], 0, 0)),
                  pl.BlockSpec((1, D_FF, D_MODEL), lambda i, be, nbu: (be[i], 0, 0)),
                  pl.BlockSpec((1, 1, D_MODEL), lambda i, be, nbu: (be[i], 0, 0))],
        out_specs=pl.BlockSpec(memory_space=pl.ANY),
        scratch_shapes=[pltpu.SMEM((slab.shape[1],), jnp.int32)] * ring + [
                        pltpu.VMEM((ring, sub, rows, LANES), jnp.uint32),
                        pltpu.VMEM((ring, sub, rows, LANES), jnp.uint32),
                        pltpu.VMEM((D_MODEL, 2 * D_FF), BF16),
                        pltpu.VMEM((D_FF, D_MODEL), BF16),
                        pltpu.SemaphoreType.DMA((ring,)),
                        pltpu.SemaphoreType.DMA((ring,)),
                        pltpu.SemaphoreType.DMA((ring,))])
    return pl.pallas_call(
        _expert_kernel,
        grid_spec=grid_spec,
        out_shape=jax.ShapeDtypeStruct((t * TOP_K + ring * rows, sub, LANES), jnp.uint32),
        compiler_params=_params(("arbitrary",)),
        name="experts",
    )(block_e, nb_used, slab, h2, row_w, w_gate_up, b_gate_up.reshape(N_EXPERTS, 1, 2 * D_FF), w_down,
      b_down.reshape(N_EXPERTS, 1, D_MODEL))


def _combine_kernel(y_ref, x1_ref, nw_ref, o_ref, sum_ref, *, final):
    lo, hi = None, None
    for k in range(TOP_K):
        (l,), (h,) = _unpack_planes([y_ref[:, k]])
        lo, hi = (l, h) if lo is None else (lo + l, hi + h)
    sum_ref[:, :PACK_PLANES, :] = lo
    sum_ref[:, PACK_PLANES:, :] = hi
    acc = x1_ref[...] + jnp.concatenate([sum_ref[:, j, :] for j in range(D_MODEL // LANES)], axis=1)
    o_ref[...] = _rms(acc) * nw_ref[...] if final else acc


def _combine(y, x1, row0, rows, norm_w, final):
    tm = _row_tile(rows, row0, (256, 128, 64, 32, 16, 8))
    off = row0 // tm
    sub = PACK_PLANES
    y4 = y.reshape(y.shape[0] // TOP_K, TOP_K, sub, LANES)
    return pl.pallas_call(
        functools.partial(_combine_kernel, final=final),
        grid=(rows // tm,),
        in_specs=[pl.BlockSpec((tm, TOP_K, sub, LANES), lambda i: (off + i, 0, 0, 0)),
                  pl.BlockSpec((tm, D_MODEL), lambda i: (off + i, 0)), _resident((1, D_MODEL))],
        out_specs=pl.BlockSpec((tm, D_MODEL), lambda i: (i, 0)),
        out_shape=jax.ShapeDtypeStruct((rows, D_MODEL), F32),
        scratch_shapes=[pltpu.VMEM((tm, D_MODEL // LANES, LANES), F32)],
        compiler_params=_params(("parallel",)),
        name="combine",
    )(y4, x1, norm_w.astype(F32).reshape(1, D_MODEL))


def kernel(x_prompt, x_sample, state_ret, state_gdn, state_conv, attn_norm_w, w_in, conv_w, a_log, dt_bias, gdn_norm_w, w_branch_a, w_branch_b, w_out, ffn_norm_w, w_router, b_router, w_gate_up, b_gate_up, w_down, b_down, final_norm_w):
    bp, lp, d = x_prompt.shape
    bs, ls, _ = x_sample.shape
    assert ls == 1 and d == D_MODEL and lp % RET_CHUNK == 0 and bs % SAMPLE_TILE == 0
    depth = w_in.shape[0]
    tp = bp * lp
    t = tp + bs
    xp, xs = x_prompt.reshape(tp, d).astype(F32), x_sample.reshape(bs, d).astype(F32)
    rp, gp, cp, rs, gs, cs = [], [], [], [], [], []
    for l in range(depth):
        wb = jnp.pad(w_in[l], ((0, 0), (0, -w_in.shape[2] % LANES))).astype(BF16)
        pp_ret, pp_gdn, pp_gate, pp_ab = _inproj(xp, attn_norm_w[l], wb)
        ps_ret, ps_gdn, ps_gate, ps_ab = _inproj(xs, attn_norm_w[l], wb)
        op_ret, s_ret_p = _ret_prompt(pp_ret, bp, lp)
        os_ret, s_ret_s = _ret_sample(ps_ret.astype(F32), state_ret[l].astype(F32))
        op_gdn, s_gdn_p, conv_p = _gdn_prompt(pp_gdn, pp_ab, bp, lp, conv_w[l], a_log[l], dt_bias[l], gdn_norm_w[l])
        os_gdn, s_gdn_s, conv_s = _gdn_sample(ps_gdn.astype(F32), ps_ab, state_gdn[l].astype(F32), state_conv[l],
                                              conv_w[l], a_log[l], dt_bias[l], gdn_norm_w[l])
        mw = _merge_weights(w_branch_a[l], w_branch_b[l], w_out[l], ffn_norm_w[l], w_router[l], b_router[l])
        x1, h2, top_i, gate = _merge((op_ret, op_gdn, pp_gate, xp), (os_ret, os_gdn, ps_gate, xs), mw)
        block_e, nb_used, slab, row_w = _route(top_i[:t, :TOP_K], gate[:t], t)
        y = _experts(h2, t, block_e, nb_used, slab, row_w, w_gate_up[l], b_gate_up[l], w_down[l], b_down[l])
        last = l == depth - 1
        norm_w = final_norm_w if last else jnp.ones((d,), F32)
        xp = _combine(y, x1, 0, tp, norm_w, last)
        xs = _combine(y, x1, tp, bs, norm_w, last)
        rp.append(s_ret_p); gp.append(s_gdn_p); cp.append(conv_p)
        rs.append(s_ret_s); gs.append(s_gdn_s); cs.append(conv_s)
    y_prompt = xp.reshape(bp, lp, d).astype(x_prompt.dtype)
    y_sample = xs.reshape(bs, ls, d).astype(x_sample.dtype)
    return (y_prompt, y_sample,
            jnp.stack(rp).astype(state_ret.dtype), jnp.stack(gp).astype(state_gdn.dtype),
            jnp.stack(cp).astype(state_conv.dtype),
            jnp.stack(rs).astype(state_ret.dtype), jnp.stack(gs).astype(state_gdn.dtype),
            jnp.stack(cs).astype(state_conv.dtype))
```

```python
import functools
import math

import numpy as np
import jax
import jax.numpy as jnp
from jax import lax
from jax.experimental import pallas as pl
from jax.experimental.pallas import tpu as pltpu

F32 = jnp.float32
BF16 = jnp.bfloat16
HIGHEST = lax.Precision.HIGHEST

D_MODEL = 1024
PAST_LEN = 16384
RET_HEADS, RET_DK, RET_DV = 4, 128, 256
RET_QK_W, RET_V_W = RET_HEADS * RET_DK, RET_HEADS * RET_DV
RET_CHUNK = 128
ROPE_BASE = 10000.0
GDN_HEADS, GDN_DK, GDN_DV = 8, 128, 128
GDN_QK_W, GDN_V_W = GDN_HEADS * GDN_DK, GDN_HEADS * GDN_DV
GDN_CHUNK = 64
CONV_W = 4
CONV_CH = 2 * GDN_QK_W + GDN_V_W
N_EXPERTS = 32
TOP_K = 4
D_FF = D_MODEL
SWIGLU_LIMIT = 7.0
SWIGLU_ALPHA = 1.702
NORM_EPS = 1e-6

_RET_W = 2 * RET_QK_W + 2 * RET_V_W
_GDN_W = CONV_CH + GDN_V_W
_AB_OFF = _RET_W + _GDN_W
_GATE_OFF = _AB_OFF + 2 * GDN_HEADS

LANES = 128
VMEM_LIMIT = 56 * 1024 * 1024
MERGE_ROWS = 512
MOE_ROWS = 256
MOE_RING = 3
SAMPLE_TILE = 8
RET_STEP_ROWS = 256
RET_STEP_SEQS = 4
GDN_PREP_ROWS = 128
GDN_SCAN_ROWS = 128
GDN_SCAN_SEQS = 4


def _pick_tile(n, candidates):
    for c in candidates:
        if n % c == 0:
            return c
    raise ValueError(f"no tile in {candidates} divides {n}")


def _params(sem, vmem=VMEM_LIMIT):
    return pltpu.CompilerParams(dimension_semantics=sem, vmem_limit_bytes=vmem)


def _resident(shape):
    nd = len(shape)
    return pl.BlockSpec(shape, lambda *_: (0,) * nd, pipeline_mode=pl.Buffered(1))


def _silu(x):
    return x * (1.0 / (1.0 + jnp.exp(-x)))


def _sigmoid(x):
    return 1.0 / (1.0 + jnp.exp(-x))


def _softplus(x):
    return jnp.maximum(x, 0.0) + jnp.log1p(jnp.exp(-jnp.abs(x)))


def _rms(x):
    return x * lax.rsqrt(jnp.mean(x * x, axis=-1, keepdims=True) + NORM_EPS)


def _dot(a, b):
    return jnp.dot(a, b, preferred_element_type=F32)


def _dot_nt(a, b):
    return lax.dot_general(a, b, (((1,), (1,)), ((), ())), preferred_element_type=F32)


def _dot_tn(a, b):
    return lax.dot_general(a, b, (((0,), (0,)), ((), ())), preferred_element_type=F32)


def _dot_hi(a, b):
    return jnp.dot(a, b, preferred_element_type=F32, precision=HIGHEST)


PACK_PLANES = D_MODEL // (2 * LANES)


def _pack_rows(x):
    as_bits = lambda v: pltpu.bitcast(v.astype(BF16).astype(F32), jnp.uint32)
    planes = []
    for j in range(PACK_PLANES):
        lo = as_bits(x[:, j * LANES:(j + 1) * LANES])
        hi = as_bits(x[:, (j + PACK_PLANES) * LANES:(j + PACK_PLANES + 1) * LANES])
        planes.append(jnp.bitwise_or(hi, lax.shift_right_logical(lo, jnp.uint32(16))))
    return planes


def _unpack_planes(planes):
    lo = [pltpu.bitcast(lax.shift_left(p, jnp.uint32(16)), F32) for p in planes]
    hi = [pltpu.bitcast(jnp.bitwise_and(p, jnp.uint32(0xFFFF0000)), F32) for p in planes]
    return lo, hi


def _dot_split(a, b):
    a_hi, b_hi = a.astype(BF16), b.astype(BF16)
    a_lo = (a - a_hi.astype(F32)).astype(BF16)
    b_lo = (b - b_hi.astype(F32)).astype(BF16)
    return _dot(a_hi, b_hi) + (_dot(a_hi, b_lo) + _dot(a_lo, b_hi))


def _row_tile(rows, row0, candidates):
    return _pick_tile(math.gcd(rows, row0) if row0 else rows, candidates)


def _inproj_kernel(x_ref, nw_ref, w_ref, oret_ref, ogdn_ref, ogate_ref, oab_ref):
    h = (_rms(x_ref[...]) * nw_ref[...]).astype(BF16)
    oret_ref[...] = _dot(h, w_ref[:, :_RET_W]).astype(BF16)
    ogdn_ref[...] = _dot(h, w_ref[:, _RET_W:_AB_OFF]).astype(BF16)
    tail = _dot(h, w_ref[:, _AB_OFF:])
    oab_ref[...] = tail[:, :LANES]
    ogate_ref[...] = tail[:, _GATE_OFF - _AB_OFF:_GATE_OFF - _AB_OFF + 2 * D_MODEL].astype(BF16)


def _inproj(x, norm_w, wb):
    rows = x.shape[0]
    tm = _pick_tile(rows, (256, 128, 64, 32, 16, 8))
    row = lambda n: pl.BlockSpec((tm, n), lambda i: (i, 0))
    return pl.pallas_call(
        _inproj_kernel,
        grid=(rows // tm,),
        in_specs=[row(D_MODEL), _resident((1, D_MODEL)), _resident(wb.shape)],
        out_specs=[row(_RET_W), row(_GDN_W), row(2 * D_MODEL), row(LANES)],
        out_shape=[jax.ShapeDtypeStruct((rows, _RET_W), BF16), jax.ShapeDtypeStruct((rows, _GDN_W), BF16),
                   jax.ShapeDtypeStruct((rows, 2 * D_MODEL), BF16), jax.ShapeDtypeStruct((rows, LANES), F32)],
        compiler_params=_params(("parallel",)),
        name="inproj",
    )(x, norm_w.reshape(1, D_MODEL), wb)


def _ret_log_gamma():
    return np.log1p(-np.exp2(-5.0 - np.arange(RET_HEADS, dtype=np.float64)))


def _rope_tables(pos):
    half = RET_DK // 2
    inv = 1.0 / (ROPE_BASE ** (jnp.arange(half, dtype=F32) / half))
    ang = pos.astype(F32)[:, None] * inv[None, :]
    cos, sin = jnp.cos(ang), jnp.sin(ang)
    return jnp.concatenate([cos, cos], axis=-1), jnp.concatenate([-sin, sin], axis=-1)


def _rotary(x, cos, sin):
    return x * cos + pltpu.roll(x, RET_DK // 2, 1) * sin


def _ret_prompt_kernel(q_ref, k_ref, v_ref, g_ref, cos_ref, sin_ref, dmask_ref, qdec_ref, kdec_ref,
                       o_ref, s_ref, *, gammas):
    @pl.when(pl.program_id(1) == 0)
    def _():
        s_ref[...] = jnp.zeros_like(s_ref)

    c = RET_CHUNK
    chains = [(sq, h) for sq in range(q_ref.shape[0]) for h in range(RET_HEADS)]
    nc = range(len(chains))
    qk = [slice(h * RET_DK, (h + 1) * RET_DK) for _, h in chains]
    vv = [slice(h * RET_DV, (h + 1) * RET_DV) for _, h in chains]
    hd = [h for _, h in chains]
    sq = [s_ for s_, _ in chains]
    s = [s_ref[sq[i], hd[i]] for i in nc]
    for ck in range(q_ref.shape[1] // c):
        r = slice(ck * c, (ck + 1) * c)
        cos, sin = cos_ref[r, :], sin_ref[r, :]
        q = [_rotary(q_ref[sq[i], r, qk[i]].astype(F32), cos, sin) for i in nc]
        k = [_rotary(k_ref[sq[i], r, qk[i]].astype(F32), cos, sin) * (RET_DK ** -0.5) for i in nc]
        v = [v_ref[sq[i], r, vv[i]] for i in nc]
        qb = [x.astype(BF16) for x in q]
        inner = [_dot_nt(qb[i], k[i].astype(BF16)) * dmask_ref[hd[i]] for i in nc]
        cross = [_dot((q[i] * qdec_ref[hd[i]]).astype(BF16), s[i].astype(BF16)) for i in nc]
        upd = [_dot_tn((k[i] * kdec_ref[hd[i]]).astype(BF16), v[i]) for i in nc]
        o = [_dot(inner[i].astype(BF16), v[i]) + cross[i] for i in nc]
        s = [s[i] * gammas[hd[i]] + upd[i] for i in nc]
        for i in nc:
            o_ref[sq[i], r, vv[i]] = (_rms(o[i]) * _silu(g_ref[sq[i], r, vv[i]].astype(F32))).astype(BF16)
    for i in nc:
        s_ref[sq[i], hd[i]] = s[i]


def _ret_prompt(p_ret, batch, seq):
    c = RET_CHUNK
    step = RET_STEP_ROWS if seq % RET_STEP_ROWS == 0 else c
    n = seq // step
    lg = _ret_log_gamma()
    idx = np.arange(c, dtype=np.float64)
    diff = idx[:, None] - idx[None, :]
    dmask = np.where(diff >= 0, np.exp(np.maximum(diff, 0.0)[None] * lg[:, None, None]), 0.0)
    qdec = np.broadcast_to(np.exp((idx + 1.0)[None, :] * lg[:, None])[:, :, None], (RET_HEADS, c, RET_DK))
    kdec = np.broadcast_to(np.exp((c - 1.0 - idx)[None, :] * lg[:, None])[:, :, None], (RET_HEADS, c, RET_DK))
    gammas = tuple(float(g) for g in np.exp(c * lg))
    cos, sin = _rope_tables(jnp.arange(seq, dtype=jnp.int32))
    tab = lambda: _resident((RET_HEADS, c, RET_DK))
    nseq = RET_STEP_SEQS if batch % RET_STEP_SEQS == 0 else 1
    p3 = p_ret.reshape(batch, seq, _RET_W)
    o, s_new = pl.pallas_call(
        functools.partial(_ret_prompt_kernel, gammas=gammas),
        grid=(batch // nseq, n),
        in_specs=[pl.BlockSpec((nseq, step, RET_QK_W), lambda b, j: (b, j, 0)),
                  pl.BlockSpec((nseq, step, RET_QK_W), lambda b, j: (b, j, 1)),
                  pl.BlockSpec((nseq, step, RET_V_W), lambda b, j: (b, j, 1)),
                  pl.BlockSpec((nseq, step, RET_V_W), lambda b, j: (b, j, 2)),
                  pl.BlockSpec((step, RET_DK), lambda b, j: (j, 0)),
                  pl.BlockSpec((step, RET_DK), lambda b, j: (j, 0)),
                  tab(), tab(), tab()],
        out_specs=[pl.BlockSpec((nseq, step, RET_V_W), lambda b, j: (b, j, 0)),
                   pl.BlockSpec((nseq, RET_HEADS, RET_DK, RET_DV), lambda b, j: (b, 0, 0, 0))],
        out_shape=[jax.ShapeDtypeStruct((batch, seq, RET_V_W), BF16),
                   jax.ShapeDtypeStruct((batch, RET_HEADS, RET_DK, RET_DV), F32)],
        compiler_params=_params(("parallel", "arbitrary")),
        name="ret_prompt",
    )(p3, p3, p3, p3, cos, sin, jnp.asarray(dmask, F32), jnp.asarray(qdec, F32), jnp.asarray(kdec, F32))
    return o.reshape(batch * seq, RET_V_W), s_new


def _columns(x):
    n = x.shape[0]
    if n < LANES:
        x = jnp.concatenate([x, jnp.zeros((LANES - n, x.shape[1]), x.dtype)], axis=0)
    return x.T


def _ret_sample_kernel(p_ref, cos_ref, sin_ref, s_ref, o_ref, so_ref, *, gammas):
    cos, sin = cos_ref[...], sin_ref[...]
    nb = p_ref.shape[0]
    for h in range(RET_HEADS):
        qk = slice(h * RET_DK, (h + 1) * RET_DK)
        q = _rotary(p_ref[:, qk], cos, sin)
        k = _rotary(p_ref[:, RET_QK_W + h * RET_DK:RET_QK_W + (h + 1) * RET_DK], cos, sin) * (RET_DK ** -0.5)
        v = p_ref[:, 2 * RET_QK_W + h * RET_DV:2 * RET_QK_W + (h + 1) * RET_DV]
        g = p_ref[:, 2 * RET_QK_W + RET_V_W + h * RET_DV:2 * RET_QK_W + RET_V_W + (h + 1) * RET_DV]
        qk_dot = jnp.sum(q * k, axis=-1, keepdims=True)
        qt, kt = _columns(q), _columns(k)
        rows = []
        for j in range(nb):
            s = s_ref[j, h]
            qs = jnp.sum(qt[:, j:j + 1] * s, axis=0, keepdims=True)
            rows.append(qk_dot[j:j + 1] * v[j:j + 1] + gammas[h] * qs)
            so_ref[j, h] = s * gammas[h] + kt[:, j:j + 1] * v[j:j + 1]
        o = jnp.concatenate(rows, axis=0)
        o_ref[:, h * RET_DV:(h + 1) * RET_DV] = (_rms(o) * _silu(g)).astype(BF16)


def _ret_sample(p_ret_s, state):
    ts = p_ret_s.shape[0]
    sb = SAMPLE_TILE
    gammas = tuple(float(g) for g in np.exp(_ret_log_gamma()))
    cos, sin = _rope_tables(jnp.full((1,), PAST_LEN, jnp.int32))
    st = pl.BlockSpec((sb, RET_HEADS, RET_DK, RET_DV), lambda i: (i, 0, 0, 0))
    return pl.pallas_call(
        functools.partial(_ret_sample_kernel, gammas=gammas),
        grid=(ts // sb,),
        in_specs=[pl.BlockSpec((sb, _RET_W), lambda i: (i, 0)), _resident((1, RET_DK)), _resident((1, RET_DK)), st],
        out_specs=[pl.BlockSpec((sb, RET_V_W), lambda i: (i, 0)), st],
        out_shape=[jax.ShapeDtypeStruct((ts, RET_V_W), BF16), jax.ShapeDtypeStruct(state.shape, F32)],
        compiler_params=_params(("parallel",)),
        name="ret_sample",
    )(p_ret_s, cos, sin, state)


def _l2norm(x):
    return x * lax.rsqrt(jnp.sum(x * x, axis=-1, keepdims=True) + NORM_EPS)


def _bdot(a, b):
    return _dot(a.astype(BF16), b.astype(BF16))


def _chunk_masks(c):
    ri = lax.broadcasted_iota(jnp.int32, (c, c), 0)
    ci = lax.broadcasted_iota(jnp.int32, (c, c), 1)
    eye = (ri == ci).astype(F32)
    diag16 = (ri // 16 == ci // 16).astype(F32)
    low32 = jnp.logical_and(ri // 32 == ci // 32, ri // 16 > ci // 16).astype(F32)
    low64 = (ri // 32 > ci // 32).astype(F32)
    return ri >= ci, ri > ci, eye, diag16, low32, low64


def _unit_lower_inverse(a, eye, diag16, low32, low64):
    many = lambda f, *ls: [f(*args) for args in zip(*ls)]
    c = eye.shape[0]
    pair = lambda p, q, rhs: _bdot(jnp.concatenate([p, q], axis=0), rhs)
    n = [-(x * diag16) for x in a]
    n2 = many(_bdot, n, n)
    n34 = many(pair, n, n2, n2)
    x = [eye + p + q + r[:c] for p, q, r in zip(n, n2, n34)]
    n4 = [r[c:] for r in n34]
    xn = many(pair, x, n4, n4)
    x = [u + v[:c] for u, v in zip(x, xn)]
    x = many(lambda u, v: u + v, x, many(_bdot, x, [v[c:] for v in xn]))
    for mask in (low32, low64):
        r = many(_bdot, [y * mask for y in a], x)
        x = many(lambda u, v: u - v, x, many(_bdot, x, r))
    return x


def _gdn_gates(ab, alog, dtb):
    g = -jnp.exp(alog) * _softplus(ab + dtb)
    return g, _sigmoid(ab)


def _gdn_prep_kernel(x_ref, prev_ref, ab_ref, cw_ref, alog_ref, dtb_ref,
                     u_ref, w_ref, qg_ref, kg_ref, at_ref, eg_ref, cv_ref, act_ref):
    c = GDN_CHUNK
    rows = x_ref.shape[0]
    x = x_ref[...]
    halo = prev_ref.shape[0]
    prev = jnp.where(pl.program_id(1) == 0, jnp.zeros_like(prev_ref), prev_ref[...])
    xcat = jnp.concatenate([prev, x], axis=0)
    ti = lax.broadcasted_iota(jnp.int32, (rows, rows + halo), 0)
    ui = lax.broadcasted_iota(jnp.int32, (rows, rows + halo), 1)
    xf = x.astype(F32)
    acc = xf * cw_ref[CONV_W - 1:CONV_W, :]
    for i in range(CONV_W - 1):
        shift = (ui == ti + (halo - (CONV_W - 1) + i)).astype(BF16)
        acc = acc + _dot(shift, xcat) * cw_ref[i:i + 1, :]
    cv_ref[0] = xf[rows - (CONV_W - 1):, :]
    act_ref[...] = _silu(acc)

    g_all, beta_all = _gdn_gates(ab_ref[...], alog_ref[...], dtb_ref[...])
    lower, strict, eye, diag16, low32, low64 = _chunk_masks(c)
    gc_all, gr_all = [], []
    for ck in range(rows // c):
        gc_ck = _dot_hi(lower.astype(F32), g_all[ck * c:(ck + 1) * c])
        gc_all.append(gc_ck)
        gr_all.append(_columns(gc_ck))
        eg_ref[ck] = jnp.exp(gc_ck[c - 1:c, :])
    chains = [(ck, h) for ck in range(rows // c) for h in range(GDN_HEADS)]
    rs = [slice(ck * c, (ck + 1) * c) for ck, _ in chains]
    sl = [slice(h * GDN_DK, (h + 1) * GDN_DK) for _, h in chains]
    nc = range(len(chains))
    q = [_l2norm(act_ref[rs[i], sl[i]]) * (GDN_DK ** -0.5) for i in nc]
    k = [_l2norm(act_ref[rs[i], GDN_QK_W + sl[i].start:GDN_QK_W + sl[i].stop]) for i in nc]
    v = [act_ref[rs[i], 2 * GDN_QK_W + sl[i].start:2 * GDN_QK_W + sl[i].stop] for i in nc]
    beta = [beta_all[rs[i], GDN_HEADS + h:GDN_HEADS + h + 1] for i, (_, h) in enumerate(chains)]
    gc = [gc_all[ck][:, h:h + 1] for ck, h in chains]
    gr = [gr_all[ck][h:h + 1, :c] for ck, h in chains]
    decay = [jnp.exp(jnp.where(lower, gc[i] - gr[i], -jnp.inf)) for i in nc]
    exp_g = [jnp.exp(x) for x in gc]
    kb = [k[i] * beta[i] for i in nc]
    kbf = [x.astype(BF16) for x in k]
    kq = [_dot_nt(jnp.concatenate([kb[i].astype(BF16), q[i].astype(BF16)], axis=0), kbf[i]) for i in nc]
    a = [kq[i][:c] * jnp.where(strict, decay[i], 0.0) for i in nc]
    attn = [kq[i][c:] * decay[i] for i in nc]
    t = _unit_lower_inverse(a, eye, diag16, low32, low64)
    uw = [_bdot(t[i], jnp.concatenate([v[i] * beta[i], kb[i] * exp_g[i]], axis=1)) for i in nc]
    for i in nc:
        u_ref[rs[i], sl[i]] = uw[i][:, :GDN_DV]
        w_ref[rs[i], sl[i]] = uw[i][:, GDN_DV:].astype(BF16)
        at_ref[rs[i], sl[i]] = jnp.concatenate([attn[i], jnp.zeros((c, GDN_DK - c), F32)], axis=1).astype(BF16)
        qg_ref[rs[i], sl[i]] = (q[i] * exp_g[i]).astype(BF16)
        kg_ref[rs[i], sl[i]] = (k[i] * jnp.exp(gc[i][c - 1:c, :] - gc[i])).astype(BF16)


def _gdn_scan_kernel(u_ref, w_ref, qg_ref, kg_ref, at_ref, eg_ref, z_ref, nw_ref, o_ref, s_ref):
    c = GDN_CHUNK

    @pl.when(pl.program_id(1) == 0)
    def _():
        s_ref[...] = jnp.zeros_like(s_ref)

    nw = nw_ref[...]
    chains = [(q, h) for q in range(u_ref.shape[0]) for h in range(GDN_HEADS)]
    sl = [slice(h * GDN_DK, (h + 1) * GDN_DK) for _, h in chains]
    nc = range(len(chains))
    s = [s_ref[q, h] for q, h in chains]
    for ck in range(u_ref.shape[1] // c):
        r = slice(ck * c, (ck + 1) * c)
        sb = [x.astype(BF16) for x in s]
        ws = [_dot(w_ref[q, r, sl[i]], sb[i]) for i, (q, _) in enumerate(chains)]
        qs = [_dot(qg_ref[q, r, sl[i]], sb[i]) for i, (q, _) in enumerate(chains)]
        vnb = [(u_ref[q, r, sl[i]] - ws[i]).astype(BF16) for i, (q, _) in enumerate(chains)]
        o = [qs[i] + _dot(at_ref[q, r, h * GDN_DK:h * GDN_DK + c], vnb[i]) for i, (q, h) in enumerate(chains)]
        s = [s[i] * eg_ref[q, ck][:, h:h + 1] + _dot_tn(kg_ref[q, r, sl[i]], vnb[i])
             for i, (q, h) in enumerate(chains)]
        for i, (q, _) in enumerate(chains):
            o_ref[q, r, sl[i]] = (_rms(o[i]) * nw * _silu(z_ref[q, r, sl[i]].astype(F32))).astype(BF16)
    for i, (q, h) in enumerate(chains):
        s_ref[q, h] = s[i]


def _gdn_prompt(p_gdn, p_ab, batch, seq, conv_w, a_log, dt_bias, gdn_norm_w):
    c = GDN_CHUNK
    rows = GDN_PREP_ROWS
    t = batch * seq
    nt = seq // rows
    alog = jnp.pad(a_log.astype(F32), (0, LANES - GDN_HEADS)).reshape(1, LANES)
    dtb = jnp.pad(dt_bias.astype(F32), (0, LANES - GDN_HEADS)).reshape(1, LANES)
    wide = lambda: pl.BlockSpec((rows, GDN_V_W), lambda b, j: (b * nt + j, 0))
    u, w, qg, kg, at, eg, conv_new = pl.pallas_call(
        _gdn_prep_kernel,
        grid=(batch, nt),
        in_specs=[pl.BlockSpec((rows, CONV_CH), lambda b, j: (b * nt + j, 0)),
                  pl.BlockSpec((16, CONV_CH), lambda b, j: (jnp.maximum((b * nt + j) * (rows // 16) - 1, 0), 0)),
                  pl.BlockSpec((rows, LANES), lambda b, j: (b * nt + j, 0)),
                  _resident((CONV_W, CONV_CH)), _resident((1, LANES)), _resident((1, LANES))],
        out_specs=[wide(), wide(), wide(), wide(), wide(),
                   pl.BlockSpec((rows // c, 1, LANES), lambda b, j: (b * nt + j, 0, 0)),
                   pl.BlockSpec((1, CONV_W - 1, CONV_CH), lambda b, j: (b, 0, 0))],
        out_shape=[jax.ShapeDtypeStruct((t, GDN_V_W), F32)] + [jax.ShapeDtypeStruct((t, GDN_V_W), BF16)] * 4
        + [jax.ShapeDtypeStruct((t // c, 1, LANES), F32),
           jax.ShapeDtypeStruct((batch, CONV_W - 1, CONV_CH), F32)],
        scratch_shapes=[pltpu.VMEM((rows, CONV_CH), F32)],
        compiler_params=_params(("parallel", "arbitrary")),
        name="gdn_prep",
    )(p_gdn, p_gdn, p_ab, conv_w.astype(F32), alog, dtb)
    srows = GDN_SCAN_ROWS
    n = seq // srows
    nq = CONV_CH // GDN_V_W
    nseq = GDN_SCAN_SEQS if batch % GDN_SCAN_SEQS == 0 else 1
    per_seq = lambda a: a.reshape((batch, seq // (t // a.shape[0])) + a.shape[1:])
    blk = lambda: pl.BlockSpec((nseq, srows, GDN_V_W), lambda b, j: (b, j, 0))
    o, s_new = pl.pallas_call(
        _gdn_scan_kernel,
        grid=(batch // nseq, n),
        in_specs=[blk(), blk(), blk(), blk(), blk(),
                  pl.BlockSpec((nseq, srows // c, 1, LANES), lambda b, j: (b, j, 0, 0)),
                  pl.BlockSpec((nseq, srows, GDN_V_W), lambda b, j: (b, j, nq)),
                  _resident((1, GDN_DV))],
        out_specs=[blk(), pl.BlockSpec((nseq, GDN_HEADS, GDN_DK, GDN_DV), lambda b, j: (b, 0, 0, 0))],
        out_shape=[jax.ShapeDtypeStruct((batch, seq, GDN_V_W), BF16),
                   jax.ShapeDtypeStruct((batch, GDN_HEADS, GDN_DK, GDN_DV), F32)],
        compiler_params=_params(("parallel", "arbitrary")),
        name="gdn_scan",
    )(per_seq(u), per_seq(w), per_seq(qg), per_seq(kg), per_seq(at), per_seq(eg), per_seq(p_gdn),
      gdn_norm_w.astype(F32).reshape(1, GDN_DV))
    return o.reshape(t, GDN_V_W), s_new, conv_new


def _gdn_sample_kernel(x_ref, ab_ref, sc_ref, cw_ref, alog_ref, dtb_ref, nw_ref, s_ref, o_ref, so_ref, sco_ref):
    nb = x_ref.shape[0]
    x = x_ref[:, :CONV_CH]
    acc = x * cw_ref[CONV_W - 1:CONV_W, :]
    for i in range(CONV_W - 1):
        acc = acc + sc_ref[i] * cw_ref[i:i + 1, :]
    for i in range(CONV_W - 2):
        sco_ref[i] = sc_ref[i + 1]
    sco_ref[CONV_W - 2] = x
    u = _silu(acc)
    g_all, beta_all = _gdn_gates(ab_ref[...], alog_ref[...], dtb_ref[...])
    eg_all = jnp.exp(g_all)
    nw = nw_ref[...]
    for h in range(GDN_HEADS):
        sl = slice(h * GDN_DK, (h + 1) * GDN_DK)
        q = _l2norm(u[:, sl]) * (GDN_DK ** -0.5)
        k = _l2norm(u[:, GDN_QK_W + h * GDN_DK:GDN_QK_W + (h + 1) * GDN_DK])
        v = u[:, 2 * GDN_QK_W + h * GDN_DV:2 * GDN_QK_W + (h + 1) * GDN_DV]
        beta = beta_all[:, GDN_HEADS + h:GDN_HEADS + h + 1]
        eg = eg_all[:, h:h + 1]
        qk_dot = jnp.sum(q * k, axis=-1, keepdims=True)
        qt, kt = _columns(q), _columns(k)
        rows = []
        for j in range(nb):
            s = s_ref[j, h]
            kcol = kt[:, j:j + 1]
            ks = jnp.sum(kcol * s, axis=0, keepdims=True)
            qs = jnp.sum(qt[:, j:j + 1] * s, axis=0, keepdims=True)
            ej = eg[j:j + 1]
            v_new = beta[j:j + 1] * (v[j:j + 1] - ej * ks)
            rows.append(ej * qs + qk_dot[j:j + 1] * v_new)
            so_ref[j, h] = s * ej + kcol * v_new
        o = jnp.concatenate(rows, axis=0)
        z = x_ref[:, CONV_CH + h * GDN_DV:CONV_CH + (h + 1) * GDN_DV]
        o_ref[:, sl] = (_rms(o) * nw * _silu(z)).astype(BF16)


def _gdn_sample(p_gdn_s, p_ab_s, state, conv_state, conv_w, a_log, dt_bias, gdn_norm_w):
    ts = p_gdn_s.shape[0]
    sb = SAMPLE_TILE
    alog = jnp.pad(a_log.astype(F32), (0, LANES - GDN_HEADS)).reshape(1, LANES)
    dtb = jnp.pad(dt_bias.astype(F32), (0, LANES - GDN_HEADS)).reshape(1, LANES)
    sc = jnp.swapaxes(conv_state.astype(F32), 0, 1)
    st = pl.BlockSpec((sb, GDN_HEADS, GDN_DK, GDN_DV), lambda i: (i, 0, 0, 0))
    scs = pl.BlockSpec((CONV_W - 1, sb, CONV_CH), lambda i: (0, i, 0))
    o, s_new, sc_new = pl.pallas_call(
        _gdn_sample_kernel,
        grid=(ts // sb,),
        in_specs=[pl.BlockSpec((sb, _GDN_W), lambda i: (i, 0)), pl.BlockSpec((sb, LANES), lambda i: (i, 0)), scs,
                  _resident((CONV_W, CONV_CH)), _resident((1, LANES)), _resident((1, LANES)),
                  _resident((1, GDN_DV)), st],
        out_specs=[pl.BlockSpec((sb, GDN_V_W), lambda i: (i, 0)), st, scs],
        out_shape=[jax.ShapeDtypeStruct((ts, GDN_V_W), BF16), jax.ShapeDtypeStruct(state.shape, F32),
                   jax.ShapeDtypeStruct(sc.shape, F32)],
        compiler_params=_params(("parallel",)),
        name="gdn_sample",
    )(p_gdn_s, p_ab_s, sc, conv_w.astype(F32), alog, dtb, gdn_norm_w.astype(F32).reshape(1, GDN_DV), state)
    return o, s_new, jnp.swapaxes(sc_new, 0, 1)


_NO_EXPERT = -1e30


def _merge_kernel(oa0_ref, ob0_ref, gate0_ref, x0_ref, oa1_ref, ob1_ref, gate1_ref, x1in_ref,
                  wa_ref, wb_ref, wo_ref, nw_ref, wr_ref, br_ref, x1_ref, h2_ref, ti_ref, tw_ref, *, n_first):
    first = pl.program_id(0) < n_first
    pick = lambda a, b: jnp.where(first, a[...], b[...])
    gate = pick(gate0_ref, gate1_ref)
    ya = _dot(pick(oa0_ref, oa1_ref), wa_ref[...])
    yb = _dot(pick(ob0_ref, ob1_ref), wb_ref[...])
    ga = gate[:, :D_MODEL].astype(F32)
    gb = gate[:, D_MODEL:].astype(F32)
    m = _sigmoid(ga) * ya + _sigmoid(gb) * yb
    x1 = pick(x0_ref, x1in_ref) + _dot(m.astype(BF16), wo_ref[...])
    x1_ref[...] = x1
    h2 = _rms(x1) * nw_ref[...]
    for j, plane in enumerate(_pack_rows(h2)):
        h2_ref[:, j, :] = plane
    lg = _dot_split(h2, wr_ref[...]) + br_ref[...]
    lane = lax.broadcasted_iota(jnp.int32, lg.shape, 1).astype(F32)
    vals, idxs = [], []
    for _ in range(TOP_K):
        top = jnp.max(lg, axis=-1, keepdims=True)
        idx = jnp.min(jnp.where(lg == top, lane, float(LANES)), axis=-1, keepdims=True)
        vals.append(top)
        idxs.append(idx)
        lg = jnp.where(lane == idx, _NO_EXPERT, lg)
    es = [jnp.exp(v - vals[0]) for v in vals]
    inv_total = 1.0 / functools.reduce(lambda a, b: a + b, es)
    ti = jnp.zeros_like(lg)
    tw = jnp.zeros_like(lg)
    for k in range(TOP_K):
        ti = jnp.where(lane == float(k), idxs[k], ti)
        tw = jnp.where(lane == float(k), es[k] * inv_total, tw)
    ti_ref[...] = ti.astype(jnp.int32)
    tw_ref[...] = tw


def _merge(group0, group1, weights):
    r0 = group0[3].shape[0]
    tm = _pick_tile(r0, (MERGE_ROWS, 128, 64, 32, 16, 8))
    pad = -group1[3].shape[0] % tm
    group1 = tuple(jnp.pad(a, ((0, pad), (0, 0))) for a in group1)
    r1 = group1[3].shape[0]
    n0, total = r0 // tm, r0 + r1
    widths = (RET_V_W, GDN_V_W, 2 * D_MODEL, D_MODEL)
    specs0 = [pl.BlockSpec((tm, n), lambda i: (jnp.minimum(i, n0 - 1), 0)) for n in widths]
    specs1 = [pl.BlockSpec((tm, n), lambda i: (jnp.maximum(i - n0, 0), 0)) for n in widths]
    out = lambda n: pl.BlockSpec((tm, n), lambda i: (i, 0))
    sub = PACK_PLANES
    sq = (D_MODEL, D_MODEL)
    return pl.pallas_call(
        functools.partial(_merge_kernel, n_first=n0),
        grid=(total // tm,),
        in_specs=specs0 + specs1 + [_resident(sq), _resident(sq), _resident(sq), _resident((1, D_MODEL)),
                                    _resident((D_MODEL, LANES)), _resident((1, LANES))],
        out_specs=[out(D_MODEL), pl.BlockSpec((tm, sub, LANES), lambda i: (i, 0, 0)), out(LANES), out(LANES)],
        out_shape=[jax.ShapeDtypeStruct((total, D_MODEL), F32), jax.ShapeDtypeStruct((total, sub, LANES), jnp.uint32),
                   jax.ShapeDtypeStruct((total, LANES), jnp.int32), jax.ShapeDtypeStruct((total, LANES), F32)],
        compiler_params=_params(("parallel",)),
        name="merge",
    )(*group0, *group1, *weights)


def _merge_weights(w_a, w_b, w_o, ffn_norm_w, w_router, b_router):
    wr = jnp.pad(w_router.astype(F32), ((0, 0), (0, LANES - N_EXPERTS)))
    br = jnp.pad(b_router.astype(F32), (0, LANES - N_EXPERTS), constant_values=_NO_EXPERT).reshape(1, LANES)
    return (w_a.astype(BF16), w_b.astype(BF16), w_o.astype(BF16), ffn_norm_w.astype(F32).reshape(1, D_MODEL), wr, br)


def _route(top_i, gate, n_tokens):
    rows = MOE_ROWS
    n = n_tokens * TOP_K
    flat_e = top_i.reshape(n).astype(jnp.int32)
    bits = max(1, (n - 1).bit_length())
    assert bits + (N_EXPERTS - 1).bit_length() <= 31
    order = lax.sort((flat_e << bits) | jnp.arange(n, dtype=jnp.int32), is_stable=False) & ((1 << bits) - 1)
    counts = jnp.sum((flat_e[:, None] == jnp.arange(N_EXPERTS, dtype=jnp.int32)[None, :]).astype(jnp.int32), axis=0)
    start = jnp.cumsum(counts) - counts
    pcounts = (counts + rows - 1) // rows * rows
    pend = jnp.cumsum(pcounts)
    pstart = pend - pcounts
    nb = -(-n // rows) + N_EXPERTS
    blk = jnp.arange(nb, dtype=jnp.int32)
    block_e = jnp.minimum(jnp.sum((pend[None, :] <= (blk * rows)[:, None]).astype(jnp.int32), axis=1),
                          N_EXPERTS - 1).astype(jnp.int32)
    nb_used = (pend[-1] // rows).astype(jnp.int32).reshape(1)
    is_e = block_e[:, None] == jnp.arange(N_EXPERTS, dtype=jnp.int32)[None, :]
    of_block = lambda table: jnp.sum(jnp.where(is_e, table[None, :], 0), axis=1)
    within = (blk * rows - of_block(pstart))[:, None] + jnp.arange(rows, dtype=jnp.int32)[None, :]
    valid = jnp.logical_and(within < of_block(counts)[:, None], (blk < nb_used[0])[:, None])
    flat = order[jnp.clip(of_block(start)[:, None] + within, 0, n - 1)]
    spare = n + (blk % MOE_RING)[:, None] * rows + jnp.arange(rows, dtype=jnp.int32)[None, :]
    src = jnp.where(valid, flat // TOP_K, 0)
    dst = jnp.where(valid, flat, spare)
    ahead = lambda k: jnp.concatenate([src[k:]] + [src[-1:]] * k, axis=0)
    slab = jnp.concatenate([src, dst, ahead(1), ahead(2)], axis=1)
    row_w = jnp.where(valid, gate[:, :TOP_K].reshape(n)[flat], 0.0)
    row_w = jnp.broadcast_to(row_w[:, :, None], (nb, rows, LANES))
    return block_e, nb_used, slab, row_w


def _expert_kernel(be_ref, nbu_ref, slab_ref, h_ref, roww_ref, wgu_ref, bgu_ref, wd_ref, bd_ref, y_ref,
                   idx0_ref, idx1_ref, idx2_ref, xbuf_ref, ybuf_ref, wgu_bf_ref, wd_bf_ref, isem, gsem, ssem):
    rows = MOE_ROWS
    ring = MOE_RING
    idx_refs = (idx0_ref, idx1_ref, idx2_ref)
    assert len(idx_refs) == ring
    sub = PACK_PLANES
    i = pl.program_id(0)
    nbu = nbu_ref[0]
    slot = i % ring
    active = i < nbu
    n_real = y_ref.shape[0] - ring * rows

    def slab_copy(blk, sl):
        return pltpu.make_async_copy(slab_ref.at[blk], idx_refs[sl], isem.at[sl])

    def gather_row(tok, sl, r):
        return pltpu.make_async_copy(h_ref.at[tok], xbuf_ref.at[sl, :, r, :], gsem.at[sl])

    def scatter_row(sl, r, f):
        return pltpu.make_async_copy(ybuf_ref.at[sl, :, r, :], y_ref.at[f], ssem.at[sl])

    def gather_wait(sl):
        pltpu.make_async_copy(xbuf_ref.at[sl], xbuf_ref.at[sl], gsem.at[sl]).wait()

    def scatter_wait(sl):
        pltpu.make_async_copy(ybuf_ref.at[sl], ybuf_ref.at[sl], ssem.at[sl]).wait()

    @pl.when(i == 0)
    def _():
        slab_copy(0, 0).start()
        ybuf_ref[...] = jnp.zeros_like(ybuf_ref)
        for sl in range(ring):
            def fill(r, carry, sl=sl):
                scatter_row(sl, r, n_real + sl * rows + r).start()
                return carry
            lax.fori_loop(0, rows, fill, 0)

    def block(sl):
        nxt, nxt2 = (sl + 1) % ring, (sl + 2) % ring
        idx_ref = idx_refs[sl]
        slab_copy(i, sl).wait()

        @pl.when(i + 1 < nbu)
        def _():
            slab_copy(i + 1, nxt).start()

        if sl == 0:
            @pl.when(i == 0)
            def _():
                def first(r, carry):
                    gather_row(idx_ref[r], 0, r).start()
                    gather_row(idx_ref[2 * rows + r], 1, r).start()
                    return carry
                lax.fori_loop(0, rows, first, 0)

        changed = jnp.logical_or(i == 0, be_ref[i] != be_ref[jnp.maximum(i - 1, 0)])

        @pl.when(changed)
        def _():
            wgu_bf_ref[...] = wgu_ref[0].astype(BF16)
            wd_bf_ref[...] = wd_ref[0].astype(BF16)

        gather_wait(sl)
        scatter_wait(sl)
        lo, hi = _unpack_planes([xbuf_ref[sl, j] for j in range(sub)])
        xb = jnp.concatenate(lo + hi, axis=1).astype(BF16)
        hb = _dot(xb, wgu_bf_ref[...]) + bgu_ref[0]
        for r in range(rows):
            gather_row(idx_ref[3 * rows + r], nxt2, r).start(priority=r % 2)
        glu = jnp.minimum(hb[:, :D_FF], SWIGLU_LIMIT)
        lin = jnp.clip(hb[:, D_FF:], -SWIGLU_LIMIT, SWIGLU_LIMIT)
        act = (glu * _sigmoid(SWIGLU_ALPHA * glu) * (lin + 1.0)).astype(BF16)
        yv = (_dot(act, wd_bf_ref[...]) + bd_ref[0]) * roww_ref[0, :, 0:1]
        for j, plane in enumerate(_pack_rows(yv)):
            ybuf_ref[sl, j] = plane
        for r in range(rows):
            scatter_row(sl, r, idx_ref[rows + r]).start(priority=(r + 1) % 2)

        @pl.when(i == nbu - 1)
        def _():
            for s in (sl, nxt, nxt2):
                scatter_wait(s)
            gather_wait(nxt)
            gather_wait(nxt2)

    for sl in range(ring):
        pl.when(jnp.logical_and(active, slot == sl))(functools.partial(block, sl))


def _experts(h2, t, block_e, nb_used, slab, row_w, w_gate_up, b_gate_up, w_down, b_down):
    rows = MOE_ROWS
    ring = MOE_RING
    assert (ring * rows) % TOP_K == 0
    nb = slab.shape[0]
    sub = PACK_PLANES
    grid_spec = pltpu.PrefetchScalarGridSpec(
        num_scalar_prefetch=2,
        grid=(nb,),
        in_specs=[pl.BlockSpec(memory_space=pl.ANY),
                  pl.BlockSpec(memory_space=pl.ANY),
                  pl.BlockSpec((1, rows, LANES), lambda i, be, nbu: (i, 0, 0)),
                  pl.BlockSpec((1, D_MODEL, 2 * D_FF), lambda i, be, nbu: (be[i], 0, 0)),
                  pl.BlockSpec((1, 1, 2 * D_FF), lambda i, be, nbu: (be[i], 0, 0)),
                  pl.BlockSpec((1, D_FF, D_MODEL), lambda i, be, nbu: (be[i], 0, 0)),
                  pl.BlockSpec((1, 1, D_MODEL), lambda i, be, nbu: (be[i], 0, 0))],
        out_specs=pl.BlockSpec(memory_space=pl.ANY),
        scratch_shapes=[pltpu.SMEM((slab.shape[1],), jnp.int32)] * ring + [
                        pltpu.VMEM((ring, sub, rows, LANES), jnp.uint32),
                        pltpu.VMEM((ring, sub, rows, LANES), jnp.uint32),
                        pltpu.VMEM((D_MODEL, 2 * D_FF), BF16),
                        pltpu.VMEM((D_FF, D_MODEL), BF16),
                        pltpu.SemaphoreType.DMA((ring,)),
                        pltpu.SemaphoreType.DMA((ring,)),
                        pltpu.SemaphoreType.DMA((ring,))])
    return pl.pallas_call(
        _expert_kernel,
        grid_spec=grid_spec,
        out_shape=jax.ShapeDtypeStruct((t * TOP_K + ring * rows, sub, LANES), jnp.uint32),
        compiler_params=_params(("arbitrary",)),
        name="experts",
    )(block_e, nb_used, slab, h2, row_w, w_gate_up, b_gate_up.reshape(N_EXPERTS, 1, 2 * D_FF), w_down,
      b_down.reshape(N_EXPERTS, 1, D_MODEL))


def _combine_kernel(y_ref, x1_ref, nw_ref, o_ref, sum_ref, *, final):
    lo, hi = None, None
    for k in range(TOP_K):
        (l,), (h,) = _unpack_planes([y_ref[:, k]])
        lo, hi = (l, h) if lo is None else (lo + l, hi + h)
    sum_ref[:, :PACK_PLANES, :] = lo
    sum_ref[:, PACK_PLANES:, :] = hi
    acc = x1_ref[...] + jnp.concatenate([sum_ref[:, j, :] for j in range(D_MODEL // LANES)], axis=1)
    o_ref[...] = _rms(acc) * nw_ref[...] if final else acc


def _combine(y, x1, row0, rows, norm_w, final):
    tm = _row_tile(rows, row0, (256, 128, 64, 32, 16, 8))
    off = row0 // tm
    sub = PACK_PLANES
    y4 = y.reshape(y.shape[0] // TOP_K, TOP_K, sub, LANES)
    return pl.pallas_call(
        functools.partial(_combine_kernel, final=final),
        grid=(rows // tm,),
        in_specs=[pl.BlockSpec((tm, TOP_K, sub, LANES), lambda i: (off + i, 0, 0, 0)),
                  pl.BlockSpec((tm, D_MODEL), lambda i: (off + i, 0)), _resident((1, D_MODEL))],
        out_specs=pl.BlockSpec((tm, D_MODEL), lambda i: (i, 0)),
        out_shape=jax.ShapeDtypeStruct((rows, D_MODEL), F32),
        scratch_shapes=[pltpu.VMEM((tm, D_MODEL // LANES, LANES), F32)],
        compiler_params=_params(("parallel",)),
        name="combine",
    )(y4, x1, norm_w.astype(F32).reshape(1, D_MODEL))


def kernel(x_prompt, x_sample, state_ret, state_gdn, state_conv, attn_norm_w, w_in, conv_w, a_log, dt_bias, gdn_norm_w, w_branch_a, w_branch_b, w_out, ffn_norm_w, w_router, b_router, w_gate_up, b_gate_up, w_down, b_down, final_norm_w):
    bp, lp, d = x_prompt.shape
    bs, ls, _ = x_sample.shape
    assert ls == 1 and d == D_MODEL and lp % RET_CHUNK == 0 and bs % SAMPLE_TILE == 0
    depth = w_in.shape[0]
    tp = bp * lp
    t = tp + bs
    xp, xs = x_prompt.reshape(tp, d).astype(F32), x_sample.reshape(bs, d).astype(F32)
    rp, gp, cp, rs, gs, cs = [], [], [], [], [], []
    for l in range(depth):
        wb = jnp.pad(w_in[l], ((0, 0), (0, -w_in.shape[2] % LANES))).astype(BF16)
        pp_ret, pp_gdn, pp_gate, pp_ab = _inproj(xp, attn_norm_w[l], wb)
        ps_ret, ps_gdn, ps_gate, ps_ab = _inproj(xs, attn_norm_w[l], wb)
        op_ret, s_ret_p = _ret_prompt(pp_ret, bp, lp)
        os_ret, s_ret_s = _ret_sample(ps_ret.astype(F32), state_ret[l].astype(F32))
        op_gdn, s_gdn_p, conv_p = _gdn_prompt(pp_gdn, pp_ab, bp, lp, conv_w[l], a_log[l], dt_bias[l], gdn_norm_w[l])
        os_gdn, s_gdn_s, conv_s = _gdn_sample(ps_gdn.astype(F32), ps_ab, state_gdn[l].astype(F32), state_conv[l],
                                              conv_w[l], a_log[l], dt_bias[l], gdn_norm_w[l])
        mw = _merge_weights(w_branch_a[l], w_branch_b[l], w_out[l], ffn_norm_w[l], w_router[l], b_router[l])
        x1, h2, top_i, gate = _merge((op_ret, op_gdn, pp_gate, xp), (os_ret, os_gdn, ps_gate, xs), mw)
        block_e, nb_used, slab, row_w = _route(top_i[:t, :TOP_K], gate[:t], t)
        y = _experts(h2, t, block_e, nb_used, slab, row_w, w_gate_up[l], b_gate_up[l], w_down[l], b_down[l])
        last = l == depth - 1
        norm_w = final_norm_w if last else jnp.ones((d,), F32)
        xp = _combine(y, x1, 0, tp, norm_w, last)
        xs = _combine(y, x1, tp, bs, norm_w, last)
        rp.append(s_ret_p); gp.append(s_gdn_p); cp.append(conv_p)
        rs.append(s_ret_s); gs.append(s_gdn_s); cs.append(conv_s)
    y_prompt = xp.reshape(bp, lp, d).astype(x_prompt.dtype)
    y_sample = xs.reshape(bs, ls, d).astype(x_sample.dtype)
    return (y_prompt, y_sample,
            jnp.stack(rp).astype(state_ret.dtype), jnp.stack(gp).astype(state_gdn.dtype),
            jnp.stack(cp).astype(state_conv.dtype),
            jnp.stack(rs).astype(state_ret.dtype), jnp.stack(gs).astype(state_gdn.dtype),
            jnp.stack(cs).astype(state_conv.dtype))
```

```python
import functools
import math

import numpy as np
import jax
import jax.numpy as jnp
from jax import lax
from jax.experimental import pallas as pl
from jax.experimental.pallas import tpu as pltpu

F32 = jnp.float32
BF16 = jnp.bfloat16
HIGHEST = lax.Precision.HIGHEST

D_MODEL = 1024
PAST_LEN = 16384
RET_HEADS, RET_DK, RET_DV = 4, 128, 256
RET_QK_W, RET_V_W = RET_HEADS * RET_DK, RET_HEADS * RET_DV
RET_CHUNK = 128
ROPE_BASE = 10000.0
GDN_HEADS, GDN_DK, GDN_DV = 8, 128, 128
GDN_QK_W, GDN_V_W = GDN_HEADS * GDN_DK, GDN_HEADS * GDN_DV
GDN_CHUNK = 64
CONV_W = 4
CONV_CH = 2 * GDN_QK_W + GDN_V_W
N_EXPERTS = 32
TOP_K = 4
D_FF = D_MODEL
SWIGLU_LIMIT = 7.0
SWIGLU_ALPHA = 1.702
NORM_EPS = 1e-6

_RET_W = 2 * RET_QK_W + 2 * RET_V_W
_GDN_W = CONV_CH + GDN_V_W
_AB_OFF = _RET_W + _GDN_W
_GATE_OFF = _AB_OFF + 2 * GDN_HEADS

LANES = 128
VMEM_LIMIT = 56 * 1024 * 1024
MERGE_ROWS = 512
MOE_ROWS = 256
MOE_RING = 3
SAMPLE_TILE = 8
RET_STEP_ROWS = 256
RET_STEP_SEQS = 4
GDN_PREP_ROWS = 128
GDN_SCAN_ROWS = 256
GDN_SCAN_SEQS = 4


def _pick_tile(n, candidates):
    for c in candidates:
        if n % c == 0:
            return c
    raise ValueError(f"no tile in {candidates} divides {n}")


def _params(sem, vmem=VMEM_LIMIT):
    return pltpu.CompilerParams(dimension_semantics=sem, vmem_limit_bytes=vmem)


def _resident(shape):
    nd = len(shape)
    return pl.BlockSpec(shape, lambda *_: (0,) * nd, pipeline_mode=pl.Buffered(1))


def _silu(x):
    return x * (1.0 / (1.0 + jnp.exp(-x)))


def _sigmoid(x):
    return 1.0 / (1.0 + jnp.exp(-x))


def _softplus(x):
    return jnp.maximum(x, 0.0) + jnp.log1p(jnp.exp(-jnp.abs(x)))


def _rms(x):
    return x * lax.rsqrt(jnp.mean(x * x, axis=-1, keepdims=True) + NORM_EPS)


def _dot(a, b):
    return jnp.dot(a, b, preferred_element_type=F32)


def _dot_nt(a, b):
    return lax.dot_general(a, b, (((1,), (1,)), ((), ())), preferred_element_type=F32)


def _dot_tn(a, b):
    return lax.dot_general(a, b, (((0,), (0,)), ((), ())), preferred_element_type=F32)


def _dot_hi(a, b):
    return jnp.dot(a, b, preferred_element_type=F32, precision=HIGHEST)


PACK_PLANES = D_MODEL // (2 * LANES)


def _pack_rows(x):
    as_bits = lambda v: pltpu.bitcast(v.astype(BF16).astype(F32), jnp.uint32)
    planes = []
    for j in range(PACK_PLANES):
        lo = as_bits(x[:, j * LANES:(j + 1) * LANES])
        hi = as_bits(x[:, (j + PACK_PLANES) * LANES:(j + PACK_PLANES + 1) * LANES])
        planes.append(jnp.bitwise_or(hi, lax.shift_right_logical(lo, jnp.uint32(16))))
    return planes


def _unpack_planes(planes):
    lo = [pltpu.bitcast(lax.shift_left(p, jnp.uint32(16)), F32) for p in planes]
    hi = [pltpu.bitcast(jnp.bitwise_and(p, jnp.uint32(0xFFFF0000)), F32) for p in planes]
    return lo, hi


def _dot_split(a, b):
    a_hi, b_hi = a.astype(BF16), b.astype(BF16)
    a_lo = (a - a_hi.astype(F32)).astype(BF16)
    b_lo = (b - b_hi.astype(F32)).astype(BF16)
    return _dot(a_hi, b_hi) + (_dot(a_hi, b_lo) + _dot(a_lo, b_hi))


def _row_tile(rows, row0, candidates):
    return _pick_tile(math.gcd(rows, row0) if row0 else rows, candidates)


def _inproj_kernel(x_ref, nw_ref, w_ref, oret_ref, ogdn_ref, ogate_ref, oab_ref):
    h = (_rms(x_ref[...]) * nw_ref[...]).astype(BF16)
    oret_ref[...] = _dot(h, w_ref[:, :_RET_W]).astype(BF16)
    ogdn_ref[...] = _dot(h, w_ref[:, _RET_W:_AB_OFF]).astype(BF16)
    tail = _dot(h, w_ref[:, _AB_OFF:])
    oab_ref[...] = tail[:, :LANES]
    ogate_ref[...] = tail[:, _GATE_OFF - _AB_OFF:_GATE_OFF - _AB_OFF + 2 * D_MODEL].astype(BF16)


def _inproj(x, norm_w, wb):
    rows = x.shape[0]
    tm = _pick_tile(rows, (256, 128, 64, 32, 16, 8))
    row = lambda n: pl.BlockSpec((tm, n), lambda i: (i, 0))
    return pl.pallas_call(
        _inproj_kernel,
        grid=(rows // tm,),
        in_specs=[row(D_MODEL), _resident((1, D_MODEL)), _resident(wb.shape)],
        out_specs=[row(_RET_W), row(_GDN_W), row(2 * D_MODEL), row(LANES)],
        out_shape=[jax.ShapeDtypeStruct((rows, _RET_W), BF16), jax.ShapeDtypeStruct((rows, _GDN_W), BF16),
                   jax.ShapeDtypeStruct((rows, 2 * D_MODEL), BF16), jax.ShapeDtypeStruct((rows, LANES), F32)],
        compiler_params=_params(("parallel",)),
        name="inproj",
    )(x, norm_w.reshape(1, D_MODEL), wb)


def _ret_log_gamma():
    return np.log1p(-np.exp2(-5.0 - np.arange(RET_HEADS, dtype=np.float64)))


def _rope_tables(pos):
    half = RET_DK // 2
    inv = 1.0 / (ROPE_BASE ** (jnp.arange(half, dtype=F32) / half))
    ang = pos.astype(F32)[:, None] * inv[None, :]
    cos, sin = jnp.cos(ang), jnp.sin(ang)
    return jnp.concatenate([cos, cos], axis=-1), jnp.concatenate([-sin, sin], axis=-1)


def _rotary(x, cos, sin):
    return x * cos + pltpu.roll(x, RET_DK // 2, 1) * sin


def _ret_prompt_kernel(q_ref, k_ref, v_ref, g_ref, cos_ref, sin_ref, dmask_ref, qdec_ref, kdec_ref,
                       o_ref, s_ref, *, gammas):
    @pl.when(pl.program_id(1) == 0)
    def _():
        s_ref[...] = jnp.zeros_like(s_ref)

    c = RET_CHUNK
    chains = [(sq, h) for sq in range(q_ref.shape[0]) for h in range(RET_HEADS)]
    nc = range(len(chains))
    qk = [slice(h * RET_DK, (h + 1) * RET_DK) for _, h in chains]
    vv = [slice(h * RET_DV, (h + 1) * RET_DV) for _, h in chains]
    hd = [h for _, h in chains]
    sq = [s_ for s_, _ in chains]
    s = [s_ref[sq[i], hd[i]] for i in nc]
    for ck in range(q_ref.shape[1] // c):
        r = slice(ck * c, (ck + 1) * c)
        cos, sin = cos_ref[r, :], sin_ref[r, :]
        q = [_rotary(q_ref[sq[i], r, qk[i]].astype(F32), cos, sin) for i in nc]
        k = [_rotary(k_ref[sq[i], r, qk[i]].astype(F32), cos, sin) * (RET_DK ** -0.5) for i in nc]
        v = [v_ref[sq[i], r, vv[i]] for i in nc]
        qb = [x.astype(BF16) for x in q]
        inner = [_dot_nt(qb[i], k[i].astype(BF16)) * dmask_ref[hd[i]] for i in nc]
        cross = [_dot((q[i] * qdec_ref[hd[i]]).astype(BF16), s[i].astype(BF16)) for i in nc]
        upd = [_dot_tn((k[i] * kdec_ref[hd[i]]).astype(BF16), v[i]) for i in nc]
        o = [_dot(inner[i].astype(BF16), v[i]) + cross[i] for i in nc]
        s = [s[i] * gammas[hd[i]] + upd[i] for i in nc]
        for i in nc:
            o_ref[sq[i], r, vv[i]] = (_rms(o[i]) * _silu(g_ref[sq[i], r, vv[i]].astype(F32))).astype(BF16)
    for i in nc:
        s_ref[sq[i], hd[i]] = s[i]


def _ret_prompt(p_ret, batch, seq):
    c = RET_CHUNK
    step = RET_STEP_ROWS if seq % RET_STEP_ROWS == 0 else c
    n = seq // step
    lg = _ret_log_gamma()
    idx = np.arange(c, dtype=np.float64)
    diff = idx[:, None] - idx[None, :]
    dmask = np.where(diff >= 0, np.exp(np.maximum(diff, 0.0)[None] * lg[:, None, None]), 0.0)
    qdec = np.broadcast_to(np.exp((idx + 1.0)[None, :] * lg[:, None])[:, :, None], (RET_HEADS, c, RET_DK))
    kdec = np.broadcast_to(np.exp((c - 1.0 - idx)[None, :] * lg[:, None])[:, :, None], (RET_HEADS, c, RET_DK))
    gammas = tuple(float(g) for g in np.exp(c * lg))
    cos, sin = _rope_tables(jnp.arange(seq, dtype=jnp.int32))
    tab = lambda: _resident((RET_HEADS, c, RET_DK))
    nseq = RET_STEP_SEQS if batch % RET_STEP_SEQS == 0 else 1
    p3 = p_ret.reshape(batch, seq, _RET_W)
    o, s_new = pl.pallas_call(
        functools.partial(_ret_prompt_kernel, gammas=gammas),
        grid=(batch // nseq, n),
        in_specs=[pl.BlockSpec((nseq, step, RET_QK_W), lambda b, j: (b, j, 0)),
                  pl.BlockSpec((nseq, step, RET_QK_W), lambda b, j: (b, j, 1)),
                  pl.BlockSpec((nseq, step, RET_V_W), lambda b, j: (b, j, 1)),
                  pl.BlockSpec((nseq, step, RET_V_W), lambda b, j: (b, j, 2)),
                  pl.BlockSpec((step, RET_DK), lambda b, j: (j, 0)),
                  pl.BlockSpec((step, RET_DK), lambda b, j: (j, 0)),
                  tab(), tab(), tab()],
        out_specs=[pl.BlockSpec((nseq, step, RET_V_W), lambda b, j: (b, j, 0)),
                   pl.BlockSpec((nseq, RET_HEADS, RET_DK, RET_DV), lambda b, j: (b, 0, 0, 0))],
        out_shape=[jax.ShapeDtypeStruct((batch, seq, RET_V_W), BF16),
                   jax.ShapeDtypeStruct((batch, RET_HEADS, RET_DK, RET_DV), F32)],
        compiler_params=_params(("parallel", "arbitrary")),
        name="ret_prompt",
    )(p3, p3, p3, p3, cos, sin, jnp.asarray(dmask, F32), jnp.asarray(qdec, F32), jnp.asarray(kdec, F32))
    return o.reshape(batch * seq, RET_V_W), s_new


def _columns(x):
    n = x.shape[0]
    if n < LANES:
        x = jnp.concatenate([x, jnp.zeros((LANES - n, x.shape[1]), x.dtype)], axis=0)
    return x.T


def _ret_sample_kernel(p_ref, cos_ref, sin_ref, s_ref, o_ref, so_ref, *, gammas):
    cos, sin = cos_ref[...], sin_ref[...]
    nb = p_ref.shape[0]
    for h in range(RET_HEADS):
        qk = slice(h * RET_DK, (h + 1) * RET_DK)
        q = _rotary(p_ref[:, qk], cos, sin)
        k = _rotary(p_ref[:, RET_QK_W + h * RET_DK:RET_QK_W + (h + 1) * RET_DK], cos, sin) * (RET_DK ** -0.5)
        v = p_ref[:, 2 * RET_QK_W + h * RET_DV:2 * RET_QK_W + (h + 1) * RET_DV]
        g = p_ref[:, 2 * RET_QK_W + RET_V_W + h * RET_DV:2 * RET_QK_W + RET_V_W + (h + 1) * RET_DV]
        qk_dot = jnp.sum(q * k, axis=-1, keepdims=True)
        kt = _columns(k)
        qb = q.astype(BF16)
        rows = []
        for j in range(nb):
            s = s_ref[j, h]
            qs = _dot(qb, s.astype(BF16))[j:j + 1]
            rows.append(qk_dot[j:j + 1] * v[j:j + 1] + gammas[h] * qs)
            so_ref[j, h] = s * gammas[h] + kt[:, j:j + 1] * v[j:j + 1]
        o = jnp.concatenate(rows, axis=0)
        o_ref[:, h * RET_DV:(h + 1) * RET_DV] = (_rms(o) * _silu(g)).astype(BF16)


def _ret_sample(p_ret_s, state):
    ts = p_ret_s.shape[0]
    sb = SAMPLE_TILE
    gammas = tuple(float(g) for g in np.exp(_ret_log_gamma()))
    cos, sin = _rope_tables(jnp.full((1,), PAST_LEN, jnp.int32))
    st = pl.BlockSpec((sb, RET_HEADS, RET_DK, RET_DV), lambda i: (i, 0, 0, 0))
    return pl.pallas_call(
        functools.partial(_ret_sample_kernel, gammas=gammas),
        grid=(ts // sb,),
        in_specs=[pl.BlockSpec((sb, _RET_W), lambda i: (i, 0)), _resident((1, RET_DK)), _resident((1, RET_DK)), st],
        out_specs=[pl.BlockSpec((sb, RET_V_W), lambda i: (i, 0)), st],
        out_shape=[jax.ShapeDtypeStruct((ts, RET_V_W), BF16), jax.ShapeDtypeStruct(state.shape, F32)],
        compiler_params=_params(("parallel",)),
        name="ret_sample",
    )(p_ret_s, cos, sin, state)


def _l2norm(x):
    return x * lax.rsqrt(jnp.sum(x * x, axis=-1, keepdims=True) + NORM_EPS)


def _bdot(a, b):
    return _dot(a.astype(BF16), b.astype(BF16))


def _chunk_masks(c):
    ri = lax.broadcasted_iota(jnp.int32, (c, c), 0)
    ci = lax.broadcasted_iota(jnp.int32, (c, c), 1)
    eye = (ri == ci).astype(F32)
    diag16 = (ri // 16 == ci // 16).astype(F32)
    low32 = jnp.logical_and(ri // 32 == ci // 32, ri // 16 > ci // 16).astype(F32)
    low64 = (ri // 32 > ci // 32).astype(F32)
    return ri >= ci, ri > ci, eye, diag16, low32, low64


def _unit_lower_inverse(a, eye, diag16, low32, low64):
    many = lambda f, *ls: [f(*args) for args in zip(*ls)]
    c = eye.shape[0]
    pair = lambda p, q, rhs: _bdot(jnp.concatenate([p, q], axis=0), rhs)
    n = [-(x * diag16) for x in a]
    n2 = many(_bdot, n, n)
    n34 = many(pair, n, n2, n2)
    x = [eye + p + q + r[:c] for p, q, r in zip(n, n2, n34)]
    n4 = [r[c:] for r in n34]
    xn = many(pair, x, n4, n4)
    x = [u + v[:c] for u, v in zip(x, xn)]
    x = many(lambda u, v: u + v, x, many(_bdot, x, [v[c:] for v in xn]))
    for mask in (low32, low64):
        r = many(_bdot, [y * mask for y in a], x)
        x = many(lambda u, v: u - v, x, many(_bdot, x, r))
    return x


def _gdn_gates(ab, alog, dtb):
    g = -jnp.exp(alog) * _softplus(ab + dtb)
    return g, _sigmoid(ab)


def _gdn_prep_kernel(x_ref, prev_ref, ab_ref, cw_ref, alog_ref, dtb_ref,
                     u_ref, w_ref, qg_ref, kg_ref, at_ref, eg_ref, cv_ref, act_ref):
    c = GDN_CHUNK
    rows = x_ref.shape[0]
    x = x_ref[...]
    halo = prev_ref.shape[0]
    prev = jnp.where(pl.program_id(1) == 0, jnp.zeros_like(prev_ref), prev_ref[...])
    xcat = jnp.concatenate([prev, x], axis=0)
    ti = lax.broadcasted_iota(jnp.int32, (rows, rows + halo), 0)
    ui = lax.broadcasted_iota(jnp.int32, (rows, rows + halo), 1)
    xf = x.astype(F32)
    acc = xf * cw_ref[CONV_W - 1:CONV_W, :]
    for i in range(CONV_W - 1):
        shift = (ui == ti + (halo - (CONV_W - 1) + i)).astype(BF16)
        acc = acc + _dot(shift, xcat) * cw_ref[i:i + 1, :]
    cv_ref[0] = xf[rows - (CONV_W - 1):, :]
    act_ref[...] = _silu(acc)

    g_all, beta_all = _gdn_gates(ab_ref[...], alog_ref[...], dtb_ref[...])
    lower, strict, eye, diag16, low32, low64 = _chunk_masks(c)
    gc_all, gr_all = [], []
    for ck in range(rows // c):
        gc_ck = _dot_hi(lower.astype(F32), g_all[ck * c:(ck + 1) * c])
        gc_all.append(gc_ck)
        gr_all.append(_columns(gc_ck))
        eg_ref[ck] = jnp.exp(gc_ck[c - 1:c, :])
    chains = [(ck, h) for ck in range(rows // c) for h in range(GDN_HEADS)]
    rs = [slice(ck * c, (ck + 1) * c) for ck, _ in chains]
    sl = [slice(h * GDN_DK, (h + 1) * GDN_DK) for _, h in chains]
    nc = range(len(chains))
    q = [_l2norm(act_ref[rs[i], sl[i]]) * (GDN_DK ** -0.5) for i in nc]
    k = [_l2norm(act_ref[rs[i], GDN_QK_W + sl[i].start:GDN_QK_W + sl[i].stop]) for i in nc]
    v = [act_ref[rs[i], 2 * GDN_QK_W + sl[i].start:2 * GDN_QK_W + sl[i].stop] for i in nc]
    beta = [beta_all[rs[i], GDN_HEADS + h:GDN_HEADS + h + 1] for i, (_, h) in enumerate(chains)]
    gc = [gc_all[ck][:, h:h + 1] for ck, h in chains]
    gr = [gr_all[ck][h:h + 1, :c] for ck, h in chains]
    decay = [jnp.exp(jnp.where(lower, gc[i] - gr[i], -jnp.inf)) for i in nc]
    exp_g = [jnp.exp(x) for x in gc]
    kb = [k[i] * beta[i] for i in nc]
    kbf = [x.astype(BF16) for x in k]
    kq = [_dot_nt(jnp.concatenate([kb[i].astype(BF16), q[i].astype(BF16)], axis=0), kbf[i]) for i in nc]
    a = [kq[i][:c] * jnp.where(strict, decay[i], 0.0) for i in nc]
    attn = [kq[i][c:] * decay[i] for i in nc]
    t = _unit_lower_inverse(a, eye, diag16, low32, low64)
    uw = [_bdot(t[i], jnp.concatenate([v[i] * beta[i], kb[i] * exp_g[i]], axis=1)) for i in nc]
    for i in nc:
        u_ref[rs[i], sl[i]] = uw[i][:, :GDN_DV]
        w_ref[rs[i], sl[i]] = uw[i][:, GDN_DV:].astype(BF16)
        at_ref[rs[i], sl[i]] = jnp.concatenate([attn[i], jnp.zeros((c, GDN_DK - c), F32)], axis=1).astype(BF16)
        qg_ref[rs[i], sl[i]] = (q[i] * exp_g[i]).astype(BF16)
        kg_ref[rs[i], sl[i]] = (k[i] * jnp.exp(gc[i][c - 1:c, :] - gc[i])).astype(BF16)


def _gdn_scan_kernel(u_ref, w_ref, qg_ref, kg_ref, at_ref, eg_ref, z_ref, nw_ref, o_ref, s_ref):
    c = GDN_CHUNK

    @pl.when(pl.program_id(1) == 0)
    def _():
        s_ref[...] = jnp.zeros_like(s_ref)

    nw = nw_ref[...]
    chains = [(q, h) for q in range(u_ref.shape[0]) for h in range(GDN_HEADS)]
    sl = [slice(h * GDN_DK, (h + 1) * GDN_DK) for _, h in chains]
    nc = range(len(chains))
    s = [s_ref[q, h] for q, h in chains]
    for ck in range(u_ref.shape[1] // c):
        r = slice(ck * c, (ck + 1) * c)
        sb = [x.astype(BF16) for x in s]
        ws = [_dot(w_ref[q, r, sl[i]], sb[i]) for i, (q, _) in enumerate(chains)]
        qs = [_dot(qg_ref[q, r, sl[i]], sb[i]) for i, (q, _) in enumerate(chains)]
        vnb = [(u_ref[q, r, sl[i]] - ws[i]).astype(BF16) for i, (q, _) in enumerate(chains)]
        o = [qs[i] + _dot(at_ref[q, r, h * GDN_DK:h * GDN_DK + c], vnb[i]) for i, (q, h) in enumerate(chains)]
        s = [s[i] * eg_ref[q, ck][:, h:h + 1] + _dot_tn(kg_ref[q, r, sl[i]], vnb[i])
             for i, (q, h) in enumerate(chains)]
        for i, (q, _) in enumerate(chains):
            o_ref[q, r, sl[i]] = (_rms(o[i]) * nw * _silu(z_ref[q, r, sl[i]].astype(F32))).astype(BF16)
    for i, (q, h) in enumerate(chains):
        s_ref[q, h] = s[i]


def _gdn_prompt(p_gdn, p_ab, batch, seq, conv_w, a_log, dt_bias, gdn_norm_w):
    c = GDN_CHUNK
    rows = GDN_PREP_ROWS
    t = batch * seq
    nt = seq // rows
    alog = jnp.pad(a_log.astype(F32), (0, LANES - GDN_HEADS)).reshape(1, LANES)
    dtb = jnp.pad(dt_bias.astype(F32), (0, LANES - GDN_HEADS)).reshape(1, LANES)
    wide = lambda: pl.BlockSpec((rows, GDN_V_W), lambda b, j: (b * nt + j, 0))
    u, w, qg, kg, at, eg, conv_new = pl.pallas_call(
        _gdn_prep_kernel,
        grid=(batch, nt),
        in_specs=[pl.BlockSpec((rows, CONV_CH), lambda b, j: (b * nt + j, 0)),
                  pl.BlockSpec((16, CONV_CH), lambda b, j: (jnp.maximum((b * nt + j) * (rows // 16) - 1, 0), 0)),
                  pl.BlockSpec((rows, LANES), lambda b, j: (b * nt + j, 0)),
                  _resident((CONV_W, CONV_CH)), _resident((1, LANES)), _resident((1, LANES))],
        out_specs=[wide(), wide(), wide(), wide(), wide(),
                   pl.BlockSpec((rows // c, 1, LANES), lambda b, j: (b * nt + j, 0, 0)),
                   pl.BlockSpec((1, CONV_W - 1, CONV_CH), lambda b, j: (b, 0, 0))],
        out_shape=[jax.ShapeDtypeStruct((t, GDN_V_W), F32)] + [jax.ShapeDtypeStruct((t, GDN_V_W), BF16)] * 4
        + [jax.ShapeDtypeStruct((t // c, 1, LANES), F32),
           jax.ShapeDtypeStruct((batch, CONV_W - 1, CONV_CH), F32)],
        scratch_shapes=[pltpu.VMEM((rows, CONV_CH), F32)],
        compiler_params=_params(("parallel", "arbitrary")),
        name="gdn_prep",
    )(p_gdn, p_gdn, p_ab, conv_w.astype(F32), alog, dtb)
    srows = GDN_SCAN_ROWS
    n = seq // srows
    nq = CONV_CH // GDN_V_W
    nseq = GDN_SCAN_SEQS if batch % GDN_SCAN_SEQS == 0 else 1
    per_seq = lambda a: a.reshape((batch, seq // (t // a.shape[0])) + a.shape[1:])
    blk = lambda: pl.BlockSpec((nseq, srows, GDN_V_W), lambda b, j: (b, j, 0))
    o, s_new = pl.pallas_call(
        _gdn_scan_kernel,
        grid=(batch // nseq, n),
        in_specs=[blk(), blk(), blk(), blk(), blk(),
                  pl.BlockSpec((nseq, srows // c, 1, LANES), lambda b, j: (b, j, 0, 0)),
                  pl.BlockSpec((nseq, srows, GDN_V_W), lambda b, j: (b, j, nq)),
                  _resident((1, GDN_DV))],
        out_specs=[blk(), pl.BlockSpec((nseq, GDN_HEADS, GDN_DK, GDN_DV), lambda b, j: (b, 0, 0, 0))],
        out_shape=[jax.ShapeDtypeStruct((batch, seq, GDN_V_W), BF16),
                   jax.ShapeDtypeStruct((batch, GDN_HEADS, GDN_DK, GDN_DV), F32)],
        compiler_params=_params(("parallel", "arbitrary")),
        name="gdn_scan",
    )(per_seq(u), per_seq(w), per_seq(qg), per_seq(kg), per_seq(at), per_seq(eg), per_seq(p_gdn),
      gdn_norm_w.astype(F32).reshape(1, GDN_DV))
    return o.reshape(t, GDN_V_W), s_new, conv_new


def _gdn_sample_kernel(x_ref, ab_ref, sc_ref, cw_ref, alog_ref, dtb_ref, nw_ref, s_ref, o_ref, so_ref, sco_ref):
    nb = x_ref.shape[0]
    x = x_ref[:, :CONV_CH]
    acc = x * cw_ref[CONV_W - 1:CONV_W, :]
    for i in range(CONV_W - 1):
        acc = acc + sc_ref[i] * cw_ref[i:i + 1, :]
    for i in range(CONV_W - 2):
        sco_ref[i] = sc_ref[i + 1]
    sco_ref[CONV_W - 2] = x
    u = _silu(acc)
    g_all, beta_all = _gdn_gates(ab_ref[...], alog_ref[...], dtb_ref[...])
    eg_all = jnp.exp(g_all)
    nw = nw_ref[...]
    for h in range(GDN_HEADS):
        sl = slice(h * GDN_DK, (h + 1) * GDN_DK)
        q = _l2norm(u[:, sl]) * (GDN_DK ** -0.5)
        k = _l2norm(u[:, GDN_QK_W + h * GDN_DK:GDN_QK_W + (h + 1) * GDN_DK])
        v = u[:, 2 * GDN_QK_W + h * GDN_DV:2 * GDN_QK_W + (h + 1) * GDN_DV]
        beta = beta_all[:, GDN_HEADS + h:GDN_HEADS + h + 1]
        eg = eg_all[:, h:h + 1]
        qk_dot = jnp.sum(q * k, axis=-1, keepdims=True)
        kt = _columns(k)
        kq = jnp.concatenate([k, q], axis=0).astype(BF16)
        rows = []
        for j in range(nb):
            s = s_ref[j, h]
            kcol = kt[:, j:j + 1]
            both = _dot(kq, s.astype(BF16))
            ks, qs = both[j:j + 1], both[nb + j:nb + j + 1]
            ej = eg[j:j + 1]
            v_new = beta[j:j + 1] * (v[j:j + 1] - ej * ks)
            rows.append(ej * qs + qk_dot[j:j + 1] * v_new)
            so_ref[j, h] = s * ej + kcol * v_new
        o = jnp.concatenate(rows, axis=0)
        z = x_ref[:, CONV_CH + h * GDN_DV:CONV_CH + (h + 1) * GDN_DV]
        o_ref[:, sl] = (_rms(o) * nw * _silu(z)).astype(BF16)


def _gdn_sample(p_gdn_s, p_ab_s, state, conv_state, conv_w, a_log, dt_bias, gdn_norm_w):
    ts = p_gdn_s.shape[0]
    sb = SAMPLE_TILE
    alog = jnp.pad(a_log.astype(F32), (0, LANES - GDN_HEADS)).reshape(1, LANES)
    dtb = jnp.pad(dt_bias.astype(F32), (0, LANES - GDN_HEADS)).reshape(1, LANES)
    sc = jnp.swapaxes(conv_state.astype(F32), 0, 1)
    st = pl.BlockSpec((sb, GDN_HEADS, GDN_DK, GDN_DV), lambda i: (i, 0, 0, 0))
    scs = pl.BlockSpec((CONV_W - 1, sb, CONV_CH), lambda i: (0, i, 0))
    o, s_new, sc_new = pl.pallas_call(
        _gdn_sample_kernel,
        grid=(ts // sb,),
        in_specs=[pl.BlockSpec((sb, _GDN_W), lambda i: (i, 0)), pl.BlockSpec((sb, LANES), lambda i: (i, 0)), scs,
                  _resident((CONV_W, CONV_CH)), _resident((1, LANES)), _resident((1, LANES)),
                  _resident((1, GDN_DV)), st],
        out_specs=[pl.BlockSpec((sb, GDN_V_W), lambda i: (i, 0)), st, scs],
        out_shape=[jax.ShapeDtypeStruct((ts, GDN_V_W), BF16), jax.ShapeDtypeStruct(state.shape, F32),
                   jax.ShapeDtypeStruct(sc.shape, F32)],
        compiler_params=_params(("parallel",)),
        name="gdn_sample",
    )(p_gdn_s, p_ab_s, sc, conv_w.astype(F32), alog, dtb, gdn_norm_w.astype(F32).reshape(1, GDN_DV), state)
    return o, s_new, jnp.swapaxes(sc_new, 0, 1)


_NO_EXPERT = -1e30


def _merge_kernel(oa0_ref, ob0_ref, gate0_ref, x0_ref, oa1_ref, ob1_ref, gate1_ref, x1in_ref,
                  wa_ref, wb_ref, wo_ref, nw_ref, wr_ref, br_ref, x1_ref, h2_ref, ti_ref, tw_ref, *, n_first):
    first = pl.program_id(0) < n_first
    pick = lambda a, b: jnp.where(first, a[...], b[...])
    gate = pick(gate0_ref, gate1_ref)
    ya = _dot(pick(oa0_ref, oa1_ref), wa_ref[...])
    yb = _dot(pick(ob0_ref, ob1_ref), wb_ref[...])
    ga = gate[:, :D_MODEL].astype(F32)
    gb = gate[:, D_MODEL:].astype(F32)
    m = _sigmoid(ga) * ya + _sigmoid(gb) * yb
    x1 = pick(x0_ref, x1in_ref) + _dot(m.astype(BF16), wo_ref[...])
    x1_ref[...] = x1
    h2 = _rms(x1) * nw_ref[...]
    for j, plane in enumerate(_pack_rows(h2)):
        h2_ref[:, j, :] = plane
    lg = _dot_split(h2, wr_ref[...]) + br_ref[...]
    lane = lax.broadcasted_iota(jnp.int32, lg.shape, 1).astype(F32)
    vals, idxs = [], []
    for _ in range(TOP_K):
        top = jnp.max(lg, axis=-1, keepdims=True)
        idx = jnp.min(jnp.where(lg == top, lane, float(LANES)), axis=-1, keepdims=True)
        vals.append(top)
        idxs.append(idx)
        lg = jnp.where(lane == idx, _NO_EXPERT, lg)
    es = [jnp.exp(v - vals[0]) for v in vals]
    inv_total = 1.0 / functools.reduce(lambda a, b: a + b, es)
    ti = jnp.zeros_like(lg)
    tw = jnp.zeros_like(lg)
    for k in range(TOP_K):
        ti = jnp.where(lane == float(k), idxs[k], ti)
        tw = jnp.where(lane == float(k), es[k] * inv_total, tw)
    ti_ref[...] = ti.astype(jnp.int32)
    tw_ref[...] = tw


def _merge(group0, group1, weights):
    r0 = group0[3].shape[0]
    tm = _pick_tile(r0, (MERGE_ROWS, 128, 64, 32, 16, 8))
    pad = -group1[3].shape[0] % tm
    group1 = tuple(jnp.pad(a, ((0, pad), (0, 0))) for a in group1)
    r1 = group1[3].shape[0]
    n0, total = r0 // tm, r0 + r1
    widths = (RET_V_W, GDN_V_W, 2 * D_MODEL, D_MODEL)
    specs0 = [pl.BlockSpec((tm, n), lambda i: (jnp.minimum(i, n0 - 1), 0)) for n in widths]
    specs1 = [pl.BlockSpec((tm, n), lambda i: (jnp.maximum(i - n0, 0), 0)) for n in widths]
    out = lambda n: pl.BlockSpec((tm, n), lambda i: (i, 0))
    sub = PACK_PLANES
    sq = (D_MODEL, D_MODEL)
    return pl.pallas_call(
        functools.partial(_merge_kernel, n_first=n0),
        grid=(total // tm,),
        in_specs=specs0 + specs1 + [_resident(sq), _resident(sq), _resident(sq), _resident((1, D_MODEL)),
                                    _resident((D_MODEL, LANES)), _resident((1, LANES))],
        out_specs=[out(D_MODEL), pl.BlockSpec((tm, sub, LANES), lambda i: (i, 0, 0)), out(LANES), out(LANES)],
        out_shape=[jax.ShapeDtypeStruct((total, D_MODEL), F32), jax.ShapeDtypeStruct((total, sub, LANES), jnp.uint32),
                   jax.ShapeDtypeStruct((total, LANES), jnp.int32), jax.ShapeDtypeStruct((total, LANES), F32)],
        compiler_params=_params(("parallel",)),
        name="merge",
    )(*group0, *group1, *weights)


def _merge_weights(w_a, w_b, w_o, ffn_norm_w, w_router, b_router):
    wr = jnp.pad(w_router.astype(F32), ((0, 0), (0, LANES - N_EXPERTS)))
    br = jnp.pad(b_router.astype(F32), (0, LANES - N_EXPERTS), constant_values=_NO_EXPERT).reshape(1, LANES)
    return (w_a.astype(BF16), w_b.astype(BF16), w_o.astype(BF16), ffn_norm_w.astype(F32).reshape(1, D_MODEL), wr, br)


def _route(top_i, gate, n_tokens):
    rows = MOE_ROWS
    n = n_tokens * TOP_K
    flat_e = top_i.reshape(n).astype(jnp.int32)
    bits = max(1, (n - 1).bit_length())
    assert bits + (N_EXPERTS - 1).bit_length() <= 31
    order = lax.sort((flat_e << bits) | jnp.arange(n, dtype=jnp.int32), is_stable=False) & ((1 << bits) - 1)
    counts = jnp.sum((flat_e[:, None] == jnp.arange(N_EXPERTS, dtype=jnp.int32)[None, :]).astype(jnp.int32), axis=0)
    start = jnp.cumsum(counts) - counts
    pcounts = (counts + rows - 1) // rows * rows
    pend = jnp.cumsum(pcounts)
    pstart = pend - pcounts
    nb = -(-n // rows) + N_EXPERTS
    blk = jnp.arange(nb, dtype=jnp.int32)
    block_e = jnp.minimum(jnp.sum((pend[None, :] <= (blk * rows)[:, None]).astype(jnp.int32), axis=1),
                          N_EXPERTS - 1).astype(jnp.int32)
    nb_used = (pend[-1] // rows).astype(jnp.int32).reshape(1)
    is_e = block_e[:, None] == jnp.arange(N_EXPERTS, dtype=jnp.int32)[None, :]
    of_block = lambda table: jnp.sum(jnp.where(is_e, table[None, :], 0), axis=1)
    within = (blk * rows - of_block(pstart))[:, None] + jnp.arange(rows, dtype=jnp.int32)[None, :]
    valid = jnp.logical_and(within < of_block(counts)[:, None], (blk < nb_used[0])[:, None])
    flat = order[jnp.clip(of_block(start)[:, None] + within, 0, n - 1)]
    spare = n + (blk % MOE_RING)[:, None] * rows + jnp.arange(rows, dtype=jnp.int32)[None, :]
    src = jnp.where(valid, flat // TOP_K, 0)
    dst = jnp.where(valid, flat, spare)
    ahead = lambda k: jnp.concatenate([src[k:]] + [src[-1:]] * k, axis=0)
    slab = jnp.concatenate([src, dst, ahead(1), ahead(2)], axis=1)
    row_w = jnp.where(valid, gate[:, :TOP_K].reshape(n)[flat], 0.0)
    row_w = jnp.broadcast_to(row_w[:, :, None], (nb, rows, LANES))
    return block_e, nb_used, slab, row_w


def _expert_kernel(be_ref, nbu_ref, slab_ref, h_ref, roww_ref, wgu_ref, bgu_ref, wd_ref, bd_ref, y_ref,
                   idx0_ref, idx1_ref, idx2_ref, xbuf_ref, ybuf_ref, wgu_bf_ref, wd_bf_ref, isem, gsem, ssem):
    rows = MOE_ROWS
    ring = MOE_RING
    idx_refs = (idx0_ref, idx1_ref, idx2_ref)
    assert len(idx_refs) == ring
    sub = PACK_PLANES
    i = pl.program_id(0)
    nbu = nbu_ref[0]
    slot = i % ring
    active = i < nbu
    n_real = y_ref.shape[0] - ring * rows

    def slab_copy(blk, sl):
        return pltpu.make_async_copy(slab_ref.at[blk], idx_refs[sl], isem.at[sl])

    def gather_row(tok, sl, r):
        return pltpu.make_async_copy(h_ref.at[tok], xbuf_ref.at[sl, :, r, :], gsem.at[sl])

    def scatter_row(sl, r, f):
        return pltpu.make_async_copy(ybuf_ref.at[sl, :, r, :], y_ref.at[f], ssem.at[sl])

    def gather_wait(sl):
        pltpu.make_async_copy(xbuf_ref.at[sl], xbuf_ref.at[sl], gsem.at[sl]).wait()

    def scatter_wait(sl):
        pltpu.make_async_copy(ybuf_ref.at[sl], ybuf_ref.at[sl], ssem.at[sl]).wait()

    @pl.when(i == 0)
    def _():
        slab_copy(0, 0).start()
        ybuf_ref[...] = jnp.zeros_like(ybuf_ref)
        for sl in range(ring):
            def fill(r, carry, sl=sl):
                scatter_row(sl, r, n_real + sl * rows + r).start()
                return carry
            lax.fori_loop(0, rows, fill, 0)

    def block(sl):
        nxt, nxt2 = (sl + 1) % ring, (sl + 2) % ring
        idx_ref = idx_refs[sl]
        slab_copy(i, sl).wait()

        @pl.when(i + 1 < nbu)
        def _():
            slab_copy(i + 1, nxt).start()

        if sl == 0:
            @pl.when(i == 0)
            def _():
                def first(r, carry):
                    gather_row(idx_ref[r], 0, r).start()
                    gather_row(idx_ref[2 * rows + r], 1, r).start()
                    return carry
                lax.fori_loop(0, rows, first, 0)

        changed = jnp.logical_or(i == 0, be_ref[i] != be_ref[jnp.maximum(i - 1, 0)])

        @pl.when(changed)
        def _():
            wgu_bf_ref[...] = wgu_ref[0].astype(BF16)
            wd_bf_ref[...] = wd_ref[0].astype(BF16)

        gather_wait(sl)
        scatter_wait(sl)
        lo, hi = _unpack_planes([xbuf_ref[sl, j] for j in range(sub)])
        xb = jnp.concatenate(lo + hi, axis=1).astype(BF16)
        hb = _dot(xb, wgu_bf_ref[...]) + bgu_ref[0]
        for r in range(rows):
            gather_row(idx_ref[3 * rows + r], nxt2, r).start(priority=r % 2)
        glu = jnp.minimum(hb[:, :D_FF], SWIGLU_LIMIT)
        lin = jnp.clip(hb[:, D_FF:], -SWIGLU_LIMIT, SWIGLU_LIMIT)
        act = (glu * _sigmoid(SWIGLU_ALPHA * glu) * (lin + 1.0)).astype(BF16)
        yv = (_dot(act, wd_bf_ref[...]) + bd_ref[0]) * roww_ref[0, :, 0:1]
        for j, plane in enumerate(_pack_rows(yv)):
            ybuf_ref[sl, j] = plane
        for r in range(rows):
            scatter_row(sl, r, idx_ref[rows + r]).start(priority=(r + 1) % 2)

        @pl.when(i == nbu - 1)
        def _():
            for s in (sl, nxt, nxt2):
                scatter_wait(s)
            gather_wait(nxt)
            gather_wait(nxt2)

    for sl in range(ring):
        pl.when(jnp.logical_and(active, slot == sl))(functools.partial(block, sl))


def _experts(h2, t, block_e, nb_used, slab, row_w, w_gate_up, b_gate_up, w_down, b_down):
    rows = MOE_ROWS
    ring = MOE_RING
    assert (ring * rows) % TOP_K == 0
    nb = slab.shape[0]
    sub = PACK_PLANES
    grid_spec = pltpu.PrefetchScalarGridSpec(
        num_scalar_prefetch=2,
        grid=(nb,),
        in_specs=[pl.BlockSpec(memory_space=pl.ANY),
                  pl.BlockSpec(memory_space=pl.ANY),
                  pl.BlockSpec((1, rows, LANES), lambda i, be, nbu: (i, 0, 0)),
                  pl.BlockSpec((1, D_MODEL, 2 * D_FF), lambda i, be, nbu: (be[i], 0, 0)),
                  pl.BlockSpec((1, 1, 2 * D_FF), lambda i, be, nbu: (be[i], 0, 0)),
                  pl.BlockSpec((1, D_FF, D_MODEL), lambda i, be, nbu: (be[i], 0, 0)),
                  pl.BlockSpec((1, 1, D_MODEL), lambda i, be, nbu: (be[i], 0, 0))],
        out_specs=pl.BlockSpec(memory_space=pl.ANY),
        scratch_shapes=[pltpu.SMEM((slab.shape[1],), jnp.int32)] * ring + [
                        pltpu.VMEM((ring, sub, rows, LANES), jnp.uint32),
                        pltpu.VMEM((ring, sub, rows, LANES), jnp.uint32),
                        pltpu.VMEM((D_MODEL, 2 * D_FF), BF16),
                        pltpu.VMEM((D_FF, D_MODEL), BF16),
                        pltpu.SemaphoreType.DMA((ring,)),
                        pltpu.SemaphoreType.DMA((ring,)),
                        pltpu.SemaphoreType.DMA((ring,))])
    return pl.pallas_call(
        _expert_kernel,
        grid_spec=grid_spec,
        out_shape=jax.ShapeDtypeStruct((t * TOP_K + ring * rows, sub, LANES), jnp.uint32),
        compiler_params=_params(("arbitrary",)),
        name="experts",
    )(block_e, nb_used, slab, h2, row_w, w_gate_up, b_gate_up.reshape(N_EXPERTS, 1, 2 * D_FF), w_down,
      b_down.reshape(N_EXPERTS, 1, D_MODEL))


def _combine_kernel(y_ref, x1_ref, nw_ref, o_ref, sum_ref, *, final):
    lo, hi = None, None
    for k in range(TOP_K):
        (l,), (h,) = _unpack_planes([y_ref[:, k]])
        lo, hi = (l, h) if lo is None else (lo + l, hi + h)
    sum_ref[:, :PACK_PLANES, :] = lo
    sum_ref[:, PACK_PLANES:, :] = hi
    acc = x1_ref[...] + jnp.concatenate([sum_ref[:, j, :] for j in range(D_MODEL // LANES)], axis=1)
    o_ref[...] = _rms(acc) * nw_ref[...] if final else acc


def _combine(y, x1, row0, rows, norm_w, final):
    tm = _row_tile(rows, row0, (256, 128, 64, 32, 16, 8))
    off = row0 // tm
    sub = PACK_PLANES
    y4 = y.reshape(y.shape[0] // TOP_K, TOP_K, sub, LANES)
    return pl.pallas_call(
        functools.partial(_combine_kernel, final=final),
        grid=(rows // tm,),
        in_specs=[pl.BlockSpec((tm, TOP_K, sub, LANES), lambda i: (off + i, 0, 0, 0)),
                  pl.BlockSpec((tm, D_MODEL), lambda i: (off + i, 0)), _resident((1, D_MODEL))],
        out_specs=pl.BlockSpec((tm, D_MODEL), lambda i: (i, 0)),
        out_shape=jax.ShapeDtypeStruct((rows, D_MODEL), F32),
        scratch_shapes=[pltpu.VMEM((tm, D_MODEL // LANES, LANES), F32)],
        compiler_params=_params(("parallel",)),
        name="combine",
    )(y4, x1, norm_w.astype(F32).reshape(1, D_MODEL))


def kernel(x_prompt, x_sample, state_ret, state_gdn, state_conv, attn_norm_w, w_in, conv_w, a_log, dt_bias, gdn_norm_w, w_branch_a, w_branch_b, w_out, ffn_norm_w, w_router, b_router, w_gate_up, b_gate_up, w_down, b_down, final_norm_w):
    bp, lp, d = x_prompt.shape
    bs, ls, _ = x_sample.shape
    assert ls == 1 and d == D_MODEL and lp % RET_CHUNK == 0 and bs % SAMPLE_TILE == 0
    depth = w_in.shape[0]
    tp = bp * lp
    t = tp + bs
    xp, xs = x_prompt.reshape(tp, d).astype(F32), x_sample.reshape(bs, d).astype(F32)
    rp, gp, cp, rs, gs, cs = [], [], [], [], [], []
    for l in range(depth):
        wb = jnp.pad(w_in[l], ((0, 0), (0, -w_in.shape[2] % LANES))).astype(BF16)
        pp_ret, pp_gdn, pp_gate, pp_ab = _inproj(xp, attn_norm_w[l], wb)
        ps_ret, ps_gdn, ps_gate, ps_ab = _inproj(xs, attn_norm_w[l], wb)
        op_ret, s_ret_p = _ret_prompt(pp_ret, bp, lp)
        os_ret, s_ret_s = _ret_sample(ps_ret.astype(F32), state_ret[l].astype(F32))
        op_gdn, s_gdn_p, conv_p = _gdn_prompt(pp_gdn, pp_ab, bp, lp, conv_w[l], a_log[l], dt_bias[l], gdn_norm_w[l])
        os_gdn, s_gdn_s, conv_s = _gdn_sample(ps_gdn.astype(F32), ps_ab, state_gdn[l].astype(F32), state_conv[l],
                                              conv_w[l], a_log[l], dt_bias[l], gdn_norm_w[l])
        mw = _merge_weights(w_branch_a[l], w_branch_b[l], w_out[l], ffn_norm_w[l], w_router[l], b_router[l])
        x1, h2, top_i, gate = _merge((op_ret, op_gdn, pp_gate, xp), (os_ret, os_gdn, ps_gate, xs), mw)
        block_e, nb_used, slab, row_w = _route(top_i[:t, :TOP_K], gate[:t], t)
        y = _experts(h2, t, block_e, nb_used, slab, row_w, w_gate_up[l], b_gate_up[l], w_down[l], b_down[l])
        last = l == depth - 1
        norm_w = final_norm_w if last else jnp.ones((d,), F32)
        xp = _combine(y, x1, 0, tp, norm_w, last)
        xs = _combine(y, x1, tp, bs, norm_w, last)
        rp.append(s_ret_p); gp.append(s_gdn_p); cp.append(conv_p)
        rs.append(s_ret_s); gs.append(s_gdn_s); cs.append(conv_s)
    y_prompt = xp.reshape(bp, lp, d).astype(x_prompt.dtype)
    y_sample = xs.reshape(bs, ls, d).astype(x_sample.dtype)
    return (y_prompt, y_sample,
            jnp.stack(rp).astype(state_ret.dtype), jnp.stack(gp).astype(state_gdn.dtype),
            jnp.stack(cp).astype(state_conv.dtype),
            jnp.stack(rs).astype(state_ret.dtype), jnp.stack(gs).astype(state_gdn.dtype),
            jnp.stack(cs).astype(state_conv.dtype))
```

```python
import functools
import math

import numpy as np
import jax
import jax.numpy as jnp
from jax import lax
from jax.experimental import pallas as pl
from jax.experimental.pallas import tpu as pltpu

F32 = jnp.float32
BF16 = jnp.bfloat16
HIGHEST = lax.Precision.HIGHEST

D_MODEL = 1024
PAST_LEN = 16384
RET_HEADS, RET_DK, RET_DV = 4, 128, 256
RET_QK_W, RET_V_W = RET_HEADS * RET_DK, RET_HEADS * RET_DV
RET_CHUNK = 128
ROPE_BASE = 10000.0
GDN_HEADS, GDN_DK, GDN_DV = 8, 128, 128
GDN_QK_W, GDN_V_W = GDN_HEADS * GDN_DK, GDN_HEADS * GDN_DV
GDN_CHUNK = 64
CONV_W = 4
CONV_CH = 2 * GDN_QK_W + GDN_V_W
N_EXPERTS = 32
TOP_K = 4
D_FF = D_MODEL
SWIGLU_LIMIT = 7.0
SWIGLU_ALPHA = 1.702
NORM_EPS = 1e-6

_RET_W = 2 * RET_QK_W + 2 * RET_V_W
_GDN_W = CONV_CH + GDN_V_W
_AB_OFF = _RET_W + _GDN_W
_GATE_OFF = _AB_OFF + 2 * GDN_HEADS

LANES = 128
VMEM_LIMIT = 56 * 1024 * 1024
MERGE_ROWS = 512
MOE_ROWS = 256
MOE_RING = 3
SAMPLE_TILE = 8
RET_STEP_ROWS = 256
RET_STEP_SEQS = 4
GDN_PREP_ROWS = 128
GDN_SCAN_ROWS = 256
GDN_SCAN_SEQS = 4


def _pick_tile(n, candidates):
    for c in candidates:
        if n % c == 0:
            return c
    raise ValueError(f"no tile in {candidates} divides {n}")


def _params(sem, vmem=VMEM_LIMIT):
    return pltpu.CompilerParams(dimension_semantics=sem, vmem_limit_bytes=vmem)


def _resident(shape):
    nd = len(shape)
    return pl.BlockSpec(shape, lambda *_: (0,) * nd, pipeline_mode=pl.Buffered(1))


def _silu(x):
    return x * (1.0 / (1.0 + jnp.exp(-x)))


def _sigmoid(x):
    return 1.0 / (1.0 + jnp.exp(-x))


def _softplus(x):
    return jnp.maximum(x, 0.0) + jnp.log1p(jnp.exp(-jnp.abs(x)))


def _rms(x):
    return x * lax.rsqrt(jnp.mean(x * x, axis=-1, keepdims=True) + NORM_EPS)


def _dot(a, b):
    return jnp.dot(a, b, preferred_element_type=F32)


def _dot_nt(a, b):
    return lax.dot_general(a, b, (((1,), (1,)), ((), ())), preferred_element_type=F32)


def _dot_tn(a, b):
    return lax.dot_general(a, b, (((0,), (0,)), ((), ())), preferred_element_type=F32)


def _dot_hi(a, b):
    return jnp.dot(a, b, preferred_element_type=F32, precision=HIGHEST)


PACK_PLANES = D_MODEL // (2 * LANES)


def _pack_rows(x):
    as_bits = lambda v: pltpu.bitcast(v.astype(BF16).astype(F32), jnp.uint32)
    planes = []
    for j in range(PACK_PLANES):
        lo = as_bits(x[:, j * LANES:(j + 1) * LANES])
        hi = as_bits(x[:, (j + PACK_PLANES) * LANES:(j + PACK_PLANES + 1) * LANES])
        planes.append(jnp.bitwise_or(hi, lax.shift_right_logical(lo, jnp.uint32(16))))
    return planes


def _unpack_planes(planes):
    lo = [pltpu.bitcast(lax.shift_left(p, jnp.uint32(16)), F32) for p in planes]
    hi = [pltpu.bitcast(jnp.bitwise_and(p, jnp.uint32(0xFFFF0000)), F32) for p in planes]
    return lo, hi


def _dot_split(a, b):
    a_hi, b_hi = a.astype(BF16), b.astype(BF16)
    a_lo = (a - a_hi.astype(F32)).astype(BF16)
    b_lo = (b - b_hi.astype(F32)).astype(BF16)
    return _dot(a_hi, b_hi) + (_dot(a_hi, b_lo) + _dot(a_lo, b_hi))


def _row_tile(rows, row0, candidates):
    return _pick_tile(math.gcd(rows, row0) if row0 else rows, candidates)


def _inproj_kernel(x_ref, nw_ref, w_ref, oret_ref, ogdn_ref, ogate_ref, oab_ref):
    h = (_rms(x_ref[...]) * nw_ref[...]).astype(BF16)
    oret_ref[...] = _dot(h, w_ref[:, :_RET_W]).astype(BF16)
    ogdn_ref[...] = _dot(h, w_ref[:, _RET_W:_AB_OFF]).astype(BF16)
    tail = _dot(h, w_ref[:, _AB_OFF:])
    oab_ref[...] = tail[:, :LANES]
    ogate_ref[...] = tail[:, _GATE_OFF - _AB_OFF:_GATE_OFF - _AB_OFF + 2 * D_MODEL].astype(BF16)


def _inproj(x, norm_w, wb):
    rows = x.shape[0]
    tm = _pick_tile(rows, (256, 128, 64, 32, 16, 8))
    row = lambda n: pl.BlockSpec((tm, n), lambda i: (i, 0))
    return pl.pallas_call(
        _inproj_kernel,
        grid=(rows // tm,),
        in_specs=[row(D_MODEL), _resident((1, D_MODEL)), _resident(wb.shape)],
        out_specs=[row(_RET_W), row(_GDN_W), row(2 * D_MODEL), row(LANES)],
        out_shape=[jax.ShapeDtypeStruct((rows, _RET_W), BF16), jax.ShapeDtypeStruct((rows, _GDN_W), BF16),
                   jax.ShapeDtypeStruct((rows, 2 * D_MODEL), BF16), jax.ShapeDtypeStruct((rows, LANES), F32)],
        compiler_params=_params(("parallel",)),
        name="inproj",
    )(x, norm_w.reshape(1, D_MODEL), wb)


def _ret_log_gamma():
    return np.log1p(-np.exp2(-5.0 - np.arange(RET_HEADS, dtype=np.float64)))


def _rope_tables(pos):
    half = RET_DK // 2
    inv = 1.0 / (ROPE_BASE ** (jnp.arange(half, dtype=F32) / half))
    ang = pos.astype(F32)[:, None] * inv[None, :]
    cos, sin = jnp.cos(ang), jnp.sin(ang)
    return jnp.concatenate([cos, cos], axis=-1), jnp.concatenate([-sin, sin], axis=-1)


def _rotary(x, cos, sin):
    return x * cos + pltpu.roll(x, RET_DK // 2, 1) * sin


def _ret_prompt_kernel(q_ref, k_ref, v_ref, g_ref, cos_ref, sin_ref, dmask_ref, qdec_ref, kdec_ref,
                       o_ref, s_ref, *, gammas):
    @pl.when(pl.program_id(1) == 0)
    def _():
        s_ref[...] = jnp.zeros_like(s_ref)

    c = RET_CHUNK
    chains = [(sq, h) for sq in range(q_ref.shape[0]) for h in range(RET_HEADS)]
    nc = range(len(chains))
    qk = [slice(h * RET_DK, (h + 1) * RET_DK) for _, h in chains]
    vv = [slice(h * RET_DV, (h + 1) * RET_DV) for _, h in chains]
    hd = [h for _, h in chains]
    sq = [s_ for s_, _ in chains]
    s = [s_ref[sq[i], hd[i]] for i in nc]
    for ck in range(q_ref.shape[1] // c):
        r = slice(ck * c, (ck + 1) * c)
        cos, sin = cos_ref[r, :], sin_ref[r, :]
        q = [_rotary(q_ref[sq[i], r, qk[i]].astype(F32), cos, sin) for i in nc]
        k = [_rotary(k_ref[sq[i], r, qk[i]].astype(F32), cos, sin) * (RET_DK ** -0.5) for i in nc]
        v = [v_ref[sq[i], r, vv[i]] for i in nc]
        qb = [x.astype(BF16) for x in q]
        inner = [_dot_nt(qb[i], k[i].astype(BF16)) * dmask_ref[hd[i]] for i in nc]
        cross = [_dot((q[i] * qdec_ref[hd[i]]).astype(BF16), s[i].astype(BF16)) for i in nc]
        upd = [_dot_tn((k[i] * kdec_ref[hd[i]]).astype(BF16), v[i]) for i in nc]
        o = [_dot(inner[i].astype(BF16), v[i]) + cross[i] for i in nc]
        s = [s[i] * gammas[hd[i]] + upd[i] for i in nc]
        for i in nc:
            o_ref[sq[i], r, vv[i]] = (_rms(o[i]) * _silu(g_ref[sq[i], r, vv[i]].astype(F32))).astype(BF16)
    for i in nc:
        s_ref[sq[i], hd[i]] = s[i]


def _ret_prompt(p_ret, batch, seq):
    c = RET_CHUNK
    step = RET_STEP_ROWS if seq % RET_STEP_ROWS == 0 else c
    n = seq // step
    lg = _ret_log_gamma()
    idx = np.arange(c, dtype=np.float64)
    diff = idx[:, None] - idx[None, :]
    dmask = np.where(diff >= 0, np.exp(np.maximum(diff, 0.0)[None] * lg[:, None, None]), 0.0)
    qdec = np.broadcast_to(np.exp((idx + 1.0)[None, :] * lg[:, None])[:, :, None], (RET_HEADS, c, RET_DK))
    kdec = np.broadcast_to(np.exp((c - 1.0 - idx)[None, :] * lg[:, None])[:, :, None], (RET_HEADS, c, RET_DK))
    gammas = tuple(float(g) for g in np.exp(c * lg))
    cos, sin = _rope_tables(jnp.arange(seq, dtype=jnp.int32))
    tab = lambda: _resident((RET_HEADS, c, RET_DK))
    nseq = RET_STEP_SEQS if batch % RET_STEP_SEQS == 0 else 1
    p3 = p_ret.reshape(batch, seq, _RET_W)
    o, s_new = pl.pallas_call(
        functools.partial(_ret_prompt_kernel, gammas=gammas),
        grid=(batch // nseq, n),
        in_specs=[pl.BlockSpec((nseq, step, RET_QK_W), lambda b, j: (b, j, 0)),
                  pl.BlockSpec((nseq, step, RET_QK_W), lambda b, j: (b, j, 1)),
                  pl.BlockSpec((nseq, step, RET_V_W), lambda b, j: (b, j, 1)),
                  pl.BlockSpec((nseq, step, RET_V_W), lambda b, j: (b, j, 2)),
                  pl.BlockSpec((step, RET_DK), lambda b, j: (j, 0)),
                  pl.BlockSpec((step, RET_DK), lambda b, j: (j, 0)),
                  tab(), tab(), tab()],
        out_specs=[pl.BlockSpec((nseq, step, RET_V_W), lambda b, j: (b, j, 0)),
                   pl.BlockSpec((nseq, RET_HEADS, RET_DK, RET_DV), lambda b, j: (b, 0, 0, 0))],
        out_shape=[jax.ShapeDtypeStruct((batch, seq, RET_V_W), BF16),
                   jax.ShapeDtypeStruct((batch, RET_HEADS, RET_DK, RET_DV), F32)],
        compiler_params=_params(("parallel", "arbitrary")),
        name="ret_prompt",
    )(p3, p3, p3, p3, cos, sin, jnp.asarray(dmask, F32), jnp.asarray(qdec, F32), jnp.asarray(kdec, F32))
    return o.reshape(batch * seq, RET_V_W), s_new


def _columns(x):
    n = x.shape[0]
    if n < LANES:
        x = jnp.concatenate([x, jnp.zeros((LANES - n, x.shape[1]), x.dtype)], axis=0)
    return x.T


def _ret_sample_kernel(p_ref, cos_ref, sin_ref, s_ref, o_ref, so_ref, *, gammas):
    cos, sin = cos_ref[...], sin_ref[...]
    nb = p_ref.shape[0]
    for h in range(RET_HEADS):
        qk = slice(h * RET_DK, (h + 1) * RET_DK)
        q = _rotary(p_ref[:, qk], cos, sin)
        k = _rotary(p_ref[:, RET_QK_W + h * RET_DK:RET_QK_W + (h + 1) * RET_DK], cos, sin) * (RET_DK ** -0.5)
        v = p_ref[:, 2 * RET_QK_W + h * RET_DV:2 * RET_QK_W + (h + 1) * RET_DV]
        g = p_ref[:, 2 * RET_QK_W + RET_V_W + h * RET_DV:2 * RET_QK_W + RET_V_W + (h + 1) * RET_DV]
        qk_dot = jnp.sum(q * k, axis=-1, keepdims=True)
        kt = _columns(k)
        qb = q.astype(BF16)
        rows = []
        for j in range(nb):
            s = s_ref[j, h]
            qs = _dot(qb, s.astype(BF16))[j:j + 1]
            rows.append(qk_dot[j:j + 1] * v[j:j + 1] + gammas[h] * qs)
            so_ref[j, h] = s * gammas[h] + kt[:, j:j + 1] * v[j:j + 1]
        o = jnp.concatenate(rows, axis=0)
        o_ref[:, h * RET_DV:(h + 1) * RET_DV] = (_rms(o) * _silu(g)).astype(BF16)


def _ret_sample(p_ret_s, state):
    ts = p_ret_s.shape[0]
    sb = SAMPLE_TILE
    gammas = tuple(float(g) for g in np.exp(_ret_log_gamma()))
    cos, sin = _rope_tables(jnp.full((1,), PAST_LEN, jnp.int32))
    st = pl.BlockSpec((sb, RET_HEADS, RET_DK, RET_DV), lambda i: (i, 0, 0, 0))
    return pl.pallas_call(
        functools.partial(_ret_sample_kernel, gammas=gammas),
        grid=(ts // sb,),
        in_specs=[pl.BlockSpec((sb, _RET_W), lambda i: (i, 0)), _resident((1, RET_DK)), _resident((1, RET_DK)), st],
        out_specs=[pl.BlockSpec((sb, RET_V_W), lambda i: (i, 0)), st],
        out_shape=[jax.ShapeDtypeStruct((ts, RET_V_W), BF16), jax.ShapeDtypeStruct(state.shape, F32)],
        compiler_params=_params(("parallel",)),
        name="ret_sample",
    )(p_ret_s, cos, sin, state)


def _l2norm(x):
    return x * lax.rsqrt(jnp.sum(x * x, axis=-1, keepdims=True) + NORM_EPS)


def _bdot(a, b):
    return _dot(a.astype(BF16), b.astype(BF16))


def _chunk_masks(c):
    ri = lax.broadcasted_iota(jnp.int32, (c, c), 0)
    ci = lax.broadcasted_iota(jnp.int32, (c, c), 1)
    eye = (ri == ci).astype(F32)
    diag16 = (ri // 16 == ci // 16).astype(F32)
    low32 = jnp.logical_and(ri // 32 == ci // 32, ri // 16 > ci // 16).astype(F32)
    low64 = (ri // 32 > ci // 32).astype(F32)
    return ri >= ci, ri > ci, eye, diag16, low32, low64


def _unit_lower_inverse(a, eye, diag16, low32, low64):
    many = lambda f, *ls: [f(*args) for args in zip(*ls)]
    c = eye.shape[0]
    pair = lambda p, q, rhs: _bdot(jnp.concatenate([p, q], axis=0), rhs)
    n = [-(x * diag16) for x in a]
    n2 = many(_bdot, n, n)
    n34 = many(pair, n, n2, n2)
    x = [eye + p + q + r[:c] for p, q, r in zip(n, n2, n34)]
    n4 = [r[c:] for r in n34]
    xn = many(pair, x, n4, n4)
    x = [u + v[:c] for u, v in zip(x, xn)]
    x = many(lambda u, v: u + v, x, many(_bdot, x, [v[c:] for v in xn]))
    for mask in (low32, low64):
        r = many(_bdot, [y * mask for y in a], x)
        x = many(lambda u, v: u - v, x, many(_bdot, x, r))
    return x


def _gdn_gates(ab, alog, dtb):
    g = -jnp.exp(alog) * _softplus(ab + dtb)
    return g, _sigmoid(ab)


def _gdn_prep_kernel(x_ref, prev_ref, ab_ref, cw_ref, alog_ref, dtb_ref,
                     u_ref, w_ref, qg_ref, kg_ref, at_ref, eg_ref, cv_ref, act_ref):
    c = GDN_CHUNK
    rows = x_ref.shape[0]
    x = x_ref[...]
    halo = prev_ref.shape[0]
    prev = jnp.where(pl.program_id(1) == 0, jnp.zeros_like(prev_ref), prev_ref[...])
    xcat = jnp.concatenate([prev, x], axis=0)
    ti = lax.broadcasted_iota(jnp.int32, (rows, rows + halo), 0)
    ui = lax.broadcasted_iota(jnp.int32, (rows, rows + halo), 1)
    xf = x.astype(F32)
    acc = xf * cw_ref[CONV_W - 1:CONV_W, :]
    for i in range(CONV_W - 1):
        shift = (ui == ti + (halo - (CONV_W - 1) + i)).astype(BF16)
        acc = acc + _dot(shift, xcat) * cw_ref[i:i + 1, :]
    cv_ref[0] = xf[rows - (CONV_W - 1):, :]
    act_ref[...] = _silu(acc)

    g_all, beta_all = _gdn_gates(ab_ref[...], alog_ref[...], dtb_ref[...])
    lower, strict, eye, diag16, low32, low64 = _chunk_masks(c)
    gc_all, gr_all = [], []
    for ck in range(rows // c):
        gc_ck = _dot_hi(lower.astype(F32), g_all[ck * c:(ck + 1) * c])
        gc_all.append(gc_ck)
        gr_all.append(_columns(gc_ck))
        eg_ref[ck] = jnp.exp(gc_ck[c - 1:c, :])
    chains = [(ck, h) for ck in range(rows // c) for h in range(GDN_HEADS)]
    rs = [slice(ck * c, (ck + 1) * c) for ck, _ in chains]
    sl = [slice(h * GDN_DK, (h + 1) * GDN_DK) for _, h in chains]
    nc = range(len(chains))
    q = [_l2norm(act_ref[rs[i], sl[i]]) * (GDN_DK ** -0.5) for i in nc]
    k = [_l2norm(act_ref[rs[i], GDN_QK_W + sl[i].start:GDN_QK_W + sl[i].stop]) for i in nc]
    v = [act_ref[rs[i], 2 * GDN_QK_W + sl[i].start:2 * GDN_QK_W + sl[i].stop] for i in nc]
    beta = [beta_all[rs[i], GDN_HEADS + h:GDN_HEADS + h + 1] for i, (_, h) in enumerate(chains)]
    gc = [gc_all[ck][:, h:h + 1] for ck, h in chains]
    gr = [gr_all[ck][h:h + 1, :c] for ck, h in chains]
    decay = [jnp.exp(jnp.where(lower, gc[i] - gr[i], -jnp.inf)) for i in nc]
    exp_g = [jnp.exp(x) for x in gc]
    kb = [k[i] * beta[i] for i in nc]
    kbf = [x.astype(BF16) for x in k]
    kq = [_dot_nt(jnp.concatenate([kb[i].astype(BF16), q[i].astype(BF16)], axis=0), kbf[i]) for i in nc]
    a = [kq[i][:c] * jnp.where(strict, decay[i], 0.0) for i in nc]
    attn = [kq[i][c:] * decay[i] for i in nc]
    t = _unit_lower_inverse(a, eye, diag16, low32, low64)
    uw = [_bdot(t[i], jnp.concatenate([v[i] * beta[i], kb[i] * exp_g[i]], axis=1)) for i in nc]
    for i in nc:
        u_ref[rs[i], sl[i]] = uw[i][:, :GDN_DV]
        w_ref[rs[i], sl[i]] = uw[i][:, GDN_DV:].astype(BF16)
        at_ref[rs[i], sl[i]] = jnp.concatenate([attn[i], jnp.zeros((c, GDN_DK - c), F32)], axis=1).astype(BF16)
        qg_ref[rs[i], sl[i]] = (q[i] * exp_g[i]).astype(BF16)
        kg_ref[rs[i], sl[i]] = (k[i] * jnp.exp(gc[i][c - 1:c, :] - gc[i])).astype(BF16)


def _gdn_scan_kernel(u_ref, w_ref, qg_ref, kg_ref, at_ref, eg_ref, z_ref, nw_ref, o_ref, s_ref):
    c = GDN_CHUNK

    @pl.when(pl.program_id(1) == 0)
    def _():
        s_ref[...] = jnp.zeros_like(s_ref)

    nw = nw_ref[...]
    chains = [(q, h) for q in range(u_ref.shape[0]) for h in range(GDN_HEADS)]
    sl = [slice(h * GDN_DK, (h + 1) * GDN_DK) for _, h in chains]
    nc = range(len(chains))
    s = [s_ref[q, h] for q, h in chains]
    for ck in range(u_ref.shape[1] // c):
        r = slice(ck * c, (ck + 1) * c)
        sb = [x.astype(BF16) for x in s]
        ws = [_dot(w_ref[q, r, sl[i]], sb[i]) for i, (q, _) in enumerate(chains)]
        qs = [_dot(qg_ref[q, r, sl[i]], sb[i]) for i, (q, _) in enumerate(chains)]
        vnb = [(u_ref[q, r, sl[i]] - ws[i]).astype(BF16) for i, (q, _) in enumerate(chains)]
        o = [qs[i] + _dot(at_ref[q, r, h * GDN_DK:h * GDN_DK + c], vnb[i]) for i, (q, h) in enumerate(chains)]
        s = [s[i] * eg_ref[q, ck][:, h:h + 1] + _dot_tn(kg_ref[q, r, sl[i]], vnb[i])
             for i, (q, h) in enumerate(chains)]
        for i, (q, _) in enumerate(chains):
            o_ref[q, r, sl[i]] = (_rms(o[i]) * nw * _silu(z_ref[q, r, sl[i]].astype(F32))).astype(BF16)
    for i, (q, h) in enumerate(chains):
        s_ref[q, h] = s[i]


def _gdn_prompt(p_gdn, p_ab, batch, seq, conv_w, a_log, dt_bias, gdn_norm_w):
    c = GDN_CHUNK
    rows = GDN_PREP_ROWS
    t = batch * seq
    nt = seq // rows
    alog = jnp.pad(a_log.astype(F32), (0, LANES - GDN_HEADS)).reshape(1, LANES)
    dtb = jnp.pad(dt_bias.astype(F32), (0, LANES - GDN_HEADS)).reshape(1, LANES)
    wide = lambda: pl.BlockSpec((rows, GDN_V_W), lambda b, j: (b * nt + j, 0))
    u, w, qg, kg, at, eg, conv_new = pl.pallas_call(
        _gdn_prep_kernel,
        grid=(batch, nt),
        in_specs=[pl.BlockSpec((rows, CONV_CH), lambda b, j: (b * nt + j, 0)),
                  pl.BlockSpec((16, CONV_CH), lambda b, j: (jnp.maximum((b * nt + j) * (rows // 16) - 1, 0), 0)),
                  pl.BlockSpec((rows, LANES), lambda b, j: (b * nt + j, 0)),
                  _resident((CONV_W, CONV_CH)), _resident((1, LANES)), _resident((1, LANES))],
        out_specs=[wide(), wide(), wide(), wide(), wide(),
                   pl.BlockSpec((rows // c, 1, LANES), lambda b, j: (b * nt + j, 0, 0)),
                   pl.BlockSpec((1, CONV_W - 1, CONV_CH), lambda b, j: (b, 0, 0))],
        out_shape=[jax.ShapeDtypeStruct((t, GDN_V_W), F32)] + [jax.ShapeDtypeStruct((t, GDN_V_W), BF16)] * 4
        + [jax.ShapeDtypeStruct((t // c, 1, LANES), F32),
           jax.ShapeDtypeStruct((batch, CONV_W - 1, CONV_CH), F32)],
        scratch_shapes=[pltpu.VMEM((rows, CONV_CH), F32)],
        compiler_params=_params(("parallel", "arbitrary")),
        name="gdn_prep",
    )(p_gdn, p_gdn, p_ab, conv_w.astype(F32), alog, dtb)
    srows = GDN_SCAN_ROWS
    n = seq // srows
    nq = CONV_CH // GDN_V_W
    nseq = GDN_SCAN_SEQS if batch % GDN_SCAN_SEQS == 0 else 1
    per_seq = lambda a: a.reshape((batch, seq // (t // a.shape[0])) + a.shape[1:])
    blk = lambda: pl.BlockSpec((nseq, srows, GDN_V_W), lambda b, j: (b, j, 0))
    o, s_new = pl.pallas_call(
        _gdn_scan_kernel,
        grid=(batch // nseq, n),
        in_specs=[blk(), blk(), blk(), blk(), blk(),
                  pl.BlockSpec((nseq, srows // c, 1, LANES), lambda b, j: (b, j, 0, 0)),
                  pl.BlockSpec((nseq, srows, GDN_V_W), lambda b, j: (b, j, nq)),
                  _resident((1, GDN_DV))],
        out_specs=[blk(), pl.BlockSpec((nseq, GDN_HEADS, GDN_DK, GDN_DV), lambda b, j: (b, 0, 0, 0))],
        out_shape=[jax.ShapeDtypeStruct((batch, seq, GDN_V_W), BF16),
                   jax.ShapeDtypeStruct((batch, GDN_HEADS, GDN_DK, GDN_DV), F32)],
        compiler_params=_params(("parallel", "arbitrary")),
        name="gdn_scan",
    )(per_seq(u), per_seq(w), per_seq(qg), per_seq(kg), per_seq(at), per_seq(eg), per_seq(p_gdn),
      gdn_norm_w.astype(F32).reshape(1, GDN_DV))
    return o.reshape(t, GDN_V_W), s_new, conv_new


def _gdn_sample_kernel(x_ref, ab_ref, sc_ref, cw_ref, alog_ref, dtb_ref, nw_ref, s_ref, o_ref, so_ref, sco_ref):
    nb = x_ref.shape[0]
    x = x_ref[:, :CONV_CH]
    acc = x * cw_ref[CONV_W - 1:CONV_W, :]
    for i in range(CONV_W - 1):
        acc = acc + sc_ref[i] * cw_ref[i:i + 1, :]
    for i in range(CONV_W - 2):
        sco_ref[i] = sc_ref[i + 1]
    sco_ref[CONV_W - 2] = x
    u = _silu(acc)
    g_all, beta_all = _gdn_gates(ab_ref[...], alog_ref[...], dtb_ref[...])
    eg_all = jnp.exp(g_all)
    nw = nw_ref[...]
    for h in range(GDN_HEADS):
        sl = slice(h * GDN_DK, (h + 1) * GDN_DK)
        q = _l2norm(u[:, sl]) * (GDN_DK ** -0.5)
        k = _l2norm(u[:, GDN_QK_W + h * GDN_DK:GDN_QK_W + (h + 1) * GDN_DK])
        v = u[:, 2 * GDN_QK_W + h * GDN_DV:2 * GDN_QK_W + (h + 1) * GDN_DV]
        beta = beta_all[:, GDN_HEADS + h:GDN_HEADS + h + 1]
        eg = eg_all[:, h:h + 1]
        qk_dot = jnp.sum(q * k, axis=-1, keepdims=True)
        kt = _columns(k)
        kq = jnp.concatenate([k, q], axis=0).astype(BF16)
        rows = []
        for j in range(nb):
            s = s_ref[j, h]
            kcol = kt[:, j:j + 1]
            both = _dot(kq, s.astype(BF16))
            ks, qs = both[j:j + 1], both[nb + j:nb + j + 1]
            ej = eg[j:j + 1]
            v_new = beta[j:j + 1] * (v[j:j + 1] - ej * ks)
            rows.append(ej * qs + qk_dot[j:j + 1] * v_new)
            so_ref[j, h] = s * ej + kcol * v_new
        o = jnp.concatenate(rows, axis=0)
        z = x_ref[:, CONV_CH + h * GDN_DV:CONV_CH + (h + 1) * GDN_DV]
        o_ref[:, sl] = (_rms(o) * nw * _silu(z)).astype(BF16)


def _gdn_sample(p_gdn_s, p_ab_s, state, conv_state, conv_w, a_log, dt_bias, gdn_norm_w):
    ts = p_gdn_s.shape[0]
    sb = SAMPLE_TILE
    alog = jnp.pad(a_log.astype(F32), (0, LANES - GDN_HEADS)).reshape(1, LANES)
    dtb = jnp.pad(dt_bias.astype(F32), (0, LANES - GDN_HEADS)).reshape(1, LANES)
    sc = jnp.swapaxes(conv_state.astype(F32), 0, 1)
    st = pl.BlockSpec((sb, GDN_HEADS, GDN_DK, GDN_DV), lambda i: (i, 0, 0, 0))
    scs = pl.BlockSpec((CONV_W - 1, sb, CONV_CH), lambda i: (0, i, 0))
    o, s_new, sc_new = pl.pallas_call(
        _gdn_sample_kernel,
        grid=(ts // sb,),
        in_specs=[pl.BlockSpec((sb, _GDN_W), lambda i: (i, 0)), pl.BlockSpec((sb, LANES), lambda i: (i, 0)), scs,
                  _resident((CONV_W, CONV_CH)), _resident((1, LANES)), _resident((1, LANES)),
                  _resident((1, GDN_DV)), st],
        out_specs=[pl.BlockSpec((sb, GDN_V_W), lambda i: (i, 0)), st, scs],
        out_shape=[jax.ShapeDtypeStruct((ts, GDN_V_W), BF16), jax.ShapeDtypeStruct(state.shape, F32),
                   jax.ShapeDtypeStruct(sc.shape, F32)],
        compiler_params=_params(("parallel",)),
        name="gdn_sample",
    )(p_gdn_s, p_ab_s, sc, conv_w.astype(F32), alog, dtb, gdn_norm_w.astype(F32).reshape(1, GDN_DV), state)
    return o, s_new, jnp.swapaxes(sc_new, 0, 1)


_NO_EXPERT = -1e30


def _merge_kernel(oa0_ref, ob0_ref, gate0_ref, x0_ref, oa1_ref, ob1_ref, gate1_ref, x1in_ref,
                  wa_ref, wb_ref, wo_ref, nw_ref, wr_ref, br_ref, x1_ref, h2_ref, ti_ref, tw_ref, *, n_first):
    first = pl.program_id(0) < n_first
    pick = lambda a, b: jnp.where(first, a[...], b[...])
    gate = pick(gate0_ref, gate1_ref)
    ya = _dot(pick(oa0_ref, oa1_ref), wa_ref[...])
    yb = _dot(pick(ob0_ref, ob1_ref), wb_ref[...])
    ga = gate[:, :D_MODEL].astype(F32)
    gb = gate[:, D_MODEL:].astype(F32)
    m = _sigmoid(ga) * ya + _sigmoid(gb) * yb
    x1 = pick(x0_ref, x1in_ref) + _dot(m.astype(BF16), wo_ref[...])
    x1_ref[...] = x1
    h2 = _rms(x1) * nw_ref[...]
    for j, plane in enumerate(_pack_rows(h2)):
        h2_ref[:, j, :] = plane
    lg = _dot_split(h2, wr_ref[...]) + br_ref[...]
    lane = lax.broadcasted_iota(jnp.int32, lg.shape, 1).astype(F32)
    vals, idxs = [], []
    for _ in range(TOP_K):
        top = jnp.max(lg, axis=-1, keepdims=True)
        idx = jnp.min(jnp.where(lg == top, lane, float(LANES)), axis=-1, keepdims=True)
        vals.append(top)
        idxs.append(idx)
        lg = jnp.where(lane == idx, _NO_EXPERT, lg)
    es = [jnp.exp(v - vals[0]) for v in vals]
    inv_total = 1.0 / functools.reduce(lambda a, b: a + b, es)
    ti = jnp.zeros_like(lg)
    tw = jnp.zeros_like(lg)
    for k in range(TOP_K):
        ti = jnp.where(lane == float(k), idxs[k], ti)
        tw = jnp.where(lane == float(k), es[k] * inv_total, tw)
    ti_ref[...] = ti.astype(jnp.int32)
    tw_ref[...] = tw


def _merge(group0, group1, weights):
    r0 = group0[3].shape[0]
    tm = _pick_tile(r0, (MERGE_ROWS, 128, 64, 32, 16, 8))
    pad = -group1[3].shape[0] % tm
    group1 = tuple(jnp.pad(a, ((0, pad), (0, 0))) for a in group1)
    r1 = group1[3].shape[0]
    n0, total = r0 // tm, r0 + r1
    widths = (RET_V_W, GDN_V_W, 2 * D_MODEL, D_MODEL)
    specs0 = [pl.BlockSpec((tm, n), lambda i: (jnp.minimum(i, n0 - 1), 0)) for n in widths]
    specs1 = [pl.BlockSpec((tm, n), lambda i: (jnp.maximum(i - n0, 0), 0)) for n in widths]
    out = lambda n: pl.BlockSpec((tm, n), lambda i: (i, 0))
    sub = PACK_PLANES
    sq = (D_MODEL, D_MODEL)
    return pl.pallas_call(
        functools.partial(_merge_kernel, n_first=n0),
        grid=(total // tm,),
        in_specs=specs0 + specs1 + [_resident(sq), _resident(sq), _resident(sq), _resident((1, D_MODEL)),
                                    _resident((D_MODEL, LANES)), _resident((1, LANES))],
        out_specs=[out(D_MODEL), pl.BlockSpec((tm, sub, LANES), lambda i: (i, 0, 0)), out(LANES), out(LANES)],
        out_shape=[jax.ShapeDtypeStruct((total, D_MODEL), F32), jax.ShapeDtypeStruct((total, sub, LANES), jnp.uint32),
                   jax.ShapeDtypeStruct((total, LANES), jnp.int32), jax.ShapeDtypeStruct((total, LANES), F32)],
        compiler_params=_params(("parallel",)),
        name="merge",
    )(*group0, *group1, *weights)


def _merge_weights(w_a, w_b, w_o, ffn_norm_w, w_router, b_router):
    wr = jnp.pad(w_router.astype(F32), ((0, 0), (0, LANES - N_EXPERTS)))
    br = jnp.pad(b_router.astype(F32), (0, LANES - N_EXPERTS), constant_values=_NO_EXPERT).reshape(1, LANES)
    return (w_a.astype(BF16), w_b.astype(BF16), w_o.astype(BF16), ffn_norm_w.astype(F32).reshape(1, D_MODEL), wr, br)


def _route(top_i, gate, n_tokens):
    rows = MOE_ROWS
    n = n_tokens * TOP_K
    flat_e = top_i.reshape(n).astype(jnp.int32)
    bits = max(1, (n - 1).bit_length())
    assert bits + (N_EXPERTS - 1).bit_length() <= 31
    order = lax.sort((flat_e << bits) | jnp.arange(n, dtype=jnp.int32), is_stable=False) & ((1 << bits) - 1)
    counts = jnp.sum((flat_e[:, None] == jnp.arange(N_EXPERTS, dtype=jnp.int32)[None, :]).astype(jnp.int32), axis=0)
    start = jnp.cumsum(counts) - counts
    pcounts = (counts + rows - 1) // rows * rows
    pend = jnp.cumsum(pcounts)
    pstart = pend - pcounts
    nb = -(-n // rows) + N_EXPERTS
    blk = jnp.arange(nb, dtype=jnp.int32)
    block_e = jnp.minimum(jnp.sum((pend[None, :] <= (blk * rows)[:, None]).astype(jnp.int32), axis=1),
                          N_EXPERTS - 1).astype(jnp.int32)
    nb_used = (pend[-1] // rows).astype(jnp.int32).reshape(1)
    is_e = block_e[:, None] == jnp.arange(N_EXPERTS, dtype=jnp.int32)[None, :]
    of_block = lambda table: jnp.sum(jnp.where(is_e, table[None, :], 0), axis=1)
    within = (blk * rows - of_block(pstart))[:, None] + jnp.arange(rows, dtype=jnp.int32)[None, :]
    valid = jnp.logical_and(within < of_block(counts)[:, None], (blk < nb_used[0])[:, None])
    flat = order[jnp.clip(of_block(start)[:, None] + within, 0, n - 1)]
    spare = n + (blk % MOE_RING)[:, None] * rows + jnp.arange(rows, dtype=jnp.int32)[None, :]
    src = jnp.where(valid, flat // TOP_K, 0)
    dst = jnp.where(valid, flat, spare)
    ahead = lambda k: jnp.concatenate([src[k:]] + [src[-1:]] * k, axis=0)
    slab = jnp.concatenate([src, dst, ahead(1), ahead(2)], axis=1)
    row_w = jnp.where(valid, gate[:, :TOP_K].reshape(n)[flat], 0.0)
    row_w = jnp.broadcast_to(row_w[:, :, None], (nb, rows, LANES))
    return block_e, nb_used, slab, row_w


def _expert_kernel(be_ref, nbu_ref, slab_ref, h_ref, roww_ref, wgu_ref, bgu_ref, wd_ref, bd_ref, y_ref,
                   idx0_ref, idx1_ref, idx2_ref, xbuf_ref, ybuf_ref, wgu_bf_ref, wd_bf_ref, isem, gsem, ssem):
    rows = MOE_ROWS
    ring = MOE_RING
    idx_refs = (idx0_ref, idx1_ref, idx2_ref)
    assert len(idx_refs) == ring
    sub = PACK_PLANES
    i = pl.program_id(0)
    nbu = nbu_ref[0]
    slot = i % ring
    active = i < nbu
    n_real = y_ref.shape[0] - ring * rows

    def slab_copy(blk, sl):
        return pltpu.make_async_copy(slab_ref.at[blk], idx_refs[sl], isem.at[sl])

    def gather_row(tok, sl, r):
        return pltpu.make_async_copy(h_ref.at[tok], xbuf_ref.at[sl, :, r, :], gsem.at[sl])

    def scatter_row(sl, r, f):
        return pltpu.make_async_copy(ybuf_ref.at[sl, :, r, :], y_ref.at[f], ssem.at[sl])

    def gather_wait(sl):
        pltpu.make_async_copy(xbuf_ref.at[sl], xbuf_ref.at[sl], gsem.at[sl]).wait()

    def scatter_wait(sl):
        pltpu.make_async_copy(ybuf_ref.at[sl], ybuf_ref.at[sl], ssem.at[sl]).wait()

    @pl.when(i == 0)
    def _():
        slab_copy(0, 0).start()
        ybuf_ref[...] = jnp.zeros_like(ybuf_ref)
        for sl in range(ring):
            def fill(r, carry, sl=sl):
                scatter_row(sl, r, n_real + sl * rows + r).start()
                return carry
            lax.fori_loop(0, rows, fill, 0)

    def block(sl):
        nxt, nxt2 = (sl + 1) % ring, (sl + 2) % ring
        idx_ref = idx_refs[sl]
        slab_copy(i, sl).wait()

        @pl.when(i + 1 < nbu)
        def _():
            slab_copy(i + 1, nxt).start()

        if sl == 0:
            @pl.when(i == 0)
            def _():
                def first(r, carry):
                    gather_row(idx_ref[r], 0, r).start()
                    gather_row(idx_ref[2 * rows + r], 1, r).start()
                    return carry
                lax.fori_loop(0, rows, first, 0)

        changed = jnp.logical_or(i == 0, be_ref[i] != be_ref[jnp.maximum(i - 1, 0)])

        @pl.when(changed)
        def _():
            wgu_bf_ref[...] = wgu_ref[0].astype(BF16)
            wd_bf_ref[...] = wd_ref[0].astype(BF16)

        gather_wait(sl)
        scatter_wait(sl)
        lo, hi = _unpack_planes([xbuf_ref[sl, j] for j in range(sub)])
        xb = jnp.concatenate(lo + hi, axis=1).astype(BF16)
        hb = _dot(xb, wgu_bf_ref[...]) + bgu_ref[0]
        for r in range(rows):
            gather_row(idx_ref[3 * rows + r], nxt2, r).start(priority=r % 2)
        glu = jnp.minimum(hb[:, :D_FF], SWIGLU_LIMIT)
        lin = jnp.clip(hb[:, D_FF:], -SWIGLU_LIMIT, SWIGLU_LIMIT)
        act = (glu * _sigmoid(SWIGLU_ALPHA * glu) * (lin + 1.0)).astype(BF16)
        yv = (_dot(act, wd_bf_ref[...]) + bd_ref[0]) * roww_ref[0, :, 0:1]
        for j, plane in enumerate(_pack_rows(yv)):
            ybuf_ref[sl, j] = plane
        for r in range(rows):
            scatter_row(sl, r, idx_ref[rows + r]).start(priority=(r + 1) % 2)

        @pl.when(i == nbu - 1)
        def _():
            for s in (sl, nxt, nxt2):
                scatter_wait(s)
            gather_wait(nxt)
            gather_wait(nxt2)

    for sl in range(ring):
        pl.when(jnp.logical_and(active, slot == sl))(functools.partial(block, sl))


def _experts(h2, t, block_e, nb_used, slab, row_w, w_gate_up, b_gate_up, w_down, b_down):
    rows = MOE_ROWS
    ring = MOE_RING
    assert (ring * rows) % TOP_K == 0
    nb = slab.shape[0]
    sub = PACK_PLANES
    grid_spec = pltpu.PrefetchScalarGridSpec(
        num_scalar_prefetch=2,
        grid=(nb,),
        in_specs=[pl.BlockSpec(memory_space=pl.ANY),
                  pl.BlockSpec(memory_space=pl.ANY),
                  pl.BlockSpec((1, rows, LANES), lambda i, be, nbu: (i, 0, 0)),
                  pl.BlockSpec((1, D_MODEL, 2 * D_FF), lambda i, be, nbu: (be[i], 0, 0)),
                  pl.BlockSpec((1, 1, 2 * D_FF), lambda i, be, nbu: (be[i], 0, 0)),
                  pl.BlockSpec((1, D_FF, D_MODEL), lambda i, be, nbu: (be[i], 0, 0)),
                  pl.BlockSpec((1, 1, D_MODEL), lambda i, be, nbu: (be[i], 0, 0))],
        out_specs=pl.BlockSpec(memory_space=pl.ANY),
        scratch_shapes=[pltpu.SMEM((slab.shape[1],), jnp.int32)] * ring + [
                        pltpu.VMEM((ring, sub, rows, LANES), jnp.uint32),
                        pltpu.VMEM((ring, sub, rows, LANES), jnp.uint32),
                        pltpu.VMEM((D_MODEL, 2 * D_FF), BF16),
                        pltpu.VMEM((D_FF, D_MODEL), BF16),
                        pltpu.SemaphoreType.DMA((ring,)),
                        pltpu.SemaphoreType.DMA((ring,)),
                        pltpu.SemaphoreType.DMA((ring,))])
    return pl.pallas_call(
        _expert_kernel,
        grid_spec=grid_spec,
        out_shape=jax.ShapeDtypeStruct((t * TOP_K + ring * rows, sub, LANES), jnp.uint32),
        compiler_params=_params(("arbitrary",)),
        name="experts",
    )(block_e, nb_used, slab, h2, row_w, w_gate_up, b_gate_up.reshape(N_EXPERTS, 1, 2 * D_FF), w_down,
      b_down.reshape(N_EXPERTS, 1, D_MODEL))


def _combine_kernel(y_ref, x1_ref, nw_ref, o_ref, sum_ref, *, final):
    lo, hi = None, None
    for k in range(TOP_K):
        (l,), (h,) = _unpack_planes([y_ref[:, k]])
        lo, hi = (l, h) if lo is None else (lo + l, hi + h)
    sum_ref[:, :PACK_PLANES, :] = lo
    sum_ref[:, PACK_PLANES:, :] = hi
    acc = x1_ref[...] + jnp.concatenate([sum_ref[:, j, :] for j in range(D_MODEL // LANES)], axis=1)
    o_ref[...] = _rms(acc) * nw_ref[...] if final else acc


def _combine(y, x1, row0, rows, norm_w, final):
    tm = _row_tile(rows, row0, (512, 256, 128, 64, 32, 16, 8))
    off = row0 // tm
    sub = PACK_PLANES
    y4 =y.reshape(y.shape[0] // TOP_K, TOP_K, sub, LANES)
    return pl.pallas_call(
        functools.partial(_combine_kernel, final=final),
        grid=(rows // tm,),
        in_specs=[pl.BlockSpec((tm, TOP_K, sub, LANES), lambda i: (off + i, 0, 0, 0)),
                  pl.BlockSpec((tm, D_MODEL), lambda i: (off + i, 0)), _resident((1, D_MODEL))],
        out_specs=pl.BlockSpec((tm, D_MODEL), lambda i: (i, 0)),
        out_shape=jax.ShapeDtypeStruct((rows, D_MODEL), F32),
        scratch_shapes=[pltpu.VMEM((tm, D_MODEL // LANES, LANES), F32)],
        compiler_params=_params(("parallel",)),
        name="combine",
    )(y4, x1, norm_w.astype(F32).reshape(1, D_MODEL))


def kernel(x_prompt, x_sample, state_ret, state_gdn, state_conv, attn_norm_w, w_in, conv_w, a_log, dt_bias, gdn_norm_w, w_branch_a, w_branch_b, w_out, ffn_norm_w, w_router, b_router, w_gate_up, b_gate_up, w_down, b_down, final_norm_w):
    bp, lp, d = x_prompt.shape
    bs, ls, _ = x_sample.shape
    assert ls == 1 and d == D_MODEL and lp % RET_CHUNK == 0 and bs % SAMPLE_TILE == 0
    depth = w_in.shape[0]
    tp = bp * lp
    t = tp + bs
    xp, xs = x_prompt.reshape(tp, d).astype(F32), x_sample.reshape(bs, d).astype(F32)
    rp, gp, cp, rs, gs, cs = [], [], [], [], [], []
    for l in range(depth):
        wb = jnp.pad(w_in[l], ((0, 0), (0, -w_in.shape[2] % LANES))).astype(BF16)
        pp_ret, pp_gdn, pp_gate, pp_ab = _inproj(xp, attn_norm_w[l], wb)
        ps_ret, ps_gdn, ps_gate, ps_ab = _inproj(xs, attn_norm_w[l], wb)
        op_ret, s_ret_p = _ret_prompt(pp_ret, bp, lp)
        os_ret, s_ret_s = _ret_sample(ps_ret.astype(F32), state_ret[l].astype(F32))
        op_gdn, s_gdn_p, conv_p = _gdn_prompt(pp_gdn, pp_ab, bp, lp, conv_w[l], a_log[l], dt_bias[l], gdn_norm_w[l])
        os_gdn, s_gdn_s, conv_s = _gdn_sample(ps_gdn.astype(F32), ps_ab, state_gdn[l].astype(F32), state_conv[l],
                                              conv_w[l], a_log[l], dt_bias[l], gdn_norm_w[l])
        mw = _merge_weights(w_branch_a[l], w_branch_b[l], w_out[l], ffn_norm_w[l], w_router[l], b_router[l])
        x1, h2, top_i, gate = _merge((op_ret, op_gdn, pp_gate, xp), (os_ret, os_gdn, ps_gate, xs), mw)
        block_e, nb_used, slab, row_w = _route(top_i[:t, :TOP_K], gate[:t], t)
        y = _experts(h2, t, block_e, nb_used, slab, row_w, w_gate_up[l], b_gate_up[l], w_down[l], b_down[l])
        last = l == depth - 1
        norm_w = final_norm_w if last else jnp.ones((d,), F32)
        xp = _combine(y, x1, 0, tp, norm_w, last)
        xs = _combine(y, x1, tp, bs, norm_w, last)
        rp.append(s_ret_p); gp.append(s_gdn_p); cp.append(conv_p)
        rs.append(s_ret_s); gs.append(s_gdn_s); cs.append(conv_s)
    y_prompt = xp.reshape(bp, lp, d).astype(x_prompt.dtype)
    y_sample = xs.reshape(bs, ls, d).astype(x_sample.dtype)
    return (y_prompt, y_sample,
            jnp.stack(rp).astype(state_ret.dtype), jnp.stack(gp).astype(state_gdn.dtype),
            jnp.stack(cp).astype(state_conv.dtype),
            jnp.stack(rs).astype(state_ret.dtype), jnp.stack(gs).astype(state_gdn.dtype),
            jnp.stack(cs).astype(state_conv.dtype))
```

```python
import functools
import math

import numpy as np
import jax
import jax.numpy as jnp
from jax import lax
from jax.experimental import pallas as pl
from jax.experimental.pallas import tpu as pltpu

F32 = jnp.float32
BF16 = jnp.bfloat16
HIGHEST = lax.Precision.HIGHEST

D_MODEL = 1024
PAST_LEN = 16384
RET_HEADS, RET_DK, RET_DV = 4, 128, 256
RET_QK_W, RET_V_W = RET_HEADS * RET_DK, RET_HEADS * RET_DV
RET_CHUNK = 128
ROPE_BASE = 10000.0
GDN_HEADS, GDN_DK, GDN_DV = 8, 128, 128
GDN_QK_W, GDN_V_W = GDN_HEADS * GDN_DK, GDN_HEADS * GDN_DV
GDN_CHUNK = 64
CONV_W = 4
CONV_CH = 2 * GDN_QK_W + GDN_V_W
N_EXPERTS = 32
TOP_K = 4
D_FF = D_MODEL
SWIGLU_LIMIT = 7.0
SWIGLU_ALPHA = 1.702
NORM_EPS = 1e-6

_RET_W = 2 * RET_QK_W + 2 * RET_V_W
_GDN_W = CONV_CH + GDN_V_W
_AB_OFF = _RET_W + _GDN_W
_GATE_OFF = _AB_OFF + 2 * GDN_HEADS

LANES = 128
VMEM_LIMIT = 56 * 1024 * 1024
MERGE_ROWS = 512
MOE_ROWS = 256
MOE_RING = 3
SAMPLE_TILE = 16
RET_STEP_ROWS = 256
RET_STEP_SEQS = 4
GDN_PREP_ROWS = 128
GDN_SCAN_ROWS = 256
GDN_SCAN_SEQS = 4


def _pick_tile(n, candidates):
    for c in candidates:
        if n % c == 0:
            return c
    raise ValueError(f"no tile in {candidates} divides {n}")


def _params(sem, vmem=VMEM_LIMIT):
    return pltpu.CompilerParams(dimension_semantics=sem, vmem_limit_bytes=vmem)


def _resident(shape):
    nd = len(shape)
    return pl.BlockSpec(shape, lambda *_: (0,) * nd, pipeline_mode=pl.Buffered(1))


def _silu(x):
    return x * (1.0 / (1.0 + jnp.exp(-x)))


def _sigmoid(x):
    return 1.0 / (1.0 + jnp.exp(-x))


def _softplus(x):
    return jnp.maximum(x, 0.0) + jnp.log1p(jnp.exp(-jnp.abs(x)))


def _rms(x):
    return x * lax.rsqrt(jnp.mean(x * x, axis=-1, keepdims=True) + NORM_EPS)


def _dot(a, b):
    return jnp.dot(a, b, preferred_element_type=F32)


def _dot_nt(a, b):
    return lax.dot_general(a, b, (((1,), (1,)), ((), ())), preferred_element_type=F32)


def _dot_tn(a, b):
    return lax.dot_general(a, b, (((0,), (0,)), ((), ())), preferred_element_type=F32)


def _dot_hi(a, b):
    return jnp.dot(a, b, preferred_element_type=F32, precision=HIGHEST)


PACK_PLANES = D_MODEL // (2 * LANES)


def _pack_rows(x):
    as_bits = lambda v: pltpu.bitcast(v.astype(BF16).astype(F32), jnp.uint32)
    planes = []
    for j in range(PACK_PLANES):
        lo = as_bits(x[:, j * LANES:(j + 1) * LANES])
        hi = as_bits(x[:, (j + PACK_PLANES) * LANES:(j + PACK_PLANES + 1) * LANES])
        planes.append(jnp.bitwise_or(hi, lax.shift_right_logical(lo, jnp.uint32(16))))
    return planes


def _unpack_planes(planes):
    lo = [pltpu.bitcast(lax.shift_left(p, jnp.uint32(16)), F32) for p in planes]
    hi = [pltpu.bitcast(jnp.bitwise_and(p, jnp.uint32(0xFFFF0000)), F32) for p in planes]
    return lo, hi


def _dot_split(a, b):
    a_hi, b_hi = a.astype(BF16), b.astype(BF16)
    a_lo = (a - a_hi.astype(F32)).astype(BF16)
    b_lo = (b - b_hi.astype(F32)).astype(BF16)
    return _dot(a_hi, b_hi) + (_dot(a_hi, b_lo) + _dot(a_lo, b_hi))


def _row_tile(rows, row0, candidates):
    return _pick_tile(math.gcd(rows, row0) if row0 else rows, candidates)


def _inproj_kernel(x_ref, nw_ref, w_ref, oret_ref, ogdn_ref, ogate_ref, oab_ref):
    h = (_rms(x_ref[...]) * nw_ref[...]).astype(BF16)
    oret_ref[...] = _dot(h, w_ref[:, :_RET_W]).astype(BF16)
    ogdn_ref[...] = _dot(h, w_ref[:, _RET_W:_AB_OFF]).astype(BF16)
    tail = _dot(h, w_ref[:, _AB_OFF:])
    oab_ref[...] = tail[:, :LANES]
    ogate_ref[...] = tail[:, _GATE_OFF - _AB_OFF:_GATE_OFF - _AB_OFF + 2 * D_MODEL].astype(BF16)


def _inproj(x, norm_w, wb):
    rows = x.shape[0]
    tm = _pick_tile(rows, (256, 128, 64, 32, 16, 8))
    row = lambda n: pl.BlockSpec((tm, n), lambda i: (i, 0))
    return pl.pallas_call(
        _inproj_kernel,
        grid=(rows // tm,),
        in_specs=[row(D_MODEL), _resident((1, D_MODEL)), _resident(wb.shape)],
        out_specs=[row(_RET_W), row(_GDN_W), row(2 * D_MODEL), row(LANES)],
        out_shape=[jax.ShapeDtypeStruct((rows, _RET_W), BF16), jax.ShapeDtypeStruct((rows, _GDN_W), BF16),
                   jax.ShapeDtypeStruct((rows, 2 * D_MODEL), BF16), jax.ShapeDtypeStruct((rows, LANES), F32)],
        compiler_params=_params(("parallel",)),
        name="inproj",
    )(x, norm_w.reshape(1, D_MODEL), wb)


def _ret_log_gamma():
    return np.log1p(-np.exp2(-5.0 - np.arange(RET_HEADS, dtype=np.float64)))


def _rope_tables(pos):
    half = RET_DK // 2
    inv = 1.0 / (ROPE_BASE ** (jnp.arange(half, dtype=F32) / half))
    ang = pos.astype(F32)[:, None] * inv[None, :]
    cos, sin = jnp.cos(ang), jnp.sin(ang)
    return jnp.concatenate([cos, cos], axis=-1), jnp.concatenate([-sin, sin], axis=-1)


def _rotary(x, cos, sin):
    return x * cos + pltpu.roll(x, RET_DK // 2, 1) * sin


def _ret_prompt_kernel(q_ref, k_ref, v_ref, g_ref, cos_ref, sin_ref, dmask_ref, qdec_ref, kdec_ref,
                       o_ref, s_ref, *, gammas):
    @pl.when(pl.program_id(1) == 0)
    def _():
        s_ref[...] = jnp.zeros_like(s_ref)

    c = RET_CHUNK
    chains = [(sq, h) for sq in range(q_ref.shape[0]) for h in range(RET_HEADS)]
    nc = range(len(chains))
    qk = [slice(h * RET_DK, (h + 1) * RET_DK) for _, h in chains]
    vv = [slice(h * RET_DV, (h + 1) * RET_DV) for _, h in chains]
    hd = [h for _, h in chains]
    sq = [s_ for s_, _ in chains]
    s = [s_ref[sq[i], hd[i]] for i in nc]
    for ck in range(q_ref.shape[1] // c):
        r = slice(ck * c, (ck + 1) * c)
        cos, sin = cos_ref[r, :], sin_ref[r, :]
        q = [_rotary(q_ref[sq[i], r, qk[i]].astype(F32), cos, sin) for i in nc]
        k = [_rotary(k_ref[sq[i], r, qk[i]].astype(F32), cos, sin) * (RET_DK ** -0.5) for i in nc]
        v = [v_ref[sq[i], r, vv[i]] for i in nc]
        qb = [x.astype(BF16) for x in q]
        inner = [_dot_nt(qb[i], k[i].astype(BF16)) * dmask_ref[hd[i]] for i in nc]
        cross = [_dot((q[i] * qdec_ref[hd[i]]).astype(BF16), s[i].astype(BF16)) for i in nc]
        upd = [_dot_tn((k[i] * kdec_ref[hd[i]]).astype(BF16), v[i]) for i in nc]
        o = [_dot(inner[i].astype(BF16), v[i]) + cross[i] for i in nc]
        s = [s[i] * gammas[hd[i]] + upd[i] for i in nc]
        for i in nc:
            o_ref[sq[i], r, vv[i]] = (_rms(o[i]) * _silu(g_ref[sq[i], r, vv[i]].astype(F32))).astype(BF16)
    for i in nc:
        s_ref[sq[i], hd[i]] = s[i]


def _ret_prompt(p_ret, batch, seq):
    c = RET_CHUNK
    step = RET_STEP_ROWS if seq % RET_STEP_ROWS == 0 else c
    n = seq // step
    lg = _ret_log_gamma()
    idx = np.arange(c, dtype=np.float64)
    diff = idx[:, None] - idx[None, :]
    dmask = np.where(diff >= 0, np.exp(np.maximum(diff, 0.0)[None] * lg[:, None, None]), 0.0)
    qdec = np.broadcast_to(np.exp((idx + 1.0)[None, :] * lg[:, None])[:, :, None], (RET_HEADS, c, RET_DK))
    kdec = np.broadcast_to(np.exp((c - 1.0 - idx)[None, :] * lg[:, None])[:, :, None], (RET_HEADS, c, RET_DK))
    gammas = tuple(float(g) for g in np.exp(c * lg))
    cos, sin = _rope_tables(jnp.arange(seq, dtype=jnp.int32))
    tab = lambda: _resident((RET_HEADS, c, RET_DK))
    nseq = RET_STEP_SEQS if batch % RET_STEP_SEQS == 0 else 1
    p3 = p_ret.reshape(batch, seq, _RET_W)
    o, s_new = pl.pallas_call(
        functools.partial(_ret_prompt_kernel, gammas=gammas),
        grid=(batch // nseq, n),
        in_specs=[pl.BlockSpec((nseq, step, RET_QK_W), lambda b, j: (b, j, 0)),
                  pl.BlockSpec((nseq, step, RET_QK_W), lambda b, j: (b, j, 1)),
                  pl.BlockSpec((nseq, step, RET_V_W), lambda b, j: (b, j, 1)),
                  pl.BlockSpec((nseq, step, RET_V_W), lambda b, j: (b, j, 2)),
                  pl.BlockSpec((step, RET_DK), lambda b, j: (j, 0)),
                  pl.BlockSpec((step, RET_DK), lambda b, j: (j, 0)),
                  tab(), tab(), tab()],
        out_specs=[pl.BlockSpec((nseq, step, RET_V_W), lambda b, j: (b, j, 0)),
                   pl.BlockSpec((nseq, RET_HEADS, RET_DK, RET_DV), lambda b, j: (b, 0, 0, 0))],
        out_shape=[jax.ShapeDtypeStruct((batch, seq, RET_V_W), BF16),
                   jax.ShapeDtypeStruct((batch, RET_HEADS, RET_DK, RET_DV), F32)],
        compiler_params=_params(("parallel", "arbitrary")),
        name="ret_prompt",
    )(p3, p3, p3, p3, cos, sin, jnp.asarray(dmask, F32), jnp.asarray(qdec, F32), jnp.asarray(kdec, F32))
    return o.reshape(batch * seq, RET_V_W), s_new


def _columns(x):
    n = x.shape[0]
    if n < LANES:
        x = jnp.concatenate([x, jnp.zeros((LANES - n, x.shape[1]), x.dtype)], axis=0)
    return x.T


def _ret_sample_kernel(p_ref, cos_ref, sin_ref, s_ref, o_ref, so_ref, *, gammas):
    cos, sin = cos_ref[...], sin_ref[...]
    nb = p_ref.shape[0]
    for h in range(RET_HEADS):
        qk = slice(h * RET_DK, (h + 1) * RET_DK)
        q = _rotary(p_ref[:, qk], cos, sin)
        k = _rotary(p_ref[:, RET_QK_W + h * RET_DK:RET_QK_W + (h + 1) * RET_DK], cos, sin) * (RET_DK ** -0.5)
        v = p_ref[:, 2 * RET_QK_W + h * RET_DV:2 * RET_QK_W + (h + 1) * RET_DV]
        g = p_ref[:, 2 * RET_QK_W + RET_V_W + h * RET_DV:2 * RET_QK_W + RET_V_W + (h + 1) * RET_DV]
        qk_dot = jnp.sum(q * k, axis=-1, keepdims=True)
        kt = _columns(k)
        qb = q.astype(BF16)
        rows = []
        for j in range(nb):
            s = s_ref[j, h]
            qs = _dot(qb, s.astype(BF16))[j:j + 1]
            rows.append(qk_dot[j:j + 1] * v[j:j + 1] + gammas[h] * qs)
            so_ref[j, h] = s * gammas[h] + kt[:, j:j + 1] * v[j:j + 1]
        o = jnp.concatenate(rows, axis=0)
        o_ref[:, h * RET_DV:(h + 1) * RET_DV] = (_rms(o) * _silu(g)).astype(BF16)


def _ret_sample(p_ret_s, state):
    ts = p_ret_s.shape[0]
    sb = SAMPLE_TILE
    gammas = tuple(float(g) for g in np.exp(_ret_log_gamma()))
    cos, sin = _rope_tables(jnp.full((1,), PAST_LEN, jnp.int32))
    st = pl.BlockSpec((sb, RET_HEADS, RET_DK, RET_DV), lambda i: (i, 0, 0, 0))
    return pl.pallas_call(
        functools.partial(_ret_sample_kernel, gammas=gammas),
        grid=(ts // sb,),
        in_specs=[pl.BlockSpec((sb, _RET_W), lambda i: (i, 0)), _resident((1, RET_DK)), _resident((1, RET_DK)), st],
        out_specs=[pl.BlockSpec((sb, RET_V_W), lambda i: (i, 0)), st],
        out_shape=[jax.ShapeDtypeStruct((ts, RET_V_W), BF16), jax.ShapeDtypeStruct(state.shape, F32)],
        compiler_params=_params(("parallel",)),
        name="ret_sample",
    )(p_ret_s, cos, sin, state)


def _l2norm(x):
    return x * lax.rsqrt(jnp.sum(x * x, axis=-1, keepdims=True) + NORM_EPS)


def _bdot(a, b):
    return _dot(a.astype(BF16), b.astype(BF16))


def _chunk_masks(c):
    ri = lax.broadcasted_iota(jnp.int32, (c, c), 0)
    ci = lax.broadcasted_iota(jnp.int32, (c, c), 1)
    eye = (ri == ci).astype(F32)
    diag16 = (ri // 16 == ci // 16).astype(F32)
    low32 = jnp.logical_and(ri // 32 == ci // 32, ri // 16 > ci // 16).astype(F32)
    low64 = (ri // 32 > ci // 32).astype(F32)
    return ri >= ci, ri > ci, eye, diag16, low32, low64


def _unit_lower_inverse(a, eye, diag16, low32, low64):
    many = lambda f, *ls: [f(*args) for args in zip(*ls)]
    c = eye.shape[0]
    pair = lambda p, q, rhs: _bdot(jnp.concatenate([p, q], axis=0), rhs)
    n = [-(x * diag16) for x in a]
    n2 = many(_bdot, n, n)
    n34 = many(pair, n, n2, n2)
    x = [eye + p + q + r[:c] for p, q, r in zip(n, n2, n34)]
    n4 = [r[c:] for r in n34]
    xn = many(pair, x, n4, n4)
    x = [u + v[:c] for u, v in zip(x, xn)]
    x = many(lambda u, v: u + v, x, many(_bdot, x, [v[c:] for v in xn]))
    for mask in (low32, low64):
        r = many(_bdot, [y * mask for y in a], x)
        x = many(lambda u, v: u - v, x, many(_bdot, x, r))
    return x


def _gdn_gates(ab, alog, dtb):
    g = -jnp.exp(alog) * _softplus(ab + dtb)
    return g, _sigmoid(ab)


def _gdn_prep_kernel(x_ref, prev_ref, ab_ref, cw_ref, alog_ref, dtb_ref,
                     u_ref, w_ref, qg_ref, kg_ref, at_ref, eg_ref, cv_ref, act_ref):
    c = GDN_CHUNK
    rows = x_ref.shape[0]
    x = x_ref[...]
    halo = prev_ref.shape[0]
    prev = jnp.where(pl.program_id(1) == 0, jnp.zeros_like(prev_ref), prev_ref[...])
    xcat = jnp.concatenate([prev, x], axis=0)
    ti = lax.broadcasted_iota(jnp.int32, (rows, rows + halo), 0)
    ui = lax.broadcasted_iota(jnp.int32, (rows, rows + halo), 1)
    xf = x.astype(F32)
    acc = xf * cw_ref[CONV_W - 1:CONV_W, :]
    for i in range(CONV_W - 1):
        shift = (ui == ti + (halo - (CONV_W - 1) + i)).astype(BF16)
        acc = acc + _dot(shift, xcat) * cw_ref[i:i + 1, :]
    cv_ref[0] = xf[rows - (CONV_W - 1):, :]
    act_ref[...] = _silu(acc)

    g_all, beta_all = _gdn_gates(ab_ref[...], alog_ref[...], dtb_ref[...])
    lower, strict, eye, diag16, low32, low64 = _chunk_masks(c)
    gc_all, gr_all = [], []
    for ck in range(rows // c):
        gc_ck = _dot_hi(lower.astype(F32), g_all[ck * c:(ck + 1) * c])
        gc_all.append(gc_ck)
        gr_all.append(_columns(gc_ck))
        eg_ref[ck] = jnp.exp(gc_ck[c - 1:c, :])
    chains = [(ck, h) for ck in range(rows // c) for h in range(GDN_HEADS)]
    rs = [slice(ck * c, (ck + 1) * c) for ck, _ in chains]
    sl = [slice(h * GDN_DK, (h + 1) * GDN_DK) for _, h in chains]
    nc = range(len(chains))
    q = [_l2norm(act_ref[rs[i], sl[i]]) * (GDN_DK ** -0.5) for i in nc]
    k = [_l2norm(act_ref[rs[i], GDN_QK_W + sl[i].start:GDN_QK_W + sl[i].stop]) for i in nc]
    v = [act_ref[rs[i], 2 * GDN_QK_W + sl[i].start:2 * GDN_QK_W + sl[i].stop] for i in nc]
    beta = [beta_all[rs[i], GDN_HEADS + h:GDN_HEADS + h + 1] for i, (_, h) in enumerate(chains)]
    gc = [gc_all[ck][:, h:h + 1] for ck, h in chains]
    gr = [gr_all[ck][h:h + 1, :c] for ck, h in chains]
    decay = [jnp.exp(jnp.where(lower, gc[i] - gr[i], -jnp.inf)) for i in nc]
    exp_g = [jnp.exp(x) for x in gc]
    kb = [k[i] * beta[i] for i in nc]
    kbf = [x.astype(BF16) for x in k]
    kq = [_dot_nt(jnp.concatenate([kb[i].astype(BF16), q[i].astype(BF16)], axis=0), kbf[i]) for i in nc]
    a = [kq[i][:c] * jnp.where(strict, decay[i], 0.0) for i in nc]
    attn = [kq[i][c:] * decay[i] for i in nc]
    t = _unit_lower_inverse(a, eye, diag16, low32, low64)
    uw = [_bdot(t[i], jnp.concatenate([v[i] * beta[i], kb[i] * exp_g[i]], axis=1)) for i in nc]
    for i in nc:
        u_ref[rs[i], sl[i]] = uw[i][:, :GDN_DV]
        w_ref[rs[i], sl[i]] = uw[i][:, GDN_DV:].astype(BF16)
        at_ref[rs[i], sl[i]] = jnp.concatenate([attn[i], jnp.zeros((c, GDN_DK - c), F32)], axis=1).astype(BF16)
        qg_ref[rs[i], sl[i]] = (q[i] * exp_g[i]).astype(BF16)
        kg_ref[rs[i], sl[i]] = (k[i] * jnp.exp(gc[i][c - 1:c, :] - gc[i])).astype(BF16)


def _gdn_scan_kernel(u_ref, w_ref, qg_ref, kg_ref, at_ref, eg_ref, z_ref, nw_ref, o_ref, s_ref):
    c = GDN_CHUNK

    @pl.when(pl.program_id(1) == 0)
    def _():
        s_ref[...] = jnp.zeros_like(s_ref)

    nw = nw_ref[...]
    chains = [(q, h) for q in range(u_ref.shape[0]) for h in range(GDN_HEADS)]
    sl = [slice(h * GDN_DK, (h + 1) * GDN_DK) for _, h in chains]
    nc = range(len(chains))
    s = [s_ref[q, h] for q, h in chains]
    for ck in range(u_ref.shape[1] // c):
        r = slice(ck * c, (ck + 1) * c)
        sb = [x.astype(BF16) for x in s]
        ws = [_dot(w_ref[q, r, sl[i]], sb[i]) for i, (q, _) in enumerate(chains)]
        qs = [_dot(qg_ref[q, r, sl[i]], sb[i]) for i, (q, _) in enumerate(chains)]
        vnb = [(u_ref[q, r, sl[i]] - ws[i]).astype(BF16) for i, (q, _) in enumerate(chains)]
        o = [qs[i] + _dot(at_ref[q, r, h * GDN_DK:h * GDN_DK + c], vnb[i]) for i, (q, h) in enumerate(chains)]
        s = [s[i] * eg_ref[q, ck][:, h:h + 1] + _dot_tn(kg_ref[q, r, sl[i]], vnb[i])
             for i, (q, h) in enumerate(chains)]
        for i, (q, _) in enumerate(chains):
            o_ref[q, r, sl[i]] = (_rms(o[i]) * nw * _silu(z_ref[q, r, sl[i]].astype(F32))).astype(BF16)
    for i, (q, h) in enumerate(chains):
        s_ref[q, h] = s[i]


def _gdn_prompt(p_gdn, p_ab, batch, seq, conv_w, a_log, dt_bias, gdn_norm_w):
    c = GDN_CHUNK
    rows = GDN_PREP_ROWS
    t = batch * seq
    nt = seq // rows
    alog = jnp.pad(a_log.astype(F32), (0, LANES - GDN_HEADS)).reshape(1, LANES)
    dtb = jnp.pad(dt_bias.astype(F32), (0, LANES - GDN_HEADS)).reshape(1, LANES)
    wide = lambda: pl.BlockSpec((rows, GDN_V_W), lambda b, j: (b * nt + j, 0))
    u, w, qg, kg, at, eg, conv_new = pl.pallas_call(
        _gdn_prep_kernel,
        grid=(batch, nt),
        in_specs=[pl.BlockSpec((rows, CONV_CH), lambda b, j: (b * nt + j, 0)),
                  pl.BlockSpec((16, CONV_CH), lambda b, j: (jnp.maximum((b * nt + j) * (rows // 16) - 1, 0), 0)),
                  pl.BlockSpec((rows, LANES), lambda b, j: (b * nt + j, 0)),
                  _resident((CONV_W, CONV_CH)), _resident((1, LANES)), _resident((1, LANES))],
        out_specs=[wide(), wide(), wide(), wide(), wide(),
                   pl.BlockSpec((rows // c, 1, LANES), lambda b, j: (b * nt + j, 0, 0)),
                   pl.BlockSpec((1, CONV_W - 1, CONV_CH), lambda b, j: (b, 0, 0))],
        out_shape=[jax.ShapeDtypeStruct((t, GDN_V_W), F32)] + [jax.ShapeDtypeStruct((t, GDN_V_W), BF16)] * 4
        + [jax.ShapeDtypeStruct((t // c, 1, LANES), F32),
           jax.ShapeDtypeStruct((batch, CONV_W - 1, CONV_CH), F32)],
        scratch_shapes=[pltpu.VMEM((rows, CONV_CH), F32)],
        compiler_params=_params(("parallel", "arbitrary")),
        name="gdn_prep",
    )(p_gdn, p_gdn, p_ab, conv_w.astype(F32), alog, dtb)
    srows = GDN_SCAN_ROWS
    n = seq // srows
    nq = CONV_CH // GDN_V_W
    nseq = GDN_SCAN_SEQS if batch % GDN_SCAN_SEQS == 0 else 1
    per_seq = lambda a: a.reshape((batch, seq // (t // a.shape[0])) + a.shape[1:])
    blk = lambda: pl.BlockSpec((nseq, srows, GDN_V_W), lambda b, j: (b, j, 0))
    o, s_new = pl.pallas_call(
        _gdn_scan_kernel,
        grid=(batch // nseq, n),
        in_specs=[blk(), blk(), blk(), blk(), blk(),
                  pl.BlockSpec((nseq, srows // c, 1, LANES), lambda b, j: (b, j, 0, 0)),
                  pl.BlockSpec((nseq, srows, GDN_V_W), lambda b, j: (b, j, nq)),
                  _resident((1, GDN_DV))],
        out_specs=[blk(), pl.BlockSpec((nseq, GDN_HEADS, GDN_DK, GDN_DV), lambda b, j: (b, 0, 0, 0))],
        out_shape=[jax.ShapeDtypeStruct((batch, seq, GDN_V_W), BF16),
                   jax.ShapeDtypeStruct((batch, GDN_HEADS, GDN_DK, GDN_DV), F32)],
        compiler_params=_params(("parallel", "arbitrary")),
        name="gdn_scan",
    )(per_seq(u), per_seq(w), per_seq(qg), per_seq(kg), per_seq(at), per_seq(eg), per_seq(p_gdn),
      gdn_norm_w.astype(F32).reshape(1, GDN_DV))
    return o.reshape(t, GDN_V_W), s_new, conv_new


def _gdn_sample_kernel(x_ref, ab_ref, sc_ref, cw_ref, alog_ref, dtb_ref, nw_ref, s_ref, o_ref, so_ref, sco_ref):
    nb = x_ref.shape[0]
    x = x_ref[:, :CONV_CH]
    acc = x * cw_ref[CONV_W - 1:CONV_W, :]
    for i in range(CONV_W - 1):
        acc = acc + sc_ref[i] * cw_ref[i:i + 1, :]
    for i in range(CONV_W - 2):
        sco_ref[i] = sc_ref[i + 1]
    sco_ref[CONV_W - 2] = x
    u = _silu(acc)
    g_all, beta_all = _gdn_gates(ab_ref[...], alog_ref[...], dtb_ref[...])
    eg_all = jnp.exp(g_all)
    nw = nw_ref[...]
    for h in range(GDN_HEADS):
        sl = slice(h * GDN_DK, (h + 1) * GDN_DK)
        q = _l2norm(u[:, sl]) * (GDN_DK ** -0.5)
        k = _l2norm(u[:, GDN_QK_W + h * GDN_DK:GDN_QK_W + (h + 1) * GDN_DK])
        v = u[:, 2 * GDN_QK_W + h * GDN_DV:2 * GDN_QK_W + (h + 1) * GDN_DV]
        beta = beta_all[:, GDN_HEADS + h:GDN_HEADS + h + 1]
        eg = eg_all[:, h:h + 1]
        qk_dot = jnp.sum(q * k, axis=-1, keepdims=True)
        kt = _columns(k)
        kq = jnp.concatenate([k, q], axis=0).astype(BF16)
        rows = []
        for j in range(nb):
            s = s_ref[j, h]
            kcol = kt[:, j:j + 1]
            both = _dot(kq, s.astype(BF16))
            ks, qs = both[j:j + 1], both[nb + j:nb + j + 1]
            ej = eg[j:j + 1]
            v_new = beta[j:j + 1] * (v[j:j + 1] - ej * ks)
            rows.append(ej * qs + qk_dot[j:j + 1] * v_new)
            so_ref[j, h] = s * ej + kcol * v_new
        o = jnp.concatenate(rows, axis=0)
        z = x_ref[:, CONV_CH + h * GDN_DV:CONV_CH + (h + 1) * GDN_DV]
        o_ref[:, sl] = (_rms(o) * nw * _silu(z)).astype(BF16)


def _gdn_sample(p_gdn_s, p_ab_s, state, conv_state, conv_w, a_log, dt_bias, gdn_norm_w):
    ts = p_gdn_s.shape[0]
    sb = SAMPLE_TILE
    alog = jnp.pad(a_log.astype(F32), (0, LANES - GDN_HEADS)).reshape(1, LANES)
    dtb = jnp.pad(dt_bias.astype(F32), (0, LANES - GDN_HEADS)).reshape(1, LANES)
    sc = jnp.swapaxes(conv_state.astype(F32), 0, 1)
    st = pl.BlockSpec((sb, GDN_HEADS, GDN_DK, GDN_DV), lambda i: (i, 0, 0, 0))
    scs = pl.BlockSpec((CONV_W - 1, sb, CONV_CH), lambda i: (0, i, 0))
    o, s_new, sc_new = pl.pallas_call(
        _gdn_sample_kernel,
        grid=(ts // sb,),
        in_specs=[pl.BlockSpec((sb, _GDN_W), lambda i: (i, 0)), pl.BlockSpec((sb, LANES), lambda i: (i, 0)), scs,
                  _resident((CONV_W, CONV_CH)), _resident((1, LANES)), _resident((1, LANES)),
                  _resident((1, GDN_DV)), st],
        out_specs=[pl.BlockSpec((sb, GDN_V_W), lambda i: (i, 0)), st, scs],
        out_shape=[jax.ShapeDtypeStruct((ts, GDN_V_W), BF16), jax.ShapeDtypeStruct(state.shape, F32),
                   jax.ShapeDtypeStruct(sc.shape, F32)],
        compiler_params=_params(("parallel",)),
        name="gdn_sample",
    )(p_gdn_s, p_ab_s, sc, conv_w.astype(F32), alog, dtb, gdn_norm_w.astype(F32).reshape(1, GDN_DV), state)
    return o, s_new, jnp.swapaxes(sc_new, 0, 1)


_NO_EXPERT = -1e30


def _merge_kernel(oa0_ref, ob0_ref, gate0_ref, x0_ref, oa1_ref, ob1_ref, gate1_ref, x1in_ref,
                  wa_ref, wb_ref, wo_ref, nw_ref, wr_ref, br_ref, x1_ref, h2_ref, ti_ref, tw_ref, *, n_first):
    first = pl.program_id(0) < n_first
    pick = lambda a, b: jnp.where(first, a[...], b[...])
    gate = pick(gate0_ref, gate1_ref)
    ya = _dot(pick(oa0_ref, oa1_ref), wa_ref[...])
    yb = _dot(pick(ob0_ref, ob1_ref), wb_ref[...])
    ga = gate[:, :D_MODEL].astype(F32)
    gb = gate[:, D_MODEL:].astype(F32)
    m = _sigmoid(ga) * ya + _sigmoid(gb) * yb
    x1 = pick(x0_ref, x1in_ref) + _dot(m.astype(BF16), wo_ref[...])
    x1_ref[...] = x1
    h2 = _rms(x1) * nw_ref[...]
    for j, plane in enumerate(_pack_rows(h2)):
        h2_ref[:, j, :] = plane
    lg = _dot_split(h2, wr_ref[...]) + br_ref[...]
    lane = lax.broadcasted_iota(jnp.int32, lg.shape, 1).astype(F32)
    vals, idxs = [], []
    for _ in range(TOP_K):
        top = jnp.max(lg, axis=-1, keepdims=True)
        idx = jnp.min(jnp.where(lg == top, lane, float(LANES)), axis=-1, keepdims=True)
        vals.append(top)
        idxs.append(idx)
        lg = jnp.where(lane == idx, _NO_EXPERT, lg)
    es = [jnp.exp(v - vals[0]) for v in vals]
    inv_total = 1.0 / functools.reduce(lambda a, b: a + b, es)
    ti = jnp.zeros_like(lg)
    tw = jnp.zeros_like(lg)
    for k in range(TOP_K):
        ti = jnp.where(lane == float(k), idxs[k], ti)
        tw = jnp.where(lane == float(k), es[k] * inv_total, tw)
    ti_ref[...] = ti.astype(jnp.int32)
    tw_ref[...] = tw


def _merge(group0, group1, weights):
    r0 = group0[3].shape[0]
    tm = _pick_tile(r0, (MERGE_ROWS, 128, 64, 32, 16, 8))
    pad = -group1[3].shape[0] % tm
    group1 = tuple(jnp.pad(a, ((0, pad), (0, 0))) for a in group1)
    r1 = group1[3].shape[0]
    n0, total = r0 // tm, r0 + r1
    widths = (RET_V_W, GDN_V_W, 2 * D_MODEL, D_MODEL)
    specs0 = [pl.BlockSpec((tm, n), lambda i: (jnp.minimum(i, n0 - 1), 0)) for n in widths]
    specs1 = [pl.BlockSpec((tm, n), lambda i: (jnp.maximum(i - n0, 0), 0)) for n in widths]
    out = lambda n: pl.BlockSpec((tm, n), lambda i: (i, 0))
    sub = PACK_PLANES
    sq = (D_MODEL, D_MODEL)
    return pl.pallas_call(
        functools.partial(_merge_kernel, n_first=n0),
        grid=(total // tm,),
        in_specs=specs0 + specs1 + [_resident(sq), _resident(sq), _resident(sq), _resident((1, D_MODEL)),
                                    _resident((D_MODEL, LANES)), _resident((1, LANES))],
        out_specs=[out(D_MODEL), pl.BlockSpec((tm, sub, LANES), lambda i: (i, 0, 0)), out(LANES), out(LANES)],
        out_shape=[jax.ShapeDtypeStruct((total, D_MODEL), F32), jax.ShapeDtypeStruct((total, sub, LANES), jnp.uint32),
                   jax.ShapeDtypeStruct((total, LANES), jnp.int32), jax.ShapeDtypeStruct((total, LANES), F32)],
        compiler_params=_params(("parallel",)),
        name="merge",
    )(*group0, *group1, *weights)


def _merge_weights(w_a, w_b, w_o, ffn_norm_w, w_router, b_router):
    wr = jnp.pad(w_router.astype(F32), ((0, 0), (0, LANES - N_EXPERTS)))
    br = jnp.pad(b_router.astype(F32), (0, LANES - N_EXPERTS), constant_values=_NO_EXPERT).reshape(1, LANES)
    return (w_a.astype(BF16), w_b.astype(BF16), w_o.astype(BF16), ffn_norm_w.astype(F32).reshape(1, D_MODEL), wr, br)


def _route(top_i, gate, n_tokens):
    rows = MOE_ROWS
    n = n_tokens * TOP_K
    flat_e = top_i.reshape(n).astype(jnp.int32)
    bits = max(1, (n - 1).bit_length())
    assert bits + (N_EXPERTS - 1).bit_length() <= 31
    order = lax.sort((flat_e << bits) | jnp.arange(n, dtype=jnp.int32), is_stable=False) & ((1 << bits) - 1)
    counts = jnp.sum((flat_e[:, None] == jnp.arange(N_EXPERTS, dtype=jnp.int32)[None, :]).astype(jnp.int32), axis=0)
    start = jnp.cumsum(counts) - counts
    pcounts = (counts + rows - 1) // rows * rows
    pend = jnp.cumsum(pcounts)
    pstart = pend - pcounts
    nb = -(-n // rows) + N_EXPERTS
    blk = jnp.arange(nb, dtype=jnp.int32)
    block_e = jnp.minimum(jnp.sum((pend[None, :] <= (blk * rows)[:, None]).astype(jnp.int32), axis=1),
                          N_EXPERTS - 1).astype(jnp.int32)
    nb_used = (pend[-1] // rows).astype(jnp.int32).reshape(1)
    is_e = block_e[:, None] == jnp.arange(N_EXPERTS, dtype=jnp.int32)[None, :]
    of_block = lambda table: jnp.sum(jnp.where(is_e, table[None, :], 0), axis=1)
    within = (blk * rows - of_block(pstart))[:, None] + jnp.arange(rows, dtype=jnp.int32)[None, :]
    valid = jnp.logical_and(within < of_block(counts)[:, None], (blk < nb_used[0])[:, None])
    flat = order[jnp.clip(of_block(start)[:, None] + within, 0, n - 1)]
    spare = n + (blk % MOE_RING)[:, None] * rows + jnp.arange(rows, dtype=jnp.int32)[None, :]
    src = jnp.where(valid, flat // TOP_K, 0)
    dst = jnp.where(valid, flat, spare)
    ahead = lambda k: jnp.concatenate([src[k:]] + [src[-1:]] * k, axis=0)
    slab = jnp.concatenate([src, dst, ahead(1), ahead(2)], axis=1)
    row_w = jnp.where(valid, gate[:, :TOP_K].reshape(n)[flat], 0.0)
    row_w = jnp.broadcast_to(row_w[:, :, None], (nb, rows, LANES))
    return block_e, nb_used, slab, row_w


def _expert_kernel(be_ref, nbu_ref, slab_ref, h_ref, roww_ref, wgu_ref, bgu_ref, wd_ref, bd_ref, y_ref,
                   idx0_ref, idx1_ref, idx2_ref, xbuf_ref, ybuf_ref, wgu_bf_ref, wd_bf_ref, isem, gsem, ssem):
    rows = MOE_ROWS
    ring = MOE_RING
    idx_refs = (idx0_ref, idx1_ref, idx2_ref)
    assert len(idx_refs) == ring
    sub = PACK_PLANES
    i = pl.program_id(0)
    nbu = nbu_ref[0]
    slot = i % ring
    active = i < nbu
    n_real = y_ref.shape[0] - ring * rows

    def slab_copy(blk, sl):
        return pltpu.make_async_copy(slab_ref.at[blk], idx_refs[sl], isem.at[sl])

    def gather_row(tok, sl, r):
        return pltpu.make_async_copy(h_ref.at[tok], xbuf_ref.at[sl, :, r, :], gsem.at[sl])

    def scatter_row(sl, r, f):
        return pltpu.make_async_copy(ybuf_ref.at[sl, :, r, :], y_ref.at[f], ssem.at[sl])

    def gather_wait(sl):
        pltpu.make_async_copy(xbuf_ref.at[sl], xbuf_ref.at[sl], gsem.at[sl]).wait()

    def scatter_wait(sl):
        pltpu.make_async_copy(ybuf_ref.at[sl], ybuf_ref.at[sl], ssem.at[sl]).wait()

    @pl.when(i == 0)
    def _():
        slab_copy(0, 0).start()
        ybuf_ref[...] = jnp.zeros_like(ybuf_ref)
        for sl in range(ring):
            def fill(r, carry, sl=sl):
                scatter_row(sl, r, n_real + sl * rows + r).start()
                return carry
            lax.fori_loop(0, rows, fill, 0)

    def block(sl):
        nxt, nxt2 = (sl + 1) % ring, (sl + 2) % ring
        idx_ref = idx_refs[sl]
        slab_copy(i, sl).wait()

        @pl.when(i + 1 < nbu)
        def _():
            slab_copy(i + 1, nxt).start()

        if sl == 0:
            @pl.when(i == 0)
            def _():
                def first(r, carry):
                    gather_row(idx_ref[r], 0, r).start()
                    gather_row(idx_ref[2 * rows + r], 1, r).start()
                    return carry
                lax.fori_loop(0, rows, first, 0)

        changed = jnp.logical_or(i == 0, be_ref[i] != be_ref[jnp.maximum(i - 1, 0)])

        @pl.when(changed)
        def _():
            wgu_bf_ref[...] = wgu_ref[0].astype(BF16)
            wd_bf_ref[...] = wd_ref[0].astype(BF16)

        gather_wait(sl)
        scatter_wait(sl)
        lo, hi = _unpack_planes([xbuf_ref[sl, j] for j in range(sub)])
        xb = jnp.concatenate(lo + hi, axis=1).astype(BF16)
        hb = _dot(xb, wgu_bf_ref[...]) + bgu_ref[0]
        for r in range(rows):
            gather_row(idx_ref[3 * rows + r], nxt2, r).start(priority=r % 2)
        glu = jnp.minimum(hb[:, :D_FF], SWIGLU_LIMIT)
        lin = jnp.clip(hb[:, D_FF:], -SWIGLU_LIMIT, SWIGLU_LIMIT)
        act = (glu * _sigmoid(SWIGLU_ALPHA * glu) * (lin + 1.0)).astype(BF16)
        yv = (_dot(act, wd_bf_ref[...]) + bd_ref[0]) * roww_ref[0, :, 0:1]
        for j, plane in enumerate(_pack_rows(yv)):
            ybuf_ref[sl, j] = plane
        for r in range(rows):
            scatter_row(sl, r, idx_ref[rows + r]).start(priority=(r + 1) % 2)

        @pl.when(i == nbu - 1)
        def _():
            for s in (sl, nxt, nxt2):
                scatter_wait(s)
            gather_wait(nxt)
            gather_wait(nxt2)

    for sl in range(ring):
        pl.when(jnp.logical_and(active, slot == sl))(functools.partial(block, sl))


def _experts(h2, t, block_e, nb_used, slab, row_w, w_gate_up, b_gate_up, w_down, b_down):
    rows = MOE_ROWS
    ring = MOE_RING
    assert (ring * rows) % TOP_K == 0
    nb = slab.shape[0]
    sub = PACK_PLANES
    grid_spec = pltpu.PrefetchScalarGridSpec(
        num_scalar_prefetch=2,
        grid=(nb,),
        in_specs=[pl.BlockSpec(memory_space=pl.ANY),
                  pl.BlockSpec(memory_space=pl.ANY),
                  pl.BlockSpec((1, rows, LANES), lambda i, be, nbu: (i, 0, 0)),
                  pl.BlockSpec((1, D_MODEL, 2 * D_FF), lambda i, be, nbu: (be[i], 0, 0)),
                  pl.BlockSpec((1, 1, 2 * D_FF), lambda i, be, nbu: (be[i], 0, 0)),
                  pl.BlockSpec((1, D_FF, D_MODEL), lambda i, be, nbu: (be[i], 0, 0)),
                  pl.BlockSpec((1, 1, D_MODEL), lambda i, be, nbu: (be[i], 0, 0))],
        out_specs=pl.BlockSpec(memory_space=pl.ANY),
        scratch_shapes=[pltpu.SMEM((slab.shape[1],), jnp.int32)] * ring + [
                        pltpu.VMEM((ring, sub, rows, LANES), jnp.uint32),
                        pltpu.VMEM((ring, sub, rows, LANES), jnp.uint32),
                        pltpu.VMEM((D_MODEL, 2 * D_FF), BF16),
                        pltpu.VMEM((D_FF, D_MODEL), BF16),
                        pltpu.SemaphoreType.DMA((ring,)),
                        pltpu.SemaphoreType.DMA((ring,)),
                        pltpu.SemaphoreType.DMA((ring,))])
    return pl.pallas_call(
        _expert_kernel,
        grid_spec=grid_spec,
        out_shape=jax.ShapeDtypeStruct((t * TOP_K + ring * rows, sub, LANES), jnp.uint32),
        compiler_params=_params(("arbitrary",)),
        name="experts",
    )(block_e, nb_used, slab, h2, row_w, w_gate_up, b_gate_up.reshape(N_EXPERTS, 1, 2 * D_FF), w_down,
      b_down.reshape(N_EXPERTS, 1, D_MODEL))


def _combine_kernel(y_ref, x1_ref, nw_ref, o_ref, sum_ref, *, final):
    lo, hi = None, None
    for k in range(TOP_K):
        (l,), (h,) = _unpack_planes([y_ref[:, k]])
        lo, hi = (l, h) if lo is None else (lo + l, hi + h)
    sum_ref[:, :PACK_PLANES, :] = lo
    sum_ref[:, PACK_PLANES:, :] = hi
    acc = x1_ref[...] + jnp.concatenate([sum_ref[:, j, :] for j in range(D_MODEL // LANES)], axis=1)
    o_ref[...] = _rms(acc) * nw_ref[...] if final else acc


def _combine(y, x1, row0, rows, norm_w, final):
    tm = _row_tile(rows, row0, (1024, 512, 256, 128, 64, 32, 16, 8))
    off = row0 // tm
    sub = PACK_PLANES
    y4 =y.reshape(y.shape[0] // TOP_K, TOP_K, sub, LANES)
    return pl.pallas_call(
        functools.partial(_combine_kernel, final=final),
        grid=(rows // tm,),
        in_specs=[pl.BlockSpec((tm, TOP_K, sub, LANES), lambda i: (off + i, 0, 0, 0)),
                  pl.BlockSpec((tm, D_MODEL), lambda i: (off + i, 0)), _resident((1, D_MODEL))],
        out_specs=pl.BlockSpec((tm, D_MODEL), lambda i: (i, 0)),
        out_shape=jax.ShapeDtypeStruct((rows, D_MODEL), F32),
        scratch_shapes=[pltpu.VMEM((tm, D_MODEL // LANES, LANES), F32)],
        compiler_params=_params(("parallel",)),
        name="combine",
    )(y4, x1, norm_w.astype(F32).reshape(1, D_MODEL))


def kernel(x_prompt, x_sample, state_ret, state_gdn, state_conv, attn_norm_w, w_in, conv_w, a_log, dt_bias, gdn_norm_w, w_branch_a, w_branch_b, w_out, ffn_norm_w, w_router, b_router, w_gate_up, b_gate_up, w_down, b_down, final_norm_w):
    bp, lp, d = x_prompt.shape
    bs, ls, _ = x_sample.shape
    assert ls == 1 and d == D_MODEL and lp % RET_CHUNK == 0 and bs % SAMPLE_TILE == 0
    depth = w_in.shape[0]
    tp = bp * lp
    t = tp + bs
    xp, xs = x_prompt.reshape(tp, d).astype(F32), x_sample.reshape(bs, d).astype(F32)
    rp, gp, cp, rs, gs, cs = [], [], [], [], [], []
    for l in range(depth):
        wb = jnp.pad(w_in[l], ((0, 0), (0, -w_in.shape[2] % LANES))).astype(BF16)
        pp_ret, pp_gdn, pp_gate, pp_ab = _inproj(xp, attn_norm_w[l], wb)
        ps_ret, ps_gdn, ps_gate, ps_ab = _inproj(xs, attn_norm_w[l], wb)
        op_ret, s_ret_p = _ret_prompt(pp_ret, bp, lp)
        os_ret, s_ret_s = _ret_sample(ps_ret.astype(F32), state_ret[l].astype(F32))
        op_gdn, s_gdn_p, conv_p = _gdn_prompt(pp_gdn, pp_ab, bp, lp, conv_w[l], a_log[l], dt_bias[l], gdn_norm_w[l])
        os_gdn, s_gdn_s, conv_s = _gdn_sample(ps_gdn.astype(F32), ps_ab, state_gdn[l].astype(F32), state_conv[l],
                                              conv_w[l], a_log[l], dt_bias[l], gdn_norm_w[l])
        mw = _merge_weights(w_branch_a[l], w_branch_b[l], w_out[l], ffn_norm_w[l], w_router[l], b_router[l])
        x1, h2, top_i, gate = _merge((op_ret, op_gdn, pp_gate, xp), (os_ret, os_gdn, ps_gate, xs), mw)
        block_e, nb_used, slab, row_w = _route(top_i[:t, :TOP_K], gate[:t], t)
        y = _experts(h2, t, block_e, nb_used, slab, row_w, w_gate_up[l], b_gate_up[l], w_down[l], b_down[l])
        last = l == depth - 1
        norm_w = final_norm_w if last else jnp.ones((d,), F32)
        xp = _combine(y, x1, 0, tp, norm_w, last)
        xs = _combine(y, x1, tp, bs, norm_w, last)
        rp.append(s_ret_p); gp.append(s_gdn_p); cp.append(conv_p)
        rs.append(s_ret_s); gs.append(s_gdn_s); cs.append(conv_s)
    y_prompt = xp.reshape(bp, lp, d).astype(x_prompt.dtype)
    y_sample = xs.reshape(bs, ls, d).astype(x_sample.dtype)
    return (y_prompt, y_sample,
            jnp.stack(rp).astype(state_ret.dtype), jnp.stack(gp).astype(state_gdn.dtype),
            jnp.stack(cp).astype(state_conv.dtype),
            jnp.stack(rs).astype(state_ret.dtype), jnp.stack(gs).astype(state_gdn.dtype),
            jnp.stack(cs).astype(state_conv.dtype))
```

```python
import functools
import math

import numpy as np
import jax
import jax.numpy as jnp
from jax import lax
from jax.experimental import pallas as pl
from jax.experimental.pallas import tpu as pltpu

F32 = jnp.float32
BF16 = jnp.bfloat16
HIGHEST = lax.Precision.HIGHEST

D_MODEL = 1024
PAST_LEN = 16384
RET_HEADS, RET_DK, RET_DV = 4, 128, 256
RET_QK_W, RET_V_W = RET_HEADS * RET_DK, RET_HEADS * RET_DV
RET_CHUNK = 128
ROPE_BASE = 10000.0
GDN_HEADS, GDN_DK, GDN_DV = 8, 128, 128
GDN_QK_W, GDN_V_W = GDN_HEADS * GDN_DK, GDN_HEADS * GDN_DV
GDN_CHUNK = 64
CONV_W = 4
CONV_CH = 2 * GDN_QK_W + GDN_V_W
N_EXPERTS = 32
TOP_K = 4
D_FF = D_MODEL
SWIGLU_LIMIT = 7.0
SWIGLU_ALPHA = 1.702
NORM_EPS = 1e-6

_RET_W = 2 * RET_QK_W + 2 * RET_V_W
_GDN_W = CONV_CH + GDN_V_W
_AB_OFF = _RET_W + _GDN_W
_GATE_OFF = _AB_OFF + 2 * GDN_HEADS

LANES = 128
VMEM_LIMIT = 56 * 1024 * 1024
MERGE_ROWS = 512
MOE_ROWS = 256
MOE_RING = 3
SAMPLE_TILE = 16
RET_STEP_ROWS = 256
RET_STEP_SEQS = 4
GDN_PREP_ROWS = 256
GDN_SCAN_ROWS = 256
GDN_SCAN_SEQS = 4


def _pick_tile(n, candidates):
    for c in candidates:
        if n % c == 0:
            return c
    raise ValueError(f"no tile in {candidates} divides {n}")


def _params(sem, vmem=VMEM_LIMIT):
    return pltpu.CompilerParams(dimension_semantics=sem, vmem_limit_bytes=vmem)


def _resident(shape):
    nd = len(shape)
    return pl.BlockSpec(shape, lambda *_: (0,) * nd, pipeline_mode=pl.Buffered(1))


def _silu(x):
    return x * (1.0 / (1.0 + jnp.exp(-x)))


def _sigmoid(x):
    return 1.0 / (1.0 + jnp.exp(-x))


def _softplus(x):
    return jnp.maximum(x, 0.0) + jnp.log1p(jnp.exp(-jnp.abs(x)))


def _rms(x):
    return x * lax.rsqrt(jnp.mean(x * x, axis=-1, keepdims=True) + NORM_EPS)


def _dot(a, b):
    return jnp.dot(a, b, preferred_element_type=F32)


def _dot_nt(a, b):
    return lax.dot_general(a, b, (((1,), (1,)), ((), ())), preferred_element_type=F32)


def _dot_tn(a, b):
    return lax.dot_general(a, b, (((0,), (0,)), ((), ())), preferred_element_type=F32)


def _dot_hi(a, b):
    return jnp.dot(a, b, preferred_element_type=F32, precision=HIGHEST)


PACK_PLANES = D_MODEL // (2 * LANES)


def _pack_rows(x):
    as_bits = lambda v: pltpu.bitcast(v.astype(BF16).astype(F32), jnp.uint32)
    planes = []
    for j in range(PACK_PLANES):
        lo = as_bits(x[:, j * LANES:(j + 1) * LANES])
        hi = as_bits(x[:, (j + PACK_PLANES) * LANES:(j + PACK_PLANES + 1) * LANES])
        planes.append(jnp.bitwise_or(hi, lax.shift_right_logical(lo, jnp.uint32(16))))
    return planes


def _unpack_planes(planes):
    lo = [pltpu.bitcast(lax.shift_left(p, jnp.uint32(16)), F32) for p in planes]
    hi = [pltpu.bitcast(jnp.bitwise_and(p, jnp.uint32(0xFFFF0000)), F32) for p in planes]
    return lo, hi


def _dot_split(a, b):
    a_hi, b_hi = a.astype(BF16), b.astype(BF16)
    a_lo = (a - a_hi.astype(F32)).astype(BF16)
    b_lo = (b - b_hi.astype(F32)).astype(BF16)
    return _dot(a_hi, b_hi) + (_dot(a_hi, b_lo) + _dot(a_lo, b_hi))


def _row_tile(rows, row0, candidates):
    return _pick_tile(math.gcd(rows, row0) if row0 else rows, candidates)


def _inproj_kernel(x_ref, nw_ref, w_ref, oret_ref, ogdn_ref, ogate_ref, oab_ref):
    h = (_rms(x_ref[...]) * nw_ref[...]).astype(BF16)
    oret_ref[...] = _dot(h, w_ref[:, :_RET_W]).astype(BF16)
    ogdn_ref[...] = _dot(h, w_ref[:, _RET_W:_AB_OFF]).astype(BF16)
    tail = _dot(h, w_ref[:, _AB_OFF:])
    oab_ref[...] = tail[:, :LANES]
    ogate_ref[...] = tail[:, _GATE_OFF - _AB_OFF:_GATE_OFF - _AB_OFF + 2 * D_MODEL].astype(BF16)


def _inproj(x, norm_w, wb):
    rows = x.shape[0]
    tm = _pick_tile(rows, (256, 128, 64, 32, 16, 8))
    row = lambda n: pl.BlockSpec((tm, n), lambda i: (i, 0))
    return pl.pallas_call(
        _inproj_kernel,
        grid=(rows // tm,),
        in_specs=[row(D_MODEL), _resident((1, D_MODEL)), _resident(wb.shape)],
        out_specs=[row(_RET_W), row(_GDN_W), row(2 * D_MODEL), row(LANES)],
        out_shape=[jax.ShapeDtypeStruct((rows, _RET_W), BF16), jax.ShapeDtypeStruct((rows, _GDN_W), BF16),
                   jax.ShapeDtypeStruct((rows, 2 * D_MODEL), BF16), jax.ShapeDtypeStruct((rows, LANES), F32)],
        compiler_params=_params(("parallel",)),
        name="inproj",
    )(x, norm_w.reshape(1, D_MODEL), wb)


def _ret_log_gamma():
    return np.log1p(-np.exp2(-5.0 - np.arange(RET_HEADS, dtype=np.float64)))


def _rope_tables(pos):
    half = RET_DK // 2
    inv = 1.0 / (ROPE_BASE ** (jnp.arange(half, dtype=F32) / half))
    ang = pos.astype(F32)[:, None] * inv[None, :]
    cos, sin = jnp.cos(ang), jnp.sin(ang)
    return jnp.concatenate([cos, cos], axis=-1), jnp.concatenate([-sin, sin], axis=-1)


def _rotary(x, cos, sin):
    return x * cos + pltpu.roll(x, RET_DK // 2, 1) * sin


def _ret_prompt_kernel(q_ref, k_ref, v_ref, g_ref, cos_ref, sin_ref, dmask_ref, qdec_ref, kdec_ref,
                       o_ref, s_ref, *, gammas):
    @pl.when(pl.program_id(1) == 0)
    def _():
        s_ref[...] = jnp.zeros_like(s_ref)

    c = RET_CHUNK
    chains = [(sq, h) for sq in range(q_ref.shape[0]) for h in range(RET_HEADS)]
    nc = range(len(chains))
    qk = [slice(h * RET_DK, (h + 1) * RET_DK) for _, h in chains]
    vv = [slice(h * RET_DV, (h + 1) * RET_DV) for _, h in chains]
    hd = [h for _, h in chains]
    sq = [s_ for s_, _ in chains]
    s = [s_ref[sq[i], hd[i]] for i in nc]
    for ck in range(q_ref.shape[1] // c):
        r = slice(ck * c, (ck + 1) * c)
        cos, sin = cos_ref[r, :], sin_ref[r, :]
        q = [_rotary(q_ref[sq[i], r, qk[i]].astype(F32), cos, sin) for i in nc]
        k = [_rotary(k_ref[sq[i], r, qk[i]].astype(F32), cos, sin) * (RET_DK ** -0.5) for i in nc]
        v = [v_ref[sq[i], r, vv[i]] for i in nc]
        qb = [x.astype(BF16) for x in q]
        inner = [_dot_nt(qb[i], k[i].astype(BF16)) * dmask_ref[hd[i]] for i in nc]
        cross = [_dot((q[i] * qdec_ref[hd[i]]).astype(BF16), s[i].astype(BF16)) for i in nc]
        upd = [_dot_tn((k[i] * kdec_ref[hd[i]]).astype(BF16), v[i]) for i in nc]
        o = [_dot(inner[i].astype(BF16), v[i]) + cross[i] for i in nc]
        s = [s[i] * gammas[hd[i]] + upd[i] for i in nc]
        for i in nc:
            o_ref[sq[i], r, vv[i]] = (_rms(o[i]) * _silu(g_ref[sq[i], r, vv[i]].astype(F32))).astype(BF16)
    for i in nc:
        s_ref[sq[i], hd[i]] = s[i]


def _ret_prompt(p_ret, batch, seq):
    c = RET_CHUNK
    step = RET_STEP_ROWS if seq % RET_STEP_ROWS == 0 else c
    n = seq // step
    lg = _ret_log_gamma()
    idx = np.arange(c, dtype=np.float64)
    diff = idx[:, None] - idx[None, :]
    dmask = np.where(diff >= 0, np.exp(np.maximum(diff, 0.0)[None] * lg[:, None, None]), 0.0)
    qdec = np.broadcast_to(np.exp((idx + 1.0)[None, :] * lg[:, None])[:, :, None], (RET_HEADS, c, RET_DK))
    kdec = np.broadcast_to(np.exp((c - 1.0 - idx)[None, :] * lg[:, None])[:, :, None], (RET_HEADS, c, RET_DK))
    gammas = tuple(float(g) for g in np.exp(c * lg))
    cos, sin = _rope_tables(jnp.arange(seq, dtype=jnp.int32))
    tab = lambda: _resident((RET_HEADS, c, RET_DK))
    nseq = RET_STEP_SEQS if batch % RET_STEP_SEQS == 0 else 1
    p3 = p_ret.reshape(batch, seq, _RET_W)
    o, s_new = pl.pallas_call(
        functools.partial(_ret_prompt_kernel, gammas=gammas),
        grid=(batch // nseq, n),
        in_specs=[pl.BlockSpec((nseq, step, RET_QK_W), lambda b, j: (b, j, 0)),
                  pl.BlockSpec((nseq, step, RET_QK_W), lambda b, j: (b, j, 1)),
                  pl.BlockSpec((nseq, step, RET_V_W), lambda b, j: (b, j, 1)),
                  pl.BlockSpec((nseq, step, RET_V_W), lambda b, j: (b, j, 2)),
                  pl.BlockSpec((step, RET_DK), lambda b, j: (j, 0)),
                  pl.BlockSpec((step, RET_DK), lambda b, j: (j, 0)),
                  tab(), tab(), tab()],
        out_specs=[pl.BlockSpec((nseq, step, RET_V_W), lambda b, j: (b, j, 0)),
                   pl.BlockSpec((nseq, RET_HEADS, RET_DK, RET_DV), lambda b, j: (b, 0, 0, 0))],
        out_shape=[jax.ShapeDtypeStruct((batch, seq, RET_V_W), BF16),
                   jax.ShapeDtypeStruct((batch, RET_HEADS, RET_DK, RET_DV), F32)],
        compiler_params=_params(("parallel", "arbitrary")),
        name="ret_prompt",
    )(p3, p3, p3, p3, cos, sin, jnp.asarray(dmask, F32), jnp.asarray(qdec, F32), jnp.asarray(kdec, F32))
    return o.reshape(batch * seq, RET_V_W), s_new


def _columns(x):
    n = x.shape[0]
    if n < LANES:
        x = jnp.concatenate([x, jnp.zeros((LANES - n, x.shape[1]), x.dtype)], axis=0)
    return x.T


def _ret_sample_kernel(p_ref, cos_ref, sin_ref, s_ref, o_ref, so_ref, *, gammas):
    cos, sin = cos_ref[...], sin_ref[...]
    nb = p_ref.shape[0]
    for h in range(RET_HEADS):
        qk = slice(h * RET_DK, (h + 1) * RET_DK)
        q = _rotary(p_ref[:, qk], cos, sin)
        k = _rotary(p_ref[:, RET_QK_W + h * RET_DK:RET_QK_W + (h + 1) * RET_DK], cos, sin) * (RET_DK ** -0.5)
        v = p_ref[:, 2 * RET_QK_W + h * RET_DV:2 * RET_QK_W + (h + 1) * RET_DV]
        g = p_ref[:, 2 * RET_QK_W + RET_V_W + h * RET_DV:2 * RET_QK_W + RET_V_W + (h + 1) * RET_DV]
        qk_dot = jnp.sum(q * k, axis=-1, keepdims=True)
        kt = _columns(k)
        qb = q.astype(BF16)
        rows = []
        for j in range(nb):
            s = s_ref[j, h]
            qs = _dot(qb, s.astype(BF16))[j:j + 1]
            rows.append(qk_dot[j:j + 1] * v[j:j + 1] + gammas[h] * qs)
            so_ref[j, h] = s * gammas[h] + kt[:, j:j + 1] * v[j:j + 1]
        o = jnp.concatenate(rows, axis=0)
        o_ref[:, h * RET_DV:(h + 1) * RET_DV] = (_rms(o) * _silu(g)).astype(BF16)


def _ret_sample(p_ret_s, state):
    ts = p_ret_s.shape[0]
    sb = SAMPLE_TILE
    gammas = tuple(float(g) for g in np.exp(_ret_log_gamma()))
    cos, sin = _rope_tables(jnp.full((1,), PAST_LEN, jnp.int32))
    st = pl.BlockSpec((sb, RET_HEADS, RET_DK, RET_DV), lambda i: (i, 0, 0, 0))
    return pl.pallas_call(
        functools.partial(_ret_sample_kernel, gammas=gammas),
        grid=(ts // sb,),
        in_specs=[pl.BlockSpec((sb, _RET_W), lambda i: (i, 0)), _resident((1, RET_DK)), _resident((1, RET_DK)), st],
        out_specs=[pl.BlockSpec((sb, RET_V_W), lambda i: (i, 0)), st],
        out_shape=[jax.ShapeDtypeStruct((ts, RET_V_W), BF16), jax.ShapeDtypeStruct(state.shape, F32)],
        compiler_params=_params(("parallel",)),
        name="ret_sample",
    )(p_ret_s, cos, sin, state)


def _l2norm(x):
    return x * lax.rsqrt(jnp.sum(x * x, axis=-1, keepdims=True) + NORM_EPS)


def _bdot(a, b):
    return _dot(a.astype(BF16), b.astype(BF16))


def _chunk_masks(c):
    ri = lax.broadcasted_iota(jnp.int32, (c, c), 0)
    ci = lax.broadcasted_iota(jnp.int32, (c, c), 1)
    eye = (ri == ci).astype(F32)
    diag16 = (ri // 16 == ci // 16).astype(F32)
    low32 = jnp.logical_and(ri // 32 == ci // 32, ri // 16 > ci // 16).astype(F32)
    low64 = (ri // 32 > ci // 32).astype(F32)
    return ri >= ci, ri > ci, eye, diag16, low32, low64


def _unit_lower_inverse(a, eye, diag16, low32, low64):
    many = lambda f, *ls: [f(*args) for args in zip(*ls)]
    c = eye.shape[0]
    pair = lambda p, q, rhs: _bdot(jnp.concatenate([p, q], axis=0), rhs)
    n = [-(x * diag16) for x in a]
    n2 = many(_bdot, n, n)
    n34 = many(pair, n, n2, n2)
    x = [eye + p + q + r[:c] for p, q, r in zip(n, n2, n34)]
    n4 = [r[c:] for r in n34]
    xn = many(pair, x, n4, n4)
    x = [u + v[:c] for u, v in zip(x, xn)]
    x = many(lambda u, v: u + v, x, many(_bdot, x, [v[c:] for v in xn]))
    for mask in (low32, low64):
        r = many(_bdot, [y * mask for y in a], x)
        x = many(lambda u, v: u - v, x, many(_bdot, x, r))
    return x


def _gdn_gates(ab, alog, dtb):
    g = -jnp.exp(alog) * _softplus(ab + dtb)
    return g, _sigmoid(ab)


def _gdn_prep_kernel(x_ref, prev_ref, ab_ref, cw_ref, alog_ref, dtb_ref,
                     u_ref, w_ref, qg_ref, kg_ref, at_ref, eg_ref, cv_ref, act_ref):
    c = GDN_CHUNK
    rows = x_ref.shape[0]
    x = x_ref[...]
    halo = prev_ref.shape[0]
    prev = jnp.where(pl.program_id(1) == 0, jnp.zeros_like(prev_ref), prev_ref[...])
    xcat = jnp.concatenate([prev, x], axis=0)
    ti = lax.broadcasted_iota(jnp.int32, (rows, rows + halo), 0)
    ui = lax.broadcasted_iota(jnp.int32, (rows, rows + halo), 1)
    xf = x.astype(F32)
    acc = xf * cw_ref[CONV_W - 1:CONV_W, :]
    for i in range(CONV_W - 1):
        shift = (ui == ti + (halo - (CONV_W - 1) + i)).astype(BF16)
        acc = acc + _dot(shift, xcat) * cw_ref[i:i + 1, :]
    cv_ref[0] = xf[rows - (CONV_W - 1):, :]
    act_ref[...] = _silu(acc)

    g_all, beta_all = _gdn_gates(ab_ref[...], alog_ref[...], dtb_ref[...])
    lower, strict, eye, diag16, low32, low64 = _chunk_masks(c)
    gc_all, gr_all = [], []
    for ck in range(rows // c):
        gc_ck = _dot_hi(lower.astype(F32), g_all[ck * c:(ck + 1) * c])
        gc_all.append(gc_ck)
        gr_all.append(_columns(gc_ck))
        eg_ref[ck] = jnp.exp(gc_ck[c - 1:c, :])
    chains = [(ck, h) for ck in range(rows // c) for h in range(GDN_HEADS)]
    rs = [slice(ck * c, (ck + 1) * c) for ck, _ in chains]
    sl = [slice(h * GDN_DK, (h + 1) * GDN_DK) for _, h in chains]
    nc = range(len(chains))
    q = [_l2norm(act_ref[rs[i], sl[i]]) * (GDN_DK ** -0.5) for i in nc]
    k = [_l2norm(act_ref[rs[i], GDN_QK_W + sl[i].start:GDN_QK_W + sl[i].stop]) for i in nc]
    v = [act_ref[rs[i], 2 * GDN_QK_W + sl[i].start:2 * GDN_QK_W + sl[i].stop] for i in nc]
    beta = [beta_all[rs[i], GDN_HEADS + h:GDN_HEADS + h + 1] for i, (_, h) in enumerate(chains)]
    gc = [gc_all[ck][:, h:h + 1] for ck, h in chains]
    gr = [gr_all[ck][h:h + 1, :c] for ck, h in chains]
    decay = [jnp.exp(jnp.where(lower, gc[i] - gr[i], -jnp.inf)) for i in nc]
    exp_g = [jnp.exp(x) for x in gc]
    kb = [k[i] * beta[i] for i in nc]
    kbf = [x.astype(BF16) for x in k]
    kq = [_dot_nt(jnp.concatenate([kb[i].astype(BF16), q[i].astype(BF16)], axis=0), kbf[i]) for i in nc]
    a = [kq[i][:c] * jnp.where(strict, decay[i], 0.0) for i in nc]
    attn = [kq[i][c:] * decay[i] for i in nc]
    t = _unit_lower_inverse(a, eye, diag16, low32, low64)
    uw = [_bdot(t[i], jnp.concatenate([v[i] * beta[i], kb[i] * exp_g[i]], axis=1)) for i in nc]
    for i in nc:
        u_ref[rs[i], sl[i]] = uw[i][:, :GDN_DV]
        w_ref[rs[i], sl[i]] = uw[i][:, GDN_DV:].astype(BF16)
        at_ref[rs[i], sl[i]] = jnp.concatenate([attn[i], jnp.zeros((c, GDN_DK - c), F32)], axis=1).astype(BF16)
        qg_ref[rs[i], sl[i]] = (q[i] * exp_g[i]).astype(BF16)
        kg_ref[rs[i], sl[i]] = (k[i] * jnp.exp(gc[i][c - 1:c, :] - gc[i])).astype(BF16)


def _gdn_scan_kernel(u_ref, w_ref, qg_ref, kg_ref, at_ref, eg_ref, z_ref, nw_ref, o_ref, s_ref):
    c = GDN_CHUNK

    @pl.when(pl.program_id(1) == 0)
    def _():
        s_ref[...] = jnp.zeros_like(s_ref)

    nw = nw_ref[...]
    chains = [(q, h) for q in range(u_ref.shape[0]) for h in range(GDN_HEADS)]
    sl = [slice(h * GDN_DK, (h + 1) * GDN_DK) for _, h in chains]
    nc = range(len(chains))
    s = [s_ref[q, h] for q, h in chains]
    for ck in range(u_ref.shape[1] // c):
        r = slice(ck * c, (ck + 1) * c)
        sb = [x.astype(BF16) for x in s]
        ws = [_dot(w_ref[q, r, sl[i]], sb[i]) for i, (q, _) in enumerate(chains)]
        qs = [_dot(qg_ref[q, r, sl[i]], sb[i]) for i, (q, _) in enumerate(chains)]
        vnb = [(u_ref[q, r, sl[i]] - ws[i]).astype(BF16) for i, (q, _) in enumerate(chains)]
        o = [qs[i] + _dot(at_ref[q, r, h * GDN_DK:h * GDN_DK + c], vnb[i]) for i, (q, h) in enumerate(chains)]
        s = [s[i] * eg_ref[q, ck][:, h:h + 1] + _dot_tn(kg_ref[q, r, sl[i]], vnb[i])
             for i, (q, h) in enumerate(chains)]
        for i, (q, _) in enumerate(chains):
            o_ref[q, r, sl[i]] = (_rms(o[i]) * nw * _silu(z_ref[q, r, sl[i]].astype(F32))).astype(BF16)
    for i, (q, h) in enumerate(chains):
        s_ref[q, h] = s[i]


def _gdn_prompt(p_gdn, p_ab, batch, seq, conv_w, a_log, dt_bias, gdn_norm_w):
    c = GDN_CHUNK
    rows = GDN_PREP_ROWS
    t = batch * seq
    nt = seq // rows
    alog = jnp.pad(a_log.astype(F32), (0, LANES - GDN_HEADS)).reshape(1, LANES)
    dtb = jnp.pad(dt_bias.astype(F32), (0, LANES - GDN_HEADS)).reshape(1, LANES)
    wide = lambda: pl.BlockSpec((rows, GDN_V_W), lambda b, j: (b * nt + j, 0))
    u, w, qg, kg, at, eg, conv_new = pl.pallas_call(
        _gdn_prep_kernel,
        grid=(batch, nt),
        in_specs=[pl.BlockSpec((rows, CONV_CH), lambda b, j: (b * nt + j, 0)),
                  pl.BlockSpec((16, CONV_CH), lambda b, j: (jnp.maximum((b * nt + j) * (rows // 16) - 1, 0), 0)),
                  pl.BlockSpec((rows, LANES), lambda b, j: (b * nt + j, 0)),
                  _resident((CONV_W, CONV_CH)), _resident((1, LANES)), _resident((1, LANES))],
        out_specs=[wide(), wide(), wide(), wide(), wide(),
                   pl.BlockSpec((rows // c, 1, LANES), lambda b, j: (b * nt + j, 0, 0)),
                   pl.BlockSpec((1, CONV_W - 1, CONV_CH), lambda b, j: (b, 0, 0))],
        out_shape=[jax.ShapeDtypeStruct((t, GDN_V_W), F32)] + [jax.ShapeDtypeStruct((t, GDN_V_W), BF16)] * 4
        + [jax.ShapeDtypeStruct((t // c, 1, LANES), F32),
           jax.ShapeDtypeStruct((batch, CONV_W - 1, CONV_CH), F32)],
        scratch_shapes=[pltpu.VMEM((rows, CONV_CH), F32)],
        compiler_params=_params(("parallel", "arbitrary")),
        name="gdn_prep",
    )(p_gdn, p_gdn, p_ab, conv_w.astype(F32), alog, dtb)
    srows = GDN_SCAN_ROWS
    n = seq // srows
    nq = CONV_CH // GDN_V_W
    nseq = GDN_SCAN_SEQS if batch % GDN_SCAN_SEQS == 0 else 1
    per_seq = lambda a: a.reshape((batch, seq // (t // a.shape[0])) + a.shape[1:])
    blk = lambda: pl.BlockSpec((nseq, srows, GDN_V_W), lambda b, j: (b, j, 0))
    o, s_new = pl.pallas_call(
        _gdn_scan_kernel,
        grid=(batch // nseq, n),
        in_specs=[blk(), blk(), blk(), blk(), blk(),
                  pl.BlockSpec((nseq, srows // c, 1, LANES), lambda b, j: (b, j, 0, 0)),
                  pl.BlockSpec((nseq, srows, GDN_V_W), lambda b, j: (b, j, nq)),
                  _resident((1, GDN_DV))],
        out_specs=[blk(), pl.BlockSpec((nseq, GDN_HEADS, GDN_DK, GDN_DV), lambda b, j: (b, 0, 0, 0))],
        out_shape=[jax.ShapeDtypeStruct((batch, seq, GDN_V_W), BF16),
                   jax.ShapeDtypeStruct((batch, GDN_HEADS, GDN_DK, GDN_DV), F32)],
        compiler_params=_params(("parallel", "arbitrary")),
        name="gdn_scan",
    )(per_seq(u), per_seq(w), per_seq(qg), per_seq(kg), per_seq(at), per_seq(eg), per_seq(p_gdn),
      gdn_norm_w.astype(F32).reshape(1, GDN_DV))
    return o.reshape(t, GDN_V_W), s_new, conv_new


def _gdn_sample_kernel(x_ref, ab_ref, sc_ref, cw_ref, alog_ref, dtb_ref, nw_ref, s_ref, o_ref, so_ref, sco_ref):
    nb = x_ref.shape[0]
    x = x_ref[:, :CONV_CH]
    acc = x * cw_ref[CONV_W - 1:CONV_W, :]
    for i in range(CONV_W - 1):
        acc = acc + sc_ref[i] * cw_ref[i:i + 1, :]
    for i in range(CONV_W - 2):
        sco_ref[i] = sc_ref[i + 1]
    sco_ref[CONV_W - 2] = x
    u = _silu(acc)
    g_all, beta_all = _gdn_gates(ab_ref[...], alog_ref[...], dtb_ref[...])
    eg_all = jnp.exp(g_all)
    nw = nw_ref[...]
    for h in range(GDN_HEADS):
        sl = slice(h * GDN_DK, (h + 1) * GDN_DK)
        q = _l2norm(u[:, sl]) * (GDN_DK ** -0.5)
        k = _l2norm(u[:, GDN_QK_W + h * GDN_DK:GDN_QK_W + (h + 1) * GDN_DK])
        v = u[:, 2 * GDN_QK_W + h * GDN_DV:2 * GDN_QK_W + (h + 1) * GDN_DV]
        beta = beta_all[:, GDN_HEADS + h:GDN_HEADS + h + 1]
        eg = eg_all[:, h:h + 1]
        qk_dot = jnp.sum(q * k, axis=-1, keepdims=True)
        kt = _columns(k)
        kq = jnp.concatenate([k, q], axis=0).astype(BF16)
        rows = []
        for j in range(nb):
            s = s_ref[j, h]
            kcol = kt[:, j:j + 1]
            both = _dot(kq, s.astype(BF16))
            ks, qs = both[j:j + 1], both[nb + j:nb + j + 1]
            ej = eg[j:j + 1]
            v_new = beta[j:j + 1] * (v[j:j + 1] - ej * ks)
            rows.append(ej * qs + qk_dot[j:j + 1] * v_new)
            so_ref[j, h] = s * ej + kcol * v_new
        o = jnp.concatenate(rows, axis=0)
        z = x_ref[:, CONV_CH + h * GDN_DV:CONV_CH + (h + 1) * GDN_DV]
        o_ref[:, sl] = (_rms(o) * nw * _silu(z)).astype(BF16)


def _gdn_sample(p_gdn_s, p_ab_s, state, conv_state, conv_w, a_log, dt_bias, gdn_norm_w):
    ts = p_gdn_s.shape[0]
    sb = SAMPLE_TILE
    alog = jnp.pad(a_log.astype(F32), (0, LANES - GDN_HEADS)).reshape(1, LANES)
    dtb = jnp.pad(dt_bias.astype(F32), (0, LANES - GDN_HEADS)).reshape(1, LANES)
    sc = jnp.swapaxes(conv_state.astype(F32), 0, 1)
    st = pl.BlockSpec((sb, GDN_HEADS, GDN_DK, GDN_DV), lambda i: (i, 0, 0, 0))
    scs = pl.BlockSpec((CONV_W - 1, sb, CONV_CH), lambda i: (0, i, 0))
    o, s_new, sc_new = pl.pallas_call(
        _gdn_sample_kernel,
        grid=(ts // sb,),
        in_specs=[pl.BlockSpec((sb, _GDN_W), lambda i: (i, 0)), pl.BlockSpec((sb, LANES), lambda i: (i, 0)), scs,
                  _resident((CONV_W, CONV_CH)), _resident((1, LANES)), _resident((1, LANES)),
                  _resident((1, GDN_DV)), st],
        out_specs=[pl.BlockSpec((sb, GDN_V_W), lambda i: (i, 0)), st, scs],
        out_shape=[jax.ShapeDtypeStruct((ts, GDN_V_W), BF16), jax.ShapeDtypeStruct(state.shape, F32),
                   jax.ShapeDtypeStruct(sc.shape, F32)],
        compiler_params=_params(("parallel",)),
        name="gdn_sample",
    )(p_gdn_s, p_ab_s, sc, conv_w.astype(F32), alog, dtb, gdn_norm_w.astype(F32).reshape(1, GDN_DV), state)
    return o, s_new, jnp.swapaxes(sc_new, 0, 1)


_NO_EXPERT = -1e30


def _merge_kernel(oa0_ref, ob0_ref, gate0_ref, x0_ref, oa1_ref, ob1_ref, gate1_ref, x1in_ref,
                  wa_ref, wb_ref, wo_ref, nw_ref, wr_ref, br_ref, x1_ref, h2_ref, ti_ref, tw_ref, *, n_first):
    first = pl.program_id(0) < n_first
    pick = lambda a, b: jnp.where(first, a[...], b[...])
    gate = pick(gate0_ref, gate1_ref)
    ya = _dot(pick(oa0_ref, oa1_ref), wa_ref[...])
    yb = _dot(pick(ob0_ref, ob1_ref), wb_ref[...])
    ga = gate[:, :D_MODEL].astype(F32)
    gb = gate[:, D_MODEL:].astype(F32)
    m = _sigmoid(ga) * ya + _sigmoid(gb) * yb
    x1 = pick(x0_ref, x1in_ref) + _dot(m.astype(BF16), wo_ref[...])
    x1_ref[...] = x1
    h2 = _rms(x1) * nw_ref[...]
    for j, plane in enumerate(_pack_rows(h2)):
        h2_ref[:, j, :] = plane
    lg = _dot_split(h2, wr_ref[...]) + br_ref[...]
    lane = lax.broadcasted_iota(jnp.int32, lg.shape, 1).astype(F32)
    vals, idxs = [], []
    for _ in range(TOP_K):
        top = jnp.max(lg, axis=-1, keepdims=True)
        idx = jnp.min(jnp.where(lg == top, lane, float(LANES)), axis=-1, keepdims=True)
        vals.append(top)
        idxs.append(idx)
        lg = jnp.where(lane == idx, _NO_EXPERT, lg)
    es = [jnp.exp(v - vals[0]) for v in vals]
    inv_total = 1.0 / functools.reduce(lambda a, b: a + b, es)
    ti = jnp.zeros_like(lg)
    tw = jnp.zeros_like(lg)
    for k in range(TOP_K):
        ti = jnp.where(lane == float(k), idxs[k], ti)
        tw = jnp.where(lane == float(k), es[k] * inv_total, tw)
    ti_ref[...] = ti.astype(jnp.int32)
    tw_ref[...] = tw


def _merge(group0, group1, weights):
    r0 = group0[3].shape[0]
    tm = _pick_tile(r0, (MERGE_ROWS, 128, 64, 32, 16, 8))
    pad = -group1[3].shape[0] % tm
    group1 = tuple(jnp.pad(a, ((0, pad), (0, 0))) for a in group1)
    r1 = group1[3].shape[0]
    n0, total = r0 // tm, r0 + r1
    widths = (RET_V_W, GDN_V_W, 2 * D_MODEL, D_MODEL)
    specs0 = [pl.BlockSpec((tm, n), lambda i: (jnp.minimum(i, n0 - 1), 0)) for n in widths]
    specs1 = [pl.BlockSpec((tm, n), lambda i: (jnp.maximum(i - n0, 0), 0)) for n in widths]
    out = lambda n: pl.BlockSpec((tm, n), lambda i: (i, 0))
    sub = PACK_PLANES
    sq = (D_MODEL, D_MODEL)
    return pl.pallas_call(
        functools.partial(_merge_kernel, n_first=n0),
        grid=(total // tm,),
        in_specs=specs0 + specs1 + [_resident(sq), _resident(sq), _resident(sq), _resident((1, D_MODEL)),
                                    _resident((D_MODEL, LANES)), _resident((1, LANES))],
        out_specs=[out(D_MODEL), pl.BlockSpec((tm, sub, LANES), lambda i: (i, 0, 0)), out(LANES), out(LANES)],
        out_shape=[jax.ShapeDtypeStruct((total, D_MODEL), F32), jax.ShapeDtypeStruct((total, sub, LANES), jnp.uint32),
                   jax.ShapeDtypeStruct((total, LANES), jnp.int32), jax.ShapeDtypeStruct((total, LANES), F32)],
        compiler_params=_params(("parallel",)),
        name="merge",
    )(*group0, *group1, *weights)


def _merge_weights(w_a, w_b, w_o, ffn_norm_w, w_router, b_router):
    wr = jnp.pad(w_router.astype(F32), ((0, 0), (0, LANES - N_EXPERTS)))
    br = jnp.pad(b_router.astype(F32), (0, LANES - N_EXPERTS), constant_values=_NO_EXPERT).reshape(1, LANES)
    return (w_a.astype(BF16), w_b.astype(BF16), w_o.astype(BF16), ffn_norm_w.astype(F32).reshape(1, D_MODEL), wr, br)


def _route(top_i, gate, n_tokens):
    rows = MOE_ROWS
    n = n_tokens * TOP_K
    flat_e = top_i.reshape(n).astype(jnp.int32)
    bits = max(1, (n - 1).bit_length())
    assert bits + (N_EXPERTS - 1).bit_length() <= 31
    order = lax.sort((flat_e << bits) | jnp.arange(n, dtype=jnp.int32), is_stable=False) & ((1 << bits) - 1)
    counts = jnp.sum((flat_e[:, None] == jnp.arange(N_EXPERTS, dtype=jnp.int32)[None, :]).astype(jnp.int32), axis=0)
    start = jnp.cumsum(counts) - counts
    pcounts = (counts + rows - 1) // rows * rows
    pend = jnp.cumsum(pcounts)
    pstart = pend - pcounts
    nb = -(-n // rows) + N_EXPERTS
    blk = jnp.arange(nb, dtype=jnp.int32)
    block_e = jnp.minimum(jnp.sum((pend[None, :] <= (blk * rows)[:, None]).astype(jnp.int32), axis=1),
                          N_EXPERTS - 1).astype(jnp.int32)
    nb_used = (pend[-1] // rows).astype(jnp.int32).reshape(1)
    is_e = block_e[:, None] == jnp.arange(N_EXPERTS, dtype=jnp.int32)[None, :]
    of_block = lambda table: jnp.sum(jnp.where(is_e, table[None, :], 0), axis=1)
    within = (blk * rows - of_block(pstart))[:, None] + jnp.arange(rows, dtype=jnp.int32)[None, :]
    valid = jnp.logical_and(within < of_block(counts)[:, None], (blk < nb_used[0])[:, None])
    flat = order[jnp.clip(of_block(start)[:, None] + within, 0, n - 1)]
    spare = n + (blk % MOE_RING)[:, None] * rows + jnp.arange(rows, dtype=jnp.int32)[None, :]
    src = jnp.where(valid, flat // TOP_K, 0)
    dst = jnp.where(valid, flat, spare)
    ahead = lambda k: jnp.concatenate([src[k:]] + [src[-1:]] * k, axis=0)
    slab = jnp.concatenate([src, dst, ahead(1), ahead(2)], axis=1)
    row_w = jnp.where(valid, gate[:, :TOP_K].reshape(n)[flat], 0.0)
    row_w = jnp.broadcast_to(row_w[:, :, None], (nb, rows, LANES))
    return block_e, nb_used, slab, row_w


def _expert_kernel(be_ref, nbu_ref, slab_ref, h_ref, roww_ref, wgu_ref, bgu_ref, wd_ref, bd_ref, y_ref,
                   idx0_ref, idx1_ref, idx2_ref, xbuf_ref, ybuf_ref, wgu_bf_ref, wd_bf_ref, isem, gsem, ssem):
    rows = MOE_ROWS
    ring = MOE_RING
    idx_refs = (idx0_ref, idx1_ref, idx2_ref)
    assert len(idx_refs) == ring
    sub = PACK_PLANES
    i = pl.program_id(0)
    nbu = nbu_ref[0]
    slot = i % ring
    active = i < nbu
    n_real = y_ref.shape[0] - ring * rows

    def slab_copy(blk, sl):
        return pltpu.make_async_copy(slab_ref.at[blk], idx_refs[sl], isem.at[sl])

    def gather_row(tok, sl, r):
        return pltpu.make_async_copy(h_ref.at[tok], xbuf_ref.at[sl, :, r, :], gsem.at[sl])

    def scatter_row(sl, r, f):
        return pltpu.make_async_copy(ybuf_ref.at[sl, :, r, :], y_ref.at[f], ssem.at[sl])

    def gather_wait(sl):
        pltpu.make_async_copy(xbuf_ref.at[sl], xbuf_ref.at[sl], gsem.at[sl]).wait()

    def scatter_wait(sl):
        pltpu.make_async_copy(ybuf_ref.at[sl], ybuf_ref.at[sl], ssem.at[sl]).wait()

    @pl.when(i == 0)
    def _():
        slab_copy(0, 0).start()
        ybuf_ref[...] = jnp.zeros_like(ybuf_ref)
        for sl in range(ring):
            def fill(r, carry, sl=sl):
                scatter_row(sl, r, n_real + sl * rows + r).start()
                return carry
            lax.fori_loop(0, rows, fill, 0)

    def block(sl):
        nxt, nxt2 = (sl + 1) % ring, (sl + 2) % ring
        idx_ref = idx_refs[sl]
        slab_copy(i, sl).wait()

        @pl.when(i + 1 < nbu)
        def _():
            slab_copy(i + 1, nxt).start()

        if sl == 0:
            @pl.when(i == 0)
            def _():
                def first(r, carry):
                    gather_row(idx_ref[r], 0, r).start()
                    gather_row(idx_ref[2 * rows + r], 1, r).start()
                    return carry
                lax.fori_loop(0, rows, first, 0)

        changed = jnp.logical_or(i == 0, be_ref[i] != be_ref[jnp.maximum(i - 1, 0)])

        @pl.when(changed)
        def _():
            wgu_bf_ref[...] = wgu_ref[0].astype(BF16)
            wd_bf_ref[...] = wd_ref[0].astype(BF16)

        gather_wait(sl)
        scatter_wait(sl)
        lo, hi = _unpack_planes([xbuf_ref[sl, j] for j in range(sub)])
        xb = jnp.concatenate(lo + hi, axis=1).astype(BF16)
        hb = _dot(xb, wgu_bf_ref[...]) + bgu_ref[0]
        for r in range(rows):
            gather_row(idx_ref[3 * rows + r], nxt2, r).start(priority=r % 2)
        glu = jnp.minimum(hb[:, :D_FF], SWIGLU_LIMIT)
        lin = jnp.clip(hb[:, D_FF:], -SWIGLU_LIMIT, SWIGLU_LIMIT)
        act = (glu * _sigmoid(SWIGLU_ALPHA * glu) * (lin + 1.0)).astype(BF16)
        yv = (_dot(act, wd_bf_ref[...]) + bd_ref[0]) * roww_ref[0, :, 0:1]
        for j, plane in enumerate(_pack_rows(yv)):
            ybuf_ref[sl, j] = plane
        for r in range(rows):
            scatter_row(sl, r, idx_ref[rows + r]).start(priority=(r + 1) % 2)

        @pl.when(i == nbu - 1)
        def _():
            for s in (sl, nxt, nxt2):
                scatter_wait(s)
            gather_wait(nxt)
            gather_wait(nxt2)

    for sl in range(ring):
        pl.when(jnp.logical_and(active, slot == sl))(functools.partial(block, sl))


def _experts(h2, t, block_e, nb_used, slab, row_w, w_gate_up, b_gate_up, w_down, b_down):
    rows = MOE_ROWS
    ring = MOE_RING
    assert (ring * rows) % TOP_K == 0
    nb = slab.shape[0]
    sub = PACK_PLANES
    grid_spec = pltpu.PrefetchScalarGridSpec(
        num_scalar_prefetch=2,
        grid=(nb,),
        in_specs=[pl.BlockSpec(memory_space=pl.ANY),
                  pl.BlockSpec(memory_space=pl.ANY),
                  pl.BlockSpec((1, rows, LANES), lambda i, be, nbu: (i, 0, 0)),
                  pl.BlockSpec((1, D_MODEL, 2 * D_FF), lambda i, be, nbu: (be[i], 0, 0)),
                  pl.BlockSpec((1, 1, 2 * D_FF), lambda i, be, nbu: (be[i], 0, 0)),
                  pl.BlockSpec((1, D_FF, D_MODEL), lambda i, be, nbu: (be[i], 0, 0)),
                  pl.BlockSpec((1, 1, D_MODEL), lambda i, be, nbu: (be[i], 0, 0))],
        out_specs=pl.BlockSpec(memory_space=pl.ANY),
        scratch_shapes=[pltpu.SMEM((slab.shape[1],), jnp.int32)] * ring + [
                        pltpu.VMEM((ring, sub, rows, LANES), jnp.uint32),
                        pltpu.VMEM((ring, sub, rows, LANES), jnp.uint32),
                        pltpu.VMEM((D_MODEL, 2 * D_FF), BF16),
                        pltpu.VMEM((D_FF, D_MODEL), BF16),
                        pltpu.SemaphoreType.DMA((ring,)),
                        pltpu.SemaphoreType.DMA((ring,)),
                        pltpu.SemaphoreType.DMA((ring,))])
    return pl.pallas_call(
        _expert_kernel,
        grid_spec=grid_spec,
        out_shape=jax.ShapeDtypeStruct((t * TOP_K + ring * rows, sub, LANES), jnp.uint32),
        compiler_params=_params(("arbitrary",)),
        name="experts",
    )(block_e, nb_used, slab, h2, row_w, w_gate_up, b_gate_up.reshape(N_EXPERTS, 1, 2 * D_FF), w_down,
      b_down.reshape(N_EXPERTS, 1, D_MODEL))


def _combine_kernel(y_ref, x1_ref, nw_ref, o_ref, sum_ref, *, final):
    lo, hi = None, None
    for k in range(TOP_K):
        (l,), (h,) = _unpack_planes([y_ref[:, k]])
        lo, hi = (l, h) if lo is None else (lo + l, hi + h)
    sum_ref[:, :PACK_PLANES, :] = lo
    sum_ref[:, PACK_PLANES:, :] = hi
    acc = x1_ref[...] + jnp.concatenate([sum_ref[:, j, :] for j in range(D_MODEL // LANES)], axis=1)
    o_ref[...] = _rms(acc) * nw_ref[...] if final else acc


def _combine(y, x1, row0, rows, norm_w, final):
    tm = _row_tile(rows, row0, (1024, 512, 256, 128, 64, 32, 16, 8))
    off = row0 // tm
    sub = PACK_PLANES
    y4 =y.reshape(y.shape[0] // TOP_K, TOP_K, sub, LANES)
    return pl.pallas_call(
        functools.partial(_combine_kernel, final=final),
        grid=(rows // tm,),
        in_specs=[pl.BlockSpec((tm, TOP_K, sub, LANES), lambda i: (off + i, 0, 0, 0)),
                  pl.BlockSpec((tm, D_MODEL), lambda i: (off + i, 0)), _resident((1, D_MODEL))],
        out_specs=pl.BlockSpec((tm, D_MODEL), lambda i: (i, 0)),
        out_shape=jax.ShapeDtypeStruct((rows, D_MODEL), F32),
        scratch_shapes=[pltpu.VMEM((tm, D_MODEL // LANES, LANES), F32)],
        compiler_params=_params(("parallel",)),
        name="combine",
    )(y4, x1, norm_w.astype(F32).reshape(1, D_MODEL))


def kernel(x_prompt, x_sample, state_ret, state_gdn, state_conv, attn_norm_w, w_in, conv_w, a_log, dt_bias, gdn_norm_w, w_branch_a, w_branch_b, w_out, ffn_norm_w, w_router, b_router, w_gate_up, b_gate_up, w_down, b_down, final_norm_w):
    bp, lp, d = x_prompt.shape
    bs, ls, _ = x_sample.shape
    assert ls == 1 and d == D_MODEL and lp % RET_CHUNK == 0 and bs % SAMPLE_TILE == 0
    depth = w_in.shape[0]
    tp = bp * lp
    t = tp + bs
    xp, xs = x_prompt.reshape(tp, d).astype(F32), x_sample.reshape(bs, d).astype(F32)
    rp, gp, cp, rs, gs, cs = [], [], [], [], [], []
    for l in range(depth):
        wb = jnp.pad(w_in[l], ((0, 0), (0, -w_in.shape[2] % LANES))).astype(BF16)
        pp_ret, pp_gdn, pp_gate, pp_ab = _inproj(xp, attn_norm_w[l], wb)
        ps_ret, ps_gdn, ps_gate, ps_ab = _inproj(xs, attn_norm_w[l], wb)
        op_ret, s_ret_p = _ret_prompt(pp_ret, bp, lp)
        os_ret, s_ret_s = _ret_sample(ps_ret.astype(F32), state_ret[l].astype(F32))
        op_gdn, s_gdn_p, conv_p = _gdn_prompt(pp_gdn, pp_ab, bp, lp, conv_w[l], a_log[l], dt_bias[l], gdn_norm_w[l])
        os_gdn, s_gdn_s, conv_s = _gdn_sample(ps_gdn.astype(F32), ps_ab, state_gdn[l].astype(F32), state_conv[l],
                                              conv_w[l], a_log[l], dt_bias[l], gdn_norm_w[l])
        mw = _merge_weights(w_branch_a[l], w_branch_b[l], w_out[l], ffn_norm_w[l], w_router[l], b_router[l])
        x1, h2, top_i, gate = _merge((op_ret, op_gdn, pp_gate, xp), (os_ret, os_gdn, ps_gate, xs), mw)
        block_e, nb_used, slab, row_w = _route(top_i[:t, :TOP_K], gate[:t], t)
        y = _experts(h2, t, block_e, nb_used, slab, row_w, w_gate_up[l], b_gate_up[l], w_down[l], b_down[l])
        last = l == depth - 1
        norm_w = final_norm_w if last else jnp.ones((d,), F32)
        xp = _combine(y, x1, 0, tp, norm_w, last)
        xs = _combine(y, x1, tp, bs, norm_w, last)
        rp.append(s_ret_p); gp.append(s_gdn_p); cp.append(conv_p)
        rs.append(s_ret_s); gs.append(s_gdn_s); cs.append(conv_s)
    y_prompt = xp.reshape(bp, lp, d).astype(x_prompt.dtype)
    y_sample = xs.reshape(bs, ls, d).astype(x_sample.dtype)
    return (y_prompt, y_sample,
            jnp.stack(rp).astype(state_ret.dtype), jnp.stack(gp).astype(state_gdn.dtype),
            jnp.stack(cp).astype(state_conv.dtype),
            jnp.stack(rs).astype(state_ret.dtype), jnp.stack(gs).astype(state_gdn.dtype),
            jnp.stack(cs).astype(state_conv.dtype))
```
